```python
import jax, jax.numpy as jnp
from jax import lax
import numpy as np

D_MODEL = 1024
BATCH = 8
SEQ = 8192
DEPTH = 1

RNN_WIDTH = D_MODEL
RNN_BLOCKS = 8
RNN_BLOCK_W = RNN_WIDTH // RNN_BLOCKS
CONV_WIDTH = 4
LRU_C = 8.0
MLA_HEADS = 8
QK_NOPE = 128
QK_ROPE = 64
V_HEAD = D_MODEL // MLA_HEADS
Q_LORA = D_MODEL // 4
KV_LORA = D_MODEL // 4
ROPE_THETA = 10000.0
Q_BLOCK = 128
D_FF = 4 * D_MODEL
EPS = 1e-6

IN_SIZES = (RNN_WIDTH, RNN_WIDTH, Q_LORA, KV_LORA, QK_ROPE, D_MODEL, D_MODEL)
IN_TOTAL = sum(IN_SIZES)

kernel_name = "hybrid_rglru_mla_sqrelu"


def rmsnorm(x, g):
    xf = x.astype(jnp.float32)
    y = xf * lax.rsqrt(jnp.mean(xf * xf, axis=-1, keepdims=True) + EPS) * g.astype(jnp.float32)
    return y.astype(x.dtype)


def split_cols(z, sizes):
    offs = np.cumsum(sizes)[:-1].tolist()
    return jnp.split(z, offs, axis=-1)


def causal_depthwise_conv(x, w, b):
    c = x.shape[-1]
    y = lax.conv_general_dilated(
        x, w[:, None, :].astype(x.dtype), window_strides=(1,), padding=[(CONV_WIDTH - 1, 0)],
        dimension_numbers=("NWC", "WIO", "NWC"), feature_group_count=c)
    return y + b


def block_diag_linear(x, w, b):
    bsz, s, c = x.shape
    xb = x.reshape(bsz, s, RNN_BLOCKS, RNN_BLOCK_W)
    y = jnp.einsum("bsnc,ncd->bsnd", xb, w) + b
    return y.reshape(bsz, s, c)


def rg_lru(xa, wa, ba, wx, bx, lam):
    xf = xa.astype(jnp.float32)
    r = jax.nn.sigmoid(block_diag_linear(xa, wa, ba).astype(jnp.float32))
    i = jax.nn.sigmoid(block_diag_linear(xa, wx, bx).astype(jnp.float32))
    log_a = -LRU_C * r * jax.nn.softplus(-lam.astype(jnp.float32))
    a = jnp.exp(log_a)
    b = jnp.sqrt(-jnp.expm1(2.0 * log_a)) * (i * xf)

    def combine(e1, e2):
        a1, b1 = e1
        a2, b2 = e2
        return a1 * a2, a2 * b1 + b2

    _, h = lax.associative_scan(combine, (a, b), axis=1)
    return h


def rope_tables(seq):
    pos = jnp.arange(seq, dtype=jnp.float32)
    inv_freq = 1.0 / (ROPE_THETA ** (jnp.arange(0, QK_ROPE, 2, dtype=jnp.float32) / QK_ROPE))
    ang = pos[:, None] * inv_freq[None, :]
    cos = jnp.cos(ang)
    sin = jnp.sin(ang)
    return jnp.concatenate([cos, cos], -1), jnp.concatenate([sin, sin], -1)


def apply_rope(x, cos, sin):
    half = x.shape[-1] // 2
    rot = jnp.concatenate([-x[..., half:], x[..., :half]], axis=-1)
    return (x.astype(jnp.float32) * cos + rot.astype(jnp.float32) * sin).astype(x.dtype)


def mla(c_q, c_kv, k_rope, q_norm, w_uq, kv_norm, w_ukv):
    bsz, s, _ = c_q.shape
    q = (rmsnorm(c_q, q_norm) @ w_uq).reshape(bsz, s, MLA_HEADS, QK_NOPE + QK_ROPE)
    kv = (rmsnorm(c_kv, kv_norm) @ w_ukv).reshape(bsz, s, MLA_HEADS, QK_NOPE + V_HEAD)
    q_nope, q_rope = q[..., :QK_NOPE], q[..., QK_NOPE:]
    k_nope, v = kv[..., :QK_NOPE], kv[..., QK_NOPE:]
    cos, sin = rope_tables(s)
    q_rope = apply_rope(q_rope, cos[None, :, None, :], sin[None, :, None, :])
    k_rope = apply_rope(k_rope, cos[None], sin[None])
    scale = (QK_NOPE + QK_ROPE) ** -0.5
    nb = s // Q_BLOCK
    qn_b = q_nope.reshape(bsz, nb, Q_BLOCK, MLA_HEADS, QK_NOPE).transpose(1, 0, 2, 3, 4)
    qr_b = q_rope.reshape(bsz, nb, Q_BLOCK, MLA_HEADS, QK_ROPE).transpose(1, 0, 2, 3, 4)
    kpos = jnp.arange(s)
    qpos_b = kpos.reshape(nb, Q_BLOCK)
    neg = jnp.finfo(jnp.float32).min

    def one_block(args):
        qn, qr, qpos = args
        sc = jnp.einsum("bqhd,bkhd->bhqk", qn, k_nope, preferred_element_type=jnp.float32)
        sc = sc + jnp.einsum("bqhr,bkr->bhqk", qr, k_rope, preferred_element_type=jnp.float32)
        mask = qpos[:, None] >= kpos[None, :]
        sc = jnp.where(mask[None, None], sc * scale, neg)
        p = jax.nn.softmax(sc, axis=-1)
        return jnp.einsum("bhqk,bkhd->bqhd", p.astype(v.dtype), v)

    o = lax.map(one_block, (qn_b, qr_b, qpos_b))
    return o.transpose(1, 0, 2, 3, 4).reshape(bsz, s, MLA_HEADS * V_HEAD)


def _fwd_setup_inputs(seed: int = 0) -> dict:
    key = jax.random.key(seed)
    ks = jax.random.split(key, 20)
    f32 = jnp.float32
    nrm = lambda k, shape, fan: jax.random.normal(k, shape, f32) * fan ** -0.5
    gain = lambda k, n: 1.0 + 0.02 * jax.random.normal(k, (n,), f32)
    u = jax.random.uniform(ks[9], (RNN_WIDTH,), f32, 0.9, 0.999)
    a0 = u ** (1.0 / LRU_C)
    lru_lambda = jnp.log(a0) - jnp.log1p(-a0)
    return {
        "x": jax.random.normal(ks[0], (BATCH, SEQ, D_MODEL), f32),
        "norm_mix": gain(ks[1], D_MODEL),
        "w_in": nrm(ks[2], (D_MODEL, IN_TOTAL), D_MODEL),
        "conv_w": nrm(ks[3], (CONV_WIDTH, RNN_WIDTH), CONV_WIDTH),
        "conv_b": 0.01 * jax.random.normal(ks[4], (RNN_WIDTH,), f32),
        "lru_wa": nrm(ks[5], (RNN_BLOCKS, RNN_BLOCK_W, RNN_BLOCK_W), RNN_BLOCK_W),
        "lru_ba": 0.01 * jax.random.normal(ks[6], (RNN_BLOCKS, RNN_BLOCK_W), f32),
        "lru_wx": nrm(ks[7], (RNN_BLOCKS, RNN_BLOCK_W, RNN_BLOCK_W), RNN_BLOCK_W),
        "lru_bx": 0.01 * jax.random.normal(ks[8], (RNN_BLOCKS, RNN_BLOCK_W), f32),
        "lru_lambda": lru_lambda,
        "q_norm": gain(ks[10], Q_LORA),
        "w_uq": nrm(ks[11], (Q_LORA, MLA_HEADS * (QK_NOPE + QK_ROPE)), Q_LORA),
        "kv_norm": gain(ks[12], KV_LORA),
        "w_ukv": nrm(ks[13], (KV_LORA, MLA_HEADS * (QK_NOPE + V_HEAD)), KV_LORA),
        "w_out": nrm(ks[14], (D_MODEL, D_MODEL), D_MODEL),
        "norm_mlp": gain(ks[15], D_MODEL),
        "w_up": nrm(ks[16], (D_MODEL, D_FF), D_MODEL),
        "w_down": nrm(ks[17], (D_FF, D_MODEL), D_FF),
        "norm_final": gain(ks[18], D_MODEL),
    }


def _fwd_reference(x, norm_mix, w_in, conv_w, conv_b, lru_wa, lru_ba, lru_wx, lru_bx, lru_lambda,
              q_norm, w_uq, kv_norm, w_ukv, w_out, norm_mlp, w_up, w_down, norm_final):
    h = x
    for _ in range(DEPTH):
        z = rmsnorm(h, norm_mix) @ w_in
        rnn_x, rnn_gate, c_q, c_kv, k_rope, gate_a, gate_b = split_cols(z, IN_SIZES)
        xa = causal_depthwise_conv(rnn_x, conv_w, conv_b)
        hr = rg_lru(xa, lru_wa, lru_ba, lru_wx, lru_bx, lru_lambda)
        y_a = hr * jax.nn.gelu(rnn_gate.astype(jnp.float32))
        y_b = mla(c_q, c_kv, k_rope, q_norm, w_uq, kv_norm, w_ukv).astype(jnp.float32)
        merged = (jax.nn.sigmoid(gate_a.astype(jnp.float32)) * y_a
                  + jax.nn.sigmoid(gate_b.astype(jnp.float32)) * y_b).astype(h.dtype)
        h = h + merged @ w_out
        u = rmsnorm(h, norm_mlp) @ w_up
        h = h + jnp.square(jax.nn.relu(u)) @ w_down
    return rmsnorm(h, norm_final)


import jax as _jax
import jax.numpy as _jnp

TWIN_FORMAT = 'train_step'
FWD_PARAMS = ['x', 'norm_mix', 'w_in', 'conv_w', 'conv_b', 'lru_wa', 'lru_ba', 'lru_wx', 'lru_bx', 'lru_lambda', 'q_norm', 'w_uq', 'kv_norm', 'w_ukv', 'w_out', 'norm_mlp', 'w_up', 'w_down', 'norm_final']
TWIN_WEIGHTS = ['norm_mix', 'w_in', 'conv_w', 'conv_b', 'lru_wa', 'lru_ba', 'lru_wx', 'lru_bx', 'lru_lambda', 'q_norm', 'w_uq', 'kv_norm', 'w_ukv', 'w_out', 'norm_mlp', 'w_up', 'w_down', 'norm_final']
TWIN_DIFF_INPUT = 'x'
TWIN_INPUTS = ['x', 'norm_mix', 'w_in', 'conv_w', 'conv_b', 'lru_wa', 'lru_ba', 'lru_wx', 'lru_bx', 'lru_lambda', 'q_norm', 'w_uq', 'kv_norm', 'w_ukv', 'w_out', 'norm_mlp', 'w_up', 'w_down', 'norm_final', 'loss_target', 'm_norm_mix', 'm_w_in', 'm_conv_w', 'm_conv_b', 'm_lru_wa', 'm_lru_ba', 'm_lru_wx', 'm_lru_bx', 'm_lru_lambda', 'm_q_norm', 'm_w_uq', 'm_kv_norm', 'm_w_ukv', 'm_w_out', 'm_norm_mlp', 'm_w_up', 'm_w_down', 'm_norm_final', 'v_norm_mix', 'v_w_in', 'v_conv_w', 'v_conv_b', 'v_lru_wa', 'v_lru_ba', 'v_lru_wx', 'v_lru_bx', 'v_lru_lambda', 'v_q_norm', 'v_w_uq', 'v_kv_norm', 'v_w_ukv', 'v_w_out', 'v_norm_mlp', 'v_w_up', 'v_w_down', 'v_norm_final']
TWIN_OUTPUTS = ['loss', 'grad_x', 'grad_norm_mix', 'grad_w_in', 'grad_conv_w', 'grad_conv_b', 'grad_lru_wa', 'grad_lru_ba', 'grad_lru_wx', 'grad_lru_bx', 'grad_lru_lambda', 'grad_q_norm', 'grad_w_uq', 'grad_kv_norm', 'grad_w_ukv', 'grad_w_out', 'grad_norm_mlp', 'grad_w_up', 'grad_w_down', 'grad_norm_final', 'delta_norm_mix', 'delta_w_in', 'delta_conv_w', 'delta_conv_b', 'delta_lru_wa', 'delta_lru_ba', 'delta_lru_wx', 'delta_lru_bx', 'delta_lru_lambda', 'delta_q_norm', 'delta_w_uq', 'delta_kv_norm', 'delta_w_ukv', 'delta_w_out', 'delta_norm_mlp', 'delta_w_up', 'delta_w_down', 'delta_norm_final', 'new_m_norm_mix', 'new_m_w_in', 'new_m_conv_w', 'new_m_conv_b', 'new_m_lru_wa', 'new_m_lru_ba', 'new_m_lru_wx', 'new_m_lru_bx', 'new_m_lru_lambda', 'new_m_q_norm', 'new_m_w_uq', 'new_m_kv_norm', 'new_m_w_ukv', 'new_m_w_out', 'new_m_norm_mlp', 'new_m_w_up', 'new_m_w_down', 'new_m_norm_final', 'new_v_norm_mix', 'new_v_w_in', 'new_v_conv_w', 'new_v_conv_b', 'new_v_lru_wa', 'new_v_lru_ba', 'new_v_lru_wx', 'new_v_lru_bx', 'new_v_lru_lambda', 'new_v_q_norm', 'new_v_w_uq', 'new_v_kv_norm', 'new_v_w_ukv', 'new_v_w_out', 'new_v_norm_mlp', 'new_v_w_up', 'new_v_w_down', 'new_v_norm_final']
TWIN_LEAF_KINDS = {'loss': 'loss', 'grad_x': 'grad_x', 'grad_norm_mix': 'grad_w', 'grad_w_in': 'grad_w', 'grad_conv_w': 'grad_w', 'grad_conv_b': 'grad_w', 'grad_lru_wa': 'grad_w', 'grad_lru_ba': 'grad_w', 'grad_lru_wx': 'grad_w', 'grad_lru_bx': 'grad_w', 'grad_lru_lambda': 'grad_w', 'grad_q_norm': 'grad_w', 'grad_w_uq': 'grad_w', 'grad_kv_norm': 'grad_w', 'grad_w_ukv': 'grad_w', 'grad_w_out': 'grad_w', 'grad_norm_mlp': 'grad_w', 'grad_w_up': 'grad_w', 'grad_w_down': 'grad_w', 'grad_norm_final': 'grad_w', 'delta_norm_mix': 'delta_w', 'delta_w_in': 'delta_w', 'delta_conv_w': 'delta_w', 'delta_conv_b': 'delta_w', 'delta_lru_wa': 'delta_w', 'delta_lru_ba': 'delta_w', 'delta_lru_wx': 'delta_w', 'delta_lru_bx': 'delta_w', 'delta_lru_lambda': 'delta_w', 'delta_q_norm': 'delta_w', 'delta_w_uq': 'delta_w', 'delta_kv_norm': 'delta_w', 'delta_w_ukv': 'delta_w', 'delta_w_out': 'delta_w', 'delta_norm_mlp': 'delta_w', 'delta_w_up': 'delta_w', 'delta_w_down': 'delta_w', 'delta_norm_final': 'delta_w', 'new_m_norm_mix': 'new_m', 'new_m_w_in': 'new_m', 'new_m_conv_w': 'new_m', 'new_m_conv_b': 'new_m', 'new_m_lru_wa': 'new_m', 'new_m_lru_ba': 'new_m', 'new_m_lru_wx': 'new_m', 'new_m_lru_bx': 'new_m', 'new_m_lru_lambda': 'new_m', 'new_m_q_norm': 'new_m', 'new_m_w_uq': 'new_m', 'new_m_kv_norm': 'new_m', 'new_m_w_ukv': 'new_m', 'new_m_w_out': 'new_m', 'new_m_norm_mlp': 'new_m', 'new_m_w_up': 'new_m', 'new_m_w_down': 'new_m', 'new_m_norm_final': 'new_m', 'new_v_norm_mix': 'new_v', 'new_v_w_in': 'new_v', 'new_v_conv_w': 'new_v', 'new_v_conv_b': 'new_v', 'new_v_lru_wa': 'new_v', 'new_v_lru_ba': 'new_v', 'new_v_lru_wx': 'new_v', 'new_v_lru_bx': 'new_v', 'new_v_lru_lambda': 'new_v', 'new_v_q_norm': 'new_v', 'new_v_w_uq': 'new_v', 'new_v_kv_norm': 'new_v', 'new_v_w_ukv': 'new_v', 'new_v_w_out': 'new_v', 'new_v_norm_mlp': 'new_v', 'new_v_w_up': 'new_v', 'new_v_w_down': 'new_v', 'new_v_norm_final': 'new_v'}


def _forward(args):
    return _fwd_reference(*[args[k] for k in FWD_PARAMS])


def _output_shape():
    def fwd():
        inp = _fwd_setup_inputs(0)
        return _fwd_reference(*[inp[k] for k in FWD_PARAMS])
    out = _jax.eval_shape(fwd)
    return out.shape, out.dtype

N_MICROBATCH = 1
ADAM_LR = 0.001
ADAM_B1 = 0.9
ADAM_B2 = 0.999
ADAM_EPS = 1e-08
ADAM_WD = 0.01
ADAM_STEP = 10
PER_EXAMPLE_BATCH_AXIS = {'x': 0, 'loss_target': 0}
SHARED_INPUTS = []
_WEIGHT_DTYPES = {'norm_mix': _jnp.float32, 'w_in': _jnp.float32, 'conv_w': _jnp.float32, 'conv_b': _jnp.float32, 'lru_wa': _jnp.float32, 'lru_ba': _jnp.float32, 'lru_wx': _jnp.float32, 'lru_bx': _jnp.float32, 'lru_lambda': _jnp.float32, 'q_norm': _jnp.float32, 'w_uq': _jnp.float32, 'kv_norm': _jnp.float32, 'w_ukv': _jnp.float32, 'w_out': _jnp.float32, 'norm_mlp': _jnp.float32, 'w_up': _jnp.float32, 'w_down': _jnp.float32, 'norm_final': _jnp.float32}
MOMENT_SCALE = {'norm_mix': 9.201295e-02, 'w_in': 4.229192e-02, 'conv_w': 6.696471e-02, 'conv_b': 7.435652e-01, 'lru_wa': 2.090116e-02, 'lru_ba': 1.645003e-02, 'lru_wx': 3.695457e-02, 'lru_bx': 2.204000e-02, 'lru_lambda': 3.565221e-02, 'q_norm': 5.353270e-02, 'w_uq': 2.154766e-02, 'kv_norm': 7.631208e-02, 'w_ukv': 2.721955e-02, 'w_out': 5.902016e-02, 'norm_mlp': 2.222424e-01, 'w_up': 1.115997e-01, 'w_down': 2.357444e-01, 'norm_final': 6.464145e+01}


def _to_microbatches(a, axis):
    t = _jnp.moveaxis(a, axis, 0)
    t = t.reshape((N_MICROBATCH, t.shape[0] // N_MICROBATCH) + t.shape[1:])
    return _jnp.moveaxis(t, 1, axis + 1)


def setup_inputs(seed: int = 0) -> dict:
    inp = _fwd_setup_inputs(seed)
    key = _jax.random.fold_in(_jax.random.key(seed), 7919)
    shape, _ = _output_shape()
    out = dict(inp)
    out["loss_target"] = _jax.random.normal(_jax.random.fold_in(key, 0), shape, _jnp.float32)
    for i, name in enumerate(TWIN_WEIGHTS):
        w = inp[name].astype(_jnp.float32)
        if MOMENT_SCALE is None:
            s = _jnp.sqrt(_jnp.mean(_jnp.square(w)) + 1e-30)
        else:
            s = MOMENT_SCALE[name]
        km, kv = _jax.random.split(_jax.random.fold_in(key, i + 1))
        out[name] = w
        out["m_" + name] = s * _jax.random.normal(km, w.shape, _jnp.float32)
        out["v_" + name] = (s * s) * _jax.random.uniform(kv, w.shape, _jnp.float32, 0.5, 1.5)
    if N_MICROBATCH > 1:
        for name, axis in PER_EXAMPLE_BATCH_AXIS.items():
            out[name] = _to_microbatches(out[name], axis)
    return {'x': out['x'], 'norm_mix': out['norm_mix'], 'w_in': out['w_in'], 'conv_w': out['conv_w'], 'conv_b': out['conv_b'], 'lru_wa': out['lru_wa'], 'lru_ba': out['lru_ba'], 'lru_wx': out['lru_wx'], 'lru_bx': out['lru_bx'], 'lru_lambda': out['lru_lambda'], 'q_norm': out['q_norm'], 'w_uq': out['w_uq'], 'kv_norm': out['kv_norm'], 'w_ukv': out['w_ukv'], 'w_out': out['w_out'], 'norm_mlp': out['norm_mlp'], 'w_up': out['w_up'], 'w_down': out['w_down'], 'norm_final': out['norm_final'], 'loss_target': out['loss_target'], 'm_norm_mix': out['m_norm_mix'], 'm_w_in': out['m_w_in'], 'm_conv_w': out['m_conv_w'], 'm_conv_b': out['m_conv_b'], 'm_lru_wa': out['m_lru_wa'], 'm_lru_ba': out['m_lru_ba'], 'm_lru_wx': out['m_lru_wx'], 'm_lru_bx': out['m_lru_bx'], 'm_lru_lambda': out['m_lru_lambda'], 'm_q_norm': out['m_q_norm'], 'm_w_uq': out['m_w_uq'], 'm_kv_norm': out['m_kv_norm'], 'm_w_ukv': out['m_w_ukv'], 'm_w_out': out['m_w_out'], 'm_norm_mlp': out['m_norm_mlp'], 'm_w_up': out['m_w_up'], 'm_w_down': out['m_w_down'], 'm_norm_final': out['m_norm_final'], 'v_norm_mix': out['v_norm_mix'], 'v_w_in': out['v_w_in'], 'v_conv_w': out['v_conv_w'], 'v_conv_b': out['v_conv_b'], 'v_lru_wa': out['v_lru_wa'], 'v_lru_ba': out['v_lru_ba'], 'v_lru_wx': out['v_lru_wx'], 'v_lru_bx': out['v_lru_bx'], 'v_lru_lambda': out['v_lru_lambda'], 'v_q_norm': out['v_q_norm'], 'v_w_uq': out['v_w_uq'], 'v_kv_norm': out['v_kv_norm'], 'v_w_ukv': out['v_w_ukv'], 'v_w_out': out['v_w_out'], 'v_norm_mlp': out['v_norm_mlp'], 'v_w_up': out['v_w_up'], 'v_w_down': out['v_w_down'], 'v_norm_final': out['v_norm_final']}


def _loss(weights, diff, rest, loss_target):
    with _jax.named_scope("forward"):
        args = {**rest, TWIN_DIFF_INPUT: diff, **{k: w.astype(_WEIGHT_DTYPES[k]) for k, w in weights.items()}}
        y = _forward(args)
    with _jax.named_scope("loss_head"):
        err = _jnp.square(y.astype(_jnp.float32) - loss_target)
        return 0.5 * _jnp.sum(_jnp.mean(err, axis=-1)) if err.ndim else 0.5 * err


def _adamw(w, g, m, v):
    m = ADAM_B1 * m + (1.0 - ADAM_B1) * g
    v = ADAM_B2 * v + (1.0 - ADAM_B2) * _jnp.square(g)
    m_hat = m / (1.0 - ADAM_B1 ** ADAM_STEP)
    v_hat = v / (1.0 - ADAM_B2 ** ADAM_STEP)
    delta = -ADAM_LR * (m_hat / (_jnp.sqrt(v_hat) + ADAM_EPS) + ADAM_WD * w)
    return delta, m, v


def reference(x, norm_mix, w_in, conv_w, conv_b, lru_wa, lru_ba, lru_wx, lru_bx, lru_lambda, q_norm, w_uq, kv_norm, w_ukv, w_out, norm_mlp, w_up, w_down, norm_final, loss_target, m_norm_mix, m_w_in, m_conv_w, m_conv_b, m_lru_wa, m_lru_ba, m_lru_wx, m_lru_bx, m_lru_lambda, m_q_norm, m_w_uq, m_kv_norm, m_w_ukv, m_w_out, m_norm_mlp, m_w_up, m_w_down, m_norm_final, v_norm_mix, v_w_in, v_conv_w, v_conv_b, v_lru_wa, v_lru_ba, v_lru_wx, v_lru_bx, v_lru_lambda, v_q_norm, v_w_uq, v_kv_norm, v_w_ukv, v_w_out, v_norm_mlp, v_w_up, v_w_down, v_norm_final):
    given = dict(x=x, norm_mix=norm_mix, w_in=w_in, conv_w=conv_w, conv_b=conv_b, lru_wa=lru_wa, lru_ba=lru_ba, lru_wx=lru_wx, lru_bx=lru_bx, lru_lambda=lru_lambda, q_norm=q_norm, w_uq=w_uq, kv_norm=kv_norm, w_ukv=w_ukv, w_out=w_out, norm_mlp=norm_mlp, w_up=w_up, w_down=w_down, norm_final=norm_final, loss_target=loss_target, m_norm_mix=m_norm_mix, m_w_in=m_w_in, m_conv_w=m_conv_w, m_conv_b=m_conv_b, m_lru_wa=m_lru_wa, m_lru_ba=m_lru_ba, m_lru_wx=m_lru_wx, m_lru_bx=m_lru_bx, m_lru_lambda=m_lru_lambda, m_q_norm=m_q_norm, m_w_uq=m_w_uq, m_kv_norm=m_kv_norm, m_w_ukv=m_w_ukv, m_w_out=m_w_out, m_norm_mlp=m_norm_mlp, m_w_up=m_w_up, m_w_down=m_w_down, m_norm_final=m_norm_final, v_norm_mix=v_norm_mix, v_w_in=v_w_in, v_conv_w=v_conv_w, v_conv_b=v_conv_b, v_lru_wa=v_lru_wa, v_lru_ba=v_lru_ba, v_lru_wx=v_lru_wx, v_lru_bx=v_lru_bx, v_lru_lambda=v_lru_lambda, v_q_norm=v_q_norm, v_w_uq=v_w_uq, v_kv_norm=v_kv_norm, v_w_ukv=v_w_ukv, v_w_out=v_w_out, v_norm_mlp=v_norm_mlp, v_w_up=v_w_up, v_w_down=v_w_down, v_norm_final=v_norm_final)
    weights = {n: given[n] for n in TWIN_WEIGHTS}
    shared = {n: given[n] for n in SHARED_INPUTS}
    per_example = {n: given[n] for n in ['x']}
    grad_fn = _jax.value_and_grad(_loss, argnums=(0, 1))

    def one_microbatch(ex, loss_target):
        ex = dict(ex)
        diff = ex.pop(TWIN_DIFF_INPUT)
        return grad_fn(weights, diff, {**shared, **ex}, loss_target)

    if N_MICROBATCH == 1:
        loss, (grad_w, grad_x) = one_microbatch(per_example, given["loss_target"])
    else:
        def body(carry, xs):
            loss_sum, grad_sum = carry
            l_k, (gw_k, gx_k) = one_microbatch(xs[0], xs[1])
            with _jax.named_scope("update"):
                return (loss_sum + l_k, _jax.tree.map(_jnp.add, grad_sum, gw_k)), gx_k

        init = (_jnp.zeros((), _jnp.float32), _jax.tree.map(_jnp.zeros_like, weights))
        (loss, grad_w), grad_x = _jax.lax.scan(body, init, (per_example, given["loss_target"]))
    with _jax.named_scope("update"):
        delta_w, new_m, new_v = {}, {}, {}
        for n in TWIN_WEIGHTS:
            delta_w[n], new_m[n], new_v[n] = _adamw(weights[n], grad_w[n], given["m_" + n], given["v_" + n])
    return (loss, grad_x, *[grad_w[n] for n in TWIN_WEIGHTS], *[delta_w[n] for n in TWIN_WEIGHTS],
            *[new_m[n] for n in TWIN_WEIGHTS], *[new_v[n] for n in TWIN_WEIGHTS])
```

```python
import functools
import math

import jax
import jax.numpy as jnp
from jax import lax
from jax.experimental import pallas as pl
from jax.experimental.pallas import tpu as pltpu

F32 = jnp.float32
BF16 = jnp.bfloat16

D = 1024
N_HEADS = 8
QK_NOPE = 128
QK_ROPE = 64
V_HEAD = 128
Q_LORA = 256
KV_LORA = 256
D_FF = 4096
RNN_BLOCKS = 8
RNN_BW = 128
LRU_C = 8.0
EPS = 1e-6
ROPE_THETA = 10000.0
HEAD_W = 256
KR_W = 128
W1_COLS = 4 * D + Q_LORA + KV_LORA + KR_W
SM_SCALE = (QK_NOPE + QK_ROPE) ** -0.5
NEG = float(jnp.finfo(jnp.float32).min)

ADAM_LR = 0.001
ADAM_B1 = 0.9
ADAM_B2 = 0.999
ADAM_EPS = 1e-08
ADAM_WD = 0.01
ADAM_STEP = 10

N_CHIPS = 4
V7X_VMEM_LIMIT = 56 * 1024 * 1024
MESH_ID = pl.DeviceIdType.MESH

SMALL_NAMES = ("norm_mix", "conv_b", "lru_wa", "lru_ba", "lru_wx", "lru_bx", "lru_lambda", "q_norm", "kv_norm", "norm_mlp", "norm_final")
SMALL_SHAPES = ((D,), (D,), (RNN_BLOCKS, RNN_BW, RNN_BW), (RNN_BLOCKS, RNN_BW), (RNN_BLOCKS, RNN_BW, RNN_BW), (RNN_BLOCKS, RNN_BW), (D,),
                (Q_LORA,), (KV_LORA,), (D,), (D,))
SMALL_SIZES = tuple(math.prod(s) for s in SMALL_SHAPES)
N_SMALL = sum(SMALL_SIZES)
CONVW_SIZE = 4 * D
S_LEN = -(-(N_SMALL + CONVW_SIZE) // 8192) * 8192
S_ROWS_HALF = S_LEN // (N_CHIPS * 2 * 128)
PACK_ROWS = -(-(N_SMALL + CONVW_SIZE // N_CHIPS) // (256 * 128)) * 256


def _pcall(body, name, grid, in_specs, out_specs, out_shape, scratch=()):
    return pl.pallas_call(
        body, name=name, grid=grid, in_specs=in_specs, out_specs=out_specs, out_shape=out_shape,
        scratch_shapes=list(scratch),
        compiler_params=pltpu.CompilerParams(dimension_semantics=("arbitrary",) * len(grid), vmem_limit_bytes=V7X_VMEM_LIMIT))


def _rows(tm, w):
    return pl.BlockSpec((tm, w), lambda i: (i, 0))


def _full(*shape):
    return pl.BlockSpec(shape, lambda *_: (0,) * len(shape))


def _sds(shape, dtype=F32):
    return jax.ShapeDtypeStruct(shape, dtype)


def _row_tile(rows, cap=256):
    t = min(rows, cap)
    while rows % t or t % 8:
        t -= 1
    return t


def _dot(a, b):
    return jnp.dot(a, b, preferred_element_type=F32)


def _dot_nt(a, b):
    return lax.dot_general(a, b, (((1,), (1,)), ((), ())), preferred_element_type=F32)


def _dot_tn(a, b):
    return lax.dot_general(a, b, (((0,), (0,)), ((), ())), preferred_element_type=F32)


def _sigmoid(x):
    return 1.0 / (1.0 + jnp.exp(-x))


_GELU_C = math.sqrt(2.0 / math.pi)


def _gelu(x):
    return x * (0.5 * (1.0 + jnp.tanh(_GELU_C * (x + 0.044715 * (x * x * x)))))


def _gelu_grad(x):
    t = jnp.tanh(_GELU_C * (x + 0.044715 * (x * x * x)))
    cdf = 0.5 * (1.0 + t)
    return cdf + x * (0.5 * (1.0 - t * t) * _GELU_C * (1.0 + 3.0 * 0.044715 * (x * x)))


def _rms_scale(x):
    return lax.rsqrt(jnp.mean(x * x, axis=-1, keepdims=True) + EPS)


def _rms_bwd(x, rs, g, dy):
    gdy = dy * g
    dx = rs * gdy - x * ((rs * rs * rs) * jnp.mean(gdy * x, axis=-1, keepdims=True))
    return dx, dy * (x * rs)


def _log1p(e):
    u = 1.0 + e
    d = u - 1.0
    return jnp.where(d == 0.0, e, jnp.log(u) * (e / jnp.where(d == 0.0, 1.0, d)))


def _softplus(y):
    return jnp.maximum(y, 0.0) + _log1p(jnp.exp(-jnp.abs(y)))


def _expm1(x):
    u = jnp.exp(x)
    lu = jnp.log(u)
    safe = jnp.where((u == 1.0) | (u == 0.0), 1.0, lu)
    return jnp.where(u == 1.0, x, jnp.where(u == 0.0, -1.0, (u - 1.0) * (x / safe)))


def _row_iota(shape):
    return lax.broadcasted_iota(jnp.int32, shape, 0)


def _lane_iota(shape):
    return lax.broadcasted_iota(jnp.int32, shape, 1)


def _rope_pair(gc):
    return gc + pltpu.roll(gc, 64, 1)


def _inproj(x, g, w1, tm):
    t = x.shape[0]
    widths = (D, 3 * D, Q_LORA, KV_LORA, KR_W)

    def body(x_ref, g_ref, w_ref, xn_ref, rx_ref, g3_ref, cq_ref, ckv_ref, kr_ref):
        xv = x_ref[...]
        xn = (xv * _rms_scale(xv) * g_ref[...]).astype(BF16)
        xn_ref[...] = xn
        col = 0
        for ref, w in zip((rx_ref, g3_ref, cq_ref, ckv_ref, kr_ref), widths):
            for c0 in range(0, w, 512):
                cw = min(512, w - c0)
                ref[:, c0:c0 + cw] = _dot(xn, w_ref[:, col + c0:col + c0 + cw])
            col += w

    return _pcall(
        body, "inproj", (t // tm,),
        [_rows(tm, D), _full(1, D), _full(D, W1_COLS)],
        [_rows(tm, D)] + [_rows(tm, w) for w in widths],
        [_sds((t, D), BF16)] + [_sds((t, w)) for w in widths],
    )(x, g, w1)


def _lru_gates(xa, wa_ref, ba, wx_ref, bx, pre_r, pre_i):
    xb = xa.astype(BF16)
    for n in range(RNN_BLOCKS):
        sl = slice(n * RNN_BW, (n + 1) * RNN_BW)
        pre_r[:, sl] = _dot(xb[:, sl], wa_ref[n])
        pre_i[:, sl] = _dot(xb[:, sl], wx_ref[n])
    r = _sigmoid(pre_r[...] + ba)
    i = _sigmoid(pre_i[...] + bx)
    return r, i


def _lru_fwd(rx, conv_w, conv_b, wa, ba, wx, bx, lam, tb):
    t = rx.shape[0]
    nb = t // tb

    def body(x_ref, xp_ref, cw_ref, cb_ref, wa_ref, ba_ref, wx_ref, bx_ref, lam_ref, h_ref, xa_ref, hc, tmp, pre_r, pre_i):
        i_blk = pl.program_id(0)

        @pl.when(i_blk == 0)
        def _():
            hc[...] = jnp.zeros_like(hc)

        xv = x_ref[...]
        xp = jnp.where(i_blk > 0, xp_ref[...], 0.0)
        row8 = _row_iota((8, D))
        xa = cb_ref[...] + cw_ref[3:4, :] * xv
        for s in (1, 2, 3):
            xr = pltpu.roll(xv, s, 0)
            tmp[...] = xr
            tmp[0:8, :] = jnp.where(row8 < s, pltpu.roll(xp, s, 0), xr[0:8, :])
            xa = xa + cw_ref[3 - s:4 - s, :] * tmp[...]
        xa_ref[...] = xa
        r, gi = _lru_gates(xa, wa_ref, ba_ref[...], wx_ref, bx_ref[...], pre_r, pre_i)
        la = (-LRU_C * _softplus(-lam_ref[...])) * r
        a = jnp.exp(la)
        b = jnp.sqrt(-_expm1(2.0 * la)) * (gi * xa)
        row = _row_iota((tb, D))
        sh = 1
        while sh < tb:
            m = row >= sh
            b = jnp.where(m, a * pltpu.roll(b, sh, 0) + b, b)
            a = jnp.where(m, a * pltpu.roll(a, sh, 0), a)
            sh *= 2
        h = a * hc[...] + b
        h_ref[...] = h
        hc[...] = h[tb - 1:tb, :]

    prev8 = pl.BlockSpec((8, D), lambda i: (jnp.maximum(i * (tb // 8) - 1, 0), 0))
    return _pcall(
        body, "lru_fwd", (nb,),
        [_rows(tb, D), prev8, _full(4, D), _full(1, D), _full(RNN_BLOCKS, RNN_BW, RNN_BW), _full(1, D),
         _full(RNN_BLOCKS, RNN_BW, RNN_BW), _full(1, D), _full(1, D)],
        [_rows(tb, D), _rows(tb, D)],
        [_sds((t, D)), _sds((t, D))],
        scratch=[pltpu.VMEM((1, D), F32), pltpu.VMEM((tb, D), F32), pltpu.VMEM((tb, D), F32), pltpu.VMEM((tb, D), F32)],
    )(rx, rx, conv_w, conv_b, wa, ba, wx, bx, lam)


def _mla_proj(cq, ckv, kr, qn, kvn, wq, wkv, rope_c, tm):
    t = cq.shape[0]

    def body(cq_ref, ckv_ref, kr_ref, qn_ref, kvn_ref, wq_ref, wkv_ref, c_ref, q_ref, k_ref, v_ref, cqn_ref, ckvn_ref):
        cqv = cq_ref[...]
        cqn = (cqv * _rms_scale(cqv) * qn_ref[...]).astype(BF16)
        ckvv = ckv_ref[...]
        ckvn = (ckvv * _rms_scale(ckvv) * kvn_ref[...]).astype(BF16)
        cqn_ref[...] = cqn
        ckvn_ref[...] = ckvn
        c = c_ref[...]
        lane = _lane_iota((tm, KR_W))
        kro = jnp.where(lane < 64, _rope_pair(kr_ref[...] * c), 0.0).astype(BF16)
        for h in range(N_HEADS):
            sl = slice(h * HEAD_W, (h + 1) * HEAD_W)
            qh = _dot(cqn, wq_ref[:, sl])
            q_ref[h, :, 0:128] = qh[:, 0:128].astype(BF16)
            q_ref[h, :, 128:256] = _rope_pair(qh[:, 128:256] * c).astype(BF16)
            kvh = _dot(ckvn, wkv_ref[:, sl])
            k_ref[h, :, 0:128] = kvh[:, 0:128].astype(BF16)
            k_ref[h, :, 128:256] = kro
            v_ref[h] = kvh[:, 128:256].astype(BF16)

    hb = lambda w: pl.BlockSpec((N_HEADS, tm, w), lambda i: (0, i, 0))
    return _pcall(
        body, "mla_proj", (t // tm,),
        [_rows(tm, Q_LORA), _rows(tm, KV_LORA), _rows(tm, KR_W), _full(1, Q_LORA), _full(1, KV_LORA),
         _full(Q_LORA, N_HEADS * HEAD_W), _full(KV_LORA, N_HEADS * HEAD_W), _rows(tm, KR_W)],
        [hb(HEAD_W), hb(HEAD_W), hb(V_HEAD), _rows(tm, Q_LORA), _rows(tm, KV_LORA)],
        [_sds((N_HEADS, t, HEAD_W), BF16), _sds((N_HEADS, t, HEAD_W), BF16), _sds((N_HEADS, t, V_HEAD), BF16),
         _sds((t, Q_LORA), BF16), _sds((t, KV_LORA), BF16)],
    )(cq, ckv, kr, qn, kvn, wq, wkv, rope_c)


def _flash_fwd(q, k, v, tq):
    t = q.shape[1]
    nq = t // tq

    def body(q_ref, k_ref, v_ref, o_ref, lse_ref, m_s, l_s, acc):
        qi, ki = pl.program_id(1), pl.program_id(2)

        @pl.when(ki == 0)
        def _():
            m_s[...] = jnp.full_like(m_s, -jnp.inf)
            l_s[...] = jnp.zeros_like(l_s)
            acc[...] = jnp.zeros_like(acc)

        @pl.when(ki <= qi)
        def _():
            s = _dot_nt(q_ref[0], k_ref[0]) * SM_SCALE
            keep = (_row_iota((tq, tq)) >= _lane_iota((tq, tq))) | (ki < qi)
            s = jnp.where(keep, s, NEG)
            m_old = m_s[...]
            m_new = jnp.maximum(m_old, jnp.max(s, axis=1, keepdims=True))
            p = jnp.exp(s - m_new)
            alpha = jnp.exp(m_old - m_new)
            l_s[...] = alpha * l_s[...] + jnp.sum(p, axis=1, keepdims=True)
            acc[...] = alpha * acc[...] + _dot(p.astype(BF16), v_ref[0])
            m_s[...] = m_new

        @pl.when(ki == qi)
        def _():
            o_ref[...] = acc[...] / l_s[...]
            lse_ref[...] = jnp.broadcast_to(m_s[...] + jnp.log(l_s[...]), (tq, V_HEAD))

    kv_spec = lambda w: pl.BlockSpec((1, tq, w), lambda h, qi, ki: (h, jnp.minimum(ki, qi), 0))
    o_spec = pl.BlockSpec((tq, V_HEAD), lambda h, qi, ki: (qi, h))
    return _pcall(
        body, "flash_fwd", (N_HEADS, nq, nq),
        [pl.BlockSpec((1, tq, HEAD_W), lambda h, qi, ki: (h, qi, 0)), kv_spec(HEAD_W), kv_spec(V_HEAD)],
        [o_spec, o_spec],
        [_sds((t, D)), _sds((t, D))],
        scratch=[pltpu.VMEM((tq, 1), F32), pltpu.VMEM((tq, 1), F32), pltpu.VMEM((tq, V_HEAD), F32)],
    )(q, k, v)


def _merge_out(x, h, g3, yb, w_out, tm):
    t = x.shape[0]

    def body(x_ref, h_ref, g3_ref, yb_ref, w_ref, h1_ref, mg_ref):
        ya = h_ref[...] * _gelu(g3_ref[:, 0:D])
        merged = (_sigmoid(g3_ref[:, D:2 * D]) * ya + _sigmoid(g3_ref[:, 2 * D:3 * D]) * yb_ref[...]).astype(BF16)
        mg_ref[...] = merged
        h1_ref[...] = x_ref[...] + _dot(merged, w_ref[...])

    return _pcall(
        body, "merge_out", (t // tm,),
        [_rows(tm, D), _rows(tm, D), _rows(tm, 3 * D), _rows(tm, D), _full(D, D)],
        [_rows(tm, D), _rows(tm, D)],
        [_sds((t, D)), _sds((t, D), BF16)],
    )(x, h, g3, yb, w_out)


def _mlp_up(h1, g, w_up, tm):
    t = h1.shape[0]

    def body(h_ref, g_ref, w_ref, u_ref, n2_ref):
        hv = h_ref[...]
        n2 = (hv * _rms_scale(hv) * g_ref[...]).astype(BF16)
        n2_ref[...] = n2
        for c0 in range(0, D_FF, 512):
            u_ref[:, c0:c0 + 512] = _dot(n2, w_ref[:, c0:c0 + 512])

    return _pcall(
        body, "mlp_up", (t // tm,),
        [_rows(tm, D), _full(1, D), _full(D, D_FF)],
        [_rows(tm, D_FF), _rows(tm, D)],
        [_sds((t, D_FF)), _sds((t, D), BF16)],
    )(h1, g, w_up)


def _mlp_down_loss(u, h1, target, w_down, g, tm):
    t = u.shape[0]

    def body(u_ref, h1_ref, tg_ref, w_ref, g_ref, act_ref, dh2_ref, loss_ref, gnf_ref, lacc):
        i = pl.program_id(0)

        @pl.when(i == 0)
        def _():
            lacc[...] = jnp.zeros_like(lacc)
            gnf_ref[...] = jnp.zeros_like(gnf_ref)

        ru = jnp.maximum(u_ref[...], 0.0)
        act = (ru * ru).astype(BF16)
        act_ref[...] = act
        h2 = h1_ref[...] + _dot(act, w_ref[...])
        rs = _rms_scale(h2)
        gv = g_ref[...]
        err = h2 * rs * gv - tg_ref[...]
        lacc[...] += jnp.sum(err * err, axis=0, keepdims=True)
        dx, dgr = _rms_bwd(h2, rs, gv, err * (1.0 / D))
        dh2_ref[...] = dx
        gnf_ref[...] += jnp.sum(dgr, axis=0, keepdims=True)

        @pl.when(i == pl.num_programs(0) - 1)
        def _():
            loss_ref[...] = jnp.broadcast_to(jnp.sum(lacc[...], axis=1, keepdims=True) * (0.5 / D), (8, 128))

    return _pcall(
        body, "mlp_down_loss", (t // tm,),
        [_rows(tm, D_FF), _rows(tm, D), _rows(tm, D), _full(D_FF, D), _full(1, D)],
        [_rows(tm, D_FF), _rows(tm, D), _full(8, 128), _full(1, D)],
        [_sds((t, D_FF), BF16), _sds((t, D)), _sds((8, 128)), _sds((1, D))],
        scratch=[pltpu.VMEM((1, D), F32)],
    )(u, h1, target, w_down, g)


def _matmul_tn(a, g, name):
    t, kdim = a.shape
    ndim = g.shape[1]
    tk = min(kdim, 1024)
    tn = ndim if ndim <= 1024 else 1024
    tt = min(t, 512)
    nt = t // tt

    def body(a_ref, g_ref, o_ref):
        @pl.when(pl.program_id(2) == 0)
        def _():
            o_ref[...] = jnp.zeros_like(o_ref)

        o_ref[...] += _dot_tn(a_ref[...].astype(BF16), g_ref[...].astype(BF16))

    return _pcall(
        body, name, (kdim // tk, ndim // tn, nt),
        [pl.BlockSpec((tt, tk), lambda i, j, s: (s, i)), pl.BlockSpec((tt, tn), lambda i, j, s: (s, j))],
        pl.BlockSpec((tk, tn), lambda i, j, s: (i, j)),
        _sds((kdim, ndim)),
    )(a, g)


def _mlp_bwd_act(dh2, u, w_down, tm):
    t = u.shape[0]

    def body(d_ref, u_ref, w_ref, du_ref):
        db = d_ref[...].astype(BF16)
        for c0 in range(0, D_FF, 512):
            da = _dot_nt(db, w_ref[c0:c0 + 512, :])
            du_ref[:, c0:c0 + 512] = (da * (2.0 * jnp.maximum(u_ref[:, c0:c0 + 512], 0.0))).astype(BF16)

    return _pcall(
        body, "mlp_bwd_act", (t // tm,),
        [_rows(tm, D), _rows(tm, D_FF), _full(D_FF, D)],
        _rows(tm, D_FF), _sds((t, D_FF), BF16),
    )(dh2, u, w_down)


def _mlp_bwd_in(du, dh2, h1, w_up, g, tm):
    t = du.shape[0]

    def body(du_ref, d_ref, h_ref, w_ref, g_ref, dh1_ref, gacc_ref):
        @pl.when(pl.program_id(0) == 0)
        def _():
            gacc_ref[...] = jnp.zeros_like(gacc_ref)

        dn2 = _dot_nt(du_ref[...], w_ref[...])
        hv = h_ref[...]
        dx, dgr = _rms_bwd(hv, _rms_scale(hv), g_ref[...], dn2)
        dh1_ref[...] = d_ref[...] + dx
        gacc_ref[...] += jnp.sum(dgr, axis=0, keepdims=True)

    return _pcall(
        body, "mlp_bwd_in", (t // tm,),
        [_rows(tm, D_FF), _rows(tm, D), _rows(tm, D), _full(D, D_FF), _full(1, D)],
        [_rows(tm, D), _full(1, D)],
        [_sds((t, D)), _sds((1, D))],
    )(du, dh2, h1, w_up, g)


def _merge_bwd(dh1, w_out, g3, h, yb, merged, tm):
    t = dh1.shape[0]

    def body(d_ref, w_ref, g3_ref, h_ref, yb_ref, mg_ref, dg3_ref, dyb_ref, delta_ref, dh_ref, dwo_ref):
        @pl.when(pl.program_id(0) == 0)
        def _():
            dwo_ref[...] = jnp.zeros_like(dwo_ref)

        db = d_ref[...].astype(BF16)
        dwo_ref[...] += _dot_tn(mg_ref[...], db)
        dm = _dot_nt(db, w_ref[...])
        gv = g3_ref[:, 0:D]
        sa = _sigmoid(g3_ref[:, D:2 * D])
        sb = _sigmoid(g3_ref[:, 2 * D:3 * D])
        gel = _gelu(gv)
        hv = h_ref[...]
        ybv = yb_ref[...]
        dya = dm * sa
        dyb = dm * sb
        dg3_ref[:, 0:D] = (dya * hv * _gelu_grad(gv)).astype(BF16)
        dg3_ref[:, D:2 * D] = (dya * (hv * gel) * (1.0 - sa)).astype(BF16)
        dg3_ref[:, 2 * D:3 * D] = (dyb * ybv * (1.0 - sb)).astype(BF16)
        dh_ref[...] = dya * gel
        dyb_ref[...] = dyb.astype(BF16)
        prod = dyb * ybv
        for hh in range(N_HEADS):
            sl = slice(hh * V_HEAD, (hh + 1) * V_HEAD)
            delta_ref[:, sl] = jnp.broadcast_to(jnp.sum(prod[:, sl], axis=1, keepdims=True), (tm, V_HEAD))

    return _pcall(
        body, "merge_bwd", (t // tm,),
        [_rows(tm, D), _full(D, D), _rows(tm, 3 * D), _rows(tm, D), _rows(tm, D), _rows(tm, D)],
        [_rows(tm, 3 * D), _rows(tm, D), _rows(tm, D), _rows(tm, D), _full(D, D)],
        [_sds((t, 3 * D), BF16), _sds((t, D), BF16), _sds((t, D)), _sds((t, D)), _sds((D, D))],
    )(dh1, w_out, g3, h, yb, merged)


def _flash_bwd(q, k, v, do, lse, delta, tq):
    t = q.shape[1]
    nq = t // tq

    def body(q_ref, k_ref, v_ref, do_ref, lse_ref, dl_ref, dq_ref, dk_ref, dv_ref, dk_acc, dv_acc):
        ki, qi = pl.program_id(1), pl.program_id(2)

        @pl.when((ki == 0) & (qi == 0))
        def _():
            dq_ref[...] = jnp.zeros_like(dq_ref)

        @pl.when(qi == 0)
        def _():
            dk_acc[...] = jnp.zeros_like(dk_acc)
            dv_acc[...] = jnp.zeros_like(dv_acc)

        @pl.when(qi >= ki)
        def _():
            qv, kv_, dov = q_ref[0], k_ref[0], do_ref[...]
            s = _dot_nt(qv, kv_) * SM_SCALE
            keep = (_row_iota((tq, tq)) >= _lane_iota((tq, tq))) | (qi > ki)
            p = jnp.where(keep, jnp.exp(s - lse_ref[:, 0:1]), 0.0)
            dv_acc[...] += _dot_tn(p.astype(BF16), dov)
            dp = _dot_nt(dov, v_ref[0])
            ds = (p * (dp - dl_ref[:, 0:1]) * SM_SCALE).astype(BF16)
            dk_acc[...] += _dot_tn(ds, qv)
            rows = pl.ds(pl.multiple_of(qi * tq, tq), tq)
            dq_ref[0, rows, :] += _dot(ds, kv_)

        @pl.when(qi == nq - 1)
        def _():
            dk_ref[0] = dk_acc[...]
            dv_ref[0] = dv_acc[...]

    qrow = lambda h, ki, qi: (jnp.maximum(qi, ki), h)
    kv_spec = lambda w: pl.BlockSpec((1, tq, w), lambda h, ki, qi: (h, ki, 0))
    return _pcall(
        body, "flash_bwd", (N_HEADS, nq, nq),
        [pl.BlockSpec((1, tq, HEAD_W), lambda h, ki, qi: (h, jnp.maximum(qi, ki), 0)), kv_spec(HEAD_W), kv_spec(V_HEAD),
         pl.BlockSpec((tq, V_HEAD), qrow), pl.BlockSpec((tq, V_HEAD), qrow), pl.BlockSpec((tq, V_HEAD), qrow)],
        [pl.BlockSpec((1, t, HEAD_W), lambda h, ki, qi: (h, 0, 0)), kv_spec(HEAD_W), kv_spec(V_HEAD)],
        [_sds((N_HEADS, t, HEAD_W)), _sds((N_HEADS, t, HEAD_W)), _sds((N_HEADS, t, V_HEAD))],
        scratch=[pltpu.VMEM((tq, HEAD_W), F32), pltpu.VMEM((tq, V_HEAD), F32)],
    )(q, k, v, do, lse, delta)


def _mla_bwd(dq, dk, dv, cqn, ckvn, cq, ckv, rope_c, wq, wkv, qn, kvn, tm):
    t = cq.shape[0]

    def body(dq_ref, dk_ref, dv_ref, cqn_ref, ckvn_ref, cq_ref, ckv_ref, c_ref, wq_ref, wkv_ref, qn_ref, kvn_ref,
             dmla_ref, dwq_ref, dwkv_ref, dqn_ref, dkvn_ref):
        @pl.when(pl.program_id(0) == 0)
        def _():
            dwq_ref[...] = jnp.zeros_like(dwq_ref)
            dwkv_ref[...] = jnp.zeros_like(dwkv_ref)
            dqn_ref[...] = jnp.zeros_like(dqn_ref)
            dkvn_ref[...] = jnp.zeros_like(dkvn_ref)

        c = c_ref[...]
        lane = _lane_iota((tm, KR_W))
        cqn, ckvn = cqn_ref[...], ckvn_ref[...]
        dcqn = jnp.zeros((tm, Q_LORA), F32)
        dckvn = jnp.zeros((tm, KV_LORA), F32)
        dkr = jnp.zeros((tm, KR_W), F32)
        for h in range(N_HEADS):
            sl = slice(h * HEAD_W, (h + 1) * HEAD_W)
            droped = jnp.where(lane < 64, dq_ref[h, :, 128:256], 0.0)
            dqp = jnp.concatenate([dq_ref[h, :, 0:128], _rope_pair(droped) * c], axis=1).astype(BF16)
            dcqn = dcqn + _dot_nt(dqp, wq_ref[:, sl])
            dwq_ref[:, sl] += _dot_tn(cqn, dqp)
            dkr = dkr + jnp.where(lane < 64, dk_ref[h, :, 128:256], 0.0)
            dkvp = jnp.concatenate([dk_ref[h, :, 0:128], dv_ref[h]], axis=1).astype(BF16)
            dckvn = dckvn + _dot_nt(dkvp, wkv_ref[:, sl])
            dwkv_ref[:, sl] += _dot_tn(ckvn, dkvp)
        cqv, ckvv = cq_ref[...], ckv_ref[...]
        dcq, dgq = _rms_bwd(cqv, _rms_scale(cqv), qn_ref[...], dcqn)
        dckv, dgkv = _rms_bwd(ckvv, _rms_scale(ckvv), kvn_ref[...], dckvn)
        dqn_ref[...] += jnp.sum(dgq, axis=0, keepdims=True)
        dkvn_ref[...] += jnp.sum(dgkv, axis=0, keepdims=True)
        dmla_ref[:, 0:256] = dcq.astype(BF16)
        dmla_ref[:, 256:512] = dckv.astype(BF16)
        dmla_ref[:, 512:640] = (_rope_pair(dkr) * c).astype(BF16)

    hb = lambda w: pl.BlockSpec((N_HEADS, tm, w), lambda i: (0, i, 0))
    wide = N_HEADS * HEAD_W
    return _pcall(
        body, "mla_bwd", (t // tm,),
        [hb(HEAD_W), hb(HEAD_W), hb(V_HEAD), _rows(tm, Q_LORA), _rows(tm, KV_LORA), _rows(tm, Q_LORA), _rows(tm, KV_LORA),
         _rows(tm, KR_W), _full(Q_LORA, wide), _full(KV_LORA, wide), _full(1, Q_LORA), _full(1, KV_LORA)],
        [_rows(tm, 640), _full(Q_LORA, wide), _full(KV_LORA, wide), _full(1, Q_LORA), _full(1, KV_LORA)],
        [_sds((t, 640), BF16), _sds((Q_LORA, wide)), _sds((KV_LORA, wide)), _sds((1, Q_LORA)), _sds((1, KV_LORA))],
    )(dq, dk, dv, cqn, ckvn, cq, ckv, rope_c, wq, wkv, qn, kvn)


def _lru_bwd(dh, xa, h, rx, conv_w, wa, ba, wx, bx, lam, tb):
    t = dh.shape[0]
    nb = t // tb

    def body(dh_ref, xa_ref, h_ref, hp_ref, x_ref, cw_ref, wa_ref, ba_ref, wx_ref, bx_ref, lam_ref,
             drx_ref, dcw_ref, dcb_ref, dwa_ref, dba_ref, dwx_ref, dbx_ref, dlam_ref, gc, dxn, tmp, pre_r, pre_i):
        step = pl.program_id(0)
        first_block = step == nb - 1

        @pl.when(step == 0)
        def _():
            gc[...] = jnp.zeros_like(gc)
            dxn[...] = jnp.zeros_like(dxn)
            for ref in (dcw_ref, dcb_ref, dwa_ref, dba_ref, dwx_ref, dbx_ref, dlam_ref):
                ref[...] = jnp.zeros_like(ref)

        xa = xa_ref[...]
        r, gi = _lru_gates(xa, wa_ref, ba_ref[...], wx_ref, bx_ref[...], pre_r, pre_i)
        lamv = lam_ref[...]
        sp = _softplus(-lamv)
        la = (-LRU_C * sp) * r
        a = jnp.exp(la)
        e2 = _expm1(2.0 * la)
        sq = jnp.sqrt(-e2)
        row = _row_iota((tb, D))
        cf = jnp.where(row == tb - 1, 1.0, pltpu.roll(a, tb - 1, 0))
        bv = dh_ref[...]
        sh = 1
        while sh < tb:
            m = row < tb - sh
            bv = jnp.where(m, bv + cf * pltpu.roll(bv, tb - sh, 0), bv)
            cf = jnp.where(m, cf * pltpu.roll(cf, tb - sh, 0), cf)
            sh *= 2
        delta = bv + cf * gc[...]
        gc[...] = a[0:1, :] * delta[0:1, :]
        hv = h_ref[...]
        hr = pltpu.roll(hv, 1, 0)
        tmp[...] = hr
        tmp[0:1, :] = jnp.where(first_block, 0.0, hp_ref[7:8, :])
        hprev = tmp[...]
        ix = gi * xa
        dla = (delta * hprev) * a - (delta * ix) * ((e2 + 1.0) / sq)
        dlam_ref[...] += jnp.sum(dla * r, axis=0, keepdims=True) * (LRU_C * _sigmoid(-lamv))
        dpr = (dla * (-LRU_C * sp)) * r * (1.0 - r)
        dsq = delta * sq
        dpi = (dsq * xa) * gi * (1.0 - gi)
        dba_ref[...] += jnp.sum(dpr, axis=0, keepdims=True)
        dbx_ref[...] += jnp.sum(dpi, axis=0, keepdims=True)
        pre_r[...] = dpr
        pre_i[...] = dpi
        xb = xa.astype(BF16)
        for n in range(RNN_BLOCKS):
            sl = slice(n * RNN_BW, (n + 1) * RNN_BW)
            dprn = pre_r[:, sl].astype(BF16)
            dpin = pre_i[:, sl].astype(BF16)
            dwa_ref[n] += _dot_tn(xb[:, sl], dprn)
            dwx_ref[n] += _dot_tn(xb[:, sl], dpin)
            tmp[:, sl] = _dot_nt(dprn, wa_ref[n]) + _dot_nt(dpin, wx_ref[n])
        dxa = dsq * gi + tmp[...]
        dcb_ref[...] += jnp.sum(dxa, axis=0, keepdims=True)
        xv = x_ref[...]
        drx = cw_ref[3:4, :] * dxa
        dcw_ref[3:4, :] += jnp.sum(dxa * xv, axis=0, keepdims=True)
        row8 = _row_iota((8, D))
        nxt = dxn[...]
        for s in (1, 2, 3):
            dr_ = pltpu.roll(dxa, tb - s, 0)
            tmp[...] = dr_
            tmp[tb - 8:tb, :] = jnp.where(row8 >= 8 - s, pltpu.roll(nxt, 8 - s, 0), dr_[tb - 8:tb, :])
            dxs = tmp[...]
            drx = drx + cw_ref[3 - s:4 - s, :] * dxs
            dcw_ref[3 - s:4 - s, :] += jnp.sum(dxs * xv, axis=0, keepdims=True)
        drx_ref[...] = drx.astype(BF16)
        dxn[...] = dxa[0:8, :]

    rev = pl.BlockSpec((tb, D), lambda i: (nb - 1 - i, 0))
    prev8 = pl.BlockSpec((8, D), lambda i: (jnp.maximum((nb - 1 - i) * (tb // 8) - 1, 0), 0))
    wblk = _full(RNN_BLOCKS, RNN_BW, RNN_BW)
    return _pcall(
        body, "lru_bwd", (nb,),
        [rev, rev, rev, prev8, rev, _full(4, D), wblk, _full(1, D), wblk, _full(1, D), _full(1, D)],
        [rev, _full(4, D), _full(1, D), wblk, _full(1, D), wblk, _full(1, D), _full(1, D)],
        [_sds((t, D), BF16), _sds((4, D)), _sds((1, D)), _sds((RNN_BLOCKS, RNN_BW, RNN_BW)), _sds((1, D)),
         _sds((RNN_BLOCKS, RNN_BW, RNN_BW)), _sds((1, D)), _sds((1, D))],
        scratch=[pltpu.VMEM((1, D), F32), pltpu.VMEM((8, D), F32), pltpu.VMEM((tb, D), F32), pltpu.VMEM((tb, D), F32),
                 pltpu.VMEM((tb, D), F32)],
    )(dh, xa, h, h, rx, conv_w, wa, ba, wx, bx, lam)


def _inproj_bwd(x, dh1, drx, dg3, dmla, w1, g, tm):
    t = x.shape[0]

    def body(x_ref, d_ref, drx_ref, dg3_ref, dmla_ref, w_ref, g_ref, dx_ref, gacc_ref):
        @pl.when(pl.program_id(0) == 0)
        def _():
            gacc_ref[...] = jnp.zeros_like(gacc_ref)

        dxn = _dot_nt(drx_ref[...], w_ref[:, 0:D])
        for c0 in range(0, 3 * D, D):
            dxn = dxn + _dot_nt(dg3_ref[:, c0:c0 + D], w_ref[:, D + c0:2 * D + c0])
        dxn = dxn + _dot_nt(dmla_ref[...], w_ref[:, 4 * D:W1_COLS])
        xv = x_ref[...]
        dx, dgr = _rms_bwd(xv, _rms_scale(xv), g_ref[...], dxn)
        dx_ref[...] = d_ref[...] + dx
        gacc_ref[...] += jnp.sum(dgr, axis=0, keepdims=True)

    return _pcall(
        body, "inproj_bwd", (t // tm,),
        [_rows(tm, D), _rows(tm, D), _rows(tm, D), _rows(tm, 3 * D), _rows(tm, 640), _full(D, W1_COLS), _full(1, D)],
        [_rows(tm, D), _full(1, D)],
        [_sds((t, D)), _sds((1, D))],
    )(x, dh1, drx, dg3, dmla, w1, g)


def _add2(a, b, out_dtype, name):
    rows, cols = a.shape
    tm = _row_tile(rows)

    def body(a_ref, b_ref, o_ref):
        o_ref[...] = (a_ref[...] + b_ref[...]).astype(out_dtype)

    return _pcall(body, name, (rows // tm,), [_rows(tm, cols), _rows(tm, cols)], _rows(tm, cols), _sds((rows, cols), out_dtype))(a, b)


def _sum4(a, name):
    _, rows, cols = a.shape
    tm = _row_tile(rows)

    def body(a_ref, o_ref):
        o_ref[...] = ((a_ref[0].astype(F32) + a_ref[1].astype(F32)) + a_ref[2].astype(F32)) + a_ref[3].astype(F32)

    return _pcall(body, name, (rows // tm,), [pl.BlockSpec((N_CHIPS, tm, cols), lambda i: (0, i, 0))], _rows(tm, cols),
                  _sds((rows, cols)))(a)


def _adamw(w, g, m, v, name):
    rows, cols = w.shape
    tm = _row_tile(rows)

    def body(w_ref, g_ref, m_ref, v_ref, d_ref, mo_ref, vo_ref):
        gv = g_ref[...]
        mn = ADAM_B1 * m_ref[...] + (1.0 - ADAM_B1) * gv
        vn = ADAM_B2 * v_ref[...] + (1.0 - ADAM_B2) * (gv * gv)
        m_hat = mn / (1.0 - ADAM_B1 ** ADAM_STEP)
        v_hat = vn / (1.0 - ADAM_B2 ** ADAM_STEP)
        d_ref[...] = -ADAM_LR * (m_hat / (jnp.sqrt(v_hat) + ADAM_EPS) + ADAM_WD * w_ref[...])
        mo_ref[...] = mn
        vo_ref[...] = vn

    spec = _rows(tm, cols)
    return _pcall(body, name, (rows // tm,), [spec] * 4, [spec] * 3, [_sds((rows, cols))] * 3)(w, g, m, v)


REL_SIBLING = (0, 0, 1)
REL_CHIPS = ((1, 0, 0), (0, 1, 0), (1, 1, 0))


def _comm(name, ins, out_shapes, n_ops, ops_fn):
    n_in, n_out = len(ins), len(out_shapes)

    def body(*refs):
        in_refs, out_refs = refs[:n_in], refs[n_in:n_in + n_out]
        send_sems, recv_sems = refs[n_in + n_out:]
        pos = (lax.axis_index("x"), lax.axis_index("y"), lax.axis_index("c"))
        ops = ops_fn(in_refs, out_refs, pos)
        assert len(ops) == n_ops
        copies = []
        for i, (rel, src, dst) in enumerate(ops):
            if rel is None:
                cp = pltpu.make_async_copy(src, dst, send_sems.at[i])
            else:
                peer = tuple((p + r) % 2 for p, r in zip(pos, rel))
                cp = pltpu.make_async_remote_copy(src_ref=src, dst_ref=dst, send_sem=send_sems.at[i], recv_sem=recv_sems.at[i],
                                                  device_id=peer, device_id_type=MESH_ID)
            cp.start()
            copies.append(cp)
        for cp in copies:
            cp.wait()

    hbm = pl.BlockSpec(memory_space=pl.ANY)
    return pl.pallas_call(
        body, name=name, in_specs=[hbm] * n_in, out_specs=[hbm] * n_out, out_shape=list(out_shapes),
        scratch_shapes=[pltpu.SemaphoreType.DMA((n_ops,)), pltpu.SemaphoreType.DMA((n_ops,))],
    )(*ins)


def _chip_of(pos, rel=(0, 0, 0)):
    return 2 * ((pos[0] + rel[0]) % 2) + (pos[1] + rel[1]) % 2


def _gather_chips(shards, name):
    def ops_fn(in_refs, out_refs, pos):
        me = _chip_of(pos)
        ops = []
        for src, dst in zip(in_refs, out_refs):
            ops.append((None, src, dst.at[me]))
            ops += [(rel, src, dst.at[me]) for rel in REL_CHIPS]
        return ops

    return _comm(name, shards, [_sds((N_CHIPS,) + s.shape, s.dtype) for s in shards], 4 * len(shards), ops_fn)


def _sibling_split(gs):
    def ops_fn(in_refs, out_refs, pos):
        c = pos[2]
        ops = []
        for a, src in enumerate(in_refs):
            ops.append((None, src.at[c], out_refs[2 * a]))
            ops.append((REL_SIBLING, src.at[1 - c], out_refs[2 * a + 1]))
        return ops

    shapes = []
    for g in gs:
        shapes += [_sds(g.shape[1:], g.dtype)] * 2
    outs = _comm("grad_sibling_split", gs, shapes, 2 * len(gs), ops_fn)
    return outs[0::2], outs[1::2]


def _chip_exchange(ps):
    def ops_fn(in_refs, out_refs, pos):
        me = _chip_of(pos)
        ops = []
        for src, dst in zip(in_refs, out_refs):
            ops.append((None, src.at[me], dst.at[me]))
            ops += [(rel, src.at[_chip_of(pos, rel)], dst.at[me]) for rel in REL_CHIPS]
        return ops

    return _comm("grad_chip_exchange", ps, [_sds(p.shape, p.dtype) for p in ps], 4 * len(ps), ops_fn)


def _sibling_join(hs):
    def ops_fn(in_refs, out_refs, pos):
        c = pos[2]
        ops = []
        for src, dst in zip(in_refs, out_refs):
            ops.append((None, src, dst.at[c]))
            ops.append((REL_SIBLING, src, dst.at[c]))
        return ops

    return _comm("grad_sibling_join", hs, [_sds((2,) + h.shape, h.dtype) for h in hs], 2 * len(hs), ops_fn)


def _rot_cols(w):
    return jnp.concatenate([-w[..., 32:], w[..., :32]], axis=-1)


def _unrot_cols(dw):
    return jnp.concatenate([dw[..., 32:], -dw[..., :32]], axis=-1)


IN_OFFS = (0, 1024, 2048, 2304, 2560, 2624, 3648, 4672)


def _w1_from_w_in(w):
    seg = [w[:, IN_OFFS[i]:IN_OFFS[i + 1]] for i in range(7)]
    rnn_x, rnn_gate, cq, ckv, kr, ga, gb = seg
    return jnp.concatenate([rnn_x, rnn_gate, ga, gb, cq, ckv, kr, _rot_cols(kr)], axis=1)


def _w_in_grad_from_parts(d_rx, d_g3, d_mla):
    kr = d_mla[:, 512:576] + _unrot_cols(d_mla[:, 576:640])
    return jnp.concatenate([d_rx, d_g3[:, 0:D], d_mla[:, 0:512], kr, d_g3[:, D:3 * D]], axis=1)


def _wq_from_w_uq(w):
    w3 = w.reshape(Q_LORA, N_HEADS, QK_NOPE + QK_ROPE)
    rope = w3[..., QK_NOPE:]
    return jnp.concatenate([w3[..., :QK_NOPE], rope, _rot_cols(rope)], axis=-1).reshape(Q_LORA, N_HEADS * HEAD_W)


def _w_uq_grad_from_wq(dw):
    d3 = dw.reshape(Q_LORA, N_HEADS, HEAD_W)
    rope = d3[..., 128:192] + _unrot_cols(d3[..., 192:256])
    return jnp.concatenate([d3[..., :128], rope], axis=-1).reshape(Q_LORA, N_HEADS * (QK_NOPE + QK_ROPE))


def _cols_from_chunks(g):
    return g.transpose(1, 0, 2).reshape(g.shape[1], N_CHIPS * g.shape[2])


def _halves_of_col_chunks(dw):
    r, c4 = dw.shape
    return dw.reshape(2, r // 2, N_CHIPS, c4 // N_CHIPS).transpose(0, 2, 1, 3)


def _halves_of_row_chunks(dw):
    r4, c = dw.shape
    return dw.reshape(N_CHIPS, 2, r4 // (2 * N_CHIPS), c).transpose(1, 0, 2, 3)


def kernel(x, norm_mix, w_in, conv_w, conv_b, lru_wa, lru_ba, lru_wx, lru_bx, lru_lambda, q_norm, w_uq, kv_norm, w_ukv, w_out, norm_mlp, w_up, w_down, norm_final, loss_target, m_norm_mix, m_w_in, m_conv_w, m_conv_b, m_lru_wa, m_lru_ba, m_lru_wx, m_lru_bx, m_lru_lambda, m_q_norm, m_w_uq, m_kv_norm, m_w_ukv, m_w_out, m_norm_mlp, m_w_up, m_w_down, m_norm_final, v_norm_mix, v_w_in, v_conv_w, v_conv_b, v_lru_wa, v_lru_ba, v_lru_wx, v_lru_bx, v_lru_lambda, v_q_norm, v_w_uq, v_kv_norm, v_w_ukv, v_w_out, v_norm_mlp, v_w_up, v_w_down, v_norm_final):
    t = x.shape[1]
    tm = min(256, t)
    tq = min(512, max(128, t // 4))
    x2 = x[0]
    target = loss_target[0]
    chip = 2 * lax.axis_index("x") + lax.axis_index("y")
    row = lambda p: p.reshape(1, -1)

    big_shards = (w_in, w_uq, w_ukv, w_out, w_up, w_down)
    gathered = _gather_chips([w.astype(BF16) for w in big_shards] + [conv_w], "weight_gather")
    w1 = _w1_from_w_in(_cols_from_chunks(gathered[0]))
    wq = _wq_from_w_uq(_cols_from_chunks(gathered[1]))
    wkv = _cols_from_chunks(gathered[2])
    w_out_f = gathered[3].reshape(D, D)
    w_up_f = _cols_from_chunks(gathered[4])
    w_down_f = gathered[5].reshape(D_FF, D)
    conv_w_f = _cols_from_chunks(gathered[6])
    wa_b, wx_b = lru_wa.astype(BF16), lru_wx.astype(BF16)

    pos = jnp.arange(t, dtype=F32)
    inv_freq = 1.0 / (ROPE_THETA ** (jnp.arange(0, QK_ROPE, 2, dtype=F32) / QK_ROPE))
    ang = pos[:, None] * inv_freq[None, :]
    rope_c = jnp.concatenate([jnp.cos(ang), jnp.cos(ang), jnp.sin(ang), jnp.sin(ang)], axis=-1)

    xn, rx, g3, cq, ckv, kr = _inproj(x2, row(norm_mix), w1, tm)
    h, xa = _lru_fwd(rx, conv_w_f, row(conv_b), wa_b, row(lru_ba), wx_b, row(lru_bx), row(lru_lambda), tm)
    q, k, v, cqn, ckvn = _mla_proj(cq, ckv, kr, row(q_norm), row(kv_norm), wq, wkv, rope_c, tm)
    yb, lse = _flash_fwd(q, k, v, tq)
    h1, merged = _merge_out(x2, h, g3, yb, w_out_f, tm)
    u, n2 = _mlp_up(h1, row(norm_mlp), w_up_f, tm)
    act, dh2, loss_blk, g_norm_final = _mlp_down_loss(u, h1, target, w_down_f, row(norm_final), tm)
    loss = lax.psum(loss_blk[0, 0], ("x", "y", "c"))

    g_w_down = _matmul_tn(act, dh2, "grad_w_down")
    du = _mlp_bwd_act(dh2, u, w_down_f, tm)
    dh1, g_norm_mlp = _mlp_bwd_in(du, dh2, h1, w_up_f, row(norm_mlp), tm)
    g_w_up = _matmul_tn(n2, du, "grad_w_up")
    dg3, dyb, delta, dh, g_w_out = _merge_bwd(dh1, w_out_f, g3, h, yb, merged, tm)
    dq, dk, dv = _flash_bwd(q, k, v, dyb, lse, delta, tq)
    dmla, g_wq, g_wkv, g_q_norm, g_kv_norm = _mla_bwd(dq, dk, dv, cqn, ckvn, cq, ckv, rope_c, wq, wkv, row(q_norm), row(kv_norm), tm)
    drx, g_conv_w, g_conv_b, g_wa, g_ba, g_wx, g_bx, g_lam = _lru_bwd(
        dh, xa, h, rx, conv_w_f, wa_b, row(lru_ba), wx_b, row(lru_bx), row(lru_lambda), tm)
    grad_x, g_norm_mix = _inproj_bwd(x2, dh1, drx, dg3, dmla, w1, row(norm_mix), tm)
    g_w_in = _w_in_grad_from_parts(_matmul_tn(xn, drx, "grad_w_in_rx"), _matmul_tn(xn, dg3, "grad_w_in_gates"),
                                   _matmul_tn(xn, dmla, "grad_w_in_mla"))
    g_w_uq = _w_uq_grad_from_wq(g_wq)

    smalls = (g_norm_mix, g_conv_b, g_wa, g_ba, g_wx, g_bx, g_lam, g_q_norm, g_kv_norm, g_norm_mlp, g_norm_final, g_conv_w)
    s_flat = jnp.concatenate([s.reshape(-1) for s in smalls] + [jnp.zeros((S_LEN - N_SMALL - CONVW_SIZE,), F32)])
    halves = [_halves_of_col_chunks(g_w_in), _halves_of_col_chunks(g_w_uq), _halves_of_col_chunks(g_wkv),
              _halves_of_row_chunks(g_w_out), _halves_of_col_chunks(g_w_up), _halves_of_row_chunks(g_w_down),
              s_flat.reshape(N_CHIPS, 2, S_ROWS_HALF, 128).transpose(1, 0, 2, 3)]
    mine, theirs = _sibling_split(halves)
    parts = []
    for a, (p, r) in enumerate(zip(mine, theirs)):
        flat = lambda z: z.reshape(-1, z.shape[-1])
        dt = F32 if a == len(halves) - 1 else BF16
        parts.append(_add2(flat(p), flat(r), dt, f"grad_pair_sum_{a}").reshape(p.shape[:-1] + (p.shape[-1],)))
    received = _chip_exchange(parts)
    reduced = [_sum4(r, f"grad_chip_sum_{a}") for a, r in enumerate(received)]
    joined = _sibling_join(reduced)
    g_big = [j.reshape(-1, j.shape[-1]) for j in joined[:-1]]
    s_all = _gather_chips([joined[-1].reshape(2 * S_ROWS_HALF, 128)], "small_grad_gather")[0].reshape(-1)

    small_grads = []
    off = 0
    for shp, n in zip(SMALL_SHAPES, SMALL_SIZES):
        small_grads.append(s_all[off:off + n].reshape(shp))
        off += n
    g_conv_w_mine = lax.dynamic_slice_in_dim(s_all[off:off + CONVW_SIZE].reshape(4, D), chip * (D // N_CHIPS), D // N_CHIPS, axis=1)

    big_m = (m_w_in, m_w_uq, m_w_ukv, m_w_out, m_w_up, m_w_down)
    big_v = (v_w_in, v_w_uq, v_w_ukv, v_w_out, v_w_up, v_w_down)
    big_names = ("w_in", "w_uq", "w_ukv", "w_out", "w_up", "w_down")
    big_upd = [_adamw(w, g, m, v, "adamw_" + n) for w, g, m, v, n in zip(big_shards, g_big, big_m, big_v, big_names)]

    small_w = (norm_mix, conv_b, lru_wa, lru_ba, lru_wx, lru_bx, lru_lambda, q_norm, kv_norm, norm_mlp, norm_final)
    small_m = (m_norm_mix, m_conv_b, m_lru_wa, m_lru_ba, m_lru_wx, m_lru_bx, m_lru_lambda, m_q_norm, m_kv_norm, m_norm_mlp, m_norm_final)
    small_v = (v_norm_mix, v_conv_b, v_lru_wa, v_lru_ba, v_lru_wx, v_lru_bx, v_lru_lambda, v_q_norm, v_kv_norm, v_norm_mlp, v_norm_final)

    def pack(items, last, fill):
        flat = jnp.concatenate([i.reshape(-1) for i in items] + [last.reshape(-1)])
        return jnp.concatenate([flat, jnp.full((PACK_ROWS * 128 - flat.shape[0],), fill, F32)]).reshape(PACK_ROWS, 128)

    packed = _adamw(pack(small_w, conv_w, 0.0), pack(small_grads, g_conv_w_mine, 0.0), pack(small_m, m_conv_w, 0.0),
                    pack(small_v, v_conv_w, 1.0), "adamw_small")

    def unpack(p):
        flat = p.reshape(-1)
        outs, o = [], 0
        for shp, n in zip(SMALL_SHAPES, SMALL_SIZES):
            outs.append(flat[o:o + n].reshape(shp))
            o += n
        return outs, flat[o:o + CONVW_SIZE // N_CHIPS].reshape(4, D // N_CHIPS)

    order = ("norm_mix", "w_in", "conv_w", "conv_b", "lru_wa", "lru_ba", "lru_wx", "lru_bx", "lru_lambda", "q_norm", "w_uq", "kv_norm",
             "w_ukv", "w_out", "norm_mlp", "w_up", "w_down", "norm_final")

    def assemble(small_list, conv_w_item, big_list):
        table = dict(zip(SMALL_NAMES, small_list))
        table["conv_w"] = conv_w_item
        table.update(zip(big_names, big_list))
        return [table[n] for n in order]

    outs = [loss, grad_x.reshape(1, t, D)]
    outs += assemble(small_grads, g_conv_w_mine, g_big)
    for j in range(3):
        sm, cw = unpack(packed[j])
        outs += assemble(sm, cw, [b[j] for b in big_upd])
    return tuple(outs)
```

```python
import functools
import math

import jax
import jax.numpy as jnp
from jax import lax
from jax.experimental import pallas as pl
from jax.experimental.pallas import tpu as pltpu

F32 = jnp.float32
BF16 = jnp.bfloat16

D = 1024
N_HEADS = 8
QK_NOPE = 128
QK_ROPE = 64
V_HEAD = 128
Q_LORA = 256
KV_LORA = 256
D_FF = 4096
RNN_BLOCKS = 8
RNN_BW = 128
LRU_C = 8.0
EPS = 1e-6
ROPE_THETA = 10000.0
HEAD_W = 256
KR_W = 128
W1_COLS = 4 * D + Q_LORA + KV_LORA + KR_W
SM_SCALE = (QK_NOPE + QK_ROPE) ** -0.5
NEG = float(jnp.finfo(jnp.float32).min)

ADAM_LR = 0.001
ADAM_B1 = 0.9
ADAM_B2 = 0.999
ADAM_EPS = 1e-08
ADAM_WD = 0.01
ADAM_STEP = 10

N_CHIPS = 4
V7X_VMEM_LIMIT = 56 * 1024 * 1024
MESH_ID = pl.DeviceIdType.MESH

SMALL_NAMES = ("norm_mix", "conv_b", "lru_wa", "lru_ba", "lru_wx", "lru_bx", "lru_lambda", "q_norm", "kv_norm", "norm_mlp", "norm_final")
SMALL_SHAPES = ((D,), (D,), (RNN_BLOCKS, RNN_BW, RNN_BW), (RNN_BLOCKS, RNN_BW), (RNN_BLOCKS, RNN_BW, RNN_BW), (RNN_BLOCKS, RNN_BW), (D,),
                (Q_LORA,), (KV_LORA,), (D,), (D,))
SMALL_SIZES = tuple(math.prod(s) for s in SMALL_SHAPES)
N_SMALL = sum(SMALL_SIZES)
CONVW_SIZE = 4 * D
S_LEN = -(-(N_SMALL + CONVW_SIZE) // 8192) * 8192
S_ROWS_HALF = S_LEN // (N_CHIPS * 2 * 128)
PACK_ROWS = -(-(N_SMALL + CONVW_SIZE // N_CHIPS) // (256 * 128)) * 256


def _pcall(body, name, grid, in_specs, out_specs, out_shape, scratch=()):
    return pl.pallas_call(
        body, name=name, grid=grid, in_specs=in_specs, out_specs=out_specs, out_shape=out_shape,
        scratch_shapes=list(scratch),
        compiler_params=pltpu.CompilerParams(dimension_semantics=("arbitrary",) * len(grid), vmem_limit_bytes=V7X_VMEM_LIMIT))


def _rows(tm, w):
    return pl.BlockSpec((tm, w), lambda i: (i, 0))


def _full(*shape):
    return pl.BlockSpec(shape, lambda *_: (0,) * len(shape))


def _sds(shape, dtype=F32):
    return jax.ShapeDtypeStruct(shape, dtype)


def _row_tile(rows, cap=256):
    t = min(rows, cap)
    while rows % t or t % 8:
        t -= 1
    return t


def _dot(a, b):
    return jnp.dot(a, b, preferred_element_type=F32)


def _dot_nt(a, b):
    return lax.dot_general(a, b, (((1,), (1,)), ((), ())), preferred_element_type=F32)


def _dot_tn(a, b):
    return lax.dot_general(a, b, (((0,), (0,)), ((), ())), preferred_element_type=F32)


def _sigmoid(x):
    return 1.0 / (1.0 + jnp.exp(-x))


_GELU_C = math.sqrt(2.0 / math.pi)


def _gelu(x):
    return x * (0.5 * (1.0 + jnp.tanh(_GELU_C * (x + 0.044715 * (x * x * x)))))


def _gelu_grad(x):
    t = jnp.tanh(_GELU_C * (x + 0.044715 * (x * x * x)))
    cdf = 0.5 * (1.0 + t)
    return cdf + x * (0.5 * (1.0 - t * t) * _GELU_C * (1.0 + 3.0 * 0.044715 * (x * x)))


def _rms_scale(x):
    return lax.rsqrt(jnp.mean(x * x, axis=-1, keepdims=True) + EPS)


def _rms_bwd(x, rs, g, dy):
    gdy = dy * g
    dx = rs * gdy - x * ((rs * rs * rs) * jnp.mean(gdy * x, axis=-1, keepdims=True))
    return dx, dy * (x * rs)


def _log1p(e):
    u = 1.0 + e
    d = u - 1.0
    return jnp.where(d == 0.0, e, jnp.log(u) * (e / jnp.where(d == 0.0, 1.0, d)))


def _softplus(y):
    return jnp.maximum(y, 0.0) + _log1p(jnp.exp(-jnp.abs(y)))


def _expm1(x):
    u = jnp.exp(x)
    lu = jnp.log(u)
    safe = jnp.where((u == 1.0) | (u == 0.0), 1.0, lu)
    return jnp.where(u == 1.0, x, jnp.where(u == 0.0, -1.0, (u - 1.0) * (x / safe)))


def _row_iota(shape):
    return lax.broadcasted_iota(jnp.int32, shape, 0)


def _lane_iota(shape):
    return lax.broadcasted_iota(jnp.int32, shape, 1)


def _rope_pair(gc):
    return gc + pltpu.roll(gc, 64, 1)


def _inproj(x, g, w1, tm):
    t = x.shape[0]
    widths = (D, 3 * D, Q_LORA, KV_LORA, KR_W)

    def body(x_ref, g_ref, w_ref, xn_ref, rx_ref, g3_ref, cq_ref, ckv_ref, kr_ref):
        xv = x_ref[...]
        xn = (xv * _rms_scale(xv) * g_ref[...]).astype(BF16)
        xn_ref[...] = xn
        col = 0
        for ref, w in zip((rx_ref, g3_ref, cq_ref, ckv_ref, kr_ref), widths):
            for c0 in range(0, w, 512):
                cw = min(512, w - c0)
                ref[:, c0:c0 + cw] = _dot(xn, w_ref[:, col + c0:col + c0 + cw])
            col += w

    return _pcall(
        body, "inproj", (t // tm,),
        [_rows(tm, D), _full(1, D), _full(D, W1_COLS)],
        [_rows(tm, D)] + [_rows(tm, w) for w in widths],
        [_sds((t, D), BF16)] + [_sds((t, w)) for w in widths],
    )(x, g, w1)


def _lru_gates(xa, wa_ref, ba, wx_ref, bx, pre_r, pre_i):
    xb = xa.astype(BF16)
    for n in range(RNN_BLOCKS):
        sl = slice(n * RNN_BW, (n + 1) * RNN_BW)
        pre_r[:, sl] = _dot(xb[:, sl], wa_ref[n])
        pre_i[:, sl] = _dot(xb[:, sl], wx_ref[n])
    r = _sigmoid(pre_r[...] + ba)
    i = _sigmoid(pre_i[...] + bx)
    return r, i


def _lru_fwd(rx, conv_w, conv_b, wa, ba, wx, bx, lam, tb):
    t = rx.shape[0]
    nb = t // tb

    def body(x_ref, xp_ref, cw_ref, cb_ref, wa_ref, ba_ref, wx_ref, bx_ref, lam_ref, h_ref, xa_ref, hc, tmp, pre_r, pre_i):
        i_blk = pl.program_id(0)

        @pl.when(i_blk == 0)
        def _():
            hc[...] = jnp.zeros_like(hc)

        xv = x_ref[...]
        xp = jnp.where(i_blk > 0, xp_ref[...], 0.0)
        row8 = _row_iota((8, D))
        xa = cb_ref[...] + cw_ref[3:4, :] * xv
        for s in (1, 2, 3):
            xr = pltpu.roll(xv, s, 0)
            tmp[...] = xr
            tmp[0:8, :] = jnp.where(row8 < s, pltpu.roll(xp, s, 0), xr[0:8, :])
            xa = xa + cw_ref[3 - s:4 - s, :] * tmp[...]
        xa_ref[...] = xa
        r, gi = _lru_gates(xa, wa_ref, ba_ref[...], wx_ref, bx_ref[...], pre_r, pre_i)
        la = (-LRU_C * _softplus(-lam_ref[...])) * r
        a = jnp.exp(la)
        b = jnp.sqrt(-_expm1(2.0 * la)) * (gi * xa)
        row = _row_iota((tb, D))
        sh = 1
        while sh < tb:
            m = row >= sh
            b = jnp.where(m, a * pltpu.roll(b, sh, 0) + b, b)
            a = jnp.where(m, a * pltpu.roll(a, sh, 0), a)
            sh *= 2
        h = a * hc[...] + b
        h_ref[...] = h
        hc[...] = h[tb - 1:tb, :]

    prev8 = pl.BlockSpec((8, D), lambda i: (jnp.maximum(i * (tb // 8) - 1, 0), 0))
    return _pcall(
        body, "lru_fwd", (nb,),
        [_rows(tb, D), prev8, _full(4, D), _full(1, D), _full(RNN_BLOCKS, RNN_BW, RNN_BW), _full(1, D),
         _full(RNN_BLOCKS, RNN_BW, RNN_BW), _full(1, D), _full(1, D)],
        [_rows(tb, D), _rows(tb, D)],
        [_sds((t, D)), _sds((t, D))],
        scratch=[pltpu.VMEM((1, D), F32), pltpu.VMEM((tb, D), F32), pltpu.VMEM((tb, D), F32), pltpu.VMEM((tb, D), F32)],
    )(rx, rx, conv_w, conv_b, wa, ba, wx, bx, lam)


def _mla_proj(cq, ckv, kr, qn, kvn, wq, wkv, rope_c, tm):
    t = cq.shape[0]

    def body(cq_ref, ckv_ref, kr_ref, qn_ref, kvn_ref, wq_ref, wkv_ref, c_ref, q_ref, k_ref, v_ref, cqn_ref, ckvn_ref):
        cqv = cq_ref[...]
        cqn = (cqv * _rms_scale(cqv) * qn_ref[...]).astype(BF16)
        ckvv = ckv_ref[...]
        ckvn = (ckvv * _rms_scale(ckvv) * kvn_ref[...]).astype(BF16)
        cqn_ref[...] = cqn
        ckvn_ref[...] = ckvn
        c = c_ref[...]
        lane = _lane_iota((tm, KR_W))
        kro = jnp.where(lane < 64, _rope_pair(kr_ref[...] * c), 0.0).astype(BF16)
        for h in range(N_HEADS):
            sl = slice(h * HEAD_W, (h + 1) * HEAD_W)
            qh = _dot(cqn, wq_ref[:, sl])
            q_ref[h, :, 0:128] = qh[:, 0:128].astype(BF16)
            q_ref[h, :, 128:256] = _rope_pair(qh[:, 128:256] * c).astype(BF16)
            kvh = _dot(ckvn, wkv_ref[:, sl])
            k_ref[h, :, 0:128] = kvh[:, 0:128].astype(BF16)
            k_ref[h, :, 128:256] = kro
            v_ref[h] = kvh[:, 128:256].astype(BF16)

    hb = lambda w: pl.BlockSpec((N_HEADS, tm, w), lambda i: (0, i, 0))
    return _pcall(
        body, "mla_proj", (t // tm,),
        [_rows(tm, Q_LORA), _rows(tm, KV_LORA), _rows(tm, KR_W), _full(1, Q_LORA), _full(1, KV_LORA),
         _full(Q_LORA, N_HEADS * HEAD_W), _full(KV_LORA, N_HEADS * HEAD_W), _rows(tm, KR_W)],
        [hb(HEAD_W), hb(HEAD_W), hb(V_HEAD), _rows(tm, Q_LORA), _rows(tm, KV_LORA)],
        [_sds((N_HEADS, t, HEAD_W), BF16), _sds((N_HEADS, t, HEAD_W), BF16), _sds((N_HEADS, t, V_HEAD), BF16),
         _sds((t, Q_LORA), BF16), _sds((t, KV_LORA), BF16)],
    )(cq, ckv, kr, qn, kvn, wq, wkv, rope_c)


EXP2_SCALE = SM_SCALE * math.log2(math.e)


def _flash_fwd(q, k, v, tq):
    t = q.shape[1]
    nq = t // tq

    def body(q_ref, k_ref, v_ref, o_ref, lse_ref):
        qi = pl.program_id(1)
        qv = q_ref[0]

        def block(ki, carry, diagonal):
            m, l, acc = carry
            rows = pl.ds(pl.multiple_of(ki * tq, tq), tq)
            s = _dot_nt(qv, k_ref[0, rows, :])
            if diagonal:
                s = jnp.where(_row_iota((tq, tq)) >= _lane_iota((tq, tq)), s, NEG)
            m_new = jnp.maximum(m, jnp.max(s, axis=1, keepdims=True))
            p = jnp.exp2((s - m_new) * EXP2_SCALE)
            alpha = jnp.exp2((m - m_new) * EXP2_SCALE)
            l = alpha * l + jnp.sum(p, axis=1, keepdims=True)
            acc = alpha * acc + _dot(p.astype(BF16), v_ref[0, rows, :])
            return m_new, l, acc

        init = (jnp.full((tq, 1), -jnp.inf, F32), jnp.zeros((tq, 1), F32), jnp.zeros((tq, V_HEAD), F32))
        carry = lax.fori_loop(0, qi, lambda ki, c: block(ki, c, False), init)
        m, l, acc = block(qi, carry, True)
        o_ref[...] = acc / l
        lse_ref[...] = jnp.broadcast_to(m * EXP2_SCALE + jnp.log(l) * math.log2(math.e), (tq, V_HEAD))

    head = lambda w: pl.BlockSpec((1, t, w), lambda h, qi: (h, 0, 0))
    o_spec = pl.BlockSpec((tq, V_HEAD), lambda h, qi: (qi, h))
    return _pcall(
        body, "flash_fwd", (N_HEADS, nq),
        [pl.BlockSpec((1, tq, HEAD_W), lambda h, qi: (h, qi, 0)), head(HEAD_W), head(V_HEAD)],
        [o_spec, o_spec],
        [_sds((t, D)), _sds((t, D))],
    )(q, k, v)


def _merge_out(x, h, g3, yb, w_out, tm):
    t = x.shape[0]

    def body(x_ref, h_ref, g3_ref, yb_ref, w_ref, h1_ref, mg_ref):
        ya = h_ref[...] * _gelu(g3_ref[:, 0:D])
        merged = (_sigmoid(g3_ref[:, D:2 * D]) * ya + _sigmoid(g3_ref[:, 2 * D:3 * D]) * yb_ref[...]).astype(BF16)
        mg_ref[...] = merged
        h1_ref[...] = x_ref[...] + _dot(merged, w_ref[...])

    return _pcall(
        body, "merge_out", (t // tm,),
        [_rows(tm, D), _rows(tm, D), _rows(tm, 3 * D), _rows(tm, D), _full(D, D)],
        [_rows(tm, D), _rows(tm, D)],
        [_sds((t, D)), _sds((t, D), BF16)],
    )(x, h, g3, yb, w_out)


def _mlp_up(h1, g, w_up, tm):
    t = h1.shape[0]

    def body(h_ref, g_ref, w_ref, u_ref, n2_ref):
        hv = h_ref[...]
        n2 = (hv * _rms_scale(hv) * g_ref[...]).astype(BF16)
        n2_ref[...] = n2
        for c0 in range(0, D_FF, 512):
            u_ref[:, c0:c0 + 512] = _dot(n2, w_ref[:, c0:c0 + 512])

    return _pcall(
        body, "mlp_up", (t // tm,),
        [_rows(tm, D), _full(1, D), _full(D, D_FF)],
        [_rows(tm, D_FF), _rows(tm, D)],
        [_sds((t, D_FF)), _sds((t, D), BF16)],
    )(h1, g, w_up)


def _mlp_down_loss(u, h1, target, w_down, g, tm):
    t = u.shape[0]

    def body(u_ref, h1_ref, tg_ref, w_ref, g_ref, act_ref, dh2_ref, loss_ref, gnf_ref, lacc):
        i = pl.program_id(0)

        @pl.when(i == 0)
        def _():
            lacc[...] = jnp.zeros_like(lacc)
            gnf_ref[...] = jnp.zeros_like(gnf_ref)

        ru = jnp.maximum(u_ref[...], 0.0)
        act = (ru * ru).astype(BF16)
        act_ref[...] = act
        h2 = h1_ref[...] + _dot(act, w_ref[...])
        rs = _rms_scale(h2)
        gv = g_ref[...]
        err = h2 * rs * gv - tg_ref[...]
        lacc[...] += jnp.sum(err * err, axis=0, keepdims=True)
        dx, dgr = _rms_bwd(h2, rs, gv, err * (1.0 / D))
        dh2_ref[...] = dx
        gnf_ref[...] += jnp.sum(dgr, axis=0, keepdims=True)

        @pl.when(i == pl.num_programs(0) - 1)
        def _():
            loss_ref[...] = jnp.broadcast_to(jnp.sum(lacc[...], axis=1, keepdims=True) * (0.5 / D), (8, 128))

    return _pcall(
        body, "mlp_down_loss", (t // tm,),
        [_rows(tm, D_FF), _rows(tm, D), _rows(tm, D), _full(D_FF, D), _full(1, D)],
        [_rows(tm, D_FF), _rows(tm, D), _full(8, 128), _full(1, D)],
        [_sds((t, D_FF), BF16), _sds((t, D)), _sds((8, 128)), _sds((1, D))],
        scratch=[pltpu.VMEM((1, D), F32)],
    )(u, h1, target, w_down, g)


def _matmul_tn(a, g, name):
    t, kdim = a.shape
    ndim = g.shape[1]
    tk = min(kdim, 1024)
    tn = ndim if ndim <= 1024 else 1024
    tt = min(t, 512)
    nt = t // tt

    def body(a_ref, g_ref, o_ref):
        @pl.when(pl.program_id(2) == 0)
        def _():
            o_ref[...] = jnp.zeros_like(o_ref)

        o_ref[...] += _dot_tn(a_ref[...].astype(BF16), g_ref[...].astype(BF16))

    return _pcall(
        body, name, (kdim // tk, ndim // tn, nt),
        [pl.BlockSpec((tt, tk), lambda i, j, s: (s, i)), pl.BlockSpec((tt, tn), lambda i, j, s: (s, j))],
        pl.BlockSpec((tk, tn), lambda i, j, s: (i, j)),
        _sds((kdim, ndim)),
    )(a, g)


def _mlp_bwd_act(dh2, u, w_down, tm):
    t = u.shape[0]

    def body(d_ref, u_ref, w_ref, du_ref):
        db = d_ref[...].astype(BF16)
        for c0 in range(0, D_FF, 512):
            da = _dot_nt(db, w_ref[c0:c0 + 512, :])
            du_ref[:, c0:c0 + 512] = (da * (2.0 * jnp.maximum(u_ref[:, c0:c0 + 512], 0.0))).astype(BF16)

    return _pcall(
        body, "mlp_bwd_act", (t // tm,),
        [_rows(tm, D), _rows(tm, D_FF), _full(D_FF, D)],
        _rows(tm, D_FF), _sds((t, D_FF), BF16),
    )(dh2, u, w_down)


def _mlp_bwd_in(du, dh2, h1, w_up, g, tm):
    t = du.shape[0]

    def body(du_ref, d_ref, h_ref, w_ref, g_ref, dh1_ref, gacc_ref):
        @pl.when(pl.program_id(0) == 0)
        def _():
            gacc_ref[...] = jnp.zeros_like(gacc_ref)

        dn2 = _dot_nt(du_ref[...], w_ref[...])
        hv = h_ref[...]
        dx, dgr = _rms_bwd(hv, _rms_scale(hv), g_ref[...], dn2)
        dh1_ref[...] = d_ref[...] + dx
        gacc_ref[...] += jnp.sum(dgr, axis=0, keepdims=True)

    return _pcall(
        body, "mlp_bwd_in", (t // tm,),
        [_rows(tm, D_FF), _rows(tm, D), _rows(tm, D), _full(D, D_FF), _full(1, D)],
        [_rows(tm, D), _full(1, D)],
        [_sds((t, D)), _sds((1, D))],
    )(du, dh2, h1, w_up, g)


def _merge_bwd(dh1, w_out, g3, h, yb, merged, lse, tm):
    t = dh1.shape[0]

    def body(d_ref, w_ref, g3_ref, h_ref, yb_ref, mg_ref, lse_ref, dg3_ref, dyb_ref, st_ref, dh_ref, dwo_ref):
        @pl.when(pl.program_id(0) == 0)
        def _():
            dwo_ref[...] = jnp.zeros_like(dwo_ref)

        db = d_ref[...].astype(BF16)
        dwo_ref[...] += _dot_tn(mg_ref[...], db)
        dm = _dot_nt(db, w_ref[...])
        gv = g3_ref[:, 0:D]
        sa = _sigmoid(g3_ref[:, D:2 * D])
        sb = _sigmoid(g3_ref[:, 2 * D:3 * D])
        gel = _gelu(gv)
        hv = h_ref[...]
        ybv = yb_ref[...]
        dya = dm * sa
        dyb = dm * sb
        dg3_ref[:, 0:D] = (dya * hv * _gelu_grad(gv)).astype(BF16)
        dg3_ref[:, D:2 * D] = (dya * (hv * gel) * (1.0 - sa)).astype(BF16)
        dg3_ref[:, 2 * D:3 * D] = (dyb * ybv * (1.0 - sb)).astype(BF16)
        dh_ref[...] = dya * gel
        dyb_ref[...] = dyb.astype(BF16)
        prod = dyb * ybv
        lane = _lane_iota((tm, V_HEAD))
        for hh in range(N_HEADS):
            sl = slice(hh * V_HEAD, (hh + 1) * V_HEAD)
            st_ref[:, sl] = jnp.where(lane < 64, lse_ref[:, sl], jnp.sum(prod[:, sl], axis=1, keepdims=True))

    return _pcall(
        body, "merge_bwd", (t // tm,),
        [_rows(tm, D), _full(D, D), _rows(tm, 3 * D), _rows(tm, D), _rows(tm, D), _rows(tm, D), _rows(tm, D)],
        [_rows(tm, 3 * D), _rows(tm, D), _rows(tm, D), _rows(tm, D), _full(D, D)],
        [_sds((t, 3 * D), BF16), _sds((t, D), BF16), _sds((t, D)), _sds((t, D)), _sds((D, D))],
    )(dh1, w_out, g3, h, yb, merged, lse)


def _flash_bwd(q, k, v, do, stats, tq):
    t = q.shape[1]
    nq = t // tq

    def body(q_ref, k_ref, v_ref, do_ref, st_ref, dq_ref, dk_ref, dv_ref):
        ki = pl.program_id(1)

        @pl.when(ki == 0)
        def _():
            dq_ref[...] = jnp.zeros_like(dq_ref)

        kv_, vv = k_ref[0], v_ref[0]
        dk_ref[...] = jnp.zeros_like(dk_ref)
        dv_ref[...] = jnp.zeros_like(dv_ref)

        def block(qi, diagonal):
            rows = pl.ds(pl.multiple_of(qi * tq, tq), tq)
            qv, dov = q_ref[0, rows, :], do_ref[rows, :]
            p = jnp.exp2(_dot_nt(qv, kv_) * EXP2_SCALE - st_ref[rows, 0:1])
            if diagonal:
                p = jnp.where(_row_iota((tq, tq)) >= _lane_iota((tq, tq)), p, 0.0)
            dv_ref[0] += _dot_tn(p.astype(BF16), dov)
            dp = _dot_nt(dov, vv)
            ds = (p * (dp - st_ref[rows, 64:65]) * SM_SCALE).astype(BF16)
            dk_ref[0] += _dot_tn(ds, qv)
            dq_ref[0, rows, :] += _dot(ds, kv_)

        block(ki, True)

        def rest(qi, carry):
            block(qi, False)
            return carry

        lax.fori_loop(ki + 1, nq, rest, 0)

    kv_spec = lambda w: pl.BlockSpec((1, tq, w), lambda h, ki: (h, ki, 0))
    col = pl.BlockSpec((t, V_HEAD), lambda h, ki: (0, h))
    whole = pl.BlockSpec((1, t, HEAD_W), lambda h, ki: (h, 0, 0))
    return _pcall(
        body, "flash_bwd", (N_HEADS, nq),
        [whole, kv_spec(HEAD_W), kv_spec(V_HEAD), col, col],
        [whole, kv_spec(HEAD_W), kv_spec(V_HEAD)],
        [_sds((N_HEADS, t, HEAD_W)), _sds((N_HEADS, t, HEAD_W)), _sds((N_HEADS, t, V_HEAD))],
    )(q, k, v, do, stats)


def _mla_bwd(dq, dk, dv, cqn, ckvn, cq, ckv, rope_c, wq, wkv, qn, kvn, tm):
    t = cq.shape[0]

    def body(dq_ref, dk_ref, dv_ref, cqn_ref, ckvn_ref, cq_ref, ckv_ref, c_ref, wq_ref, wkv_ref, qn_ref, kvn_ref,
             dmla_ref, dwq_ref, dwkv_ref, dqn_ref, dkvn_ref):
        @pl.when(pl.program_id(0) == 0)
        def _():
            dwq_ref[...] = jnp.zeros_like(dwq_ref)
            dwkv_ref[...] = jnp.zeros_like(dwkv_ref)
            dqn_ref[...] = jnp.zeros_like(dqn_ref)
            dkvn_ref[...] = jnp.zeros_like(dkvn_ref)

        c = c_ref[...]
        lane = _lane_iota((tm, KR_W))
        cqn, ckvn = cqn_ref[...], ckvn_ref[...]
        dcqn = jnp.zeros((tm, Q_LORA), F32)
        dckvn = jnp.zeros((tm, KV_LORA), F32)
        dkr = jnp.zeros((tm, KR_W), F32)
        for h in range(N_HEADS):
            sl = slice(h * HEAD_W, (h + 1) * HEAD_W)
            droped = jnp.where(lane < 64, dq_ref[h, :, 128:256], 0.0)
            dqp = jnp.concatenate([dq_ref[h, :, 0:128], _rope_pair(droped) * c], axis=1).astype(BF16)
            dcqn = dcqn + _dot_nt(dqp, wq_ref[:, sl])
            dwq_ref[:, sl] += _dot_tn(cqn, dqp)
            dkr = dkr + jnp.where(lane < 64, dk_ref[h, :, 128:256], 0.0)
            dkvp = jnp.concatenate([dk_ref[h, :, 0:128], dv_ref[h]], axis=1).astype(BF16)
            dckvn = dckvn + _dot_nt(dkvp, wkv_ref[:, sl])
            dwkv_ref[:, sl] += _dot_tn(ckvn, dkvp)
        cqv, ckvv = cq_ref[...], ckv_ref[...]
        dcq, dgq = _rms_bwd(cqv, _rms_scale(cqv), qn_ref[...], dcqn)
        dckv, dgkv = _rms_bwd(ckvv, _rms_scale(ckvv), kvn_ref[...], dckvn)
        dqn_ref[...] += jnp.sum(dgq, axis=0, keepdims=True)
        dkvn_ref[...] += jnp.sum(dgkv, axis=0, keepdims=True)
        dmla_ref[:, 0:256] = dcq.astype(BF16)
        dmla_ref[:, 256:512] = dckv.astype(BF16)
        dmla_ref[:, 512:640] = (_rope_pair(dkr) * c).astype(BF16)

    hb = lambda w: pl.BlockSpec((N_HEADS, tm, w), lambda i: (0, i, 0))
    wide = N_HEADS * HEAD_W
    return _pcall(
        body, "mla_bwd", (t // tm,),
        [hb(HEAD_W), hb(HEAD_W), hb(V_HEAD), _rows(tm, Q_LORA), _rows(tm, KV_LORA), _rows(tm, Q_LORA), _rows(tm, KV_LORA),
         _rows(tm, KR_W), _full(Q_LORA, wide), _full(KV_LORA, wide), _full(1, Q_LORA), _full(1, KV_LORA)],
        [_rows(tm, 640), _full(Q_LORA, wide), _full(KV_LORA, wide), _full(1, Q_LORA), _full(1, KV_LORA)],
        [_sds((t, 640), BF16), _sds((Q_LORA, wide)), _sds((KV_LORA, wide)), _sds((1, Q_LORA)), _sds((1, KV_LORA))],
    )(dq, dk, dv, cqn, ckvn, cq, ckv, rope_c, wq, wkv, qn, kvn)


def _lru_bwd(dh, xa, h, rx, conv_w, wa, ba, wx, bx, lam, tb):
    t = dh.shape[0]
    nb = t // tb

    def body(dh_ref, xa_ref, h_ref, hp_ref, x_ref, cw_ref, wa_ref, ba_ref, wx_ref, bx_ref, lam_ref,
             drx_ref, dcw_ref, dcb_ref, dwa_ref, dba_ref, dwx_ref, dbx_ref, dlam_ref, gc, dxn, tmp, pre_r, pre_i):
        step = pl.program_id(0)
        first_block = step == nb - 1

        @pl.when(step == 0)
        def _():
            gc[...] = jnp.zeros_like(gc)
            dxn[...] = jnp.zeros_like(dxn)
            for ref in (dcw_ref, dcb_ref, dwa_ref, dba_ref, dwx_ref, dbx_ref, dlam_ref):
                ref[...] = jnp.zeros_like(ref)

        xa = xa_ref[...]
        r, gi = _lru_gates(xa, wa_ref, ba_ref[...], wx_ref, bx_ref[...], pre_r, pre_i)
        lamv = lam_ref[...]
        sp = _softplus(-lamv)
        la = (-LRU_C * sp) * r
        a = jnp.exp(la)
        e2 = _expm1(2.0 * la)
        sq = jnp.sqrt(-e2)
        row = _row_iota((tb, D))
        cf = jnp.where(row == tb - 1, 1.0, pltpu.roll(a, tb - 1, 0))
        bv = dh_ref[...]
        sh = 1
        while sh < tb:
            m = row < tb - sh
            bv = jnp.where(m, bv + cf * pltpu.roll(bv, tb - sh, 0), bv)
            cf = jnp.where(m, cf * pltpu.roll(cf, tb - sh, 0), cf)
            sh *= 2
        delta = bv + cf * gc[...]
        gc[...] = a[0:1, :] * delta[0:1, :]
        hv = h_ref[...]
        hr = pltpu.roll(hv, 1, 0)
        tmp[...] = hr
        tmp[0:1, :] = jnp.where(first_block, 0.0, hp_ref[7:8, :])
        hprev = tmp[...]
        ix = gi * xa
        dla = (delta * hprev) * a - (delta * ix) * ((e2 + 1.0) / sq)
        dlam_ref[...] += jnp.sum(dla * r, axis=0, keepdims=True) * (LRU_C * _sigmoid(-lamv))
        dpr = (dla * (-LRU_C * sp)) * r * (1.0 - r)
        dsq = delta * sq
        dpi = (dsq * xa) * gi * (1.0 - gi)
        dba_ref[...] += jnp.sum(dpr, axis=0, keepdims=True)
        dbx_ref[...] += jnp.sum(dpi, axis=0, keepdims=True)
        pre_r[...] = dpr
        pre_i[...] = dpi
        xb = xa.astype(BF16)
        for n in range(RNN_BLOCKS):
            sl = slice(n * RNN_BW, (n + 1) * RNN_BW)
            dprn = pre_r[:, sl].astype(BF16)
            dpin = pre_i[:, sl].astype(BF16)
            dwa_ref[n] += _dot_tn(xb[:, sl], dprn)
            dwx_ref[n] += _dot_tn(xb[:, sl], dpin)
            tmp[:, sl] = _dot_nt(dprn, wa_ref[n]) + _dot_nt(dpin, wx_ref[n])
        dxa = dsq * gi + tmp[...]
        dcb_ref[...] += jnp.sum(dxa, axis=0, keepdims=True)
        xv = x_ref[...]
        drx = cw_ref[3:4, :] * dxa
        dcw_ref[3:4, :] += jnp.sum(dxa * xv, axis=0, keepdims=True)
        row8 = _row_iota((8, D))
        nxt = dxn[...]
        for s in (1, 2, 3):
            dr_ = pltpu.roll(dxa, tb - s, 0)
            tmp[...] = dr_
            tmp[tb - 8:tb, :] = jnp.where(row8 >= 8 - s, pltpu.roll(nxt, 8 - s, 0), dr_[tb - 8:tb, :])
            dxs = tmp[...]
            drx = drx + cw_ref[3 - s:4 - s, :] * dxs
            dcw_ref[3 - s:4 - s, :] += jnp.sum(dxs * xv, axis=0, keepdims=True)
        drx_ref[...] = drx.astype(BF16)
        dxn[...] = dxa[0:8, :]

    rev = pl.BlockSpec((tb, D), lambda i: (nb - 1 - i, 0))
    prev8 = pl.BlockSpec((8, D), lambda i: (jnp.maximum((nb - 1 - i) * (tb // 8) - 1, 0), 0))
    wblk = _full(RNN_BLOCKS, RNN_BW, RNN_BW)
    return _pcall(
        body, "lru_bwd", (nb,),
        [rev, rev, rev, prev8, rev, _full(4, D), wblk, _full(1, D), wblk, _full(1, D), _full(1, D)],
        [rev, _full(4, D), _full(1, D), wblk, _full(1, D), wblk, _full(1, D), _full(1, D)],
        [_sds((t, D), BF16), _sds((4, D)), _sds((1, D)), _sds((RNN_BLOCKS, RNN_BW, RNN_BW)), _sds((1, D)),
         _sds((RNN_BLOCKS, RNN_BW, RNN_BW)), _sds((1, D)), _sds((1, D))],
        scratch=[pltpu.VMEM((1, D), F32), pltpu.VMEM((8, D), F32), pltpu.VMEM((tb, D), F32), pltpu.VMEM((tb, D), F32),
                 pltpu.VMEM((tb, D), F32)],
    )(dh, xa, h, h, rx, conv_w, wa, ba, wx, bx, lam)


def _inproj_bwd(x, dh1, drx, dg3, dmla, w1, g, tm):
    t = x.shape[0]

    def body(x_ref, d_ref, drx_ref, dg3_ref, dmla_ref, w_ref, g_ref, dx_ref, gacc_ref):
        @pl.when(pl.program_id(0) == 0)
        def _():
            gacc_ref[...] = jnp.zeros_like(gacc_ref)

        dxn = _dot_nt(drx_ref[...], w_ref[:, 0:D])
        for c0 in range(0, 3 * D, D):
            dxn = dxn + _dot_nt(dg3_ref[:, c0:c0 + D], w_ref[:, D + c0:2 * D + c0])
        dxn = dxn + _dot_nt(dmla_ref[...], w_ref[:, 4 * D:W1_COLS])
        xv = x_ref[...]
        dx, dgr = _rms_bwd(xv, _rms_scale(xv), g_ref[...], dxn)
        dx_ref[...] = d_ref[...] + dx
        gacc_ref[...] += jnp.sum(dgr, axis=0, keepdims=True)

    return _pcall(
        body, "inproj_bwd", (t // tm,),
        [_rows(tm, D), _rows(tm, D), _rows(tm, D), _rows(tm, 3 * D), _rows(tm, 640), _full(D, W1_COLS), _full(1, D)],
        [_rows(tm, D), _full(1, D)],
        [_sds((t, D)), _sds((1, D))],
    )(x, dh1, drx, dg3, dmla, w1, g)


def _add2(a, b, out_dtype, name):
    rows, cols = a.shape
    tm = _row_tile(rows)

    def body(a_ref, b_ref, o_ref):
        o_ref[...] = (a_ref[...] + b_ref[...]).astype(out_dtype)

    return _pcall(body, name, (rows // tm,), [_rows(tm, cols), _rows(tm, cols)], _rows(tm, cols), _sds((rows, cols), out_dtype))(a, b)


def _sum4(a, name):
    _, rows, cols = a.shape
    tm = _row_tile(rows)

    def body(a_ref, o_ref):
        o_ref[...] = ((a_ref[0].astype(F32) + a_ref[1].astype(F32)) + a_ref[2].astype(F32)) + a_ref[3].astype(F32)

    return _pcall(body, name, (rows // tm,), [pl.BlockSpec((N_CHIPS, tm, cols), lambda i: (0, i, 0))], _rows(tm, cols),
                  _sds((rows, cols)))(a)


def _adamw(w, g, m, v, name):
    rows, cols = w.shape
    tm = _row_tile(rows)

    def body(w_ref, g_ref, m_ref, v_ref, d_ref, mo_ref, vo_ref):
        gv = g_ref[...]
        mn = ADAM_B1 * m_ref[...] + (1.0 - ADAM_B1) * gv
        vn = ADAM_B2 * v_ref[...] + (1.0 - ADAM_B2) * (gv * gv)
        m_hat = mn / (1.0 - ADAM_B1 ** ADAM_STEP)
        v_hat = vn / (1.0 - ADAM_B2 ** ADAM_STEP)
        d_ref[...] = -ADAM_LR * (m_hat / (jnp.sqrt(v_hat) + ADAM_EPS) + ADAM_WD * w_ref[...])
        mo_ref[...] = mn
        vo_ref[...] = vn

    spec = _rows(tm, cols)
    return _pcall(body, name, (rows // tm,), [spec] * 4, [spec] * 3, [_sds((rows, cols))] * 3)(w, g, m, v)


REL_SIBLING = (0, 0, 1)
REL_CHIPS = ((1, 0, 0), (0, 1, 0), (1, 1, 0))


V7X_DMA_CHUNK_BYTES = 1 << 20


def _split_copy(src, dst, shape, itemsize):
    nbytes = math.prod(shape) * itemsize
    if nbytes <= V7X_DMA_CHUNK_BYTES or len(shape) < 2:
        return [(src, dst)]
    if len(shape) > 2:
        out = []
        for k in range(shape[0]):
            out += _split_copy(src.at[k], dst.at[k], shape[1:], itemsize)
        return out
    rows = shape[0]
    sub = 8 * (4 // itemsize)
    parts = max(1, min(-(-nbytes // V7X_DMA_CHUNK_BYTES), rows // sub))
    while rows % parts or (rows // parts) % sub:
        parts -= 1
    step = rows // parts
    return [(src.at[pl.ds(k * step, step)], dst.at[pl.ds(k * step, step)]) for k in range(parts)]


def _comm(name, ins, out_shapes, n_ops, ops_fn):
    n_in, n_out = len(ins), len(out_shapes)

    def body(*refs):
        in_refs, out_refs = refs[:n_in], refs[n_in:n_in + n_out]
        send_sems, recv_sems = refs[n_in + n_out:]
        pos = (lax.axis_index("x"), lax.axis_index("y"), lax.axis_index("c"))
        ops = ops_fn(in_refs, out_refs, pos)
        assert len(ops) == n_ops

        def make(i, rel, src, dst):
            if rel is None:
                return pltpu.make_async_copy(src, dst, send_sems.at[i])
            peer = tuple((p + r) % 2 for p, r in zip(pos, rel))
            return pltpu.make_async_remote_copy(src_ref=src, dst_ref=dst, send_sem=send_sems.at[i], recv_sem=recv_sems.at[i],
                                                device_id=peer, device_id_type=MESH_ID)

        for i, (rel, src, dst) in enumerate(ops):
            for s_piece, d_piece in _split_copy(src, dst, src.shape, jnp.dtype(src.dtype).itemsize):
                make(i, rel, s_piece, d_piece).start()
        for i, (rel, src, dst) in enumerate(ops):
            make(i, rel, src, dst).wait()

    hbm = pl.BlockSpec(memory_space=pl.ANY)
    return pl.pallas_call(
        body, name=name, in_specs=[hbm] * n_in, out_specs=[hbm] * n_out, out_shape=list(out_shapes),
        scratch_shapes=[pltpu.SemaphoreType.DMA((n_ops,)), pltpu.SemaphoreType.DMA((n_ops,))],
    )(*ins)


def _chip_of(pos, rel=(0, 0, 0)):
    return 2 * ((pos[0] + rel[0]) % 2) + (pos[1] + rel[1]) % 2


def _gather_chips(shards, name):
    def ops_fn(in_refs, out_refs, pos):
        me = _chip_of(pos)
        ops = []
        for src, dst in zip(in_refs, out_refs):
            ops.append((None, src, dst.at[me]))
            ops += [(rel, src, dst.at[me]) for rel in REL_CHIPS]
        return ops

    return _comm(name, shards, [_sds((N_CHIPS,) + s.shape, s.dtype) for s in shards], 4 * len(shards), ops_fn)


def _sibling_split(gs):
    def ops_fn(in_refs, out_refs, pos):
        c = pos[2]
        ops = []
        for a, src in enumerate(in_refs):
            ops.append((None, src.at[c], out_refs[2 * a]))
            ops.append((REL_SIBLING, src.at[1 - c], out_refs[2 * a + 1]))
        return ops

    shapes = []
    for g in gs:
        shapes += [_sds(g.shape[1:], g.dtype)] * 2
    outs = _comm("grad_sibling_split", gs, shapes, 2 * len(gs), ops_fn)
    return outs[0::2], outs[1::2]


def _chip_exchange(ps):
    def ops_fn(in_refs, out_refs, pos):
        me = _chip_of(pos)
        ops = []
        for src, dst in zip(in_refs, out_refs):
            ops.append((None, src.at[me], dst.at[me]))
            ops += [(rel, src.at[_chip_of(pos, rel)], dst.at[me]) for rel in REL_CHIPS]
        return ops

    return _comm("grad_chip_exchange", ps, [_sds(p.shape, p.dtype) for p in ps], 4 * len(ps), ops_fn)


def _sibling_join(hs):
    def ops_fn(in_refs, out_refs, pos):
        c = pos[2]
        ops = []
        for src, dst in zip(in_refs, out_refs):
            ops.append((None, src, dst.at[c]))
            ops.append((REL_SIBLING, src, dst.at[c]))
        return ops

    return _comm("grad_sibling_join", hs, [_sds((2,) + h.shape, h.dtype) for h in hs], 2 * len(hs), ops_fn)


def _rot_cols(w):
    return jnp.concatenate([-w[..., 32:], w[..., :32]], axis=-1)


def _unrot_cols(dw):
    return jnp.concatenate([dw[..., 32:], -dw[..., :32]], axis=-1)


IN_OFFS = (0, 1024, 2048, 2304, 2560, 2624, 3648, 4672)


def _w1_from_w_in(w):
    seg = [w[:, IN_OFFS[i]:IN_OFFS[i + 1]] for i in range(7)]
    rnn_x, rnn_gate, cq, ckv, kr, ga, gb = seg
    return jnp.concatenate([rnn_x, rnn_gate, ga, gb, cq, ckv, kr, _rot_cols(kr)], axis=1)


def _w_in_grad_from_parts(d_rx, d_g3, d_mla):
    kr = d_mla[:, 512:576] + _unrot_cols(d_mla[:, 576:640])
    return jnp.concatenate([d_rx, d_g3[:, 0:D], d_mla[:, 0:512], kr, d_g3[:, D:3 * D]], axis=1)


def _wq_from_w_uq(w):
    w3 = w.reshape(Q_LORA, N_HEADS, QK_NOPE + QK_ROPE)
    rope = w3[..., QK_NOPE:]
    return jnp.concatenate([w3[..., :QK_NOPE], rope, _rot_cols(rope)], axis=-1).reshape(Q_LORA, N_HEADS * HEAD_W)


def _w_uq_grad_from_wq(dw):
    d3 = dw.reshape(Q_LORA, N_HEADS, HEAD_W)
    rope = d3[..., 128:192] + _unrot_cols(d3[..., 192:256])
    return jnp.concatenate([d3[..., :128], rope], axis=-1).reshape(Q_LORA, N_HEADS * (QK_NOPE + QK_ROPE))


def _cols_from_chunks(g):
    return g.transpose(1, 0, 2).reshape(g.shape[1], N_CHIPS * g.shape[2])


def _halves_of_col_chunks(dw):
    r, c4 = dw.shape
    return dw.reshape(2, r // 2, N_CHIPS, c4 // N_CHIPS).transpose(0, 2, 1, 3)


def _halves_of_row_chunks(dw):
    r4, c = dw.shape
    return dw.reshape(N_CHIPS, 2, r4 // (2 * N_CHIPS), c).transpose(1, 0, 2, 3)


def kernel(x, norm_mix, w_in, conv_w, conv_b, lru_wa, lru_ba, lru_wx, lru_bx, lru_lambda, q_norm, w_uq, kv_norm, w_ukv, w_out, norm_mlp, w_up, w_down, norm_final, loss_target, m_norm_mix, m_w_in, m_conv_w, m_conv_b, m_lru_wa, m_lru_ba, m_lru_wx, m_lru_bx, m_lru_lambda, m_q_norm, m_w_uq, m_kv_norm, m_w_ukv, m_w_out, m_norm_mlp, m_w_up, m_w_down, m_norm_final, v_norm_mix, v_w_in, v_conv_w, v_conv_b, v_lru_wa, v_lru_ba, v_lru_wx, v_lru_bx, v_lru_lambda, v_q_norm, v_w_uq, v_kv_norm, v_w_ukv, v_w_out, v_norm_mlp, v_w_up, v_w_down, v_norm_final):
    t = x.shape[1]
    tm = min(256, t)
    tq = min(512, max(128, t // 4))
    x2 = x[0]
    target = loss_target[0]
    chip = 2 * lax.axis_index("x") + lax.axis_index("y")
    row = lambda p: p.reshape(1, -1)

    big_shards = (w_in, w_uq, w_ukv, w_out, w_up, w_down)
    gathered = _gather_chips([w.astype(BF16) for w in big_shards] + [conv_w], "weight_gather")
    w1 = _w1_from_w_in(_cols_from_chunks(gathered[0]))
    wq = _wq_from_w_uq(_cols_from_chunks(gathered[1]))
    wkv = _cols_from_chunks(gathered[2])
    w_out_f = gathered[3].reshape(D, D)
    w_up_f = _cols_from_chunks(gathered[4])
    w_down_f = gathered[5].reshape(D_FF, D)
    conv_w_f = _cols_from_chunks(gathered[6])
    wa_b, wx_b = lru_wa.astype(BF16), lru_wx.astype(BF16)

    pos = jnp.arange(t, dtype=F32)
    inv_freq = 1.0 / (ROPE_THETA ** (jnp.arange(0, QK_ROPE, 2, dtype=F32) / QK_ROPE))
    ang = pos[:, None] * inv_freq[None, :]
    rope_c = jnp.concatenate([jnp.cos(ang), jnp.cos(ang), jnp.sin(ang), jnp.sin(ang)], axis=-1)

    xn, rx, g3, cq, ckv, kr = _inproj(x2, row(norm_mix), w1, tm)
    h, xa = _lru_fwd(rx, conv_w_f, row(conv_b), wa_b, row(lru_ba), wx_b, row(lru_bx), row(lru_lambda), tm)
    q, k, v, cqn, ckvn = _mla_proj(cq, ckv, kr, row(q_norm), row(kv_norm), wq, wkv, rope_c, tm)
    yb, lse = _flash_fwd(q, k, v, tq)
    h1, merged = _merge_out(x2, h, g3, yb, w_out_f, tm)
    u, n2 = _mlp_up(h1, row(norm_mlp), w_up_f, tm)
    act, dh2, loss_blk, g_norm_final = _mlp_down_loss(u, h1, target, w_down_f, row(norm_final), tm)
    loss = lax.psum(loss_blk[0, 0], ("x", "y", "c"))

    g_w_down = _matmul_tn(act, dh2, "grad_w_down")
    du = _mlp_bwd_act(dh2, u, w_down_f, tm)
    dh1, g_norm_mlp = _mlp_bwd_in(du, dh2, h1, w_up_f, row(norm_mlp), tm)
    g_w_up = _matmul_tn(n2, du, "grad_w_up")
    dg3, dyb, stats, dh, g_w_out = _merge_bwd(dh1, w_out_f, g3, h, yb, merged, lse, tm)
    dq, dk, dv = _flash_bwd(q, k, v, dyb, stats, tq)
    dmla, g_wq, g_wkv, g_q_norm, g_kv_norm = _mla_bwd(dq, dk, dv, cqn, ckvn, cq, ckv, rope_c, wq, wkv, row(q_norm), row(kv_norm), tm)
    drx, g_conv_w, g_conv_b, g_wa, g_ba, g_wx, g_bx, g_lam = _lru_bwd(
        dh, xa, h, rx, conv_w_f, wa_b, row(lru_ba), wx_b, row(lru_bx), row(lru_lambda), tm)
    grad_x, g_norm_mix = _inproj_bwd(x2, dh1, drx, dg3, dmla, w1, row(norm_mix), tm)
    g_w_in = _w_in_grad_from_parts(_matmul_tn(xn, drx, "grad_w_in_rx"), _matmul_tn(xn, dg3, "grad_w_in_gates"),
                                   _matmul_tn(xn, dmla, "grad_w_in_mla"))
    g_w_uq = _w_uq_grad_from_wq(g_wq)

    smalls = (g_norm_mix, g_conv_b, g_wa, g_ba, g_wx, g_bx, g_lam, g_q_norm, g_kv_norm, g_norm_mlp, g_norm_final, g_conv_w)
    s_flat = jnp.concatenate([s.reshape(-1) for s in smalls] + [jnp.zeros((S_LEN - N_SMALL - CONVW_SIZE,), F32)])
    halves = [_halves_of_col_chunks(g_w_in), _halves_of_col_chunks(g_w_uq), _halves_of_col_chunks(g_wkv),
              _halves_of_row_chunks(g_w_out), _halves_of_col_chunks(g_w_up), _halves_of_row_chunks(g_w_down),
              s_flat.reshape(N_CHIPS, 2, S_ROWS_HALF, 128).transpose(1, 0, 2, 3)]
    mine, theirs = _sibling_split(halves)
    parts = []
    for a, (p, r) in enumerate(zip(mine, theirs)):
        flat = lambda z: z.reshape(-1, z.shape[-1])
        dt = F32 if a == len(halves) - 1 else BF16
        parts.append(_add2(flat(p), flat(r), dt, f"grad_pair_sum_{a}").reshape(p.shape[:-1] + (p.shape[-1],)))
    received = _chip_exchange(parts)
    reduced = [_sum4(r, f"grad_chip_sum_{a}") for a, r in enumerate(received)]
    joined = _sibling_join(reduced)
    g_big = [j.reshape(-1, j.shape[-1]) for j in joined[:-1]]
    s_all = _gather_chips([joined[-1].reshape(2 * S_ROWS_HALF, 128)], "small_grad_gather")[0].reshape(-1)

    small_grads = []
    off = 0
    for shp, n in zip(SMALL_SHAPES, SMALL_SIZES):
        small_grads.append(s_all[off:off + n].reshape(shp))
        off += n
    g_conv_w_mine = lax.dynamic_slice_in_dim(s_all[off:off + CONVW_SIZE].reshape(4, D), chip * (D // N_CHIPS), D // N_CHIPS, axis=1)

    big_m = (m_w_in, m_w_uq, m_w_ukv, m_w_out, m_w_up, m_w_down)
    big_v = (v_w_in, v_w_uq, v_w_ukv, v_w_out, v_w_up, v_w_down)
    big_names = ("w_in", "w_uq", "w_ukv", "w_out", "w_up", "w_down")
    big_upd = [_adamw(w, g, m, v, "adamw_" + n) for w, g, m, v, n in zip(big_shards, g_big, big_m, big_v, big_names)]

    small_w = (norm_mix, conv_b, lru_wa, lru_ba, lru_wx, lru_bx, lru_lambda, q_norm, kv_norm, norm_mlp, norm_final)
    small_m = (m_norm_mix, m_conv_b, m_lru_wa, m_lru_ba, m_lru_wx, m_lru_bx, m_lru_lambda, m_q_norm, m_kv_norm, m_norm_mlp, m_norm_final)
    small_v = (v_norm_mix, v_conv_b, v_lru_wa, v_lru_ba, v_lru_wx, v_lru_bx, v_lru_lambda, v_q_norm, v_kv_norm, v_norm_mlp, v_norm_final)

    def pack(items, last, fill):
        flat = jnp.concatenate([i.reshape(-1) for i in items] + [last.reshape(-1)])
        return jnp.concatenate([flat, jnp.full((PACK_ROWS * 128 - flat.shape[0],), fill, F32)]).reshape(PACK_ROWS, 128)

    packed = _adamw(pack(small_w, conv_w, 0.0), pack(small_grads, g_conv_w_mine, 0.0), pack(small_m, m_conv_w, 0.0),
                    pack(small_v, v_conv_w, 1.0), "adamw_small")

    def unpack(p):
        flat = p.reshape(-1)
        outs, o = [], 0
        for shp, n in zip(SMALL_SHAPES, SMALL_SIZES):
            outs.append(flat[o:o + n].reshape(shp))
            o += n
        return outs, flat[o:o + CONVW_SIZE // N_CHIPS].reshape(4, D // N_CHIPS)

    order = ("norm_mix", "w_in", "conv_w", "conv_b", "lru_wa", "lru_ba", "lru_wx", "lru_bx", "lru_lambda", "q_norm", "w_uq", "kv_norm",
             "w_ukv", "w_out", "norm_mlp", "w_up", "w_down", "norm_final")

    def assemble(small_list, conv_w_item, big_list):
        table = dict(zip(SMALL_NAMES, small_list))
        table["conv_w"] = conv_w_item
        table.update(zip(big_names, big_list))
        return [table[n] for n in order]

    outs = [loss, grad_x.reshape(1, t, D)]
    outs += assemble(small_grads, g_conv_w_mine, g_big)
    for j in range(3):
        sm, cw = unpack(packed[j])
        outs += assemble(sm, cw, [b[j] for b in big_upd])
    return tuple(outs)
```

```python
import functools
import math

import jax
import jax.numpy as jnp
from jax import lax
from jax.experimental import pallas as pl
from jax.experimental.pallas import tpu as pltpu

F32 = jnp.float32
BF16 = jnp.bfloat16

D = 1024
N_HEADS = 8
QK_NOPE = 128
QK_ROPE = 64
V_HEAD = 128
Q_LORA = 256
KV_LORA = 256
D_FF = 4096
RNN_BLOCKS = 8
RNN_BW = 128
LRU_C = 8.0
EPS = 1e-6
ROPE_THETA = 10000.0
HEAD_W = 256
KR_W = 128
W1_COLS = 4 * D + Q_LORA + KV_LORA + KR_W
SM_SCALE = (QK_NOPE + QK_ROPE) ** -0.5
NEG = float(jnp.finfo(jnp.float32).min)

ADAM_LR = 0.001
ADAM_B1 = 0.9
ADAM_B2 = 0.999
ADAM_EPS = 1e-08
ADAM_WD = 0.01
ADAM_STEP = 10

N_CHIPS = 4
V7X_VMEM_LIMIT = 56 * 1024 * 1024
MESH_ID = pl.DeviceIdType.MESH

SMALL_NAMES = ("norm_mix", "conv_b", "lru_wa", "lru_ba", "lru_wx", "lru_bx", "lru_lambda", "q_norm", "kv_norm", "norm_mlp", "norm_final")
SMALL_SHAPES = ((D,), (D,), (RNN_BLOCKS, RNN_BW, RNN_BW), (RNN_BLOCKS, RNN_BW), (RNN_BLOCKS, RNN_BW, RNN_BW), (RNN_BLOCKS, RNN_BW), (D,),
                (Q_LORA,), (KV_LORA,), (D,), (D,))
SMALL_SIZES = tuple(math.prod(s) for s in SMALL_SHAPES)
N_SMALL = sum(SMALL_SIZES)
CONVW_SIZE = 4 * D
S_LEN = -(-(N_SMALL + CONVW_SIZE) // 8192) * 8192
S_ROWS_HALF = S_LEN // (N_CHIPS * 2 * 128)
PACK_ROWS = -(-(N_SMALL + CONVW_SIZE // N_CHIPS) // (256 * 128)) * 256


def _pcall(body, name, grid, in_specs, out_specs, out_shape, scratch=()):
    return pl.pallas_call(
        body, name=name, grid=grid, in_specs=in_specs, out_specs=out_specs, out_shape=out_shape,
        scratch_shapes=list(scratch),
        compiler_params=pltpu.CompilerParams(dimension_semantics=("arbitrary",) * len(grid), vmem_limit_bytes=V7X_VMEM_LIMIT))


def _rows(tm, w):
    return pl.BlockSpec((tm, w), lambda i: (i, 0))


def _full(*shape):
    return pl.BlockSpec(shape, lambda *_: (0,) * len(shape))


def _sds(shape, dtype=F32):
    return jax.ShapeDtypeStruct(shape, dtype)


def _row_tile(rows, cap=256):
    t = min(rows, cap)
    while rows % t or t % 8:
        t -= 1
    return t


def _dot(a, b):
    return jnp.dot(a, b, preferred_element_type=F32)


def _dot_nt(a, b):
    return lax.dot_general(a, b, (((1,), (1,)), ((), ())), preferred_element_type=F32)


def _dot_tn(a, b):
    return lax.dot_general(a, b, (((0,), (0,)), ((), ())), preferred_element_type=F32)


def _sigmoid(x):
    return 1.0 / (1.0 + jnp.exp(-x))


_GELU_C = math.sqrt(2.0 / math.pi)


def _gelu(x):
    return x * (0.5 * (1.0 + jnp.tanh(_GELU_C * (x + 0.044715 * (x * x * x)))))


def _gelu_grad(x):
    t = jnp.tanh(_GELU_C * (x + 0.044715 * (x * x * x)))
    cdf = 0.5 * (1.0 + t)
    return cdf + x * (0.5 * (1.0 - t * t) * _GELU_C * (1.0 + 3.0 * 0.044715 * (x * x)))


def _rms_scale(x):
    return lax.rsqrt(jnp.mean(x * x, axis=-1, keepdims=True) + EPS)


def _rms_bwd(x, rs, g, dy):
    gdy = dy * g
    dx = rs * gdy - x * ((rs * rs * rs) * jnp.mean(gdy * x, axis=-1, keepdims=True))
    return dx, dy * (x * rs)


def _log1p(e):
    u = 1.0 + e
    d = u - 1.0
    return jnp.where(d == 0.0, e, jnp.log(u) * (e / jnp.where(d == 0.0, 1.0, d)))


def _softplus(y):
    return jnp.maximum(y, 0.0) + _log1p(jnp.exp(-jnp.abs(y)))


def _expm1(x):
    u = jnp.exp(x)
    lu = jnp.log(u)
    safe = jnp.where((u == 1.0) | (u == 0.0), 1.0, lu)
    return jnp.where(u == 1.0, x, jnp.where(u == 0.0, -1.0, (u - 1.0) * (x / safe)))


def _row_iota(shape):
    return lax.broadcasted_iota(jnp.int32, shape, 0)


def _lane_iota(shape):
    return lax.broadcasted_iota(jnp.int32, shape, 1)


def _rope_pair(gc):
    return gc + pltpu.roll(gc, 64, 1)


def _inproj(x, g, w1, tm):
    t = x.shape[0]
    widths = (D, 3 * D, Q_LORA, KV_LORA, KR_W)

    def body(x_ref, g_ref, w_ref, xn_ref, rx_ref, g3_ref, cq_ref, ckv_ref, kr_ref):
        xv = x_ref[...]
        xn = (xv * _rms_scale(xv) * g_ref[...]).astype(BF16)
        xn_ref[...] = xn
        col = 0
        for ref, w in zip((rx_ref, g3_ref, cq_ref, ckv_ref, kr_ref), widths):
            for c0 in range(0, w, 512):
                cw = min(512, w - c0)
                ref[:, c0:c0 + cw] = _dot(xn, w_ref[:, col + c0:col + c0 + cw])
            col += w

    return _pcall(
        body, "inproj", (t // tm,),
        [_rows(tm, D), _full(1, D), _full(D, W1_COLS)],
        [_rows(tm, D)] + [_rows(tm, w) for w in widths],
        [_sds((t, D), BF16)] + [_sds((t, w)) for w in widths],
    )(x, g, w1)


def _lru_gates(xa, wa_ref, ba, wx_ref, bx, pre_r, pre_i):
    xb = xa.astype(BF16)
    for n in range(RNN_BLOCKS):
        sl = slice(n * RNN_BW, (n + 1) * RNN_BW)
        pre_r[:, sl] = _dot(xb[:, sl], wa_ref[n])
        pre_i[:, sl] = _dot(xb[:, sl], wx_ref[n])
    r = _sigmoid(pre_r[...] + ba)
    i = _sigmoid(pre_i[...] + bx)
    return r, i


def _lru_fwd(rx, conv_w, conv_b, wa, ba, wx, bx, lam, tb):
    t = rx.shape[0]
    nb = t // tb

    def body(x_ref, xp_ref, cw_ref, cb_ref, wa_ref, ba_ref, wx_ref, bx_ref, lam_ref, h_ref, xa_ref, hc, tmp, pre_r, pre_i):
        i_blk = pl.program_id(0)

        @pl.when(i_blk == 0)
        def _():
            hc[...] = jnp.zeros_like(hc)

        xv = x_ref[...]
        xp = jnp.where(i_blk > 0, xp_ref[...], 0.0)
        row8 = _row_iota((8, D))
        xa = cb_ref[...] + cw_ref[3:4, :] * xv
        for s in (1, 2, 3):
            xr = pltpu.roll(xv, s, 0)
            tmp[...] = xr
            tmp[0:8, :] = jnp.where(row8 < s, pltpu.roll(xp, s, 0), xr[0:8, :])
            xa = xa + cw_ref[3 - s:4 - s, :] * tmp[...]
        xa_ref[...] = xa
        r, gi = _lru_gates(xa, wa_ref, ba_ref[...], wx_ref, bx_ref[...], pre_r, pre_i)
        la = (-LRU_C * _softplus(-lam_ref[...])) * r
        a = jnp.exp(la)
        b = jnp.sqrt(-_expm1(2.0 * la)) * (gi * xa)
        row = _row_iota((tb, D))
        sh = 1
        while sh < tb:
            m = row >= sh
            b = jnp.where(m, a * pltpu.roll(b, sh, 0) + b, b)
            a = jnp.where(m, a * pltpu.roll(a, sh, 0), a)
            sh *= 2
        h = a * hc[...] + b
        h_ref[...] = h
        hc[...] = h[tb - 1:tb, :]

    prev8 = pl.BlockSpec((8, D), lambda i: (jnp.maximum(i * (tb // 8) - 1, 0), 0))
    return _pcall(
        body, "lru_fwd", (nb,),
        [_rows(tb, D), prev8, _full(4, D), _full(1, D), _full(RNN_BLOCKS, RNN_BW, RNN_BW), _full(1, D),
         _full(RNN_BLOCKS, RNN_BW, RNN_BW), _full(1, D), _full(1, D)],
        [_rows(tb, D), _rows(tb, D)],
        [_sds((t, D)), _sds((t, D))],
        scratch=[pltpu.VMEM((1, D), F32), pltpu.VMEM((tb, D), F32), pltpu.VMEM((tb, D), F32), pltpu.VMEM((tb, D), F32)],
    )(rx, rx, conv_w, conv_b, wa, ba, wx, bx, lam)


def _mla_proj(cq, ckv, kr, qn, kvn, wq, wkv, rope_c, tm):
    t = cq.shape[0]

    def body(cq_ref, ckv_ref, kr_ref, qn_ref, kvn_ref, wq_ref, wkv_ref, c_ref, q_ref, k_ref, v_ref, cqn_ref, ckvn_ref):
        cqv = cq_ref[...]
        cqn = (cqv * _rms_scale(cqv) * qn_ref[...]).astype(BF16)
        ckvv = ckv_ref[...]
        ckvn = (ckvv * _rms_scale(ckvv) * kvn_ref[...]).astype(BF16)
        cqn_ref[...] = cqn
        ckvn_ref[...] = ckvn
        c = c_ref[...]
        lane = _lane_iota((tm, KR_W))
        kro = jnp.where(lane < 64, _rope_pair(kr_ref[...] * c), 0.0).astype(BF16)
        for h in range(N_HEADS):
            sl = slice(h * HEAD_W, (h + 1) * HEAD_W)
            qh = _dot(cqn, wq_ref[:, sl])
            q_ref[h, :, 0:128] = qh[:, 0:128].astype(BF16)
            q_ref[h, :, 128:256] = _rope_pair(qh[:, 128:256] * c).astype(BF16)
            kvh = _dot(ckvn, wkv_ref[:, sl])
            k_ref[h, :, 0:128] = kvh[:, 0:128].astype(BF16)
            k_ref[h, :, 128:256] = kro
            v_ref[h] = kvh[:, 128:256].astype(BF16)

    hb = lambda w: pl.BlockSpec((N_HEADS, tm, w), lambda i: (0, i, 0))
    return _pcall(
        body, "mla_proj", (t // tm,),
        [_rows(tm, Q_LORA), _rows(tm, KV_LORA), _rows(tm, KR_W), _full(1, Q_LORA), _full(1, KV_LORA),
         _full(Q_LORA, N_HEADS * HEAD_W), _full(KV_LORA, N_HEADS * HEAD_W), _rows(tm, KR_W)],
        [hb(HEAD_W), hb(HEAD_W), hb(V_HEAD), _rows(tm, Q_LORA), _rows(tm, KV_LORA)],
        [_sds((N_HEADS, t, HEAD_W), BF16), _sds((N_HEADS, t, HEAD_W), BF16), _sds((N_HEADS, t, V_HEAD), BF16),
         _sds((t, Q_LORA), BF16), _sds((t, KV_LORA), BF16)],
    )(cq, ckv, kr, qn, kvn, wq, wkv, rope_c)


EXP2_SCALE = SM_SCALE * math.log2(math.e)


def _flash_fwd(q, k, v, tq):
    t = q.shape[1]
    nq = t // tq

    def body(q_ref, k_ref, v_ref, o_ref, lse_ref):
        qi = pl.program_id(1)
        qv = q_ref[0]

        def block(ki, carry, diagonal):
            m, l, acc = carry
            rows = pl.ds(pl.multiple_of(ki * tq, tq), tq)
            s = _dot_nt(qv, k_ref[0, rows, :])
            if diagonal:
                s = jnp.where(_row_iota((tq, tq)) >= _lane_iota((tq, tq)), s, NEG)
            m_new = jnp.maximum(m, jnp.max(s, axis=1, keepdims=True))
            p = jnp.exp2((s - m_new) * EXP2_SCALE)
            alpha = jnp.exp2((m - m_new) * EXP2_SCALE)
            l = alpha * l + jnp.sum(p, axis=1, keepdims=True)
            acc = alpha * acc + _dot(p.astype(BF16), v_ref[0, rows, :])
            return m_new, l, acc

        init = (jnp.full((tq, 1), -jnp.inf, F32), jnp.zeros((tq, 1), F32), jnp.zeros((tq, V_HEAD), F32))
        carry = lax.fori_loop(0, qi // 2, lambda i, c: block(2 * i + 1, block(2 * i, c, False), False), init)
        carry = lax.fori_loop(2 * (qi // 2), qi, lambda ki, c: block(ki, c, False), carry)
        m, l, acc = block(qi, carry, True)
        o_ref[...] = acc / l
        lse_ref[...] = jnp.broadcast_to(m * EXP2_SCALE + jnp.log(l) * math.log2(math.e), (tq, V_HEAD))

    head = lambda w: pl.BlockSpec((1, t, w), lambda h, qi: (h, 0, 0))
    o_spec = pl.BlockSpec((tq, V_HEAD), lambda h, qi: (qi, h))
    return _pcall(
        body, "flash_fwd", (N_HEADS, nq),
        [pl.BlockSpec((1, tq, HEAD_W), lambda h, qi: (h, qi, 0)), head(HEAD_W), head(V_HEAD)],
        [o_spec, o_spec],
        [_sds((t, D)), _sds((t, D))],
    )(q, k, v)


def _merge_out(x, h, g3, yb, w_out, tm):
    t = x.shape[0]

    def body(x_ref, h_ref, g3_ref, yb_ref, w_ref, h1_ref, mg_ref):
        ya = h_ref[...] * _gelu(g3_ref[:, 0:D])
        merged = (_sigmoid(g3_ref[:, D:2 * D]) * ya + _sigmoid(g3_ref[:, 2 * D:3 * D]) * yb_ref[...]).astype(BF16)
        mg_ref[...] = merged
        h1_ref[...] = x_ref[...] + _dot(merged, w_ref[...])

    return _pcall(
        body, "merge_out", (t // tm,),
        [_rows(tm, D), _rows(tm, D), _rows(tm, 3 * D), _rows(tm, D), _full(D, D)],
        [_rows(tm, D), _rows(tm, D)],
        [_sds((t, D)), _sds((t, D), BF16)],
    )(x, h, g3, yb, w_out)


def _mlp_up(h1, g, w_up, tm):
    t = h1.shape[0]

    def body(h_ref, g_ref, w_ref, u_ref, n2_ref):
        hv = h_ref[...]
        n2 = (hv * _rms_scale(hv) * g_ref[...]).astype(BF16)
        n2_ref[...] = n2
        for c0 in range(0, D_FF, 512):
            u_ref[:, c0:c0 + 512] = _dot(n2, w_ref[:, c0:c0 + 512])

    return _pcall(
        body, "mlp_up", (t // tm,),
        [_rows(tm, D), _full(1, D), _full(D, D_FF)],
        [_rows(tm, D_FF), _rows(tm, D)],
        [_sds((t, D_FF)), _sds((t, D), BF16)],
    )(h1, g, w_up)


def _mlp_down_loss(u, h1, target, w_down, g, tm):
    t = u.shape[0]

    def body(u_ref, h1_ref, tg_ref, w_ref, g_ref, act_ref, dh2_ref, loss_ref, gnf_ref, lacc):
        i = pl.program_id(0)

        @pl.when(i == 0)
        def _():
            lacc[...] = jnp.zeros_like(lacc)
            gnf_ref[...] = jnp.zeros_like(gnf_ref)

        ru = jnp.maximum(u_ref[...], 0.0)
        act = (ru * ru).astype(BF16)
        act_ref[...] = act
        h2 = h1_ref[...] + _dot(act, w_ref[...])
        rs = _rms_scale(h2)
        gv = g_ref[...]
        err = h2 * rs * gv - tg_ref[...]
        lacc[...] += jnp.sum(err * err, axis=0, keepdims=True)
        dx, dgr = _rms_bwd(h2, rs, gv, err * (1.0 / D))
        dh2_ref[...] = dx
        gnf_ref[...] += jnp.sum(dgr, axis=0, keepdims=True)

        @pl.when(i == pl.num_programs(0) - 1)
        def _():
            loss_ref[...] = jnp.broadcast_to(jnp.sum(lacc[...], axis=1, keepdims=True) * (0.5 / D), (8, 128))

    return _pcall(
        body, "mlp_down_loss", (t // tm,),
        [_rows(tm, D_FF), _rows(tm, D), _rows(tm, D), _full(D_FF, D), _full(1, D)],
        [_rows(tm, D_FF), _rows(tm, D), _full(8, 128), _full(1, D)],
        [_sds((t, D_FF), BF16), _sds((t, D)), _sds((8, 128)), _sds((1, D))],
        scratch=[pltpu.VMEM((1, D), F32)],
    )(u, h1, target, w_down, g)


def _matmul_tn(a, g, name):
    t, kdim = a.shape
    ndim = g.shape[1]
    tk = min(kdim, 1024)
    tn = ndim if ndim <= 1024 else 1024
    tt = min(t, 512)
    nt = t // tt

    def body(a_ref, g_ref, o_ref):
        @pl.when(pl.program_id(2) == 0)
        def _():
            o_ref[...] = jnp.zeros_like(o_ref)

        o_ref[...] += _dot_tn(a_ref[...].astype(BF16), g_ref[...].astype(BF16))

    return _pcall(
        body, name, (kdim // tk, ndim // tn, nt),
        [pl.BlockSpec((tt, tk), lambda i, j, s: (s, i)), pl.BlockSpec((tt, tn), lambda i, j, s: (s, j))],
        pl.BlockSpec((tk, tn), lambda i, j, s: (i, j)),
        _sds((kdim, ndim)),
    )(a, g)


def _mlp_bwd_act(dh2, u, w_down, tm):
    t = u.shape[0]

    def body(d_ref, u_ref, w_ref, du_ref):
        db = d_ref[...].astype(BF16)
        for c0 in range(0, D_FF, 512):
            da = _dot_nt(db, w_ref[c0:c0 + 512, :])
            du_ref[:, c0:c0 + 512] = (da * (2.0 * jnp.maximum(u_ref[:, c0:c0 + 512], 0.0))).astype(BF16)

    return _pcall(
        body, "mlp_bwd_act", (t // tm,),
        [_rows(tm, D), _rows(tm, D_FF), _full(D_FF, D)],
        _rows(tm, D_FF), _sds((t, D_FF), BF16),
    )(dh2, u, w_down)


def _mlp_bwd_in(du, dh2, h1, w_up, g, tm):
    t = du.shape[0]

    def body(du_ref, d_ref, h_ref, w_ref, g_ref, dh1_ref, gacc_ref):
        @pl.when(pl.program_id(0) == 0)
        def _():
            gacc_ref[...] = jnp.zeros_like(gacc_ref)

        dn2 = _dot_nt(du_ref[...], w_ref[...])
        hv = h_ref[...]
        dx, dgr = _rms_bwd(hv, _rms_scale(hv), g_ref[...], dn2)
        dh1_ref[...] = d_ref[...] + dx
        gacc_ref[...] += jnp.sum(dgr, axis=0, keepdims=True)

    return _pcall(
        body, "mlp_bwd_in", (t // tm,),
        [_rows(tm, D_FF), _rows(tm, D), _rows(tm, D), _full(D, D_FF), _full(1, D)],
        [_rows(tm, D), _full(1, D)],
        [_sds((t, D)), _sds((1, D))],
    )(du, dh2, h1, w_up, g)


def _merge_bwd(dh1, w_out, g3, h, yb, merged, lse, tm):
    t = dh1.shape[0]

    def body(d_ref, w_ref, g3_ref, h_ref, yb_ref, mg_ref, lse_ref, dg3_ref, dyb_ref, st_ref, dh_ref, dwo_ref):
        @pl.when(pl.program_id(0) == 0)
        def _():
            dwo_ref[...] = jnp.zeros_like(dwo_ref)

        db = d_ref[...].astype(BF16)
        dwo_ref[...] += _dot_tn(mg_ref[...], db)
        dm = _dot_nt(db, w_ref[...])
        gv = g3_ref[:, 0:D]
        sa = _sigmoid(g3_ref[:, D:2 * D])
        sb = _sigmoid(g3_ref[:, 2 * D:3 * D])
        gel = _gelu(gv)
        hv = h_ref[...]
        ybv = yb_ref[...]
        dya = dm * sa
        dyb = dm * sb
        dg3_ref[:, 0:D] = (dya * hv * _gelu_grad(gv)).astype(BF16)
        dg3_ref[:, D:2 * D] = (dya * (hv * gel) * (1.0 - sa)).astype(BF16)
        dg3_ref[:, 2 * D:3 * D] = (dyb * ybv * (1.0 - sb)).astype(BF16)
        dh_ref[...] = dya * gel
        dyb_ref[...] = dyb.astype(BF16)
        prod = dyb * ybv
        lane = _lane_iota((tm, V_HEAD))
        for hh in range(N_HEADS):
            sl = slice(hh * V_HEAD, (hh + 1) * V_HEAD)
            st_ref[:, sl] = jnp.where(lane < 64, lse_ref[:, sl], jnp.sum(prod[:, sl], axis=1, keepdims=True))

    return _pcall(
        body, "merge_bwd", (t // tm,),
        [_rows(tm, D), _full(D, D), _rows(tm, 3 * D), _rows(tm, D), _rows(tm, D), _rows(tm, D), _rows(tm, D)],
        [_rows(tm, 3 * D), _rows(tm, D), _rows(tm, D), _rows(tm, D), _full(D, D)],
        [_sds((t, 3 * D), BF16), _sds((t, D), BF16), _sds((t, D)), _sds((t, D)), _sds((D, D))],
    )(dh1, w_out, g3, h, yb, merged, lse)


def _flash_bwd(q, k, v, do, stats, tq):
    t = q.shape[1]
    nq = t // tq

    def body(q_ref, k_ref, v_ref, do_ref, st_ref, dq_ref, dk_ref, dv_ref):
        ki = pl.program_id(1)

        @pl.when(ki == 0)
        def _():
            dq_ref[...] = jnp.zeros_like(dq_ref)

        kv_, vv = k_ref[0], v_ref[0]
        dk_ref[...] = jnp.zeros_like(dk_ref)
        dv_ref[...] = jnp.zeros_like(dv_ref)

        def block(qi, diagonal):
            rows = pl.ds(pl.multiple_of(qi * tq, tq), tq)
            qv, dov = q_ref[0, rows, :], do_ref[rows, :]
            p = jnp.exp2(_dot_nt(qv, kv_) * EXP2_SCALE - st_ref[rows, 0:1])
            if diagonal:
                p = jnp.where(_row_iota((tq, tq)) >= _lane_iota((tq, tq)), p, 0.0)
            dv_ref[0] += _dot_tn(p.astype(BF16), dov)
            dp = _dot_nt(dov, vv)
            ds = (p * (dp - st_ref[rows, 64:65]) * SM_SCALE).astype(BF16)
            dk_ref[0] += _dot_tn(ds, qv)
            dq_ref[0, rows, :] += _dot(ds, kv_)

        block(ki, True)

        def two(i, carry):
            block(ki + 1 + 2 * i, False)
            block(ki + 2 + 2 * i, False)
            return carry

        def one(qi, carry):
            block(qi, False)
            return carry

        pairs = (nq - 1 - ki) // 2
        lax.fori_loop(0, pairs, two, 0)
        lax.fori_loop(ki + 1 + 2 * pairs, nq, one, 0)

    kv_spec = lambda w: pl.BlockSpec((1, tq, w), lambda h, ki: (h, ki, 0))
    col = pl.BlockSpec((t, V_HEAD), lambda h, ki: (0, h))
    whole = pl.BlockSpec((1, t, HEAD_W), lambda h, ki: (h, 0, 0))
    return _pcall(
        body, "flash_bwd", (N_HEADS, nq),
        [whole, kv_spec(HEAD_W), kv_spec(V_HEAD), col, col],
        [whole, kv_spec(HEAD_W), kv_spec(V_HEAD)],
        [_sds((N_HEADS, t, HEAD_W)), _sds((N_HEADS, t, HEAD_W)), _sds((N_HEADS, t, V_HEAD))],
    )(q, k, v, do, stats)


def _mla_bwd(dq, dk, dv, cqn, ckvn, cq, ckv, rope_c, wq, wkv, qn, kvn, tm):
    t = cq.shape[0]

    def body(dq_ref, dk_ref, dv_ref, cqn_ref, ckvn_ref, cq_ref, ckv_ref, c_ref, wq_ref, wkv_ref, qn_ref, kvn_ref,
             dmla_ref, dwq_ref, dwkv_ref, dqn_ref, dkvn_ref):
        @pl.when(pl.program_id(0) == 0)
        def _():
            dwq_ref[...] = jnp.zeros_like(dwq_ref)
            dwkv_ref[...] = jnp.zeros_like(dwkv_ref)
            dqn_ref[...] = jnp.zeros_like(dqn_ref)
            dkvn_ref[...] = jnp.zeros_like(dkvn_ref)

        c = c_ref[...]
        lane = _lane_iota((tm, KR_W))
        cqn, ckvn = cqn_ref[...], ckvn_ref[...]
        dcqn = jnp.zeros((tm, Q_LORA), F32)
        dckvn = jnp.zeros((tm, KV_LORA), F32)
        dkr = jnp.zeros((tm, KR_W), F32)
        for h in range(N_HEADS):
            sl = slice(h * HEAD_W, (h + 1) * HEAD_W)
            droped = jnp.where(lane < 64, dq_ref[h, :, 128:256], 0.0)
            dqp = jnp.concatenate([dq_ref[h, :, 0:128], _rope_pair(droped) * c], axis=1).astype(BF16)
            dcqn = dcqn + _dot_nt(dqp, wq_ref[:, sl])
            dwq_ref[:, sl] += _dot_tn(cqn, dqp)
            dkr = dkr + jnp.where(lane < 64, dk_ref[h, :, 128:256], 0.0)
            dkvp = jnp.concatenate([dk_ref[h, :, 0:128], dv_ref[h]], axis=1).astype(BF16)
            dckvn = dckvn + _dot_nt(dkvp, wkv_ref[:, sl])
            dwkv_ref[:, sl] += _dot_tn(ckvn, dkvp)
        cqv, ckvv = cq_ref[...], ckv_ref[...]
        dcq, dgq = _rms_bwd(cqv, _rms_scale(cqv), qn_ref[...], dcqn)
        dckv, dgkv = _rms_bwd(ckvv, _rms_scale(ckvv), kvn_ref[...], dckvn)
        dqn_ref[...] += jnp.sum(dgq, axis=0, keepdims=True)
        dkvn_ref[...] += jnp.sum(dgkv, axis=0, keepdims=True)
        dmla_ref[:, 0:256] = dcq.astype(BF16)
        dmla_ref[:, 256:512] = dckv.astype(BF16)
        dmla_ref[:, 512:640] = (_rope_pair(dkr) * c).astype(BF16)

    hb = lambda w: pl.BlockSpec((N_HEADS, tm, w), lambda i: (0, i, 0))
    wide = N_HEADS * HEAD_W
    return _pcall(
        body, "mla_bwd", (t // tm,),
        [hb(HEAD_W), hb(HEAD_W), hb(V_HEAD), _rows(tm, Q_LORA), _rows(tm, KV_LORA), _rows(tm, Q_LORA), _rows(tm, KV_LORA),
         _rows(tm, KR_W), _full(Q_LORA, wide), _full(KV_LORA, wide), _full(1, Q_LORA), _full(1, KV_LORA)],
        [_rows(tm, 640), _full(Q_LORA, wide), _full(KV_LORA, wide), _full(1, Q_LORA), _full(1, KV_LORA)],
        [_sds((t, 640), BF16), _sds((Q_LORA, wide)), _sds((KV_LORA, wide)), _sds((1, Q_LORA)), _sds((1, KV_LORA))],
    )(dq, dk, dv, cqn, ckvn, cq, ckv, rope_c, wq, wkv, qn, kvn)


def _lru_bwd(dh, xa, h, rx, conv_w, wa, ba, wx, bx, lam, tb):
    t = dh.shape[0]
    nb = t // tb

    def body(dh_ref, xa_ref, h_ref, hp_ref, x_ref, cw_ref, wa_ref, ba_ref, wx_ref, bx_ref, lam_ref,
             drx_ref, dcw_ref, dcb_ref, dwa_ref, dba_ref, dwx_ref, dbx_ref, dlam_ref, gc, dxn, tmp, pre_r, pre_i):
        step = pl.program_id(0)
        first_block = step == nb - 1

        @pl.when(step == 0)
        def _():
            gc[...] = jnp.zeros_like(gc)
            dxn[...] = jnp.zeros_like(dxn)
            for ref in (dcw_ref, dcb_ref, dwa_ref, dba_ref, dwx_ref, dbx_ref, dlam_ref):
                ref[...] = jnp.zeros_like(ref)

        xa = xa_ref[...]
        r, gi = _lru_gates(xa, wa_ref, ba_ref[...], wx_ref, bx_ref[...], pre_r, pre_i)
        lamv = lam_ref[...]
        sp = _softplus(-lamv)
        la = (-LRU_C * sp) * r
        a = jnp.exp(la)
        e2 = _expm1(2.0 * la)
        sq = jnp.sqrt(-e2)
        row = _row_iota((tb, D))
        cf = jnp.where(row == tb - 1, 1.0, pltpu.roll(a, tb - 1, 0))
        bv = dh_ref[...]
        sh = 1
        while sh < tb:
            m = row < tb - sh
            bv = jnp.where(m, bv + cf * pltpu.roll(bv, tb - sh, 0), bv)
            cf = jnp.where(m, cf * pltpu.roll(cf, tb - sh, 0), cf)
            sh *= 2
        delta = bv + cf * gc[...]
        gc[...] = a[0:1, :] * delta[0:1, :]
        hv = h_ref[...]
        hr = pltpu.roll(hv, 1, 0)
        tmp[...] = hr
        tmp[0:1, :] = jnp.where(first_block, 0.0, hp_ref[7:8, :])
        hprev = tmp[...]
        ix = gi * xa
        dla = (delta * hprev) * a - (delta * ix) * ((e2 + 1.0) / sq)
        dlam_ref[...] += jnp.sum(dla * r, axis=0, keepdims=True) * (LRU_C * _sigmoid(-lamv))
        dpr = (dla * (-LRU_C * sp)) * r * (1.0 - r)
        dsq = delta * sq
        dpi = (dsq * xa) * gi * (1.0 - gi)
        dba_ref[...] += jnp.sum(dpr, axis=0, keepdims=True)
        dbx_ref[...] += jnp.sum(dpi, axis=0, keepdims=True)
        pre_r[...] = dpr
        pre_i[...] = dpi
        xb = xa.astype(BF16)
        for n in range(RNN_BLOCKS):
            sl = slice(n * RNN_BW, (n + 1) * RNN_BW)
            dprn = pre_r[:, sl].astype(BF16)
            dpin = pre_i[:, sl].astype(BF16)
            dwa_ref[n] += _dot_tn(xb[:, sl], dprn)
            dwx_ref[n] += _dot_tn(xb[:, sl], dpin)
            tmp[:, sl] = _dot_nt(dprn, wa_ref[n]) + _dot_nt(dpin, wx_ref[n])
        dxa = dsq * gi + tmp[...]
        dcb_ref[...] += jnp.sum(dxa, axis=0, keepdims=True)
        xv = x_ref[...]
        drx = cw_ref[3:4, :] * dxa
        dcw_ref[3:4, :] += jnp.sum(dxa * xv, axis=0, keepdims=True)
        row8 = _row_iota((8, D))
        nxt = dxn[...]
        for s in (1, 2, 3):
            dr_ = pltpu.roll(dxa, tb - s, 0)
            tmp[...] = dr_
            tmp[tb - 8:tb, :] = jnp.where(row8 >= 8 - s, pltpu.roll(nxt, 8 - s, 0), dr_[tb - 8:tb, :])
            dxs = tmp[...]
            drx = drx + cw_ref[3 - s:4 - s, :] * dxs
            dcw_ref[3 - s:4 - s, :] += jnp.sum(dxs * xv, axis=0, keepdims=True)
        drx_ref[...] = drx.astype(BF16)
        dxn[...] = dxa[0:8, :]

    rev = pl.BlockSpec((tb, D), lambda i: (nb - 1 - i, 0))
    prev8 = pl.BlockSpec((8, D), lambda i: (jnp.maximum((nb - 1 - i) * (tb // 8) - 1, 0), 0))
    wblk = _full(RNN_BLOCKS, RNN_BW, RNN_BW)
    return _pcall(
        body, "lru_bwd", (nb,),
        [rev, rev, rev, prev8, rev, _full(4, D), wblk, _full(1, D), wblk, _full(1, D), _full(1, D)],
        [rev, _full(4, D), _full(1, D), wblk, _full(1, D), wblk, _full(1, D), _full(1, D)],
        [_sds((t, D), BF16), _sds((4, D)), _sds((1, D)), _sds((RNN_BLOCKS, RNN_BW, RNN_BW)), _sds((1, D)),
         _sds((RNN_BLOCKS, RNN_BW, RNN_BW)), _sds((1, D)), _sds((1, D))],
        scratch=[pltpu.VMEM((1, D), F32), pltpu.VMEM((8, D), F32), pltpu.VMEM((tb, D), F32), pltpu.VMEM((tb, D), F32),
                 pltpu.VMEM((tb, D), F32)],
    )(dh, xa, h, h, rx, conv_w, wa, ba, wx, bx, lam)


def _inproj_bwd(x, dh1, drx, dg3, dmla, w1, g, tm):
    t = x.shape[0]

    def body(x_ref, d_ref, drx_ref, dg3_ref, dmla_ref, w_ref, g_ref, dx_ref, gacc_ref):
        @pl.when(pl.program_id(0) == 0)
        def _():
            gacc_ref[...] = jnp.zeros_like(gacc_ref)

        dxn = _dot_nt(drx_ref[...], w_ref[:, 0:D])
        for c0 in range(0, 3 * D, D):
            dxn = dxn + _dot_nt(dg3_ref[:, c0:c0 + D], w_ref[:, D + c0:2 * D + c0])
        dxn = dxn + _dot_nt(dmla_ref[...], w_ref[:, 4 * D:W1_COLS])
        xv = x_ref[...]
        dx, dgr = _rms_bwd(xv, _rms_scale(xv), g_ref[...], dxn)
        dx_ref[...] = d_ref[...] + dx
        gacc_ref[...] += jnp.sum(dgr, axis=0, keepdims=True)

    return _pcall(
        body, "inproj_bwd", (t // tm,),
        [_rows(tm, D), _rows(tm, D), _rows(tm, D), _rows(tm, 3 * D), _rows(tm, 640), _full(D, W1_COLS), _full(1, D)],
        [_rows(tm, D), _full(1, D)],
        [_sds((t, D)), _sds((1, D))],
    )(x, dh1, drx, dg3, dmla, w1, g)


def _pcall_indexed(body, name, index, grid, in_specs, out_specs, out_shape):
    call = pl.pallas_call(
        body, name=name, out_shape=out_shape,
        grid_spec=pltpu.PrefetchScalarGridSpec(num_scalar_prefetch=1, grid=grid, in_specs=in_specs, out_specs=out_specs),
        compiler_params=pltpu.CompilerParams(dimension_semantics=("arbitrary",) * len(grid), vmem_limit_bytes=V7X_VMEM_LIMIT))
    return lambda *operands: call(index, *operands)


def _pair_sum(halves, theirs, core, out_dtype, name):
    _, rows, cols = halves.shape
    tm = _row_tile(rows)

    def body(c_ref, a_ref, b_ref, o_ref):
        o_ref[...] = (a_ref[0] + b_ref[...]).astype(out_dtype)

    plain = pl.BlockSpec((tm, cols), lambda i, c: (i, 0))
    return _pcall_indexed(body, name, core, (rows // tm,),
                          [pl.BlockSpec((1, tm, cols), lambda i, c: (c[0], i, 0)), plain], plain,
                          _sds((rows, cols), out_dtype))(halves, theirs)


def _chip_sum(parts, recv, chip, name):
    _, rows, cols = parts.shape
    tm = _row_tile(rows)

    def body(c_ref, a_ref, r_ref, o_ref):
        o_ref[...] = ((a_ref[0].astype(F32) + r_ref[0].astype(F32)) + r_ref[1].astype(F32)) + r_ref[2].astype(F32)

    return _pcall_indexed(body, name, chip, (rows // tm,),
                          [pl.BlockSpec((1, tm, cols), lambda i, c: (c[0], i, 0)),
                           pl.BlockSpec((N_CHIPS - 1, tm, cols), lambda i, c: (0, i, 0))],
                          pl.BlockSpec((tm, cols), lambda i, c: (i, 0)), _sds((rows, cols)))(parts, recv)


def _adam_math(w, gv, m, v):
    mn = ADAM_B1 * m + (1.0 - ADAM_B1) * gv
    vn = ADAM_B2 * v + (1.0 - ADAM_B2) * (gv * gv)
    m_hat = mn / (1.0 - ADAM_B1 ** ADAM_STEP)
    v_hat = vn / (1.0 - ADAM_B2 ** ADAM_STEP)
    return -ADAM_LR * (m_hat / (jnp.sqrt(v_hat) + ADAM_EPS) + ADAM_WD * w), mn, vn


def _adamw(w, g, m, v, name):
    rows, cols = w.shape
    tm = _row_tile(rows)

    def body(w_ref, g_ref, m_ref, v_ref, d_ref, mo_ref, vo_ref):
        d_ref[...], mo_ref[...], vo_ref[...] = _adam_math(w_ref[...], g_ref[...], m_ref[...], v_ref[...])

    spec = _rows(tm, cols)
    return _pcall(body, name, (rows // tm,), [spec] * 4, [spec] * 3, [_sds((rows, cols))] * 3)(w, g, m, v)


def _adamw_halves(w, mine, theirs, m, v, core, name):
    rows, cols = w.shape
    tm = _row_tile(rows // 2)
    nh = rows // 2 // tm

    def body(c_ref, w_ref, a_ref, b_ref, m_ref, v_ref, g_ref, d_ref, mo_ref, vo_ref):
        gv = jnp.where(pl.program_id(0) // nh == c_ref[0], a_ref[...], b_ref[...])
        g_ref[...] = gv
        d_ref[...], mo_ref[...], vo_ref[...] = _adam_math(w_ref[...], gv, m_ref[...], v_ref[...])

    full = pl.BlockSpec((tm, cols), lambda i, c: (i, 0))
    half = pl.BlockSpec((tm, cols), lambda i, c: (i % nh, 0))
    return _pcall_indexed(body, name, core, (rows // tm,), [full, half, half, full, full], [full] * 4,
                          [_sds((rows, cols))] * 4)(w, mine, theirs, m, v)


REL_SIBLING = (0, 0, 1)
REL_CHIPS = ((1, 0, 0), (0, 1, 0), (1, 1, 0))


V7X_DMA_CHUNK_BYTES = 1 << 20


def _split_copy(src, dst, shape, itemsize):
    nbytes = math.prod(shape) * itemsize
    if nbytes <= V7X_DMA_CHUNK_BYTES or len(shape) < 2:
        return [(src, dst)]
    if len(shape) > 2:
        out = []
        for k in range(shape[0]):
            out += _split_copy(src.at[k], dst.at[k], shape[1:], itemsize)
        return out
    rows = shape[0]
    sub = 8 * (4 // itemsize)
    parts = max(1, min(-(-nbytes // V7X_DMA_CHUNK_BYTES), rows // sub))
    while rows % parts or (rows // parts) % sub:
        parts -= 1
    step = rows // parts
    return [(src.at[pl.ds(k * step, step)], dst.at[pl.ds(k * step, step)]) for k in range(parts)]


def _comm(name, ins, out_shapes, n_ops, ops_fn):
    n_in, n_out = len(ins), len(out_shapes)

    def body(*refs):
        in_refs, out_refs = refs[:n_in], refs[n_in:n_in + n_out]
        send_sems, recv_sems = refs[n_in + n_out:]
        pos = (lax.axis_index("x"), lax.axis_index("y"), lax.axis_index("c"))
        ops = ops_fn(in_refs, out_refs, pos)
        assert len(ops) == n_ops

        def make(i, rel, src, dst):
            if rel is None:
                return pltpu.make_async_copy(src, dst, send_sems.at[i])
            peer = tuple((p + r) % 2 for p, r in zip(pos, rel))
            return pltpu.make_async_remote_copy(src_ref=src, dst_ref=dst, send_sem=send_sems.at[i], recv_sem=recv_sems.at[i],
                                                device_id=peer, device_id_type=MESH_ID)

        for i, (rel, src, dst) in enumerate(ops):
            for s_piece, d_piece in _split_copy(src, dst, src.shape, jnp.dtype(src.dtype).itemsize):
                make(i, rel, s_piece, d_piece).start()
        for i, (rel, src, dst) in enumerate(ops):
            make(i, rel, src, dst).wait()

    hbm = pl.BlockSpec(memory_space=pl.ANY)
    return pl.pallas_call(
        body, name=name, in_specs=[hbm] * n_in, out_specs=[hbm] * n_out, out_shape=list(out_shapes),
        scratch_shapes=[pltpu.SemaphoreType.DMA((n_ops,)), pltpu.SemaphoreType.DMA((n_ops,))],
    )(*ins)


def _chip_of(pos, rel=(0, 0, 0)):
    return 2 * ((pos[0] + rel[0]) % 2) + (pos[1] + rel[1]) % 2


def _gather_chips(shards, chip, name):
    def ops_fn(in_refs, out_refs, pos):
        me = _chip_of(pos)
        return [(rel, src, dst.at[me]) for src, dst in zip(in_refs, out_refs) for rel in REL_CHIPS]

    outs = _comm(name, shards, [_sds((N_CHIPS,) + s.shape, s.dtype) for s in shards], 3 * len(shards), ops_fn)
    return [lax.dynamic_update_index_in_dim(o, s, chip, 0) for o, s in zip(outs, shards)]


def _sibling_split(gs):
    def ops_fn(in_refs, out_refs, pos):
        return [(REL_SIBLING, src.at[1 - pos[2]], dst) for src, dst in zip(in_refs, out_refs)]

    return _comm("grad_sibling_split", gs, [_sds(g.shape[1:], g.dtype) for g in gs], len(gs), ops_fn)


def _chip_exchange(ps):
    def ops_fn(in_refs, out_refs, pos):
        return [(rel, src.at[_chip_of(pos, rel)], dst.at[j])
                for src, dst in zip(in_refs, out_refs) for j, rel in enumerate(REL_CHIPS)]

    return _comm("grad_chip_exchange", ps, [_sds((N_CHIPS - 1,) + p.shape[1:], p.dtype) for p in ps], 3 * len(ps), ops_fn)


def _sibling_join(hs):
    def ops_fn(in_refs, out_refs, pos):
        return [(REL_SIBLING, src, dst) for src, dst in zip(in_refs, out_refs)]

    return _comm("grad_sibling_join", hs, [_sds(h.shape, h.dtype) for h in hs], len(hs), ops_fn)


def _rot_cols(w):
    return jnp.concatenate([-w[..., 32:], w[..., :32]], axis=-1)


def _unrot_cols(dw):
    return jnp.concatenate([dw[..., 32:], -dw[..., :32]], axis=-1)


IN_OFFS = (0, 1024, 2048, 2304, 2560, 2624, 3648, 4672)


def _w1_from_w_in(w):
    seg = [w[:, IN_OFFS[i]:IN_OFFS[i + 1]] for i in range(7)]
    rnn_x, rnn_gate, cq, ckv, kr, ga, gb = seg
    return jnp.concatenate([rnn_x, rnn_gate, ga, gb, cq, ckv, kr, _rot_cols(kr)], axis=1)


def _w_in_grad_from_parts(d_rx, d_g3, d_mla):
    kr = d_mla[:, 512:576] + _unrot_cols(d_mla[:, 576:640])
    return jnp.concatenate([d_rx, d_g3[:, 0:D], d_mla[:, 0:512], kr, d_g3[:, D:3 * D]], axis=1)


def _wq_from_w_uq(w):
    w3 = w.reshape(Q_LORA, N_HEADS, QK_NOPE + QK_ROPE)
    rope = w3[..., QK_NOPE:]
    return jnp.concatenate([w3[..., :QK_NOPE], rope, _rot_cols(rope)], axis=-1).reshape(Q_LORA, N_HEADS * HEAD_W)


def _w_uq_grad_from_wq(dw):
    d3 = dw.reshape(Q_LORA, N_HEADS, HEAD_W)
    rope = d3[..., 128:192] + _unrot_cols(d3[..., 192:256])
    return jnp.concatenate([d3[..., :128], rope], axis=-1).reshape(Q_LORA, N_HEADS * (QK_NOPE + QK_ROPE))


def _cols_from_chunks(g):
    return g.transpose(1, 0, 2).reshape(g.shape[1], N_CHIPS * g.shape[2])


def _halves_of_col_chunks(dw):
    r, c4 = dw.shape
    return dw.reshape(2, r // 2, N_CHIPS, c4 // N_CHIPS).transpose(0, 2, 1, 3)


def _halves_of_row_chunks(dw):
    r4, c = dw.shape
    return dw.reshape(N_CHIPS, 2, r4 // (2 * N_CHIPS), c).transpose(1, 0, 2, 3)


def kernel(x, norm_mix, w_in, conv_w, conv_b, lru_wa, lru_ba, lru_wx, lru_bx, lru_lambda, q_norm, w_uq, kv_norm, w_ukv, w_out, norm_mlp, w_up, w_down, norm_final, loss_target, m_norm_mix, m_w_in, m_conv_w, m_conv_b, m_lru_wa, m_lru_ba, m_lru_wx, m_lru_bx, m_lru_lambda, m_q_norm, m_w_uq, m_kv_norm, m_w_ukv, m_w_out, m_norm_mlp, m_w_up, m_w_down, m_norm_final, v_norm_mix, v_w_in, v_conv_w, v_conv_b, v_lru_wa, v_lru_ba, v_lru_wx, v_lru_bx, v_lru_lambda, v_q_norm, v_w_uq, v_kv_norm, v_w_ukv, v_w_out, v_norm_mlp, v_w_up, v_w_down, v_norm_final):
    t = x.shape[1]
    tm = min(256, t)
    tq = min(512, max(128, t // 4))
    x2 = x[0]
    target = loss_target[0]
    chip = 2 * lax.axis_index("x") + lax.axis_index("y")
    core = lax.axis_index("c")
    chip_ix, core_ix = chip.reshape(1).astype(jnp.int32), core.reshape(1).astype(jnp.int32)
    row = lambda p: p.reshape(1, -1)

    big_shards = (w_in, w_uq, w_ukv, w_out, w_up, w_down)
    gathered = _gather_chips([w.astype(BF16) for w in big_shards] + [conv_w], chip, "weight_gather")
    w1 = _w1_from_w_in(_cols_from_chunks(gathered[0]))
    wq = _wq_from_w_uq(_cols_from_chunks(gathered[1]))
    wkv = _cols_from_chunks(gathered[2])
    w_out_f = gathered[3].reshape(D, D)
    w_up_f = _cols_from_chunks(gathered[4])
    w_down_f = gathered[5].reshape(D_FF, D)
    conv_w_f = _cols_from_chunks(gathered[6])
    wa_b, wx_b = lru_wa.astype(BF16), lru_wx.astype(BF16)

    pos = jnp.arange(t, dtype=F32)
    inv_freq = 1.0 / (ROPE_THETA ** (jnp.arange(0, QK_ROPE, 2, dtype=F32) / QK_ROPE))
    ang = pos[:, None] * inv_freq[None, :]
    rope_c = jnp.concatenate([jnp.cos(ang), jnp.cos(ang), jnp.sin(ang), jnp.sin(ang)], axis=-1)

    xn, rx, g3, cq, ckv, kr = _inproj(x2, row(norm_mix), w1, tm)
    h, xa = _lru_fwd(rx, conv_w_f, row(conv_b), wa_b, row(lru_ba), wx_b, row(lru_bx), row(lru_lambda), tm)
    q, k, v, cqn, ckvn = _mla_proj(cq, ckv, kr, row(q_norm), row(kv_norm), wq, wkv, rope_c, tm)
    yb, lse = _flash_fwd(q, k, v, tq)
    h1, merged = _merge_out(x2, h, g3, yb, w_out_f, tm)
    u, n2 = _mlp_up(h1, row(norm_mlp), w_up_f, tm)
    act, dh2, loss_blk, g_norm_final = _mlp_down_loss(u, h1, target, w_down_f, row(norm_final), tm)
    loss = lax.psum(loss_blk[0, 0], ("x", "y", "c"))

    g_w_down = _matmul_tn(act, dh2, "grad_w_down")
    du = _mlp_bwd_act(dh2, u, w_down_f, tm)
    dh1, g_norm_mlp = _mlp_bwd_in(du, dh2, h1, w_up_f, row(norm_mlp), tm)
    g_w_up = _matmul_tn(n2, du, "grad_w_up")
    dg3, dyb, stats, dh, g_w_out = _merge_bwd(dh1, w_out_f, g3, h, yb, merged, lse, tm)
    dq, dk, dv = _flash_bwd(q, k, v, dyb, stats, tq)
    dmla, g_wq, g_wkv, g_q_norm, g_kv_norm = _mla_bwd(dq, dk, dv, cqn, ckvn, cq, ckv, rope_c, wq, wkv, row(q_norm), row(kv_norm), tm)
    drx, g_conv_w, g_conv_b, g_wa, g_ba, g_wx, g_bx, g_lam = _lru_bwd(
        dh, xa, h, rx, conv_w_f, wa_b, row(lru_ba), wx_b, row(lru_bx), row(lru_lambda), tm)
    grad_x, g_norm_mix = _inproj_bwd(x2, dh1, drx, dg3, dmla, w1, row(norm_mix), tm)
    g_w_in = _w_in_grad_from_parts(_matmul_tn(xn, drx, "grad_w_in_rx"), _matmul_tn(xn, dg3, "grad_w_in_gates"),
                                   _matmul_tn(xn, dmla, "grad_w_in_mla"))
    g_w_uq = _w_uq_grad_from_wq(g_wq)

    smalls = (g_norm_mix, g_conv_b, g_wa, g_ba, g_wx, g_bx, g_lam, g_q_norm, g_kv_norm, g_norm_mlp, g_norm_final, g_conv_w)
    s_flat = jnp.concatenate([s.reshape(-1) for s in smalls] + [jnp.zeros((S_LEN - N_SMALL - CONVW_SIZE,), F32)])
    halves = [_halves_of_col_chunks(g_w_in), _halves_of_col_chunks(g_w_uq), _halves_of_col_chunks(g_wkv),
              _halves_of_row_chunks(g_w_out), _halves_of_col_chunks(g_w_up), _halves_of_row_chunks(g_w_down),
              s_flat.reshape(N_CHIPS, 2, S_ROWS_HALF, 128).transpose(1, 0, 2, 3)]
    theirs = _sibling_split(halves)
    parts = []
    for a, (hv, r) in enumerate(zip(halves, theirs)):
        dt = F32 if a == len(halves) - 1 else BF16
        pair = _pair_sum(hv.reshape(2, -1, hv.shape[-1]), r.reshape(-1, r.shape[-1]), core_ix, dt, f"grad_pair_sum_{a}")
        parts.append(pair.reshape(r.shape))
    received = _chip_exchange(parts)
    reduced = [_chip_sum(p, r, chip_ix, f"grad_chip_sum_{a}") for a, (p, r) in enumerate(zip(parts, received))]
    reduced_sibling = _sibling_join(reduced)
    s_mine, s_theirs = reduced[-1], reduced_sibling[-1]
    s_chunk = jnp.where(core == 0, jnp.concatenate([s_mine, s_theirs]), jnp.concatenate([s_theirs, s_mine]))
    s_all = _gather_chips([s_chunk], chip, "small_grad_gather")[0].reshape(-1)

    small_grads = []
    off = 0
    for shp, n in zip(SMALL_SHAPES, SMALL_SIZES):
        small_grads.append(s_all[off:off + n].reshape(shp))
        off += n
    g_conv_w_mine = lax.dynamic_slice_in_dim(s_all[off:off + CONVW_SIZE].reshape(4, D), chip * (D // N_CHIPS), D // N_CHIPS, axis=1)

    big_m = (m_w_in, m_w_uq, m_w_ukv, m_w_out, m_w_up, m_w_down)
    big_v = (v_w_in, v_w_uq, v_w_ukv, v_w_out, v_w_up, v_w_down)
    big_names = ("w_in", "w_uq", "w_ukv", "w_out", "w_up", "w_down")
    big_upd = [_adamw_halves(w, gm, gt, m, v, core_ix, "adamw_" + n)
               for w, gm, gt, m, v, n in zip(big_shards, reduced, reduced_sibling, big_m, big_v, big_names)]

    small_w = (norm_mix, conv_b, lru_wa, lru_ba, lru_wx, lru_bx, lru_lambda, q_norm, kv_norm, norm_mlp, norm_final)
    small_m = (m_norm_mix, m_conv_b, m_lru_wa, m_lru_ba, m_lru_wx, m_lru_bx, m_lru_lambda, m_q_norm, m_kv_norm, m_norm_mlp, m_norm_final)
    small_v = (v_norm_mix, v_conv_b, v_lru_wa, v_lru_ba, v_lru_wx, v_lru_bx, v_lru_lambda, v_q_norm, v_kv_norm, v_norm_mlp, v_norm_final)

    def pack(items, last, fill):
        flat = jnp.concatenate([i.reshape(-1) for i in items] + [last.reshape(-1)])
        return jnp.concatenate([flat, jnp.full((PACK_ROWS * 128 - flat.shape[0],), fill, F32)]).reshape(PACK_ROWS, 128)

    packed = _adamw(pack(small_w, conv_w, 0.0), pack(small_grads, g_conv_w_mine, 0.0), pack(small_m, m_conv_w, 0.0),
                    pack(small_v, v_conv_w, 1.0), "adamw_small")

    def unpack(p):
        flat = p.reshape(-1)
        outs, o = [], 0
        for shp, n in zip(SMALL_SHAPES, SMALL_SIZES):
            outs.append(flat[o:o + n].reshape(shp))
            o += n
        return outs, flat[o:o + CONVW_SIZE // N_CHIPS].reshape(4, D // N_CHIPS)

    order = ("norm_mix", "w_in", "conv_w", "conv_b", "lru_wa", "lru_ba", "lru_wx", "lru_bx", "lru_lambda", "q_norm", "w_uq", "kv_norm",
             "w_ukv", "w_out", "norm_mlp", "w_up", "w_down", "norm_final")

    def assemble(small_list, conv_w_item, big_list):
        table = dict(zip(SMALL_NAMES, small_list))
        table["conv_w"] = conv_w_item
        table.update(zip(big_names, big_list))
        return [table[n] for n in order]

    outs = [loss, grad_x.reshape(1, t, D)]
    outs += assemble(small_grads, g_conv_w_mine, [b[0] for b in big_upd])
    for j in range(3):
        sm, cw = unpack(packed[j])
        outs += assemble(sm, cw, [b[j + 1] for b in big_upd])
    return tuple(outs)
```

```python
import functools
import math

import jax
import jax.numpy as jnp
from jax import lax
from jax.experimental import pallas as pl
from jax.experimental.pallas import tpu as pltpu

F32 = jnp.float32
BF16 = jnp.bfloat16

D = 1024
N_HEADS = 8
QK_NOPE = 128
QK_ROPE = 64
V_HEAD = 128
Q_LORA = 256
KV_LORA = 256
D_FF = 4096
RNN_BLOCKS = 8
RNN_BW = 128
LRU_C = 8.0
EPS = 1e-6
ROPE_THETA = 10000.0
HEAD_W = 256
KR_W = 128
W1_COLS = 4 * D + Q_LORA + KV_LORA + KR_W
SM_SCALE = (QK_NOPE + QK_ROPE) ** -0.5
NEG = float(jnp.finfo(jnp.float32).min)

ADAM_LR = 0.001
ADAM_B1 = 0.9
ADAM_B2 = 0.999
ADAM_EPS = 1e-08
ADAM_WD = 0.01
ADAM_STEP = 10

N_CHIPS = 4
V7X_VMEM_LIMIT = 56 * 1024 * 1024
MESH_ID = pl.DeviceIdType.MESH

SMALL_NAMES = ("norm_mix", "conv_b", "lru_wa", "lru_ba", "lru_wx", "lru_bx", "lru_lambda", "q_norm", "kv_norm", "norm_mlp", "norm_final")
SMALL_SHAPES = ((D,), (D,), (RNN_BLOCKS, RNN_BW, RNN_BW), (RNN_BLOCKS, RNN_BW), (RNN_BLOCKS, RNN_BW, RNN_BW), (RNN_BLOCKS, RNN_BW), (D,),
                (Q_LORA,), (KV_LORA,), (D,), (D,))
SMALL_SIZES = tuple(math.prod(s) for s in SMALL_SHAPES)
N_SMALL = sum(SMALL_SIZES)
CONVW_SIZE = 4 * D
S_LEN = -(-(N_SMALL + CONVW_SIZE) // 8192) * 8192
S_ROWS_HALF = S_LEN // (N_CHIPS * 2 * 128)
PACK_ROWS = -(-(N_SMALL + CONVW_SIZE // N_CHIPS) // (256 * 128)) * 256


def _pcall(body, name, grid, in_specs, out_specs, out_shape, scratch=()):
    return pl.pallas_call(
        body, name=name, grid=grid, in_specs=in_specs, out_specs=out_specs, out_shape=out_shape,
        scratch_shapes=list(scratch),
        compiler_params=pltpu.CompilerParams(dimension_semantics=("arbitrary",) * len(grid), vmem_limit_bytes=V7X_VMEM_LIMIT))


def _rows(tm, w):
    return pl.BlockSpec((tm, w), lambda i: (i, 0))


def _full(*shape):
    return pl.BlockSpec(shape, lambda *_: (0,) * len(shape))


def _sds(shape, dtype=F32):
    return jax.ShapeDtypeStruct(shape, dtype)


def _row_tile(rows, cap=256):
    t = min(rows, cap)
    while rows % t or t % 8:
        t -= 1
    return t


def _dot(a, b):
    return jnp.dot(a, b, preferred_element_type=F32)


def _dot_nt(a, b):
    return lax.dot_general(a, b, (((1,), (1,)), ((), ())), preferred_element_type=F32)


def _dot_tn(a, b):
    return lax.dot_general(a, b, (((0,), (0,)), ((), ())), preferred_element_type=F32)


def _sigmoid(x):
    return 1.0 / (1.0 + jnp.exp(-x))


_GELU_C = math.sqrt(2.0 / math.pi)


def _gelu(x):
    return x * (0.5 * (1.0 + jnp.tanh(_GELU_C * (x + 0.044715 * (x * x * x)))))


def _gelu_grad(x):
    t = jnp.tanh(_GELU_C * (x + 0.044715 * (x * x * x)))
    cdf = 0.5 * (1.0 + t)
    return cdf + x * (0.5 * (1.0 - t * t) * _GELU_C * (1.0 + 3.0 * 0.044715 * (x * x)))


def _rms_scale(x):
    return lax.rsqrt(jnp.mean(x * x, axis=-1, keepdims=True) + EPS)


def _rms_bwd(x, rs, g, dy):
    gdy = dy * g
    dx = rs * gdy - x * ((rs * rs * rs) * jnp.mean(gdy * x, axis=-1, keepdims=True))
    return dx, dy * (x * rs)


def _log1p(e):
    u = 1.0 + e
    d = u - 1.0
    return jnp.where(d == 0.0, e, jnp.log(u) * (e / jnp.where(d == 0.0, 1.0, d)))


def _softplus(y):
    return jnp.maximum(y, 0.0) + _log1p(jnp.exp(-jnp.abs(y)))


def _expm1(x):
    u = jnp.exp(x)
    lu = jnp.log(u)
    safe = jnp.where((u == 1.0) | (u == 0.0), 1.0, lu)
    return jnp.where(u == 1.0, x, jnp.where(u == 0.0, -1.0, (u - 1.0) * (x / safe)))


def _row_iota(shape):
    return lax.broadcasted_iota(jnp.int32, shape, 0)


def _lane_iota(shape):
    return lax.broadcasted_iota(jnp.int32, shape, 1)


def _rope_pair(gc):
    return gc + pltpu.roll(gc, 64, 1)


def _inproj(x, g, w1, tm):
    t = x.shape[0]
    widths = (D, 3 * D, Q_LORA, KV_LORA, KR_W)

    def body(x_ref, g_ref, w_ref, xn_ref, rx_ref, g3_ref, cq_ref, ckv_ref, kr_ref):
        xv = x_ref[...]
        xn = (xv * _rms_scale(xv) * g_ref[...]).astype(BF16)
        xn_ref[...] = xn
        col = 0
        for ref, w in zip((rx_ref, g3_ref, cq_ref, ckv_ref, kr_ref), widths):
            for c0 in range(0, w, 512):
                cw = min(512, w - c0)
                ref[:, c0:c0 + cw] = _dot(xn, w_ref[:, col + c0:col + c0 + cw])
            col += w

    return _pcall(
        body, "inproj", (t // tm,),
        [_rows(tm, D), _full(1, D), _full(D, W1_COLS)],
        [_rows(tm, D)] + [_rows(tm, w) for w in widths],
        [_sds((t, D), BF16)] + [_sds((t, w)) for w in widths],
    )(x, g, w1)


def _lru_gates(xa, wa_ref, ba, wx_ref, bx, pre_r, pre_i):
    xb = xa.astype(BF16)
    for n in range(RNN_BLOCKS):
        sl = slice(n * RNN_BW, (n + 1) * RNN_BW)
        pre_r[:, sl] = _dot(xb[:, sl], wa_ref[n])
        pre_i[:, sl] = _dot(xb[:, sl], wx_ref[n])
    r = _sigmoid(pre_r[...] + ba)
    i = _sigmoid(pre_i[...] + bx)
    return r, i


def _lru_fwd(rx, conv_w, conv_b, wa, ba, wx, bx, lam, tb):
    t = rx.shape[0]
    nb = t // tb

    def body(x_ref, xp_ref, cw_ref, cb_ref, wa_ref, ba_ref, wx_ref, bx_ref, lam_ref, h_ref, xa_ref, hc, tmp, pre_r, pre_i):
        i_blk = pl.program_id(0)

        @pl.when(i_blk == 0)
        def _():
            hc[...] = jnp.zeros_like(hc)

        xv = x_ref[...]
        xp = jnp.where(i_blk > 0, xp_ref[...], 0.0)
        row8 = _row_iota((8, D))
        xa = cb_ref[...] + cw_ref[3:4, :] * xv
        for s in (1, 2, 3):
            xr = pltpu.roll(xv, s, 0)
            tmp[...] = xr
            tmp[0:8, :] = jnp.where(row8 < s, pltpu.roll(xp, s, 0), xr[0:8, :])
            xa = xa + cw_ref[3 - s:4 - s, :] * tmp[...]
        xa_ref[...] = xa
        r, gi = _lru_gates(xa, wa_ref, ba_ref[...], wx_ref, bx_ref[...], pre_r, pre_i)
        la = (-LRU_C * _softplus(-lam_ref[...])) * r
        a = jnp.exp(la)
        b = jnp.sqrt(-_expm1(2.0 * la)) * (gi * xa)
        row = _row_iota((tb, D))
        sh = 1
        while sh < tb:
            m = row >= sh
            b = jnp.where(m, a * pltpu.roll(b, sh, 0) + b, b)
            a = jnp.where(m, a * pltpu.roll(a, sh, 0), a)
            sh *= 2
        h = a * hc[...] + b
        h_ref[...] = h
        hc[...] = h[tb - 1:tb, :]

    prev8 = pl.BlockSpec((8, D), lambda i: (jnp.maximum(i * (tb // 8) - 1, 0), 0))
    return _pcall(
        body, "lru_fwd", (nb,),
        [_rows(tb, D), prev8, _full(4, D), _full(1, D), _full(RNN_BLOCKS, RNN_BW, RNN_BW), _full(1, D),
         _full(RNN_BLOCKS, RNN_BW, RNN_BW), _full(1, D), _full(1, D)],
        [_rows(tb, D), _rows(tb, D)],
        [_sds((t, D)), _sds((t, D))],
        scratch=[pltpu.VMEM((1, D), F32), pltpu.VMEM((tb, D), F32), pltpu.VMEM((tb, D), F32), pltpu.VMEM((tb, D), F32)],
    )(rx, rx, conv_w, conv_b, wa, ba, wx, bx, lam)


def _mla_proj(cq, ckv, kr, qn, kvn, wq, wkv, rope_c, tm):
    t = cq.shape[0]

    def body(cq_ref, ckv_ref, kr_ref, qn_ref, kvn_ref, wq_ref, wkv_ref, c_ref, q_ref, k_ref, v_ref, cqn_ref, ckvn_ref):
        cqv = cq_ref[...]
        cqn = (cqv * _rms_scale(cqv) * qn_ref[...]).astype(BF16)
        ckvv = ckv_ref[...]
        ckvn = (ckvv * _rms_scale(ckvv) * kvn_ref[...]).astype(BF16)
        cqn_ref[...] = cqn
        ckvn_ref[...] = ckvn
        c = c_ref[...]
        lane = _lane_iota((tm, KR_W))
        kro = jnp.where(lane < 64, _rope_pair(kr_ref[...] * c), 0.0).astype(BF16)
        for h in range(N_HEADS):
            sl = slice(h * HEAD_W, (h + 1) * HEAD_W)
            qh = _dot(cqn, wq_ref[:, sl])
            q_ref[h, :, 0:128] = qh[:, 0:128].astype(BF16)
            q_ref[h, :, 128:256] = _rope_pair(qh[:, 128:256] * c).astype(BF16)
            kvh = _dot(ckvn, wkv_ref[:, sl])
            k_ref[h, :, 0:128] = kvh[:, 0:128].astype(BF16)
            k_ref[h, :, 128:256] = kro
            v_ref[h] = kvh[:, 128:256].astype(BF16)

    hb = lambda w: pl.BlockSpec((N_HEADS, tm, w), lambda i: (0, i, 0))
    return _pcall(
        body, "mla_proj", (t // tm,),
        [_rows(tm, Q_LORA), _rows(tm, KV_LORA), _rows(tm, KR_W), _full(1, Q_LORA), _full(1, KV_LORA),
         _full(Q_LORA, N_HEADS * HEAD_W), _full(KV_LORA, N_HEADS * HEAD_W), _rows(tm, KR_W)],
        [hb(HEAD_W), hb(HEAD_W), hb(V_HEAD), _rows(tm, Q_LORA), _rows(tm, KV_LORA)],
        [_sds((N_HEADS, t, HEAD_W), BF16), _sds((N_HEADS, t, HEAD_W), BF16), _sds((N_HEADS, t, V_HEAD), BF16),
         _sds((t, Q_LORA), BF16), _sds((t, KV_LORA), BF16)],
    )(cq, ckv, kr, qn, kvn, wq, wkv, rope_c)


EXP2_SCALE = SM_SCALE * math.log2(math.e)


def _flash_fwd(q, k, v, tq):
    t = q.shape[1]
    nq = t // tq

    def body(q_ref, k_ref, v_ref, o_ref, lse_ref):
        qi = pl.program_id(1)
        qv = q_ref[0]

        def block(ki, carry, diagonal):
            m, l, acc = carry
            rows = pl.ds(pl.multiple_of(ki * tq, tq), tq)
            s = _dot_nt(qv, k_ref[0, rows, :])
            if diagonal:
                s = jnp.where(_row_iota((tq, tq)) >= _lane_iota((tq, tq)), s, NEG)
            m_new = jnp.maximum(m, jnp.max(s, axis=1, keepdims=True))
            p = jnp.exp2((s - m_new) * EXP2_SCALE)
            alpha = jnp.exp2((m - m_new) * EXP2_SCALE)
            l = alpha * l + jnp.sum(p, axis=1, keepdims=True)
            acc = alpha * acc + _dot(p.astype(BF16), v_ref[0, rows, :])
            return m_new, l, acc

        init = (jnp.full((tq, 1), -jnp.inf, F32), jnp.zeros((tq, 1), F32), jnp.zeros((tq, V_HEAD), F32))
        carry = lax.fori_loop(0, qi // 2, lambda i, c: block(2 * i + 1, block(2 * i, c, False), False), init)
        carry = lax.fori_loop(2 * (qi // 2), qi, lambda ki, c: block(ki, c, False), carry)
        m, l, acc = block(qi, carry, True)
        o_ref[...] = acc / l
        lse = jnp.broadcast_to(m * EXP2_SCALE + jnp.log(l) * math.log2(math.e), (tq, V_HEAD))
        lse_ref[0, 0] = jnp.transpose(lse)[0:8, :]

    head = lambda w: pl.BlockSpec((1, t, w), lambda h, qi: (h, 0, 0))
    return _pcall(
        body, "flash_fwd", (N_HEADS, nq),
        [pl.BlockSpec((1, tq, HEAD_W), lambda h, qi: (h, qi, 0)), head(HEAD_W), head(V_HEAD)],
        [pl.BlockSpec((tq, V_HEAD), lambda h, qi: (qi, h)), pl.BlockSpec((1, 1, 8, tq), lambda h, qi: (h, qi, 0, 0))],
        [_sds((t, D)), _sds((N_HEADS, nq, 8, tq))],
    )(q, k, v)


def _merge_out(x, h, g3, yb, w_out, tm):
    t = x.shape[0]

    def body(x_ref, h_ref, g3_ref, yb_ref, w_ref, h1_ref, mg_ref):
        ya = h_ref[...] * _gelu(g3_ref[:, 0:D])
        merged = (_sigmoid(g3_ref[:, D:2 * D]) * ya + _sigmoid(g3_ref[:, 2 * D:3 * D]) * yb_ref[...]).astype(BF16)
        mg_ref[...] = merged
        h1_ref[...] = x_ref[...] + _dot(merged, w_ref[...])

    return _pcall(
        body, "merge_out", (t // tm,),
        [_rows(tm, D), _rows(tm, D), _rows(tm, 3 * D), _rows(tm, D), _full(D, D)],
        [_rows(tm, D), _rows(tm, D)],
        [_sds((t, D)), _sds((t, D), BF16)],
    )(x, h, g3, yb, w_out)


def _mlp_up(h1, g, w_up, tm):
    t = h1.shape[0]

    def body(h_ref, g_ref, w_ref, u_ref, n2_ref):
        hv = h_ref[...]
        n2 = (hv * _rms_scale(hv) * g_ref[...]).astype(BF16)
        n2_ref[...] = n2
        for c0 in range(0, D_FF, 512):
            u_ref[:, c0:c0 + 512] = _dot(n2, w_ref[:, c0:c0 + 512])

    return _pcall(
        body, "mlp_up", (t // tm,),
        [_rows(tm, D), _full(1, D), _full(D, D_FF)],
        [_rows(tm, D_FF), _rows(tm, D)],
        [_sds((t, D_FF)), _sds((t, D), BF16)],
    )(h1, g, w_up)


def _mlp_down_loss(u, h1, target, w_down, g, tm):
    t = u.shape[0]

    def body(u_ref, h1_ref, tg_ref, w_ref, g_ref, act_ref, dh2_ref, loss_ref, gnf_ref, lacc):
        i = pl.program_id(0)

        @pl.when(i == 0)
        def _():
            lacc[...] = jnp.zeros_like(lacc)
            gnf_ref[...] = jnp.zeros_like(gnf_ref)

        ru = jnp.maximum(u_ref[...], 0.0)
        act = (ru * ru).astype(BF16)
        act_ref[...] = act
        h2 = h1_ref[...] + _dot(act, w_ref[...])
        rs = _rms_scale(h2)
        gv = g_ref[...]
        err = h2 * rs * gv - tg_ref[...]
        lacc[...] += jnp.sum(err * err, axis=0, keepdims=True)
        dx, dgr = _rms_bwd(h2, rs, gv, err * (1.0 / D))
        dh2_ref[...] = dx
        gnf_ref[...] += jnp.sum(dgr, axis=0, keepdims=True)

        @pl.when(i == pl.num_programs(0) - 1)
        def _():
            loss_ref[...] = jnp.broadcast_to(jnp.sum(lacc[...], axis=1, keepdims=True) * (0.5 / D), (8, 128))

    return _pcall(
        body, "mlp_down_loss", (t // tm,),
        [_rows(tm, D_FF), _rows(tm, D), _rows(tm, D), _full(D_FF, D), _full(1, D)],
        [_rows(tm, D_FF), _rows(tm, D), _full(8, 128), _full(1, D)],
        [_sds((t, D_FF), BF16), _sds((t, D)), _sds((8, 128)), _sds((1, D))],
        scratch=[pltpu.VMEM((1, D), F32)],
    )(u, h1, target, w_down, g)


def _matmul_tn(a, g, name):
    t, kdim = a.shape
    ndim = g.shape[1]
    tk = min(kdim, 1024)
    tn = ndim if ndim <= 1024 else 1024
    tt = min(t, 512)
    nt = t // tt

    def body(a_ref, g_ref, o_ref):
        @pl.when(pl.program_id(2) == 0)
        def _():
            o_ref[...] = jnp.zeros_like(o_ref)

        o_ref[...] += _dot_tn(a_ref[...].astype(BF16), g_ref[...].astype(BF16))

    return _pcall(
        body, name, (kdim // tk, ndim // tn, nt),
        [pl.BlockSpec((tt, tk), lambda i, j, s: (s, i)), pl.BlockSpec((tt, tn), lambda i, j, s: (s, j))],
        pl.BlockSpec((tk, tn), lambda i, j, s: (i, j)),
        _sds((kdim, ndim)),
    )(a, g)


def _mlp_bwd_act(dh2, u, w_down, tm):
    t = u.shape[0]

    def body(d_ref, u_ref, w_ref, du_ref):
        db = d_ref[...].astype(BF16)
        for c0 in range(0, D_FF, 512):
            da = _dot_nt(db, w_ref[c0:c0 + 512, :])
            du_ref[:, c0:c0 + 512] = (da * (2.0 * jnp.maximum(u_ref[:, c0:c0 + 512], 0.0))).astype(BF16)

    return _pcall(
        body, "mlp_bwd_act", (t // tm,),
        [_rows(tm, D), _rows(tm, D_FF), _full(D_FF, D)],
        _rows(tm, D_FF), _sds((t, D_FF), BF16),
    )(dh2, u, w_down)


def _mlp_bwd_in(du, dh2, h1, w_up, g, tm):
    t = du.shape[0]

    def body(du_ref, d_ref, h_ref, w_ref, g_ref, dh1_ref, gacc_ref):
        @pl.when(pl.program_id(0) == 0)
        def _():
            gacc_ref[...] = jnp.zeros_like(gacc_ref)

        dn2 = _dot_nt(du_ref[...], w_ref[...])
        hv = h_ref[...]
        dx, dgr = _rms_bwd(hv, _rms_scale(hv), g_ref[...], dn2)
        dh1_ref[...] = d_ref[...] + dx
        gacc_ref[...] += jnp.sum(dgr, axis=0, keepdims=True)

    return _pcall(
        body, "mlp_bwd_in", (t // tm,),
        [_rows(tm, D_FF), _rows(tm, D), _rows(tm, D), _full(D, D_FF), _full(1, D)],
        [_rows(tm, D), _full(1, D)],
        [_sds((t, D)), _sds((1, D))],
    )(du, dh2, h1, w_up, g)


def _merge_bwd(dh1, w_out, g3, h, yb, merged, tm):
    t = dh1.shape[0]

    def body(d_ref, w_ref, g3_ref, h_ref, yb_ref, mg_ref, dg3_ref, dyb_ref, dl_ref, dh_ref, dwo_ref):
        @pl.when(pl.program_id(0) == 0)
        def _():
            dwo_ref[...] = jnp.zeros_like(dwo_ref)

        db = d_ref[...].astype(BF16)
        dwo_ref[...] += _dot_tn(mg_ref[...], db)
        dm = _dot_nt(db, w_ref[...])
        gv = g3_ref[:, 0:D]
        sa = _sigmoid(g3_ref[:, D:2 * D])
        sb = _sigmoid(g3_ref[:, 2 * D:3 * D])
        gel = _gelu(gv)
        hv = h_ref[...]
        ybv = yb_ref[...]
        dya = dm * sa
        dyb = dm * sb
        dg3_ref[:, 0:D] = (dya * hv * _gelu_grad(gv)).astype(BF16)
        dg3_ref[:, D:2 * D] = (dya * (hv * gel) * (1.0 - sa)).astype(BF16)
        dg3_ref[:, 2 * D:3 * D] = (dyb * ybv * (1.0 - sb)).astype(BF16)
        dh_ref[...] = dya * gel
        dyb_ref[...] = dyb.astype(BF16)
        prod = dyb * ybv
        ones = jnp.ones((8, V_HEAD), F32)
        for hh in range(N_HEADS):
            dl_ref[hh] = lax.dot_general(ones, prod[:, hh * V_HEAD:(hh + 1) * V_HEAD], (((1,), (1,)), ((), ())),
                                         precision=lax.Precision.HIGHEST, preferred_element_type=F32)

    return _pcall(
        body, "merge_bwd", (t // tm,),
        [_rows(tm, D), _full(D, D), _rows(tm, 3 * D), _rows(tm, D), _rows(tm, D), _rows(tm, D)],
        [_rows(tm, 3 * D), _rows(tm, D), pl.BlockSpec((N_HEADS, 8, tm), lambda i: (0, 0, i)), _rows(tm, D), _full(D, D)],
        [_sds((t, 3 * D), BF16), _sds((t, D), BF16), _sds((N_HEADS, 8, t)), _sds((t, D)), _sds((D, D))],
    )(dh1, w_out, g3, h, yb, merged)


def _flash_bwd(q, k, v, do, lse, delta, tq):
    t = q.shape[1]
    nq = t // tq

    def body(q_ref, k_ref, v_ref, do_ref, lse_ref, dl_ref, dqt_ref, dk_ref, dv_ref):
        ki = pl.program_id(1)

        @pl.when(ki == 0)
        def _():
            dqt_ref[...] = jnp.zeros_like(dqt_ref)

        kblk, vblk = k_ref[0], v_ref[0]
        kt = jnp.transpose(kblk)
        dk_ref[...] = jnp.zeros_like(dk_ref)
        dv_ref[...] = jnp.zeros_like(dv_ref)

        def block(qi, diagonal):
            rows = pl.ds(pl.multiple_of(qi * tq, tq), tq)
            qv, dov = q_ref[0, rows, :], do_ref[rows, :]
            p = jnp.exp2(_dot_nt(kblk, qv) * EXP2_SCALE - lse_ref[0, qi, 0:1, :])
            if diagonal:
                p = jnp.where(_lane_iota((tq, tq)) >= _row_iota((tq, tq)), p, 0.0)
            dv_ref[0] += _dot(p.astype(BF16), dov)
            dp = _dot_nt(vblk, dov)
            ds = (p * (dp - dl_ref[0, qi, 0:1, :]) * SM_SCALE).astype(BF16)
            dk_ref[0] += _dot(ds, qv)
            dqt_ref[0, qi] += _dot(kt, ds)

        block(ki, True)

        def two(i, carry):
            block(ki + 1 + 2 * i, False)
            block(ki + 2 + 2 * i, False)
            return carry

        def one(qi, carry):
            block(qi, False)
            return carry

        pairs = (nq - 1 - ki) // 2
        lax.fori_loop(0, pairs, two, 0)
        lax.fori_loop(ki + 1 + 2 * pairs, nq, one, 0)

    kv_spec = lambda w: pl.BlockSpec((1, tq, w), lambda h, ki: (h, ki, 0))
    stat = pl.BlockSpec((1, nq, 8, tq), lambda h, ki: (h, 0, 0, 0))
    return _pcall(
        body, "flash_bwd", (N_HEADS, nq),
        [pl.BlockSpec((1, t, HEAD_W), lambda h, ki: (h, 0, 0)), kv_spec(HEAD_W), kv_spec(V_HEAD),
         pl.BlockSpec((t, V_HEAD), lambda h, ki: (0, h)), stat, stat],
        [pl.BlockSpec((1, nq, HEAD_W, tq), lambda h, ki: (h, 0, 0, 0)), kv_spec(HEAD_W), kv_spec(V_HEAD)],
        [_sds((N_HEADS, nq, HEAD_W, tq)), _sds((N_HEADS, t, HEAD_W)), _sds((N_HEADS, t, V_HEAD))],
    )(q, k, v, do, lse, delta)


def _mla_bwd(dqt, dk, dv, cqn, ckvn, cq, ckv, rope_c, wq, wkv, qn, kvn, tm):
    t = cq.shape[0]

    def body(dq_ref, dk_ref, dv_ref, cqn_ref, ckvn_ref, cq_ref, ckv_ref, c_ref, wq_ref, wkv_ref, qn_ref, kvn_ref,
             dmla_ref, dwq_ref, dwkv_ref, dqn_ref, dkvn_ref):
        @pl.when(pl.program_id(0) == 0)
        def _():
            dwq_ref[...] = jnp.zeros_like(dwq_ref)
            dwkv_ref[...] = jnp.zeros_like(dwkv_ref)
            dqn_ref[...] = jnp.zeros_like(dqn_ref)
            dkvn_ref[...] = jnp.zeros_like(dkvn_ref)

        c = c_ref[...]
        lane = _lane_iota((tm, KR_W))
        cqn, ckvn = cqn_ref[...], ckvn_ref[...]
        dcqn = jnp.zeros((tm, Q_LORA), F32)
        dckvn = jnp.zeros((tm, KV_LORA), F32)
        dkr = jnp.zeros((tm, KR_W), F32)
        for h in range(N_HEADS):
            sl = slice(h * HEAD_W, (h + 1) * HEAD_W)
            dqh = jnp.transpose(dq_ref[h, 0])
            droped = jnp.where(lane < 64, dqh[:, 128:256], 0.0)
            dqp = jnp.concatenate([dqh[:, 0:128], _rope_pair(droped) * c], axis=1).astype(BF16)
            dcqn = dcqn + _dot_nt(dqp, wq_ref[:, sl])
            dwq_ref[:, sl] += _dot_tn(cqn, dqp)
            dkr = dkr + jnp.where(lane < 64, dk_ref[h, :, 128:256], 0.0)
            dkvp = jnp.concatenate([dk_ref[h, :, 0:128], dv_ref[h]], axis=1).astype(BF16)
            dckvn = dckvn + _dot_nt(dkvp, wkv_ref[:, sl])
            dwkv_ref[:, sl] += _dot_tn(ckvn, dkvp)
        cqv, ckvv = cq_ref[...], ckv_ref[...]
        dcq, dgq = _rms_bwd(cqv, _rms_scale(cqv), qn_ref[...], dcqn)
        dckv, dgkv = _rms_bwd(ckvv, _rms_scale(ckvv), kvn_ref[...], dckvn)
        dqn_ref[...] += jnp.sum(dgq, axis=0, keepdims=True)
        dkvn_ref[...] += jnp.sum(dgkv, axis=0, keepdims=True)
        dmla_ref[:, 0:256] = dcq.astype(BF16)
        dmla_ref[:, 256:512] = dckv.astype(BF16)
        dmla_ref[:, 512:640] = (_rope_pair(dkr) * c).astype(BF16)

    hb = lambda w: pl.BlockSpec((N_HEADS, tm, w), lambda i: (0, i, 0))
    wide = N_HEADS * HEAD_W
    per_q = dqt.shape[3] // tm
    dq_spec = pl.BlockSpec((N_HEADS, 1, HEAD_W, tm), lambda i: (0, i // per_q, 0, i % per_q))
    return _pcall(
        body, "mla_bwd", (t // tm,),
        [dq_spec, hb(HEAD_W), hb(V_HEAD), _rows(tm, Q_LORA), _rows(tm, KV_LORA), _rows(tm, Q_LORA), _rows(tm, KV_LORA),
         _rows(tm, KR_W), _full(Q_LORA, wide), _full(KV_LORA, wide), _full(1, Q_LORA), _full(1, KV_LORA)],
        [_rows(tm, 640), _full(Q_LORA, wide), _full(KV_LORA, wide), _full(1, Q_LORA), _full(1, KV_LORA)],
        [_sds((t, 640), BF16), _sds((Q_LORA, wide)), _sds((KV_LORA, wide)), _sds((1, Q_LORA)), _sds((1, KV_LORA))],
    )(dqt, dk, dv, cqn, ckvn, cq, ckv, rope_c, wq, wkv, qn, kvn)


def _lru_bwd(dh, xa, h, rx, conv_w, wa, ba, wx, bx, lam, tb):
    t = dh.shape[0]
    nb = t // tb

    def body(dh_ref, xa_ref, h_ref, hp_ref, x_ref, cw_ref, wa_ref, ba_ref, wx_ref, bx_ref, lam_ref,
             drx_ref, dcw_ref, dcb_ref, dwa_ref, dba_ref, dwx_ref, dbx_ref, dlam_ref, gc, dxn, tmp, pre_r, pre_i):
        step = pl.program_id(0)
        first_block = step == nb - 1

        @pl.when(step == 0)
        def _():
            gc[...] = jnp.zeros_like(gc)
            dxn[...] = jnp.zeros_like(dxn)
            for ref in (dcw_ref, dcb_ref, dwa_ref, dba_ref, dwx_ref, dbx_ref, dlam_ref):
                ref[...] = jnp.zeros_like(ref)

        xa = xa_ref[...]
        r, gi = _lru_gates(xa, wa_ref, ba_ref[...], wx_ref, bx_ref[...], pre_r, pre_i)
        lamv = lam_ref[...]
        sp = _softplus(-lamv)
        la = (-LRU_C * sp) * r
        a = jnp.exp(la)
        e2 = _expm1(2.0 * la)
        sq = jnp.sqrt(-e2)
        row = _row_iota((tb, D))
        cf = jnp.where(row == tb - 1, 1.0, pltpu.roll(a, tb - 1, 0))
        bv = dh_ref[...]
        sh = 1
        while sh < tb:
            m = row < tb - sh
            bv = jnp.where(m, bv + cf * pltpu.roll(bv, tb - sh, 0), bv)
            cf = jnp.where(m, cf * pltpu.roll(cf, tb - sh, 0), cf)
            sh *= 2
        delta = bv + cf * gc[...]
        gc[...] = a[0:1, :] * delta[0:1, :]
        hv = h_ref[...]
        hr = pltpu.roll(hv, 1, 0)
        tmp[...] = hr
        tmp[0:1, :] = jnp.where(first_block, 0.0, hp_ref[7:8, :])
        hprev = tmp[...]
        ix = gi * xa
        dla = (delta * hprev) * a - (delta * ix) * ((e2 + 1.0) / sq)
        dlam_ref[...] += jnp.sum(dla * r, axis=0, keepdims=True) * (LRU_C * _sigmoid(-lamv))
        dpr = (dla * (-LRU_C * sp)) * r * (1.0 - r)
        dsq = delta * sq
        dpi = (dsq * xa) * gi * (1.0 - gi)
        dba_ref[...] += jnp.sum(dpr, axis=0, keepdims=True)
        dbx_ref[...] += jnp.sum(dpi, axis=0, keepdims=True)
        pre_r[...] = dpr
        pre_i[...] = dpi
        xb = xa.astype(BF16)
        for n in range(RNN_BLOCKS):
            sl = slice(n * RNN_BW, (n + 1) * RNN_BW)
            dprn = pre_r[:, sl].astype(BF16)
            dpin = pre_i[:, sl].astype(BF16)
            dwa_ref[n] += _dot_tn(xb[:, sl], dprn)
            dwx_ref[n] += _dot_tn(xb[:, sl], dpin)
            tmp[:, sl] = _dot_nt(dprn, wa_ref[n]) + _dot_nt(dpin, wx_ref[n])
        dxa = dsq * gi + tmp[...]
        dcb_ref[...] += jnp.sum(dxa, axis=0, keepdims=True)
        xv = x_ref[...]
        drx = cw_ref[3:4, :] * dxa
        dcw_ref[3:4, :] += jnp.sum(dxa * xv, axis=0, keepdims=True)
        row8 = _row_iota((8, D))
        nxt = dxn[...]
        for s in (1, 2, 3):
            dr_ = pltpu.roll(dxa, tb - s, 0)
            tmp[...] = dr_
            tmp[tb - 8:tb, :] = jnp.where(row8 >= 8 - s, pltpu.roll(nxt, 8 - s, 0), dr_[tb - 8:tb, :])
            dxs = tmp[...]
            drx = drx + cw_ref[3 - s:4 - s, :] * dxs
            dcw_ref[3 - s:4 - s, :] += jnp.sum(dxs * xv, axis=0, keepdims=True)
        drx_ref[...] = drx.astype(BF16)
        dxn[...] = dxa[0:8, :]

    rev = pl.BlockSpec((tb, D), lambda i: (nb - 1 - i, 0))
    prev8 = pl.BlockSpec((8, D), lambda i: (jnp.maximum((nb - 1 - i) * (tb // 8) - 1, 0), 0))
    wblk = _full(RNN_BLOCKS, RNN_BW, RNN_BW)
    return _pcall(
        body, "lru_bwd", (nb,),
        [rev, rev, rev, prev8, rev, _full(4, D), wblk, _full(1, D), wblk, _full(1, D), _full(1, D)],
        [rev, _full(4, D), _full(1, D), wblk, _full(1, D), wblk, _full(1, D), _full(1, D)],
        [_sds((t, D), BF16), _sds((4, D)), _sds((1, D)), _sds((RNN_BLOCKS, RNN_BW, RNN_BW)), _sds((1, D)),
         _sds((RNN_BLOCKS, RNN_BW, RNN_BW)), _sds((1, D)), _sds((1, D))],
        scratch=[pltpu.VMEM((1, D), F32), pltpu.VMEM((8, D), F32), pltpu.VMEM((tb, D), F32), pltpu.VMEM((tb, D), F32),
                 pltpu.VMEM((tb, D), F32)],
    )(dh, xa, h, h, rx, conv_w, wa, ba, wx, bx, lam)


def _inproj_bwd(x, dh1, drx, dg3, dmla, w1, g, tm):
    t = x.shape[0]

    def body(x_ref, d_ref, drx_ref, dg3_ref, dmla_ref, w_ref, g_ref, dx_ref, gacc_ref):
        @pl.when(pl.program_id(0) == 0)
        def _():
            gacc_ref[...] = jnp.zeros_like(gacc_ref)

        dxn = _dot_nt(drx_ref[...], w_ref[:, 0:D])
        for c0 in range(0, 3 * D, D):
            dxn = dxn + _dot_nt(dg3_ref[:, c0:c0 + D], w_ref[:, D + c0:2 * D + c0])
        dxn = dxn + _dot_nt(dmla_ref[...], w_ref[:, 4 * D:W1_COLS])
        xv = x_ref[...]
        dx, dgr = _rms_bwd(xv, _rms_scale(xv), g_ref[...], dxn)
        dx_ref[...] = d_ref[...] + dx
        gacc_ref[...] += jnp.sum(dgr, axis=0, keepdims=True)

    return _pcall(
        body, "inproj_bwd", (t // tm,),
        [_rows(tm, D), _rows(tm, D), _rows(tm, D), _rows(tm, 3 * D), _rows(tm, 640), _full(D, W1_COLS), _full(1, D)],
        [_rows(tm, D), _full(1, D)],
        [_sds((t, D)), _sds((1, D))],
    )(x, dh1, drx, dg3, dmla, w1, g)


def _pcall_indexed(body, name, index, grid, in_specs, out_specs, out_shape):
    call = pl.pallas_call(
        body, name=name, out_shape=out_shape,
        grid_spec=pltpu.PrefetchScalarGridSpec(num_scalar_prefetch=1, grid=grid, in_specs=in_specs, out_specs=out_specs),
        compiler_params=pltpu.CompilerParams(dimension_semantics=("arbitrary",) * len(grid), vmem_limit_bytes=V7X_VMEM_LIMIT))
    return lambda *operands: call(index, *operands)


def _pair_sum(halves, theirs, core, out_dtype, name):
    _, rows, cols = halves.shape
    tm = _row_tile(rows)

    def body(c_ref, a_ref, b_ref, o_ref):
        o_ref[...] = (a_ref[0] + b_ref[...]).astype(out_dtype)

    plain = pl.BlockSpec((tm, cols), lambda i, c: (i, 0))
    return _pcall_indexed(body, name, core, (rows // tm,),
                          [pl.BlockSpec((1, tm, cols), lambda i, c: (c[0], i, 0)), plain], plain,
                          _sds((rows, cols), out_dtype))(halves, theirs)


def _chip_sum(parts, recv, chip, name):
    _, rows, cols = parts.shape
    tm = _row_tile(rows)

    def body(c_ref, a_ref, r_ref, o_ref):
        o_ref[...] = ((a_ref[0].astype(F32) + r_ref[0].astype(F32)) + r_ref[1].astype(F32)) + r_ref[2].astype(F32)

    return _pcall_indexed(body, name, chip, (rows // tm,),
                          [pl.BlockSpec((1, tm, cols), lambda i, c: (c[0], i, 0)),
                           pl.BlockSpec((N_CHIPS - 1, tm, cols), lambda i, c: (0, i, 0))],
                          pl.BlockSpec((tm, cols), lambda i, c: (i, 0)), _sds((rows, cols)))(parts, recv)


def _adam_math(w, gv, m, v):
    mn = ADAM_B1 * m + (1.0 - ADAM_B1) * gv
    vn = ADAM_B2 * v + (1.0 - ADAM_B2) * (gv * gv)
    m_hat = mn / (1.0 - ADAM_B1 ** ADAM_STEP)
    v_hat = vn / (1.0 - ADAM_B2 ** ADAM_STEP)
    return -ADAM_LR * (m_hat / (jnp.sqrt(v_hat) + ADAM_EPS) + ADAM_WD * w), mn, vn


def _adamw(w, g, m, v, name):
    rows, cols = w.shape
    tm = _row_tile(rows)

    def body(w_ref, g_ref, m_ref, v_ref, d_ref, mo_ref, vo_ref):
        d_ref[...], mo_ref[...], vo_ref[...] = _adam_math(w_ref[...], g_ref[...], m_ref[...], v_ref[...])

    spec = _rows(tm, cols)
    return _pcall(body, name, (rows // tm,), [spec] * 4, [spec] * 3, [_sds((rows, cols))] * 3)(w, g, m, v)


def _adamw_halves(w, mine, theirs, m, v, core, name):
    rows, cols = w.shape
    tm = _row_tile(rows // 2)
    nh = rows // 2 // tm

    def body(c_ref, w_ref, a_ref, b_ref, m_ref, v_ref, g_ref, d_ref, mo_ref, vo_ref):
        gv = jnp.where(pl.program_id(0) // nh == c_ref[0], a_ref[...], b_ref[...])
        g_ref[...] = gv
        d_ref[...], mo_ref[...], vo_ref[...] = _adam_math(w_ref[...], gv, m_ref[...], v_ref[...])

    full = pl.BlockSpec((tm, cols), lambda i, c: (i, 0))
    half = pl.BlockSpec((tm, cols), lambda i, c: (i % nh, 0))
    return _pcall_indexed(body, name, core, (rows // tm,), [full, half, half, full, full], [full] * 4,
                          [_sds((rows, cols))] * 4)(w, mine, theirs, m, v)


REL_SIBLING = (0, 0, 1)
REL_CHIPS = ((1, 0, 0), (0, 1, 0), (1, 1, 0))


V7X_DMA_CHUNK_BYTES = 1 << 20


def _split_copy(src, dst, shape, itemsize):
    nbytes = math.prod(shape) * itemsize
    if nbytes <= V7X_DMA_CHUNK_BYTES or len(shape) < 2:
        return [(src, dst)]
    if len(shape) > 2:
        out = []
        for k in range(shape[0]):
            out += _split_copy(src.at[k], dst.at[k], shape[1:], itemsize)
        return out
    rows = shape[0]
    sub = 8 * (4 // itemsize)
    parts = max(1, min(-(-nbytes // V7X_DMA_CHUNK_BYTES), rows // sub))
    while rows % parts or (rows // parts) % sub:
        parts -= 1
    step = rows // parts
    return [(src.at[pl.ds(k * step, step)], dst.at[pl.ds(k * step, step)]) for k in range(parts)]


def _comm(name, ins, out_shapes, n_ops, ops_fn):
    n_in, n_out = len(ins), len(out_shapes)

    def body(*refs):
        in_refs, out_refs = refs[:n_in], refs[n_in:n_in + n_out]
        send_sems, recv_sems = refs[n_in + n_out:]
        pos = (lax.axis_index("x"), lax.axis_index("y"), lax.axis_index("c"))
        ops = ops_fn(in_refs, out_refs, pos)
        assert len(ops) == n_ops

        def make(i, rel, src, dst):
            if rel is None:
                return pltpu.make_async_copy(src, dst, send_sems.at[i])
            peer = tuple((p + r) % 2 for p, r in zip(pos, rel))
            return pltpu.make_async_remote_copy(src_ref=src, dst_ref=dst, send_sem=send_sems.at[i], recv_sem=recv_sems.at[i],
                                                device_id=peer, device_id_type=MESH_ID)

        for i, (rel, src, dst) in enumerate(ops):
            for s_piece, d_piece in _split_copy(src, dst, src.shape, jnp.dtype(src.dtype).itemsize):
                make(i, rel, s_piece, d_piece).start()
        for i, (rel, src, dst) in enumerate(ops):
            make(i, rel, src, dst).wait()

    hbm = pl.BlockSpec(memory_space=pl.ANY)
    return pl.pallas_call(
        body, name=name, in_specs=[hbm] * n_in, out_specs=[hbm] * n_out, out_shape=list(out_shapes),
        scratch_shapes=[pltpu.SemaphoreType.DMA((n_ops,)), pltpu.SemaphoreType.DMA((n_ops,))],
    )(*ins)


def _chip_of(pos, rel=(0, 0, 0)):
    return 2 * ((pos[0] + rel[0]) % 2) + (pos[1] + rel[1]) % 2


def _gather_chips(shards, chip, name):
    def ops_fn(in_refs, out_refs, pos):
        me = _chip_of(pos)
        return [(rel, src, dst.at[me]) for src, dst in zip(in_refs, out_refs) for rel in REL_CHIPS]

    outs = _comm(name, shards, [_sds((N_CHIPS,) + s.shape, s.dtype) for s in shards], 3 * len(shards), ops_fn)
    return [lax.dynamic_update_index_in_dim(o, s, chip, 0) for o, s in zip(outs, shards)]


def _sibling_split(gs):
    def ops_fn(in_refs, out_refs, pos):
        return [(REL_SIBLING, src.at[1 - pos[2]], dst) for src, dst in zip(in_refs, out_refs)]

    return _comm("grad_sibling_split", gs, [_sds(g.shape[1:], g.dtype) for g in gs], len(gs), ops_fn)


def _chip_exchange(ps):
    def ops_fn(in_refs, out_refs, pos):
        return [(rel, src.at[_chip_of(pos, rel)], dst.at[j])
                for src, dst in zip(in_refs, out_refs) for j, rel in enumerate(REL_CHIPS)]

    return _comm("grad_chip_exchange", ps, [_sds((N_CHIPS - 1,) + p.shape[1:], p.dtype) for p in ps], 3 * len(ps), ops_fn)


def _sibling_join(hs):
    def ops_fn(in_refs, out_refs, pos):
        return [(REL_SIBLING, src, dst) for src, dst in zip(in_refs, out_refs)]

    return _comm("grad_sibling_join", hs, [_sds(h.shape, h.dtype) for h in hs], len(hs), ops_fn)


def _rot_cols(w):
    return jnp.concatenate([-w[..., 32:], w[..., :32]], axis=-1)


def _unrot_cols(dw):
    return jnp.concatenate([dw[..., 32:], -dw[..., :32]], axis=-1)


IN_OFFS = (0, 1024, 2048, 2304, 2560, 2624, 3648, 4672)


def _w1_from_w_in(w):
    seg = [w[:, IN_OFFS[i]:IN_OFFS[i + 1]] for i in range(7)]
    rnn_x, rnn_gate, cq, ckv, kr, ga, gb = seg
    return jnp.concatenate([rnn_x, rnn_gate, ga, gb, cq, ckv, kr, _rot_cols(kr)], axis=1)


def _w_in_grad_from_parts(d_rx, d_g3, d_mla):
    kr = d_mla[:, 512:576] + _unrot_cols(d_mla[:, 576:640])
    return jnp.concatenate([d_rx, d_g3[:, 0:D], d_mla[:, 0:512], kr, d_g3[:, D:3 * D]], axis=1)


def _wq_from_w_uq(w):
    w3 = w.reshape(Q_LORA, N_HEADS, QK_NOPE + QK_ROPE)
    rope = w3[..., QK_NOPE:]
    return jnp.concatenate([w3[..., :QK_NOPE], rope, _rot_cols(rope)], axis=-1).reshape(Q_LORA, N_HEADS * HEAD_W)


def _w_uq_grad_from_wq(dw):
    d3 = dw.reshape(Q_LORA, N_HEADS, HEAD_W)
    rope = d3[..., 128:192] + _unrot_cols(d3[..., 192:256])
    return jnp.concatenate([d3[..., :128], rope], axis=-1).reshape(Q_LORA, N_HEADS * (QK_NOPE + QK_ROPE))


def _cols_from_chunks(g):
    return g.transpose(1, 0, 2).reshape(g.shape[1], N_CHIPS * g.shape[2])


def _halves_of_col_chunks(dw):
    r, c4 = dw.shape
    return dw.reshape(2, r // 2, N_CHIPS, c4 // N_CHIPS).transpose(0, 2, 1, 3)


def _halves_of_row_chunks(dw):
    r4, c = dw.shape
    return dw.reshape(N_CHIPS, 2, r4 // (2 * N_CHIPS), c).transpose(1, 0, 2, 3)


def kernel(x, norm_mix, w_in, conv_w, conv_b, lru_wa, lru_ba, lru_wx, lru_bx, lru_lambda, q_norm, w_uq, kv_norm, w_ukv, w_out, norm_mlp, w_up, w_down, norm_final, loss_target, m_norm_mix, m_w_in, m_conv_w, m_conv_b, m_lru_wa, m_lru_ba, m_lru_wx, m_lru_bx, m_lru_lambda, m_q_norm, m_w_uq, m_kv_norm, m_w_ukv, m_w_out, m_norm_mlp, m_w_up, m_w_down, m_norm_final, v_norm_mix, v_w_in, v_conv_w, v_conv_b, v_lru_wa, v_lru_ba, v_lru_wx, v_lru_bx, v_lru_lambda, v_q_norm, v_w_uq, v_kv_norm, v_w_ukv, v_w_out, v_norm_mlp, v_w_up, v_w_down, v_norm_final):
    t = x.shape[1]
    tm = min(256, t)
    tq = min(512, max(tm, t // 4))
    x2 = x[0]
    target = loss_target[0]
    chip = 2 * lax.axis_index("x") + lax.axis_index("y")
    core = lax.axis_index("c")
    chip_ix, core_ix = chip.reshape(1).astype(jnp.int32), core.reshape(1).astype(jnp.int32)
    row = lambda p: p.reshape(1, -1)

    big_shards = (w_in, w_uq, w_ukv, w_out, w_up, w_down)
    gathered = _gather_chips([w.astype(BF16) for w in big_shards] + [conv_w], chip, "weight_gather")
    w1 = _w1_from_w_in(_cols_from_chunks(gathered[0]))
    wq = _wq_from_w_uq(_cols_from_chunks(gathered[1]))
    wkv = _cols_from_chunks(gathered[2])
    w_out_f = gathered[3].reshape(D, D)
    w_up_f = _cols_from_chunks(gathered[4])
    w_down_f = gathered[5].reshape(D_FF, D)
    conv_w_f = _cols_from_chunks(gathered[6])
    wa_b, wx_b = lru_wa.astype(BF16), lru_wx.astype(BF16)

    pos = jnp.arange(t, dtype=F32)
    inv_freq = 1.0 / (ROPE_THETA ** (jnp.arange(0, QK_ROPE, 2, dtype=F32) / QK_ROPE))
    ang = pos[:, None] * inv_freq[None, :]
    rope_c = jnp.concatenate([jnp.cos(ang), jnp.cos(ang), jnp.sin(ang), jnp.sin(ang)], axis=-1)

    xn, rx, g3, cq, ckv, kr = _inproj(x2, row(norm_mix), w1, tm)
    h, xa = _lru_fwd(rx, conv_w_f, row(conv_b), wa_b, row(lru_ba), wx_b, row(lru_bx), row(lru_lambda), tm)
    q, k, v, cqn, ckvn = _mla_proj(cq, ckv, kr, row(q_norm), row(kv_norm), wq, wkv, rope_c, tm)
    nq = t // tq
    yb, lse = _flash_fwd(q, k, v, tq)
    h1, merged = _merge_out(x2, h, g3, yb, w_out_f, tm)
    u, n2 = _mlp_up(h1, row(norm_mlp), w_up_f, tm)
    act, dh2, loss_blk, g_norm_final = _mlp_down_loss(u, h1, target, w_down_f, row(norm_final), tm)

    g_w_down = _matmul_tn(act, dh2, "grad_w_down")
    du = _mlp_bwd_act(dh2, u, w_down_f, tm)
    dh1, g_norm_mlp = _mlp_bwd_in(du, dh2, h1, w_up_f, row(norm_mlp), tm)
    g_w_up = _matmul_tn(n2, du, "grad_w_up")
    dg3, dyb, delta, dh, g_w_out = _merge_bwd(dh1, w_out_f, g3, h, yb, merged, tm)
    delta = delta.reshape(N_HEADS, 8, nq, tq).swapaxes(1, 2)
    dq, dk, dv = _flash_bwd(q, k, v, dyb, lse, delta, tq)
    dmla, g_wq, g_wkv, g_q_norm, g_kv_norm = _mla_bwd(dq, dk, dv, cqn, ckvn, cq, ckv, rope_c, wq, wkv, row(q_norm), row(kv_norm), tm)
    drx, g_conv_w, g_conv_b, g_wa, g_ba, g_wx, g_bx, g_lam = _lru_bwd(
        dh, xa, h, rx, conv_w_f, wa_b, row(lru_ba), wx_b, row(lru_bx), row(lru_lambda), tm)
    grad_x, g_norm_mix = _inproj_bwd(x2, dh1, drx, dg3, dmla, w1, row(norm_mix), tm)
    g_w_in = _w_in_grad_from_parts(_matmul_tn(xn, drx, "grad_w_in_rx"), _matmul_tn(xn, dg3, "grad_w_in_gates"),
                                   _matmul_tn(xn, dmla, "grad_w_in_mla"))
    g_w_uq = _w_uq_grad_from_wq(g_wq)

    smalls = (g_norm_mix, g_conv_b, g_wa, g_ba, g_wx, g_bx, g_lam, g_q_norm, g_kv_norm, g_norm_mlp, g_norm_final, g_conv_w)
    s_flat = jnp.concatenate([s.reshape(-1) for s in smalls] + [loss_blk[0, 0:1], jnp.zeros((S_LEN - N_SMALL - CONVW_SIZE - 1,), F32)])
    halves = [_halves_of_col_chunks(g_w_in), _halves_of_col_chunks(g_w_uq), _halves_of_col_chunks(g_wkv),
              _halves_of_row_chunks(g_w_out), _halves_of_col_chunks(g_w_up), _halves_of_row_chunks(g_w_down),
              s_flat.reshape(N_CHIPS, 2, S_ROWS_HALF, 128).transpose(1, 0, 2, 3)]
    theirs = _sibling_split(halves)
    parts = []
    for a, (hv, r) in enumerate(zip(halves, theirs)):
        dt = F32 if a == len(halves) - 1 else BF16
        pair = _pair_sum(hv.reshape(2, -1, hv.shape[-1]), r.reshape(-1, r.shape[-1]), core_ix, dt, f"grad_pair_sum_{a}")
        parts.append(pair.reshape(r.shape))
    received = _chip_exchange(parts)
    reduced = [_chip_sum(p, r, chip_ix, f"grad_chip_sum_{a}") for a, (p, r) in enumerate(zip(parts, received))]
    reduced_sibling = _sibling_join(reduced)
    s_mine, s_theirs = reduced[-1], reduced_sibling[-1]
    s_chunk = jnp.where(core == 0, jnp.concatenate([s_mine, s_theirs]), jnp.concatenate([s_theirs, s_mine]))
    s_all = _gather_chips([s_chunk], chip, "small_grad_gather")[0].reshape(-1)

    small_grads = []
    off = 0
    for shp, n in zip(SMALL_SHAPES, SMALL_SIZES):
        small_grads.append(s_all[off:off + n].reshape(shp))
        off += n
    g_conv_w_mine = lax.dynamic_slice_in_dim(s_all[off:off + CONVW_SIZE].reshape(4, D), chip * (D // N_CHIPS), D // N_CHIPS, axis=1)
    loss = s_all[off + CONVW_SIZE]

    big_m = (m_w_in, m_w_uq, m_w_ukv, m_w_out, m_w_up, m_w_down)
    big_v = (v_w_in, v_w_uq, v_w_ukv, v_w_out, v_w_up, v_w_down)
    big_names = ("w_in", "w_uq", "w_ukv", "w_out", "w_up", "w_down")
    big_upd = [_adamw_halves(w, gm, gt, m, v, core_ix, "adamw_" + n)
               for w, gm, gt, m, v, n in zip(big_shards, reduced, reduced_sibling, big_m, big_v, big_names)]

    small_w = (norm_mix, conv_b, lru_wa, lru_ba, lru_wx, lru_bx, lru_lambda, q_norm, kv_norm, norm_mlp, norm_final)
    small_m = (m_norm_mix, m_conv_b, m_lru_wa, m_lru_ba, m_lru_wx, m_lru_bx, m_lru_lambda, m_q_norm, m_kv_norm, m_norm_mlp, m_norm_final)
    small_v = (v_norm_mix, v_conv_b, v_lru_wa, v_lru_ba, v_lru_wx, v_lru_bx, v_lru_lambda, v_q_norm, v_kv_norm, v_norm_mlp, v_norm_final)

    def pack(items, last, fill):
        flat = jnp.concatenate([i.reshape(-1) for i in items] + [last.reshape(-1)])
        return jnp.concatenate([flat, jnp.full((PACK_ROWS * 128 - flat.shape[0],), fill, F32)]).reshape(PACK_ROWS, 128)

    packed = _adamw(pack(small_w, conv_w, 0.0), pack(small_grads, g_conv_w_mine, 0.0), pack(small_m, m_conv_w, 0.0),
                    pack(small_v, v_conv_w, 1.0), "adamw_small")

    def unpack(p):
        flat = p.reshape(-1)
        outs, o = [], 0
        for shp, n in zip(SMALL_SHAPES, SMALL_SIZES):
            outs.append(flat[o:o + n].reshape(shp))
            o += n
        return outs, flat[o:o + CONVW_SIZE // N_CHIPS].reshape(4, D // N_CHIPS)

    order = ("norm_mix", "w_in", "conv_w", "conv_b", "lru_wa", "lru_ba", "lru_wx", "lru_bx", "lru_lambda", "q_norm", "w_uq", "kv_norm",
             "w_ukv", "w_out", "norm_mlp", "w_up", "w_down", "norm_final")

    def assemble(small_list, conv_w_item, big_list):
        table = dict(zip(SMALL_NAMES, small_list))
        table["conv_w"] = conv_w_item
        table.update(zip(big_names, big_list))
        return [table[n] for n in order]

    outs = [loss, grad_x.reshape(1, t, D)]
    outs += assemble(small_grads, g_conv_w_mine, [b[0] for b in big_upd])
    for j in range(3):
        sm, cw = unpack(packed[j])
        outs += assemble(sm, cw, [b[j + 1] for b in big_upd])
    return tuple(outs)
```

```python
import functools
import math

import jax
import jax.numpy as jnp
from jax import lax
from jax.experimental import pallas as pl
from jax.experimental.pallas import tpu as pltpu

F32 = jnp.float32
BF16 = jnp.bfloat16

D = 1024
N_HEADS = 8
QK_NOPE = 128
QK_ROPE = 64
V_HEAD = 128
Q_LORA = 256
KV_LORA = 256
D_FF = 4096
RNN_BLOCKS = 8
RNN_BW = 128
LRU_C = 8.0
EPS = 1e-6
ROPE_THETA = 10000.0
HEAD_W = 256
KR_W = 128
W1_COLS = 4 * D + Q_LORA + KV_LORA + KR_W
SM_SCALE = (QK_NOPE + QK_ROPE) ** -0.5
NEG = float(jnp.finfo(jnp.float32).min)

ADAM_LR = 0.001
ADAM_B1 = 0.9
ADAM_B2 = 0.999
ADAM_EPS = 1e-08
ADAM_WD = 0.01
ADAM_STEP = 10

N_CHIPS = 4
V7X_VMEM_LIMIT = 56 * 1024 * 1024
MESH_ID = pl.DeviceIdType.MESH

SMALL_NAMES = ("norm_mix", "conv_b", "lru_wa", "lru_ba", "lru_wx", "lru_bx", "lru_lambda", "q_norm", "kv_norm", "norm_mlp", "norm_final")
SMALL_SHAPES = ((D,), (D,), (RNN_BLOCKS, RNN_BW, RNN_BW), (RNN_BLOCKS, RNN_BW), (RNN_BLOCKS, RNN_BW, RNN_BW), (RNN_BLOCKS, RNN_BW), (D,),
                (Q_LORA,), (KV_LORA,), (D,), (D,))
SMALL_SIZES = tuple(math.prod(s) for s in SMALL_SHAPES)
N_SMALL = sum(SMALL_SIZES)
CONVW_SIZE = 4 * D
S_LEN = -(-(N_SMALL + CONVW_SIZE) // 8192) * 8192
S_ROWS_HALF = S_LEN // (N_CHIPS * 2 * 128)
PACK_ROWS = -(-(N_SMALL + CONVW_SIZE // N_CHIPS) // (256 * 128)) * 256


def _pcall(body, name, grid, in_specs, out_specs, out_shape, scratch=()):
    return pl.pallas_call(
        body, name=name, grid=grid, in_specs=in_specs, out_specs=out_specs, out_shape=out_shape,
        scratch_shapes=list(scratch),
        compiler_params=pltpu.CompilerParams(dimension_semantics=("arbitrary",) * len(grid), vmem_limit_bytes=V7X_VMEM_LIMIT))


def _rows(tm, w):
    return pl.BlockSpec((tm, w), lambda i: (i, 0))


def _full(*shape):
    return pl.BlockSpec(shape, lambda *_: (0,) * len(shape))


def _sds(shape, dtype=F32):
    return jax.ShapeDtypeStruct(shape, dtype)


def _row_tile(rows, cap=256):
    t = min(rows, cap)
    while rows % t or t % 8:
        t -= 1
    return t


def _dot(a, b):
    return jnp.dot(a, b, preferred_element_type=F32)


def _dot_nt(a, b):
    return lax.dot_general(a, b, (((1,), (1,)), ((), ())), preferred_element_type=F32)


def _dot_tn(a, b):
    return lax.dot_general(a, b, (((0,), (0,)), ((), ())), preferred_element_type=F32)


def _sigmoid(x):
    return 1.0 / (1.0 + jnp.exp(-x))


_GELU_C = math.sqrt(2.0 / math.pi)


def _gelu(x):
    return x * (0.5 * (1.0 + jnp.tanh(_GELU_C * (x + 0.044715 * (x * x * x)))))


def _gelu_grad(x):
    t = jnp.tanh(_GELU_C * (x + 0.044715 * (x * x * x)))
    cdf = 0.5 * (1.0 + t)
    return cdf + x * (0.5 * (1.0 - t * t) * _GELU_C * (1.0 + 3.0 * 0.044715 * (x * x)))


def _rms_scale(x):
    return lax.rsqrt(jnp.mean(x * x, axis=-1, keepdims=True) + EPS)


def _rms_bwd(x, rs, g, dy):
    gdy = dy * g
    dx = rs * gdy - x * ((rs * rs * rs) * jnp.mean(gdy * x, axis=-1, keepdims=True))
    return dx, dy * (x * rs)


def _log1p(e):
    u = 1.0 + e
    d = u - 1.0
    return jnp.where(d == 0.0, e, jnp.log(u) * (e / jnp.where(d == 0.0, 1.0, d)))


def _softplus(y):
    return jnp.maximum(y, 0.0) + _log1p(jnp.exp(-jnp.abs(y)))


def _expm1(x):
    u = jnp.exp(x)
    lu = jnp.log(u)
    safe = jnp.where((u == 1.0) | (u == 0.0), 1.0, lu)
    return jnp.where(u == 1.0, x, jnp.where(u == 0.0, -1.0, (u - 1.0) * (x / safe)))


def _row_iota(shape):
    return lax.broadcasted_iota(jnp.int32, shape, 0)


def _lane_iota(shape):
    return lax.broadcasted_iota(jnp.int32, shape, 1)


def _rope_pair(gc):
    return gc + pltpu.roll(gc, 64, 1)


def _inproj(x, g, w1, tm, shards, chip):
    t = x.shape[0]
    nsteps = t // tm
    widths = (D, 3 * D, Q_LORA, KV_LORA, KR_W)
    srcs = [s.reshape(2, s.shape[0] // 2, s.shape[1]) for s in shards]
    n_sh = len(shards)

    def body(x_ref, g_ref, w_ref, *refs):
        sh_refs = refs[:n_sh]
        xn_ref, rx_ref, g3_ref, cq_ref, ckv_ref, kr_ref = refs[n_sh:n_sh + 6]
        gat_refs = refs[n_sh + 6:2 * n_sh + 6]
        sems = refs[2 * n_sh + 6:]
        pos = _mesh_pos()
        ici, d2d = _halved_gather_ops(pos, sh_refs, gat_refs, ())

        @pl.when(pl.program_id(0) == 0)
        def _():
            _start_copies(ici, sems, pos)

        xv = x_ref[...]
        xn = (xv * _rms_scale(xv) * g_ref[...]).astype(BF16)
        xn_ref[...] = xn
        col = 0
        for ref, w in zip((rx_ref, g3_ref, cq_ref, ckv_ref, kr_ref), widths):
            for c0 in range(0, w, 512):
                cw = min(512, w - c0)
                ref[:, c0:c0 + cw] = _dot(xn, w_ref[:, col + c0:col + c0 + cw])
            col += w

        @pl.when(pl.program_id(0) == nsteps - 1)
        def _():
            _wait_copies(ici, sems, pos)
            _start_copies(d2d, sems, pos, base=len(ici))
            _wait_copies(d2d, sems, pos, base=len(ici))

    hbm = pl.BlockSpec(memory_space=pl.ANY)
    outs = pl.pallas_call(
        body, name="inproj", grid=(nsteps,),
        in_specs=[_rows(tm, D), _full(1, D), _full(D, W1_COLS)] + [hbm] * n_sh,
        out_specs=[_rows(tm, D)] + [_rows(tm, w) for w in widths] + [hbm] * n_sh,
        out_shape=[_sds((t, D), BF16)] + [_sds((t, w)) for w in widths] + [_sds((N_CHIPS,) + s.shape, s.dtype) for s in srcs],
        scratch_shapes=[pltpu.SemaphoreType.DMA((6 * n_sh,)), pltpu.SemaphoreType.DMA((6 * n_sh,))],
        compiler_params=pltpu.CompilerParams(dimension_semantics=("arbitrary",), vmem_limit_bytes=V7X_VMEM_LIMIT),
    )(x, g, w1, *srcs)
    gathered = [lax.dynamic_update_index_in_dim(o, s, chip, 0).reshape((N_CHIPS,) + sh.shape)
                for o, s, sh in zip(outs[6:], srcs, shards)]
    return outs[:6], gathered


def _lru_gates(xa, wa_ref, ba, wx_ref, bx, pre_r, pre_i):
    xb = xa.astype(BF16)
    for n in range(RNN_BLOCKS):
        sl = slice(n * RNN_BW, (n + 1) * RNN_BW)
        pre_r[:, sl] = _dot(xb[:, sl], wa_ref[n])
        pre_i[:, sl] = _dot(xb[:, sl], wx_ref[n])
    r = _sigmoid(pre_r[...] + ba)
    i = _sigmoid(pre_i[...] + bx)
    return r, i


def _lru_fwd(rx, conv_w, conv_b, wa, ba, wx, bx, lam, tb):
    t = rx.shape[0]
    nb = t // tb

    def body(x_ref, xp_ref, cw_ref, cb_ref, wa_ref, ba_ref, wx_ref, bx_ref, lam_ref, h_ref, xa_ref, hc, tmp, pre_r, pre_i):
        i_blk = pl.program_id(0)

        @pl.when(i_blk == 0)
        def _():
            hc[...] = jnp.zeros_like(hc)

        xv = x_ref[...]
        xp = jnp.where(i_blk > 0, xp_ref[...], 0.0)
        row8 = _row_iota((8, D))
        xa = cb_ref[...] + cw_ref[3:4, :] * xv
        for s in (1, 2, 3):
            xr = pltpu.roll(xv, s, 0)
            tmp[...] = xr
            tmp[0:8, :] = jnp.where(row8 < s, pltpu.roll(xp, s, 0), xr[0:8, :])
            xa = xa + cw_ref[3 - s:4 - s, :] * tmp[...]
        xa_ref[...] = xa
        r, gi = _lru_gates(xa, wa_ref, ba_ref[...], wx_ref, bx_ref[...], pre_r, pre_i)
        la = (-LRU_C * _softplus(-lam_ref[...])) * r
        a = jnp.exp(la)
        b = jnp.sqrt(-_expm1(2.0 * la)) * (gi * xa)
        row = _row_iota((tb, D))
        sh = 1
        while sh < tb:
            m = row >= sh
            b = jnp.where(m, a * pltpu.roll(b, sh, 0) + b, b)
            a = jnp.where(m, a * pltpu.roll(a, sh, 0), a)
            sh *= 2
        h = a * hc[...] + b
        h_ref[...] = h
        hc[...] = h[tb - 1:tb, :]

    prev8 = pl.BlockSpec((8, D), lambda i: (jnp.maximum(i * (tb // 8) - 1, 0), 0))
    return _pcall(
        body, "lru_fwd", (nb,),
        [_rows(tb, D), prev8, _full(4, D), _full(1, D), _full(RNN_BLOCKS, RNN_BW, RNN_BW), _full(1, D),
         _full(RNN_BLOCKS, RNN_BW, RNN_BW), _full(1, D), _full(1, D)],
        [_rows(tb, D), _rows(tb, D)],
        [_sds((t, D)), _sds((t, D))],
        scratch=[pltpu.VMEM((1, D), F32), pltpu.VMEM((tb, D), F32), pltpu.VMEM((tb, D), F32), pltpu.VMEM((tb, D), F32)],
    )(rx, rx, conv_w, conv_b, wa, ba, wx, bx, lam)


def _mla_proj(cq, ckv, kr, qn, kvn, wq, wkv, rope_c, tm):
    t = cq.shape[0]

    def body(cq_ref, ckv_ref, kr_ref, qn_ref, kvn_ref, wq_ref, wkv_ref, c_ref, q_ref, k_ref, v_ref, cqn_ref, ckvn_ref):
        cqv = cq_ref[...]
        cqn = (cqv * _rms_scale(cqv) * qn_ref[...]).astype(BF16)
        ckvv = ckv_ref[...]
        ckvn = (ckvv * _rms_scale(ckvv) * kvn_ref[...]).astype(BF16)
        cqn_ref[...] = cqn
        ckvn_ref[...] = ckvn
        c = c_ref[...]
        lane = _lane_iota((tm, KR_W))
        kro = jnp.where(lane < 64, _rope_pair(kr_ref[...] * c), 0.0).astype(BF16)
        for h in range(N_HEADS):
            sl = slice(h * HEAD_W, (h + 1) * HEAD_W)
            qh = _dot(cqn, wq_ref[:, sl])
            q_ref[h, :, 0:128] = qh[:, 0:128].astype(BF16)
            q_ref[h, :, 128:256] = _rope_pair(qh[:, 128:256] * c).astype(BF16)
            kvh = _dot(ckvn, wkv_ref[:, sl])
            k_ref[h, :, 0:128] = kvh[:, 0:128].astype(BF16)
            k_ref[h, :, 128:256] = kro
            v_ref[h] = kvh[:, 128:256].astype(BF16)

    hb = lambda w: pl.BlockSpec((N_HEADS, tm, w), lambda i: (0, i, 0))
    return _pcall(
        body, "mla_proj", (t // tm,),
        [_rows(tm, Q_LORA), _rows(tm, KV_LORA), _rows(tm, KR_W), _full(1, Q_LORA), _full(1, KV_LORA),
         _full(Q_LORA, N_HEADS * HEAD_W), _full(KV_LORA, N_HEADS * HEAD_W), _rows(tm, KR_W)],
        [hb(HEAD_W), hb(HEAD_W), hb(V_HEAD), _rows(tm, Q_LORA), _rows(tm, KV_LORA)],
        [_sds((N_HEADS, t, HEAD_W), BF16), _sds((N_HEADS, t, HEAD_W), BF16), _sds((N_HEADS, t, V_HEAD), BF16),
         _sds((t, Q_LORA), BF16), _sds((t, KV_LORA), BF16)],
    )(cq, ckv, kr, qn, kvn, wq, wkv, rope_c)


EXP2_SCALE = SM_SCALE * math.log2(math.e)


def _flash_fwd(q, k, v, tq):
    t = q.shape[1]
    nq = t // tq

    def body(q_ref, k_ref, v_ref, o_ref, lse_ref):
        qi = pl.program_id(1)
        qv = q_ref[0]

        def block(ki, carry, diagonal):
            m, l, acc = carry
            rows = pl.ds(pl.multiple_of(ki * tq, tq), tq)
            s = _dot_nt(qv, k_ref[0, rows, :])
            if diagonal:
                s = jnp.where(_row_iota((tq, tq)) >= _lane_iota((tq, tq)), s, NEG)
            m_new = jnp.maximum(m, jnp.max(s, axis=1, keepdims=True))
            p = jnp.exp2((s - m_new) * EXP2_SCALE)
            alpha = jnp.exp2((m - m_new) * EXP2_SCALE)
            l = alpha * l + jnp.sum(p, axis=1, keepdims=True)
            acc = alpha * acc + _dot(p.astype(BF16), v_ref[0, rows, :])
            return m_new, l, acc

        init = (jnp.full((tq, 1), -jnp.inf, F32), jnp.zeros((tq, 1), F32), jnp.zeros((tq, V_HEAD), F32))
        carry = lax.fori_loop(0, qi // 2, lambda i, c: block(2 * i + 1, block(2 * i, c, False), False), init)
        carry = lax.fori_loop(2 * (qi // 2), qi, lambda ki, c: block(ki, c, False), carry)
        m, l, acc = block(qi, carry, True)
        o_ref[...] = acc / l
        lse = jnp.broadcast_to(m * EXP2_SCALE + jnp.log(l) * math.log2(math.e), (tq, V_HEAD))
        lse_ref[0, 0] = jnp.transpose(lse)[0:8, :]

    head = lambda w: pl.BlockSpec((1, t, w), lambda h, qi: (h, 0, 0))
    return _pcall(
        body, "flash_fwd", (N_HEADS, nq),
        [pl.BlockSpec((1, tq, HEAD_W), lambda h, qi: (h, qi, 0)), head(HEAD_W), head(V_HEAD)],
        [pl.BlockSpec((tq, V_HEAD), lambda h, qi: (qi, h)), pl.BlockSpec((1, 1, 8, tq), lambda h, qi: (h, qi, 0, 0))],
        [_sds((t, D)), _sds((N_HEADS, nq, 8, tq))],
    )(q, k, v)


def _merge_out(x, h, g3, yb, w_out, tm):
    t = x.shape[0]

    def body(x_ref, h_ref, g3_ref, yb_ref, w_ref, h1_ref, mg_ref):
        ya = h_ref[...] * _gelu(g3_ref[:, 0:D])
        merged = (_sigmoid(g3_ref[:, D:2 * D]) * ya + _sigmoid(g3_ref[:, 2 * D:3 * D]) * yb_ref[...]).astype(BF16)
        mg_ref[...] = merged
        h1_ref[...] = x_ref[...] + _dot(merged, w_ref[...])

    return _pcall(
        body, "merge_out", (t // tm,),
        [_rows(tm, D), _rows(tm, D), _rows(tm, 3 * D), _rows(tm, D), _full(D, D)],
        [_rows(tm, D), _rows(tm, D)],
        [_sds((t, D)), _sds((t, D), BF16)],
    )(x, h, g3, yb, w_out)


def _mlp_up(h1, g, w_up, tm):
    t = h1.shape[0]

    def body(h_ref, g_ref, w_ref, u_ref, n2_ref):
        hv = h_ref[...]
        n2 = (hv * _rms_scale(hv) * g_ref[...]).astype(BF16)
        n2_ref[...] = n2
        for c0 in range(0, D_FF, 512):
            u_ref[:, c0:c0 + 512] = _dot(n2, w_ref[:, c0:c0 + 512])

    return _pcall(
        body, "mlp_up", (t // tm,),
        [_rows(tm, D), _full(1, D), _full(D, D_FF)],
        [_rows(tm, D_FF), _rows(tm, D)],
        [_sds((t, D_FF)), _sds((t, D), BF16)],
    )(h1, g, w_up)


def _mlp_down_loss(u, h1, target, w_down, g, tm):
    t = u.shape[0]

    def body(u_ref, h1_ref, tg_ref, w_ref, g_ref, act_ref, dh2_ref, loss_ref, gnf_ref, lacc):
        i = pl.program_id(0)

        @pl.when(i == 0)
        def _():
            lacc[...] = jnp.zeros_like(lacc)
            gnf_ref[...] = jnp.zeros_like(gnf_ref)

        ru = jnp.maximum(u_ref[...], 0.0)
        act = (ru * ru).astype(BF16)
        act_ref[...] = act
        h2 = h1_ref[...] + _dot(act, w_ref[...])
        rs = _rms_scale(h2)
        gv = g_ref[...]
        err = h2 * rs * gv - tg_ref[...]
        lacc[...] += jnp.sum(err * err, axis=0, keepdims=True)
        dx, dgr = _rms_bwd(h2, rs, gv, err * (1.0 / D))
        dh2_ref[...] = dx
        gnf_ref[...] += jnp.sum(dgr, axis=0, keepdims=True)

        @pl.when(i == pl.num_programs(0) - 1)
        def _():
            loss_ref[...] = jnp.broadcast_to(jnp.sum(lacc[...], axis=1, keepdims=True) * (0.5 / D), (8, 128))

    return _pcall(
        body, "mlp_down_loss", (t // tm,),
        [_rows(tm, D_FF), _rows(tm, D), _rows(tm, D), _full(D_FF, D), _full(1, D)],
        [_rows(tm, D_FF), _rows(tm, D), _full(8, 128), _full(1, D)],
        [_sds((t, D_FF), BF16), _sds((t, D)), _sds((8, 128)), _sds((1, D))],
        scratch=[pltpu.VMEM((1, D), F32)],
    )(u, h1, target, w_down, g)


def _matmul_tn(a, g, name):
    t, kdim = a.shape
    ndim = g.shape[1]
    tk = min(kdim, 1024)
    tn = ndim if ndim <= 1024 else 1024
    tt = min(t, 512)
    nt = t // tt

    def body(a_ref, g_ref, o_ref):
        @pl.when(pl.program_id(2) == 0)
        def _():
            o_ref[...] = jnp.zeros_like(o_ref)

        o_ref[...] += _dot_tn(a_ref[...].astype(BF16), g_ref[...].astype(BF16))

    return _pcall(
        body, name, (kdim // tk, ndim // tn, nt),
        [pl.BlockSpec((tt, tk), lambda i, j, s: (s, i)), pl.BlockSpec((tt, tn), lambda i, j, s: (s, j))],
        pl.BlockSpec((tk, tn), lambda i, j, s: (i, j)),
        _sds((kdim, ndim)),
    )(a, g)


def _mlp_bwd_act(dh2, u, w_down, tm):
    t = u.shape[0]

    def body(d_ref, u_ref, w_ref, du_ref):
        db = d_ref[...].astype(BF16)
        for c0 in range(0, D_FF, 512):
            da = _dot_nt(db, w_ref[c0:c0 + 512, :])
            du_ref[:, c0:c0 + 512] = (da * (2.0 * jnp.maximum(u_ref[:, c0:c0 + 512], 0.0))).astype(BF16)

    return _pcall(
        body, "mlp_bwd_act", (t // tm,),
        [_rows(tm, D), _rows(tm, D_FF), _full(D_FF, D)],
        _rows(tm, D_FF), _sds((t, D_FF), BF16),
    )(dh2, u, w_down)


def _mlp_bwd_in(du, dh2, h1, w_up, g, tm):
    t = du.shape[0]

    def body(du_ref, d_ref, h_ref, w_ref, g_ref, dh1_ref, gacc_ref):
        @pl.when(pl.program_id(0) == 0)
        def _():
            gacc_ref[...] = jnp.zeros_like(gacc_ref)

        dn2 = _dot_nt(du_ref[...], w_ref[...])
        hv = h_ref[...]
        dx, dgr = _rms_bwd(hv, _rms_scale(hv), g_ref[...], dn2)
        dh1_ref[...] = d_ref[...] + dx
        gacc_ref[...] += jnp.sum(dgr, axis=0, keepdims=True)

    return _pcall(
        body, "mlp_bwd_in", (t // tm,),
        [_rows(tm, D_FF), _rows(tm, D), _rows(tm, D), _full(D, D_FF), _full(1, D)],
        [_rows(tm, D), _full(1, D)],
        [_sds((t, D)), _sds((1, D))],
    )(du, dh2, h1, w_up, g)


def _merge_bwd(dh1, w_out, g3, h, yb, merged, tm):
    t = dh1.shape[0]

    def body(d_ref, w_ref, g3_ref, h_ref, yb_ref, mg_ref, dg3_ref, dyb_ref, dl_ref, dh_ref, dwo_ref):
        @pl.when(pl.program_id(0) == 0)
        def _():
            dwo_ref[...] = jnp.zeros_like(dwo_ref)

        db = d_ref[...].astype(BF16)
        dwo_ref[...] += _dot_tn(mg_ref[...], db)
        dm = _dot_nt(db, w_ref[...])
        gv = g3_ref[:, 0:D]
        sa = _sigmoid(g3_ref[:, D:2 * D])
        sb = _sigmoid(g3_ref[:, 2 * D:3 * D])
        gel = _gelu(gv)
        hv = h_ref[...]
        ybv = yb_ref[...]
        dya = dm * sa
        dyb = dm * sb
        dg3_ref[:, 0:D] = (dya * hv * _gelu_grad(gv)).astype(BF16)
        dg3_ref[:, D:2 * D] = (dya * (hv * gel) * (1.0 - sa)).astype(BF16)
        dg3_ref[:, 2 * D:3 * D] = (dyb * ybv * (1.0 - sb)).astype(BF16)
        dh_ref[...] = dya * gel
        dyb_ref[...] = dyb.astype(BF16)
        prod = dyb * ybv
        ones = jnp.ones((8, V_HEAD), F32)
        for hh in range(N_HEADS):
            dl_ref[hh] = lax.dot_general(ones, prod[:, hh * V_HEAD:(hh + 1) * V_HEAD], (((1,), (1,)), ((), ())),
                                         precision=lax.Precision.HIGHEST, preferred_element_type=F32)

    return _pcall(
        body, "merge_bwd", (t // tm,),
        [_rows(tm, D), _full(D, D), _rows(tm, 3 * D), _rows(tm, D), _rows(tm, D), _rows(tm, D)],
        [_rows(tm, 3 * D), _rows(tm, D), pl.BlockSpec((N_HEADS, 8, tm), lambda i: (0, 0, i)), _rows(tm, D), _full(D, D)],
        [_sds((t, 3 * D), BF16), _sds((t, D), BF16), _sds((N_HEADS, 8, t)), _sds((t, D)), _sds((D, D))],
    )(dh1, w_out, g3, h, yb, merged)


def _flash_bwd(q, k, v, do, lse, delta, tq):
    t = q.shape[1]
    nq = t // tq

    def body(q_ref, k_ref, v_ref, do_ref, lse_ref, dl_ref, dqt_ref, dk_ref, dv_ref):
        ki = pl.program_id(1)

        @pl.when(ki == 0)
        def _():
            dqt_ref[...] = jnp.zeros_like(dqt_ref)

        kblk, vblk = k_ref[0], v_ref[0]
        kt = jnp.transpose(kblk)
        dk_ref[...] = jnp.zeros_like(dk_ref)
        dv_ref[...] = jnp.zeros_like(dv_ref)

        def block(qi, diagonal):
            rows = pl.ds(pl.multiple_of(qi * tq, tq), tq)
            qv, dov = q_ref[0, rows, :], do_ref[rows, :]
            p = jnp.exp2(_dot_nt(kblk, qv) * EXP2_SCALE - lse_ref[0, qi, 0:1, :])
            if diagonal:
                p = jnp.where(_lane_iota((tq, tq)) >= _row_iota((tq, tq)), p, 0.0)
            dv_ref[0] += _dot(p.astype(BF16), dov)
            dp = _dot_nt(vblk, dov)
            ds = (p * (dp - dl_ref[0, qi, 0:1, :]) * SM_SCALE).astype(BF16)
            dk_ref[0] += _dot(ds, qv)
            dqt_ref[0, qi] += _dot(kt, ds)

        block(ki, True)

        def two(i, carry):
            block(ki + 1 + 2 * i, False)
            block(ki + 2 + 2 * i, False)
            return carry

        def one(qi, carry):
            block(qi, False)
            return carry

        pairs = (nq - 1 - ki) // 2
        lax.fori_loop(0, pairs, two, 0)
        lax.fori_loop(ki + 1 + 2 * pairs, nq, one, 0)

    kv_spec = lambda w: pl.BlockSpec((1, tq, w), lambda h, ki: (h, ki, 0))
    stat = pl.BlockSpec((1, nq, 8, tq), lambda h, ki: (h, 0, 0, 0))
    return _pcall(
        body, "flash_bwd", (N_HEADS, nq),
        [pl.BlockSpec((1, t, HEAD_W), lambda h, ki: (h, 0, 0)), kv_spec(HEAD_W), kv_spec(V_HEAD),
         pl.BlockSpec((t, V_HEAD), lambda h, ki: (0, h)), stat, stat],
        [pl.BlockSpec((1, nq, HEAD_W, tq), lambda h, ki: (h, 0, 0, 0)), kv_spec(HEAD_W), kv_spec(V_HEAD)],
        [_sds((N_HEADS, nq, HEAD_W, tq)), _sds((N_HEADS, t, HEAD_W)), _sds((N_HEADS, t, V_HEAD))],
    )(q, k, v, do, lse, delta)


def _mla_bwd(dqt, dk, dv, cqn, ckvn, cq, ckv, rope_c, wq, wkv, qn, kvn, tm):
    t = cq.shape[0]

    def body(dq_ref, dk_ref, dv_ref, cqn_ref, ckvn_ref, cq_ref, ckv_ref, c_ref, wq_ref, wkv_ref, qn_ref, kvn_ref,
             dmla_ref, dwq_ref, dwkv_ref, dqn_ref, dkvn_ref):
        @pl.when(pl.program_id(0) == 0)
        def _():
            dwq_ref[...] = jnp.zeros_like(dwq_ref)
            dwkv_ref[...] = jnp.zeros_like(dwkv_ref)
            dqn_ref[...] = jnp.zeros_like(dqn_ref)
            dkvn_ref[...] = jnp.zeros_like(dkvn_ref)

        c = c_ref[...]
        lane = _lane_iota((tm, KR_W))
        cqn, ckvn = cqn_ref[...], ckvn_ref[...]
        dcqn = jnp.zeros((tm, Q_LORA), F32)
        dckvn = jnp.zeros((tm, KV_LORA), F32)
        dkr = jnp.zeros((tm, KR_W), F32)
        for h in range(N_HEADS):
            sl = slice(h * HEAD_W, (h + 1) * HEAD_W)
            dqh = jnp.transpose(dq_ref[h, 0])
            droped = jnp.where(lane < 64, dqh[:, 128:256], 0.0)
            dqp = jnp.concatenate([dqh[:, 0:128], _rope_pair(droped) * c], axis=1).astype(BF16)
            dcqn = dcqn + _dot_nt(dqp, wq_ref[:, sl])
            dwq_ref[:, sl] += _dot_tn(cqn, dqp)
            dkr = dkr + jnp.where(lane < 64, dk_ref[h, :, 128:256], 0.0)
            dkvp = jnp.concatenate([dk_ref[h, :, 0:128], dv_ref[h]], axis=1).astype(BF16)
            dckvn = dckvn + _dot_nt(dkvp, wkv_ref[:, sl])
            dwkv_ref[:, sl] += _dot_tn(ckvn, dkvp)
        cqv, ckvv = cq_ref[...], ckv_ref[...]
        dcq, dgq = _rms_bwd(cqv, _rms_scale(cqv), qn_ref[...], dcqn)
        dckv, dgkv = _rms_bwd(ckvv, _rms_scale(ckvv), kvn_ref[...], dckvn)
        dqn_ref[...] += jnp.sum(dgq, axis=0, keepdims=True)
        dkvn_ref[...] += jnp.sum(dgkv, axis=0, keepdims=True)
        dmla_ref[:, 0:256] = dcq.astype(BF16)
        dmla_ref[:, 256:512] = dckv.astype(BF16)
        dmla_ref[:, 512:640] = (_rope_pair(dkr) * c).astype(BF16)

    hb = lambda w: pl.BlockSpec((N_HEADS, tm, w), lambda i: (0, i, 0))
    wide = N_HEADS * HEAD_W
    per_q = dqt.shape[3] // tm
    dq_spec = pl.BlockSpec((N_HEADS, 1, HEAD_W, tm), lambda i: (0, i // per_q, 0, i % per_q))
    return _pcall(
        body, "mla_bwd", (t // tm,),
        [dq_spec, hb(HEAD_W), hb(V_HEAD), _rows(tm, Q_LORA), _rows(tm, KV_LORA), _rows(tm, Q_LORA), _rows(tm, KV_LORA),
         _rows(tm, KR_W), _full(Q_LORA, wide), _full(KV_LORA, wide), _full(1, Q_LORA), _full(1, KV_LORA)],
        [_rows(tm, 640), _full(Q_LORA, wide), _full(KV_LORA, wide), _full(1, Q_LORA), _full(1, KV_LORA)],
        [_sds((t, 640), BF16), _sds((Q_LORA, wide)), _sds((KV_LORA, wide)), _sds((1, Q_LORA)), _sds((1, KV_LORA))],
    )(dqt, dk, dv, cqn, ckvn, cq, ckv, rope_c, wq, wkv, qn, kvn)


def _lru_bwd(dh, xa, h, rx, conv_w, wa, ba, wx, bx, lam, tb):
    t = dh.shape[0]
    nb = t // tb

    def body(dh_ref, xa_ref, h_ref, hp_ref, x_ref, cw_ref, wa_ref, ba_ref, wx_ref, bx_ref, lam_ref,
             drx_ref, dcw_ref, dcb_ref, dwa_ref, dba_ref, dwx_ref, dbx_ref, dlam_ref, gc, dxn, tmp, pre_r, pre_i):
        step = pl.program_id(0)
        first_block = step == nb - 1

        @pl.when(step == 0)
        def _():
            gc[...] = jnp.zeros_like(gc)
            dxn[...] = jnp.zeros_like(dxn)
            for ref in (dcw_ref, dcb_ref, dwa_ref, dba_ref, dwx_ref, dbx_ref, dlam_ref):
                ref[...] = jnp.zeros_like(ref)

        xa = xa_ref[...]
        r, gi = _lru_gates(xa, wa_ref, ba_ref[...], wx_ref, bx_ref[...], pre_r, pre_i)
        lamv = lam_ref[...]
        sp = _softplus(-lamv)
        la = (-LRU_C * sp) * r
        a = jnp.exp(la)
        e2 = _expm1(2.0 * la)
        sq = jnp.sqrt(-e2)
        row = _row_iota((tb, D))
        cf = jnp.where(row == tb - 1, 1.0, pltpu.roll(a, tb - 1, 0))
        bv = dh_ref[...]
        sh = 1
        while sh < tb:
            m = row < tb - sh
            bv = jnp.where(m, bv + cf * pltpu.roll(bv, tb - sh, 0), bv)
            cf = jnp.where(m, cf * pltpu.roll(cf, tb - sh, 0), cf)
            sh *= 2
        delta = bv + cf * gc[...]
        gc[...] = a[0:1, :] * delta[0:1, :]
        hv = h_ref[...]
        hr = pltpu.roll(hv, 1, 0)
        tmp[...] = hr
        tmp[0:1, :] = jnp.where(first_block, 0.0, hp_ref[7:8, :])
        hprev = tmp[...]
        ix = gi * xa
        dla = (delta * hprev) * a - (delta * ix) * ((e2 + 1.0) / sq)
        dlam_ref[...] += jnp.sum(dla * r, axis=0, keepdims=True) * (LRU_C * _sigmoid(-lamv))
        dpr = (dla * (-LRU_C * sp)) * r * (1.0 - r)
        dsq = delta * sq
        dpi = (dsq * xa) * gi * (1.0 - gi)
        dba_ref[...] += jnp.sum(dpr, axis=0, keepdims=True)
        dbx_ref[...] += jnp.sum(dpi, axis=0, keepdims=True)
        pre_r[...] = dpr
        pre_i[...] = dpi
        xb = xa.astype(BF16)
        for n in range(RNN_BLOCKS):
            sl = slice(n * RNN_BW, (n + 1) * RNN_BW)
            dprn = pre_r[:, sl].astype(BF16)
            dpin = pre_i[:, sl].astype(BF16)
            dwa_ref[n] += _dot_tn(xb[:, sl], dprn)
            dwx_ref[n] += _dot_tn(xb[:, sl], dpin)
            tmp[:, sl] = _dot_nt(dprn, wa_ref[n]) + _dot_nt(dpin, wx_ref[n])
        dxa = dsq * gi + tmp[...]
        dcb_ref[...] += jnp.sum(dxa, axis=0, keepdims=True)
        xv = x_ref[...]
        drx = cw_ref[3:4, :] * dxa
        dcw_ref[3:4, :] += jnp.sum(dxa * xv, axis=0, keepdims=True)
        row8 = _row_iota((8, D))
        nxt = dxn[...]
        for s in (1, 2, 3):
            dr_ = pltpu.roll(dxa, tb - s, 0)
            tmp[...] = dr_
            tmp[tb - 8:tb, :] = jnp.where(row8 >= 8 - s, pltpu.roll(nxt, 8 - s, 0), dr_[tb - 8:tb, :])
            dxs = tmp[...]
            drx = drx + cw_ref[3 - s:4 - s, :] * dxs
            dcw_ref[3 - s:4 - s, :] += jnp.sum(dxs * xv, axis=0, keepdims=True)
        drx_ref[...] = drx.astype(BF16)
        dxn[...] = dxa[0:8, :]

    rev = pl.BlockSpec((tb, D), lambda i: (nb - 1 - i, 0))
    prev8 = pl.BlockSpec((8, D), lambda i: (jnp.maximum((nb - 1 - i) * (tb // 8) - 1, 0), 0))
    wblk = _full(RNN_BLOCKS, RNN_BW, RNN_BW)
    return _pcall(
        body, "lru_bwd", (nb,),
        [rev, rev, rev, prev8, rev, _full(4, D), wblk, _full(1, D), wblk, _full(1, D), _full(1, D)],
        [rev, _full(4, D), _full(1, D), wblk, _full(1, D), wblk, _full(1, D), _full(1, D)],
        [_sds((t, D), BF16), _sds((4, D)), _sds((1, D)), _sds((RNN_BLOCKS, RNN_BW, RNN_BW)), _sds((1, D)),
         _sds((RNN_BLOCKS, RNN_BW, RNN_BW)), _sds((1, D)), _sds((1, D))],
        scratch=[pltpu.VMEM((1, D), F32), pltpu.VMEM((8, D), F32), pltpu.VMEM((tb, D), F32), pltpu.VMEM((tb, D), F32),
                 pltpu.VMEM((tb, D), F32)],
    )(dh, xa, h, h, rx, conv_w, wa, ba, wx, bx, lam)


def _inproj_bwd(x, dh1, drx, dg3, dmla, w1, g, tm):
    t = x.shape[0]

    def body(x_ref, d_ref, drx_ref, dg3_ref, dmla_ref, w_ref, g_ref, dx_ref, gacc_ref):
        @pl.when(pl.program_id(0) == 0)
        def _():
            gacc_ref[...] = jnp.zeros_like(gacc_ref)

        dxn = _dot_nt(drx_ref[...], w_ref[:, 0:D])
        for c0 in range(0, 3 * D, D):
            dxn = dxn + _dot_nt(dg3_ref[:, c0:c0 + D], w_ref[:, D + c0:2 * D + c0])
        dxn = dxn + _dot_nt(dmla_ref[...], w_ref[:, 4 * D:W1_COLS])
        xv = x_ref[...]
        dx, dgr = _rms_bwd(xv, _rms_scale(xv), g_ref[...], dxn)
        dx_ref[...] = d_ref[...] + dx
        gacc_ref[...] += jnp.sum(dgr, axis=0, keepdims=True)

    return _pcall(
        body, "inproj_bwd", (t // tm,),
        [_rows(tm, D), _rows(tm, D), _rows(tm, D), _rows(tm, 3 * D), _rows(tm, 640), _full(D, W1_COLS), _full(1, D)],
        [_rows(tm, D), _full(1, D)],
        [_sds((t, D)), _sds((1, D))],
    )(x, dh1, drx, dg3, dmla, w1, g)


def _pcall_indexed(body, name, index, grid, in_specs, out_specs, out_shape):
    call = pl.pallas_call(
        body, name=name, out_shape=out_shape,
        grid_spec=pltpu.PrefetchScalarGridSpec(num_scalar_prefetch=1, grid=grid, in_specs=in_specs, out_specs=out_specs),
        compiler_params=pltpu.CompilerParams(dimension_semantics=("arbitrary",) * len(grid), vmem_limit_bytes=V7X_VMEM_LIMIT))
    return lambda *operands: call(index, *operands)


def _pair_sum(halves, theirs, core, out_dtype, name):
    _, rows, cols = halves.shape
    tm = _row_tile(rows)

    def body(c_ref, a_ref, b_ref, o_ref):
        o_ref[...] = (a_ref[0] + b_ref[...]).astype(out_dtype)

    plain = pl.BlockSpec((tm, cols), lambda i, c: (i, 0))
    return _pcall_indexed(body, name, core, (rows // tm,),
                          [pl.BlockSpec((1, tm, cols), lambda i, c: (c[0], i, 0)), plain], plain,
                          _sds((rows, cols), out_dtype))(halves, theirs)


def _chip_sum(parts, recv, chip, name):
    _, rows, cols = parts.shape
    tm = _row_tile(rows)

    def body(c_ref, a_ref, r_ref, o_ref):
        o_ref[...] = ((a_ref[0].astype(F32) + r_ref[0].astype(F32)) + r_ref[1].astype(F32)) + r_ref[2].astype(F32)

    return _pcall_indexed(body, name, chip, (rows // tm,),
                          [pl.BlockSpec((1, tm, cols), lambda i, c: (c[0], i, 0)),
                           pl.BlockSpec((N_CHIPS - 1, tm, cols), lambda i, c: (0, i, 0))],
                          pl.BlockSpec((tm, cols), lambda i, c: (i, 0)), _sds((rows, cols)))(parts, recv)


def _adam_math(w, gv, m, v):
    mn = ADAM_B1 * m + (1.0 - ADAM_B1) * gv
    vn = ADAM_B2 * v + (1.0 - ADAM_B2) * (gv * gv)
    m_hat = mn / (1.0 - ADAM_B1 ** ADAM_STEP)
    v_hat = vn / (1.0 - ADAM_B2 ** ADAM_STEP)
    return -ADAM_LR * (m_hat / (jnp.sqrt(v_hat) + ADAM_EPS) + ADAM_WD * w), mn, vn


def _adamw(w, g, m, v, name):
    rows, cols = w.shape
    tm = _row_tile(rows)

    def body(w_ref, g_ref, m_ref, v_ref, d_ref, mo_ref, vo_ref):
        d_ref[...], mo_ref[...], vo_ref[...] = _adam_math(w_ref[...], g_ref[...], m_ref[...], v_ref[...])

    spec = _rows(tm, cols)
    return _pcall(body, name, (rows // tm,), [spec] * 4, [spec] * 3, [_sds((rows, cols))] * 3)(w, g, m, v)


def _adamw_halves(w, mine, theirs, m, v, core, name):
    rows, cols = w.shape
    tm = _row_tile(rows // 2)
    nh = rows // 2 // tm

    def body(c_ref, w_ref, a_ref, b_ref, m_ref, v_ref, g_ref, d_ref, mo_ref, vo_ref):
        gv = jnp.where(pl.program_id(0) // nh == c_ref[0], a_ref[...], b_ref[...])
        g_ref[...] = gv
        d_ref[...], mo_ref[...], vo_ref[...] = _adam_math(w_ref[...], gv, m_ref[...], v_ref[...])

    full = pl.BlockSpec((tm, cols), lambda i, c: (i, 0))
    half = pl.BlockSpec((tm, cols), lambda i, c: (i % nh, 0))
    return _pcall_indexed(body, name, core, (rows // tm,), [full, half, half, full, full], [full] * 4,
                          [_sds((rows, cols))] * 4)(w, mine, theirs, m, v)


REL_SIBLING = (0, 0, 1)
REL_CHIPS = ((1, 0, 0), (0, 1, 0), (1, 1, 0))


V7X_DMA_CHUNK_BYTES = 1 << 20


def _split_copy(src, dst, shape, itemsize):
    nbytes = math.prod(shape) * itemsize
    if nbytes <= V7X_DMA_CHUNK_BYTES or len(shape) < 2:
        return [(src, dst)]
    if len(shape) > 2:
        out = []
        for k in range(shape[0]):
            out += _split_copy(src.at[k], dst.at[k], shape[1:], itemsize)
        return out
    rows = shape[0]
    sub = 8 * (4 // itemsize)
    parts = max(1, min(-(-nbytes // V7X_DMA_CHUNK_BYTES), rows // sub))
    while rows % parts or (rows // parts) % sub:
        parts -= 1
    step = rows // parts
    return [(src.at[pl.ds(k * step, step)], dst.at[pl.ds(k * step, step)]) for k in range(parts)]


def _mesh_pos():
    return (lax.axis_index("x"), lax.axis_index("y"), lax.axis_index("c"))


def _make_copy(i, op, sems, pos, src=None, dst=None):
    rel = op[0]
    src, dst = (op[1], op[2]) if src is None else (src, dst)
    send_sems, recv_sems = sems
    if rel is None:
        return pltpu.make_async_copy(src, dst, send_sems.at[i])
    peer = tuple((p + r) % 2 for p, r in zip(pos, rel))
    return pltpu.make_async_remote_copy(src_ref=src, dst_ref=dst, send_sem=send_sems.at[i], recv_sem=recv_sems.at[i],
                                        device_id=peer, device_id_type=MESH_ID)


def _start_copies(ops, sems, pos, base=0):
    for i, op in enumerate(ops):
        for s_piece, d_piece in _split_copy(op[1], op[2], op[1].shape, jnp.dtype(op[1].dtype).itemsize):
            _make_copy(base + i, op, sems, pos, s_piece, d_piece).start()


def _wait_copies(ops, sems, pos, base=0):
    for i, op in enumerate(ops):
        _make_copy(base + i, op, sems, pos).wait()


def _comm(name, ins, out_shapes, n_ops, ops_fn):
    n_in, n_out = len(ins), len(out_shapes)

    def body(*refs):
        in_refs, out_refs = refs[:n_in], refs[n_in:n_in + n_out]
        sems = refs[n_in + n_out:]
        pos = _mesh_pos()
        ops = ops_fn(in_refs, out_refs, pos)
        assert len(ops) == n_ops
        _start_copies(ops, sems, pos)
        _wait_copies(ops, sems, pos)

    hbm = pl.BlockSpec(memory_space=pl.ANY)
    return pl.pallas_call(
        body, name=name, in_specs=[hbm] * n_in, out_specs=[hbm] * n_out, out_shape=list(out_shapes),
        scratch_shapes=[pltpu.SemaphoreType.DMA((n_ops,)), pltpu.SemaphoreType.DMA((n_ops,))],
    )(*ins)


def _chip_of(pos, rel=(0, 0, 0)):
    return 2 * ((pos[0] + rel[0]) % 2) + (pos[1] + rel[1]) % 2


def _gather_chips(shards, chip, name):
    def ops_fn(in_refs, out_refs, pos):
        me = _chip_of(pos)
        return [(rel, src, dst.at[me]) for src, dst in zip(in_refs, out_refs) for rel in REL_CHIPS]

    outs = _comm(name, shards, [_sds((N_CHIPS,) + s.shape, s.dtype) for s in shards], 3 * len(shards), ops_fn)
    return [lax.dynamic_update_index_in_dim(o, s, chip, 0) for o, s in zip(outs, shards)]


def _halved_gather_ops(pos, srcs, dsts, whole):
    me, c = _chip_of(pos), pos[2]
    ici, d2d = [], []
    for a, (src, dst) in enumerate(zip(srcs, dsts)):
        for rel in REL_CHIPS:
            if a in whole:
                ici.append((rel, src, dst.at[me]))
            else:
                ici.append((rel, src.at[c], dst.at[me, c]))
                arrived = dst.at[_chip_of(pos, rel), c]
                d2d.append((REL_SIBLING, arrived, arrived))
    return ici, d2d


def _gather_halved(shards, whole, chip, name):
    srcs = [s if a in whole else s.reshape(2, s.shape[0] // 2, s.shape[1]) for a, s in enumerate(shards)]
    n_sh = len(shards)
    n_ici, n_d2d = 3 * n_sh, 3 * (n_sh - len(whole))

    def body(*refs):
        in_refs, out_refs, sems = refs[:n_sh], refs[n_sh:2 * n_sh], refs[2 * n_sh:]
        pos = _mesh_pos()
        ici, d2d = _halved_gather_ops(pos, in_refs, out_refs, whole)
        _start_copies(ici, sems, pos)
        _wait_copies(ici, sems, pos)
        _start_copies(d2d, sems, pos, base=n_ici)
        _wait_copies(d2d, sems, pos, base=n_ici)

    hbm = pl.BlockSpec(memory_space=pl.ANY)
    outs = pl.pallas_call(
        body, name=name, in_specs=[hbm] * n_sh, out_specs=[hbm] * n_sh,
        out_shape=[_sds((N_CHIPS,) + s.shape, s.dtype) for s in srcs],
        scratch_shapes=[pltpu.SemaphoreType.DMA((n_ici + n_d2d,)), pltpu.SemaphoreType.DMA((n_ici + n_d2d,))],
    )(*srcs)
    return [lax.dynamic_update_index_in_dim(o, s, chip, 0).reshape((N_CHIPS,) + sh.shape)
            for o, s, sh in zip(outs, srcs, shards)]


def _sibling_split(gs):
    def ops_fn(in_refs, out_refs, pos):
        return [(REL_SIBLING, src.at[1 - pos[2]], dst) for src, dst in zip(in_refs, out_refs)]

    return _comm("grad_sibling_split", gs, [_sds(g.shape[1:], g.dtype) for g in gs], len(gs), ops_fn)


def _chip_exchange(ps):
    def ops_fn(in_refs, out_refs, pos):
        return [(rel, src.at[_chip_of(pos, rel)], dst.at[j])
                for src, dst in zip(in_refs, out_refs) for j, rel in enumerate(REL_CHIPS)]

    return _comm("grad_chip_exchange", ps, [_sds((N_CHIPS - 1,) + p.shape[1:], p.dtype) for p in ps], 3 * len(ps), ops_fn)


def _sibling_join(hs):
    def ops_fn(in_refs, out_refs, pos):
        return [(REL_SIBLING, src, dst) for src, dst in zip(in_refs, out_refs)]

    return _comm("grad_sibling_join", hs, [_sds(h.shape, h.dtype) for h in hs], len(hs), ops_fn)


def _rot_cols(w):
    return jnp.concatenate([-w[..., 32:], w[..., :32]], axis=-1)


def _unrot_cols(dw):
    return jnp.concatenate([dw[..., 32:], -dw[..., :32]], axis=-1)


IN_OFFS = (0, 1024, 2048, 2304, 2560, 2624, 3648, 4672)


def _w1_from_w_in(w):
    seg = [w[:, IN_OFFS[i]:IN_OFFS[i + 1]] for i in range(7)]
    rnn_x, rnn_gate, cq, ckv, kr, ga, gb = seg
    return jnp.concatenate([rnn_x, rnn_gate, ga, gb, cq, ckv, kr, _rot_cols(kr)], axis=1)


def _w_in_grad_from_parts(d_rx, d_g3, d_mla):
    kr = d_mla[:, 512:576] + _unrot_cols(d_mla[:, 576:640])
    return jnp.concatenate([d_rx, d_g3[:, 0:D], d_mla[:, 0:512], kr, d_g3[:, D:3 * D]], axis=1)


def _wq_from_w_uq(w):
    w3 = w.reshape(Q_LORA, N_HEADS, QK_NOPE + QK_ROPE)
    rope = w3[..., QK_NOPE:]
    return jnp.concatenate([w3[..., :QK_NOPE], rope, _rot_cols(rope)], axis=-1).reshape(Q_LORA, N_HEADS * HEAD_W)


def _w_uq_grad_from_wq(dw):
    d3 = dw.reshape(Q_LORA, N_HEADS, HEAD_W)
    rope = d3[..., 128:192] + _unrot_cols(d3[..., 192:256])
    return jnp.concatenate([d3[..., :128], rope], axis=-1).reshape(Q_LORA, N_HEADS * (QK_NOPE + QK_ROPE))


def _cols_from_chunks(g):
    return g.transpose(1, 0, 2).reshape(g.shape[1], N_CHIPS * g.shape[2])


def _halves_of_col_chunks(dw):
    r, c4 = dw.shape
    return dw.reshape(2, r // 2, N_CHIPS, c4 // N_CHIPS).transpose(0, 2, 1, 3)


def _halves_of_row_chunks(dw):
    r4, c = dw.shape
    return dw.reshape(N_CHIPS, 2, r4 // (2 * N_CHIPS), c).transpose(1, 0, 2, 3)


def kernel(x, norm_mix, w_in, conv_w, conv_b, lru_wa, lru_ba, lru_wx, lru_bx, lru_lambda, q_norm, w_uq, kv_norm, w_ukv, w_out, norm_mlp, w_up, w_down, norm_final, loss_target, m_norm_mix, m_w_in, m_conv_w, m_conv_b, m_lru_wa, m_lru_ba, m_lru_wx, m_lru_bx, m_lru_lambda, m_q_norm, m_w_uq, m_kv_norm, m_w_ukv, m_w_out, m_norm_mlp, m_w_up, m_w_down, m_norm_final, v_norm_mix, v_w_in, v_conv_w, v_conv_b, v_lru_wa, v_lru_ba, v_lru_wx, v_lru_bx, v_lru_lambda, v_q_norm, v_w_uq, v_kv_norm, v_w_ukv, v_w_out, v_norm_mlp, v_w_up, v_w_down, v_norm_final):
    t = x.shape[1]
    tm = min(256, t)
    tq = min(512, max(tm, t // 4))
    x2 = x[0]
    target = loss_target[0]
    chip = 2 * lax.axis_index("x") + lax.axis_index("y")
    core = lax.axis_index("c")
    chip_ix, core_ix = chip.reshape(1).astype(jnp.int32), core.reshape(1).astype(jnp.int32)
    row = lambda p: p.reshape(1, -1)

    big_shards = (w_in, w_uq, w_ukv, w_out, w_up, w_down)
    w_in_g, conv_w_g = _gather_halved([w_in.astype(BF16), conv_w], (1,), chip, "weight_gather_first")
    w1 = _w1_from_w_in(_cols_from_chunks(w_in_g))
    conv_w_f = _cols_from_chunks(conv_w_g)
    wa_b, wx_b = lru_wa.astype(BF16), lru_wx.astype(BF16)

    pos = jnp.arange(t, dtype=F32)
    inv_freq = 1.0 / (ROPE_THETA ** (jnp.arange(0, QK_ROPE, 2, dtype=F32) / QK_ROPE))
    ang = pos[:, None] * inv_freq[None, :]
    rope_c = jnp.concatenate([jnp.cos(ang), jnp.cos(ang), jnp.sin(ang), jnp.sin(ang)], axis=-1)

    (xn, rx, g3, cq, ckv, kr), gathered = _inproj(x2, row(norm_mix), w1, tm, [w.astype(BF16) for w in big_shards[1:]], chip)
    wq = _wq_from_w_uq(_cols_from_chunks(gathered[0]))
    wkv = _cols_from_chunks(gathered[1])
    w_out_f = gathered[2].reshape(D, D)
    w_up_f = _cols_from_chunks(gathered[3])
    w_down_f = gathered[4].reshape(D_FF, D)
    h, xa = _lru_fwd(rx, conv_w_f, row(conv_b), wa_b, row(lru_ba), wx_b, row(lru_bx), row(lru_lambda), tm)
    q, k, v, cqn, ckvn = _mla_proj(cq, ckv, kr, row(q_norm), row(kv_norm), wq, wkv, rope_c, tm)
    nq = t // tq
    yb, lse = _flash_fwd(q, k, v, tq)
    h1, merged = _merge_out(x2, h, g3, yb, w_out_f, tm)
    u, n2 = _mlp_up(h1, row(norm_mlp), w_up_f, tm)
    act, dh2, loss_blk, g_norm_final = _mlp_down_loss(u, h1, target, w_down_f, row(norm_final), tm)

    g_w_down = _matmul_tn(act, dh2, "grad_w_down")
    du = _mlp_bwd_act(dh2, u, w_down_f, tm)
    dh1, g_norm_mlp = _mlp_bwd_in(du, dh2, h1, w_up_f, row(norm_mlp), tm)
    g_w_up = _matmul_tn(n2, du, "grad_w_up")
    dg3, dyb, delta, dh, g_w_out = _merge_bwd(dh1, w_out_f, g3, h, yb, merged, tm)
    delta = delta.reshape(N_HEADS, 8, nq, tq).swapaxes(1, 2)
    dq, dk, dv = _flash_bwd(q, k, v, dyb, lse, delta, tq)
    dmla, g_wq, g_wkv, g_q_norm, g_kv_norm = _mla_bwd(dq, dk, dv, cqn, ckvn, cq, ckv, rope_c, wq, wkv, row(q_norm), row(kv_norm), tm)
    drx, g_conv_w, g_conv_b, g_wa, g_ba, g_wx, g_bx, g_lam = _lru_bwd(
        dh, xa, h, rx, conv_w_f, wa_b, row(lru_ba), wx_b, row(lru_bx), row(lru_lambda), tm)
    grad_x, g_norm_mix = _inproj_bwd(x2, dh1, drx, dg3, dmla, w1, row(norm_mix), tm)
    g_w_in = _w_in_grad_from_parts(_matmul_tn(xn, drx, "grad_w_in_rx"), _matmul_tn(xn, dg3, "grad_w_in_gates"),
                                   _matmul_tn(xn, dmla, "grad_w_in_mla"))
    g_w_uq = _w_uq_grad_from_wq(g_wq)

    smalls = (g_norm_mix, g_conv_b, g_wa, g_ba, g_wx, g_bx, g_lam, g_q_norm, g_kv_norm, g_norm_mlp, g_norm_final, g_conv_w)
    s_flat = jnp.concatenate([s.reshape(-1) for s in smalls] + [loss_blk[0, 0:1], jnp.zeros((S_LEN - N_SMALL - CONVW_SIZE - 1,), F32)])
    halves = [_halves_of_col_chunks(g_w_in), _halves_of_col_chunks(g_w_uq), _halves_of_col_chunks(g_wkv),
              _halves_of_row_chunks(g_w_out), _halves_of_col_chunks(g_w_up), _halves_of_row_chunks(g_w_down),
              s_flat.reshape(N_CHIPS, 2, S_ROWS_HALF, 128).transpose(1, 0, 2, 3)]
    theirs = _sibling_split(halves)
    parts = []
    for a, (hv, r) in enumerate(zip(halves, theirs)):
        dt = F32 if a == len(halves) - 1 else BF16
        pair = _pair_sum(hv.reshape(2, -1, hv.shape[-1]), r.reshape(-1, r.shape[-1]), core_ix, dt, f"grad_pair_sum_{a}")
        parts.append(pair.reshape(r.shape))
    received = _chip_exchange(parts)
    reduced = [_chip_sum(p, r, chip_ix, f"grad_chip_sum_{a}") for a, (p, r) in enumerate(zip(parts, received))]
    reduced_sibling = _sibling_join(reduced)
    s_mine, s_theirs = reduced[-1], reduced_sibling[-1]
    s_chunk = jnp.where(core == 0, jnp.concatenate([s_mine, s_theirs]), jnp.concatenate([s_theirs, s_mine]))
    s_all = _gather_chips([s_chunk], chip, "small_grad_gather")[0].reshape(-1)

    small_grads = []
    off = 0
    for shp, n in zip(SMALL_SHAPES, SMALL_SIZES):
        small_grads.append(s_all[off:off + n].reshape(shp))
        off += n
    g_conv_w_mine = lax.dynamic_slice_in_dim(s_all[off:off + CONVW_SIZE].reshape(4, D), chip * (D // N_CHIPS), D // N_CHIPS, axis=1)
    loss = s_all[off + CONVW_SIZE]

    big_m = (m_w_in, m_w_uq, m_w_ukv, m_w_out, m_w_up, m_w_down)
    big_v = (v_w_in, v_w_uq, v_w_ukv, v_w_out, v_w_up, v_w_down)
    big_names = ("w_in", "w_uq", "w_ukv", "w_out", "w_up", "w_down")
    big_upd = [_adamw_halves(w, gm, gt, m, v, core_ix, "adamw_" + n)
               for w, gm, gt, m, v, n in zip(big_shards, reduced, reduced_sibling, big_m, big_v, big_names)]

    small_w = (norm_mix, conv_b, lru_wa, lru_ba, lru_wx, lru_bx, lru_lambda, q_norm, kv_norm, norm_mlp, norm_final)
    small_m = (m_norm_mix, m_conv_b, m_lru_wa, m_lru_ba, m_lru_wx, m_lru_bx, m_lru_lambda, m_q_norm, m_kv_norm, m_norm_mlp, m_norm_final)
    small_v = (v_norm_mix, v_conv_b, v_lru_wa, v_lru_ba, v_lru_wx, v_lru_bx, v_lru_lambda, v_q_norm, v_kv_norm, v_norm_mlp, v_norm_final)

    def pack(items, last, fill):
        flat = jnp.concatenate([i.reshape(-1) for i in items] + [last.reshape(-1)])
        return jnp.concatenate([flat, jnp.full((PACK_ROWS * 128 - flat.shape[0],), fill, F32)]).reshape(PACK_ROWS, 128)

    packed = _adamw(pack(small_w, conv_w, 0.0), pack(small_grads, g_conv_w_mine, 0.0), pack(small_m, m_conv_w, 0.0),
                    pack(small_v, v_conv_w, 1.0), "adamw_small")

    def unpack(p):
        flat = p.reshape(-1)
        outs, o = [], 0
        for shp, n in zip(SMALL_SHAPES, SMALL_SIZES):
            outs.append(flat[o:o + n].reshape(shp))
            o += n
        return outs, flat[o:o + CONVW_SIZE // N_CHIPS].reshape(4, D // N_CHIPS)

    order = ("norm_mix", "w_in", "conv_w", "conv_b", "lru_wa", "lru_ba", "lru_wx", "lru_bx", "lru_lambda", "q_norm", "w_uq", "kv_norm",
             "w_ukv", "w_out", "norm_mlp", "w_up", "w_down", "norm_final")

    def assemble(small_list, conv_w_item, big_list):
        table = dict(zip(SMALL_NAMES, small_list))
        table["conv_w"] = conv_w_item
        table.update(zip(big_names, big_list))
        return [table[n] for n in order]

    outs = [loss, grad_x.reshape(1, t, D)]
    outs += assemble(small_grads, g_conv_w_mine, [b[0] for b in big_upd])
    for j in range(3):
        sm, cw = unpack(packed[j])
        outs += assemble(sm, cw, [b[j + 1] for b in big_upd])
    return tuple(outs)
```

```python
import functools
import math

import jax
import jax.numpy as jnp
from jax import lax
from jax.experimental import pallas as pl
from jax.experimental.pallas import tpu as pltpu

F32 = jnp.float32
BF16 = jnp.bfloat16

D = 1024
N_HEADS = 8
QK_NOPE = 128
QK_ROPE = 64
V_HEAD = 128
Q_LORA = 256
KV_LORA = 256
D_FF = 4096
RNN_BLOCKS = 8
RNN_BW = 128
LRU_C = 8.0
EPS = 1e-6
ROPE_THETA = 10000.0
HEAD_W = 256
KR_W = 128
W1_COLS = 4 * D + Q_LORA + KV_LORA + KR_W
SM_SCALE = (QK_NOPE + QK_ROPE) ** -0.5
NEG = float(jnp.finfo(jnp.float32).min)

ADAM_LR = 0.001
ADAM_B1 = 0.9
ADAM_B2 = 0.999
ADAM_EPS = 1e-08
ADAM_WD = 0.01
ADAM_STEP = 10

N_CHIPS = 4
V7X_VMEM_LIMIT = 56 * 1024 * 1024
MESH_ID = pl.DeviceIdType.MESH

SMALL_NAMES = ("norm_mix", "conv_b", "lru_wa", "lru_ba", "lru_wx", "lru_bx", "lru_lambda", "q_norm", "kv_norm", "norm_mlp", "norm_final")
SMALL_SHAPES = ((D,), (D,), (RNN_BLOCKS, RNN_BW, RNN_BW), (RNN_BLOCKS, RNN_BW), (RNN_BLOCKS, RNN_BW, RNN_BW), (RNN_BLOCKS, RNN_BW), (D,),
                (Q_LORA,), (KV_LORA,), (D,), (D,))
SMALL_SIZES = tuple(math.prod(s) for s in SMALL_SHAPES)
N_SMALL = sum(SMALL_SIZES)
CONVW_SIZE = 4 * D
S_LEN = -(-(N_SMALL + CONVW_SIZE) // 8192) * 8192
S_ROWS_HALF = S_LEN // (N_CHIPS * 2 * 128)
PACK_ROWS = -(-(N_SMALL + CONVW_SIZE // N_CHIPS) // (256 * 128)) * 256


def _pcall(body, name, grid, in_specs, out_specs, out_shape, scratch=()):
    return pl.pallas_call(
        body, name=name, grid=grid, in_specs=in_specs, out_specs=out_specs, out_shape=out_shape,
        scratch_shapes=list(scratch),
        compiler_params=pltpu.CompilerParams(dimension_semantics=("arbitrary",) * len(grid), vmem_limit_bytes=V7X_VMEM_LIMIT))


def _rows(tm, w):
    return pl.BlockSpec((tm, w), lambda i: (i, 0))


def _full(*shape):
    return pl.BlockSpec(shape, lambda *_: (0,) * len(shape))


def _sds(shape, dtype=F32):
    return jax.ShapeDtypeStruct(shape, dtype)


def _row_tile(rows, cap=256):
    t = min(rows, cap)
    while rows % t or t % 8:
        t -= 1
    return t


def _dot(a, b):
    return jnp.dot(a, b, preferred_element_type=F32)


def _dot_nt(a, b):
    return lax.dot_general(a, b, (((1,), (1,)), ((), ())), preferred_element_type=F32)


def _dot_tn(a, b):
    return lax.dot_general(a, b, (((0,), (0,)), ((), ())), preferred_element_type=F32)


def _sigmoid(x):
    return 1.0 / (1.0 + jnp.exp(-x))


_GELU_C = math.sqrt(2.0 / math.pi)


def _gelu(x):
    return x * (0.5 * (1.0 + jnp.tanh(_GELU_C * (x + 0.044715 * (x * x * x)))))


def _gelu_grad(x):
    t = jnp.tanh(_GELU_C * (x + 0.044715 * (x * x * x)))
    cdf = 0.5 * (1.0 + t)
    return cdf + x * (0.5 * (1.0 - t * t) * _GELU_C * (1.0 + 3.0 * 0.044715 * (x * x)))


def _rms_scale(x):
    return lax.rsqrt(jnp.mean(x * x, axis=-1, keepdims=True) + EPS)


def _rms_bwd(x, rs, g, dy):
    gdy = dy * g
    dx = rs * gdy - x * ((rs * rs * rs) * jnp.mean(gdy * x, axis=-1, keepdims=True))
    return dx, dy * (x * rs)


def _log1p(e):
    u = 1.0 + e
    d = u - 1.0
    return jnp.where(d == 0.0, e, jnp.log(u) * (e / jnp.where(d == 0.0, 1.0, d)))


def _softplus(y):
    return jnp.maximum(y, 0.0) + _log1p(jnp.exp(-jnp.abs(y)))


def _expm1(x):
    u = jnp.exp(x)
    lu = jnp.log(u)
    safe = jnp.where((u == 1.0) | (u == 0.0), 1.0, lu)
    return jnp.where(u == 1.0, x, jnp.where(u == 0.0, -1.0, (u - 1.0) * (x / safe)))


def _row_iota(shape):
    return lax.broadcasted_iota(jnp.int32, shape, 0)


def _lane_iota(shape):
    return lax.broadcasted_iota(jnp.int32, shape, 1)


def _scan_groups_fwd(a, b):
    sub = lax.broadcasted_iota(jnp.int32, a.shape, 1)
    for sh in (1, 2, 4):
        m = sub >= sh
        b = jnp.where(m, a * pltpu.roll(b, sh, 1) + b, b)
        a = jnp.where(m, a * pltpu.roll(a, sh, 1), a)
    return a, b


def _scan_groups_bwd(c, b):
    sub = lax.broadcasted_iota(jnp.int32, c.shape, 1)
    for sh in (1, 2, 4):
        m = sub < 8 - sh
        b = jnp.where(m, b + c * pltpu.roll(b, 8 - sh, 1), b)
        c = jnp.where(m, c * pltpu.roll(c, 8 - sh, 1), c)
    return c, b


def _rope_pair(gc):
    return gc + pltpu.roll(gc, 64, 1)


def _inproj(x, g, w1, tm, shards, chip):
    t = x.shape[0]
    nsteps = t // tm
    widths = (D, 3 * D, Q_LORA, KV_LORA, KR_W)
    srcs = [s.reshape(2, s.shape[0] // 2, s.shape[1]) for s in shards]
    n_sh = len(shards)

    def body(x_ref, g_ref, w_ref, *refs):
        sh_refs = refs[:n_sh]
        xn_ref, rx_ref, g3_ref, cq_ref, ckv_ref, kr_ref = refs[n_sh:n_sh + 6]
        gat_refs = refs[n_sh + 6:2 * n_sh + 6]
        sems = refs[2 * n_sh + 6:]
        pos = _mesh_pos()
        ici, d2d = _halved_gather_ops(pos, sh_refs, gat_refs, ())

        @pl.when(pl.program_id(0) == 0)
        def _():
            _start_copies(ici, sems, pos)

        xv = x_ref[...]
        xn = (xv * _rms_scale(xv) * g_ref[...]).astype(BF16)
        xn_ref[...] = xn
        col = 0
        for ref, w in zip((rx_ref, g3_ref, cq_ref, ckv_ref, kr_ref), widths):
            for c0 in range(0, w, 512):
                cw = min(512, w - c0)
                ref[:, c0:c0 + cw] = _dot(xn, w_ref[:, col + c0:col + c0 + cw])
            col += w

        @pl.when(pl.program_id(0) == nsteps - 1)
        def _():
            _wait_copies(ici, sems, pos)
            _start_copies(d2d, sems, pos, base=len(ici))
            _wait_copies(d2d, sems, pos, base=len(ici))

    hbm = pl.BlockSpec(memory_space=pl.ANY)
    outs = pl.pallas_call(
        body, name="inproj", grid=(nsteps,),
        in_specs=[_rows(tm, D), _full(1, D), _full(D, W1_COLS)] + [hbm] * n_sh,
        out_specs=[_rows(tm, D)] + [_rows(tm, w) for w in widths] + [hbm] * n_sh,
        out_shape=[_sds((t, D), BF16)] + [_sds((t, w)) for w in widths] + [_sds((N_CHIPS,) + s.shape, s.dtype) for s in srcs],
        scratch_shapes=[pltpu.SemaphoreType.DMA((6 * n_sh,)), pltpu.SemaphoreType.DMA((6 * n_sh,))],
        compiler_params=pltpu.CompilerParams(dimension_semantics=("arbitrary",), vmem_limit_bytes=V7X_VMEM_LIMIT),
    )(x, g, w1, *srcs)
    gathered = [lax.dynamic_update_index_in_dim(o, s, chip, 0).reshape((N_CHIPS,) + sh.shape)
                for o, s, sh in zip(outs[6:], srcs, shards)]
    return outs[:6], gathered


def _lru_gates(xa, wa_ref, ba, wx_ref, bx, pre_r, pre_i):
    xb = xa.astype(BF16)
    for n in range(RNN_BLOCKS):
        sl = slice(n * RNN_BW, (n + 1) * RNN_BW)
        pre_r[:, sl] = _dot(xb[:, sl], wa_ref[n])
        pre_i[:, sl] = _dot(xb[:, sl], wx_ref[n])
    r = _sigmoid(pre_r[...] + ba)
    i = _sigmoid(pre_i[...] + bx)
    return r, i


def _lru_fwd(rx, conv_w, conv_b, wa, ba, wx, bx, lam, tb):
    t = rx.shape[0]
    nb = t // tb

    def body(x_ref, xp_ref, cw_ref, cb_ref, wa_ref, ba_ref, wx_ref, bx_ref, lam_ref, h_ref, xa_ref, hc, tmp, pre_r, pre_i):
        i_blk = pl.program_id(0)

        @pl.when(i_blk == 0)
        def _():
            hc[...] = jnp.zeros_like(hc)

        xv = x_ref[...]
        xp = jnp.where(i_blk > 0, xp_ref[...], 0.0)
        row8 = _row_iota((8, D))
        xa = cb_ref[...] + cw_ref[3:4, :] * xv
        for s in (1, 2, 3):
            xr = pltpu.roll(xv, s, 0)
            tmp[...] = xr
            tmp[0:8, :] = jnp.where(row8 < s, pltpu.roll(xp, s, 0), xr[0:8, :])
            xa = xa + cw_ref[3 - s:4 - s, :] * tmp[...]
        xa_ref[...] = xa
        r, gi = _lru_gates(xa, wa_ref, ba_ref[...], wx_ref, bx_ref[...], pre_r, pre_i)
        la = (-LRU_C * _softplus(-lam_ref[...])) * r
        a = jnp.exp(la)
        b = jnp.sqrt(-_expm1(2.0 * la)) * (gi * xa)
        a3, b3 = _scan_groups_fwd(a.reshape(tb // 8, 8, D), b.reshape(tb // 8, 8, D))
        carry = hc[...]
        for grp in range(tb // 8):
            hg = a3[grp] * carry + b3[grp]
            h_ref[8 * grp:8 * grp + 8, :] = hg
            carry = hg[7:8, :]
        hc[...] = carry

    prev8 = pl.BlockSpec((8, D), lambda i: (jnp.maximum(i * (tb // 8) - 1, 0), 0))
    return _pcall(
        body, "lru_fwd", (nb,),
        [_rows(tb, D), prev8, _full(4, D), _full(1, D), _full(RNN_BLOCKS, RNN_BW, RNN_BW), _full(1, D),
         _full(RNN_BLOCKS, RNN_BW, RNN_BW), _full(1, D), _full(1, D)],
        [_rows(tb, D), _rows(tb, D)],
        [_sds((t, D)), _sds((t, D))],
        scratch=[pltpu.VMEM((1, D), F32), pltpu.VMEM((tb, D), F32), pltpu.VMEM((tb, D), F32), pltpu.VMEM((tb, D), F32)],
    )(rx, rx, conv_w, conv_b, wa, ba, wx, bx, lam)


def _mla_proj(cq, ckv, kr, qn, kvn, wq, wkv, rope_c, tm):
    t = cq.shape[0]

    def body(cq_ref, ckv_ref, kr_ref, qn_ref, kvn_ref, wq_ref, wkv_ref, c_ref, q_ref, k_ref, v_ref, cqn_ref, ckvn_ref):
        cqv = cq_ref[...]
        cqn = (cqv * _rms_scale(cqv) * qn_ref[...]).astype(BF16)
        ckvv = ckv_ref[...]
        ckvn = (ckvv * _rms_scale(ckvv) * kvn_ref[...]).astype(BF16)
        cqn_ref[...] = cqn
        ckvn_ref[...] = ckvn
        c = c_ref[...]
        lane = _lane_iota((tm, KR_W))
        kro = jnp.where(lane < 64, _rope_pair(kr_ref[...] * c), 0.0).astype(BF16)
        for h in range(N_HEADS):
            sl = slice(h * HEAD_W, (h + 1) * HEAD_W)
            qh = _dot(cqn, wq_ref[:, sl])
            q_ref[h, :, 0:128] = qh[:, 0:128].astype(BF16)
            q_ref[h, :, 128:256] = _rope_pair(qh[:, 128:256] * c).astype(BF16)
            kvh = _dot(ckvn, wkv_ref[:, sl])
            k_ref[h, :, 0:128] = kvh[:, 0:128].astype(BF16)
            k_ref[h, :, 128:256] = kro
            v_ref[h, :, 0:V_HEAD] = kvh[:, 128:256].astype(BF16)
            v_ref[h, :, V_HEAD:2 * V_HEAD] = jnp.ones((tm, V_HEAD), BF16)

    hb = lambda w: pl.BlockSpec((N_HEADS, tm, w), lambda i: (0, i, 0))
    return _pcall(
        body, "mla_proj", (t // tm,),
        [_rows(tm, Q_LORA), _rows(tm, KV_LORA), _rows(tm, KR_W), _full(1, Q_LORA), _full(1, KV_LORA),
         _full(Q_LORA, N_HEADS * HEAD_W), _full(KV_LORA, N_HEADS * HEAD_W), _rows(tm, KR_W)],
        [hb(HEAD_W), hb(HEAD_W), hb(2 * V_HEAD), _rows(tm, Q_LORA), _rows(tm, KV_LORA)],
        [_sds((N_HEADS, t, HEAD_W), BF16), _sds((N_HEADS, t, HEAD_W), BF16), _sds((N_HEADS, t, 2 * V_HEAD), BF16),
         _sds((t, Q_LORA), BF16), _sds((t, KV_LORA), BF16)],
    )(cq, ckv, kr, qn, kvn, wq, wkv, rope_c)


EXP2_SCALE = SM_SCALE * math.log2(math.e)


def _flash_fwd(q, k, v, tq):
    t = q.shape[1]
    nq = t // tq

    def body(q_ref, k_ref, v_ref, o_ref, lse_ref, s_even, s_odd):
        qi = pl.program_id(1)
        qv = q_ref[0]

        def scores(ki, buf):
            buf[...] = _dot_nt(qv, k_ref[0, pl.ds(pl.multiple_of(ki * tq, tq), tq), :])

        def softmax_pv(ki, buf, carry, diagonal):
            m, acc = carry
            s = buf[...]
            if diagonal:
                s = jnp.where(_row_iota((tq, tq)) >= _lane_iota((tq, tq)), s, NEG)
            m_new = jnp.maximum(m, jnp.max(s, axis=1, keepdims=True))
            p = jnp.exp2((s - m_new) * EXP2_SCALE)
            alpha = jnp.exp2((m - m_new) * EXP2_SCALE)
            acc = alpha * acc + _dot(p.astype(BF16), v_ref[0, pl.ds(pl.multiple_of(ki * tq, tq), tq), :])
            return m_new, acc

        def finish(carry):
            m, acc = carry
            l = acc[:, V_HEAD:2 * V_HEAD]
            o_ref[...] = acc[:, 0:V_HEAD] / l
            lse = m * EXP2_SCALE + jnp.log(l) * math.log2(math.e)
            lse_ref[0, 0] = jnp.transpose(lse)[0:8, :]

        def two(i, carry):
            scores(2 * i + 1, s_odd)
            carry = softmax_pv(2 * i, s_even, carry, False)
            scores(2 * i + 2, s_even)
            return softmax_pv(2 * i + 1, s_odd, carry, False)

        init = (jnp.full((tq, 1), -jnp.inf, F32), jnp.zeros((tq, 2 * V_HEAD), F32))
        scores(0, s_even)
        carry = lax.fori_loop(0, qi // 2, two, init)

        @pl.when(qi % 2 == 0)
        def _():
            finish(softmax_pv(qi, s_even, carry, True))

        @pl.when(qi % 2 == 1)
        def _():
            scores(qi, s_odd)
            finish(softmax_pv(qi, s_odd, softmax_pv(qi - 1, s_even, carry, False), True))

    head = lambda w: pl.BlockSpec((1, t, w), lambda h, qi: (h, 0, 0))
    return _pcall(
        body, "flash_fwd", (N_HEADS, nq),
        [pl.BlockSpec((1, tq, HEAD_W), lambda h, qi: (h, qi, 0)), head(HEAD_W), head(2 * V_HEAD)],
        [pl.BlockSpec((tq, V_HEAD), lambda h, qi: (qi, h)), pl.BlockSpec((1, 1, 8, tq), lambda h, qi: (h, qi, 0, 0))],
        [_sds((t, D)), _sds((N_HEADS, nq, 8, tq))],
        scratch=[pltpu.VMEM((tq, tq), F32), pltpu.VMEM((tq, tq), F32)],
    )(q, k, v)


def _merge_out(x, h, g3, yb, w_out, tm):
    t = x.shape[0]

    def body(x_ref, h_ref, g3_ref, yb_ref, w_ref, h1_ref, mg_ref):
        ya = h_ref[...] * _gelu(g3_ref[:, 0:D])
        merged = (_sigmoid(g3_ref[:, D:2 * D]) * ya + _sigmoid(g3_ref[:, 2 * D:3 * D]) * yb_ref[...]).astype(BF16)
        mg_ref[...] = merged
        h1_ref[...] = x_ref[...] + _dot(merged, w_ref[...])

    return _pcall(
        body, "merge_out", (t // tm,),
        [_rows(tm, D), _rows(tm, D), _rows(tm, 3 * D), _rows(tm, D), _full(D, D)],
        [_rows(tm, D), _rows(tm, D)],
        [_sds((t, D)), _sds((t, D), BF16)],
    )(x, h, g3, yb, w_out)


def _mlp_up(h1, g, w_up, tm):
    t = h1.shape[0]

    def body(h_ref, g_ref, w_ref, u_ref, n2_ref):
        hv = h_ref[...]
        n2 = (hv * _rms_scale(hv) * g_ref[...]).astype(BF16)
        n2_ref[...] = n2
        for c0 in range(0, D_FF, 512):
            u_ref[:, c0:c0 + 512] = _dot(n2, w_ref[:, c0:c0 + 512])

    return _pcall(
        body, "mlp_up", (t // tm,),
        [_rows(tm, D), _full(1, D), _full(D, D_FF)],
        [_rows(tm, D_FF), _rows(tm, D)],
        [_sds((t, D_FF)), _sds((t, D), BF16)],
    )(h1, g, w_up)


def _mlp_down_loss(u, h1, target, w_down, g, tm):
    t = u.shape[0]

    def body(u_ref, h1_ref, tg_ref, w_ref, g_ref, act_ref, dh2_ref, loss_ref, gnf_ref, lacc):
        i = pl.program_id(0)

        @pl.when(i == 0)
        def _():
            lacc[...] = jnp.zeros_like(lacc)
            gnf_ref[...] = jnp.zeros_like(gnf_ref)

        ru = jnp.maximum(u_ref[...], 0.0)
        act = (ru * ru).astype(BF16)
        act_ref[...] = act
        h2 = h1_ref[...] + _dot(act, w_ref[...])
        rs = _rms_scale(h2)
        gv = g_ref[...]
        err = h2 * rs * gv - tg_ref[...]
        lacc[...] += jnp.sum(err * err, axis=0, keepdims=True)
        dx, dgr = _rms_bwd(h2, rs, gv, err * (1.0 / D))
        dh2_ref[...] = dx
        gnf_ref[...] += jnp.sum(dgr, axis=0, keepdims=True)

        @pl.when(i == pl.num_programs(0) - 1)
        def _():
            loss_ref[...] = jnp.broadcast_to(jnp.sum(lacc[...], axis=1, keepdims=True) * (0.5 / D), (8, 128))

    return _pcall(
        body, "mlp_down_loss", (t // tm,),
        [_rows(tm, D_FF), _rows(tm, D), _rows(tm, D), _full(D_FF, D), _full(1, D)],
        [_rows(tm, D_FF), _rows(tm, D), _full(8, 128), _full(1, D)],
        [_sds((t, D_FF), BF16), _sds((t, D)), _sds((8, 128)), _sds((1, D))],
        scratch=[pltpu.VMEM((1, D), F32)],
    )(u, h1, target, w_down, g)


def _matmul_tn(a, g, name, chunked=None):
    t, kdim = a.shape
    ndim = g.shape[1]
    tk = min(kdim, 1024)
    tn = ndim if ndim <= 1024 else 1024
    if chunked == "cols":
        tk, tn = kdim // 2, ndim // N_CHIPS
    elif chunked == "rows":
        tk = kdim // (2 * N_CHIPS)
    tt = min(t, 512)
    nt = t // tt

    def body(a_ref, g_ref, o_ref):
        @pl.when(pl.program_id(2) == 0)
        def _():
            o_ref[...] = jnp.zeros_like(o_ref)

        o_ref[...] += _dot_tn(a_ref[...].astype(BF16), g_ref[...].astype(BF16))

    if chunked == "cols":
        out_spec, out_shape = pl.BlockSpec((None, None, tk, tn), lambda i, j, s: (i, j, 0, 0)), _sds((2, N_CHIPS, tk, tn))
    elif chunked == "rows":
        out_spec, out_shape = pl.BlockSpec((None, None, tk, tn), lambda i, j, s: (i % 2, i // 2, 0, j)), _sds((2, N_CHIPS, tk, ndim))
    else:
        out_spec, out_shape = pl.BlockSpec((tk, tn), lambda i, j, s: (i, j)), _sds((kdim, ndim))
    return _pcall(
        body, name, (kdim // tk, ndim // tn, nt),
        [pl.BlockSpec((tt, tk), lambda i, j, s: (s, i)), pl.BlockSpec((tt, tn), lambda i, j, s: (s, j))],
        out_spec, out_shape,
    )(a, g)


def _mlp_bwd_act(dh2, u, w_down, tm):
    t = u.shape[0]

    def body(d_ref, u_ref, w_ref, du_ref):
        db = d_ref[...].astype(BF16)
        for c0 in range(0, D_FF, 512):
            da = _dot_nt(db, w_ref[c0:c0 + 512, :])
            du_ref[:, c0:c0 + 512] = (da * (2.0 * jnp.maximum(u_ref[:, c0:c0 + 512], 0.0))).astype(BF16)

    return _pcall(
        body, "mlp_bwd_act", (t // tm,),
        [_rows(tm, D), _rows(tm, D_FF), _full(D_FF, D)],
        _rows(tm, D_FF), _sds((t, D_FF), BF16),
    )(dh2, u, w_down)


def _mlp_bwd_in(du, dh2, h1, w_up, g, tm):
    t = du.shape[0]

    def body(du_ref, d_ref, h_ref, w_ref, g_ref, dh1_ref, gacc_ref):
        @pl.when(pl.program_id(0) == 0)
        def _():
            gacc_ref[...] = jnp.zeros_like(gacc_ref)

        dn2 = _dot_nt(du_ref[...], w_ref[...])
        hv = h_ref[...]
        dx, dgr = _rms_bwd(hv, _rms_scale(hv), g_ref[...], dn2)
        dh1_ref[...] = d_ref[...] + dx
        gacc_ref[...] += jnp.sum(dgr, axis=0, keepdims=True)

    return _pcall(
        body, "mlp_bwd_in", (t // tm,),
        [_rows(tm, D_FF), _rows(tm, D), _rows(tm, D), _full(D, D_FF), _full(1, D)],
        [_rows(tm, D), _full(1, D)],
        [_sds((t, D)), _sds((1, D))],
    )(du, dh2, h1, w_up, g)


def _merge_bwd(dh1, w_out, g3, h, yb, merged, tm):
    t = dh1.shape[0]

    def body(d_ref, w_ref, g3_ref, h_ref, yb_ref, mg_ref, dg3_ref, dyb_ref, dl_ref, dh_ref, dwo_ref):
        @pl.when(pl.program_id(0) == 0)
        def _():
            dwo_ref[...] = jnp.zeros_like(dwo_ref)

        db = d_ref[...].astype(BF16)
        dwo_ref[...] += _dot_tn(mg_ref[...], db)
        dm = _dot_nt(db, w_ref[...])
        gv = g3_ref[:, 0:D]
        sa = _sigmoid(g3_ref[:, D:2 * D])
        sb = _sigmoid(g3_ref[:, 2 * D:3 * D])
        gel = _gelu(gv)
        hv = h_ref[...]
        ybv = yb_ref[...]
        dya = dm * sa
        dyb = dm * sb
        dg3_ref[:, 0:D] = (dya * hv * _gelu_grad(gv)).astype(BF16)
        dg3_ref[:, D:2 * D] = (dya * (hv * gel) * (1.0 - sa)).astype(BF16)
        dg3_ref[:, 2 * D:3 * D] = (dyb * ybv * (1.0 - sb)).astype(BF16)
        dh_ref[...] = dya * gel
        dyb_ref[...] = dyb.astype(BF16)
        prod = dyb * ybv
        ones = jnp.ones((8, V_HEAD), F32)
        for hh in range(N_HEADS):
            dl_ref[hh] = lax.dot_general(ones, prod[:, hh * V_HEAD:(hh + 1) * V_HEAD], (((1,), (1,)), ((), ())),
                                         precision=lax.Precision.HIGHEST, preferred_element_type=F32)

    return _pcall(
        body, "merge_bwd", (t // tm,),
        [_rows(tm, D), _full(D, D), _rows(tm, 3 * D), _rows(tm, D), _rows(tm, D), _rows(tm, D)],
        [_rows(tm, 3 * D), _rows(tm, D), pl.BlockSpec((N_HEADS, 8, tm), lambda i: (0, 0, i)), _rows(tm, D), _full(D, D)],
        [_sds((t, 3 * D), BF16), _sds((t, D), BF16), _sds((N_HEADS, 8, t)), _sds((t, D)), _sds((D, D))],
    )(dh1, w_out, g3, h, yb, merged)


def _flash_bwd(q, k, v, do, lse, delta, tq):
    t = q.shape[1]
    nq = t // tq

    def body(q_ref, k_ref, v_ref, do_ref, lse_ref, dl_ref, dqt_ref, dk_ref, dv_ref):
        ki = pl.program_id(1)

        @pl.when(ki == 0)
        def _():
            dqt_ref[...] = jnp.zeros_like(dqt_ref)

        kblk, vblk = k_ref[0], v_ref[0]
        kt = jnp.transpose(kblk)
        dk_ref[...] = jnp.zeros_like(dk_ref)
        dv_ref[...] = jnp.zeros_like(dv_ref)

        def block(qi, diagonal):
            rows = pl.ds(pl.multiple_of(qi * tq, tq), tq)
            qv, dov = q_ref[0, rows, :], do_ref[rows, :]
            p = jnp.exp2(_dot_nt(kblk, qv) * EXP2_SCALE - lse_ref[0, qi, 0:1, :])
            if diagonal:
                p = jnp.where(_lane_iota((tq, tq)) >= _row_iota((tq, tq)), p, 0.0)
            dv_ref[0] += _dot(p.astype(BF16), dov)
            dp = _dot_nt(vblk, dov)
            ds = (p * (dp - dl_ref[0, qi, 0:1, :]) * SM_SCALE).astype(BF16)
            dk_ref[0] += _dot(ds, qv)
            dqt_ref[0, qi] += _dot(kt, ds)

        block(ki, True)

        def two(i, carry):
            block(ki + 1 + 2 * i, False)
            block(ki + 2 + 2 * i, False)
            return carry

        def one(qi, carry):
            block(qi, False)
            return carry

        pairs = (nq - 1 - ki) // 2
        lax.fori_loop(0, pairs, two, 0)
        lax.fori_loop(ki + 1 + 2 * pairs, nq, one, 0)

    kv_spec = lambda w: pl.BlockSpec((1, tq, w), lambda h, ki: (h, ki, 0))
    stat = pl.BlockSpec((1, nq, 8, tq), lambda h, ki: (h, 0, 0, 0))
    return _pcall(
        body, "flash_bwd", (N_HEADS, nq),
        [pl.BlockSpec((1, t, HEAD_W), lambda h, ki: (h, 0, 0)), kv_spec(HEAD_W), kv_spec(V_HEAD),
         pl.BlockSpec((t, V_HEAD), lambda h, ki: (0, h)), stat, stat],
        [pl.BlockSpec((1, nq, HEAD_W, tq), lambda h, ki: (h, 0, 0, 0)), kv_spec(HEAD_W), kv_spec(V_HEAD)],
        [_sds((N_HEADS, nq, HEAD_W, tq)), _sds((N_HEADS, t, HEAD_W)), _sds((N_HEADS, t, V_HEAD))],
    )(q, k, v, do, lse, delta)


def _mla_bwd(dqt, dk, dv, cqn, ckvn, cq, ckv, rope_c, wq, wkv, qn, kvn, tm):
    t = cq.shape[0]

    def body(dq_ref, dk_ref, dv_ref, cqn_ref, ckvn_ref, cq_ref, ckv_ref, c_ref, wq_ref, wkv_ref, qn_ref, kvn_ref,
             dmla_ref, dwq_ref, dwkv_ref, dqn_ref, dkvn_ref):
        @pl.when(pl.program_id(0) == 0)
        def _():
            dwq_ref[...] = jnp.zeros_like(dwq_ref)
            dwkv_ref[...] = jnp.zeros_like(dwkv_ref)
            dqn_ref[...] = jnp.zeros_like(dqn_ref)
            dkvn_ref[...] = jnp.zeros_like(dkvn_ref)

        c = c_ref[...]
        lane = _lane_iota((tm, KR_W))
        cqn, ckvn = cqn_ref[...], ckvn_ref[...]
        dcqn = jnp.zeros((tm, Q_LORA), F32)
        dckvn = jnp.zeros((tm, KV_LORA), F32)
        dkr = jnp.zeros((tm, KR_W), F32)
        for h in range(N_HEADS):
            sl = slice(h * HEAD_W, (h + 1) * HEAD_W)
            dqh = jnp.transpose(dq_ref[h, 0])
            droped = jnp.where(lane < 64, dqh[:, 128:256], 0.0)
            dqp = jnp.concatenate([dqh[:, 0:128], _rope_pair(droped) * c], axis=1).astype(BF16)
            dcqn = dcqn + _dot_nt(dqp, wq_ref[:, sl])
            dwq_ref[:, sl] += _dot_tn(cqn, dqp)
            dkr = dkr + jnp.where(lane < 64, dk_ref[h, :, 128:256], 0.0)
            dkvp = jnp.concatenate([dk_ref[h, :, 0:128], dv_ref[h]], axis=1).astype(BF16)
            dckvn = dckvn + _dot_nt(dkvp, wkv_ref[:, sl])
            dwkv_ref[:, sl] += _dot_tn(ckvn, dkvp)
        cqv, ckvv = cq_ref[...], ckv_ref[...]
        dcq, dgq = _rms_bwd(cqv, _rms_scale(cqv), qn_ref[...], dcqn)
        dckv, dgkv = _rms_bwd(ckvv, _rms_scale(ckvv), kvn_ref[...], dckvn)
        dqn_ref[...] += jnp.sum(dgq, axis=0, keepdims=True)
        dkvn_ref[...] += jnp.sum(dgkv, axis=0, keepdims=True)
        dmla_ref[:, 0:256] = dcq.astype(BF16)
        dmla_ref[:, 256:512] = dckv.astype(BF16)
        dmla_ref[:, 512:640] = (_rope_pair(dkr) * c).astype(BF16)

    hb = lambda w: pl.BlockSpec((N_HEADS, tm, w), lambda i: (0, i, 0))
    wide = N_HEADS * HEAD_W
    per_q = dqt.shape[3] // tm
    dq_spec = pl.BlockSpec((N_HEADS, 1, HEAD_W, tm), lambda i: (0, i // per_q, 0, i % per_q))
    return _pcall(
        body, "mla_bwd", (t // tm,),
        [dq_spec, hb(HEAD_W), hb(V_HEAD), _rows(tm, Q_LORA), _rows(tm, KV_LORA), _rows(tm, Q_LORA), _rows(tm, KV_LORA),
         _rows(tm, KR_W), _full(Q_LORA, wide), _full(KV_LORA, wide), _full(1, Q_LORA), _full(1, KV_LORA)],
        [_rows(tm, 640), _full(Q_LORA, wide), _full(KV_LORA, wide), _full(1, Q_LORA), _full(1, KV_LORA)],
        [_sds((t, 640), BF16), _sds((Q_LORA, wide)), _sds((KV_LORA, wide)), _sds((1, Q_LORA)), _sds((1, KV_LORA))],
    )(dqt, dk, dv, cqn, ckvn, cq, ckv, rope_c, wq, wkv, qn, kvn)


def _lru_bwd(dh, xa, h, rx, conv_w, wa, ba, wx, bx, lam, tb):
    t = dh.shape[0]
    nb = t // tb

    def body(dh_ref, xa_ref, h_ref, hp_ref, x_ref, cw_ref, wa_ref, ba_ref, wx_ref, bx_ref, lam_ref,
             drx_ref, dcw_ref, dcb_ref, dwa_ref, dba_ref, dwx_ref, dbx_ref, dlam_ref, gc, dxn, tmp, pre_r, pre_i):
        step = pl.program_id(0)
        first_block = step == nb - 1

        @pl.when(step == 0)
        def _():
            gc[...] = jnp.zeros_like(gc)
            dxn[...] = jnp.zeros_like(dxn)
            for ref in (dcw_ref, dcb_ref, dwa_ref, dba_ref, dwx_ref, dbx_ref, dlam_ref):
                ref[...] = jnp.zeros_like(ref)

        xa = xa_ref[...]
        r, gi = _lru_gates(xa, wa_ref, ba_ref[...], wx_ref, bx_ref[...], pre_r, pre_i)
        lamv = lam_ref[...]
        sp = _softplus(-lamv)
        la = (-LRU_C * sp) * r
        a = jnp.exp(la)
        e2 = _expm1(2.0 * la)
        sq = jnp.sqrt(-e2)
        row = _row_iota((tb, D))
        cf = jnp.where(row == tb - 1, 1.0, pltpu.roll(a, tb - 1, 0))
        c3, b3 = _scan_groups_bwd(cf.reshape(tb // 8, 8, D), dh_ref[...].reshape(tb // 8, 8, D))
        carry = gc[...]
        for grp in reversed(range(tb // 8)):
            dg = b3[grp] + c3[grp] * carry
            pre_r[8 * grp:8 * grp + 8, :] = dg
            carry = dg[0:1, :]
        delta = pre_r[...]
        gc[...] = a[0:1, :] * carry
        hv = h_ref[...]
        hr = pltpu.roll(hv, 1, 0)
        tmp[...] = hr
        tmp[0:1, :] = jnp.where(first_block, 0.0, hp_ref[7:8, :])
        hprev = tmp[...]
        ix = gi * xa
        dla = (delta * hprev) * a - (delta * ix) * ((e2 + 1.0) / sq)
        dlam_ref[...] += jnp.sum(dla * r, axis=0, keepdims=True) * (LRU_C * _sigmoid(-lamv))
        dpr = (dla * (-LRU_C * sp)) * r * (1.0 - r)
        dsq = delta * sq
        dpi = (dsq * xa) * gi * (1.0 - gi)
        dba_ref[...] += jnp.sum(dpr, axis=0, keepdims=True)
        dbx_ref[...] += jnp.sum(dpi, axis=0, keepdims=True)
        pre_r[...] = dpr
        pre_i[...] = dpi
        xb = xa.astype(BF16)
        for n in range(RNN_BLOCKS):
            sl = slice(n * RNN_BW, (n + 1) * RNN_BW)
            dprn = pre_r[:, sl].astype(BF16)
            dpin = pre_i[:, sl].astype(BF16)
            dwa_ref[n] += _dot_tn(xb[:, sl], dprn)
            dwx_ref[n] += _dot_tn(xb[:, sl], dpin)
            tmp[:, sl] = _dot_nt(dprn, wa_ref[n]) + _dot_nt(dpin, wx_ref[n])
        dxa = dsq * gi + tmp[...]
        dcb_ref[...] += jnp.sum(dxa, axis=0, keepdims=True)
        xv = x_ref[...]
        drx = cw_ref[3:4, :] * dxa
        dcw_ref[3:4, :] += jnp.sum(dxa * xv, axis=0, keepdims=True)
        row8 = _row_iota((8, D))
        nxt = dxn[...]
        for s in (1, 2, 3):
            dr_ = pltpu.roll(dxa, tb - s, 0)
            tmp[...] = dr_
            tmp[tb - 8:tb, :] = jnp.where(row8 >= 8 - s, pltpu.roll(nxt, 8 - s, 0), dr_[tb - 8:tb, :])
            dxs = tmp[...]
            drx = drx + cw_ref[3 - s:4 - s, :] * dxs
            dcw_ref[3 - s:4 - s, :] += jnp.sum(dxs * xv, axis=0, keepdims=True)
        drx_ref[...] = drx.astype(BF16)
        dxn[...] = dxa[0:8, :]

    rev = pl.BlockSpec((tb, D), lambda i: (nb - 1 - i, 0))
    prev8 = pl.BlockSpec((8, D), lambda i: (jnp.maximum((nb - 1 - i) * (tb // 8) - 1, 0), 0))
    wblk = _full(RNN_BLOCKS, RNN_BW, RNN_BW)
    return _pcall(
        body, "lru_bwd", (nb,),
        [rev, rev, rev, prev8, rev, _full(4, D), wblk, _full(1, D), wblk, _full(1, D), _full(1, D)],
        [rev, _full(4, D), _full(1, D), wblk, _full(1, D), wblk, _full(1, D), _full(1, D)],
        [_sds((t, D), BF16), _sds((4, D)), _sds((1, D)), _sds((RNN_BLOCKS, RNN_BW, RNN_BW)), _sds((1, D)),
         _sds((RNN_BLOCKS, RNN_BW, RNN_BW)), _sds((1, D)), _sds((1, D))],
        scratch=[pltpu.VMEM((1, D), F32), pltpu.VMEM((8, D), F32), pltpu.VMEM((tb, D), F32), pltpu.VMEM((tb, D), F32),
                 pltpu.VMEM((tb, D), F32)],
    )(dh, xa, h, h, rx, conv_w, wa, ba, wx, bx, lam)


def _inproj_bwd(x, dh1, drx, dg3, dmla, w1, g, tm):
    t = x.shape[0]

    def body(x_ref, d_ref, drx_ref, dg3_ref, dmla_ref, w_ref, g_ref, dx_ref, gacc_ref):
        @pl.when(pl.program_id(0) == 0)
        def _():
            gacc_ref[...] = jnp.zeros_like(gacc_ref)

        dxn = _dot_nt(drx_ref[...], w_ref[:, 0:D])
        for c0 in range(0, 3 * D, D):
            dxn = dxn + _dot_nt(dg3_ref[:, c0:c0 + D], w_ref[:, D + c0:2 * D + c0])
        dxn = dxn + _dot_nt(dmla_ref[...], w_ref[:, 4 * D:W1_COLS])
        xv = x_ref[...]
        dx, dgr = _rms_bwd(xv, _rms_scale(xv), g_ref[...], dxn)
        dx_ref[...] = d_ref[...] + dx
        gacc_ref[...] += jnp.sum(dgr, axis=0, keepdims=True)

    return _pcall(
        body, "inproj_bwd", (t // tm,),
        [_rows(tm, D), _rows(tm, D), _rows(tm, D), _rows(tm, 3 * D), _rows(tm, 640), _full(D, W1_COLS), _full(1, D)],
        [_rows(tm, D), _full(1, D)],
        [_sds((t, D)), _sds((1, D))],
    )(x, dh1, drx, dg3, dmla, w1, g)


def _pcall_indexed(body, name, index, grid, in_specs, out_specs, out_shape):
    call = pl.pallas_call(
        body, name=name, out_shape=out_shape,
        grid_spec=pltpu.PrefetchScalarGridSpec(num_scalar_prefetch=1, grid=grid, in_specs=in_specs, out_specs=out_specs),
        compiler_params=pltpu.CompilerParams(dimension_semantics=("arbitrary",) * len(grid), vmem_limit_bytes=V7X_VMEM_LIMIT))
    return lambda *operands: call(index, *operands)


def _pair_sum(halves, theirs, core, out_dtype, name):
    _, rows, cols = halves.shape
    tm = _row_tile(rows)

    def body(c_ref, a_ref, b_ref, o_ref):
        o_ref[...] = (a_ref[0] + b_ref[...]).astype(out_dtype)

    plain = pl.BlockSpec((tm, cols), lambda i, c: (i, 0))
    return _pcall_indexed(body, name, core, (rows // tm,),
                          [pl.BlockSpec((1, tm, cols), lambda i, c: (c[0], i, 0)), plain], plain,
                          _sds((rows, cols), out_dtype))(halves, theirs)


def _chip_sum(parts, recv, chip, name):
    _, rows, cols = parts.shape
    tm = _row_tile(rows)

    def body(c_ref, a_ref, r_ref, o_ref):
        o_ref[...] = ((a_ref[0].astype(F32) + r_ref[0].astype(F32)) + r_ref[1].astype(F32)) + r_ref[2].astype(F32)

    return _pcall_indexed(body, name, chip, (rows // tm,),
                          [pl.BlockSpec((1, tm, cols), lambda i, c: (c[0], i, 0)),
                           pl.BlockSpec((N_CHIPS - 1, tm, cols), lambda i, c: (0, i, 0))],
                          pl.BlockSpec((tm, cols), lambda i, c: (i, 0)), _sds((rows, cols)))(parts, recv)


def _adam_math(w, gv, m, v):
    mn = ADAM_B1 * m + (1.0 - ADAM_B1) * gv
    vn = ADAM_B2 * v + (1.0 - ADAM_B2) * (gv * gv)
    m_hat = mn / (1.0 - ADAM_B1 ** ADAM_STEP)
    v_hat = vn / (1.0 - ADAM_B2 ** ADAM_STEP)
    return -ADAM_LR * (m_hat / (jnp.sqrt(v_hat) + ADAM_EPS) + ADAM_WD * w), mn, vn


def _adamw(w, g, m, v, name):
    rows, cols = w.shape
    tm = _row_tile(rows)

    def body(w_ref, g_ref, m_ref, v_ref, d_ref, mo_ref, vo_ref):
        d_ref[...], mo_ref[...], vo_ref[...] = _adam_math(w_ref[...], g_ref[...], m_ref[...], v_ref[...])

    spec = _rows(tm, cols)
    return _pcall(body, name, (rows // tm,), [spec] * 4, [spec] * 3, [_sds((rows, cols))] * 3)(w, g, m, v)


def _adamw_halves(w, mine, theirs, m, v, core, name):
    rows, cols = w.shape
    tm = _row_tile(rows // 2)
    nh = rows // 2 // tm

    def body(c_ref, w_ref, a_ref, b_ref, m_ref, v_ref, g_ref, d_ref, mo_ref, vo_ref):
        gv = jnp.where(pl.program_id(0) // nh == c_ref[0], a_ref[...], b_ref[...])
        g_ref[...] = gv
        d_ref[...], mo_ref[...], vo_ref[...] = _adam_math(w_ref[...], gv, m_ref[...], v_ref[...])

    full = pl.BlockSpec((tm, cols), lambda i, c: (i, 0))
    half = pl.BlockSpec((tm, cols), lambda i, c: (i % nh, 0))
    return _pcall_indexed(body, name, core, (rows // tm,), [full, half, half, full, full], [full] * 4,
                          [_sds((rows, cols))] * 4)(w, mine, theirs, m, v)


REL_SIBLING = (0, 0, 1)
REL_CHIPS = ((1, 0, 0), (0, 1, 0), (1, 1, 0))


V7X_DMA_CHUNK_BYTES = 1 << 20


def _split_copy(src, dst, shape, itemsize):
    nbytes = math.prod(shape) * itemsize
    if nbytes <= V7X_DMA_CHUNK_BYTES or len(shape) < 2:
        return [(src, dst)]
    if len(shape) > 2:
        out = []
        for k in range(shape[0]):
            out += _split_copy(src.at[k], dst.at[k], shape[1:], itemsize)
        return out
    rows = shape[0]
    sub = 8 * (4 // itemsize)
    parts = max(1, min(-(-nbytes // V7X_DMA_CHUNK_BYTES), rows // sub))
    while rows % parts or (rows // parts) % sub:
        parts -= 1
    step = rows // parts
    return [(src.at[pl.ds(k * step, step)], dst.at[pl.ds(k * step, step)]) for k in range(parts)]


def _mesh_pos():
    return (lax.axis_index("x"), lax.axis_index("y"), lax.axis_index("c"))


def _make_copy(i, op, sems, pos, src=None, dst=None):
    rel = op[0]
    src, dst = (op[1], op[2]) if src is None else (src, dst)
    send_sems, recv_sems = sems
    if rel is None:
        return pltpu.make_async_copy(src, dst, send_sems.at[i])
    peer = tuple((p + r) % 2 for p, r in zip(pos, rel))
    return pltpu.make_async_remote_copy(src_ref=src, dst_ref=dst, send_sem=send_sems.at[i], recv_sem=recv_sems.at[i],
                                        device_id=peer, device_id_type=MESH_ID)


def _start_copies(ops, sems, pos, base=0):
    for i, op in enumerate(ops):
        for s_piece, d_piece in _split_copy(op[1], op[2], op[1].shape, jnp.dtype(op[1].dtype).itemsize):
            _make_copy(base + i, op, sems, pos, s_piece, d_piece).start()


def _wait_copies(ops, sems, pos, base=0):
    for i, op in enumerate(ops):
        _make_copy(base + i, op, sems, pos).wait()


def _comm(name, ins, out_shapes, n_ops, ops_fn):
    n_in, n_out = len(ins), len(out_shapes)

    def body(*refs):
        in_refs, out_refs = refs[:n_in], refs[n_in:n_in + n_out]
        sems = refs[n_in + n_out:]
        pos = _mesh_pos()
        ops = ops_fn(in_refs, out_refs, pos)
        assert len(ops) == n_ops
        _start_copies(ops, sems, pos)
        _wait_copies(ops, sems, pos)

    hbm = pl.BlockSpec(memory_space=pl.ANY)
    return pl.pallas_call(
        body, name=name, in_specs=[hbm] * n_in, out_specs=[hbm] * n_out, out_shape=list(out_shapes),
        scratch_shapes=[pltpu.SemaphoreType.DMA((n_ops,)), pltpu.SemaphoreType.DMA((n_ops,))],
    )(*ins)


def _chip_of(pos, rel=(0, 0, 0)):
    return 2 * ((pos[0] + rel[0]) % 2) + (pos[1] + rel[1]) % 2


def _gather_chips(shards, chip, name):
    def ops_fn(in_refs, out_refs, pos):
        me = _chip_of(pos)
        return [(rel, src, dst.at[me]) for src, dst in zip(in_refs, out_refs) for rel in REL_CHIPS]

    outs = _comm(name, shards, [_sds((N_CHIPS,) + s.shape, s.dtype) for s in shards], 3 * len(shards), ops_fn)
    return [lax.dynamic_update_index_in_dim(o, s, chip, 0) for o, s in zip(outs, shards)]


def _halved_gather_ops(pos, srcs, dsts, whole):
    me, c = _chip_of(pos), pos[2]
    ici, d2d = [], []
    for a, (src, dst) in enumerate(zip(srcs, dsts)):
        for rel in REL_CHIPS:
            if a in whole:
                ici.append((rel, src, dst.at[me]))
            else:
                ici.append((rel, src.at[c], dst.at[me, c]))
                arrived = dst.at[_chip_of(pos, rel), c]
                d2d.append((REL_SIBLING, arrived, arrived))
    return ici, d2d


def _gather_halved(shards, whole, chip, name):
    srcs = [s if a in whole else s.reshape(2, s.shape[0] // 2, s.shape[1]) for a, s in enumerate(shards)]
    n_sh = len(shards)
    n_ici, n_d2d = 3 * n_sh, 3 * (n_sh - len(whole))

    def body(*refs):
        in_refs, out_refs, sems = refs[:n_sh], refs[n_sh:2 * n_sh], refs[2 * n_sh:]
        pos = _mesh_pos()
        ici, d2d = _halved_gather_ops(pos, in_refs, out_refs, whole)
        _start_copies(ici, sems, pos)
        _wait_copies(ici, sems, pos)
        _start_copies(d2d, sems, pos, base=n_ici)
        _wait_copies(d2d, sems, pos, base=n_ici)

    hbm = pl.BlockSpec(memory_space=pl.ANY)
    outs = pl.pallas_call(
        body, name=name, in_specs=[hbm] * n_sh, out_specs=[hbm] * n_sh,
        out_shape=[_sds((N_CHIPS,) + s.shape, s.dtype) for s in srcs],
        scratch_shapes=[pltpu.SemaphoreType.DMA((n_ici + n_d2d,)), pltpu.SemaphoreType.DMA((n_ici + n_d2d,))],
    )(*srcs)
    return [lax.dynamic_update_index_in_dim(o, s, chip, 0).reshape((N_CHIPS,) + sh.shape)
            for o, s, sh in zip(outs, srcs, shards)]


def _sibling_split(gs):
    def ops_fn(in_refs, out_refs, pos):
        return [(REL_SIBLING, src.at[1 - pos[2]], dst) for src, dst in zip(in_refs, out_refs)]

    return _comm("grad_sibling_split", gs, [_sds(g.shape[1:], g.dtype) for g in gs], len(gs), ops_fn)


def _chip_exchange(ps):
    def ops_fn(in_refs, out_refs, pos):
        return [(rel, src.at[_chip_of(pos, rel)], dst.at[j])
                for src, dst in zip(in_refs, out_refs) for j, rel in enumerate(REL_CHIPS)]

    return _comm("grad_chip_exchange", ps, [_sds((N_CHIPS - 1,) + p.shape[1:], p.dtype) for p in ps], 3 * len(ps), ops_fn)


def _sibling_join(hs):
    def ops_fn(in_refs, out_refs, pos):
        return [(REL_SIBLING, src, dst) for src, dst in zip(in_refs, out_refs)]

    return _comm("grad_sibling_join", hs, [_sds(h.shape, h.dtype) for h in hs], len(hs), ops_fn)


def _rot_cols(w):
    return jnp.concatenate([-w[..., 32:], w[..., :32]], axis=-1)


def _unrot_cols(dw):
    return jnp.concatenate([dw[..., 32:], -dw[..., :32]], axis=-1)


IN_OFFS = (0, 1024, 2048, 2304, 2560, 2624, 3648, 4672)


def _w1_from_w_in(w):
    seg = [w[:, IN_OFFS[i]:IN_OFFS[i + 1]] for i in range(7)]
    rnn_x, rnn_gate, cq, ckv, kr, ga, gb = seg
    return jnp.concatenate([rnn_x, rnn_gate, ga, gb, cq, ckv, kr, _rot_cols(kr)], axis=1)


def _w_in_grad_from_parts(d_rx, d_g3, d_mla):
    kr = d_mla[:, 512:576] + _unrot_cols(d_mla[:, 576:640])
    return jnp.concatenate([d_rx, d_g3[:, 0:D], d_mla[:, 0:512], kr, d_g3[:, D:3 * D]], axis=1)


def _wq_from_w_uq(w):
    w3 = w.reshape(Q_LORA, N_HEADS, QK_NOPE + QK_ROPE)
    rope = w3[..., QK_NOPE:]
    return jnp.concatenate([w3[..., :QK_NOPE], rope, _rot_cols(rope)], axis=-1).reshape(Q_LORA, N_HEADS * HEAD_W)


def _w_uq_grad_from_wq(dw):
    d3 = dw.reshape(Q_LORA, N_HEADS, HEAD_W)
    rope = d3[..., 128:192] + _unrot_cols(d3[..., 192:256])
    return jnp.concatenate([d3[..., :128], rope], axis=-1).reshape(Q_LORA, N_HEADS * (QK_NOPE + QK_ROPE))


def _cols_from_chunks(g):
    return g.transpose(1, 0, 2).reshape(g.shape[1], N_CHIPS * g.shape[2])


def _halves_of_col_chunks(dw):
    r, c4 = dw.shape
    return dw.reshape(2, r // 2, N_CHIPS, c4 // N_CHIPS).transpose(0, 2, 1, 3)


def _halves_of_row_chunks(dw):
    r4, c = dw.shape
    return dw.reshape(N_CHIPS, 2, r4 // (2 * N_CHIPS), c).transpose(1, 0, 2, 3)


def kernel(x, norm_mix, w_in, conv_w, conv_b, lru_wa, lru_ba, lru_wx, lru_bx, lru_lambda, q_norm, w_uq, kv_norm, w_ukv, w_out, norm_mlp, w_up, w_down, norm_final, loss_target, m_norm_mix, m_w_in, m_conv_w, m_conv_b, m_lru_wa, m_lru_ba, m_lru_wx, m_lru_bx, m_lru_lambda, m_q_norm, m_w_uq, m_kv_norm, m_w_ukv, m_w_out, m_norm_mlp, m_w_up, m_w_down, m_norm_final, v_norm_mix, v_w_in, v_conv_w, v_conv_b, v_lru_wa, v_lru_ba, v_lru_wx, v_lru_bx, v_lru_lambda, v_q_norm, v_w_uq, v_kv_norm, v_w_ukv, v_w_out, v_norm_mlp, v_w_up, v_w_down, v_norm_final):
    t = x.shape[1]
    tm = min(256, t)
    tq = min(512, max(tm, t // 4))
    x2 = x[0]
    target = loss_target[0]
    chip = 2 * lax.axis_index("x") + lax.axis_index("y")
    core = lax.axis_index("c")
    chip_ix, core_ix = chip.reshape(1).astype(jnp.int32), core.reshape(1).astype(jnp.int32)
    row = lambda p: p.reshape(1, -1)

    big_shards = (w_in, w_uq, w_ukv, w_out, w_up, w_down)
    w_in_g, conv_w_g = _gather_halved([w_in.astype(BF16), conv_w], (1,), chip, "weight_gather_first")
    w1 = _w1_from_w_in(_cols_from_chunks(w_in_g))
    conv_w_f = _cols_from_chunks(conv_w_g)
    wa_b, wx_b = lru_wa.astype(BF16), lru_wx.astype(BF16)

    pos = jnp.arange(t, dtype=F32)
    inv_freq = 1.0 / (ROPE_THETA ** (jnp.arange(0, QK_ROPE, 2, dtype=F32) / QK_ROPE))
    ang = pos[:, None] * inv_freq[None, :]
    rope_c = jnp.concatenate([jnp.cos(ang), jnp.cos(ang), jnp.sin(ang), jnp.sin(ang)], axis=-1)

    (xn, rx, g3, cq, ckv, kr), gathered = _inproj(x2, row(norm_mix), w1, tm, [w.astype(BF16) for w in big_shards[1:]], chip)
    wq = _wq_from_w_uq(_cols_from_chunks(gathered[0]))
    wkv = _cols_from_chunks(gathered[1])
    w_out_f = gathered[2].reshape(D, D)
    w_up_f = _cols_from_chunks(gathered[3])
    w_down_f = gathered[4].reshape(D_FF, D)
    h, xa = _lru_fwd(rx, conv_w_f, row(conv_b), wa_b, row(lru_ba), wx_b, row(lru_bx), row(lru_lambda), tm)
    q, k, v, cqn, ckvn = _mla_proj(cq, ckv, kr, row(q_norm), row(kv_norm), wq, wkv, rope_c, tm)
    nq = t // tq
    yb, lse = _flash_fwd(q, k, v, tq)
    h1, merged = _merge_out(x2, h, g3, yb, w_out_f, tm)
    u, n2 = _mlp_up(h1, row(norm_mlp), w_up_f, tm)
    act, dh2, loss_blk, g_norm_final = _mlp_down_loss(u, h1, target, w_down_f, row(norm_final), tm)

    g_w_down = _matmul_tn(act, dh2, "grad_w_down", "rows")
    du = _mlp_bwd_act(dh2, u, w_down_f, tm)
    dh1, g_norm_mlp = _mlp_bwd_in(du, dh2, h1, w_up_f, row(norm_mlp), tm)
    g_w_up = _matmul_tn(n2, du, "grad_w_up", "cols")
    dg3, dyb, delta, dh, g_w_out = _merge_bwd(dh1, w_out_f, g3, h, yb, merged, tm)
    delta = delta.reshape(N_HEADS, 8, nq, tq).swapaxes(1, 2)
    dq, dk, dv = _flash_bwd(q, k, v, dyb, lse, delta, tq)
    dmla, g_wq, g_wkv, g_q_norm, g_kv_norm = _mla_bwd(dq, dk, dv, cqn, ckvn, cq, ckv, rope_c, wq, wkv, row(q_norm), row(kv_norm), tm)
    drx, g_conv_w, g_conv_b, g_wa, g_ba, g_wx, g_bx, g_lam = _lru_bwd(
        dh, xa, h, rx, conv_w_f, wa_b, row(lru_ba), wx_b, row(lru_bx), row(lru_lambda), tm)
    grad_x, g_norm_mix = _inproj_bwd(x2, dh1, drx, dg3, dmla, w1, row(norm_mix), tm)
    g_w_in = _w_in_grad_from_parts(_matmul_tn(xn, drx, "grad_w_in_rx"), _matmul_tn(xn, dg3, "grad_w_in_gates"),
                                   _matmul_tn(xn, dmla, "grad_w_in_mla"))
    g_w_uq = _w_uq_grad_from_wq(g_wq)

    smalls = (g_norm_mix, g_conv_b, g_wa, g_ba, g_wx, g_bx, g_lam, g_q_norm, g_kv_norm, g_norm_mlp, g_norm_final, g_conv_w)
    s_flat = jnp.concatenate([s.reshape(-1) for s in smalls] + [loss_blk[0, 0:1], jnp.zeros((S_LEN - N_SMALL - CONVW_SIZE - 1,), F32)])
    halves = [_halves_of_col_chunks(g_w_in), _halves_of_col_chunks(g_w_uq), _halves_of_col_chunks(g_wkv),
              _halves_of_row_chunks(g_w_out), g_w_up, g_w_down,
              s_flat.reshape(N_CHIPS, 2, S_ROWS_HALF, 128).transpose(1, 0, 2, 3)]
    theirs = _sibling_split(halves)
    parts = []
    for a, (hv, r) in enumerate(zip(halves, theirs)):
        dt = F32 if a == len(halves) - 1 else BF16
        pair = _pair_sum(hv.reshape(2, -1, hv.shape[-1]), r.reshape(-1, r.shape[-1]), core_ix, dt, f"grad_pair_sum_{a}")
        parts.append(pair.reshape(r.shape))
    received = _chip_exchange(parts)
    reduced = [_chip_sum(p, r, chip_ix, f"grad_chip_sum_{a}") for a, (p, r) in enumerate(zip(parts, received))]
    reduced_sibling = _sibling_join(reduced)
    s_mine, s_theirs = reduced[-1], reduced_sibling[-1]
    s_chunk = jnp.where(core == 0, jnp.concatenate([s_mine, s_theirs]), jnp.concatenate([s_theirs, s_mine]))
    s_all = _gather_chips([s_chunk], chip, "small_grad_gather")[0].reshape(-1)

    small_grads = []
    off = 0
    for shp, n in zip(SMALL_SHAPES, SMALL_SIZES):
        small_grads.append(s_all[off:off + n].reshape(shp))
        off += n
    g_conv_w_mine = lax.dynamic_slice_in_dim(s_all[off:off + CONVW_SIZE].reshape(4, D), chip * (D // N_CHIPS), D // N_CHIPS, axis=1)
    loss = s_all[off + CONVW_SIZE]

    big_m = (m_w_in, m_w_uq, m_w_ukv, m_w_out, m_w_up, m_w_down)
    big_v = (v_w_in, v_w_uq, v_w_ukv, v_w_out, v_w_up, v_w_down)
    big_names = ("w_in", "w_uq", "w_ukv", "w_out", "w_up", "w_down")
    big_upd = [_adamw_halves(w, gm, gt, m, v, core_ix, "adamw_" + n)
               for w, gm, gt, m, v, n in zip(big_shards, reduced, reduced_sibling, big_m, big_v, big_names)]

    small_w = (norm_mix, conv_b, lru_wa, lru_ba, lru_wx, lru_bx, lru_lambda, q_norm, kv_norm, norm_mlp, norm_final)
    small_m = (m_norm_mix, m_conv_b, m_lru_wa, m_lru_ba, m_lru_wx, m_lru_bx, m_lru_lambda, m_q_norm, m_kv_norm, m_norm_mlp, m_norm_final)
    small_v = (v_norm_mix, v_conv_b, v_lru_wa, v_lru_ba, v_lru_wx, v_lru_bx, v_lru_lambda, v_q_norm, v_kv_norm, v_norm_mlp, v_norm_final)

    def pack(items, last, fill):
        flat = jnp.concatenate([i.reshape(-1) for i in items] + [last.reshape(-1)])
        return jnp.concatenate([flat, jnp.full((PACK_ROWS * 128 - flat.shape[0],), fill, F32)]).reshape(PACK_ROWS, 128)

    packed = _adamw(pack(small_w, conv_w, 0.0), pack(small_grads, g_conv_w_mine, 0.0), pack(small_m, m_conv_w, 0.0),
                    pack(small_v, v_conv_w, 1.0), "adamw_small")

    def unpack(p):
        flat = p.reshape(-1)
        outs, o = [], 0
        for shp, n in zip(SMALL_SHAPES, SMALL_SIZES):
            outs.append(flat[o:o + n].reshape(shp))
            o += n
        return outs, flat[o:o + CONVW_SIZE // N_CHIPS].reshape(4, D // N_CHIPS)

    order = ("norm_mix", "w_in", "conv_w", "conv_b", "lru_wa", "lru_ba", "lru_wx", "lru_bx", "lru_lambda", "q_norm", "w_uq", "kv_norm",
             "w_ukv", "w_out", "norm_mlp", "w_up", "w_down", "norm_final")

    def assemble(small_list, conv_w_item, big_list):
        table = dict(zip(SMALL_NAMES, small_list))
        table["conv_w"] = conv_w_item
        table.update(zip(big_names, big_list))
        return [table[n] for n in order]

    outs = [loss, grad_x.reshape(1, t, D)]
    outs += assemble(small_grads, g_conv_w_mine, [b[0] for b in big_upd])
    for j in range(3):
        sm, cw = unpack(packed[j])
        outs += assemble(sm, cw, [b[j + 1] for b in big_upd])
    return tuple(outs)
```

```python
import functools
import math

import jax
import jax.numpy as jnp
from jax import lax
from jax.experimental import pallas as pl
from jax.experimental.pallas import tpu as pltpu

F32 = jnp.float32
BF16 = jnp.bfloat16

D = 1024
N_HEADS = 8
QK_NOPE = 128
QK_ROPE = 64
V_HEAD = 128
Q_LORA = 256
KV_LORA = 256
D_FF = 4096
RNN_BLOCKS = 8
RNN_BW = 128
LRU_C = 8.0
EPS = 1e-6
ROPE_THETA = 10000.0
HEAD_W = 256
KR_W = 128
W1_COLS = 4 * D + Q_LORA + KV_LORA + KR_W
SM_SCALE = (QK_NOPE + QK_ROPE) ** -0.5
NEG = float(jnp.finfo(jnp.float32).min)

ADAM_LR = 0.001
ADAM_B1 = 0.9
ADAM_B2 = 0.999
ADAM_EPS = 1e-08
ADAM_WD = 0.01
ADAM_STEP = 10

N_CHIPS = 4
V7X_VMEM_LIMIT = 56 * 1024 * 1024
MESH_ID = pl.DeviceIdType.MESH

SMALL_NAMES = ("norm_mix", "conv_b", "lru_wa", "lru_ba", "lru_wx", "lru_bx", "lru_lambda", "q_norm", "kv_norm", "norm_mlp", "norm_final")
SMALL_SHAPES = ((D,), (D,), (RNN_BLOCKS, RNN_BW, RNN_BW), (RNN_BLOCKS, RNN_BW), (RNN_BLOCKS, RNN_BW, RNN_BW), (RNN_BLOCKS, RNN_BW), (D,),
                (Q_LORA,), (KV_LORA,), (D,), (D,))
SMALL_SIZES = tuple(math.prod(s) for s in SMALL_SHAPES)
N_SMALL = sum(SMALL_SIZES)
CONVW_SIZE = 4 * D
S_LEN = -(-(N_SMALL + CONVW_SIZE) // 8192) * 8192
S_ROWS_HALF = S_LEN // (N_CHIPS * 2 * 128)
PACK_ROWS = -(-(N_SMALL + CONVW_SIZE // N_CHIPS) // (256 * 128)) * 256


def _pcall(body, name, grid, in_specs, out_specs, out_shape, scratch=(), comm=None):
    params = pltpu.CompilerParams(dimension_semantics=("arbitrary",) * len(grid), vmem_limit_bytes=V7X_VMEM_LIMIT)
    if comm is None:
        return pl.pallas_call(body, name=name, grid=grid, in_specs=in_specs, out_specs=out_specs, out_shape=out_shape,
                              scratch_shapes=list(scratch), compiler_params=params)
    c_ins, c_shapes, n_ops, ops_fn = comm
    single = not isinstance(out_specs, (list, tuple))
    out_specs, out_shape = ([out_specs], [out_shape]) if single else (list(out_specs), list(out_shape))
    n_in, n_out, n_sc, n_ci, n_co = len(in_specs), len(out_specs), len(scratch), len(c_ins), len(c_shapes)

    def wrapped(*refs):
        ins, refs = refs[:n_in], refs[n_in:]
        c_in_refs, refs = refs[:n_ci], refs[n_ci:]
        outs, refs = refs[:n_out], refs[n_out:]
        c_out_refs, refs = refs[:n_co], refs[n_co:]
        own_scratch, sems = refs[:n_sc], refs[n_sc:]
        pos = _mesh_pos()
        ops = ops_fn(c_in_refs, c_out_refs, pos)
        assert len(ops) == n_ops
        first, last = True, True
        for d, n in enumerate(grid):
            first = first & (pl.program_id(d) == 0)
            last = last & (pl.program_id(d) == n - 1)

        @pl.when(first)
        def _():
            _start_copies(ops, sems, pos)

        body(*ins, *outs, *own_scratch)

        @pl.when(last)
        def _():
            _wait_copies(ops, sems, pos)

    hbm = pl.BlockSpec(memory_space=pl.ANY)
    call = pl.pallas_call(
        wrapped, name=name, grid=grid, in_specs=list(in_specs) + [hbm] * n_ci, out_specs=out_specs + [hbm] * n_co,
        out_shape=out_shape + list(c_shapes),
        scratch_shapes=list(scratch) + [pltpu.SemaphoreType.DMA((n_ops,)), pltpu.SemaphoreType.DMA((n_ops,))],
        compiler_params=params)

    def run(*operands):
        res = call(*operands, *c_ins)
        own = res[0] if single else res[:n_out]
        return own, res[n_out:]

    return run


def _rows(tm, w):
    return pl.BlockSpec((tm, w), lambda i: (i, 0))


def _full(*shape):
    return pl.BlockSpec(shape, lambda *_: (0,) * len(shape))


def _sds(shape, dtype=F32):
    return jax.ShapeDtypeStruct(shape, dtype)


def _row_tile(rows, cap=256):
    t = min(rows, cap)
    while rows % t or t % 8:
        t -= 1
    return t


def _dot(a, b):
    return jnp.dot(a, b, preferred_element_type=F32)


def _dot_nt(a, b):
    return lax.dot_general(a, b, (((1,), (1,)), ((), ())), preferred_element_type=F32)


def _dot_tn(a, b):
    return lax.dot_general(a, b, (((0,), (0,)), ((), ())), preferred_element_type=F32)


def _sigmoid(x):
    return 1.0 / (1.0 + jnp.exp(-x))


_GELU_C = math.sqrt(2.0 / math.pi)


def _gelu(x):
    return x * (0.5 * (1.0 + jnp.tanh(_GELU_C * (x + 0.044715 * (x * x * x)))))


def _gelu_grad(x):
    t = jnp.tanh(_GELU_C * (x + 0.044715 * (x * x * x)))
    cdf = 0.5 * (1.0 + t)
    return cdf + x * (0.5 * (1.0 - t * t) * _GELU_C * (1.0 + 3.0 * 0.044715 * (x * x)))


def _rms_scale(x):
    return lax.rsqrt(jnp.mean(x * x, axis=-1, keepdims=True) + EPS)


def _rms_bwd(x, rs, g, dy):
    gdy = dy * g
    dx = rs * gdy - x * ((rs * rs * rs) * jnp.mean(gdy * x, axis=-1, keepdims=True))
    return dx, dy * (x * rs)


def _log1p(e):
    u = 1.0 + e
    d = u - 1.0
    return jnp.where(d == 0.0, e, jnp.log(u) * (e / jnp.where(d == 0.0, 1.0, d)))


def _softplus(y):
    return jnp.maximum(y, 0.0) + _log1p(jnp.exp(-jnp.abs(y)))


def _expm1(x):
    u = jnp.exp(x)
    lu = jnp.log(u)
    safe = jnp.where((u == 1.0) | (u == 0.0), 1.0, lu)
    return jnp.where(u == 1.0, x, jnp.where(u == 0.0, -1.0, (u - 1.0) * (x / safe)))


def _row_iota(shape):
    return lax.broadcasted_iota(jnp.int32, shape, 0)


def _lane_iota(shape):
    return lax.broadcasted_iota(jnp.int32, shape, 1)


def _scan_groups_fwd(a, b):
    sub = lax.broadcasted_iota(jnp.int32, a.shape, 1)
    for sh in (1, 2, 4):
        m = sub >= sh
        b = jnp.where(m, a * pltpu.roll(b, sh, 1) + b, b)
        a = jnp.where(m, a * pltpu.roll(a, sh, 1), a)
    return a, b


def _scan_groups_bwd(c, b):
    sub = lax.broadcasted_iota(jnp.int32, c.shape, 1)
    for sh in (1, 2, 4):
        m = sub < 8 - sh
        b = jnp.where(m, b + c * pltpu.roll(b, 8 - sh, 1), b)
        c = jnp.where(m, c * pltpu.roll(c, 8 - sh, 1), c)
    return c, b


def _rope_pair(gc):
    return gc + pltpu.roll(gc, 64, 1)


def _inproj(x, g, w1, tm, shards, chip):
    t = x.shape[0]
    nsteps = t // tm
    widths = (D, 3 * D, Q_LORA, KV_LORA, KR_W)
    srcs = [s.reshape(2, s.shape[0] // 2, s.shape[1]) for s in shards]
    n_sh = len(shards)

    def body(x_ref, g_ref, w_ref, *refs):
        sh_refs = refs[:n_sh]
        xn_ref, rx_ref, g3_ref, cq_ref, ckv_ref, kr_ref = refs[n_sh:n_sh + 6]
        gat_refs = refs[n_sh + 6:2 * n_sh + 6]
        sems = refs[2 * n_sh + 6:]
        pos = _mesh_pos()
        ici, d2d = _halved_gather_ops(pos, sh_refs, gat_refs, ())

        @pl.when(pl.program_id(0) == 0)
        def _():
            _start_copies(ici, sems, pos)

        xv = x_ref[...]
        xn = (xv * _rms_scale(xv) * g_ref[...]).astype(BF16)
        xn_ref[...] = xn
        col = 0
        for ref, w in zip((rx_ref, g3_ref, cq_ref, ckv_ref, kr_ref), widths):
            for c0 in range(0, w, 512):
                cw = min(512, w - c0)
                ref[:, c0:c0 + cw] = _dot(xn, w_ref[:, col + c0:col + c0 + cw])
            col += w

        @pl.when(pl.program_id(0) == nsteps - 1)
        def _():
            _wait_copies(ici, sems, pos)
            _start_copies(d2d, sems, pos, base=len(ici))
            _wait_copies(d2d, sems, pos, base=len(ici))

    hbm = pl.BlockSpec(memory_space=pl.ANY)
    outs = pl.pallas_call(
        body, name="inproj", grid=(nsteps,),
        in_specs=[_rows(tm, D), _full(1, D), _full(D, W1_COLS)] + [hbm] * n_sh,
        out_specs=[_rows(tm, D)] + [_rows(tm, w) for w in widths] + [hbm] * n_sh,
        out_shape=[_sds((t, D), BF16)] + [_sds((t, w)) for w in widths] + [_sds((N_CHIPS,) + s.shape, s.dtype) for s in srcs],
        scratch_shapes=[pltpu.SemaphoreType.DMA((6 * n_sh,)), pltpu.SemaphoreType.DMA((6 * n_sh,))],
        compiler_params=pltpu.CompilerParams(dimension_semantics=("arbitrary",), vmem_limit_bytes=V7X_VMEM_LIMIT),
    )(x, g, w1, *srcs)
    gathered = [lax.dynamic_update_index_in_dim(o, s, chip, 0).reshape((N_CHIPS,) + sh.shape)
                for o, s, sh in zip(outs[6:], srcs, shards)]
    return outs[:6], gathered


def _lru_gates(xa, wa_ref, ba, wx_ref, bx, pre_r, pre_i):
    xb = xa.astype(BF16)
    for n in range(RNN_BLOCKS):
        sl = slice(n * RNN_BW, (n + 1) * RNN_BW)
        pre_r[:, sl] = _dot(xb[:, sl], wa_ref[n])
        pre_i[:, sl] = _dot(xb[:, sl], wx_ref[n])
    r = _sigmoid(pre_r[...] + ba)
    i = _sigmoid(pre_i[...] + bx)
    return r, i


def _lru_fwd(rx, conv_w, conv_b, wa, ba, wx, bx, lam, tb):
    t = rx.shape[0]
    nb = t // tb

    def body(x_ref, xp_ref, cw_ref, cb_ref, wa_ref, ba_ref, wx_ref, bx_ref, lam_ref, h_ref, xa_ref, hc, tmp, pre_r, pre_i):
        i_blk = pl.program_id(0)

        @pl.when(i_blk == 0)
        def _():
            hc[...] = jnp.zeros_like(hc)

        xv = x_ref[...]
        xp = jnp.where(i_blk > 0, xp_ref[...], 0.0)
        row8 = _row_iota((8, D))
        xa = cb_ref[...] + cw_ref[3:4, :] * xv
        for s in (1, 2, 3):
            xr = pltpu.roll(xv, s, 0)
            tmp[...] = xr
            tmp[0:8, :] = jnp.where(row8 < s, pltpu.roll(xp, s, 0), xr[0:8, :])
            xa = xa + cw_ref[3 - s:4 - s, :] * tmp[...]
        xa_ref[...] = xa
        r, gi = _lru_gates(xa, wa_ref, ba_ref[...], wx_ref, bx_ref[...], pre_r, pre_i)
        la = (-LRU_C * _softplus(-lam_ref[...])) * r
        a = jnp.exp(la)
        b = jnp.sqrt(-_expm1(2.0 * la)) * (gi * xa)
        a3, b3 = _scan_groups_fwd(a.reshape(tb // 8, 8, D), b.reshape(tb // 8, 8, D))
        carry = hc[...]
        for grp in range(tb // 8):
            hg = a3[grp] * carry + b3[grp]
            h_ref[8 * grp:8 * grp + 8, :] = hg
            carry = hg[7:8, :]
        hc[...] = carry

    prev8 = pl.BlockSpec((8, D), lambda i: (jnp.maximum(i * (tb // 8) - 1, 0), 0))
    return _pcall(
        body, "lru_fwd", (nb,),
        [_rows(tb, D), prev8, _full(4, D), _full(1, D), _full(RNN_BLOCKS, RNN_BW, RNN_BW), _full(1, D),
         _full(RNN_BLOCKS, RNN_BW, RNN_BW), _full(1, D), _full(1, D)],
        [_rows(tb, D), _rows(tb, D)],
        [_sds((t, D)), _sds((t, D))],
        scratch=[pltpu.VMEM((1, D), F32), pltpu.VMEM((tb, D), F32), pltpu.VMEM((tb, D), F32), pltpu.VMEM((tb, D), F32)],
    )(rx, rx, conv_w, conv_b, wa, ba, wx, bx, lam)


def _mla_proj(cq, ckv, kr, qn, kvn, wq, wkv, rope_c, tm):
    t = cq.shape[0]

    def body(cq_ref, ckv_ref, kr_ref, qn_ref, kvn_ref, wq_ref, wkv_ref, c_ref, q_ref, k_ref, v_ref, cqn_ref, ckvn_ref):
        cqv = cq_ref[...]
        cqn = (cqv * _rms_scale(cqv) * qn_ref[...]).astype(BF16)
        ckvv = ckv_ref[...]
        ckvn = (ckvv * _rms_scale(ckvv) * kvn_ref[...]).astype(BF16)
        cqn_ref[...] = cqn
        ckvn_ref[...] = ckvn
        c = c_ref[...]
        lane = _lane_iota((tm, KR_W))
        kro = jnp.where(lane < 64, _rope_pair(kr_ref[...] * c), 0.0).astype(BF16)
        for h in range(N_HEADS):
            sl = slice(h * HEAD_W, (h + 1) * HEAD_W)
            qh = _dot(cqn, wq_ref[:, sl])
            q_ref[h, :, 0:128] = qh[:, 0:128].astype(BF16)
            q_ref[h, :, 128:256] = _rope_pair(qh[:, 128:256] * c).astype(BF16)
            kvh = _dot(ckvn, wkv_ref[:, sl])
            k_ref[h, :, 0:128] = kvh[:, 0:128].astype(BF16)
            k_ref[h, :, 128:256] = kro
            v_ref[h, :, 0:V_HEAD] = kvh[:, 128:256].astype(BF16)
            v_ref[h, :, V_HEAD:2 * V_HEAD] = jnp.ones((tm, V_HEAD), BF16)

    hb = lambda w: pl.BlockSpec((N_HEADS, tm, w), lambda i: (0, i, 0))
    return _pcall(
        body, "mla_proj", (t // tm,),
        [_rows(tm, Q_LORA), _rows(tm, KV_LORA), _rows(tm, KR_W), _full(1, Q_LORA), _full(1, KV_LORA),
         _full(Q_LORA, N_HEADS * HEAD_W), _full(KV_LORA, N_HEADS * HEAD_W), _rows(tm, KR_W)],
        [hb(HEAD_W), hb(HEAD_W), hb(2 * V_HEAD), _rows(tm, Q_LORA), _rows(tm, KV_LORA)],
        [_sds((N_HEADS, t, HEAD_W), BF16), _sds((N_HEADS, t, HEAD_W), BF16), _sds((N_HEADS, t, 2 * V_HEAD), BF16),
         _sds((t, Q_LORA), BF16), _sds((t, KV_LORA), BF16)],
    )(cq, ckv, kr, qn, kvn, wq, wkv, rope_c)


EXP2_SCALE = SM_SCALE * math.log2(math.e)


def _flash_fwd(q, k, v, tq):
    t = q.shape[1]
    nq = t // tq

    def body(q_ref, k_ref, v_ref, o_ref, lse_ref, s_even, s_odd):
        qi = pl.program_id(1)
        qv = q_ref[0]

        def scores(ki, buf):
            buf[...] = _dot_nt(qv, k_ref[0, pl.ds(pl.multiple_of(ki * tq, tq), tq), :])

        def softmax_pv(ki, buf, carry, diagonal):
            m, acc = carry
            s = buf[...]
            if diagonal:
                s = jnp.where(_row_iota((tq, tq)) >= _lane_iota((tq, tq)), s, NEG)
            m_new = jnp.maximum(m, jnp.max(s, axis=1, keepdims=True))
            p = jnp.exp2((s - m_new) * EXP2_SCALE)
            alpha = jnp.exp2((m - m_new) * EXP2_SCALE)
            acc = alpha * acc + _dot(p.astype(BF16), v_ref[0, pl.ds(pl.multiple_of(ki * tq, tq), tq), :])
            return m_new, acc

        def finish(carry):
            m, acc = carry
            l = acc[:, V_HEAD:2 * V_HEAD]
            o_ref[...] = acc[:, 0:V_HEAD] / l
            lse = m * EXP2_SCALE + jnp.log(l) * math.log2(math.e)
            lse_ref[0, 0] = jnp.transpose(lse)[0:8, :]

        def two(i, carry):
            scores(2 * i + 1, s_odd)
            carry = softmax_pv(2 * i, s_even, carry, False)
            scores(2 * i + 2, s_even)
            return softmax_pv(2 * i + 1, s_odd, carry, False)

        init = (jnp.full((tq, 1), -jnp.inf, F32), jnp.zeros((tq, 2 * V_HEAD), F32))
        scores(0, s_even)
        carry = lax.fori_loop(0, qi // 2, two, init)

        @pl.when(qi % 2 == 0)
        def _():
            finish(softmax_pv(qi, s_even, carry, True))

        @pl.when(qi % 2 == 1)
        def _():
            scores(qi, s_odd)
            finish(softmax_pv(qi, s_odd, softmax_pv(qi - 1, s_even, carry, False), True))

    head = lambda w: pl.BlockSpec((1, t, w), lambda h, qi: (h, 0, 0))
    return _pcall(
        body, "flash_fwd", (N_HEADS, nq),
        [pl.BlockSpec((1, tq, HEAD_W), lambda h, qi: (h, qi, 0)), head(HEAD_W), head(2 * V_HEAD)],
        [pl.BlockSpec((tq, V_HEAD), lambda h, qi: (qi, h)), pl.BlockSpec((1, 1, 8, tq), lambda h, qi: (h, qi, 0, 0))],
        [_sds((t, D)), _sds((N_HEADS, nq, 8, tq))],
        scratch=[pltpu.VMEM((tq, tq), F32), pltpu.VMEM((tq, tq), F32)],
    )(q, k, v)


def _merge_out(x, h, g3, yb, w_out, tm):
    t = x.shape[0]

    def body(x_ref, h_ref, g3_ref, yb_ref, w_ref, h1_ref, mg_ref):
        ya = h_ref[...] * _gelu(g3_ref[:, 0:D])
        merged = (_sigmoid(g3_ref[:, D:2 * D]) * ya + _sigmoid(g3_ref[:, 2 * D:3 * D]) * yb_ref[...]).astype(BF16)
        mg_ref[...] = merged
        h1_ref[...] = x_ref[...] + _dot(merged, w_ref[...])

    return _pcall(
        body, "merge_out", (t // tm,),
        [_rows(tm, D), _rows(tm, D), _rows(tm, 3 * D), _rows(tm, D), _full(D, D)],
        [_rows(tm, D), _rows(tm, D)],
        [_sds((t, D)), _sds((t, D), BF16)],
    )(x, h, g3, yb, w_out)


def _mlp_up(h1, g, w_up, tm):
    t = h1.shape[0]

    def body(h_ref, g_ref, w_ref, u_ref, n2_ref):
        hv = h_ref[...]
        n2 = (hv * _rms_scale(hv) * g_ref[...]).astype(BF16)
        n2_ref[...] = n2
        for c0 in range(0, D_FF, 512):
            u_ref[:, c0:c0 + 512] = _dot(n2, w_ref[:, c0:c0 + 512])

    return _pcall(
        body, "mlp_up", (t // tm,),
        [_rows(tm, D), _full(1, D), _full(D, D_FF)],
        [_rows(tm, D_FF), _rows(tm, D)],
        [_sds((t, D_FF)), _sds((t, D), BF16)],
    )(h1, g, w_up)


def _mlp_down_loss(u, h1, target, w_down, g, tm):
    t = u.shape[0]

    def body(u_ref, h1_ref, tg_ref, w_ref, g_ref, act_ref, dh2_ref, loss_ref, gnf_ref, lacc):
        i = pl.program_id(0)

        @pl.when(i == 0)
        def _():
            lacc[...] = jnp.zeros_like(lacc)
            gnf_ref[...] = jnp.zeros_like(gnf_ref)

        ru = jnp.maximum(u_ref[...], 0.0)
        act = (ru * ru).astype(BF16)
        act_ref[...] = act
        h2 = h1_ref[...] + _dot(act, w_ref[...])
        rs = _rms_scale(h2)
        gv = g_ref[...]
        err = h2 * rs * gv - tg_ref[...]
        lacc[...] += jnp.sum(err * err, axis=0, keepdims=True)
        dx, dgr = _rms_bwd(h2, rs, gv, err * (1.0 / D))
        dh2_ref[...] = dx
        gnf_ref[...] += jnp.sum(dgr, axis=0, keepdims=True)

        @pl.when(i == pl.num_programs(0) - 1)
        def _():
            loss_ref[...] = jnp.broadcast_to(jnp.sum(lacc[...], axis=1, keepdims=True) * (0.5 / D), (8, 128))

    return _pcall(
        body, "mlp_down_loss", (t // tm,),
        [_rows(tm, D_FF), _rows(tm, D), _rows(tm, D), _full(D_FF, D), _full(1, D)],
        [_rows(tm, D_FF), _rows(tm, D), _full(8, 128), _full(1, D)],
        [_sds((t, D_FF), BF16), _sds((t, D)), _sds((8, 128)), _sds((1, D))],
        scratch=[pltpu.VMEM((1, D), F32)],
    )(u, h1, target, w_down, g)


def _matmul_tn(a, g, name, chunked=None):
    t, kdim = a.shape
    ndim = g.shape[1]
    tk = min(kdim, 1024)
    tn = ndim if ndim <= 1024 else 1024
    if chunked == "cols":
        assert tk == kdim and tn == ndim // N_CHIPS
    elif chunked == "rows":
        assert tk == kdim // N_CHIPS and tn == ndim
    tt = min(t, 512)
    nt = t // tt

    def body(a_ref, g_ref, o_ref):
        @pl.when(pl.program_id(2) == 0)
        def _():
            o_ref[...] = jnp.zeros_like(o_ref)

        o_ref[...] += _dot_tn(a_ref[...].astype(BF16), g_ref[...].astype(BF16)).reshape(o_ref.shape)

    if chunked is not None:
        out_spec = pl.BlockSpec((2, None, tk // 2, tn), lambda i, j, s: (0, i + j, 0, 0))
        out_shape = _sds((2, N_CHIPS, tk // 2, tn))
    else:
        out_spec, out_shape = pl.BlockSpec((tk, tn), lambda i, j, s: (i, j)), _sds((kdim, ndim))
    return _pcall(
        body, name, (kdim // tk, ndim // tn, nt),
        [pl.BlockSpec((tt, tk), lambda i, j, s: (s, i)), pl.BlockSpec((tt, tn), lambda i, j, s: (s, j))],
        out_spec, out_shape,
    )(a, g)


def _mlp_bwd_act(dh2, u, w_down, tm):
    t = u.shape[0]

    def body(d_ref, u_ref, w_ref, du_ref):
        db = d_ref[...].astype(BF16)
        for c0 in range(0, D_FF, 512):
            da = _dot_nt(db, w_ref[c0:c0 + 512, :])
            du_ref[:, c0:c0 + 512] = (da * (2.0 * jnp.maximum(u_ref[:, c0:c0 + 512], 0.0))).astype(BF16)

    return _pcall(
        body, "mlp_bwd_act", (t // tm,),
        [_rows(tm, D), _rows(tm, D_FF), _full(D_FF, D)],
        _rows(tm, D_FF), _sds((t, D_FF), BF16),
    )(dh2, u, w_down)


def _mlp_bwd_in(du, dh2, h1, w_up, g, tm):
    t = du.shape[0]

    def body(du_ref, d_ref, h_ref, w_ref, g_ref, dh1_ref, gacc_ref):
        @pl.when(pl.program_id(0) == 0)
        def _():
            gacc_ref[...] = jnp.zeros_like(gacc_ref)

        dn2 = _dot_nt(du_ref[...], w_ref[...])
        hv = h_ref[...]
        dx, dgr = _rms_bwd(hv, _rms_scale(hv), g_ref[...], dn2)
        dh1_ref[...] = d_ref[...] + dx
        gacc_ref[...] += jnp.sum(dgr, axis=0, keepdims=True)

    return _pcall(
        body, "mlp_bwd_in", (t // tm,),
        [_rows(tm, D_FF), _rows(tm, D), _rows(tm, D), _full(D, D_FF), _full(1, D)],
        [_rows(tm, D), _full(1, D)],
        [_sds((t, D)), _sds((1, D))],
    )(du, dh2, h1, w_up, g)


def _merge_bwd(dh1, w_out, g3, h, yb, merged, tm):
    t = dh1.shape[0]

    def body(d_ref, w_ref, g3_ref, h_ref, yb_ref, mg_ref, dg3_ref, dyb_ref, dl_ref, dh_ref, dwo_ref):
        @pl.when(pl.program_id(0) == 0)
        def _():
            dwo_ref[...] = jnp.zeros_like(dwo_ref)

        db = d_ref[...].astype(BF16)
        dwo_ref[...] += _dot_tn(mg_ref[...], db)
        dm = _dot_nt(db, w_ref[...])
        gv = g3_ref[:, 0:D]
        sa = _sigmoid(g3_ref[:, D:2 * D])
        sb = _sigmoid(g3_ref[:, 2 * D:3 * D])
        gel = _gelu(gv)
        hv = h_ref[...]
        ybv = yb_ref[...]
        dya = dm * sa
        dyb = dm * sb
        dg3_ref[:, 0:D] = (dya * hv * _gelu_grad(gv)).astype(BF16)
        dg3_ref[:, D:2 * D] = (dya * (hv * gel) * (1.0 - sa)).astype(BF16)
        dg3_ref[:, 2 * D:3 * D] = (dyb * ybv * (1.0 - sb)).astype(BF16)
        dh_ref[...] = dya * gel
        dyb_ref[...] = dyb.astype(BF16)
        prod = dyb * ybv
        ones = jnp.ones((8, V_HEAD), F32)
        for hh in range(N_HEADS):
            dl_ref[hh] = lax.dot_general(ones, prod[:, hh * V_HEAD:(hh + 1) * V_HEAD], (((1,), (1,)), ((), ())),
                                         precision=lax.Precision.HIGHEST, preferred_element_type=F32)

    return _pcall(
        body, "merge_bwd", (t // tm,),
        [_rows(tm, D), _full(D, D), _rows(tm, 3 * D), _rows(tm, D), _rows(tm, D), _rows(tm, D)],
        [_rows(tm, 3 * D), _rows(tm, D), pl.BlockSpec((N_HEADS, 8, tm), lambda i: (0, 0, i)), _rows(tm, D), _full(D, D)],
        [_sds((t, 3 * D), BF16), _sds((t, D), BF16), _sds((N_HEADS, 8, t)), _sds((t, D)), _sds((D, D))],
    )(dh1, w_out, g3, h, yb, merged)


def _flash_bwd(q, k, v, do, lse, delta, tq, comm=None):
    t = q.shape[1]
    nq = t // tq

    def body(q_ref, k_ref, v_ref, do_ref, lse_ref, dl_ref, dqt_ref, dk_ref, dv_ref):
        ki = pl.program_id(1)

        @pl.when(ki == 0)
        def _():
            dqt_ref[...] = jnp.zeros_like(dqt_ref)

        kblk, vblk = k_ref[0], v_ref[0]
        kt = jnp.transpose(kblk)
        dk_ref[...] = jnp.zeros_like(dk_ref)
        dv_ref[...] = jnp.zeros_like(dv_ref)

        def block(qi, diagonal):
            rows = pl.ds(pl.multiple_of(qi * tq, tq), tq)
            qv, dov = q_ref[0, rows, :], do_ref[rows, :]
            p = jnp.exp2(_dot_nt(kblk, qv) * EXP2_SCALE - lse_ref[0, qi, 0:1, :])
            if diagonal:
                p = jnp.where(_lane_iota((tq, tq)) >= _row_iota((tq, tq)), p, 0.0)
            dv_ref[0] += _dot(p.astype(BF16), dov)
            dp = _dot_nt(vblk, dov)
            ds = (p * (dp - dl_ref[0, qi, 0:1, :]) * SM_SCALE).astype(BF16)
            dk_ref[0] += _dot(ds, qv)
            dqt_ref[0, qi] += _dot(kt, ds)

        block(ki, True)

        def two(i, carry):
            block(ki + 1 + 2 * i, False)
            block(ki + 2 + 2 * i, False)
            return carry

        def one(qi, carry):
            block(qi, False)
            return carry

        pairs = (nq - 1 - ki) // 2
        lax.fori_loop(0, pairs, two, 0)
        lax.fori_loop(ki + 1 + 2 * pairs, nq, one, 0)

    kv_spec = lambda w: pl.BlockSpec((1, tq, w), lambda h, ki: (h, ki, 0))
    stat = pl.BlockSpec((1, nq, 8, tq), lambda h, ki: (h, 0, 0, 0))
    return _pcall(
        body, "flash_bwd", (N_HEADS, nq),
        [pl.BlockSpec((1, t, HEAD_W), lambda h, ki: (h, 0, 0)), kv_spec(HEAD_W), kv_spec(V_HEAD),
         pl.BlockSpec((t, V_HEAD), lambda h, ki: (0, h)), stat, stat],
        [pl.BlockSpec((1, nq, HEAD_W, tq), lambda h, ki: (h, 0, 0, 0)), kv_spec(HEAD_W), kv_spec(V_HEAD)],
        [_sds((N_HEADS, nq, HEAD_W, tq)), _sds((N_HEADS, t, HEAD_W)), _sds((N_HEADS, t, V_HEAD))],
        comm=comm,
    )(q, k, v, do, lse, delta)


def _mla_bwd(dqt, dk, dv, cqn, ckvn, cq, ckv, rope_c, wq, wkv, qn, kvn, tm):
    t = cq.shape[0]

    def body(dq_ref, dk_ref, dv_ref, cqn_ref, ckvn_ref, cq_ref, ckv_ref, c_ref, wq_ref, wkv_ref, qn_ref, kvn_ref,
             dmla_ref, dwq_ref, dwkv_ref, dqn_ref, dkvn_ref):
        @pl.when(pl.program_id(0) == 0)
        def _():
            dwq_ref[...] = jnp.zeros_like(dwq_ref)
            dwkv_ref[...] = jnp.zeros_like(dwkv_ref)
            dqn_ref[...] = jnp.zeros_like(dqn_ref)
            dkvn_ref[...] = jnp.zeros_like(dkvn_ref)

        c = c_ref[...]
        lane = _lane_iota((tm, KR_W))
        cqn, ckvn = cqn_ref[...], ckvn_ref[...]
        dcqn = jnp.zeros((tm, Q_LORA), F32)
        dckvn = jnp.zeros((tm, KV_LORA), F32)
        dkr = jnp.zeros((tm, KR_W), F32)
        for h in range(N_HEADS):
            sl = slice(h * HEAD_W, (h + 1) * HEAD_W)
            dqh = jnp.transpose(dq_ref[h, 0])
            droped = jnp.where(lane < 64, dqh[:, 128:256], 0.0)
            dqp = jnp.concatenate([dqh[:, 0:128], _rope_pair(droped) * c], axis=1).astype(BF16)
            dcqn = dcqn + _dot_nt(dqp, wq_ref[:, sl])
            dwq_ref[:, sl] += _dot_tn(cqn, dqp)
            dkr = dkr + jnp.where(lane < 64, dk_ref[h, :, 128:256], 0.0)
            dkvp = jnp.concatenate([dk_ref[h, :, 0:128], dv_ref[h]], axis=1).astype(BF16)
            dckvn = dckvn + _dot_nt(dkvp, wkv_ref[:, sl])
            dwkv_ref[:, sl] += _dot_tn(ckvn, dkvp)
        cqv, ckvv = cq_ref[...], ckv_ref[...]
        dcq, dgq = _rms_bwd(cqv, _rms_scale(cqv), qn_ref[...], dcqn)
        dckv, dgkv = _rms_bwd(ckvv, _rms_scale(ckvv), kvn_ref[...], dckvn)
        dqn_ref[...] += jnp.sum(dgq, axis=0, keepdims=True)
        dkvn_ref[...] += jnp.sum(dgkv, axis=0, keepdims=True)
        dmla_ref[:, 0:256] = dcq.astype(BF16)
        dmla_ref[:, 256:512] = dckv.astype(BF16)
        dmla_ref[:, 512:640] = (_rope_pair(dkr) * c).astype(BF16)

    hb = lambda w: pl.BlockSpec((N_HEADS, tm, w), lambda i: (0, i, 0))
    wide = N_HEADS * HEAD_W
    per_q = dqt.shape[3] // tm
    dq_spec = pl.BlockSpec((N_HEADS, 1, HEAD_W, tm), lambda i: (0, i // per_q, 0, i % per_q))
    return _pcall(
        body, "mla_bwd", (t // tm,),
        [dq_spec, hb(HEAD_W), hb(V_HEAD), _rows(tm, Q_LORA), _rows(tm, KV_LORA), _rows(tm, Q_LORA), _rows(tm, KV_LORA),
         _rows(tm, KR_W), _full(Q_LORA, wide), _full(KV_LORA, wide), _full(1, Q_LORA), _full(1, KV_LORA)],
        [_rows(tm, 640), _full(Q_LORA, wide), _full(KV_LORA, wide), _full(1, Q_LORA), _full(1, KV_LORA)],
        [_sds((t, 640), BF16), _sds((Q_LORA, wide)), _sds((KV_LORA, wide)), _sds((1, Q_LORA)), _sds((1, KV_LORA))],
    )(dqt, dk, dv, cqn, ckvn, cq, ckv, rope_c, wq, wkv, qn, kvn)


def _lru_bwd(dh, xa, h, rx, conv_w, wa, ba, wx, bx, lam, tb, comm=None):
    t = dh.shape[0]
    nb = t // tb

    def body(dh_ref, xa_ref, h_ref, hp_ref, x_ref, cw_ref, wa_ref, ba_ref, wx_ref, bx_ref, lam_ref,
             drx_ref, dcw_ref, dcb_ref, dwa_ref, dba_ref, dwx_ref, dbx_ref, dlam_ref, gc, dxn, tmp, pre_r, pre_i):
        step = pl.program_id(0)
        first_block = step == nb - 1

        @pl.when(step == 0)
        def _():
            gc[...] = jnp.zeros_like(gc)
            dxn[...] = jnp.zeros_like(dxn)
            for ref in (dcw_ref, dcb_ref, dwa_ref, dba_ref, dwx_ref, dbx_ref, dlam_ref):
                ref[...] = jnp.zeros_like(ref)

        xa = xa_ref[...]
        r, gi = _lru_gates(xa, wa_ref, ba_ref[...], wx_ref, bx_ref[...], pre_r, pre_i)
        lamv = lam_ref[...]
        sp = _softplus(-lamv)
        la = (-LRU_C * sp) * r
        a = jnp.exp(la)
        e2 = _expm1(2.0 * la)
        sq = jnp.sqrt(-e2)
        row = _row_iota((tb, D))
        cf = jnp.where(row == tb - 1, 1.0, pltpu.roll(a, tb - 1, 0))
        c3, b3 = _scan_groups_bwd(cf.reshape(tb // 8, 8, D), dh_ref[...].reshape(tb // 8, 8, D))
        carry = gc[...]
        for grp in reversed(range(tb // 8)):
            dg = b3[grp] + c3[grp] * carry
            pre_r[8 * grp:8 * grp + 8, :] = dg
            carry = dg[0:1, :]
        delta = pre_r[...]
        gc[...] = a[0:1, :] * carry
        hv = h_ref[...]
        hr = pltpu.roll(hv, 1, 0)
        tmp[...] = hr
        tmp[0:1, :] = jnp.where(first_block, 0.0, hp_ref[7:8, :])
        hprev = tmp[...]
        ix = gi * xa
        dla = (delta * hprev) * a - (delta * ix) * ((e2 + 1.0) / sq)
        dlam_ref[...] += jnp.sum(dla * r, axis=0, keepdims=True) * (LRU_C * _sigmoid(-lamv))
        dpr = (dla * (-LRU_C * sp)) * r * (1.0 - r)
        dsq = delta * sq
        dpi = (dsq * xa) * gi * (1.0 - gi)
        dba_ref[...] += jnp.sum(dpr, axis=0, keepdims=True)
        dbx_ref[...] += jnp.sum(dpi, axis=0, keepdims=True)
        pre_r[...] = dpr
        pre_i[...] = dpi
        xb = xa.astype(BF16)
        for n in range(RNN_BLOCKS):
            sl = slice(n * RNN_BW, (n + 1) * RNN_BW)
            dprn = pre_r[:, sl].astype(BF16)
            dpin = pre_i[:, sl].astype(BF16)
            dwa_ref[n] += _dot_tn(xb[:, sl], dprn)
            dwx_ref[n] += _dot_tn(xb[:, sl], dpin)
            tmp[:, sl] = _dot_nt(dprn, wa_ref[n]) + _dot_nt(dpin, wx_ref[n])
        dxa = dsq * gi + tmp[...]
        dcb_ref[...] += jnp.sum(dxa, axis=0, keepdims=True)
        xv = x_ref[...]
        drx = cw_ref[3:4, :] * dxa
        dcw_ref[3:4, :] += jnp.sum(dxa * xv, axis=0, keepdims=True)
        row8 = _row_iota((8, D))
        nxt = dxn[...]
        for s in (1, 2, 3):
            dr_ = pltpu.roll(dxa, tb - s, 0)
            tmp[...] = dr_
            tmp[tb - 8:tb, :] = jnp.where(row8 >= 8 - s, pltpu.roll(nxt, 8 - s, 0), dr_[tb - 8:tb, :])
            dxs = tmp[...]
            drx = drx + cw_ref[3 - s:4 - s, :] * dxs
            dcw_ref[3 - s:4 - s, :] += jnp.sum(dxs * xv, axis=0, keepdims=True)
        drx_ref[...] = drx.astype(BF16)
        dxn[...] = dxa[0:8, :]

    rev = pl.BlockSpec((tb, D), lambda i: (nb - 1 - i, 0))
    prev8 = pl.BlockSpec((8, D), lambda i: (jnp.maximum((nb - 1 - i) * (tb // 8) - 1, 0), 0))
    wblk = _full(RNN_BLOCKS, RNN_BW, RNN_BW)
    return _pcall(
        body, "lru_bwd", (nb,),
        [rev, rev, rev, prev8, rev, _full(4, D), wblk, _full(1, D), wblk, _full(1, D), _full(1, D)],
        [rev, _full(4, D), _full(1, D), wblk, _full(1, D), wblk, _full(1, D), _full(1, D)],
        [_sds((t, D), BF16), _sds((4, D)), _sds((1, D)), _sds((RNN_BLOCKS, RNN_BW, RNN_BW)), _sds((1, D)),
         _sds((RNN_BLOCKS, RNN_BW, RNN_BW)), _sds((1, D)), _sds((1, D))],
        scratch=[pltpu.VMEM((1, D), F32), pltpu.VMEM((8, D), F32), pltpu.VMEM((tb, D), F32), pltpu.VMEM((tb, D), F32),
                 pltpu.VMEM((tb, D), F32)],
        comm=comm,
    )(dh, xa, h, h, rx, conv_w, wa, ba, wx, bx, lam)


def _inproj_bwd(x, dh1, drx, dg3, dmla, w1, g, tm, comm=None):
    t = x.shape[0]

    def body(x_ref, d_ref, drx_ref, dg3_ref, dmla_ref, w_ref, g_ref, dx_ref, gacc_ref):
        @pl.when(pl.program_id(0) == 0)
        def _():
            gacc_ref[...] = jnp.zeros_like(gacc_ref)

        dxn = _dot_nt(drx_ref[...], w_ref[:, 0:D])
        for c0 in range(0, 3 * D, D):
            dxn = dxn + _dot_nt(dg3_ref[:, c0:c0 + D], w_ref[:, D + c0:2 * D + c0])
        dxn = dxn + _dot_nt(dmla_ref[...], w_ref[:, 4 * D:W1_COLS])
        xv = x_ref[...]
        dx, dgr = _rms_bwd(xv, _rms_scale(xv), g_ref[...], dxn)
        dx_ref[...] = d_ref[...] + dx
        gacc_ref[...] += jnp.sum(dgr, axis=0, keepdims=True)

    return _pcall(
        body, "inproj_bwd", (t // tm,),
        [_rows(tm, D), _rows(tm, D), _rows(tm, D), _rows(tm, 3 * D), _rows(tm, 640), _full(D, W1_COLS), _full(1, D)],
        [_rows(tm, D), _full(1, D)],
        [_sds((t, D)), _sds((1, D))],
        comm=comm,
    )(x, dh1, drx, dg3, dmla, w1, g)


def _pcall_indexed(body, name, index, grid, in_specs, out_specs, out_shape):
    call = pl.pallas_call(
        body, name=name, out_shape=out_shape,
        grid_spec=pltpu.PrefetchScalarGridSpec(num_scalar_prefetch=1, grid=grid, in_specs=in_specs, out_specs=out_specs),
        compiler_params=pltpu.CompilerParams(dimension_semantics=("arbitrary",) * len(grid), vmem_limit_bytes=V7X_VMEM_LIMIT))
    return lambda *operands: call(index, *operands)


def _pair_sum(halves, theirs, core, out_dtype, name):
    _, rows, cols = halves.shape
    tm = _row_tile(rows)

    def body(c_ref, a_ref, b_ref, o_ref):
        o_ref[...] = (a_ref[0] + b_ref[...]).astype(out_dtype)

    plain = pl.BlockSpec((tm, cols), lambda i, c: (i, 0))
    return _pcall_indexed(body, name, core, (rows // tm,),
                          [pl.BlockSpec((1, tm, cols), lambda i, c: (c[0], i, 0)), plain], plain,
                          _sds((rows, cols), out_dtype))(halves, theirs)


def _chip_sum(parts, recv, chip, name):
    _, rows, cols = parts.shape
    tm = _row_tile(rows)

    def body(c_ref, a_ref, r_ref, o_ref):
        o_ref[...] = ((a_ref[0].astype(F32) + r_ref[0].astype(F32)) + r_ref[1].astype(F32)) + r_ref[2].astype(F32)

    return _pcall_indexed(body, name, chip, (rows // tm,),
                          [pl.BlockSpec((1, tm, cols), lambda i, c: (c[0], i, 0)),
                           pl.BlockSpec((N_CHIPS - 1, tm, cols), lambda i, c: (0, i, 0))],
                          pl.BlockSpec((tm, cols), lambda i, c: (i, 0)), _sds((rows, cols)))(parts, recv)


def _adam_math(w, gv, m, v):
    mn = ADAM_B1 * m + (1.0 - ADAM_B1) * gv
    vn = ADAM_B2 * v + (1.0 - ADAM_B2) * (gv * gv)
    m_hat = mn / (1.0 - ADAM_B1 ** ADAM_STEP)
    v_hat = vn / (1.0 - ADAM_B2 ** ADAM_STEP)
    return -ADAM_LR * (m_hat / (jnp.sqrt(v_hat) + ADAM_EPS) + ADAM_WD * w), mn, vn


def _adamw(w, g, m, v, name):
    rows, cols = w.shape
    tm = _row_tile(rows)

    def body(w_ref, g_ref, m_ref, v_ref, d_ref, mo_ref, vo_ref):
        d_ref[...], mo_ref[...], vo_ref[...] = _adam_math(w_ref[...], g_ref[...], m_ref[...], v_ref[...])

    spec = _rows(tm, cols)
    return _pcall(body, name, (rows // tm,), [spec] * 4, [spec] * 3, [_sds((rows, cols))] * 3)(w, g, m, v)


def _adamw_halves(w, mine, theirs, m, v, core, name):
    rows, cols = w.shape
    tm = _row_tile(rows // 2)
    nh = rows // 2 // tm

    def body(c_ref, w_ref, a_ref, b_ref, m_ref, v_ref, g_ref, d_ref, mo_ref, vo_ref):
        gv = jnp.where(pl.program_id(0) // nh == c_ref[0], a_ref[...], b_ref[...])
        g_ref[...] = gv
        d_ref[...], mo_ref[...], vo_ref[...] = _adam_math(w_ref[...], gv, m_ref[...], v_ref[...])

    full = pl.BlockSpec((tm, cols), lambda i, c: (i, 0))
    half = pl.BlockSpec((tm, cols), lambda i, c: (i % nh, 0))
    return _pcall_indexed(body, name, core, (rows // tm,), [full, half, half, full, full], [full] * 4,
                          [_sds((rows, cols))] * 4)(w, mine, theirs, m, v)


REL_SIBLING = (0, 0, 1)
REL_CHIPS = ((1, 0, 0), (0, 1, 0), (1, 1, 0))


V7X_DMA_CHUNK_BYTES = 1 << 20


def _split_copy(src, dst, shape, itemsize):
    nbytes = math.prod(shape) * itemsize
    if nbytes <= V7X_DMA_CHUNK_BYTES or len(shape) < 2:
        return [(src, dst)]
    if len(shape) > 2:
        out = []
        for k in range(shape[0]):
            out += _split_copy(src.at[k], dst.at[k], shape[1:], itemsize)
        return out
    rows = shape[0]
    sub = 8 * (4 // itemsize)
    parts = max(1, min(-(-nbytes // V7X_DMA_CHUNK_BYTES), rows // sub))
    while rows % parts or (rows // parts) % sub:
        parts -= 1
    step = rows // parts
    return [(src.at[pl.ds(k * step, step)], dst.at[pl.ds(k * step, step)]) for k in range(parts)]


def _mesh_pos():
    return (lax.axis_index("x"), lax.axis_index("y"), lax.axis_index("c"))


def _make_copy(i, op, sems, pos, src=None, dst=None):
    rel = op[0]
    src, dst = (op[1], op[2]) if src is None else (src, dst)
    send_sems, recv_sems = sems
    if rel is None:
        return pltpu.make_async_copy(src, dst, send_sems.at[i])
    peer = tuple((p + r) % 2 for p, r in zip(pos, rel))
    return pltpu.make_async_remote_copy(src_ref=src, dst_ref=dst, send_sem=send_sems.at[i], recv_sem=recv_sems.at[i],
                                        device_id=peer, device_id_type=MESH_ID)


def _start_copies(ops, sems, pos, base=0):
    for i, op in enumerate(ops):
        for s_piece, d_piece in _split_copy(op[1], op[2], op[1].shape, jnp.dtype(op[1].dtype).itemsize):
            _make_copy(base + i, op, sems, pos, s_piece, d_piece).start()


def _wait_copies(ops, sems, pos, base=0):
    for i, op in enumerate(ops):
        _make_copy(base + i, op, sems, pos).wait()


def _comm(name, ins, out_shapes, n_ops, ops_fn):
    n_in, n_out = len(ins), len(out_shapes)

    def body(*refs):
        in_refs, out_refs = refs[:n_in], refs[n_in:n_in + n_out]
        sems = refs[n_in + n_out:]
        pos = _mesh_pos()
        ops = ops_fn(in_refs, out_refs, pos)
        assert len(ops) == n_ops
        _start_copies(ops, sems, pos)
        _wait_copies(ops, sems, pos)

    hbm = pl.BlockSpec(memory_space=pl.ANY)
    return pl.pallas_call(
        body, name=name, in_specs=[hbm] * n_in, out_specs=[hbm] * n_out, out_shape=list(out_shapes),
        scratch_shapes=[pltpu.SemaphoreType.DMA((n_ops,)), pltpu.SemaphoreType.DMA((n_ops,))],
    )(*ins)


def _chip_of(pos, rel=(0, 0, 0)):
    return 2 * ((pos[0] + rel[0]) % 2) + (pos[1] + rel[1]) % 2


def _gather_chips(shards, chip, name):
    def ops_fn(in_refs, out_refs, pos):
        me = _chip_of(pos)
        return [(rel, src, dst.at[me]) for src, dst in zip(in_refs, out_refs) for rel in REL_CHIPS]

    outs = _comm(name, shards, [_sds((N_CHIPS,) + s.shape, s.dtype) for s in shards], 3 * len(shards), ops_fn)
    return [lax.dynamic_update_index_in_dim(o, s, chip, 0) for o, s in zip(outs, shards)]


def _halved_gather_ops(pos, srcs, dsts, whole):
    me, c = _chip_of(pos), pos[2]
    ici, d2d = [], []
    for a, (src, dst) in enumerate(zip(srcs, dsts)):
        for rel in REL_CHIPS:
            if a in whole:
                ici.append((rel, src, dst.at[me]))
            else:
                ici.append((rel, src.at[c], dst.at[me, c]))
                arrived = dst.at[_chip_of(pos, rel), c]
                d2d.append((REL_SIBLING, arrived, arrived))
    return ici, d2d


def _gather_halved(shards, whole, chip, name):
    srcs = [s if a in whole else s.reshape(2, s.shape[0] // 2, s.shape[1]) for a, s in enumerate(shards)]
    n_sh = len(shards)
    n_ici, n_d2d = 3 * n_sh, 3 * (n_sh - len(whole))

    def body(*refs):
        in_refs, out_refs, sems = refs[:n_sh], refs[n_sh:2 * n_sh], refs[2 * n_sh:]
        pos = _mesh_pos()
        ici, d2d = _halved_gather_ops(pos, in_refs, out_refs, whole)
        _start_copies(ici, sems, pos)
        _wait_copies(ici, sems, pos)
        _start_copies(d2d, sems, pos, base=n_ici)
        _wait_copies(d2d, sems, pos, base=n_ici)

    hbm = pl.BlockSpec(memory_space=pl.ANY)
    outs = pl.pallas_call(
        body, name=name, in_specs=[hbm] * n_sh, out_specs=[hbm] * n_sh,
        out_shape=[_sds((N_CHIPS,) + s.shape, s.dtype) for s in srcs],
        scratch_shapes=[pltpu.SemaphoreType.DMA((n_ici + n_d2d,)), pltpu.SemaphoreType.DMA((n_ici + n_d2d,))],
    )(*srcs)
    return [lax.dynamic_update_index_in_dim(o, s, chip, 0).reshape((N_CHIPS,) + sh.shape)
            for o, s, sh in zip(outs, srcs, shards)]


def _split_comm(gs):
    def ops_fn(in_refs, out_refs, pos):
        return [(REL_SIBLING, src.at[1 - pos[2]], dst) for src, dst in zip(in_refs, out_refs)]

    return gs, [_sds(g.shape[1:], g.dtype) for g in gs], len(gs), ops_fn


def _exchange_comm(ps):
    def ops_fn(in_refs, out_refs, pos):
        return [(rel, src.at[_chip_of(pos, rel)], dst.at[j])
                for src, dst in zip(in_refs, out_refs) for j, rel in enumerate(REL_CHIPS)]

    return ps, [_sds((N_CHIPS - 1,) + p.shape[1:], p.dtype) for p in ps], 3 * len(ps), ops_fn


def _join_comm(hs):
    def ops_fn(in_refs, out_refs, pos):
        return [(REL_SIBLING, src, dst) for src, dst in zip(in_refs, out_refs)]

    return hs, [_sds(h.shape, h.dtype) for h in hs], len(hs), ops_fn


def _run_comm(name, comm):
    ins, shapes, n_ops, ops_fn = comm
    return _comm(name, ins, shapes, n_ops, ops_fn)


def _rot_cols(w):
    return jnp.concatenate([-w[..., 32:], w[..., :32]], axis=-1)


def _unrot_cols(dw):
    return jnp.concatenate([dw[..., 32:], -dw[..., :32]], axis=-1)


IN_OFFS = (0, 1024, 2048, 2304, 2560, 2624, 3648, 4672)


def _w1_from_w_in(w):
    seg = [w[:, IN_OFFS[i]:IN_OFFS[i + 1]] for i in range(7)]
    rnn_x, rnn_gate, cq, ckv, kr, ga, gb = seg
    return jnp.concatenate([rnn_x, rnn_gate, ga, gb, cq, ckv, kr, _rot_cols(kr)], axis=1)


def _w_in_grad_from_parts(d_rx, d_g3, d_mla):
    kr = d_mla[:, 512:576] + _unrot_cols(d_mla[:, 576:640])
    return jnp.concatenate([d_rx, d_g3[:, 0:D], d_mla[:, 0:512], kr, d_g3[:, D:3 * D]], axis=1)


def _wq_from_w_uq(w):
    w3 = w.reshape(Q_LORA, N_HEADS, QK_NOPE + QK_ROPE)
    rope = w3[..., QK_NOPE:]
    return jnp.concatenate([w3[..., :QK_NOPE], rope, _rot_cols(rope)], axis=-1).reshape(Q_LORA, N_HEADS * HEAD_W)


def _w_uq_grad_from_wq(dw):
    d3 = dw.reshape(Q_LORA, N_HEADS, HEAD_W)
    rope = d3[..., 128:192] + _unrot_cols(d3[..., 192:256])
    return jnp.concatenate([d3[..., :128], rope], axis=-1).reshape(Q_LORA, N_HEADS * (QK_NOPE + QK_ROPE))


def _cols_from_chunks(g):
    return g.transpose(1, 0, 2).reshape(g.shape[1], N_CHIPS * g.shape[2])


def _halves_of_col_chunks(dw):
    r, c4 = dw.shape
    return dw.reshape(2, r // 2, N_CHIPS, c4 // N_CHIPS).transpose(0, 2, 1, 3)


def _halves_of_row_chunks(dw):
    r4, c = dw.shape
    return dw.reshape(N_CHIPS, 2, r4 // (2 * N_CHIPS), c).transpose(1, 0, 2, 3)


def kernel(x, norm_mix, w_in, conv_w, conv_b, lru_wa, lru_ba, lru_wx, lru_bx, lru_lambda, q_norm, w_uq, kv_norm, w_ukv, w_out, norm_mlp, w_up, w_down, norm_final, loss_target, m_norm_mix, m_w_in, m_conv_w, m_conv_b, m_lru_wa, m_lru_ba, m_lru_wx, m_lru_bx, m_lru_lambda, m_q_norm, m_w_uq, m_kv_norm, m_w_ukv, m_w_out, m_norm_mlp, m_w_up, m_w_down, m_norm_final, v_norm_mix, v_w_in, v_conv_w, v_conv_b, v_lru_wa, v_lru_ba, v_lru_wx, v_lru_bx, v_lru_lambda, v_q_norm, v_w_uq, v_kv_norm, v_w_ukv, v_w_out, v_norm_mlp, v_w_up, v_w_down, v_norm_final):
    t = x.shape[1]
    tm = min(256, t)
    tq = min(512, max(tm, t // 4))
    x2 = x[0]
    target = loss_target[0]
    chip = 2 * lax.axis_index("x") + lax.axis_index("y")
    core = lax.axis_index("c")
    chip_ix, core_ix = chip.reshape(1).astype(jnp.int32), core.reshape(1).astype(jnp.int32)
    row = lambda p: p.reshape(1, -1)

    big_shards = (w_in, w_uq, w_ukv, w_out, w_up, w_down)
    w_in_g, conv_w_g = _gather_halved([w_in.astype(BF16), conv_w], (1,), chip, "weight_gather_first")
    w1 = _w1_from_w_in(_cols_from_chunks(w_in_g))
    conv_w_f = _cols_from_chunks(conv_w_g)
    wa_b, wx_b = lru_wa.astype(BF16), lru_wx.astype(BF16)

    pos = jnp.arange(t, dtype=F32)
    inv_freq = 1.0 / (ROPE_THETA ** (jnp.arange(0, QK_ROPE, 2, dtype=F32) / QK_ROPE))
    ang = pos[:, None] * inv_freq[None, :]
    rope_c = jnp.concatenate([jnp.cos(ang), jnp.cos(ang), jnp.sin(ang), jnp.sin(ang)], axis=-1)

    (xn, rx, g3, cq, ckv, kr), gathered = _inproj(x2, row(norm_mix), w1, tm, [w.astype(BF16) for w in big_shards[1:]], chip)
    wq = _wq_from_w_uq(_cols_from_chunks(gathered[0]))
    wkv = _cols_from_chunks(gathered[1])
    w_out_f = gathered[2].reshape(D, D)
    w_up_f = _cols_from_chunks(gathered[3])
    w_down_f = gathered[4].reshape(D_FF, D)
    h, xa = _lru_fwd(rx, conv_w_f, row(conv_b), wa_b, row(lru_ba), wx_b, row(lru_bx), row(lru_lambda), tm)
    q, k, v, cqn, ckvn = _mla_proj(cq, ckv, kr, row(q_norm), row(kv_norm), wq, wkv, rope_c, tm)
    nq = t // tq
    yb, lse = _flash_fwd(q, k, v, tq)
    h1, merged = _merge_out(x2, h, g3, yb, w_out_f, tm)
    u, n2 = _mlp_up(h1, row(norm_mlp), w_up_f, tm)
    act, dh2, loss_blk, g_norm_final = _mlp_down_loss(u, h1, target, w_down_f, row(norm_final), tm)

    g_w_down = _matmul_tn(act, dh2, "grad_w_down", "rows")
    du = _mlp_bwd_act(dh2, u, w_down_f, tm)
    dh1, g_norm_mlp = _mlp_bwd_in(du, dh2, h1, w_up_f, row(norm_mlp), tm)
    g_w_up = _matmul_tn(n2, du, "grad_w_up", "cols")
    dg3, dyb, delta, dh, g_w_out = _merge_bwd(dh1, w_out_f, g3, h, yb, merged, tm)
    delta = delta.reshape(N_HEADS, 8, nq, tq).swapaxes(1, 2)
    def pair_sums(hvs, theirs, dtypes, tag):
        return [_pair_sum(hv.reshape(2, -1, hv.shape[-1]), r.reshape(-1, r.shape[-1]), core_ix, dt, f"grad_pair_sum_{tag}{a}").reshape(r.shape)
                for a, (hv, r, dt) in enumerate(zip(hvs, theirs, dtypes))]

    def chip_sums(parts, received, tag):
        return [_chip_sum(p, r, chip_ix, f"grad_chip_sum_{tag}{a}") for a, (p, r) in enumerate(zip(parts, received))]

    early = [_halves_of_row_chunks(g_w_out), g_w_up, g_w_down]
    (dq, dk, dv), early_theirs = _flash_bwd(q, k, v, dyb, lse, delta, tq, comm=_split_comm(early))
    early_parts = pair_sums(early, early_theirs, [BF16] * 3, "early")
    dmla, g_wq, g_wkv, g_q_norm, g_kv_norm = _mla_bwd(dq, dk, dv, cqn, ckvn, cq, ckv, rope_c, wq, wkv, row(q_norm), row(kv_norm), tm)
    (drx, g_conv_w, g_conv_b, g_wa, g_ba, g_wx, g_bx, g_lam), early_received = _lru_bwd(
        dh, xa, h, rx, conv_w_f, wa_b, row(lru_ba), wx_b, row(lru_bx), row(lru_lambda), tm, comm=_exchange_comm(early_parts))
    early_reduced = chip_sums(early_parts, early_received, "early")
    (grad_x, g_norm_mix), early_sibling = _inproj_bwd(x2, dh1, drx, dg3, dmla, w1, row(norm_mix), tm, comm=_join_comm(early_reduced))
    g_w_in = _w_in_grad_from_parts(_matmul_tn(xn, drx, "grad_w_in_rx"), _matmul_tn(xn, dg3, "grad_w_in_gates"),
                                   _matmul_tn(xn, dmla, "grad_w_in_mla"))
    g_w_uq = _w_uq_grad_from_wq(g_wq)

    smalls = (g_norm_mix, g_conv_b, g_wa, g_ba, g_wx, g_bx, g_lam, g_q_norm, g_kv_norm, g_norm_mlp, g_norm_final, g_conv_w)
    s_flat = jnp.concatenate([s.reshape(-1) for s in smalls] + [loss_blk[0, 0:1], jnp.zeros((S_LEN - N_SMALL - CONVW_SIZE - 1,), F32)])
    late = [_halves_of_col_chunks(g_w_in), _halves_of_col_chunks(g_w_uq), _halves_of_col_chunks(g_wkv),
            s_flat.reshape(N_CHIPS, 2, S_ROWS_HALF, 128).transpose(1, 0, 2, 3)]
    late_theirs = _run_comm("grad_sibling_split", _split_comm(late))
    late_parts = pair_sums(late, late_theirs, [BF16] * 3 + [F32], "late")
    late_reduced = chip_sums(late_parts, _run_comm("grad_chip_exchange", _exchange_comm(late_parts)), "late")
    late_sibling = _run_comm("grad_sibling_join", _join_comm(late_reduced))
    reduced = late_reduced[:3] + early_reduced
    reduced_sibling = list(late_sibling[:3]) + list(early_sibling)
    s_mine, s_theirs = late_reduced[3], late_sibling[3]
    s_chunk = jnp.where(core == 0, jnp.concatenate([s_mine, s_theirs]), jnp.concatenate([s_theirs, s_mine]))
    s_all = _gather_chips([s_chunk], chip, "small_grad_gather")[0].reshape(-1)

    small_grads = []
    off = 0
    for shp, n in zip(SMALL_SHAPES, SMALL_SIZES):
        small_grads.append(s_all[off:off + n].reshape(shp))
        off += n
    g_conv_w_mine = lax.dynamic_slice_in_dim(s_all[off:off + CONVW_SIZE].reshape(4, D), chip * (D // N_CHIPS), D // N_CHIPS, axis=1)
    loss = s_all[off + CONVW_SIZE]

    big_m = (m_w_in, m_w_uq, m_w_ukv, m_w_out, m_w_up, m_w_down)
    big_v = (v_w_in, v_w_uq, v_w_ukv, v_w_out, v_w_up, v_w_down)
    big_names = ("w_in", "w_uq", "w_ukv", "w_out", "w_up", "w_down")
    big_upd = [_adamw_halves(w, gm, gt, m, v, core_ix, "adamw_" + n)
               for w, gm, gt, m, v, n in zip(big_shards, reduced, reduced_sibling, big_m, big_v, big_names)]

    small_w = (norm_mix, conv_b, lru_wa, lru_ba, lru_wx, lru_bx, lru_lambda, q_norm, kv_norm, norm_mlp, norm_final)
    small_m = (m_norm_mix, m_conv_b, m_lru_wa, m_lru_ba, m_lru_wx, m_lru_bx, m_lru_lambda, m_q_norm, m_kv_norm, m_norm_mlp, m_norm_final)
    small_v = (v_norm_mix, v_conv_b, v_lru_wa, v_lru_ba, v_lru_wx, v_lru_bx, v_lru_lambda, v_q_norm, v_kv_norm, v_norm_mlp, v_norm_final)

    def pack(items, last, fill):
        flat = jnp.concatenate([i.reshape(-1) for i in items] + [last.reshape(-1)])
        return jnp.concatenate([flat, jnp.full((PACK_ROWS * 128 - flat.shape[0],), fill, F32)]).reshape(PACK_ROWS, 128)

    packed = _adamw(pack(small_w, conv_w, 0.0), pack(small_grads, g_conv_w_mine, 0.0), pack(small_m, m_conv_w, 0.0),
                    pack(small_v, v_conv_w, 1.0), "adamw_small")

    def unpack(p):
        flat = p.reshape(-1)
        outs, o = [], 0
        for shp, n in zip(SMALL_SHAPES, SMALL_SIZES):
            outs.append(flat[o:o + n].reshape(shp))
            o += n
        return outs, flat[o:o + CONVW_SIZE // N_CHIPS].reshape(4, D // N_CHIPS)

    order = ("norm_mix", "w_in", "conv_w", "conv_b", "lru_wa", "lru_ba", "lru_wx", "lru_bx", "lru_lambda", "q_norm", "w_uq", "kv_norm",
             "w_ukv", "w_out", "norm_mlp", "w_up", "w_down", "norm_final")

    def assemble(small_list, conv_w_item, big_list):
        table = dict(zip(SMALL_NAMES, small_list))
        table["conv_w"] = conv_w_item
        table.update(zip(big_names, big_list))
        return [table[n] for n in order]

    outs = [loss, grad_x.reshape(1, t, D)]
    outs += assemble(small_grads, g_conv_w_mine, [b[0] for b in big_upd])
    for j in range(3):
        sm, cw = unpack(packed[j])
        outs += assemble(sm, cw, [b[j + 1] for b in big_upd])
    return tuple(outs)
```

```python
import functools
import math

import jax
import jax.numpy as jnp
from jax import lax
from jax.experimental import pallas as pl
from jax.experimental.pallas import tpu as pltpu

F32 = jnp.float32
BF16 = jnp.bfloat16

D = 1024
N_HEADS = 8
QK_NOPE = 128
QK_ROPE = 64
V_HEAD = 128
Q_LORA = 256
KV_LORA = 256
D_FF = 4096
RNN_BLOCKS = 8
RNN_BW = 128
LRU_C = 8.0
EPS = 1e-6
ROPE_THETA = 10000.0
HEAD_W = 256
KR_W = 128
W1_COLS = 4 * D + Q_LORA + KV_LORA + KR_W
SM_SCALE = (QK_NOPE + QK_ROPE) ** -0.5
EXP2_SCALE = SM_SCALE * math.log2(math.e)
NEG = float(jnp.finfo(jnp.float32).min)

ADAM_LR = 0.001
ADAM_B1 = 0.9
ADAM_B2 = 0.999
ADAM_EPS = 1e-08
ADAM_WD = 0.01
ADAM_STEP = 10

N_CHIPS = 4
V7X_VMEM_LIMIT = 56 * 1024 * 1024
MESH_ID = pl.DeviceIdType.MESH

SMALL_NAMES = ("norm_mix", "conv_b", "lru_wa", "lru_ba", "lru_wx", "lru_bx", "lru_lambda", "q_norm", "kv_norm", "norm_mlp", "norm_final")
SMALL_SHAPES = ((D,), (D,), (RNN_BLOCKS, RNN_BW, RNN_BW), (RNN_BLOCKS, RNN_BW), (RNN_BLOCKS, RNN_BW, RNN_BW), (RNN_BLOCKS, RNN_BW), (D,),
                (Q_LORA,), (KV_LORA,), (D,), (D,))
SMALL_SIZES = tuple(math.prod(s) for s in SMALL_SHAPES)
N_SMALL = sum(SMALL_SIZES)
CONVW_SIZE = 4 * D
S_LEN = -(-(N_SMALL + CONVW_SIZE) // 8192) * 8192
S_ROWS_HALF = S_LEN // (N_CHIPS * 2 * 128)
PACK_ROWS = -(-(N_SMALL + CONVW_SIZE // N_CHIPS) // (256 * 128)) * 256


def _pcall(body, name, grid, in_specs, out_specs, out_shape, scratch=(), comm=None):
    params = pltpu.CompilerParams(dimension_semantics=("arbitrary",) * len(grid), vmem_limit_bytes=V7X_VMEM_LIMIT)
    if comm is None:
        return pl.pallas_call(body, name=name, grid=grid, in_specs=in_specs, out_specs=out_specs, out_shape=out_shape,
                              scratch_shapes=list(scratch), compiler_params=params)
    c_ins, c_shapes, n_ops, ops_fn = comm
    single = not isinstance(out_specs, (list, tuple))
    out_specs, out_shape = ([out_specs], [out_shape]) if single else (list(out_specs), list(out_shape))
    n_in, n_out, n_sc, n_ci, n_co = len(in_specs), len(out_specs), len(scratch), len(c_ins), len(c_shapes)

    def wrapped(*refs):
        ins, refs = refs[:n_in], refs[n_in:]
        c_in_refs, refs = refs[:n_ci], refs[n_ci:]
        outs, refs = refs[:n_out], refs[n_out:]
        c_out_refs, refs = refs[:n_co], refs[n_co:]
        own_scratch, sems = refs[:n_sc], refs[n_sc:]
        pos = _mesh_pos()
        ops = ops_fn(c_in_refs, c_out_refs, pos)
        assert len(ops) == n_ops
        first, last = True, True
        for d, n in enumerate(grid):
            first = first & (pl.program_id(d) == 0)
            last = last & (pl.program_id(d) == n - 1)

        @pl.when(first)
        def _():
            _start_copies(ops, sems, pos)

        body(*ins, *outs, *own_scratch)

        @pl.when(last)
        def _():
            _wait_copies(ops, sems, pos)

    hbm = pl.BlockSpec(memory_space=pl.ANY)
    call = pl.pallas_call(
        wrapped, name=name, grid=grid, in_specs=list(in_specs) + [hbm] * n_ci, out_specs=out_specs + [hbm] * n_co,
        out_shape=out_shape + list(c_shapes),
        scratch_shapes=list(scratch) + [pltpu.SemaphoreType.DMA((n_ops,)), pltpu.SemaphoreType.DMA((n_ops,))],
        compiler_params=params)

    def run(*operands):
        res = call(*operands, *c_ins)
        own = res[0] if single else res[:n_out]
        return own, res[n_out:]

    return run


def _rows(tm, w):
    return pl.BlockSpec((tm, w), lambda i: (i, 0))


def _full(*shape):
    return pl.BlockSpec(shape, lambda *_: (0,) * len(shape))


def _sds(shape, dtype=F32):
    return jax.ShapeDtypeStruct(shape, dtype)


def _row_tile(rows, cap=256):
    t = min(rows, cap)
    while rows % t or t % 8:
        t -= 1
    return t


def _dot(a, b):
    return jnp.dot(a, b, preferred_element_type=F32)


def _dot_nt(a, b):
    return lax.dot_general(a, b, (((1,), (1,)), ((), ())), preferred_element_type=F32)


def _dot_tn(a, b):
    return lax.dot_general(a, b, (((0,), (0,)), ((), ())), preferred_element_type=F32)


def _sigmoid(x):
    return 1.0 / (1.0 + jnp.exp(-x))


_GELU_C = math.sqrt(2.0 / math.pi)


def _gelu(x):
    return x * (0.5 * (1.0 + jnp.tanh(_GELU_C * (x + 0.044715 * (x * x * x)))))


def _gelu_grad(x):
    t = jnp.tanh(_GELU_C * (x + 0.044715 * (x * x * x)))
    cdf = 0.5 * (1.0 + t)
    return cdf + x * (0.5 * (1.0 - t * t) * _GELU_C * (1.0 + 3.0 * 0.044715 * (x * x)))


def _rms_scale(x):
    return lax.rsqrt(jnp.mean(x * x, axis=-1, keepdims=True) + EPS)


def _rms_bwd(x, rs, g, dy):
    gdy = dy * g
    dx = rs * gdy - x * ((rs * rs * rs) * jnp.mean(gdy * x, axis=-1, keepdims=True))
    return dx, dy * (x * rs)


def _log1p(e):
    u = 1.0 + e
    d = u - 1.0
    return jnp.where(d == 0.0, e, jnp.log(u) * (e / jnp.where(d == 0.0, 1.0, d)))


def _softplus(y):
    return jnp.maximum(y, 0.0) + _log1p(jnp.exp(-jnp.abs(y)))


def _expm1(x):
    u = jnp.exp(x)
    lu = jnp.log(u)
    safe = jnp.where((u == 1.0) | (u == 0.0), 1.0, lu)
    return jnp.where(u == 1.0, x, jnp.where(u == 0.0, -1.0, (u - 1.0) * (x / safe)))


def _row_iota(shape):
    return lax.broadcasted_iota(jnp.int32, shape, 0)


def _lane_iota(shape):
    return lax.broadcasted_iota(jnp.int32, shape, 1)


def _scan_groups_fwd(a, b):
    sub = lax.broadcasted_iota(jnp.int32, a.shape, 1)
    for sh in (1, 2, 4):
        m = sub >= sh
        b = jnp.where(m, a * pltpu.roll(b, sh, 1) + b, b)
        a = jnp.where(m, a * pltpu.roll(a, sh, 1), a)
    return a, b


def _scan_groups_bwd(c, b):
    sub = lax.broadcasted_iota(jnp.int32, c.shape, 1)
    for sh in (1, 2, 4):
        m = sub < 8 - sh
        b = jnp.where(m, b + c * pltpu.roll(b, 8 - sh, 1), b)
        c = jnp.where(m, c * pltpu.roll(c, 8 - sh, 1), c)
    return c, b


def _rope_pair(gc):
    return gc + pltpu.roll(gc, 64, 1)


def _inproj(x, g, w1, tm, shards, chip):
    t = x.shape[0]
    nsteps = t // tm
    widths = (D, 3 * D, Q_LORA, KV_LORA, KR_W)
    srcs = [s.reshape(2, s.shape[0] // 2, s.shape[1]) for s in shards]
    n_sh = len(shards)

    def body(x_ref, g_ref, w_ref, *refs):
        sh_refs = refs[:n_sh]
        xn_ref, rx_ref, g3_ref, cq_ref, ckv_ref, kr_ref = refs[n_sh:n_sh + 6]
        gat_refs = refs[n_sh + 6:2 * n_sh + 6]
        sems = refs[2 * n_sh + 6:]
        pos = _mesh_pos()
        ici, d2d = _halved_gather_ops(pos, sh_refs, gat_refs, ())

        @pl.when(pl.program_id(0) == 0)
        def _():
            _start_copies(ici, sems, pos)

        xv = x_ref[...]
        xn = (xv * _rms_scale(xv) * g_ref[...]).astype(BF16)
        xn_ref[...] = xn
        col = 0
        for ref, w in zip((rx_ref, g3_ref, cq_ref, ckv_ref, kr_ref), widths):
            for c0 in range(0, w, 512):
                cw = min(512, w - c0)
                ref[:, c0:c0 + cw] = _dot(xn, w_ref[:, col + c0:col + c0 + cw])
            col += w

        @pl.when(pl.program_id(0) == nsteps - 1)
        def _():
            _wait_copies(ici, sems, pos)
            _start_copies(d2d, sems, pos, base=len(ici))
            _wait_copies(d2d, sems, pos, base=len(ici))

    hbm = pl.BlockSpec(memory_space=pl.ANY)
    outs = pl.pallas_call(
        body, name="inproj", grid=(nsteps,),
        in_specs=[_rows(tm, D), _full(1, D), _full(D, W1_COLS)] + [hbm] * n_sh,
        out_specs=[_rows(tm, D)] + [_rows(tm, w) for w in widths] + [hbm] * n_sh,
        out_shape=[_sds((t, D), BF16)] + [_sds((t, w)) for w in widths] + [_sds((N_CHIPS,) + s.shape, s.dtype) for s in srcs],
        scratch_shapes=[pltpu.SemaphoreType.DMA((6 * n_sh,)), pltpu.SemaphoreType.DMA((6 * n_sh,))],
        compiler_params=pltpu.CompilerParams(dimension_semantics=("arbitrary",), vmem_limit_bytes=V7X_VMEM_LIMIT),
    )(x, g, w1, *srcs)
    gathered = [lax.dynamic_update_index_in_dim(o, s, chip, 0).reshape((N_CHIPS,) + sh.shape)
                for o, s, sh in zip(outs[6:], srcs, shards)]
    return outs[:6], gathered


def _lru_gates(xa, wa_ref, ba, wx_ref, bx, pre_r, pre_i):
    xb = xa.astype(BF16)
    for n in range(RNN_BLOCKS):
        sl = slice(n * RNN_BW, (n + 1) * RNN_BW)
        pre_r[:, sl] = _dot(xb[:, sl], wa_ref[n])
        pre_i[:, sl] = _dot(xb[:, sl], wx_ref[n])
    r = _sigmoid(pre_r[...] + ba)
    i = _sigmoid(pre_i[...] + bx)
    return r, i


def _lru_fwd(rx, conv_w, conv_b, wa, ba, wx, bx, lam, tb):
    t = rx.shape[0]
    nb = t // tb

    def body(x_ref, xp_ref, cw_ref, cb_ref, wa_ref, ba_ref, wx_ref, bx_ref, lam_ref, h_ref, xa_ref, hc, tmp, pre_r, pre_i):
        i_blk = pl.program_id(0)

        @pl.when(i_blk == 0)
        def _():
            hc[...] = jnp.zeros_like(hc)

        xv = x_ref[...]
        xp = jnp.where(i_blk > 0, xp_ref[...], 0.0)
        row8 = _row_iota((8, D))
        xa = cb_ref[...] + cw_ref[3:4, :] * xv
        for s in (1, 2, 3):
            xr = pltpu.roll(xv, s, 0)
            tmp[...] = xr
            tmp[0:8, :] = jnp.where(row8 < s, pltpu.roll(xp, s, 0), xr[0:8, :])
            xa = xa + cw_ref[3 - s:4 - s, :] * tmp[...]
        xa_ref[...] = xa
        r, gi = _lru_gates(xa, wa_ref, ba_ref[...], wx_ref, bx_ref[...], pre_r, pre_i)
        la = (-LRU_C * _softplus(-lam_ref[...])) * r
        a = jnp.exp(la)
        b = jnp.sqrt(-_expm1(2.0 * la)) * (gi * xa)
        a3, b3 = _scan_groups_fwd(a.reshape(tb // 8, 8, D), b.reshape(tb // 8, 8, D))
        carry = hc[...]
        for grp in range(tb // 8):
            hg = a3[grp] * carry + b3[grp]
            h_ref[8 * grp:8 * grp + 8, :] = hg
            carry = hg[7:8, :]
        hc[...] = carry

    prev8 = pl.BlockSpec((8, D), lambda i: (jnp.maximum(i * (tb // 8) - 1, 0), 0))
    return _pcall(
        body, "lru_fwd", (nb,),
        [_rows(tb, D), prev8, _full(4, D), _full(1, D), _full(RNN_BLOCKS, RNN_BW, RNN_BW), _full(1, D),
         _full(RNN_BLOCKS, RNN_BW, RNN_BW), _full(1, D), _full(1, D)],
        [_rows(tb, D), _rows(tb, D)],
        [_sds((t, D)), _sds((t, D))],
        scratch=[pltpu.VMEM((1, D), F32), pltpu.VMEM((tb, D), F32), pltpu.VMEM((tb, D), F32), pltpu.VMEM((tb, D), F32)],
    )(rx, rx, conv_w, conv_b, wa, ba, wx, bx, lam)


def _mla_proj(cq, ckv, kr, qn, kvn, wq, wkv, rope_c, tm):
    t = cq.shape[0]

    def body(cq_ref, ckv_ref, kr_ref, qn_ref, kvn_ref, wq_ref, wkv_ref, c_ref, q_ref, k_ref, v_ref, cqn_ref, ckvn_ref):
        cqv = cq_ref[...]
        cqn = (cqv * _rms_scale(cqv) * qn_ref[...]).astype(BF16)
        ckvv = ckv_ref[...]
        ckvn = (ckvv * _rms_scale(ckvv) * kvn_ref[...]).astype(BF16)
        cqn_ref[...] = cqn
        ckvn_ref[...] = ckvn
        c = c_ref[...]
        lane = _lane_iota((tm, KR_W))
        kro = jnp.where(lane < 64, _rope_pair(kr_ref[...] * c), 0.0).astype(BF16)
        for h in range(N_HEADS):
            sl = slice(h * HEAD_W, (h + 1) * HEAD_W)
            qh = _dot(cqn, wq_ref[:, sl])
            q_ref[h, :, 0:128] = (qh[:, 0:128] * EXP2_SCALE).astype(BF16)
            q_ref[h, :, 128:256] = (_rope_pair(qh[:, 128:256] * c) * EXP2_SCALE).astype(BF16)
            kvh = _dot(ckvn, wkv_ref[:, sl])
            k_ref[h, :, 0:128] = kvh[:, 0:128].astype(BF16)
            k_ref[h, :, 128:256] = kro
            v_ref[h, :, 0:V_HEAD] = kvh[:, 128:256].astype(BF16)
            v_ref[h, :, V_HEAD:2 * V_HEAD] = jnp.ones((tm, V_HEAD), BF16)

    hb = lambda w: pl.BlockSpec((N_HEADS, tm, w), lambda i: (0, i, 0))
    return _pcall(
        body, "mla_proj", (t // tm,),
        [_rows(tm, Q_LORA), _rows(tm, KV_LORA), _rows(tm, KR_W), _full(1, Q_LORA), _full(1, KV_LORA),
         _full(Q_LORA, N_HEADS * HEAD_W), _full(KV_LORA, N_HEADS * HEAD_W), _rows(tm, KR_W)],
        [hb(HEAD_W), hb(HEAD_W), hb(2 * V_HEAD), _rows(tm, Q_LORA), _rows(tm, KV_LORA)],
        [_sds((N_HEADS, t, HEAD_W), BF16), _sds((N_HEADS, t, HEAD_W), BF16), _sds((N_HEADS, t, 2 * V_HEAD), BF16),
         _sds((t, Q_LORA), BF16), _sds((t, KV_LORA), BF16)],
    )(cq, ckv, kr, qn, kvn, wq, wkv, rope_c)


def _flash_fwd(q, k, v, tq):
    t = q.shape[1]
    nq = t // tq

    def body(q_ref, k_ref, v_ref, o_ref, lse_ref, s_even, s_odd):
        qi = pl.program_id(1)
        qv = q_ref[0]

        def scores(ki, buf):
            buf[...] = _dot_nt(qv, k_ref[0, pl.ds(pl.multiple_of(ki * tq, tq), tq), :])

        def softmax_pv(ki, buf, carry, diagonal):
            m, acc = carry
            s = buf[...]
            if diagonal:
                s = jnp.where(_row_iota((tq, tq)) >= _lane_iota((tq, tq)), s, NEG)
            m_new = jnp.maximum(m, jnp.max(s, axis=1, keepdims=True))
            p = jnp.exp2(s - m_new)
            alpha = jnp.exp2(m - m_new)
            acc = alpha * acc + _dot(p.astype(BF16), v_ref[0, pl.ds(pl.multiple_of(ki * tq, tq), tq), :])
            return m_new, acc

        def finish(carry):
            m, acc = carry
            l = acc[:, V_HEAD:2 * V_HEAD]
            o_ref[...] = acc[:, 0:V_HEAD] / l
            lse = m + jnp.log(l) * math.log2(math.e)
            lse_ref[0, 0] = jnp.transpose(lse)[0:8, :]

        def two(i, carry):
            scores(2 * i + 1, s_odd)
            carry = softmax_pv(2 * i, s_even, carry, False)
            scores(2 * i + 2, s_even)
            return softmax_pv(2 * i + 1, s_odd, carry, False)

        init = (jnp.full((tq, 1), -jnp.inf, F32), jnp.zeros((tq, 2 * V_HEAD), F32))
        scores(0, s_even)
        carry = lax.fori_loop(0, qi // 2, two, init)

        @pl.when(qi % 2 == 0)
        def _():
            finish(softmax_pv(qi, s_even, carry, True))

        @pl.when(qi % 2 == 1)
        def _():
            scores(qi, s_odd)
            finish(softmax_pv(qi, s_odd, softmax_pv(qi - 1, s_even, carry, False), True))

    head = lambda w: pl.BlockSpec((1, t, w), lambda h, qi: (h, 0, 0))
    return _pcall(
        body, "flash_fwd", (N_HEADS, nq),
        [pl.BlockSpec((1, tq, HEAD_W), lambda h, qi: (h, qi, 0)), head(HEAD_W), head(2 * V_HEAD)],
        [pl.BlockSpec((tq, V_HEAD), lambda h, qi: (qi, h)), pl.BlockSpec((1, 1, 8, tq), lambda h, qi: (h, qi, 0, 0))],
        [_sds((t, D)), _sds((N_HEADS, nq, 8, tq))],
        scratch=[pltpu.VMEM((tq, tq), F32), pltpu.VMEM((tq, tq), F32)],
    )(q, k, v)


def _merge_out(x, h, g3, yb, w_out, tm):
    t = x.shape[0]

    def body(x_ref, h_ref, g3_ref, yb_ref, w_ref, h1_ref, mg_ref):
        ya = h_ref[...] * _gelu(g3_ref[:, 0:D])
        merged = (_sigmoid(g3_ref[:, D:2 * D]) * ya + _sigmoid(g3_ref[:, 2 * D:3 * D]) * yb_ref[...]).astype(BF16)
        mg_ref[...] = merged
        h1_ref[...] = x_ref[...] + _dot(merged, w_ref[...])

    return _pcall(
        body, "merge_out", (t // tm,),
        [_rows(tm, D), _rows(tm, D), _rows(tm, 3 * D), _rows(tm, D), _full(D, D)],
        [_rows(tm, D), _rows(tm, D)],
        [_sds((t, D)), _sds((t, D), BF16)],
    )(x, h, g3, yb, w_out)


def _mlp_up(h1, g, w_up, tm):
    t = h1.shape[0]

    def body(h_ref, g_ref, w_ref, u_ref, n2_ref):
        hv = h_ref[...]
        n2 = (hv * _rms_scale(hv) * g_ref[...]).astype(BF16)
        n2_ref[...] = n2
        for c0 in range(0, D_FF, 512):
            u_ref[:, c0:c0 + 512] = _dot(n2, w_ref[:, c0:c0 + 512])

    return _pcall(
        body, "mlp_up", (t // tm,),
        [_rows(tm, D), _full(1, D), _full(D, D_FF)],
        [_rows(tm, D_FF), _rows(tm, D)],
        [_sds((t, D_FF)), _sds((t, D), BF16)],
    )(h1, g, w_up)


def _mlp_down_loss(u, h1, target, w_down, g, tm):
    t = u.shape[0]

    def body(u_ref, h1_ref, tg_ref, w_ref, g_ref, act_ref, dh2_ref, loss_ref, gnf_ref, lacc):
        i = pl.program_id(0)

        @pl.when(i == 0)
        def _():
            lacc[...] = jnp.zeros_like(lacc)
            gnf_ref[...] = jnp.zeros_like(gnf_ref)

        ru = jnp.maximum(u_ref[...], 0.0)
        act = (ru * ru).astype(BF16)
        act_ref[...] = act
        h2 = h1_ref[...] + _dot(act, w_ref[...])
        rs = _rms_scale(h2)
        gv = g_ref[...]
        err = h2 * rs * gv - tg_ref[...]
        lacc[...] += jnp.sum(err * err, axis=0, keepdims=True)
        dx, dgr = _rms_bwd(h2, rs, gv, err * (1.0 / D))
        dh2_ref[...] = dx
        gnf_ref[...] += jnp.sum(dgr, axis=0, keepdims=True)

        @pl.when(i == pl.num_programs(0) - 1)
        def _():
            loss_ref[...] = jnp.broadcast_to(jnp.sum(lacc[...], axis=1, keepdims=True) * (0.5 / D), (8, 128))

    return _pcall(
        body, "mlp_down_loss", (t // tm,),
        [_rows(tm, D_FF), _rows(tm, D), _rows(tm, D), _full(D_FF, D), _full(1, D)],
        [_rows(tm, D_FF), _rows(tm, D), _full(8, 128), _full(1, D)],
        [_sds((t, D_FF), BF16), _sds((t, D)), _sds((8, 128)), _sds((1, D))],
        scratch=[pltpu.VMEM((1, D), F32)],
    )(u, h1, target, w_down, g)


def _matmul_tn(a, g, name, chunked=None, comm=None):
    t, kdim = a.shape
    ndim = g.shape[1]
    tk = min(kdim, 1024)
    tn = ndim if ndim <= 1024 else 1024
    if chunked == "cols":
        assert tk == kdim and tn == ndim // N_CHIPS
    elif chunked == "rows":
        assert tk == kdim // N_CHIPS and tn == ndim
    tt = min(t, 512)
    nt = t // tt

    def body(a_ref, g_ref, o_ref):
        @pl.when(pl.program_id(2) == 0)
        def _():
            o_ref[...] = jnp.zeros_like(o_ref)

        o_ref[...] += _dot_tn(a_ref[...].astype(BF16), g_ref[...].astype(BF16)).reshape(o_ref.shape)

    if chunked is not None:
        out_spec = pl.BlockSpec((2, None, tk // 2, tn), lambda i, j, s: (0, i + j, 0, 0))
        out_shape = _sds((2, N_CHIPS, tk // 2, tn))
    else:
        out_spec, out_shape = pl.BlockSpec((tk, tn), lambda i, j, s: (i, j)), _sds((kdim, ndim))
    return _pcall(
        body, name, (kdim // tk, ndim // tn, nt),
        [pl.BlockSpec((tt, tk), lambda i, j, s: (s, i)), pl.BlockSpec((tt, tn), lambda i, j, s: (s, j))],
        out_spec, out_shape, comm=comm,
    )(a, g)


def _mlp_bwd_act(dh2, u, w_down, tm):
    t = u.shape[0]

    def body(d_ref, u_ref, w_ref, du_ref):
        db = d_ref[...].astype(BF16)
        for c0 in range(0, D_FF, 512):
            da = _dot_nt(db, w_ref[c0:c0 + 512, :])
            du_ref[:, c0:c0 + 512] = (da * (2.0 * jnp.maximum(u_ref[:, c0:c0 + 512], 0.0))).astype(BF16)

    return _pcall(
        body, "mlp_bwd_act", (t // tm,),
        [_rows(tm, D), _rows(tm, D_FF), _full(D_FF, D)],
        _rows(tm, D_FF), _sds((t, D_FF), BF16),
    )(dh2, u, w_down)


def _mlp_bwd_in(du, dh2, h1, w_up, g, tm):
    t = du.shape[0]

    def body(du_ref, d_ref, h_ref, w_ref, g_ref, dh1_ref, gacc_ref):
        @pl.when(pl.program_id(0) == 0)
        def _():
            gacc_ref[...] = jnp.zeros_like(gacc_ref)

        dn2 = _dot_nt(du_ref[...], w_ref[...])
        hv = h_ref[...]
        dx, dgr = _rms_bwd(hv, _rms_scale(hv), g_ref[...], dn2)
        dh1_ref[...] = d_ref[...] + dx
        gacc_ref[...] += jnp.sum(dgr, axis=0, keepdims=True)

    return _pcall(
        body, "mlp_bwd_in", (t // tm,),
        [_rows(tm, D_FF), _rows(tm, D), _rows(tm, D), _full(D, D_FF), _full(1, D)],
        [_rows(tm, D), _full(1, D)],
        [_sds((t, D)), _sds((1, D))],
    )(du, dh2, h1, w_up, g)


def _merge_bwd(dh1, w_out, g3, h, yb, merged, tm):
    t = dh1.shape[0]

    def body(d_ref, w_ref, g3_ref, h_ref, yb_ref, mg_ref, dg3_ref, dyb_ref, dl_ref, dh_ref, dwo_ref):
        @pl.when(pl.program_id(0) == 0)
        def _():
            dwo_ref[...] = jnp.zeros_like(dwo_ref)

        db = d_ref[...].astype(BF16)
        dwo_ref[...] += _dot_tn(mg_ref[...], db)
        dm = _dot_nt(db, w_ref[...])
        gv = g3_ref[:, 0:D]
        sa = _sigmoid(g3_ref[:, D:2 * D])
        sb = _sigmoid(g3_ref[:, 2 * D:3 * D])
        gel = _gelu(gv)
        hv = h_ref[...]
        ybv = yb_ref[...]
        dya = dm * sa
        dyb = dm * sb
        dg3_ref[:, 0:D] = (dya * hv * _gelu_grad(gv)).astype(BF16)
        dg3_ref[:, D:2 * D] = (dya * (hv * gel) * (1.0 - sa)).astype(BF16)
        dg3_ref[:, 2 * D:3 * D] = (dyb * ybv * (1.0 - sb)).astype(BF16)
        dh_ref[...] = dya * gel
        dyb_ref[...] = dyb.astype(BF16)
        prod = dyb * ybv
        ones = jnp.ones((8, V_HEAD), F32)
        for hh in range(N_HEADS):
            dl_ref[hh] = lax.dot_general(ones, prod[:, hh * V_HEAD:(hh + 1) * V_HEAD], (((1,), (1,)), ((), ())),
                                         precision=lax.Precision.HIGHEST, preferred_element_type=F32)

    return _pcall(
        body, "merge_bwd", (t // tm,),
        [_rows(tm, D), _full(D, D), _rows(tm, 3 * D), _rows(tm, D), _rows(tm, D), _rows(tm, D)],
        [_rows(tm, 3 * D), _rows(tm, D), pl.BlockSpec((N_HEADS, 8, tm), lambda i: (0, 0, i)), _rows(tm, D), _full(D, D)],
        [_sds((t, 3 * D), BF16), _sds((t, D), BF16), _sds((N_HEADS, 8, t)), _sds((t, D)), _sds((D, D))],
    )(dh1, w_out, g3, h, yb, merged)


def _flash_bwd(q, k, v, do, lse, delta, tq, comm=None):
    t = q.shape[1]
    nq = t // tq

    def body(q_ref, k_ref, v_ref, do_ref, lse_ref, dl_ref, dqt_ref, dk_ref, dv_ref):
        ki = pl.program_id(1)

        @pl.when(ki == 0)
        def _():
            dqt_ref[...] = jnp.zeros_like(dqt_ref)

        kblk, vblk = k_ref[0], v_ref[0]
        kt = jnp.transpose(kblk)
        dk_ref[...] = jnp.zeros_like(dk_ref)
        dv_ref[...] = jnp.zeros_like(dv_ref)

        def block(qi, diagonal):
            rows = pl.ds(pl.multiple_of(qi * tq, tq), tq)
            qv, dov = q_ref[0, rows, :], do_ref[rows, :]
            p = jnp.exp2(_dot_nt(kblk, qv) - lse_ref[0, qi, 0:1, :])
            if diagonal:
                p = jnp.where(_lane_iota((tq, tq)) >= _row_iota((tq, tq)), p, 0.0)
            dv_ref[0] += _dot(p.astype(BF16), dov)
            dp = _dot_nt(vblk, dov)
            ds = (p * (dp - dl_ref[0, qi, 0:1, :]) * math.log(2.0)).astype(BF16)
            dk_ref[0] += _dot(ds, qv)
            dqt_ref[0, qi] += _dot(kt, ds)

        block(ki, True)

        def two(i, carry):
            block(ki + 1 + 2 * i, False)
            block(ki + 2 + 2 * i, False)
            return carry

        def one(qi, carry):
            block(qi, False)
            return carry

        pairs = (nq - 1 - ki) // 2
        lax.fori_loop(0, pairs, two, 0)
        lax.fori_loop(ki + 1 + 2 * pairs, nq, one, 0)

    kv_spec = lambda w: pl.BlockSpec((1, tq, w), lambda h, ki: (h, ki, 0))
    stat = pl.BlockSpec((1, nq, 8, tq), lambda h, ki: (h, 0, 0, 0))
    return _pcall(
        body, "flash_bwd", (N_HEADS, nq),
        [pl.BlockSpec((1, t, HEAD_W), lambda h, ki: (h, 0, 0)), kv_spec(HEAD_W), kv_spec(V_HEAD),
         pl.BlockSpec((t, V_HEAD), lambda h, ki: (0, h)), stat, stat],
        [pl.BlockSpec((1, nq, HEAD_W, tq), lambda h, ki: (h, 0, 0, 0)), kv_spec(HEAD_W), kv_spec(V_HEAD)],
        [_sds((N_HEADS, nq, HEAD_W, tq)), _sds((N_HEADS, t, HEAD_W)), _sds((N_HEADS, t, V_HEAD))],
        comm=comm,
    )(q, k, v, do, lse, delta)


def _mla_bwd(dqt, dk, dv, cqn, ckvn, cq, ckv, rope_c, wq, wkv, qn, kvn, tm):
    t = cq.shape[0]

    def body(dq_ref, dk_ref, dv_ref, cqn_ref, ckvn_ref, cq_ref, ckv_ref, c_ref, wq_ref, wkv_ref, qn_ref, kvn_ref,
             dmla_ref, dwq_ref, dwkv_ref, dqn_ref, dkvn_ref):
        @pl.when(pl.program_id(0) == 0)
        def _():
            dwq_ref[...] = jnp.zeros_like(dwq_ref)
            dwkv_ref[...] = jnp.zeros_like(dwkv_ref)
            dqn_ref[...] = jnp.zeros_like(dqn_ref)
            dkvn_ref[...] = jnp.zeros_like(dkvn_ref)

        c = c_ref[...]
        lane = _lane_iota((tm, KR_W))
        cqn, ckvn = cqn_ref[...], ckvn_ref[...]
        dcqn = jnp.zeros((tm, Q_LORA), F32)
        dckvn = jnp.zeros((tm, KV_LORA), F32)
        dkr = jnp.zeros((tm, KR_W), F32)
        for h in range(N_HEADS):
            sl = slice(h * HEAD_W, (h + 1) * HEAD_W)
            dqh = jnp.transpose(dq_ref[h, 0]) * EXP2_SCALE
            droped = jnp.where(lane < 64, dqh[:, 128:256], 0.0)
            dqp = jnp.concatenate([dqh[:, 0:128], _rope_pair(droped) * c], axis=1).astype(BF16)
            dcqn = dcqn + _dot_nt(dqp, wq_ref[:, sl])
            dwq_ref[:, sl] += _dot_tn(cqn, dqp)
            dkr = dkr + jnp.where(lane < 64, dk_ref[h, :, 128:256], 0.0)
            dkvp = jnp.concatenate([dk_ref[h, :, 0:128], dv_ref[h]], axis=1).astype(BF16)
            dckvn = dckvn + _dot_nt(dkvp, wkv_ref[:, sl])
            dwkv_ref[:, sl] += _dot_tn(ckvn, dkvp)
        cqv, ckvv = cq_ref[...], ckv_ref[...]
        dcq, dgq = _rms_bwd(cqv, _rms_scale(cqv), qn_ref[...], dcqn)
        dckv, dgkv = _rms_bwd(ckvv, _rms_scale(ckvv), kvn_ref[...], dckvn)
        dqn_ref[...] += jnp.sum(dgq, axis=0, keepdims=True)
        dkvn_ref[...] += jnp.sum(dgkv, axis=0, keepdims=True)
        dmla_ref[:, 0:256] = dcq.astype(BF16)
        dmla_ref[:, 256:512] = dckv.astype(BF16)
        dmla_ref[:, 512:640] = (_rope_pair(dkr) * c).astype(BF16)

    hb = lambda w: pl.BlockSpec((N_HEADS, tm, w), lambda i: (0, i, 0))
    wide = N_HEADS * HEAD_W
    per_q = dqt.shape[3] // tm
    dq_spec = pl.BlockSpec((N_HEADS, 1, HEAD_W, tm), lambda i: (0, i // per_q, 0, i % per_q))
    return _pcall(
        body, "mla_bwd", (t // tm,),
        [dq_spec, hb(HEAD_W), hb(V_HEAD), _rows(tm, Q_LORA), _rows(tm, KV_LORA), _rows(tm, Q_LORA), _rows(tm, KV_LORA),
         _rows(tm, KR_W), _full(Q_LORA, wide), _full(KV_LORA, wide), _full(1, Q_LORA), _full(1, KV_LORA)],
        [_rows(tm, 640), _full(Q_LORA, wide), _full(KV_LORA, wide), _full(1, Q_LORA), _full(1, KV_LORA)],
        [_sds((t, 640), BF16), _sds((Q_LORA, wide)), _sds((KV_LORA, wide)), _sds((1, Q_LORA)), _sds((1, KV_LORA))],
    )(dqt, dk, dv, cqn, ckvn, cq, ckv, rope_c, wq, wkv, qn, kvn)


def _lru_bwd(dh, xa, h, rx, conv_w, wa, ba, wx, bx, lam, tb, comm=None):
    t = dh.shape[0]
    nb = t // tb

    def body(dh_ref, xa_ref, h_ref, hp_ref, x_ref, cw_ref, wa_ref, ba_ref, wx_ref, bx_ref, lam_ref,
             drx_ref, dcw_ref, dcb_ref, dwa_ref, dba_ref, dwx_ref, dbx_ref, dlam_ref, gc, dxn, tmp, pre_r, pre_i):
        step = pl.program_id(0)
        first_block = step == nb - 1

        @pl.when(step == 0)
        def _():
            gc[...] = jnp.zeros_like(gc)
            dxn[...] = jnp.zeros_like(dxn)
            for ref in (dcw_ref, dcb_ref, dwa_ref, dba_ref, dwx_ref, dbx_ref, dlam_ref):
                ref[...] = jnp.zeros_like(ref)

        xa = xa_ref[...]
        r, gi = _lru_gates(xa, wa_ref, ba_ref[...], wx_ref, bx_ref[...], pre_r, pre_i)
        lamv = lam_ref[...]
        sp = _softplus(-lamv)
        la = (-LRU_C * sp) * r
        a = jnp.exp(la)
        e2 = _expm1(2.0 * la)
        sq = jnp.sqrt(-e2)
        row = _row_iota((tb, D))
        cf = jnp.where(row == tb - 1, 1.0, pltpu.roll(a, tb - 1, 0))
        c3, b3 = _scan_groups_bwd(cf.reshape(tb // 8, 8, D), dh_ref[...].reshape(tb // 8, 8, D))
        carry = gc[...]
        for grp in reversed(range(tb // 8)):
            dg = b3[grp] + c3[grp] * carry
            pre_r[8 * grp:8 * grp + 8, :] = dg
            carry = dg[0:1, :]
        delta = pre_r[...]
        gc[...] = a[0:1, :] * carry
        hv = h_ref[...]
        hr = pltpu.roll(hv, 1, 0)
        tmp[...] = hr
        tmp[0:1, :] = jnp.where(first_block, 0.0, hp_ref[7:8, :])
        hprev = tmp[...]
        ix = gi * xa
        dla = (delta * hprev) * a - (delta * ix) * ((e2 + 1.0) / sq)
        dlam_ref[...] += jnp.sum(dla * r, axis=0, keepdims=True) * (LRU_C * _sigmoid(-lamv))
        dpr = (dla * (-LRU_C * sp)) * r * (1.0 - r)
        dsq = delta * sq
        dpi = (dsq * xa) * gi * (1.0 - gi)
        dba_ref[...] += jnp.sum(dpr, axis=0, keepdims=True)
        dbx_ref[...] += jnp.sum(dpi, axis=0, keepdims=True)
        pre_r[...] = dpr
        pre_i[...] = dpi
        xb = xa.astype(BF16)
        for n in range(RNN_BLOCKS):
            sl = slice(n * RNN_BW, (n + 1) * RNN_BW)
            dprn = pre_r[:, sl].astype(BF16)
            dpin = pre_i[:, sl].astype(BF16)
            dwa_ref[n] += _dot_tn(xb[:, sl], dprn)
            dwx_ref[n] += _dot_tn(xb[:, sl], dpin)
            tmp[:, sl] = _dot_nt(dprn, wa_ref[n]) + _dot_nt(dpin, wx_ref[n])
        dxa = dsq * gi + tmp[...]
        dcb_ref[...] += jnp.sum(dxa, axis=0, keepdims=True)
        xv = x_ref[...]
        drx = cw_ref[3:4, :] * dxa
        dcw_ref[3:4, :] += jnp.sum(dxa * xv, axis=0, keepdims=True)
        row8 = _row_iota((8, D))
        nxt = dxn[...]
        for s in (1, 2, 3):
            dr_ = pltpu.roll(dxa, tb - s, 0)
            tmp[...] = dr_
            tmp[tb - 8:tb, :] = jnp.where(row8 >= 8 - s, pltpu.roll(nxt, 8 - s, 0), dr_[tb - 8:tb, :])
            dxs = tmp[...]
            drx = drx + cw_ref[3 - s:4 - s, :] * dxs
            dcw_ref[3 - s:4 - s, :] += jnp.sum(dxs * xv, axis=0, keepdims=True)
        drx_ref[...] = drx.astype(BF16)
        dxn[...] = dxa[0:8, :]

    rev = pl.BlockSpec((tb, D), lambda i: (nb - 1 - i, 0))
    prev8 = pl.BlockSpec((8, D), lambda i: (jnp.maximum((nb - 1 - i) * (tb // 8) - 1, 0), 0))
    wblk = _full(RNN_BLOCKS, RNN_BW, RNN_BW)
    return _pcall(
        body, "lru_bwd", (nb,),
        [rev, rev, rev, prev8, rev, _full(4, D), wblk, _full(1, D), wblk, _full(1, D), _full(1, D)],
        [rev, _full(4, D), _full(1, D), wblk, _full(1, D), wblk, _full(1, D), _full(1, D)],
        [_sds((t, D), BF16), _sds((4, D)), _sds((1, D)), _sds((RNN_BLOCKS, RNN_BW, RNN_BW)), _sds((1, D)),
         _sds((RNN_BLOCKS, RNN_BW, RNN_BW)), _sds((1, D)), _sds((1, D))],
        scratch=[pltpu.VMEM((1, D), F32), pltpu.VMEM((8, D), F32), pltpu.VMEM((tb, D), F32), pltpu.VMEM((tb, D), F32),
                 pltpu.VMEM((tb, D), F32)],
        comm=comm,
    )(dh, xa, h, h, rx, conv_w, wa, ba, wx, bx, lam)


def _inproj_bwd(x, dh1, drx, dg3, dmla, w1, g, tm, comm=None):
    t = x.shape[0]

    def body(x_ref, d_ref, drx_ref, dg3_ref, dmla_ref, w_ref, g_ref, dx_ref, gacc_ref):
        @pl.when(pl.program_id(0) == 0)
        def _():
            gacc_ref[...] = jnp.zeros_like(gacc_ref)

        dxn = _dot_nt(drx_ref[...], w_ref[:, 0:D])
        for c0 in range(0, 3 * D, D):
            dxn = dxn + _dot_nt(dg3_ref[:, c0:c0 + D], w_ref[:, D + c0:2 * D + c0])
        dxn = dxn + _dot_nt(dmla_ref[...], w_ref[:, 4 * D:W1_COLS])
        xv = x_ref[...]
        dx, dgr = _rms_bwd(xv, _rms_scale(xv), g_ref[...], dxn)
        dx_ref[...] = d_ref[...] + dx
        gacc_ref[...] += jnp.sum(dgr, axis=0, keepdims=True)

    return _pcall(
        body, "inproj_bwd", (t // tm,),
        [_rows(tm, D), _rows(tm, D), _rows(tm, D), _rows(tm, 3 * D), _rows(tm, 640), _full(D, W1_COLS), _full(1, D)],
        [_rows(tm, D), _full(1, D)],
        [_sds((t, D)), _sds((1, D))],
        comm=comm,
    )(x, dh1, drx, dg3, dmla, w1, g)


def _pcall_indexed(body, name, index, grid, in_specs, out_specs, out_shape):
    call = pl.pallas_call(
        body, name=name, out_shape=out_shape,
        grid_spec=pltpu.PrefetchScalarGridSpec(num_scalar_prefetch=1, grid=grid, in_specs=in_specs, out_specs=out_specs),
        compiler_params=pltpu.CompilerParams(dimension_semantics=("arbitrary",) * len(grid), vmem_limit_bytes=V7X_VMEM_LIMIT))
    return lambda *operands: call(index, *operands)


def _pair_sum(halves, theirs, core, out_dtype, name):
    _, rows, cols = halves.shape
    tm = _row_tile(rows)

    def body(c_ref, a_ref, b_ref, o_ref):
        o_ref[...] = (a_ref[0] + b_ref[...]).astype(out_dtype)

    plain = pl.BlockSpec((tm, cols), lambda i, c: (i, 0))
    return _pcall_indexed(body, name, core, (rows // tm,),
                          [pl.BlockSpec((1, tm, cols), lambda i, c: (c[0], i, 0)), plain], plain,
                          _sds((rows, cols), out_dtype))(halves, theirs)


def _chip_sum(parts, recv, chip, name):
    _, rows, cols = parts.shape
    tm = _row_tile(rows)

    def body(c_ref, a_ref, r_ref, o_ref):
        o_ref[...] = ((a_ref[0].astype(F32) + r_ref[0].astype(F32)) + r_ref[1].astype(F32)) + r_ref[2].astype(F32)

    return _pcall_indexed(body, name, chip, (rows // tm,),
                          [pl.BlockSpec((1, tm, cols), lambda i, c: (c[0], i, 0)),
                           pl.BlockSpec((N_CHIPS - 1, tm, cols), lambda i, c: (0, i, 0))],
                          pl.BlockSpec((tm, cols), lambda i, c: (i, 0)), _sds((rows, cols)))(parts, recv)


def _adam_math(w, gv, m, v):
    mn = ADAM_B1 * m + (1.0 - ADAM_B1) * gv
    vn = ADAM_B2 * v + (1.0 - ADAM_B2) * (gv * gv)
    m_hat = mn / (1.0 - ADAM_B1 ** ADAM_STEP)
    v_hat = vn / (1.0 - ADAM_B2 ** ADAM_STEP)
    return -ADAM_LR * (m_hat / (jnp.sqrt(v_hat) + ADAM_EPS) + ADAM_WD * w), mn, vn


def _adamw(w, g, m, v, name):
    rows, cols = w.shape
    tm = _row_tile(rows)

    def body(w_ref, g_ref, m_ref, v_ref, d_ref, mo_ref, vo_ref):
        d_ref[...], mo_ref[...], vo_ref[...] = _adam_math(w_ref[...], g_ref[...], m_ref[...], v_ref[...])

    spec = _rows(tm, cols)
    return _pcall(body, name, (rows // tm,), [spec] * 4, [spec] * 3, [_sds((rows, cols))] * 3)(w, g, m, v)


def _adamw_halves(w, mine, theirs, m, v, core, name, by_cols=False):
    rows, cols = w.shape
    if by_cols:
        tm, tc = mine.shape[0] // 2, cols // 2
        grid = (rows // tm, 2)
        full = pl.BlockSpec((tm, tc), lambda i, j, c: (i, j))
        half = pl.BlockSpec((tm, tc), lambda i, j, c: (i, 0))
    else:
        tm = _row_tile(rows // 2)
        nh = rows // 2 // tm
        grid = (rows // tm,)
        full = pl.BlockSpec((tm, cols), lambda i, c: (i, 0))
        half = pl.BlockSpec((tm, cols), lambda i, c: (i % nh, 0))

    def body(c_ref, w_ref, a_ref, b_ref, m_ref, v_ref, g_ref, d_ref, mo_ref, vo_ref):
        which = pl.program_id(1) if by_cols else pl.program_id(0) // nh
        gv = jnp.where(which == c_ref[0], a_ref[...], b_ref[...])
        g_ref[...] = gv
        d_ref[...], mo_ref[...], vo_ref[...] = _adam_math(w_ref[...], gv, m_ref[...], v_ref[...])

    return _pcall_indexed(body, name, core, grid, [full, half, half, full, full], [full] * 4,
                          [_sds((rows, cols))] * 4)(w, mine, theirs, m, v)


REL_SIBLING = (0, 0, 1)
REL_CHIPS = ((1, 0, 0), (0, 1, 0), (1, 1, 0))


V7X_DMA_CHUNK_BYTES = 1 << 20


def _split_copy(src, dst, shape, itemsize):
    nbytes = math.prod(shape) * itemsize
    if nbytes <= V7X_DMA_CHUNK_BYTES or len(shape) < 2:
        return [(src, dst)]
    if len(shape) > 2:
        out = []
        for k in range(shape[0]):
            out += _split_copy(src.at[k], dst.at[k], shape[1:], itemsize)
        return out
    rows = shape[0]
    sub = 8 * (4 // itemsize)
    parts = max(1, min(-(-nbytes // V7X_DMA_CHUNK_BYTES), rows // sub))
    while rows % parts or (rows // parts) % sub:
        parts -= 1
    step = rows // parts
    return [(src.at[pl.ds(k * step, step)], dst.at[pl.ds(k * step, step)]) for k in range(parts)]


def _mesh_pos():
    return (lax.axis_index("x"), lax.axis_index("y"), lax.axis_index("c"))


def _make_copy(i, op, sems, pos, src=None, dst=None):
    rel = op[0]
    src, dst = (op[1], op[2]) if src is None else (src, dst)
    send_sems, recv_sems = sems
    if rel is None:
        return pltpu.make_async_copy(src, dst, send_sems.at[i])
    peer = tuple((p + r) % 2 for p, r in zip(pos, rel))
    return pltpu.make_async_remote_copy(src_ref=src, dst_ref=dst, send_sem=send_sems.at[i], recv_sem=recv_sems.at[i],
                                        device_id=peer, device_id_type=MESH_ID)


def _start_copies(ops, sems, pos, base=0):
    for i, op in enumerate(ops):
        for s_piece, d_piece in _split_copy(op[1], op[2], op[1].shape, jnp.dtype(op[1].dtype).itemsize):
            _make_copy(base + i, op, sems, pos, s_piece, d_piece).start()


def _wait_copies(ops, sems, pos, base=0):
    for i, op in enumerate(ops):
        _make_copy(base + i, op, sems, pos).wait()


def _comm(name, ins, out_shapes, n_ops, ops_fn):
    n_in, n_out = len(ins), len(out_shapes)

    def body(*refs):
        in_refs, out_refs = refs[:n_in], refs[n_in:n_in + n_out]
        sems = refs[n_in + n_out:]
        pos = _mesh_pos()
        ops = ops_fn(in_refs, out_refs, pos)
        assert len(ops) == n_ops
        _start_copies(ops, sems, pos)
        _wait_copies(ops, sems, pos)

    hbm = pl.BlockSpec(memory_space=pl.ANY)
    return pl.pallas_call(
        body, name=name, in_specs=[hbm] * n_in, out_specs=[hbm] * n_out, out_shape=list(out_shapes),
        scratch_shapes=[pltpu.SemaphoreType.DMA((n_ops,)), pltpu.SemaphoreType.DMA((n_ops,))],
    )(*ins)


def _chip_of(pos, rel=(0, 0, 0)):
    return 2 * ((pos[0] + rel[0]) % 2) + (pos[1] + rel[1]) % 2


def _gather_chips(shards, chip, name):
    def ops_fn(in_refs, out_refs, pos):
        me = _chip_of(pos)
        return [(rel, src, dst.at[me]) for src, dst in zip(in_refs, out_refs) for rel in REL_CHIPS]

    outs = _comm(name, shards, [_sds((N_CHIPS,) + s.shape, s.dtype) for s in shards], 3 * len(shards), ops_fn)
    return [lax.dynamic_update_index_in_dim(o, s, chip, 0) for o, s in zip(outs, shards)]


def _halved_gather_ops(pos, srcs, dsts, whole):
    me, c = _chip_of(pos), pos[2]
    ici, d2d = [], []
    for a, (src, dst) in enumerate(zip(srcs, dsts)):
        for rel in REL_CHIPS:
            if a in whole:
                ici.append((rel, src, dst.at[me]))
            else:
                ici.append((rel, src.at[c], dst.at[me, c]))
                arrived = dst.at[_chip_of(pos, rel), c]
                d2d.append((REL_SIBLING, arrived, arrived))
    return ici, d2d


def _gather_halved(shards, whole, chip, name):
    srcs = [s if a in whole else s.reshape(2, s.shape[0] // 2, s.shape[1]) for a, s in enumerate(shards)]
    n_sh = len(shards)
    n_ici, n_d2d = 3 * n_sh, 3 * (n_sh - len(whole))

    def body(*refs):
        in_refs, out_refs, sems = refs[:n_sh], refs[n_sh:2 * n_sh], refs[2 * n_sh:]
        pos = _mesh_pos()
        ici, d2d = _halved_gather_ops(pos, in_refs, out_refs, whole)
        _start_copies(ici, sems, pos)
        _wait_copies(ici, sems, pos)
        _start_copies(d2d, sems, pos, base=n_ici)
        _wait_copies(d2d, sems, pos, base=n_ici)

    hbm = pl.BlockSpec(memory_space=pl.ANY)
    outs = pl.pallas_call(
        body, name=name, in_specs=[hbm] * n_sh, out_specs=[hbm] * n_sh,
        out_shape=[_sds((N_CHIPS,) + s.shape, s.dtype) for s in srcs],
        scratch_shapes=[pltpu.SemaphoreType.DMA((n_ici + n_d2d,)), pltpu.SemaphoreType.DMA((n_ici + n_d2d,))],
    )(*srcs)
    return [lax.dynamic_update_index_in_dim(o, s, chip, 0).reshape((N_CHIPS,) + sh.shape)
            for o, s, sh in zip(outs, srcs, shards)]


def _split_comm(gs):
    def ops_fn(in_refs, out_refs, pos):
        return [(REL_SIBLING, src.at[1 - pos[2]], dst) for src, dst in zip(in_refs, out_refs)]

    return gs, [_sds(g.shape[1:], g.dtype) for g in gs], len(gs), ops_fn


def _exchange_comm(ps):
    def ops_fn(in_refs, out_refs, pos):
        return [(rel, src.at[_chip_of(pos, rel)], dst.at[j])
                for src, dst in zip(in_refs, out_refs) for j, rel in enumerate(REL_CHIPS)]

    return ps, [_sds((N_CHIPS - 1,) + p.shape[1:], p.dtype) for p in ps], 3 * len(ps), ops_fn


def _join_comm(hs):
    def ops_fn(in_refs, out_refs, pos):
        return [(REL_SIBLING, src, dst) for src, dst in zip(in_refs, out_refs)]

    return hs, [_sds(h.shape, h.dtype) for h in hs], len(hs), ops_fn


def _run_comm(name, comm):
    ins, shapes, n_ops, ops_fn = comm
    return _comm(name, ins, shapes, n_ops, ops_fn)


def _rot_cols(w):
    return jnp.concatenate([-w[..., 32:], w[..., :32]], axis=-1)


def _unrot_cols(dw):
    return jnp.concatenate([dw[..., 32:], -dw[..., :32]], axis=-1)


IN_OFFS = (0, 1024, 2048, 2304, 2560, 2624, 3648, 4672)


def _w1_from_w_in(w):
    seg = [w[:, IN_OFFS[i]:IN_OFFS[i + 1]] for i in range(7)]
    rnn_x, rnn_gate, cq, ckv, kr, ga, gb = seg
    return jnp.concatenate([rnn_x, rnn_gate, ga, gb, cq, ckv, kr, _rot_cols(kr)], axis=1)


def _w_in_grad_from_parts(d_rx, d_g3, d_mla):
    kr = d_mla[:, 512:576] + _unrot_cols(d_mla[:, 576:640])
    return jnp.concatenate([d_rx, d_g3[:, 0:D], d_mla[:, 0:512], kr, d_g3[:, D:3 * D]], axis=1)


def _wq_from_w_uq(w):
    w3 = w.reshape(Q_LORA, N_HEADS, QK_NOPE + QK_ROPE)
    rope = w3[..., QK_NOPE:]
    return jnp.concatenate([w3[..., :QK_NOPE], rope, _rot_cols(rope)], axis=-1).reshape(Q_LORA, N_HEADS * HEAD_W)


def _w_uq_grad_from_wq(dw):
    d3 = dw.reshape(Q_LORA, N_HEADS, HEAD_W)
    rope = d3[..., 128:192] + _unrot_cols(d3[..., 192:256])
    return jnp.concatenate([d3[..., :128], rope], axis=-1).reshape(Q_LORA, N_HEADS * (QK_NOPE + QK_ROPE))


def _cols_from_chunks(g):
    return g.transpose(1, 0, 2).reshape(g.shape[1], N_CHIPS * g.shape[2])


def _halves_of_col_chunks(dw):
    r, c4 = dw.shape
    return dw.reshape(2, r // 2, N_CHIPS, c4 // N_CHIPS).transpose(0, 2, 1, 3)


def _halves_of_row_chunks(dw):
    r4, c = dw.shape
    return dw.reshape(N_CHIPS, 2, r4 // (2 * N_CHIPS), c).transpose(1, 0, 2, 3)


def kernel(x, norm_mix, w_in, conv_w, conv_b, lru_wa, lru_ba, lru_wx, lru_bx, lru_lambda, q_norm, w_uq, kv_norm, w_ukv, w_out, norm_mlp, w_up, w_down, norm_final, loss_target, m_norm_mix, m_w_in, m_conv_w, m_conv_b, m_lru_wa, m_lru_ba, m_lru_wx, m_lru_bx, m_lru_lambda, m_q_norm, m_w_uq, m_kv_norm, m_w_ukv, m_w_out, m_norm_mlp, m_w_up, m_w_down, m_norm_final, v_norm_mix, v_w_in, v_conv_w, v_conv_b, v_lru_wa, v_lru_ba, v_lru_wx, v_lru_bx, v_lru_lambda, v_q_norm, v_w_uq, v_kv_norm, v_w_ukv, v_w_out, v_norm_mlp, v_w_up, v_w_down, v_norm_final):
    t = x.shape[1]
    tm = min(256, t)
    tq = min(512, max(tm, t // 4))
    x2 = x[0]
    target = loss_target[0]
    chip = 2 * lax.axis_index("x") + lax.axis_index("y")
    core = lax.axis_index("c")
    chip_ix, core_ix = chip.reshape(1).astype(jnp.int32), core.reshape(1).astype(jnp.int32)
    row = lambda p: p.reshape(1, -1)

    big_shards = (w_in, w_uq, w_ukv, w_out, w_up, w_down)
    w_in_g, conv_w_g = _gather_halved([w_in.astype(BF16), conv_w], (1,), chip, "weight_gather_first")
    w1 = _w1_from_w_in(_cols_from_chunks(w_in_g))
    conv_w_f = _cols_from_chunks(conv_w_g)
    wa_b, wx_b = lru_wa.astype(BF16), lru_wx.astype(BF16)

    pos = jnp.arange(t, dtype=F32)
    inv_freq = 1.0 / (ROPE_THETA ** (jnp.arange(0, QK_ROPE, 2, dtype=F32) / QK_ROPE))
    ang = pos[:, None] * inv_freq[None, :]
    rope_c = jnp.concatenate([jnp.cos(ang), jnp.cos(ang), jnp.sin(ang), jnp.sin(ang)], axis=-1)

    (xn, rx, g3, cq, ckv, kr), gathered = _inproj(x2, row(norm_mix), w1, tm, [w.astype(BF16) for w in big_shards[1:]], chip)
    wq = _wq_from_w_uq(_cols_from_chunks(gathered[0]))
    wkv = _cols_from_chunks(gathered[1])
    w_out_f = gathered[2].reshape(D, D)
    w_up_f = _cols_from_chunks(gathered[3])
    w_down_f = gathered[4].reshape(D_FF, D)
    h, xa = _lru_fwd(rx, conv_w_f, row(conv_b), wa_b, row(lru_ba), wx_b, row(lru_bx), row(lru_lambda), tm)
    q, k, v, cqn, ckvn = _mla_proj(cq, ckv, kr, row(q_norm), row(kv_norm), wq, wkv, rope_c, tm)
    nq = t // tq
    yb, lse = _flash_fwd(q, k, v, tq)
    h1, merged = _merge_out(x2, h, g3, yb, w_out_f, tm)
    u, n2 = _mlp_up(h1, row(norm_mlp), w_up_f, tm)
    act, dh2, loss_blk, g_norm_final = _mlp_down_loss(u, h1, target, w_down_f, row(norm_final), tm)

    g_w_down = _matmul_tn(act, dh2, "grad_w_down", "rows")
    du = _mlp_bwd_act(dh2, u, w_down_f, tm)
    dh1, g_norm_mlp = _mlp_bwd_in(du, dh2, h1, w_up_f, row(norm_mlp), tm)
    g_w_up = _matmul_tn(n2, du, "grad_w_up", "cols")
    dg3, dyb, delta, dh, g_w_out = _merge_bwd(dh1, w_out_f, g3, h, yb, merged, tm)
    delta = delta.reshape(N_HEADS, 8, nq, tq).swapaxes(1, 2)
    def pair_sums(hvs, theirs, dtypes, tag):
        return [_pair_sum(hv.reshape(2, -1, hv.shape[-1]), r.reshape(-1, r.shape[-1]), core_ix, dt, f"grad_pair_sum_{tag}{a}").reshape(r.shape)
                for a, (hv, r, dt) in enumerate(zip(hvs, theirs, dtypes))]

    def chip_sums(parts, received, tag):
        return [_chip_sum(p, r, chip_ix, f"grad_chip_sum_{tag}{a}") for a, (p, r) in enumerate(zip(parts, received))]

    early = [_halves_of_row_chunks(g_w_out), g_w_up, g_w_down]
    (dq, dk, dv), early_theirs = _flash_bwd(q, k, v, dyb, lse, delta, tq, comm=_split_comm(early))
    early_parts = pair_sums(early, early_theirs, [BF16] * 3, "early")
    dmla, g_wq, g_wkv, g_q_norm, g_kv_norm = _mla_bwd(dq, dk, dv, cqn, ckvn, cq, ckv, rope_c, wq, wkv, row(q_norm), row(kv_norm), tm)
    (drx, g_conv_w, g_conv_b, g_wa, g_ba, g_wx, g_bx, g_lam), early_received = _lru_bwd(
        dh, xa, h, rx, conv_w_f, wa_b, row(lru_ba), wx_b, row(lru_bx), row(lru_lambda), tm, comm=_exchange_comm(early_parts))
    early_reduced = chip_sums(early_parts, early_received, "early")
    grad_x, g_norm_mix = _inproj_bwd(x2, dh1, drx, dg3, dmla, w1, row(norm_mix), tm)
    g_w_in_gates, early_sibling = _matmul_tn(xn, dg3, "grad_w_in_gates", comm=_join_comm(early_reduced))
    g_w_in = _w_in_grad_from_parts(_matmul_tn(xn, drx, "grad_w_in_rx"), g_w_in_gates, _matmul_tn(xn, dmla, "grad_w_in_mla"))
    g_w_uq = _w_uq_grad_from_wq(g_wq)

    smalls = (g_norm_mix, g_conv_b, g_wa, g_ba, g_wx, g_bx, g_lam, g_q_norm, g_kv_norm, g_norm_mlp, g_norm_final, g_conv_w)
    s_flat = jnp.concatenate([s.reshape(-1) for s in smalls] + [loss_blk[0, 0:1], jnp.zeros((S_LEN - N_SMALL - CONVW_SIZE - 1,), F32)])
    late = [_halves_of_col_chunks(g_w_in), _halves_of_col_chunks(g_w_uq), _halves_of_col_chunks(g_wkv),
            s_flat.reshape(N_CHIPS, 2, S_ROWS_HALF, 128).transpose(1, 0, 2, 3)]
    late_theirs = _run_comm("grad_sibling_split", _split_comm(late))
    late_parts = pair_sums(late, late_theirs, [BF16] * 3 + [F32], "late")
    late_reduced = chip_sums(late_parts, _run_comm("grad_chip_exchange", _exchange_comm(late_parts)), "late")
    late_sibling = _run_comm("grad_sibling_join", _join_comm(late_reduced))
    reduced = late_reduced[:3] + early_reduced
    reduced_sibling = list(late_sibling[:3]) + list(early_sibling)
    s_mine, s_theirs = late_reduced[3], late_sibling[3]
    s_chunk = jnp.where(core == 0, jnp.concatenate([s_mine, s_theirs]), jnp.concatenate([s_theirs, s_mine]))
    s_all = _gather_chips([s_chunk], chip, "small_grad_gather")[0].reshape(-1)

    small_grads = []
    off = 0
    for shp, n in zip(SMALL_SHAPES, SMALL_SIZES):
        small_grads.append(s_all[off:off + n].reshape(shp))
        off += n
    g_conv_w_mine = lax.dynamic_slice_in_dim(s_all[off:off + CONVW_SIZE].reshape(4, D), chip * (D // N_CHIPS), D // N_CHIPS, axis=1)
    loss = s_all[off + CONVW_SIZE]

    big_m = (m_w_in, m_w_uq, m_w_ukv, m_w_out, m_w_up, m_w_down)
    big_v = (v_w_in, v_w_uq, v_w_ukv, v_w_out, v_w_up, v_w_down)
    big_names = ("w_in", "w_uq", "w_ukv", "w_out", "w_up", "w_down")
    big_upd = [_adamw_halves(w, gm, gt, m, v, core_ix, "adamw_" + n)
               for w, gm, gt, m, v, n in zip(big_shards[1:], reduced[1:], reduced_sibling[1:], big_m[1:], big_v[1:], big_names[1:])]
    w_in_upd = _adamw_halves(w_in.T, reduced[0].T, reduced_sibling[0].T, m_w_in.T, v_w_in.T, core_ix, "adamw_w_in", by_cols=True)
    big_upd = [[u.T for u in w_in_upd]] + big_upd

    small_w = (norm_mix, conv_b, lru_wa, lru_ba, lru_wx, lru_bx, lru_lambda, q_norm, kv_norm, norm_mlp, norm_final)
    small_m = (m_norm_mix, m_conv_b, m_lru_wa, m_lru_ba, m_lru_wx, m_lru_bx, m_lru_lambda, m_q_norm, m_kv_norm, m_norm_mlp, m_norm_final)
    small_v = (v_norm_mix, v_conv_b, v_lru_wa, v_lru_ba, v_lru_wx, v_lru_bx, v_lru_lambda, v_q_norm, v_kv_norm, v_norm_mlp, v_norm_final)

    def pack(items, last, fill):
        flat = jnp.concatenate([i.reshape(-1) for i in items] + [last.reshape(-1)])
        return jnp.concatenate([flat, jnp.full((PACK_ROWS * 128 - flat.shape[0],), fill, F32)]).reshape(PACK_ROWS, 128)

    packed = _adamw(pack(small_w, conv_w, 0.0), pack(small_grads, g_conv_w_mine, 0.0), pack(small_m, m_conv_w, 0.0),
                    pack(small_v, v_conv_w, 1.0), "adamw_small")

    def unpack(p):
        flat = p.reshape(-1)
        outs, o = [], 0
        for shp, n in zip(SMALL_SHAPES, SMALL_SIZES):
            outs.append(flat[o:o + n].reshape(shp))
            o += n
        return outs, flat[o:o + CONVW_SIZE // N_CHIPS].reshape(4, D // N_CHIPS)

    order = ("norm_mix", "w_in", "conv_w", "conv_b", "lru_wa", "lru_ba", "lru_wx", "lru_bx", "lru_lambda", "q_norm", "w_uq", "kv_norm",
             "w_ukv", "w_out", "norm_mlp", "w_up", "w_down", "norm_final")

    def assemble(small_list, conv_w_item, big_list):
        table = dict(zip(SMALL_NAMES, small_list))
        table["conv_w"] = conv_w_item
        table.update(zip(big_names, big_list))
        return [table[n] for n in order]

    outs = [loss, grad_x.reshape(1, t, D)]
    outs += assemble(small_grads, g_conv_w_mine, [b[0] for b in big_upd])
    for j in range(3):
        sm, cw = unpack(packed[j])
        outs += assemble(sm, cw, [b[j + 1] for b in big_upd])
    return tuple(outs)
```

```python
import functools
import math

import jax
import jax.numpy as jnp
import numpy as np
from jax import lax
from jax.experimental import pallas as pl
from jax.experimental.pallas import tpu as pltpu

F32 = jnp.float32
BF16 = jnp.bfloat16

D = 1024
N_HEADS = 8
QK_NOPE = 128
QK_ROPE = 64
V_HEAD = 128
Q_LORA = 256
KV_LORA = 256
D_FF = 4096
RNN_BLOCKS = 8
RNN_BW = 128
LRU_C = 8.0
EPS = 1e-6
ROPE_THETA = 10000.0
HEAD_W = 256
KR_W = 128
W1_COLS = 4 * D + Q_LORA + KV_LORA + KR_W
SM_SCALE = (QK_NOPE + QK_ROPE) ** -0.5
EXP2_SCALE = SM_SCALE * math.log2(math.e)
NEG = float(jnp.finfo(jnp.float32).min)

ADAM_LR = 0.001
ADAM_B1 = 0.9
ADAM_B2 = 0.999
ADAM_EPS = 1e-08
ADAM_WD = 0.01
ADAM_STEP = 10

N_CHIPS = 4
V7X_VMEM_LIMIT = 56 * 1024 * 1024
MESH_ID = pl.DeviceIdType.MESH

SMALL_NAMES = ("norm_mix", "conv_b", "lru_wa", "lru_ba", "lru_wx", "lru_bx", "lru_lambda", "q_norm", "kv_norm", "norm_mlp", "norm_final")
SMALL_SHAPES = ((D,), (D,), (RNN_BLOCKS, RNN_BW, RNN_BW), (RNN_BLOCKS, RNN_BW), (RNN_BLOCKS, RNN_BW, RNN_BW), (RNN_BLOCKS, RNN_BW), (D,),
                (Q_LORA,), (KV_LORA,), (D,), (D,))
SMALL_SIZES = tuple(math.prod(s) for s in SMALL_SHAPES)
N_SMALL = sum(SMALL_SIZES)
CONVW_SIZE = 4 * D
S_LEN = -(-(N_SMALL + CONVW_SIZE) // 8192) * 8192
S_ROWS_HALF = S_LEN // (N_CHIPS * 2 * 128)
PACK_ROWS = -(-(N_SMALL + CONVW_SIZE // N_CHIPS) // (256 * 128)) * 256


def _pcall(body, name, grid, in_specs, out_specs, out_shape, scratch=(), comm=None):
    params = pltpu.CompilerParams(dimension_semantics=("arbitrary",) * len(grid), vmem_limit_bytes=V7X_VMEM_LIMIT)
    if comm is None:
        return pl.pallas_call(body, name=name, grid=grid, in_specs=in_specs, out_specs=out_specs, out_shape=out_shape,
                              scratch_shapes=list(scratch), compiler_params=params)
    c_ins, c_shapes, n_ops, ops_fn = comm
    single = not isinstance(out_specs, (list, tuple))
    out_specs, out_shape = ([out_specs], [out_shape]) if single else (list(out_specs), list(out_shape))
    n_in, n_out, n_sc, n_ci, n_co = len(in_specs), len(out_specs), len(scratch), len(c_ins), len(c_shapes)

    def wrapped(*refs):
        ins, refs = refs[:n_in], refs[n_in:]
        c_in_refs, refs = refs[:n_ci], refs[n_ci:]
        outs, refs = refs[:n_out], refs[n_out:]
        c_out_refs, refs = refs[:n_co], refs[n_co:]
        own_scratch, sems = refs[:n_sc], refs[n_sc:]
        pos = _mesh_pos()
        ops = ops_fn(c_in_refs, c_out_refs, pos)
        assert len(ops) == n_ops
        first, last = True, True
        for d, n in enumerate(grid):
            first = first & (pl.program_id(d) == 0)
            last = last & (pl.program_id(d) == n - 1)

        @pl.when(first)
        def _():
            _start_copies(ops, sems, pos)

        body(*ins, *outs, *own_scratch)

        @pl.when(last)
        def _():
            _wait_copies(ops, sems, pos)

    hbm = pl.BlockSpec(memory_space=pl.ANY)
    call = pl.pallas_call(
        wrapped, name=name, grid=grid, in_specs=list(in_specs) + [hbm] * n_ci, out_specs=out_specs + [hbm] * n_co,
        out_shape=out_shape + list(c_shapes),
        scratch_shapes=list(scratch) + [pltpu.SemaphoreType.DMA((n_ops,)), pltpu.SemaphoreType.DMA((n_ops,))],
        compiler_params=params)

    def run(*operands):
        res = call(*operands, *c_ins)
        own = res[0] if single else res[:n_out]
        return own, res[n_out:]

    return run


def _rows(tm, w):
    return pl.BlockSpec((tm, w), lambda i: (i, 0))


def _full(*shape):
    return pl.BlockSpec(shape, lambda *_: (0,) * len(shape))


def _sds(shape, dtype=F32):
    return jax.ShapeDtypeStruct(shape, dtype)


def _row_tile(rows, cap=256):
    t = min(rows, cap)
    while rows % t or t % 8:
        t -= 1
    return t


def _dot(a, b):
    return jnp.dot(a, b, preferred_element_type=F32)


def _dot_nt(a, b):
    return lax.dot_general(a, b, (((1,), (1,)), ((), ())), preferred_element_type=F32)


def _dot_tn(a, b):
    return lax.dot_general(a, b, (((0,), (0,)), ((), ())), preferred_element_type=F32)


def _sigmoid(x):
    return 1.0 / (1.0 + jnp.exp(-x))


_GELU_C = math.sqrt(2.0 / math.pi)


def _gelu(x):
    return x * (0.5 * (1.0 + jnp.tanh(_GELU_C * (x + 0.044715 * (x * x * x)))))


def _gelu_grad(x):
    t = jnp.tanh(_GELU_C * (x + 0.044715 * (x * x * x)))
    cdf = 0.5 * (1.0 + t)
    return cdf + x * (0.5 * (1.0 - t * t) * _GELU_C * (1.0 + 3.0 * 0.044715 * (x * x)))


def _rms_scale(x):
    return lax.rsqrt(jnp.mean(x * x, axis=-1, keepdims=True) + EPS)


def _rms_bwd(x, rs, g, dy):
    gdy = dy * g
    dx = rs * gdy - x * ((rs * rs * rs) * jnp.mean(gdy * x, axis=-1, keepdims=True))
    return dx, dy * (x * rs)


def _log1p(e):
    u = 1.0 + e
    d = u - 1.0
    return jnp.where(d == 0.0, e, jnp.log(u) * (e / jnp.where(d == 0.0, 1.0, d)))


def _softplus(y):
    return jnp.maximum(y, 0.0) + _log1p(jnp.exp(-jnp.abs(y)))


def _expm1(x):
    u = jnp.exp(x)
    lu = jnp.log(u)
    safe = jnp.where((u == 1.0) | (u == 0.0), 1.0, lu)
    return jnp.where(u == 1.0, x, jnp.where(u == 0.0, -1.0, (u - 1.0) * (x / safe)))


def _row_iota(shape):
    return lax.broadcasted_iota(jnp.int32, shape, 0)


def _lane_iota(shape):
    return lax.broadcasted_iota(jnp.int32, shape, 1)


def _scan_groups_fwd(a, b):
    sub = lax.broadcasted_iota(jnp.int32, a.shape, 1)
    for sh in (1, 2, 4):
        m = sub >= sh
        b = jnp.where(m, a * pltpu.roll(b, sh, 1) + b, b)
        a = jnp.where(m, a * pltpu.roll(a, sh, 1), a)
    return a, b


def _scan_groups_bwd(c, b):
    sub = lax.broadcasted_iota(jnp.int32, c.shape, 1)
    for sh in (1, 2, 4):
        m = sub < 8 - sh
        b = jnp.where(m, b + c * pltpu.roll(b, 8 - sh, 1), b)
        c = jnp.where(m, c * pltpu.roll(c, 8 - sh, 1), c)
    return c, b


def _rope_pair(gc):
    return gc + pltpu.roll(gc, 64, 1)


def _inproj(x, g, w1, tm, shards, chip):
    t = x.shape[0]
    nsteps = t // tm
    widths = (D, 3 * D, Q_LORA, KV_LORA, KR_W)
    srcs = [s.reshape(2, s.shape[0] // 2, s.shape[1]) for s in shards]
    n_sh = len(shards)

    def body(x_ref, g_ref, w_ref, *refs):
        sh_refs = refs[:n_sh]
        xn_ref, rx_ref, g3_ref, cq_ref, ckv_ref, kr_ref = refs[n_sh:n_sh + 6]
        gat_refs = refs[n_sh + 6:2 * n_sh + 6]
        sems = refs[2 * n_sh + 6:]
        pos = _mesh_pos()
        ici, d2d = _halved_gather_ops(pos, sh_refs, gat_refs, ())

        @pl.when(pl.program_id(0) == 0)
        def _():
            _start_copies(ici, sems, pos)

        xv = x_ref[...]
        xn = (xv * _rms_scale(xv) * g_ref[...]).astype(BF16)
        xn_ref[...] = xn
        col = 0
        for ref, w in zip((rx_ref, g3_ref, cq_ref, ckv_ref, kr_ref), widths):
            for c0 in range(0, w, 512):
                cw = min(512, w - c0)
                ref[:, c0:c0 + cw] = _dot(xn, w_ref[:, col + c0:col + c0 + cw])
            col += w

        @pl.when(pl.program_id(0) == nsteps - 1)
        def _():
            _wait_copies(ici, sems, pos)
            _start_copies(d2d, sems, pos, base=len(ici))
            _wait_copies(d2d, sems, pos, base=len(ici))

    hbm = pl.BlockSpec(memory_space=pl.ANY)
    outs = pl.pallas_call(
        body, name="inproj", grid=(nsteps,),
        in_specs=[_rows(tm, D), _full(1, D), _full(D, W1_COLS)] + [hbm] * n_sh,
        out_specs=[_rows(tm, D)] + [_rows(tm, w) for w in widths] + [hbm] * n_sh,
        out_shape=[_sds((t, D), BF16)] + [_sds((t, w)) for w in widths] + [_sds((N_CHIPS,) + s.shape, s.dtype) for s in srcs],
        scratch_shapes=[pltpu.SemaphoreType.DMA((6 * n_sh,)), pltpu.SemaphoreType.DMA((6 * n_sh,))],
        compiler_params=pltpu.CompilerParams(dimension_semantics=("arbitrary",), vmem_limit_bytes=V7X_VMEM_LIMIT),
    )(x, g, w1, *srcs)
    gathered = [lax.dynamic_update_index_in_dim(o, s, chip, 0).reshape((N_CHIPS,) + sh.shape)
                for o, s, sh in zip(outs[6:], srcs, shards)]
    return outs[:6], gathered


def _lru_gates(xa, wa_ref, ba, wx_ref, bx, pre_r, pre_i):
    xb = xa.astype(BF16)
    for n in range(RNN_BLOCKS):
        sl = slice(n * RNN_BW, (n + 1) * RNN_BW)
        pre_r[:, sl] = _dot(xb[:, sl], wa_ref[n])
        pre_i[:, sl] = _dot(xb[:, sl], wx_ref[n])
    r = _sigmoid(pre_r[...] + ba)
    i = _sigmoid(pre_i[...] + bx)
    return r, i


def _lru_fwd(rx, conv_w, conv_b, wa, ba, wx, bx, lam, tb):
    t = rx.shape[0]
    nb = t // tb

    def body(x_ref, xp_ref, cw_ref, cb_ref, wa_ref, ba_ref, wx_ref, bx_ref, lam_ref, h_ref, xa_ref, hc, tmp, pre_r, pre_i):
        i_blk = pl.program_id(0)

        @pl.when(i_blk == 0)
        def _():
            hc[...] = jnp.zeros_like(hc)

        xv = x_ref[...]
        xp = jnp.where(i_blk > 0, xp_ref[...], 0.0)
        row8 = _row_iota((8, D))
        xa = cb_ref[...] + cw_ref[3:4, :] * xv
        for s in (1, 2, 3):
            xr = pltpu.roll(xv, s, 0)
            tmp[...] = xr
            tmp[0:8, :] = jnp.where(row8 < s, pltpu.roll(xp, s, 0), xr[0:8, :])
            xa = xa + cw_ref[3 - s:4 - s, :] * tmp[...]
        xa_ref[...] = xa
        r, gi = _lru_gates(xa, wa_ref, ba_ref[...], wx_ref, bx_ref[...], pre_r, pre_i)
        la = (-LRU_C * _softplus(-lam_ref[...])) * r
        a = jnp.exp(la)
        b = jnp.sqrt(-_expm1(2.0 * la)) * (gi * xa)
        a3, b3 = _scan_groups_fwd(a.reshape(tb // 8, 8, D), b.reshape(tb // 8, 8, D))
        carry = hc[...]
        for grp in range(tb // 8):
            hg = a3[grp] * carry + b3[grp]
            h_ref[8 * grp:8 * grp + 8, :] = hg
            carry = hg[7:8, :]
        hc[...] = carry

    prev8 = pl.BlockSpec((8, D), lambda i: (jnp.maximum(i * (tb // 8) - 1, 0), 0))
    return _pcall(
        body, "lru_fwd", (nb,),
        [_rows(tb, D), prev8, _full(4, D), _full(1, D), _full(RNN_BLOCKS, RNN_BW, RNN_BW), _full(1, D),
         _full(RNN_BLOCKS, RNN_BW, RNN_BW), _full(1, D), _full(1, D)],
        [_rows(tb, D), _rows(tb, D)],
        [_sds((t, D)), _sds((t, D))],
        scratch=[pltpu.VMEM((1, D), F32), pltpu.VMEM((tb, D), F32), pltpu.VMEM((tb, D), F32), pltpu.VMEM((tb, D), F32)],
    )(rx, rx, conv_w, conv_b, wa, ba, wx, bx, lam)


def _mla_proj(cq, ckv, kr, qn, kvn, wq, wkv, rope_c, tm):
    t = cq.shape[0]

    def body(cq_ref, ckv_ref, kr_ref, qn_ref, kvn_ref, wq_ref, wkv_ref, c_ref, q_ref, k_ref, v_ref, cqn_ref, ckvn_ref):
        cqv = cq_ref[...]
        cqn = (cqv * _rms_scale(cqv) * qn_ref[...]).astype(BF16)
        ckvv = ckv_ref[...]
        ckvn = (ckvv * _rms_scale(ckvv) * kvn_ref[...]).astype(BF16)
        cqn_ref[...] = cqn
        ckvn_ref[...] = ckvn
        c = c_ref[...]
        lane = _lane_iota((tm, KR_W))
        kro = jnp.where(lane < 64, _rope_pair(kr_ref[...] * c), 0.0).astype(BF16)
        for h in range(N_HEADS):
            sl = slice(h * HEAD_W, (h + 1) * HEAD_W)
            qh = _dot(cqn, wq_ref[:, sl])
            q_ref[h, :, 0:128] = (qh[:, 0:128] * EXP2_SCALE).astype(BF16)
            q_ref[h, :, 128:256] = (_rope_pair(qh[:, 128:256] * c) * EXP2_SCALE).astype(BF16)
            kvh = _dot(ckvn, wkv_ref[:, sl])
            k_ref[h, :, 0:128] = kvh[:, 0:128].astype(BF16)
            k_ref[h, :, 128:256] = kro
            v_ref[h, :, 0:V_HEAD] = kvh[:, 128:256].astype(BF16)
            v_ref[h, :, V_HEAD:2 * V_HEAD] = jnp.ones((tm, V_HEAD), BF16)

    hb = lambda w: pl.BlockSpec((N_HEADS, tm, w), lambda i: (0, i, 0))
    return _pcall(
        body, "mla_proj", (t // tm,),
        [_rows(tm, Q_LORA), _rows(tm, KV_LORA), _rows(tm, KR_W), _full(1, Q_LORA), _full(1, KV_LORA),
         _full(Q_LORA, N_HEADS * HEAD_W), _full(KV_LORA, N_HEADS * HEAD_W), _rows(tm, KR_W)],
        [hb(HEAD_W), hb(HEAD_W), hb(2 * V_HEAD), _rows(tm, Q_LORA), _rows(tm, KV_LORA)],
        [_sds((N_HEADS, t, HEAD_W), BF16), _sds((N_HEADS, t, HEAD_W), BF16), _sds((N_HEADS, t, 2 * V_HEAD), BF16),
         _sds((t, Q_LORA), BF16), _sds((t, KV_LORA), BF16)],
    )(cq, ckv, kr, qn, kvn, wq, wkv, rope_c)


def _flash_fwd(q, k, v, tq):
    t = q.shape[1]
    nq = t // tq

    def body(q_ref, k_ref, v_ref, o_ref, lse_ref, s_even, s_odd):
        qi = pl.program_id(1)
        qv = q_ref[0]

        def scores(ki, buf):
            buf[...] = _dot_nt(qv, k_ref[0, pl.ds(pl.multiple_of(ki * tq, tq), tq), :])

        def softmax_pv(ki, buf, carry, diagonal):
            m, acc = carry
            s = buf[...]
            if diagonal:
                s = jnp.where(_row_iota((tq, tq)) >= _lane_iota((tq, tq)), s, NEG)
            m_new = jnp.maximum(m, jnp.max(s, axis=1, keepdims=True))
            p = jnp.exp2(s - m_new)
            alpha = jnp.exp2(m - m_new)
            acc = alpha * acc + _dot(p.astype(BF16), v_ref[0, pl.ds(pl.multiple_of(ki * tq, tq), tq), :])
            return m_new, acc

        def finish(carry):
            m, acc = carry
            l = acc[:, V_HEAD:2 * V_HEAD]
            o_ref[...] = acc[:, 0:V_HEAD] / l
            lse = m + jnp.log(l) * math.log2(math.e)
            lse_ref[0, 0] = jnp.transpose(lse)[0:8, :]

        def two(i, carry):
            scores(2 * i + 1, s_odd)
            carry = softmax_pv(2 * i, s_even, carry, False)
            scores(2 * i + 2, s_even)
            return softmax_pv(2 * i + 1, s_odd, carry, False)

        init = (jnp.full((tq, 1), -jnp.inf, F32), jnp.zeros((tq, 2 * V_HEAD), F32))
        scores(0, s_even)
        carry = lax.fori_loop(0, qi // 2, two, init)

        @pl.when(qi % 2 == 0)
        def _():
            finish(softmax_pv(qi, s_even, carry, True))

        @pl.when(qi % 2 == 1)
        def _():
            scores(qi, s_odd)
            finish(softmax_pv(qi, s_odd, softmax_pv(qi - 1, s_even, carry, False), True))

    head = lambda w: pl.BlockSpec((1, t, w), lambda h, qi: (h, 0, 0))
    return _pcall(
        body, "flash_fwd", (N_HEADS, nq),
        [pl.BlockSpec((1, tq, HEAD_W), lambda h, qi: (h, qi, 0)), head(HEAD_W), head(2 * V_HEAD)],
        [pl.BlockSpec((tq, V_HEAD), lambda h, qi: (qi, h)), pl.BlockSpec((1, 1, 8, tq), lambda h, qi: (h, qi, 0, 0))],
        [_sds((t, D)), _sds((N_HEADS, nq, 8, tq))],
        scratch=[pltpu.VMEM((tq, tq), F32), pltpu.VMEM((tq, tq), F32)],
    )(q, k, v)


def _merge_out(x, h, g3, yb, w_out, tm):
    t = x.shape[0]

    def body(x_ref, h_ref, g3_ref, yb_ref, w_ref, h1_ref, mg_ref):
        ya = h_ref[...] * _gelu(g3_ref[:, 0:D])
        merged = (_sigmoid(g3_ref[:, D:2 * D]) * ya + _sigmoid(g3_ref[:, 2 * D:3 * D]) * yb_ref[...]).astype(BF16)
        mg_ref[...] = merged
        h1_ref[...] = x_ref[...] + _dot(merged, w_ref[...])

    return _pcall(
        body, "merge_out", (t // tm,),
        [_rows(tm, D), _rows(tm, D), _rows(tm, 3 * D), _rows(tm, D), _full(D, D)],
        [_rows(tm, D), _rows(tm, D)],
        [_sds((t, D)), _sds((t, D), BF16)],
    )(x, h, g3, yb, w_out)


def _mlp_up(h1, g, w_up, tm):
    t = h1.shape[0]

    def body(h_ref, g_ref, w_ref, u_ref, n2_ref):
        hv = h_ref[...]
        n2 = (hv * _rms_scale(hv) * g_ref[...]).astype(BF16)
        n2_ref[...] = n2
        for c0 in range(0, D_FF, 512):
            u_ref[:, c0:c0 + 512] = _dot(n2, w_ref[:, c0:c0 + 512])

    return _pcall(
        body, "mlp_up", (t // tm,),
        [_rows(tm, D), _full(1, D), _full(D, D_FF)],
        [_rows(tm, D_FF), _rows(tm, D)],
        [_sds((t, D_FF)), _sds((t, D), BF16)],
    )(h1, g, w_up)


def _mlp_down_loss(u, h1, target, w_down, g, tm):
    t = u.shape[0]

    def body(u_ref, h1_ref, tg_ref, w_ref, g_ref, act_ref, dh2_ref, loss_ref, gnf_ref, lacc):
        i = pl.program_id(0)

        @pl.when(i == 0)
        def _():
            lacc[...] = jnp.zeros_like(lacc)
            gnf_ref[...] = jnp.zeros_like(gnf_ref)

        ru = jnp.maximum(u_ref[...], 0.0)
        act = (ru * ru).astype(BF16)
        act_ref[...] = act
        h2 = h1_ref[...] + _dot(act, w_ref[...])
        rs = _rms_scale(h2)
        gv = g_ref[...]
        err = h2 * rs * gv - tg_ref[...]
        lacc[...] += jnp.sum(err * err, axis=0, keepdims=True)
        dx, dgr = _rms_bwd(h2, rs, gv, err * (1.0 / D))
        dh2_ref[...] = dx
        gnf_ref[...] += jnp.sum(dgr, axis=0, keepdims=True)

        @pl.when(i == pl.num_programs(0) - 1)
        def _():
            loss_ref[...] = jnp.broadcast_to(jnp.sum(lacc[...], axis=1, keepdims=True) * (0.5 / D), (8, 128))

    return _pcall(
        body, "mlp_down_loss", (t // tm,),
        [_rows(tm, D_FF), _rows(tm, D), _rows(tm, D), _full(D_FF, D), _full(1, D)],
        [_rows(tm, D_FF), _rows(tm, D), _full(8, 128), _full(1, D)],
        [_sds((t, D_FF), BF16), _sds((t, D)), _sds((8, 128)), _sds((1, D))],
        scratch=[pltpu.VMEM((1, D), F32)],
    )(u, h1, target, w_down, g)


def _matmul_tn(a, g, name, chunked=None, comm=None):
    t, kdim = a.shape
    ndim = g.shape[1]
    tk = min(kdim, 1024)
    tn = ndim if ndim <= 1024 else 1024
    if chunked == "cols":
        assert tk == kdim and tn == ndim // N_CHIPS
    elif chunked == "rows":
        assert tk == kdim // N_CHIPS and tn == ndim
    tt = min(t, 2048)
    nt = t // tt

    def body(a_ref, g_ref, o_ref):
        @pl.when(pl.program_id(2) == 0)
        def _():
            o_ref[...] = jnp.zeros_like(o_ref)

        o_ref[...] += _dot_tn(a_ref[...].astype(BF16), g_ref[...].astype(BF16)).reshape(o_ref.shape)

    if chunked is not None:
        out_spec = pl.BlockSpec((2, None, tk // 2, tn), lambda i, j, s: (0, i + j, 0, 0))
        out_shape = _sds((2, N_CHIPS, tk // 2, tn))
    else:
        out_spec, out_shape = pl.BlockSpec((tk, tn), lambda i, j, s: (i, j)), _sds((kdim, ndim))
    return _pcall(
        body, name, (kdim // tk, ndim // tn, nt),
        [pl.BlockSpec((tt, tk), lambda i, j, s: (s, i)), pl.BlockSpec((tt, tn), lambda i, j, s: (s, j))],
        out_spec, out_shape, comm=comm,
    )(a, g)


def _mlp_bwd_act(dh2, u, w_down, tm):
    t = u.shape[0]

    def body(d_ref, u_ref, w_ref, du_ref):
        db = d_ref[...].astype(BF16)
        for c0 in range(0, D_FF, 512):
            da = _dot_nt(db, w_ref[c0:c0 + 512, :])
            du_ref[:, c0:c0 + 512] = (da * (2.0 * jnp.maximum(u_ref[:, c0:c0 + 512], 0.0))).astype(BF16)

    return _pcall(
        body, "mlp_bwd_act", (t // tm,),
        [_rows(tm, D), _rows(tm, D_FF), _full(D_FF, D)],
        _rows(tm, D_FF), _sds((t, D_FF), BF16),
    )(dh2, u, w_down)


def _mlp_bwd_in(du, dh2, h1, w_up, g, tm):
    t = du.shape[0]

    def body(du_ref, d_ref, h_ref, w_ref, g_ref, dh1_ref, gacc_ref):
        @pl.when(pl.program_id(0) == 0)
        def _():
            gacc_ref[...] = jnp.zeros_like(gacc_ref)

        dn2 = _dot_nt(du_ref[...], w_ref[...])
        hv = h_ref[...]
        dx, dgr = _rms_bwd(hv, _rms_scale(hv), g_ref[...], dn2)
        dh1_ref[...] = d_ref[...] + dx
        gacc_ref[...] += jnp.sum(dgr, axis=0, keepdims=True)

    return _pcall(
        body, "mlp_bwd_in", (t // tm,),
        [_rows(tm, D_FF), _rows(tm, D), _rows(tm, D), _full(D, D_FF), _full(1, D)],
        [_rows(tm, D), _full(1, D)],
        [_sds((t, D)), _sds((1, D))],
    )(du, dh2, h1, w_up, g)


def _merge_bwd(dh1, w_out, g3, h, yb, merged, tm):
    t = dh1.shape[0]

    def body(d_ref, w_ref, g3_ref, h_ref, yb_ref, mg_ref, dg3_ref, dyb_ref, dl_ref, dh_ref, dwo_ref):
        @pl.when(pl.program_id(0) == 0)
        def _():
            dwo_ref[...] = jnp.zeros_like(dwo_ref)

        db = d_ref[...].astype(BF16)
        dwo_ref[...] += _dot_tn(mg_ref[...], db)
        dm = _dot_nt(db, w_ref[...])
        gv = g3_ref[:, 0:D]
        sa = _sigmoid(g3_ref[:, D:2 * D])
        sb = _sigmoid(g3_ref[:, 2 * D:3 * D])
        gel = _gelu(gv)
        hv = h_ref[...]
        ybv = yb_ref[...]
        dya = dm * sa
        dyb = dm * sb
        dg3_ref[:, 0:D] = (dya * hv * _gelu_grad(gv)).astype(BF16)
        dg3_ref[:, D:2 * D] = (dya * (hv * gel) * (1.0 - sa)).astype(BF16)
        dg3_ref[:, 2 * D:3 * D] = (dyb * ybv * (1.0 - sb)).astype(BF16)
        dh_ref[...] = dya * gel
        dyb_ref[...] = dyb.astype(BF16)
        prod = dyb * ybv
        ones = jnp.ones((8, V_HEAD), F32)
        for hh in range(N_HEADS):
            dl_ref[hh] = lax.dot_general(ones, prod[:, hh * V_HEAD:(hh + 1) * V_HEAD], (((1,), (1,)), ((), ())),
                                         precision=lax.Precision.HIGHEST, preferred_element_type=F32)

    return _pcall(
        body, "merge_bwd", (t // tm,),
        [_rows(tm, D), _full(D, D), _rows(tm, 3 * D), _rows(tm, D), _rows(tm, D), _rows(tm, D)],
        [_rows(tm, 3 * D), _rows(tm, D), pl.BlockSpec((N_HEADS, 8, tm), lambda i: (0, 0, i)), _rows(tm, D), _full(D, D)],
        [_sds((t, 3 * D), BF16), _sds((t, D), BF16), _sds((N_HEADS, 8, t)), _sds((t, D)), _sds((D, D))],
    )(dh1, w_out, g3, h, yb, merged)


def _flash_bwd(q, k, v, do, lse, delta, tq, comm=None):
    t = q.shape[1]
    nq = t // tq

    def body(q_ref, k_ref, v_ref, do_ref, lse_ref, dl_ref, dqt_ref, dk_ref, dv_ref):
        ki = pl.program_id(1)

        @pl.when(ki == 0)
        def _():
            dqt_ref[...] = jnp.zeros_like(dqt_ref)

        kblk, vblk = k_ref[0], v_ref[0]
        kt = jnp.transpose(kblk)
        dk_ref[...] = jnp.zeros_like(dk_ref)
        dv_ref[...] = jnp.zeros_like(dv_ref)

        def block(qi, diagonal):
            rows = pl.ds(pl.multiple_of(qi * tq, tq), tq)
            qv, dov = q_ref[0, rows, :], do_ref[rows, :]
            p = jnp.exp2(_dot_nt(kblk, qv) - lse_ref[0, qi, 0:1, :])
            if diagonal:
                p = jnp.where(_lane_iota((tq, tq)) >= _row_iota((tq, tq)), p, 0.0)
            dv_ref[0] += _dot(p.astype(BF16), dov)
            dp = _dot_nt(vblk, dov)
            ds = (p * (dp - dl_ref[0, qi, 0:1, :]) * math.log(2.0)).astype(BF16)
            dk_ref[0] += _dot(ds, qv)
            dqt_ref[0, qi] += _dot(kt, ds)

        block(ki, True)

        def two(i, carry):
            block(ki + 1 + 2 * i, False)
            block(ki + 2 + 2 * i, False)
            return carry

        def one(qi, carry):
            block(qi, False)
            return carry

        pairs = (nq - 1 - ki) // 2
        lax.fori_loop(0, pairs, two, 0)
        lax.fori_loop(ki + 1 + 2 * pairs, nq, one, 0)

    kv_spec = lambda w: pl.BlockSpec((1, tq, w), lambda h, ki: (h, ki, 0))
    stat = pl.BlockSpec((1, nq, 8, tq), lambda h, ki: (h, 0, 0, 0))
    return _pcall(
        body, "flash_bwd", (N_HEADS, nq),
        [pl.BlockSpec((1, t, HEAD_W), lambda h, ki: (h, 0, 0)), kv_spec(HEAD_W), kv_spec(V_HEAD),
         pl.BlockSpec((t, V_HEAD), lambda h, ki: (0, h)), stat, stat],
        [pl.BlockSpec((1, nq, HEAD_W, tq), lambda h, ki: (h, 0, 0, 0)), kv_spec(HEAD_W), kv_spec(V_HEAD)],
        [_sds((N_HEADS, nq, HEAD_W, tq)), _sds((N_HEADS, t, HEAD_W)), _sds((N_HEADS, t, V_HEAD))],
        comm=comm,
    )(q, k, v, do, lse, delta)


def _mla_bwd(dqt, dk, dv, cqn, ckvn, cq, ckv, rope_c, wq, wkv, qn, kvn, tm):
    t = cq.shape[0]

    def body(dq_ref, dk_ref, dv_ref, cqn_ref, ckvn_ref, cq_ref, ckv_ref, c_ref, wq_ref, wkv_ref, qn_ref, kvn_ref,
             dmla_ref, dwq_ref, dwkv_ref, dqn_ref, dkvn_ref):
        @pl.when(pl.program_id(0) == 0)
        def _():
            dwq_ref[...] = jnp.zeros_like(dwq_ref)
            dwkv_ref[...] = jnp.zeros_like(dwkv_ref)
            dqn_ref[...] = jnp.zeros_like(dqn_ref)
            dkvn_ref[...] = jnp.zeros_like(dkvn_ref)

        c = c_ref[...]
        lane = _lane_iota((tm, KR_W))
        cqn, ckvn = cqn_ref[...], ckvn_ref[...]
        dcqn = jnp.zeros((tm, Q_LORA), F32)
        dckvn = jnp.zeros((tm, KV_LORA), F32)
        dkr = jnp.zeros((tm, KR_W), F32)
        for h in range(N_HEADS):
            sl = slice(h * HEAD_W, (h + 1) * HEAD_W)
            dqh = jnp.transpose(dq_ref[h, 0]) * EXP2_SCALE
            droped = jnp.where(lane < 64, dqh[:, 128:256], 0.0)
            dqp = jnp.concatenate([dqh[:, 0:128], _rope_pair(droped) * c], axis=1).astype(BF16)
            dcqn = dcqn + _dot_nt(dqp, wq_ref[:, sl])
            dwq_ref[:, sl] += _dot_tn(cqn, dqp)
            dkr = dkr + jnp.where(lane < 64, dk_ref[h, :, 128:256], 0.0)
            dkvp = jnp.concatenate([dk_ref[h, :, 0:128], dv_ref[h]], axis=1).astype(BF16)
            dckvn = dckvn + _dot_nt(dkvp, wkv_ref[:, sl])
            dwkv_ref[:, sl] += _dot_tn(ckvn, dkvp)
        cqv, ckvv = cq_ref[...], ckv_ref[...]
        dcq, dgq = _rms_bwd(cqv, _rms_scale(cqv), qn_ref[...], dcqn)
        dckv, dgkv = _rms_bwd(ckvv, _rms_scale(ckvv), kvn_ref[...], dckvn)
        dqn_ref[...] += jnp.sum(dgq, axis=0, keepdims=True)
        dkvn_ref[...] += jnp.sum(dgkv, axis=0, keepdims=True)
        dmla_ref[:, 0:256] = dcq.astype(BF16)
        dmla_ref[:, 256:512] = dckv.astype(BF16)
        dmla_ref[:, 512:640] = (_rope_pair(dkr) * c).astype(BF16)

    hb = lambda w: pl.BlockSpec((N_HEADS, tm, w), lambda i: (0, i, 0))
    wide = N_HEADS * HEAD_W
    per_q = dqt.shape[3] // tm
    dq_spec = pl.BlockSpec((N_HEADS, 1, HEAD_W, tm), lambda i: (0, i // per_q, 0, i % per_q))
    return _pcall(
        body, "mla_bwd", (t // tm,),
        [dq_spec, hb(HEAD_W), hb(V_HEAD), _rows(tm, Q_LORA), _rows(tm, KV_LORA), _rows(tm, Q_LORA), _rows(tm, KV_LORA),
         _rows(tm, KR_W), _full(Q_LORA, wide), _full(KV_LORA, wide), _full(1, Q_LORA), _full(1, KV_LORA)],
        [_rows(tm, 640), _full(Q_LORA, wide), _full(KV_LORA, wide), _full(1, Q_LORA), _full(1, KV_LORA)],
        [_sds((t, 640), BF16), _sds((Q_LORA, wide)), _sds((KV_LORA, wide)), _sds((1, Q_LORA)), _sds((1, KV_LORA))],
    )(dqt, dk, dv, cqn, ckvn, cq, ckv, rope_c, wq, wkv, qn, kvn)


def _lru_bwd(dh, xa, h, rx, conv_w, wa, ba, wx, bx, lam, tb, comm=None):
    t = dh.shape[0]
    nb = t // tb

    def body(dh_ref, xa_ref, h_ref, hp_ref, x_ref, cw_ref, wa_ref, ba_ref, wx_ref, bx_ref, lam_ref,
             drx_ref, dcw_ref, dcb_ref, dwa_ref, dba_ref, dwx_ref, dbx_ref, dlam_ref, gc, dxn, tmp, pre_r, pre_i):
        step = pl.program_id(0)
        first_block = step == nb - 1

        @pl.when(step == 0)
        def _():
            gc[...] = jnp.zeros_like(gc)
            dxn[...] = jnp.zeros_like(dxn)
            for ref in (dcw_ref, dcb_ref, dwa_ref, dba_ref, dwx_ref, dbx_ref, dlam_ref):
                ref[...] = jnp.zeros_like(ref)

        xa = xa_ref[...]
        r, gi = _lru_gates(xa, wa_ref, ba_ref[...], wx_ref, bx_ref[...], pre_r, pre_i)
        lamv = lam_ref[...]
        sp = _softplus(-lamv)
        la = (-LRU_C * sp) * r
        a = jnp.exp(la)
        e2 = _expm1(2.0 * la)
        sq = jnp.sqrt(-e2)
        row = _row_iota((tb, D))
        cf = jnp.where(row == tb - 1, 1.0, pltpu.roll(a, tb - 1, 0))
        c3, b3 = _scan_groups_bwd(cf.reshape(tb // 8, 8, D), dh_ref[...].reshape(tb // 8, 8, D))
        carry = gc[...]
        for grp in reversed(range(tb // 8)):
            dg = b3[grp] + c3[grp] * carry
            pre_r[8 * grp:8 * grp + 8, :] = dg
            carry = dg[0:1, :]
        delta = pre_r[...]
        gc[...] = a[0:1, :] * carry
        hv = h_ref[...]
        hr = pltpu.roll(hv, 1, 0)
        tmp[...] = hr
        tmp[0:1, :] = jnp.where(first_block, 0.0, hp_ref[7:8, :])
        hprev = tmp[...]
        ix = gi * xa
        dla = (delta * hprev) * a - (delta * ix) * ((e2 + 1.0) / sq)
        dlam_ref[...] += jnp.sum(dla * r, axis=0, keepdims=True) * (LRU_C * _sigmoid(-lamv))
        dpr = (dla * (-LRU_C * sp)) * r * (1.0 - r)
        dsq = delta * sq
        dpi = (dsq * xa) * gi * (1.0 - gi)
        dba_ref[...] += jnp.sum(dpr, axis=0, keepdims=True)
        dbx_ref[...] += jnp.sum(dpi, axis=0, keepdims=True)
        pre_r[...] = dpr
        pre_i[...] = dpi
        xb = xa.astype(BF16)
        for n in range(RNN_BLOCKS):
            sl = slice(n * RNN_BW, (n + 1) * RNN_BW)
            dprn = pre_r[:, sl].astype(BF16)
            dpin = pre_i[:, sl].astype(BF16)
            dwa_ref[n] += _dot_tn(xb[:, sl], dprn)
            dwx_ref[n] += _dot_tn(xb[:, sl], dpin)
            tmp[:, sl] = _dot_nt(dprn, wa_ref[n]) + _dot_nt(dpin, wx_ref[n])
        dxa = dsq * gi + tmp[...]
        dcb_ref[...] += jnp.sum(dxa, axis=0, keepdims=True)
        xv = x_ref[...]
        drx = cw_ref[3:4, :] * dxa
        dcw_ref[3:4, :] += jnp.sum(dxa * xv, axis=0, keepdims=True)
        row8 = _row_iota((8, D))
        nxt = dxn[...]
        for s in (1, 2, 3):
            dr_ = pltpu.roll(dxa, tb - s, 0)
            tmp[...] = dr_
            tmp[tb - 8:tb, :] = jnp.where(row8 >= 8 - s, pltpu.roll(nxt, 8 - s, 0), dr_[tb - 8:tb, :])
            dxs = tmp[...]
            drx = drx + cw_ref[3 - s:4 - s, :] * dxs
            dcw_ref[3 - s:4 - s, :] += jnp.sum(dxs * xv, axis=0, keepdims=True)
        drx_ref[...] = drx.astype(BF16)
        dxn[...] = dxa[0:8, :]

    rev = pl.BlockSpec((tb, D), lambda i: (nb - 1 - i, 0))
    prev8 = pl.BlockSpec((8, D), lambda i: (jnp.maximum((nb - 1 - i) * (tb // 8) - 1, 0), 0))
    wblk = _full(RNN_BLOCKS, RNN_BW, RNN_BW)
    return _pcall(
        body, "lru_bwd", (nb,),
        [rev, rev, rev, prev8, rev, _full(4, D), wblk, _full(1, D), wblk, _full(1, D), _full(1, D)],
        [rev, _full(4, D), _full(1, D), wblk, _full(1, D), wblk, _full(1, D), _full(1, D)],
        [_sds((t, D), BF16), _sds((4, D)), _sds((1, D)), _sds((RNN_BLOCKS, RNN_BW, RNN_BW)), _sds((1, D)),
         _sds((RNN_BLOCKS, RNN_BW, RNN_BW)), _sds((1, D)), _sds((1, D))],
        scratch=[pltpu.VMEM((1, D), F32), pltpu.VMEM((8, D), F32), pltpu.VMEM((tb, D), F32), pltpu.VMEM((tb, D), F32),
                 pltpu.VMEM((tb, D), F32)],
        comm=comm,
    )(dh, xa, h, h, rx, conv_w, wa, ba, wx, bx, lam)


def _inproj_bwd(x, dh1, drx, dg3, dmla, w1, g, tm, comm=None):
    t = x.shape[0]

    def body(x_ref, d_ref, drx_ref, dg3_ref, dmla_ref, w_ref, g_ref, dx_ref, gacc_ref):
        @pl.when(pl.program_id(0) == 0)
        def _():
            gacc_ref[...] = jnp.zeros_like(gacc_ref)

        dxn = _dot_nt(drx_ref[...], w_ref[:, 0:D])
        for c0 in range(0, 3 * D, D):
            dxn = dxn + _dot_nt(dg3_ref[:, c0:c0 + D], w_ref[:, D + c0:2 * D + c0])
        dxn = dxn + _dot_nt(dmla_ref[...], w_ref[:, 4 * D:W1_COLS])
        xv = x_ref[...]
        dx, dgr = _rms_bwd(xv, _rms_scale(xv), g_ref[...], dxn)
        dx_ref[...] = d_ref[...] + dx
        gacc_ref[...] += jnp.sum(dgr, axis=0, keepdims=True)

    return _pcall(
        body, "inproj_bwd", (t // tm,),
        [_rows(tm, D), _rows(tm, D), _rows(tm, D), _rows(tm, 3 * D), _rows(tm, 640), _full(D, W1_COLS), _full(1, D)],
        [_rows(tm, D), _full(1, D)],
        [_sds((t, D)), _sds((1, D))],
        comm=comm,
    )(x, dh1, drx, dg3, dmla, w1, g)


def _pcall_indexed(body, name, index, grid, in_specs, out_specs, out_shape):
    call = pl.pallas_call(
        body, name=name, out_shape=out_shape,
        grid_spec=pltpu.PrefetchScalarGridSpec(num_scalar_prefetch=1, grid=grid, in_specs=in_specs, out_specs=out_specs),
        compiler_params=pltpu.CompilerParams(dimension_semantics=("arbitrary",) * len(grid), vmem_limit_bytes=V7X_VMEM_LIMIT))
    return lambda *operands: call(index, *operands)


def _pair_sum(halves, theirs, core, out_dtype, name):
    _, rows, cols = halves.shape
    tm = _row_tile(rows)

    def body(c_ref, a_ref, b_ref, o_ref):
        o_ref[...] = (a_ref[0] + b_ref[...]).astype(out_dtype)

    plain = pl.BlockSpec((tm, cols), lambda i, c: (i, 0))
    return _pcall_indexed(body, name, core, (rows // tm,),
                          [pl.BlockSpec((1, tm, cols), lambda i, c: (c[0], i, 0)), plain], plain,
                          _sds((rows, cols), out_dtype))(halves, theirs)


def _chip_sum(parts, recv, chip, name):
    _, rows, cols = parts.shape
    tm = _row_tile(rows)

    def body(c_ref, a_ref, r_ref, o_ref):
        o_ref[...] = ((a_ref[0].astype(F32) + r_ref[0].astype(F32)) + r_ref[1].astype(F32)) + r_ref[2].astype(F32)

    return _pcall_indexed(body, name, chip, (rows // tm,),
                          [pl.BlockSpec((1, tm, cols), lambda i, c: (c[0], i, 0)),
                           pl.BlockSpec((N_CHIPS - 1, tm, cols), lambda i, c: (0, i, 0))],
                          pl.BlockSpec((tm, cols), lambda i, c: (i, 0)), _sds((rows, cols)))(parts, recv)


def _adam_math(w, gv, m, v):
    mn = ADAM_B1 * m + (1.0 - ADAM_B1) * gv
    vn = ADAM_B2 * v + (1.0 - ADAM_B2) * (gv * gv)
    m_hat = mn / (1.0 - ADAM_B1 ** ADAM_STEP)
    v_hat = vn / (1.0 - ADAM_B2 ** ADAM_STEP)
    return -ADAM_LR * (m_hat / (jnp.sqrt(v_hat) + ADAM_EPS) + ADAM_WD * w), mn, vn


def _adamw(w, g, m, v, name):
    rows, cols = w.shape
    tm = _row_tile(rows)

    def body(w_ref, g_ref, m_ref, v_ref, d_ref, mo_ref, vo_ref):
        d_ref[...], mo_ref[...], vo_ref[...] = _adam_math(w_ref[...], g_ref[...], m_ref[...], v_ref[...])

    spec = _rows(tm, cols)
    return _pcall(body, name, (rows // tm,), [spec] * 4, [spec] * 3, [_sds((rows, cols))] * 3)(w, g, m, v)


def _adamw_halves(w, mine, theirs, m, v, core, name, by_cols=False):
    rows, cols = w.shape
    if by_cols:
        tm, tc = mine.shape[0] // 2, cols // 2
        grid = (rows // tm, 2)
        full = pl.BlockSpec((tm, tc), lambda i, j, c: (i, j))
        half = pl.BlockSpec((tm, tc), lambda i, j, c: (i, 0))
    else:
        tm = _row_tile(rows // 2)
        nh = rows // 2 // tm
        grid = (rows // tm,)
        full = pl.BlockSpec((tm, cols), lambda i, c: (i, 0))
        half = pl.BlockSpec((tm, cols), lambda i, c: (i % nh, 0))

    def body(c_ref, w_ref, a_ref, b_ref, m_ref, v_ref, g_ref, d_ref, mo_ref, vo_ref):
        which = pl.program_id(1) if by_cols else pl.program_id(0) // nh
        gv = jnp.where(which == c_ref[0], a_ref[...], b_ref[...])
        g_ref[...] = gv
        d_ref[...], mo_ref[...], vo_ref[...] = _adam_math(w_ref[...], gv, m_ref[...], v_ref[...])

    return _pcall_indexed(body, name, core, grid, [full, half, half, full, full], [full] * 4,
                          [_sds((rows, cols))] * 4)(w, mine, theirs, m, v)


REL_SIBLING = (0, 0, 1)
REL_CHIPS = ((1, 0, 0), (0, 1, 0), (1, 1, 0))


V7X_DMA_CHUNK_BYTES = 1 << 20


def _split_copy(src, dst, shape, itemsize):
    nbytes = math.prod(shape) * itemsize
    if nbytes <= V7X_DMA_CHUNK_BYTES or len(shape) < 2:
        return [(src, dst)]
    if len(shape) > 2:
        out = []
        for k in range(shape[0]):
            out += _split_copy(src.at[k], dst.at[k], shape[1:], itemsize)
        return out
    rows = shape[0]
    sub = 8 * (4 // itemsize)
    parts = max(1, min(-(-nbytes // V7X_DMA_CHUNK_BYTES), rows // sub))
    while rows % parts or (rows // parts) % sub:
        parts -= 1
    step = rows // parts
    return [(src.at[pl.ds(k * step, step)], dst.at[pl.ds(k * step, step)]) for k in range(parts)]


def _mesh_pos():
    return (lax.axis_index("x"), lax.axis_index("y"), lax.axis_index("c"))


def _make_copy(i, op, sems, pos, src=None, dst=None):
    rel = op[0]
    src, dst = (op[1], op[2]) if src is None else (src, dst)
    send_sems, recv_sems = sems
    if rel is None:
        return pltpu.make_async_copy(src, dst, send_sems.at[i])
    peer = tuple((p + r) % 2 for p, r in zip(pos, rel))
    return pltpu.make_async_remote_copy(src_ref=src, dst_ref=dst, send_sem=send_sems.at[i], recv_sem=recv_sems.at[i],
                                        device_id=peer, device_id_type=MESH_ID)


def _start_copies(ops, sems, pos, base=0):
    for i, op in enumerate(ops):
        for s_piece, d_piece in _split_copy(op[1], op[2], op[1].shape, jnp.dtype(op[1].dtype).itemsize):
            _make_copy(base + i, op, sems, pos, s_piece, d_piece).start()


def _wait_copies(ops, sems, pos, base=0):
    for i, op in enumerate(ops):
        _make_copy(base + i, op, sems, pos).wait()


def _comm(name, ins, out_shapes, n_ops, ops_fn):
    n_in, n_out = len(ins), len(out_shapes)

    def body(*refs):
        in_refs, out_refs = refs[:n_in], refs[n_in:n_in + n_out]
        sems = refs[n_in + n_out:]
        pos = _mesh_pos()
        ops = ops_fn(in_refs, out_refs, pos)
        assert len(ops) == n_ops
        _start_copies(ops, sems, pos)
        _wait_copies(ops, sems, pos)

    hbm = pl.BlockSpec(memory_space=pl.ANY)
    return pl.pallas_call(
        body, name=name, in_specs=[hbm] * n_in, out_specs=[hbm] * n_out, out_shape=list(out_shapes),
        scratch_shapes=[pltpu.SemaphoreType.DMA((n_ops,)), pltpu.SemaphoreType.DMA((n_ops,))],
    )(*ins)


def _chip_of(pos, rel=(0, 0, 0)):
    return 2 * ((pos[0] + rel[0]) % 2) + (pos[1] + rel[1]) % 2


def _gather_chips(shards, chip, name):
    def ops_fn(in_refs, out_refs, pos):
        me = _chip_of(pos)
        return [(rel, src, dst.at[me]) for src, dst in zip(in_refs, out_refs) for rel in REL_CHIPS]

    outs = _comm(name, shards, [_sds((N_CHIPS,) + s.shape, s.dtype) for s in shards], 3 * len(shards), ops_fn)
    return [lax.dynamic_update_index_in_dim(o, s, chip, 0) for o, s in zip(outs, shards)]


def _halved_gather_ops(pos, srcs, dsts, whole):
    me, c = _chip_of(pos), pos[2]
    ici, d2d = [], []
    for a, (src, dst) in enumerate(zip(srcs, dsts)):
        for rel in REL_CHIPS:
            if a in whole:
                ici.append((rel, src, dst.at[me]))
            else:
                ici.append((rel, src.at[c], dst.at[me, c]))
                arrived = dst.at[_chip_of(pos, rel), c]
                d2d.append((REL_SIBLING, arrived, arrived))
    return ici, d2d


def _gather_halved(shards, whole, chip, name):
    srcs = [s if a in whole else s.reshape(2, s.shape[0] // 2, s.shape[1]) for a, s in enumerate(shards)]
    n_sh = len(shards)
    n_ici, n_d2d = 3 * n_sh, 3 * (n_sh - len(whole))

    def body(*refs):
        in_refs, out_refs, sems = refs[:n_sh], refs[n_sh:2 * n_sh], refs[2 * n_sh:]
        pos = _mesh_pos()
        ici, d2d = _halved_gather_ops(pos, in_refs, out_refs, whole)
        _start_copies(ici, sems, pos)
        _wait_copies(ici, sems, pos)
        _start_copies(d2d, sems, pos, base=n_ici)
        _wait_copies(d2d, sems, pos, base=n_ici)

    hbm = pl.BlockSpec(memory_space=pl.ANY)
    outs = pl.pallas_call(
        body, name=name, in_specs=[hbm] * n_sh, out_specs=[hbm] * n_sh,
        out_shape=[_sds((N_CHIPS,) + s.shape, s.dtype) for s in srcs],
        scratch_shapes=[pltpu.SemaphoreType.DMA((n_ici + n_d2d,)), pltpu.SemaphoreType.DMA((n_ici + n_d2d,))],
    )(*srcs)
    return [lax.dynamic_update_index_in_dim(o, s, chip, 0).reshape((N_CHIPS,) + sh.shape)
            for o, s, sh in zip(outs, srcs, shards)]


def _split_comm(gs):
    def ops_fn(in_refs, out_refs, pos):
        return [(REL_SIBLING, src.at[1 - pos[2]], dst) for src, dst in zip(in_refs, out_refs)]

    return gs, [_sds(g.shape[1:], g.dtype) for g in gs], len(gs), ops_fn


def _exchange_comm(ps):
    def ops_fn(in_refs, out_refs, pos):
        return [(rel, src.at[_chip_of(pos, rel)], dst.at[j])
                for src, dst in zip(in_refs, out_refs) for j, rel in enumerate(REL_CHIPS)]

    return ps, [_sds((N_CHIPS - 1,) + p.shape[1:], p.dtype) for p in ps], 3 * len(ps), ops_fn


def _join_comm(hs):
    def ops_fn(in_refs, out_refs, pos):
        return [(REL_SIBLING, src, dst) for src, dst in zip(in_refs, out_refs)]

    return hs, [_sds(h.shape, h.dtype) for h in hs], len(hs), ops_fn


def _run_comm(name, comm):
    ins, shapes, n_ops, ops_fn = comm
    return _comm(name, ins, shapes, n_ops, ops_fn)


def _rope_table(t):
    pos = np.arange(t, dtype=np.float32)
    inv_freq = (np.float32(1.0) / (np.float32(ROPE_THETA) ** (np.arange(0, QK_ROPE, 2, dtype=np.float32) / np.float32(QK_ROPE)))).astype(np.float32)
    ang = (pos[:, None] * inv_freq[None, :]).astype(np.float32)
    return np.concatenate([np.cos(ang), np.cos(ang), np.sin(ang), np.sin(ang)], axis=-1).astype(np.float32)


def _rot_cols(w):
    return jnp.concatenate([-w[..., 32:], w[..., :32]], axis=-1)


def _unrot_cols(dw):
    return jnp.concatenate([dw[..., 32:], -dw[..., :32]], axis=-1)


IN_OFFS = (0, 1024, 2048, 2304, 2560, 2624, 3648, 4672)


def _w1_from_w_in(w):
    seg = [w[:, IN_OFFS[i]:IN_OFFS[i + 1]] for i in range(7)]
    rnn_x, rnn_gate, cq, ckv, kr, ga, gb = seg
    return jnp.concatenate([rnn_x, rnn_gate, ga, gb, cq, ckv, kr, _rot_cols(kr)], axis=1)


def _w_in_grad_from_parts(d_rx, d_g3, d_mla):
    kr = d_mla[:, 512:576] + _unrot_cols(d_mla[:, 576:640])
    return jnp.concatenate([d_rx, d_g3[:, 0:D], d_mla[:, 0:512], kr, d_g3[:, D:3 * D]], axis=1)


def _wq_from_w_uq(w):
    w3 = w.reshape(Q_LORA, N_HEADS, QK_NOPE + QK_ROPE)
    rope = w3[..., QK_NOPE:]
    return jnp.concatenate([w3[..., :QK_NOPE], rope, _rot_cols(rope)], axis=-1).reshape(Q_LORA, N_HEADS * HEAD_W)


def _w_uq_grad_from_wq(dw):
    d3 = dw.reshape(Q_LORA, N_HEADS, HEAD_W)
    rope = d3[..., 128:192] + _unrot_cols(d3[..., 192:256])
    return jnp.concatenate([d3[..., :128], rope], axis=-1).reshape(Q_LORA, N_HEADS * (QK_NOPE + QK_ROPE))


def _cols_from_chunks(g):
    return g.transpose(1, 0, 2).reshape(g.shape[1], N_CHIPS * g.shape[2])


def _halves_of_col_chunks(dw):
    r, c4 = dw.shape
    return dw.reshape(2, r // 2, N_CHIPS, c4 // N_CHIPS).transpose(0, 2, 1, 3)


def _halves_of_row_chunks(dw):
    r4, c = dw.shape
    return dw.reshape(N_CHIPS, 2, r4 // (2 * N_CHIPS), c).transpose(1, 0, 2, 3)


def kernel(x, norm_mix, w_in, conv_w, conv_b, lru_wa, lru_ba, lru_wx, lru_bx, lru_lambda, q_norm, w_uq, kv_norm, w_ukv, w_out, norm_mlp, w_up, w_down, norm_final, loss_target, m_norm_mix, m_w_in, m_conv_w, m_conv_b, m_lru_wa, m_lru_ba, m_lru_wx, m_lru_bx, m_lru_lambda, m_q_norm, m_w_uq, m_kv_norm, m_w_ukv, m_w_out, m_norm_mlp, m_w_up, m_w_down, m_norm_final, v_norm_mix, v_w_in, v_conv_w, v_conv_b, v_lru_wa, v_lru_ba, v_lru_wx, v_lru_bx, v_lru_lambda, v_q_norm, v_w_uq, v_kv_norm, v_w_ukv, v_w_out, v_norm_mlp, v_w_up, v_w_down, v_norm_final):
    t = x.shape[1]
    tm = min(512, t)
    tb = min(256, t)
    tq = min(512, max(tm, t // 4))
    x2 = x[0]
    target = loss_target[0]
    chip = 2 * lax.axis_index("x") + lax.axis_index("y")
    core = lax.axis_index("c")
    chip_ix, core_ix = chip.reshape(1).astype(jnp.int32), core.reshape(1).astype(jnp.int32)
    row = lambda p: p.reshape(1, -1)

    big_shards = (w_in, w_uq, w_ukv, w_out, w_up, w_down)
    w_in_g, conv_w_g = _gather_halved([w_in.astype(BF16), conv_w], (1,), chip, "weight_gather_first")
    w1 = _w1_from_w_in(_cols_from_chunks(w_in_g))
    conv_w_f = _cols_from_chunks(conv_w_g)
    wa_b, wx_b = lru_wa.astype(BF16), lru_wx.astype(BF16)

    rope_c = jnp.asarray(_rope_table(t))

    (xn, rx, g3, cq, ckv, kr), gathered = _inproj(x2, row(norm_mix), w1, tm, [w.astype(BF16) for w in big_shards[1:]], chip)
    wq = _wq_from_w_uq(_cols_from_chunks(gathered[0]))
    wkv = _cols_from_chunks(gathered[1])
    w_out_f = gathered[2].reshape(D, D)
    w_up_f = _cols_from_chunks(gathered[3])
    w_down_f = gathered[4].reshape(D_FF, D)
    h, xa = _lru_fwd(rx, conv_w_f, row(conv_b), wa_b, row(lru_ba), wx_b, row(lru_bx), row(lru_lambda), tb)
    q, k, v, cqn, ckvn = _mla_proj(cq, ckv, kr, row(q_norm), row(kv_norm), wq, wkv, rope_c, tm)
    nq = t // tq
    yb, lse = _flash_fwd(q, k, v, tq)
    h1, merged = _merge_out(x2, h, g3, yb, w_out_f, tm)
    u, n2 = _mlp_up(h1, row(norm_mlp), w_up_f, tm)
    act, dh2, loss_blk, g_norm_final = _mlp_down_loss(u, h1, target, w_down_f, row(norm_final), tm)

    g_w_down = _matmul_tn(act, dh2, "grad_w_down", "rows")
    du = _mlp_bwd_act(dh2, u, w_down_f, tm)
    dh1, g_norm_mlp = _mlp_bwd_in(du, dh2, h1, w_up_f, row(norm_mlp), tm)
    g_w_up = _matmul_tn(n2, du, "grad_w_up", "cols")
    dg3, dyb, delta, dh, g_w_out = _merge_bwd(dh1, w_out_f, g3, h, yb, merged, tm)
    delta = delta.reshape(N_HEADS, 8, nq, tq).swapaxes(1, 2)
    def pair_sums(hvs, theirs, dtypes, tag):
        return [_pair_sum(hv.reshape(2, -1, hv.shape[-1]), r.reshape(-1, r.shape[-1]), core_ix, dt, f"grad_pair_sum_{tag}{a}").reshape(r.shape)
                for a, (hv, r, dt) in enumerate(zip(hvs, theirs, dtypes))]

    def chip_sums(parts, received, tag):
        return [_chip_sum(p, r, chip_ix, f"grad_chip_sum_{tag}{a}") for a, (p, r) in enumerate(zip(parts, received))]

    early = [_halves_of_row_chunks(g_w_out), g_w_up, g_w_down]
    (dq, dk, dv), early_theirs = _flash_bwd(q, k, v, dyb, lse, delta, tq, comm=_split_comm(early))
    early_parts = pair_sums(early, early_theirs, [BF16] * 3, "early")
    dmla, g_wq, g_wkv, g_q_norm, g_kv_norm = _mla_bwd(dq, dk, dv, cqn, ckvn, cq, ckv, rope_c, wq, wkv, row(q_norm), row(kv_norm), tm)
    (drx, g_conv_w, g_conv_b, g_wa, g_ba, g_wx, g_bx, g_lam), early_received = _lru_bwd(
        dh, xa, h, rx, conv_w_f, wa_b, row(lru_ba), wx_b, row(lru_bx), row(lru_lambda), tb, comm=_exchange_comm(early_parts))
    early_reduced = chip_sums(early_parts, early_received, "early")
    grad_x, g_norm_mix = _inproj_bwd(x2, dh1, drx, dg3, dmla, w1, row(norm_mix), tm)
    g_w_in_gates, early_sibling = _matmul_tn(xn, dg3, "grad_w_in_gates", comm=_join_comm(early_reduced))
    g_w_in = _w_in_grad_from_parts(_matmul_tn(xn, drx, "grad_w_in_rx"), g_w_in_gates, _matmul_tn(xn, dmla, "grad_w_in_mla"))
    g_w_uq = _w_uq_grad_from_wq(g_wq)

    smalls = (g_norm_mix, g_conv_b, g_wa, g_ba, g_wx, g_bx, g_lam, g_q_norm, g_kv_norm, g_norm_mlp, g_norm_final, g_conv_w)
    s_flat = jnp.concatenate([s.reshape(-1) for s in smalls] + [loss_blk[0, 0:1], jnp.zeros((S_LEN - N_SMALL - CONVW_SIZE - 1,), F32)])
    late = [_halves_of_col_chunks(g_w_in), _halves_of_col_chunks(g_w_uq), _halves_of_col_chunks(g_wkv),
            s_flat.reshape(N_CHIPS, 2, S_ROWS_HALF, 128).transpose(1, 0, 2, 3)]
    late_theirs = _run_comm("grad_sibling_split", _split_comm(late))
    late_parts = pair_sums(late, late_theirs, [BF16] * 3 + [F32], "late")
    late_reduced = chip_sums(late_parts, _run_comm("grad_chip_exchange", _exchange_comm(late_parts)), "late")
    late_sibling = _run_comm("grad_sibling_join", _join_comm(late_reduced))
    reduced = late_reduced[:3] + early_reduced
    reduced_sibling = list(late_sibling[:3]) + list(early_sibling)
    s_mine, s_theirs = late_reduced[3], late_sibling[3]
    s_chunk = jnp.where(core == 0, jnp.concatenate([s_mine, s_theirs]), jnp.concatenate([s_theirs, s_mine]))
    s_all = _gather_chips([s_chunk], chip, "small_grad_gather")[0].reshape(-1)

    small_grads = []
    off = 0
    for shp, n in zip(SMALL_SHAPES, SMALL_SIZES):
        small_grads.append(s_all[off:off + n].reshape(shp))
        off += n
    g_conv_w_mine = lax.dynamic_slice_in_dim(s_all[off:off + CONVW_SIZE].reshape(4, D), chip * (D // N_CHIPS), D // N_CHIPS, axis=1)
    loss = s_all[off + CONVW_SIZE]

    big_m = (m_w_in, m_w_uq, m_w_ukv, m_w_out, m_w_up, m_w_down)
    big_v = (v_w_in, v_w_uq, v_w_ukv, v_w_out, v_w_up, v_w_down)
    big_names = ("w_in", "w_uq", "w_ukv", "w_out", "w_up", "w_down")
    big_upd = [_adamw_halves(w, gm, gt, m, v, core_ix, "adamw_" + n)
               for w, gm, gt, m, v, n in zip(big_shards[1:], reduced[1:], reduced_sibling[1:], big_m[1:], big_v[1:], big_names[1:])]
    w_in_upd = _adamw_halves(w_in.T, reduced[0].T, reduced_sibling[0].T, m_w_in.T, v_w_in.T, core_ix, "adamw_w_in", by_cols=True)
    big_upd = [[u.T for u in w_in_upd]] + big_upd

    small_w = (norm_mix, conv_b, lru_wa, lru_ba, lru_wx, lru_bx, lru_lambda, q_norm, kv_norm, norm_mlp, norm_final)
    small_m = (m_norm_mix, m_conv_b, m_lru_wa, m_lru_ba, m_lru_wx, m_lru_bx, m_lru_lambda, m_q_norm, m_kv_norm, m_norm_mlp, m_norm_final)
    small_v = (v_norm_mix, v_conv_b, v_lru_wa, v_lru_ba, v_lru_wx, v_lru_bx, v_lru_lambda, v_q_norm, v_kv_norm, v_norm_mlp, v_norm_final)

    def pack(items, last, fill):
        flat = jnp.concatenate([i.reshape(-1) for i in items] + [last.reshape(-1)])
        return jnp.concatenate([flat, jnp.full((PACK_ROWS * 128 - flat.shape[0],), fill, F32)]).reshape(PACK_ROWS, 128)

    packed = _adamw(pack(small_w, conv_w, 0.0), pack(small_grads, g_conv_w_mine, 0.0), pack(small_m, m_conv_w, 0.0),
                    pack(small_v, v_conv_w, 1.0), "adamw_small")

    def unpack(p):
        flat = p.reshape(-1)
        outs, o = [], 0
        for shp, n in zip(SMALL_SHAPES, SMALL_SIZES):
            outs.append(flat[o:o + n].reshape(shp))
            o += n
        return outs, flat[o:o + CONVW_SIZE // N_CHIPS].reshape(4, D // N_CHIPS)

    order = ("norm_mix", "w_in", "conv_w", "conv_b", "lru_wa", "lru_ba", "lru_wx", "lru_bx", "lru_lambda", "q_norm", "w_uq", "kv_norm",
             "w_ukv", "w_out", "norm_mlp", "w_up", "w_down", "norm_final")

    def assemble(small_list, conv_w_item, big_list):
        table = dict(zip(SMALL_NAMES, small_list))
        table["conv_w"] = conv_w_item
        table.update(zip(big_names, big_list))
        return [table[n] for n in order]

    outs = [loss, grad_x.reshape(1, t, D)]
    outs += assemble(small_grads, g_conv_w_mine, [b[0] for b in big_upd])
    for j in range(3):
        sm, cw = unpack(packed[j])
        outs += assemble(sm, cw, [b[j + 1] for b in big_upd])
    return tuple(outs)
```

```python
import functools
import math

import jax
import jax.numpy as jnp
import numpy as np
from jax import lax
from jax.experimental import pallas as pl
from jax.experimental.pallas import tpu as pltpu

F32 = jnp.float32
BF16 = jnp.bfloat16

D = 1024
N_HEADS = 8
QK_NOPE = 128
QK_ROPE = 64
V_HEAD = 128
Q_LORA = 256
KV_LORA = 256
D_FF = 4096
RNN_BLOCKS = 8
RNN_BW = 128
LRU_C = 8.0
EPS = 1e-6
ROPE_THETA = 10000.0
HEAD_W = 256
KR_W = 128
W1_COLS = 4 * D + Q_LORA + KV_LORA + KR_W
SM_SCALE = (QK_NOPE + QK_ROPE) ** -0.5
EXP2_SCALE = SM_SCALE * math.log2(math.e)
NEG = float(jnp.finfo(jnp.float32).min)

ADAM_LR = 0.001
ADAM_B1 = 0.9
ADAM_B2 = 0.999
ADAM_EPS = 1e-08
ADAM_WD = 0.01
ADAM_STEP = 10

N_CHIPS = 4
V7X_VMEM_LIMIT = 56 * 1024 * 1024
MESH_ID = pl.DeviceIdType.MESH

SMALL_NAMES = ("norm_mix", "conv_b", "lru_wa", "lru_ba", "lru_wx", "lru_bx", "lru_lambda", "q_norm", "kv_norm", "norm_mlp", "norm_final")
SMALL_SHAPES = ((D,), (D,), (RNN_BLOCKS, RNN_BW, RNN_BW), (RNN_BLOCKS, RNN_BW), (RNN_BLOCKS, RNN_BW, RNN_BW), (RNN_BLOCKS, RNN_BW), (D,),
                (Q_LORA,), (KV_LORA,), (D,), (D,))
SMALL_SIZES = tuple(math.prod(s) for s in SMALL_SHAPES)
N_SMALL = sum(SMALL_SIZES)
CONVW_SIZE = 4 * D
S_LEN = -(-(N_SMALL + CONVW_SIZE) // 8192) * 8192
S_ROWS_HALF = S_LEN // (N_CHIPS * 2 * 128)
PACK_ROWS = -(-(N_SMALL + CONVW_SIZE // N_CHIPS) // (256 * 128)) * 256


def _pcall(body, name, grid, in_specs, out_specs, out_shape, scratch=(), comm=None):
    params = pltpu.CompilerParams(dimension_semantics=("arbitrary",) * len(grid), vmem_limit_bytes=V7X_VMEM_LIMIT)
    if comm is None:
        return pl.pallas_call(body, name=name, grid=grid, in_specs=in_specs, out_specs=out_specs, out_shape=out_shape,
                              scratch_shapes=list(scratch), compiler_params=params)
    c_ins, c_shapes, n_ops, ops_fn = comm
    single = not isinstance(out_specs, (list, tuple))
    out_specs, out_shape = ([out_specs], [out_shape]) if single else (list(out_specs), list(out_shape))
    n_in, n_out, n_sc, n_ci, n_co = len(in_specs), len(out_specs), len(scratch), len(c_ins), len(c_shapes)

    def wrapped(*refs):
        ins, refs = refs[:n_in], refs[n_in:]
        c_in_refs, refs = refs[:n_ci], refs[n_ci:]
        outs, refs = refs[:n_out], refs[n_out:]
        c_out_refs, refs = refs[:n_co], refs[n_co:]
        own_scratch, sems = refs[:n_sc], refs[n_sc:]
        pos = _mesh_pos()
        ops = ops_fn(c_in_refs, c_out_refs, pos)
        ops, then = ops if isinstance(ops, tuple) else (ops, [])
        assert len(ops) + len(then) == n_ops
        first, last = True, True
        for d, n in enumerate(grid):
            first = first & (pl.program_id(d) == 0)
            last = last & (pl.program_id(d) == n - 1)

        @pl.when(first)
        def _():
            _start_copies(ops, sems, pos)

        body(*ins, *outs, *own_scratch)

        @pl.when(last)
        def _():
            _wait_copies(ops, sems, pos)
            _start_copies(then, sems, pos, base=len(ops))
            _wait_copies(then, sems, pos, base=len(ops))

    hbm = pl.BlockSpec(memory_space=pl.ANY)
    call = pl.pallas_call(
        wrapped, name=name, grid=grid, in_specs=list(in_specs) + [hbm] * n_ci, out_specs=out_specs + [hbm] * n_co,
        out_shape=out_shape + list(c_shapes),
        scratch_shapes=list(scratch) + [pltpu.SemaphoreType.DMA((n_ops,)), pltpu.SemaphoreType.DMA((n_ops,))],
        compiler_params=params)

    def run(*operands):
        res = call(*operands, *c_ins)
        own = res[0] if single else res[:n_out]
        return own, res[n_out:]

    return run


def _rows(tm, w):
    return pl.BlockSpec((tm, w), lambda i: (i, 0))


def _full(*shape):
    return pl.BlockSpec(shape, lambda *_: (0,) * len(shape))


def _sds(shape, dtype=F32):
    return jax.ShapeDtypeStruct(shape, dtype)


def _row_tile(rows, cap=256):
    t = min(rows, cap)
    while rows % t or t % 8:
        t -= 1
    return t


def _dot(a, b):
    return jnp.dot(a, b, preferred_element_type=F32)


def _dot_nt(a, b):
    return lax.dot_general(a, b, (((1,), (1,)), ((), ())), preferred_element_type=F32)


def _dot_tn(a, b):
    return lax.dot_general(a, b, (((0,), (0,)), ((), ())), preferred_element_type=F32)


def _sigmoid(x):
    return 1.0 / (1.0 + jnp.exp(-x))


_GELU_C = math.sqrt(2.0 / math.pi)


def _gelu(x):
    return x * (0.5 * (1.0 + jnp.tanh(_GELU_C * (x + 0.044715 * (x * x * x)))))


def _gelu_grad(x):
    t = jnp.tanh(_GELU_C * (x + 0.044715 * (x * x * x)))
    cdf = 0.5 * (1.0 + t)
    return cdf + x * (0.5 * (1.0 - t * t) * _GELU_C * (1.0 + 3.0 * 0.044715 * (x * x)))


def _rms_scale(x):
    return lax.rsqrt(jnp.mean(x * x, axis=-1, keepdims=True) + EPS)


def _rms_bwd(x, rs, g, dy):
    gdy = dy * g
    dx = rs * gdy - x * ((rs * rs * rs) * jnp.mean(gdy * x, axis=-1, keepdims=True))
    return dx, dy * (x * rs)


def _log1p(e):
    u = 1.0 + e
    d = u - 1.0
    return jnp.where(d == 0.0, e, jnp.log(u) * (e / jnp.where(d == 0.0, 1.0, d)))


def _softplus(y):
    return jnp.maximum(y, 0.0) + _log1p(jnp.exp(-jnp.abs(y)))


def _expm1(x):
    u = jnp.exp(x)
    lu = jnp.log(u)
    safe = jnp.where((u == 1.0) | (u == 0.0), 1.0, lu)
    return jnp.where(u == 1.0, x, jnp.where(u == 0.0, -1.0, (u - 1.0) * (x / safe)))


def _row_iota(shape):
    return lax.broadcasted_iota(jnp.int32, shape, 0)


def _lane_iota(shape):
    return lax.broadcasted_iota(jnp.int32, shape, 1)


def _scan_groups_fwd(a, b):
    sub = lax.broadcasted_iota(jnp.int32, a.shape, 1)
    for sh in (1, 2, 4):
        m = sub >= sh
        b = jnp.where(m, a * pltpu.roll(b, sh, 1) + b, b)
        a = jnp.where(m, a * pltpu.roll(a, sh, 1), a)
    return a, b


def _scan_groups_bwd(c, b):
    sub = lax.broadcasted_iota(jnp.int32, c.shape, 1)
    for sh in (1, 2, 4):
        m = sub < 8 - sh
        b = jnp.where(m, b + c * pltpu.roll(b, 8 - sh, 1), b)
        c = jnp.where(m, c * pltpu.roll(c, 8 - sh, 1), c)
    return c, b


def _rope_pair(gc):
    return gc + pltpu.roll(gc, 64, 1)


def _inproj(x, g, w1, tm, comm=None):
    t = x.shape[0]
    widths = (D, 3 * D, Q_LORA, KV_LORA, KR_W)

    def body(x_ref, g_ref, w_ref, xn_ref, rx_ref, g3_ref, cq_ref, ckv_ref, kr_ref):
        xv = x_ref[...]
        xn = (xv * _rms_scale(xv) * g_ref[...]).astype(BF16)
        xn_ref[...] = xn
        col = 0
        for ref, w in zip((rx_ref, g3_ref, cq_ref, ckv_ref, kr_ref), widths):
            for c0 in range(0, w, 512):
                cw = min(512, w - c0)
                ref[:, c0:c0 + cw] = _dot(xn, w_ref[:, col + c0:col + c0 + cw])
            col += w

    return _pcall(
        body, "inproj", (t // tm,),
        [_rows(tm, D), _full(1, D), _full(D, W1_COLS)],
        [_rows(tm, D)] + [_rows(tm, w) for w in widths],
        [_sds((t, D), BF16)] + [_sds((t, w)) for w in widths],
        comm=comm,
    )(x, g, w1)


def _lru_gates(xa, wa_ref, ba, wx_ref, bx, pre_r, pre_i):
    xb = xa.astype(BF16)
    for n in range(RNN_BLOCKS):
        sl = slice(n * RNN_BW, (n + 1) * RNN_BW)
        pre_r[:, sl] = _dot(xb[:, sl], wa_ref[n])
        pre_i[:, sl] = _dot(xb[:, sl], wx_ref[n])
    r = _sigmoid(pre_r[...] + ba)
    i = _sigmoid(pre_i[...] + bx)
    return r, i


def _lru_fwd(rx, conv_w, conv_b, wa, ba, wx, bx, lam, tb, comm=None):
    t = rx.shape[0]
    nb = t // tb

    def body(x_ref, xp_ref, cw_ref, cb_ref, wa_ref, ba_ref, wx_ref, bx_ref, lam_ref, h_ref, xa_ref, hc, tmp, pre_r, pre_i):
        i_blk = pl.program_id(0)

        @pl.when(i_blk == 0)
        def _():
            hc[...] = jnp.zeros_like(hc)

        xv = x_ref[...]
        xp = jnp.where(i_blk > 0, xp_ref[...], 0.0)
        row8 = _row_iota((8, D))
        xa = cb_ref[...] + cw_ref[3:4, :] * xv
        for s in (1, 2, 3):
            xr = pltpu.roll(xv, s, 0)
            tmp[...] = xr
            tmp[0:8, :] = jnp.where(row8 < s, pltpu.roll(xp, s, 0), xr[0:8, :])
            xa = xa + cw_ref[3 - s:4 - s, :] * tmp[...]
        xa_ref[...] = xa
        r, gi = _lru_gates(xa, wa_ref, ba_ref[...], wx_ref, bx_ref[...], pre_r, pre_i)
        la = (-LRU_C * _softplus(-lam_ref[...])) * r
        a = jnp.exp(la)
        b = jnp.sqrt(-_expm1(2.0 * la)) * (gi * xa)
        a3, b3 = _scan_groups_fwd(a.reshape(tb // 8, 8, D), b.reshape(tb // 8, 8, D))
        carry = hc[...]
        for grp in range(tb // 8):
            hg = a3[grp] * carry + b3[grp]
            h_ref[8 * grp:8 * grp + 8, :] = hg
            carry = hg[7:8, :]
        hc[...] = carry

    prev8 = pl.BlockSpec((8, D), lambda i: (jnp.maximum(i * (tb // 8) - 1, 0), 0))
    return _pcall(
        body, "lru_fwd", (nb,),
        [_rows(tb, D), prev8, _full(4, D), _full(1, D), _full(RNN_BLOCKS, RNN_BW, RNN_BW), _full(1, D),
         _full(RNN_BLOCKS, RNN_BW, RNN_BW), _full(1, D), _full(1, D)],
        [_rows(tb, D), _rows(tb, D)],
        [_sds((t, D)), _sds((t, D))],
        scratch=[pltpu.VMEM((1, D), F32), pltpu.VMEM((tb, D), F32), pltpu.VMEM((tb, D), F32), pltpu.VMEM((tb, D), F32)],
        comm=comm,
    )(rx, rx, conv_w, conv_b, wa, ba, wx, bx, lam)


def _mla_proj(cq, ckv, kr, qn, kvn, wq, wkv, rope_c, tm):
    t = cq.shape[0]

    def body(cq_ref, ckv_ref, kr_ref, qn_ref, kvn_ref, wq_ref, wkv_ref, c_ref, q_ref, k_ref, v_ref, cqn_ref, ckvn_ref):
        cqv = cq_ref[...]
        cqn = (cqv * _rms_scale(cqv) * qn_ref[...]).astype(BF16)
        ckvv = ckv_ref[...]
        ckvn = (ckvv * _rms_scale(ckvv) * kvn_ref[...]).astype(BF16)
        cqn_ref[...] = cqn
        ckvn_ref[...] = ckvn
        c = c_ref[...]
        lane = _lane_iota((tm, KR_W))
        kro = jnp.where(lane < 64, _rope_pair(kr_ref[...] * c), 0.0).astype(BF16)
        for h in range(N_HEADS):
            sl = slice(h * HEAD_W, (h + 1) * HEAD_W)
            qh = _dot(cqn, wq_ref[:, sl])
            q_ref[h, :, 0:128] = (qh[:, 0:128] * EXP2_SCALE).astype(BF16)
            q_ref[h, :, 128:256] = (_rope_pair(qh[:, 128:256] * c) * EXP2_SCALE).astype(BF16)
            kvh = _dot(ckvn, wkv_ref[:, sl])
            k_ref[h, :, 0:128] = kvh[:, 0:128].astype(BF16)
            k_ref[h, :, 128:256] = kro
            v_ref[h, :, 0:V_HEAD] = kvh[:, 128:256].astype(BF16)
            v_ref[h, :, V_HEAD:2 * V_HEAD] = jnp.ones((tm, V_HEAD), BF16)

    hb = lambda w: pl.BlockSpec((N_HEADS, tm, w), lambda i: (0, i, 0))
    return _pcall(
        body, "mla_proj", (t // tm,),
        [_rows(tm, Q_LORA), _rows(tm, KV_LORA), _rows(tm, KR_W), _full(1, Q_LORA), _full(1, KV_LORA),
         _full(Q_LORA, N_HEADS * HEAD_W), _full(KV_LORA, N_HEADS * HEAD_W), _rows(tm, KR_W)],
        [hb(HEAD_W), hb(HEAD_W), hb(2 * V_HEAD), _rows(tm, Q_LORA), _rows(tm, KV_LORA)],
        [_sds((N_HEADS, t, HEAD_W), BF16), _sds((N_HEADS, t, HEAD_W), BF16), _sds((N_HEADS, t, 2 * V_HEAD), BF16),
         _sds((t, Q_LORA), BF16), _sds((t, KV_LORA), BF16)],
    )(cq, ckv, kr, qn, kvn, wq, wkv, rope_c)


def _flash_fwd(q, k, v, tq):
    t = q.shape[1]
    nq = t // tq

    def body(q_ref, k_ref, v_ref, o_ref, lse_ref, s_even, s_odd):
        qi = pl.program_id(1)
        qv = q_ref[0]

        def scores(ki, buf):
            buf[...] = _dot_nt(qv, k_ref[0, pl.ds(pl.multiple_of(ki * tq, tq), tq), :])

        def softmax_pv(ki, buf, carry, diagonal):
            m, acc = carry
            s = buf[...]
            if diagonal:
                s = jnp.where(_row_iota((tq, tq)) >= _lane_iota((tq, tq)), s, NEG)
            m_new = jnp.maximum(m, jnp.max(s, axis=1, keepdims=True))
            p = jnp.exp2(s - m_new)
            alpha = jnp.exp2(m - m_new)
            acc = alpha * acc + _dot(p.astype(BF16), v_ref[0, pl.ds(pl.multiple_of(ki * tq, tq), tq), :])
            return m_new, acc

        def finish(carry):
            m, acc = carry
            l = acc[:, V_HEAD:2 * V_HEAD]
            o_ref[...] = acc[:, 0:V_HEAD] / l
            lse = m + jnp.log(l) * math.log2(math.e)
            lse_ref[0, 0] = jnp.transpose(lse)[0:8, :]

        def two(i, carry):
            scores(2 * i + 1, s_odd)
            carry = softmax_pv(2 * i, s_even, carry, False)
            scores(2 * i + 2, s_even)
            return softmax_pv(2 * i + 1, s_odd, carry, False)

        init = (jnp.full((tq, 1), -jnp.inf, F32), jnp.zeros((tq, 2 * V_HEAD), F32))
        scores(0, s_even)
        carry = lax.fori_loop(0, qi // 2, two, init)

        @pl.when(qi % 2 == 0)
        def _():
            finish(softmax_pv(qi, s_even, carry, True))

        @pl.when(qi % 2 == 1)
        def _():
            scores(qi, s_odd)
            finish(softmax_pv(qi, s_odd, softmax_pv(qi - 1, s_even, carry, False), True))

    head = lambda w: pl.BlockSpec((1, t, w), lambda h, qi: (h, 0, 0))
    return _pcall(
        body, "flash_fwd", (N_HEADS, nq),
        [pl.BlockSpec((1, tq, HEAD_W), lambda h, qi: (h, qi, 0)), head(HEAD_W), head(2 * V_HEAD)],
        [pl.BlockSpec((tq, V_HEAD), lambda h, qi: (qi, h)), pl.BlockSpec((1, 1, 8, tq), lambda h, qi: (h, qi, 0, 0))],
        [_sds((t, D)), _sds((N_HEADS, nq, 8, tq))],
        scratch=[pltpu.VMEM((tq, tq), F32), pltpu.VMEM((tq, tq), F32)],
    )(q, k, v)


def _merge_out(x, h, g3, yb, w_out, tm):
    t = x.shape[0]

    def body(x_ref, h_ref, g3_ref, yb_ref, w_ref, h1_ref, mg_ref):
        ya = h_ref[...] * _gelu(g3_ref[:, 0:D])
        merged = (_sigmoid(g3_ref[:, D:2 * D]) * ya + _sigmoid(g3_ref[:, 2 * D:3 * D]) * yb_ref[...]).astype(BF16)
        mg_ref[...] = merged
        h1_ref[...] = x_ref[...] + _dot(merged, w_ref[...])

    return _pcall(
        body, "merge_out", (t // tm,),
        [_rows(tm, D), _rows(tm, D), _rows(tm, 3 * D), _rows(tm, D), _full(D, D)],
        [_rows(tm, D), _rows(tm, D)],
        [_sds((t, D)), _sds((t, D), BF16)],
    )(x, h, g3, yb, w_out)


def _mlp_up(h1, g, w_up, tm):
    t = h1.shape[0]

    def body(h_ref, g_ref, w_ref, u_ref, n2_ref):
        hv = h_ref[...]
        n2 = (hv * _rms_scale(hv) * g_ref[...]).astype(BF16)
        n2_ref[...] = n2
        for c0 in range(0, D_FF, 512):
            u_ref[:, c0:c0 + 512] = _dot(n2, w_ref[:, c0:c0 + 512])

    return _pcall(
        body, "mlp_up", (t // tm,),
        [_rows(tm, D), _full(1, D), _full(D, D_FF)],
        [_rows(tm, D_FF), _rows(tm, D)],
        [_sds((t, D_FF)), _sds((t, D), BF16)],
    )(h1, g, w_up)


def _mlp_down_loss(u, h1, target, w_down, g, tm):
    t = u.shape[0]

    def body(u_ref, h1_ref, tg_ref, w_ref, g_ref, act_ref, dh2_ref, loss_ref, gnf_ref, lacc):
        i = pl.program_id(0)

        @pl.when(i == 0)
        def _():
            lacc[...] = jnp.zeros_like(lacc)
            gnf_ref[...] = jnp.zeros_like(gnf_ref)

        ru = jnp.maximum(u_ref[...], 0.0)
        act = (ru * ru).astype(BF16)
        act_ref[...] = act
        h2 = h1_ref[...] + _dot(act, w_ref[...])
        rs = _rms_scale(h2)
        gv = g_ref[...]
        err = h2 * rs * gv - tg_ref[...]
        lacc[...] += jnp.sum(err * err, axis=0, keepdims=True)
        dx, dgr = _rms_bwd(h2, rs, gv, err * (1.0 / D))
        dh2_ref[...] = dx
        gnf_ref[...] += jnp.sum(dgr, axis=0, keepdims=True)

        @pl.when(i == pl.num_programs(0) - 1)
        def _():
            loss_ref[...] = jnp.broadcast_to(jnp.sum(lacc[...], axis=1, keepdims=True) * (0.5 / D), (8, 128))

    return _pcall(
        body, "mlp_down_loss", (t // tm,),
        [_rows(tm, D_FF), _rows(tm, D), _rows(tm, D), _full(D_FF, D), _full(1, D)],
        [_rows(tm, D_FF), _rows(tm, D), _full(8, 128), _full(1, D)],
        [_sds((t, D_FF), BF16), _sds((t, D)), _sds((8, 128)), _sds((1, D))],
        scratch=[pltpu.VMEM((1, D), F32)],
    )(u, h1, target, w_down, g)


def _matmul_tn(a, g, name, chunked=None, comm=None):
    t, kdim = a.shape
    ndim = g.shape[1]
    tk = min(kdim, 1024)
    tn = ndim if ndim <= 1024 else 1024
    if chunked == "cols":
        assert tk == kdim and tn == ndim // N_CHIPS
    elif chunked == "rows":
        assert tk == kdim // N_CHIPS and tn == ndim
    tt = min(t, 2048)
    nt = t // tt

    def body(a_ref, g_ref, o_ref):
        @pl.when(pl.program_id(2) == 0)
        def _():
            o_ref[...] = jnp.zeros_like(o_ref)

        o_ref[...] += _dot_tn(a_ref[...].astype(BF16), g_ref[...].astype(BF16)).reshape(o_ref.shape)

    if chunked is not None:
        out_spec = pl.BlockSpec((2, None, tk // 2, tn), lambda i, j, s: (0, i + j, 0, 0))
        out_shape = _sds((2, N_CHIPS, tk // 2, tn))
    else:
        out_spec, out_shape = pl.BlockSpec((tk, tn), lambda i, j, s: (i, j)), _sds((kdim, ndim))
    return _pcall(
        body, name, (kdim // tk, ndim // tn, nt),
        [pl.BlockSpec((tt, tk), lambda i, j, s: (s, i)), pl.BlockSpec((tt, tn), lambda i, j, s: (s, j))],
        out_spec, out_shape, comm=comm,
    )(a, g)


def _mlp_bwd_act(dh2, u, w_down, tm):
    t = u.shape[0]

    def body(d_ref, u_ref, w_ref, du_ref):
        db = d_ref[...].astype(BF16)
        for c0 in range(0, D_FF, 512):
            da = _dot_nt(db, w_ref[c0:c0 + 512, :])
            du_ref[:, c0:c0 + 512] = (da * (2.0 * jnp.maximum(u_ref[:, c0:c0 + 512], 0.0))).astype(BF16)

    return _pcall(
        body, "mlp_bwd_act", (t // tm,),
        [_rows(tm, D), _rows(tm, D_FF), _full(D_FF, D)],
        _rows(tm, D_FF), _sds((t, D_FF), BF16),
    )(dh2, u, w_down)


def _mlp_bwd_in(du, dh2, h1, w_up, g, tm):
    t = du.shape[0]

    def body(du_ref, d_ref, h_ref, w_ref, g_ref, dh1_ref, gacc_ref):
        @pl.when(pl.program_id(0) == 0)
        def _():
            gacc_ref[...] = jnp.zeros_like(gacc_ref)

        dn2 = _dot_nt(du_ref[...], w_ref[...])
        hv = h_ref[...]
        dx, dgr = _rms_bwd(hv, _rms_scale(hv), g_ref[...], dn2)
        dh1_ref[...] = d_ref[...] + dx
        gacc_ref[...] += jnp.sum(dgr, axis=0, keepdims=True)

    return _pcall(
        body, "mlp_bwd_in", (t // tm,),
        [_rows(tm, D_FF), _rows(tm, D), _rows(tm, D), _full(D, D_FF), _full(1, D)],
        [_rows(tm, D), _full(1, D)],
        [_sds((t, D)), _sds((1, D))],
    )(du, dh2, h1, w_up, g)


def _merge_bwd(dh1, w_out, g3, h, yb, merged, tm):
    t = dh1.shape[0]

    def body(d_ref, w_ref, g3_ref, h_ref, yb_ref, mg_ref, dg3_ref, dyb_ref, dl_ref, dh_ref, dwo_ref):
        @pl.when(pl.program_id(0) == 0)
        def _():
            dwo_ref[...] = jnp.zeros_like(dwo_ref)

        db = d_ref[...].astype(BF16)
        dwo_ref[...] += _dot_tn(mg_ref[...], db)
        dm = _dot_nt(db, w_ref[...])
        gv = g3_ref[:, 0:D]
        sa = _sigmoid(g3_ref[:, D:2 * D])
        sb = _sigmoid(g3_ref[:, 2 * D:3 * D])
        gel = _gelu(gv)
        hv = h_ref[...]
        ybv = yb_ref[...]
        dya = dm * sa
        dyb = dm * sb
        dg3_ref[:, 0:D] = (dya * hv * _gelu_grad(gv)).astype(BF16)
        dg3_ref[:, D:2 * D] = (dya * (hv * gel) * (1.0 - sa)).astype(BF16)
        dg3_ref[:, 2 * D:3 * D] = (dyb * ybv * (1.0 - sb)).astype(BF16)
        dh_ref[...] = dya * gel
        dyb_ref[...] = dyb.astype(BF16)
        prod = dyb * ybv
        ones = jnp.ones((8, V_HEAD), F32)
        for hh in range(N_HEADS):
            dl_ref[hh] = lax.dot_general(ones, prod[:, hh * V_HEAD:(hh + 1) * V_HEAD], (((1,), (1,)), ((), ())),
                                         precision=lax.Precision.HIGHEST, preferred_element_type=F32)

    return _pcall(
        body, "merge_bwd", (t // tm,),
        [_rows(tm, D), _full(D, D), _rows(tm, 3 * D), _rows(tm, D), _rows(tm, D), _rows(tm, D)],
        [_rows(tm, 3 * D), _rows(tm, D), pl.BlockSpec((N_HEADS, 8, tm), lambda i: (0, 0, i)), _rows(tm, D), _full(D, D)],
        [_sds((t, 3 * D), BF16), _sds((t, D), BF16), _sds((N_HEADS, 8, t)), _sds((t, D)), _sds((D, D))],
    )(dh1, w_out, g3, h, yb, merged)


def _flash_bwd(q, k, v, do, lse, delta, tq, comm=None):
    t = q.shape[1]
    nq = t // tq

    def body(q_ref, k_ref, v_ref, do_ref, lse_ref, dl_ref, dqt_ref, dk_ref, dv_ref):
        ki = pl.program_id(1)

        @pl.when(ki == 0)
        def _():
            dqt_ref[...] = jnp.zeros_like(dqt_ref)

        kblk, vblk = k_ref[0], v_ref[0]
        kt = jnp.transpose(kblk)
        dk_ref[...] = jnp.zeros_like(dk_ref)
        dv_ref[...] = jnp.zeros_like(dv_ref)

        def block(qi, diagonal):
            rows = pl.ds(pl.multiple_of(qi * tq, tq), tq)
            qv, dov = q_ref[0, rows, :], do_ref[rows, :]
            p = jnp.exp2(_dot_nt(kblk, qv) - lse_ref[0, qi, 0:1, :])
            if diagonal:
                p = jnp.where(_lane_iota((tq, tq)) >= _row_iota((tq, tq)), p, 0.0)
            dv_ref[0] += _dot(p.astype(BF16), dov)
            dp = _dot_nt(vblk, dov)
            ds = (p * (dp - dl_ref[0, qi, 0:1, :]) * math.log(2.0)).astype(BF16)
            dk_ref[0] += _dot(ds, qv)
            dqt_ref[0, qi] += _dot(kt, ds)

        block(ki, True)

        def two(i, carry):
            block(ki + 1 + 2 * i, False)
            block(ki + 2 + 2 * i, False)
            return carry

        def one(qi, carry):
            block(qi, False)
            return carry

        pairs = (nq - 1 - ki) // 2
        lax.fori_loop(0, pairs, two, 0)
        lax.fori_loop(ki + 1 + 2 * pairs, nq, one, 0)

    kv_spec = lambda w: pl.BlockSpec((1, tq, w), lambda h, ki: (h, ki, 0))
    stat = pl.BlockSpec((1, nq, 8, tq), lambda h, ki: (h, 0, 0, 0))
    return _pcall(
        body, "flash_bwd", (N_HEADS, nq),
        [pl.BlockSpec((1, t, HEAD_W), lambda h, ki: (h, 0, 0)), kv_spec(HEAD_W), kv_spec(V_HEAD),
         pl.BlockSpec((t, V_HEAD), lambda h, ki: (0, h)), stat, stat],
        [pl.BlockSpec((1, nq, HEAD_W, tq), lambda h, ki: (h, 0, 0, 0)), kv_spec(HEAD_W), kv_spec(V_HEAD)],
        [_sds((N_HEADS, nq, HEAD_W, tq)), _sds((N_HEADS, t, HEAD_W)), _sds((N_HEADS, t, V_HEAD))],
        comm=comm,
    )(q, k, v, do, lse, delta)


def _mla_bwd(dqt, dk, dv, cqn, ckvn, cq, ckv, rope_c, wq, wkv, qn, kvn, tm):
    t = cq.shape[0]

    def body(dq_ref, dk_ref, dv_ref, cqn_ref, ckvn_ref, cq_ref, ckv_ref, c_ref, wq_ref, wkv_ref, qn_ref, kvn_ref,
             dmla_ref, dwq_ref, dwkv_ref, dqn_ref, dkvn_ref):
        @pl.when(pl.program_id(0) == 0)
        def _():
            dwq_ref[...] = jnp.zeros_like(dwq_ref)
            dwkv_ref[...] = jnp.zeros_like(dwkv_ref)
            dqn_ref[...] = jnp.zeros_like(dqn_ref)
            dkvn_ref[...] = jnp.zeros_like(dkvn_ref)

        c = c_ref[...]
        lane = _lane_iota((tm, KR_W))
        cqn, ckvn = cqn_ref[...], ckvn_ref[...]
        dcqn = jnp.zeros((tm, Q_LORA), F32)
        dckvn = jnp.zeros((tm, KV_LORA), F32)
        dkr = jnp.zeros((tm, KR_W), F32)
        for h in range(N_HEADS):
            sl = slice(h * HEAD_W, (h + 1) * HEAD_W)
            dqh = jnp.transpose(dq_ref[h, 0]) * EXP2_SCALE
            droped = jnp.where(lane < 64, dqh[:, 128:256], 0.0)
            dqp = jnp.concatenate([dqh[:, 0:128], _rope_pair(droped) * c], axis=1).astype(BF16)
            dcqn = dcqn + _dot_nt(dqp, wq_ref[:, sl])
            dwq_ref[:, sl] += _dot_tn(cqn, dqp)
            dkr = dkr + jnp.where(lane < 64, dk_ref[h, :, 128:256], 0.0)
            dkvp = jnp.concatenate([dk_ref[h, :, 0:128], dv_ref[h]], axis=1).astype(BF16)
            dckvn = dckvn + _dot_nt(dkvp, wkv_ref[:, sl])
            dwkv_ref[:, sl] += _dot_tn(ckvn, dkvp)
        cqv, ckvv = cq_ref[...], ckv_ref[...]
        dcq, dgq = _rms_bwd(cqv, _rms_scale(cqv), qn_ref[...], dcqn)
        dckv, dgkv = _rms_bwd(ckvv, _rms_scale(ckvv), kvn_ref[...], dckvn)
        dqn_ref[...] += jnp.sum(dgq, axis=0, keepdims=True)
        dkvn_ref[...] += jnp.sum(dgkv, axis=0, keepdims=True)
        dmla_ref[:, 0:256] = dcq.astype(BF16)
        dmla_ref[:, 256:512] = dckv.astype(BF16)
        dmla_ref[:, 512:640] = (_rope_pair(dkr) * c).astype(BF16)

    hb = lambda w: pl.BlockSpec((N_HEADS, tm, w), lambda i: (0, i, 0))
    wide = N_HEADS * HEAD_W
    per_q = dqt.shape[3] // tm
    dq_spec = pl.BlockSpec((N_HEADS, 1, HEAD_W, tm), lambda i: (0, i // per_q, 0, i % per_q))
    return _pcall(
        body, "mla_bwd", (t // tm,),
        [dq_spec, hb(HEAD_W), hb(V_HEAD), _rows(tm, Q_LORA), _rows(tm, KV_LORA), _rows(tm, Q_LORA), _rows(tm, KV_LORA),
         _rows(tm, KR_W), _full(Q_LORA, wide), _full(KV_LORA, wide), _full(1, Q_LORA), _full(1, KV_LORA)],
        [_rows(tm, 640), _full(Q_LORA, wide), _full(KV_LORA, wide), _full(1, Q_LORA), _full(1, KV_LORA)],
        [_sds((t, 640), BF16), _sds((Q_LORA, wide)), _sds((KV_LORA, wide)), _sds((1, Q_LORA)), _sds((1, KV_LORA))],
    )(dqt, dk, dv, cqn, ckvn, cq, ckv, rope_c, wq, wkv, qn, kvn)


def _lru_bwd(dh, xa, h, rx, conv_w, wa, ba, wx, bx, lam, tb, comm=None):
    t = dh.shape[0]
    nb = t // tb

    def body(dh_ref, xa_ref, h_ref, hp_ref, x_ref, cw_ref, wa_ref, ba_ref, wx_ref, bx_ref, lam_ref,
             drx_ref, dcw_ref, dcb_ref, dwa_ref, dba_ref, dwx_ref, dbx_ref, dlam_ref, gc, dxn, tmp, pre_r, pre_i):
        step = pl.program_id(0)
        first_block = step == nb - 1

        @pl.when(step == 0)
        def _():
            gc[...] = jnp.zeros_like(gc)
            dxn[...] = jnp.zeros_like(dxn)
            for ref in (dcw_ref, dcb_ref, dwa_ref, dba_ref, dwx_ref, dbx_ref, dlam_ref):
                ref[...] = jnp.zeros_like(ref)

        xa = xa_ref[...]
        r, gi = _lru_gates(xa, wa_ref, ba_ref[...], wx_ref, bx_ref[...], pre_r, pre_i)
        lamv = lam_ref[...]
        sp = _softplus(-lamv)
        la = (-LRU_C * sp) * r
        a = jnp.exp(la)
        e2 = _expm1(2.0 * la)
        sq = jnp.sqrt(-e2)
        row = _row_iota((tb, D))
        cf = jnp.where(row == tb - 1, 1.0, pltpu.roll(a, tb - 1, 0))
        c3, b3 = _scan_groups_bwd(cf.reshape(tb // 8, 8, D), dh_ref[...].reshape(tb // 8, 8, D))
        carry = gc[...]
        for grp in reversed(range(tb // 8)):
            dg = b3[grp] + c3[grp] * carry
            pre_r[8 * grp:8 * grp + 8, :] = dg
            carry = dg[0:1, :]
        delta = pre_r[...]
        gc[...] = a[0:1, :] * carry
        hv = h_ref[...]
        hr = pltpu.roll(hv, 1, 0)
        tmp[...] = hr
        tmp[0:1, :] = jnp.where(first_block, 0.0, hp_ref[7:8, :])
        hprev = tmp[...]
        ix = gi * xa
        dla = (delta * hprev) * a - (delta * ix) * ((e2 + 1.0) / sq)
        dlam_ref[...] += jnp.sum(dla * r, axis=0, keepdims=True) * (LRU_C * _sigmoid(-lamv))
        dpr = (dla * (-LRU_C * sp)) * r * (1.0 - r)
        dsq = delta * sq
        dpi = (dsq * xa) * gi * (1.0 - gi)
        dba_ref[...] += jnp.sum(dpr, axis=0, keepdims=True)
        dbx_ref[...] += jnp.sum(dpi, axis=0, keepdims=True)
        pre_r[...] = dpr
        pre_i[...] = dpi
        xb = xa.astype(BF16)
        for n in range(RNN_BLOCKS):
            sl = slice(n * RNN_BW, (n + 1) * RNN_BW)
            dprn = pre_r[:, sl].astype(BF16)
            dpin = pre_i[:, sl].astype(BF16)
            dwa_ref[n] += _dot_tn(xb[:, sl], dprn)
            dwx_ref[n] += _dot_tn(xb[:, sl], dpin)
            tmp[:, sl] = _dot_nt(dprn, wa_ref[n]) + _dot_nt(dpin, wx_ref[n])
        dxa = dsq * gi + tmp[...]
        dcb_ref[...] += jnp.sum(dxa, axis=0, keepdims=True)
        xv = x_ref[...]
        drx = cw_ref[3:4, :] * dxa
        dcw_ref[3:4, :] += jnp.sum(dxa * xv, axis=0, keepdims=True)
        row8 = _row_iota((8, D))
        nxt = dxn[...]
        for s in (1, 2, 3):
            dr_ = pltpu.roll(dxa, tb - s, 0)
            tmp[...] = dr_
            tmp[tb - 8:tb, :] = jnp.where(row8 >= 8 - s, pltpu.roll(nxt, 8 - s, 0), dr_[tb - 8:tb, :])
            dxs = tmp[...]
            drx = drx + cw_ref[3 - s:4 - s, :] * dxs
            dcw_ref[3 - s:4 - s, :] += jnp.sum(dxs * xv, axis=0, keepdims=True)
        drx_ref[...] = drx.astype(BF16)
        dxn[...] = dxa[0:8, :]

    rev = pl.BlockSpec((tb, D), lambda i: (nb - 1 - i, 0))
    prev8 = pl.BlockSpec((8, D), lambda i: (jnp.maximum((nb - 1 - i) * (tb // 8) - 1, 0), 0))
    wblk = _full(RNN_BLOCKS, RNN_BW, RNN_BW)
    return _pcall(
        body, "lru_bwd", (nb,),
        [rev, rev, rev, prev8, rev, _full(4, D), wblk, _full(1, D), wblk, _full(1, D), _full(1, D)],
        [rev, _full(4, D), _full(1, D), wblk, _full(1, D), wblk, _full(1, D), _full(1, D)],
        [_sds((t, D), BF16), _sds((4, D)), _sds((1, D)), _sds((RNN_BLOCKS, RNN_BW, RNN_BW)), _sds((1, D)),
         _sds((RNN_BLOCKS, RNN_BW, RNN_BW)), _sds((1, D)), _sds((1, D))],
        scratch=[pltpu.VMEM((1, D), F32), pltpu.VMEM((8, D), F32), pltpu.VMEM((tb, D), F32), pltpu.VMEM((tb, D), F32),
                 pltpu.VMEM((tb, D), F32)],
        comm=comm,
    )(dh, xa, h, h, rx, conv_w, wa, ba, wx, bx, lam)


def _inproj_bwd(x, dh1, drx, dg3, dmla, w1, g, tm, comm=None):
    t = x.shape[0]

    def body(x_ref, d_ref, drx_ref, dg3_ref, dmla_ref, w_ref, g_ref, dx_ref, gacc_ref):
        @pl.when(pl.program_id(0) == 0)
        def _():
            gacc_ref[...] = jnp.zeros_like(gacc_ref)

        dxn = _dot_nt(drx_ref[...], w_ref[:, 0:D])
        for c0 in range(0, 3 * D, D):
            dxn = dxn + _dot_nt(dg3_ref[:, c0:c0 + D], w_ref[:, D + c0:2 * D + c0])
        dxn = dxn + _dot_nt(dmla_ref[...], w_ref[:, 4 * D:W1_COLS])
        xv = x_ref[...]
        dx, dgr = _rms_bwd(xv, _rms_scale(xv), g_ref[...], dxn)
        dx_ref[...] = d_ref[...] + dx
        gacc_ref[...] += jnp.sum(dgr, axis=0, keepdims=True)

    return _pcall(
        body, "inproj_bwd", (t // tm,),
        [_rows(tm, D), _rows(tm, D), _rows(tm, D), _rows(tm, 3 * D), _rows(tm, 640), _full(D, W1_COLS), _full(1, D)],
        [_rows(tm, D), _full(1, D)],
        [_sds((t, D)), _sds((1, D))],
        comm=comm,
    )(x, dh1, drx, dg3, dmla, w1, g)


def _pcall_indexed(body, name, index, grid, in_specs, out_specs, out_shape):
    call = pl.pallas_call(
        body, name=name, out_shape=out_shape,
        grid_spec=pltpu.PrefetchScalarGridSpec(num_scalar_prefetch=1, grid=grid, in_specs=in_specs, out_specs=out_specs),
        compiler_params=pltpu.CompilerParams(dimension_semantics=("arbitrary",) * len(grid), vmem_limit_bytes=V7X_VMEM_LIMIT))
    return lambda *operands: call(index, *operands)


def _pair_sum(halves, theirs, core, out_dtype, name):
    _, rows, cols = halves.shape
    tm = _row_tile(rows)

    def body(c_ref, a_ref, b_ref, o_ref):
        o_ref[...] = (a_ref[0] + b_ref[...]).astype(out_dtype)

    plain = pl.BlockSpec((tm, cols), lambda i, c: (i, 0))
    return _pcall_indexed(body, name, core, (rows // tm,),
                          [pl.BlockSpec((1, tm, cols), lambda i, c: (c[0], i, 0)), plain], plain,
                          _sds((rows, cols), out_dtype))(halves, theirs)


def _chip_sum(parts, recv, chip, name):
    _, rows, cols = parts.shape
    tm = _row_tile(rows)

    def body(c_ref, a_ref, r_ref, o_ref):
        o_ref[...] = ((a_ref[0].astype(F32) + r_ref[0].astype(F32)) + r_ref[1].astype(F32)) + r_ref[2].astype(F32)

    return _pcall_indexed(body, name, chip, (rows // tm,),
                          [pl.BlockSpec((1, tm, cols), lambda i, c: (c[0], i, 0)),
                           pl.BlockSpec((N_CHIPS - 1, tm, cols), lambda i, c: (0, i, 0))],
                          pl.BlockSpec((tm, cols), lambda i, c: (i, 0)), _sds((rows, cols)))(parts, recv)


def _adam_math(w, gv, m, v):
    mn = ADAM_B1 * m + (1.0 - ADAM_B1) * gv
    vn = ADAM_B2 * v + (1.0 - ADAM_B2) * (gv * gv)
    m_hat = mn / (1.0 - ADAM_B1 ** ADAM_STEP)
    v_hat = vn / (1.0 - ADAM_B2 ** ADAM_STEP)
    return -ADAM_LR * (m_hat / (jnp.sqrt(v_hat) + ADAM_EPS) + ADAM_WD * w), mn, vn


def _adamw(w, g, m, v, name):
    rows, cols = w.shape
    tm = _row_tile(rows)

    def body(w_ref, g_ref, m_ref, v_ref, d_ref, mo_ref, vo_ref):
        d_ref[...], mo_ref[...], vo_ref[...] = _adam_math(w_ref[...], g_ref[...], m_ref[...], v_ref[...])

    spec = _rows(tm, cols)
    return _pcall(body, name, (rows // tm,), [spec] * 4, [spec] * 3, [_sds((rows, cols))] * 3)(w, g, m, v)


def _adamw_halves(w, mine, theirs, m, v, core, name, by_cols=False):
    rows, cols = w.shape
    if by_cols:
        tm, tc = mine.shape[0] // 2, cols // 2
        grid = (rows // tm, 2)
        full = pl.BlockSpec((tm, tc), lambda i, j, c: (i, j))
        half = pl.BlockSpec((tm, tc), lambda i, j, c: (i, 0))
    else:
        tm = _row_tile(rows // 2)
        nh = rows // 2 // tm
        grid = (rows // tm,)
        full = pl.BlockSpec((tm, cols), lambda i, c: (i, 0))
        half = pl.BlockSpec((tm, cols), lambda i, c: (i % nh, 0))

    def body(c_ref, w_ref, a_ref, b_ref, m_ref, v_ref, g_ref, d_ref, mo_ref, vo_ref):
        which = pl.program_id(1) if by_cols else pl.program_id(0) // nh
        gv = jnp.where(which == c_ref[0], a_ref[...], b_ref[...])
        g_ref[...] = gv
        d_ref[...], mo_ref[...], vo_ref[...] = _adam_math(w_ref[...], gv, m_ref[...], v_ref[...])

    return _pcall_indexed(body, name, core, grid, [full, half, half, full, full], [full] * 4,
                          [_sds((rows, cols))] * 4)(w, mine, theirs, m, v)


REL_SIBLING = (0, 0, 1)
REL_CHIPS = ((1, 0, 0), (0, 1, 0), (1, 1, 0))


V7X_DMA_CHUNK_BYTES = 1 << 20


def _split_copy(src, dst, shape, itemsize):
    nbytes = math.prod(shape) * itemsize
    if nbytes <= V7X_DMA_CHUNK_BYTES or len(shape) < 2:
        return [(src, dst)]
    if len(shape) > 2:
        out = []
        for k in range(shape[0]):
            out += _split_copy(src.at[k], dst.at[k], shape[1:], itemsize)
        return out
    rows = shape[0]
    sub = 8 * (4 // itemsize)
    parts = max(1, min(-(-nbytes // V7X_DMA_CHUNK_BYTES), rows // sub))
    while rows % parts or (rows // parts) % sub:
        parts -= 1
    step = rows // parts
    return [(src.at[pl.ds(k * step, step)], dst.at[pl.ds(k * step, step)]) for k in range(parts)]


def _mesh_pos():
    return (lax.axis_index("x"), lax.axis_index("y"), lax.axis_index("c"))


def _make_copy(i, op, sems, pos, src=None, dst=None):
    rel = op[0]
    src, dst = (op[1], op[2]) if src is None else (src, dst)
    send_sems, recv_sems = sems
    if rel is None:
        return pltpu.make_async_copy(src, dst, send_sems.at[i])
    peer = tuple((p + r) % 2 for p, r in zip(pos, rel))
    return pltpu.make_async_remote_copy(src_ref=src, dst_ref=dst, send_sem=send_sems.at[i], recv_sem=recv_sems.at[i],
                                        device_id=peer, device_id_type=MESH_ID)


def _start_copies(ops, sems, pos, base=0):
    for i, op in enumerate(ops):
        for s_piece, d_piece in _split_copy(op[1], op[2], op[1].shape, jnp.dtype(op[1].dtype).itemsize):
            _make_copy(base + i, op, sems, pos, s_piece, d_piece).start()


def _wait_copies(ops, sems, pos, base=0):
    for i, op in enumerate(ops):
        _make_copy(base + i, op, sems, pos).wait()


def _comm(name, ins, out_shapes, n_ops, ops_fn):
    n_in, n_out = len(ins), len(out_shapes)

    def body(*refs):
        in_refs, out_refs = refs[:n_in], refs[n_in:n_in + n_out]
        sems = refs[n_in + n_out:]
        pos = _mesh_pos()
        ops = ops_fn(in_refs, out_refs, pos)
        assert len(ops) == n_ops
        _start_copies(ops, sems, pos)
        _wait_copies(ops, sems, pos)

    hbm = pl.BlockSpec(memory_space=pl.ANY)
    return pl.pallas_call(
        body, name=name, in_specs=[hbm] * n_in, out_specs=[hbm] * n_out, out_shape=list(out_shapes),
        scratch_shapes=[pltpu.SemaphoreType.DMA((n_ops,)), pltpu.SemaphoreType.DMA((n_ops,))],
    )(*ins)


def _chip_of(pos, rel=(0, 0, 0)):
    return 2 * ((pos[0] + rel[0]) % 2) + (pos[1] + rel[1]) % 2


def _gather_chips(shards, chip, name):
    def ops_fn(in_refs, out_refs, pos):
        me = _chip_of(pos)
        return [(rel, src, dst.at[me]) for src, dst in zip(in_refs, out_refs) for rel in REL_CHIPS]

    outs = _comm(name, shards, [_sds((N_CHIPS,) + s.shape, s.dtype) for s in shards], 3 * len(shards), ops_fn)
    return [lax.dynamic_update_index_in_dim(o, s, chip, 0) for o, s in zip(outs, shards)]


def _halved_gather_ops(pos, srcs, dsts, whole):
    me, c = _chip_of(pos), pos[2]
    ici, d2d = [], []
    for a, (src, dst) in enumerate(zip(srcs, dsts)):
        for rel in REL_CHIPS:
            if a in whole:
                ici.append((rel, src, dst.at[me]))
            else:
                ici.append((rel, src.at[c], dst.at[me, c]))
                arrived = dst.at[_chip_of(pos, rel), c]
                d2d.append((REL_SIBLING, arrived, arrived))
    return ici, d2d


def _halved_gather_comm(shards):
    srcs = [s.reshape(2, s.shape[0] // 2, s.shape[1]) for s in shards]

    def ops_fn(in_refs, out_refs, pos):
        return _halved_gather_ops(pos, in_refs, out_refs, ())

    def finish(outs, chip):
        return [lax.dynamic_update_index_in_dim(o, s, chip, 0).reshape((N_CHIPS,) + sh.shape) for o, s, sh in zip(outs, srcs, shards)]

    return (srcs, [_sds((N_CHIPS,) + s.shape, s.dtype) for s in srcs], 6 * len(shards), ops_fn), finish


def _gather_halved(shards, whole, chip, name):
    srcs = [s if a in whole else s.reshape(2, s.shape[0] // 2, s.shape[1]) for a, s in enumerate(shards)]
    n_sh = len(shards)
    n_ici, n_d2d = 3 * n_sh, 3 * (n_sh - len(whole))

    def body(*refs):
        in_refs, out_refs, sems = refs[:n_sh], refs[n_sh:2 * n_sh], refs[2 * n_sh:]
        pos = _mesh_pos()
        ici, d2d = _halved_gather_ops(pos, in_refs, out_refs, whole)
        _start_copies(ici, sems, pos)
        _wait_copies(ici, sems, pos)
        _start_copies(d2d, sems, pos, base=n_ici)
        _wait_copies(d2d, sems, pos, base=n_ici)

    hbm = pl.BlockSpec(memory_space=pl.ANY)
    outs = pl.pallas_call(
        body, name=name, in_specs=[hbm] * n_sh, out_specs=[hbm] * n_sh,
        out_shape=[_sds((N_CHIPS,) + s.shape, s.dtype) for s in srcs],
        scratch_shapes=[pltpu.SemaphoreType.DMA((n_ici + n_d2d,)), pltpu.SemaphoreType.DMA((n_ici + n_d2d,))],
    )(*srcs)
    return [lax.dynamic_update_index_in_dim(o, s, chip, 0).reshape((N_CHIPS,) + sh.shape)
            for o, s, sh in zip(outs, srcs, shards)]


def _split_comm(gs):
    def ops_fn(in_refs, out_refs, pos):
        return [(REL_SIBLING, src.at[1 - pos[2]], dst) for src, dst in zip(in_refs, out_refs)]

    return gs, [_sds(g.shape[1:], g.dtype) for g in gs], len(gs), ops_fn


def _exchange_comm(ps):
    def ops_fn(in_refs, out_refs, pos):
        return [(rel, src.at[_chip_of(pos, rel)], dst.at[j])
                for src, dst in zip(in_refs, out_refs) for j, rel in enumerate(REL_CHIPS)]

    return ps, [_sds((N_CHIPS - 1,) + p.shape[1:], p.dtype) for p in ps], 3 * len(ps), ops_fn


def _join_comm(hs):
    def ops_fn(in_refs, out_refs, pos):
        return [(REL_SIBLING, src, dst) for src, dst in zip(in_refs, out_refs)]

    return hs, [_sds(h.shape, h.dtype) for h in hs], len(hs), ops_fn


def _run_comm(name, comm):
    ins, shapes, n_ops, ops_fn = comm
    return _comm(name, ins, shapes, n_ops, ops_fn)


def _rope_table(t):
    pos = np.arange(t, dtype=np.float32)
    inv_freq = (np.float32(1.0) / (np.float32(ROPE_THETA) ** (np.arange(0, QK_ROPE, 2, dtype=np.float32) / np.float32(QK_ROPE)))).astype(np.float32)
    ang = (pos[:, None] * inv_freq[None, :]).astype(np.float32)
    return np.concatenate([np.cos(ang), np.cos(ang), np.sin(ang), np.sin(ang)], axis=-1).astype(np.float32)


def _rot_cols(w):
    return jnp.concatenate([-w[..., 32:], w[..., :32]], axis=-1)


def _unrot_cols(dw):
    return jnp.concatenate([dw[..., 32:], -dw[..., :32]], axis=-1)


IN_OFFS = (0, 1024, 2048, 2304, 2560, 2624, 3648, 4672)


def _w1_from_w_in(w):
    seg = [w[:, IN_OFFS[i]:IN_OFFS[i + 1]] for i in range(7)]
    rnn_x, rnn_gate, cq, ckv, kr, ga, gb = seg
    return jnp.concatenate([rnn_x, rnn_gate, ga, gb, cq, ckv, kr, _rot_cols(kr)], axis=1)


def _w_in_grad_from_parts(d_rx, d_g3, d_mla):
    kr = d_mla[:, 512:576] + _unrot_cols(d_mla[:, 576:640])
    return jnp.concatenate([d_rx, d_g3[:, 0:D], d_mla[:, 0:512], kr, d_g3[:, D:3 * D]], axis=1)


def _wq_from_w_uq(w):
    w3 = w.reshape(Q_LORA, N_HEADS, QK_NOPE + QK_ROPE)
    rope = w3[..., QK_NOPE:]
    return jnp.concatenate([w3[..., :QK_NOPE], rope, _rot_cols(rope)], axis=-1).reshape(Q_LORA, N_HEADS * HEAD_W)


def _w_uq_grad_from_wq(dw):
    d3 = dw.reshape(Q_LORA, N_HEADS, HEAD_W)
    rope = d3[..., 128:192] + _unrot_cols(d3[..., 192:256])
    return jnp.concatenate([d3[..., :128], rope], axis=-1).reshape(Q_LORA, N_HEADS * (QK_NOPE + QK_ROPE))


def _cols_from_chunks(g):
    return g.transpose(1, 0, 2).reshape(g.shape[1], N_CHIPS * g.shape[2])


def _halves_of_col_chunks(dw):
    r, c4 = dw.shape
    return dw.reshape(2, r // 2, N_CHIPS, c4 // N_CHIPS).transpose(0, 2, 1, 3)


def _halves_of_row_chunks(dw):
    r4, c = dw.shape
    return dw.reshape(N_CHIPS, 2, r4 // (2 * N_CHIPS), c).transpose(1, 0, 2, 3)


def kernel(x, norm_mix, w_in, conv_w, conv_b, lru_wa, lru_ba, lru_wx, lru_bx, lru_lambda, q_norm, w_uq, kv_norm, w_ukv, w_out, norm_mlp, w_up, w_down, norm_final, loss_target, m_norm_mix, m_w_in, m_conv_w, m_conv_b, m_lru_wa, m_lru_ba, m_lru_wx, m_lru_bx, m_lru_lambda, m_q_norm, m_w_uq, m_kv_norm, m_w_ukv, m_w_out, m_norm_mlp, m_w_up, m_w_down, m_norm_final, v_norm_mix, v_w_in, v_conv_w, v_conv_b, v_lru_wa, v_lru_ba, v_lru_wx, v_lru_bx, v_lru_lambda, v_q_norm, v_w_uq, v_kv_norm, v_w_ukv, v_w_out, v_norm_mlp, v_w_up, v_w_down, v_norm_final):
    t = x.shape[1]
    tm = min(512, t)
    tb = min(256, t)
    tq = min(512, max(tm, t // 4))
    x2 = x[0]
    target = loss_target[0]
    chip = 2 * lax.axis_index("x") + lax.axis_index("y")
    core = lax.axis_index("c")
    chip_ix, core_ix = chip.reshape(1).astype(jnp.int32), core.reshape(1).astype(jnp.int32)
    row = lambda p: p.reshape(1, -1)

    big_shards = (w_in, w_uq, w_ukv, w_out, w_up, w_down)
    w_in_g, conv_w_g = _gather_halved([w_in.astype(BF16), conv_w], (1,), chip, "weight_gather_first")
    w1 = _w1_from_w_in(_cols_from_chunks(w_in_g))
    conv_w_f = _cols_from_chunks(conv_w_g)
    wa_b, wx_b = lru_wa.astype(BF16), lru_wx.astype(BF16)

    rope_c = jnp.asarray(_rope_table(t))

    comm_a, finish_a = _halved_gather_comm([w.astype(BF16) for w in (w_uq, w_ukv, w_out)])
    (xn, rx, g3, cq, ckv, kr), gathered = _inproj(x2, row(norm_mix), w1, tm, comm=comm_a)
    g_uq, g_ukv, g_out = finish_a(gathered, chip)
    wq = _wq_from_w_uq(_cols_from_chunks(g_uq))
    wkv = _cols_from_chunks(g_ukv)
    w_out_f = g_out.reshape(D, D)
    comm_b, finish_b = _halved_gather_comm([w.astype(BF16) for w in (w_up, w_down)])
    (h, xa), gathered = _lru_fwd(rx, conv_w_f, row(conv_b), wa_b, row(lru_ba), wx_b, row(lru_bx), row(lru_lambda), tb, comm=comm_b)
    g_up, g_down = finish_b(gathered, chip)
    w_up_f = _cols_from_chunks(g_up)
    w_down_f = g_down.reshape(D_FF, D)
    q, k, v, cqn, ckvn = _mla_proj(cq, ckv, kr, row(q_norm), row(kv_norm), wq, wkv, rope_c, tm)
    nq = t // tq
    yb, lse = _flash_fwd(q, k, v, tq)
    h1, merged = _merge_out(x2, h, g3, yb, w_out_f, tm)
    u, n2 = _mlp_up(h1, row(norm_mlp), w_up_f, tm)
    act, dh2, loss_blk, g_norm_final = _mlp_down_loss(u, h1, target, w_down_f, row(norm_final), tm)

    g_w_down = _matmul_tn(act, dh2, "grad_w_down", "rows")
    du = _mlp_bwd_act(dh2, u, w_down_f, tm)
    dh1, g_norm_mlp = _mlp_bwd_in(du, dh2, h1, w_up_f, row(norm_mlp), tm)
    g_w_up = _matmul_tn(n2, du, "grad_w_up", "cols")
    dg3, dyb, delta, dh, g_w_out = _merge_bwd(dh1, w_out_f, g3, h, yb, merged, tm)
    delta = delta.reshape(N_HEADS, 8, nq, tq).swapaxes(1, 2)
    def pair_sums(hvs, theirs, dtypes, tag):
        return [_pair_sum(hv.reshape(2, -1, hv.shape[-1]), r.reshape(-1, r.shape[-1]), core_ix, dt, f"grad_pair_sum_{tag}{a}").reshape(r.shape)
                for a, (hv, r, dt) in enumerate(zip(hvs, theirs, dtypes))]

    def chip_sums(parts, received, tag):
        return [_chip_sum(p, r, chip_ix, f"grad_chip_sum_{tag}{a}") for a, (p, r) in enumerate(zip(parts, received))]

    early = [_halves_of_row_chunks(g_w_out), g_w_up, g_w_down]
    (dq, dk, dv), early_theirs = _flash_bwd(q, k, v, dyb, lse, delta, tq, comm=_split_comm(early))
    early_parts = pair_sums(early, early_theirs, [BF16] * 3, "early")
    dmla, g_wq, g_wkv, g_q_norm, g_kv_norm = _mla_bwd(dq, dk, dv, cqn, ckvn, cq, ckv, rope_c, wq, wkv, row(q_norm), row(kv_norm), tm)
    (drx, g_conv_w, g_conv_b, g_wa, g_ba, g_wx, g_bx, g_lam), early_received = _lru_bwd(
        dh, xa, h, rx, conv_w_f, wa_b, row(lru_ba), wx_b, row(lru_bx), row(lru_lambda), tb, comm=_exchange_comm(early_parts))
    early_reduced = chip_sums(early_parts, early_received, "early")
    grad_x, g_norm_mix = _inproj_bwd(x2, dh1, drx, dg3, dmla, w1, row(norm_mix), tm)
    g_w_in_gates, early_sibling = _matmul_tn(xn, dg3, "grad_w_in_gates", comm=_join_comm(early_reduced))
    g_w_in = _w_in_grad_from_parts(_matmul_tn(xn, drx, "grad_w_in_rx"), g_w_in_gates, _matmul_tn(xn, dmla, "grad_w_in_mla"))
    g_w_uq = _w_uq_grad_from_wq(g_wq)

    smalls = (g_norm_mix, g_conv_b, g_wa, g_ba, g_wx, g_bx, g_lam, g_q_norm, g_kv_norm, g_norm_mlp, g_norm_final, g_conv_w)
    s_flat = jnp.concatenate([s.reshape(-1) for s in smalls] + [loss_blk[0, 0:1], jnp.zeros((S_LEN - N_SMALL - CONVW_SIZE - 1,), F32)])
    late = [_halves_of_col_chunks(g_w_in), _halves_of_col_chunks(g_w_uq), _halves_of_col_chunks(g_wkv),
            s_flat.reshape(N_CHIPS, 2, S_ROWS_HALF, 128).transpose(1, 0, 2, 3)]
    late_theirs = _run_comm("grad_sibling_split", _split_comm(late))
    late_parts = pair_sums(late, late_theirs, [BF16] * 3 + [F32], "late")
    late_reduced = chip_sums(late_parts, _run_comm("grad_chip_exchange", _exchange_comm(late_parts)), "late")
    late_sibling = _run_comm("grad_sibling_join", _join_comm(late_reduced))
    reduced = late_reduced[:3] + early_reduced
    reduced_sibling = list(late_sibling[:3]) + list(early_sibling)
    s_mine, s_theirs = late_reduced[3], late_sibling[3]
    s_chunk = jnp.where(core == 0, jnp.concatenate([s_mine, s_theirs]), jnp.concatenate([s_theirs, s_mine]))
    s_all = _gather_chips([s_chunk], chip, "small_grad_gather")[0].reshape(-1)

    small_grads = []
    off = 0
    for shp, n in zip(SMALL_SHAPES, SMALL_SIZES):
        small_grads.append(s_all[off:off + n].reshape(shp))
        off += n
    g_conv_w_mine = lax.dynamic_slice_in_dim(s_all[off:off + CONVW_SIZE].reshape(4, D), chip * (D // N_CHIPS), D // N_CHIPS, axis=1)
    loss = s_all[off + CONVW_SIZE]

    big_m = (m_w_in, m_w_uq, m_w_ukv, m_w_out, m_w_up, m_w_down)
    big_v = (v_w_in, v_w_uq, v_w_ukv, v_w_out, v_w_up, v_w_down)
    big_names = ("w_in", "w_uq", "w_ukv", "w_out", "w_up", "w_down")
    big_upd = [_adamw_halves(w, gm, gt, m, v, core_ix, "adamw_" + n)
               for w, gm, gt, m, v, n in zip(big_shards[1:], reduced[1:], reduced_sibling[1:], big_m[1:], big_v[1:], big_names[1:])]
    w_in_upd = _adamw_halves(w_in.T, reduced[0].T, reduced_sibling[0].T, m_w_in.T, v_w_in.T, core_ix, "adamw_w_in", by_cols=True)
    big_upd = [[u.T for u in w_in_upd]] + big_upd

    small_w = (norm_mix, conv_b, lru_wa, lru_ba, lru_wx, lru_bx, lru_lambda, q_norm, kv_norm, norm_mlp, norm_final)
    small_m = (m_norm_mix, m_conv_b, m_lru_wa, m_lru_ba, m_lru_wx, m_lru_bx, m_lru_lambda, m_q_norm, m_kv_norm, m_norm_mlp, m_norm_final)
    small_v = (v_norm_mix, v_conv_b, v_lru_wa, v_lru_ba, v_lru_wx, v_lru_bx, v_lru_lambda, v_q_norm, v_kv_norm, v_norm_mlp, v_norm_final)

    def pack(items, last, fill):
        flat = jnp.concatenate([i.reshape(-1) for i in items] + [last.reshape(-1)])
        return jnp.concatenate([flat, jnp.full((PACK_ROWS * 128 - flat.shape[0],), fill, F32)]).reshape(PACK_ROWS, 128)

    packed = _adamw(pack(small_w, conv_w, 0.0), pack(small_grads, g_conv_w_mine, 0.0), pack(small_m, m_conv_w, 0.0),
                    pack(small_v, v_conv_w, 1.0), "adamw_small")

    def unpack(p):
        flat = p.reshape(-1)
        outs, o = [], 0
        for shp, n in zip(SMALL_SHAPES, SMALL_SIZES):
            outs.append(flat[o:o + n].reshape(shp))
            o += n
        return outs, flat[o:o + CONVW_SIZE // N_CHIPS].reshape(4, D // N_CHIPS)

    order = ("norm_mix", "w_in", "conv_w", "conv_b", "lru_wa", "lru_ba", "lru_wx", "lru_bx", "lru_lambda", "q_norm", "w_uq", "kv_norm",
             "w_ukv", "w_out", "norm_mlp", "w_up", "w_down", "norm_final")

    def assemble(small_list, conv_w_item, big_list):
        table = dict(zip(SMALL_NAMES, small_list))
        table["conv_w"] = conv_w_item
        table.update(zip(big_names, big_list))
        return [table[n] for n in order]

    outs = [loss, grad_x.reshape(1, t, D)]
    outs += assemble(small_grads, g_conv_w_mine, [b[0] for b in big_upd])
    for j in range(3):
        sm, cw = unpack(packed[j])
        outs += assemble(sm, cw, [b[j + 1] for b in big_upd])
    return tuple(outs)
```

```python
import functools
import math

import jax
import jax.numpy as jnp
import numpy as np
from jax import lax
from jax.experimental import pallas as pl
from jax.experimental.pallas import tpu as pltpu

F32 = jnp.float32
BF16 = jnp.bfloat16

D = 1024
N_HEADS = 8
QK_NOPE = 128
QK_ROPE = 64
V_HEAD = 128
Q_LORA = 256
KV_LORA = 256
D_FF = 4096
RNN_BLOCKS = 8
RNN_BW = 128
LRU_C = 8.0
EPS = 1e-6
ROPE_THETA = 10000.0
HEAD_W = 256
KR_W = 128
W1_COLS = 4 * D + Q_LORA + KV_LORA + KR_W
SM_SCALE = (QK_NOPE + QK_ROPE) ** -0.5
EXP2_SCALE = SM_SCALE * math.log2(math.e)
NEG = float(jnp.finfo(jnp.float32).min)

ADAM_LR = 0.001
ADAM_B1 = 0.9
ADAM_B2 = 0.999
ADAM_EPS = 1e-08
ADAM_WD = 0.01
ADAM_STEP = 10

N_CHIPS = 4
V7X_VMEM_LIMIT = 56 * 1024 * 1024
MESH_ID = pl.DeviceIdType.MESH

SMALL_NAMES = ("norm_mix", "conv_b", "lru_wa", "lru_ba", "lru_wx", "lru_bx", "lru_lambda", "q_norm", "kv_norm", "norm_mlp", "norm_final")
SMALL_SHAPES = ((D,), (D,), (RNN_BLOCKS, RNN_BW, RNN_BW), (RNN_BLOCKS, RNN_BW), (RNN_BLOCKS, RNN_BW, RNN_BW), (RNN_BLOCKS, RNN_BW), (D,),
                (Q_LORA,), (KV_LORA,), (D,), (D,))
SMALL_SIZES = tuple(math.prod(s) for s in SMALL_SHAPES)
N_SMALL = sum(SMALL_SIZES)
CONVW_SIZE = 4 * D
S_LEN = -(-(N_SMALL + CONVW_SIZE) // 8192) * 8192
S_ROWS_HALF = S_LEN // (N_CHIPS * 2 * 128)
PACK_ROWS = -(-(N_SMALL + CONVW_SIZE // N_CHIPS) // (256 * 128)) * 256


def _pcall(body, name, grid, in_specs, out_specs, out_shape, scratch=(), comm=None):
    params = pltpu.CompilerParams(dimension_semantics=("arbitrary",) * len(grid), vmem_limit_bytes=V7X_VMEM_LIMIT)
    if comm is None:
        return pl.pallas_call(body, name=name, grid=grid, in_specs=in_specs, out_specs=out_specs, out_shape=out_shape,
                              scratch_shapes=list(scratch), compiler_params=params)
    c_ins, c_shapes, n_ops, ops_fn = comm
    single = not isinstance(out_specs, (list, tuple))
    out_specs, out_shape = ([out_specs], [out_shape]) if single else (list(out_specs), list(out_shape))
    n_in, n_out, n_sc, n_ci, n_co = len(in_specs), len(out_specs), len(scratch), len(c_ins), len(c_shapes)

    def wrapped(*refs):
        ins, refs = refs[:n_in], refs[n_in:]
        c_in_refs, refs = refs[:n_ci], refs[n_ci:]
        outs, refs = refs[:n_out], refs[n_out:]
        c_out_refs, refs = refs[:n_co], refs[n_co:]
        own_scratch, sems = refs[:n_sc], refs[n_sc:]
        pos = _mesh_pos()
        ops = ops_fn(c_in_refs, c_out_refs, pos)
        ops, then = ops if isinstance(ops, tuple) else (ops, [])
        assert len(ops) + len(then) == n_ops
        first, last = True, True
        for d, n in enumerate(grid):
            first = first & (pl.program_id(d) == 0)
            last = last & (pl.program_id(d) == n - 1)

        @pl.when(first)
        def _():
            _start_copies(ops, sems, pos)

        body(*ins, *outs, *own_scratch)

        @pl.when(last)
        def _():
            _wait_copies(ops, sems, pos)
            _start_copies(then, sems, pos, base=len(ops))
            _wait_copies(then, sems, pos, base=len(ops))

    hbm = pl.BlockSpec(memory_space=pl.ANY)
    call = pl.pallas_call(
        wrapped, name=name, grid=grid, in_specs=list(in_specs) + [hbm] * n_ci, out_specs=out_specs + [hbm] * n_co,
        out_shape=out_shape + list(c_shapes),
        scratch_shapes=list(scratch) + [pltpu.SemaphoreType.DMA((n_ops,)), pltpu.SemaphoreType.DMA((n_ops,))],
        compiler_params=params)

    def run(*operands):
        res = call(*operands, *c_ins)
        own = res[0] if single else res[:n_out]
        return own, res[n_out:]

    return run


def _rows(tm, w):
    return pl.BlockSpec((tm, w), lambda i: (i, 0))


def _full(*shape):
    return pl.BlockSpec(shape, lambda *_: (0,) * len(shape))


def _sds(shape, dtype=F32):
    return jax.ShapeDtypeStruct(shape, dtype)


def _row_tile(rows, cap=256, mult=8):
    t = min(rows, cap)
    while rows % t or t % mult:
        t -= 1
    return t


def _dot(a, b):
    return jnp.dot(a, b, preferred_element_type=F32)


def _dot_nt(a, b):
    return lax.dot_general(a, b, (((1,), (1,)), ((), ())), preferred_element_type=F32)


def _dot_tn(a, b):
    return lax.dot_general(a, b, (((0,), (0,)), ((), ())), preferred_element_type=F32)


def _sigmoid(x):
    return 1.0 / (1.0 + jnp.exp(-x))


_GELU_C = math.sqrt(2.0 / math.pi)


def _gelu(x):
    return x * (0.5 * (1.0 + jnp.tanh(_GELU_C * (x + 0.044715 * (x * x * x)))))


def _gelu_grad(x):
    t = jnp.tanh(_GELU_C * (x + 0.044715 * (x * x * x)))
    cdf = 0.5 * (1.0 + t)
    return cdf + x * (0.5 * (1.0 - t * t) * _GELU_C * (1.0 + 3.0 * 0.044715 * (x * x)))


def _rms_scale(x):
    return lax.rsqrt(jnp.mean(x * x, axis=-1, keepdims=True) + EPS)


def _rms_bwd(x, rs, g, dy):
    gdy = dy * g
    dx = rs * gdy - x * ((rs * rs * rs) * jnp.mean(gdy * x, axis=-1, keepdims=True))
    return dx, dy * (x * rs)


def _log1p(e):
    u = 1.0 + e
    d = u - 1.0
    return jnp.where(d == 0.0, e, jnp.log(u) * (e / jnp.where(d == 0.0, 1.0, d)))


def _softplus(y):
    return jnp.maximum(y, 0.0) + _log1p(jnp.exp(-jnp.abs(y)))


def _expm1(x):
    u = jnp.exp(x)
    lu = jnp.log(u)
    safe = jnp.where((u == 1.0) | (u == 0.0), 1.0, lu)
    return jnp.where(u == 1.0, x, jnp.where(u == 0.0, -1.0, (u - 1.0) * (x / safe)))


def _row_iota(shape):
    return lax.broadcasted_iota(jnp.int32, shape, 0)


def _lane_iota(shape):
    return lax.broadcasted_iota(jnp.int32, shape, 1)


def _scan_groups_fwd(a, b):
    sub = lax.broadcasted_iota(jnp.int32, a.shape, 1)
    for sh in (1, 2, 4):
        m = sub >= sh
        b = jnp.where(m, a * pltpu.roll(b, sh, 1) + b, b)
        a = jnp.where(m, a * pltpu.roll(a, sh, 1), a)
    return a, b


def _scan_groups_bwd(c, b):
    sub = lax.broadcasted_iota(jnp.int32, c.shape, 1)
    for sh in (1, 2, 4):
        m = sub < 8 - sh
        b = jnp.where(m, b + c * pltpu.roll(b, 8 - sh, 1), b)
        c = jnp.where(m, c * pltpu.roll(c, 8 - sh, 1), c)
    return c, b


def _rope_pair(gc):
    return gc + pltpu.roll(gc, 64, 1)


def _inproj(x, g, w1, tm, comm=None):
    t = x.shape[0]
    widths = (D, 3 * D, Q_LORA, KV_LORA, KR_W)

    def body(x_ref, g_ref, w_ref, xn_ref, rx_ref, g3_ref, cq_ref, ckv_ref, kr_ref):
        xv = x_ref[...]
        xn = (xv * _rms_scale(xv) * g_ref[...]).astype(BF16)
        xn_ref[...] = xn
        col = 0
        for ref, w in zip((rx_ref, g3_ref, cq_ref, ckv_ref, kr_ref), widths):
            for c0 in range(0, w, 512):
                cw = min(512, w - c0)
                ref[:, c0:c0 + cw] = _dot_nt(xn, w_ref[col + c0:col + c0 + cw, :])
            col += w

    return _pcall(
        body, "inproj", (t // tm,),
        [_rows(tm, D), _full(1, D), _full(W1_COLS, D)],
        [_rows(tm, D)] + [_rows(tm, w) for w in widths],
        [_sds((t, D), BF16)] + [_sds((t, w)) for w in widths],
        comm=comm,
    )(x, g, w1)


def _lru_gates(xa, wa_ref, ba, wx_ref, bx, pre_r, pre_i):
    xb = xa.astype(BF16)
    for n in range(RNN_BLOCKS):
        sl = slice(n * RNN_BW, (n + 1) * RNN_BW)
        pre_r[:, sl] = _dot(xb[:, sl], wa_ref[n])
        pre_i[:, sl] = _dot(xb[:, sl], wx_ref[n])
    r = _sigmoid(pre_r[...] + ba)
    i = _sigmoid(pre_i[...] + bx)
    return r, i


def _lru_fwd(rx, conv_w, conv_b, wa, ba, wx, bx, lam, tb, comm=None):
    t = rx.shape[0]
    nb = t // tb

    def body(x_ref, xp_ref, cw_ref, cb_ref, wa_ref, ba_ref, wx_ref, bx_ref, lam_ref, h_ref, xa_ref, hc, tmp, pre_r, pre_i):
        i_blk = pl.program_id(0)

        @pl.when(i_blk == 0)
        def _():
            hc[...] = jnp.zeros_like(hc)

        xv = x_ref[...]
        xp = jnp.where(i_blk > 0, xp_ref[...], 0.0)
        row8 = _row_iota((8, D))
        xa = cb_ref[...] + cw_ref[3:4, :] * xv
        for s in (1, 2, 3):
            xr = pltpu.roll(xv, s, 0)
            tmp[...] = xr
            tmp[0:8, :] = jnp.where(row8 < s, pltpu.roll(xp, s, 0), xr[0:8, :])
            xa = xa + cw_ref[3 - s:4 - s, :] * tmp[...]
        xa_ref[...] = xa
        r, gi = _lru_gates(xa, wa_ref, ba_ref[...], wx_ref, bx_ref[...], pre_r, pre_i)
        la = (-LRU_C * _softplus(-lam_ref[...])) * r
        a = jnp.exp(la)
        b = jnp.sqrt(-_expm1(2.0 * la)) * (gi * xa)
        a3, b3 = _scan_groups_fwd(a.reshape(tb // 8, 8, D), b.reshape(tb // 8, 8, D))
        carry = hc[...]
        for grp in range(tb // 8):
            hg = a3[grp] * carry + b3[grp]
            h_ref[8 * grp:8 * grp + 8, :] = hg
            carry = hg[7:8, :]
        hc[...] = carry

    prev8 = pl.BlockSpec((8, D), lambda i: (jnp.maximum(i * (tb // 8) - 1, 0), 0))
    return _pcall(
        body, "lru_fwd", (nb,),
        [_rows(tb, D), prev8, _full(4, D), _full(1, D), _full(RNN_BLOCKS, RNN_BW, RNN_BW), _full(1, D),
         _full(RNN_BLOCKS, RNN_BW, RNN_BW), _full(1, D), _full(1, D)],
        [_rows(tb, D), _rows(tb, D)],
        [_sds((t, D)), _sds((t, D))],
        scratch=[pltpu.VMEM((1, D), F32), pltpu.VMEM((tb, D), F32), pltpu.VMEM((tb, D), F32), pltpu.VMEM((tb, D), F32)],
        comm=comm,
    )(rx, rx, conv_w, conv_b, wa, ba, wx, bx, lam)


def _mla_proj(cq, ckv, kr, qn, kvn, wq, wkv, rope_c, tm):
    t = cq.shape[0]

    def body(cq_ref, ckv_ref, kr_ref, qn_ref, kvn_ref, wq_ref, wkv_ref, c_ref, q_ref, k_ref, v_ref, cqn_ref, ckvn_ref):
        cqv = cq_ref[...]
        cqn = (cqv * _rms_scale(cqv) * qn_ref[...]).astype(BF16)
        ckvv = ckv_ref[...]
        ckvn = (ckvv * _rms_scale(ckvv) * kvn_ref[...]).astype(BF16)
        cqn_ref[...] = cqn
        ckvn_ref[...] = ckvn
        c = c_ref[...]
        lane = _lane_iota((tm, KR_W))
        kro = jnp.where(lane < 64, _rope_pair(kr_ref[...] * c), 0.0).astype(BF16)
        for h in range(N_HEADS):
            sl = slice(h * HEAD_W, (h + 1) * HEAD_W)
            qh = _dot(cqn, wq_ref[:, sl])
            q_ref[h, :, 0:128] = (qh[:, 0:128] * EXP2_SCALE).astype(BF16)
            q_ref[h, :, 128:256] = (_rope_pair(qh[:, 128:256] * c) * EXP2_SCALE).astype(BF16)
            kvh = _dot(ckvn, wkv_ref[:, sl])
            k_ref[h, :, 0:128] = kvh[:, 0:128].astype(BF16)
            k_ref[h, :, 128:256] = kro
            v_ref[h, :, 0:V_HEAD] = kvh[:, 128:256].astype(BF16)
            v_ref[h, :, V_HEAD:2 * V_HEAD] = jnp.ones((tm, V_HEAD), BF16)

    hb = lambda w: pl.BlockSpec((N_HEADS, tm, w), lambda i: (0, i, 0))
    return _pcall(
        body, "mla_proj", (t // tm,),
        [_rows(tm, Q_LORA), _rows(tm, KV_LORA), _rows(tm, KR_W), _full(1, Q_LORA), _full(1, KV_LORA),
         _full(Q_LORA, N_HEADS * HEAD_W), _full(KV_LORA, N_HEADS * HEAD_W), _rows(tm, KR_W)],
        [hb(HEAD_W), hb(HEAD_W), hb(2 * V_HEAD), _rows(tm, Q_LORA), _rows(tm, KV_LORA)],
        [_sds((N_HEADS, t, HEAD_W), BF16), _sds((N_HEADS, t, HEAD_W), BF16), _sds((N_HEADS, t, 2 * V_HEAD), BF16),
         _sds((t, Q_LORA), BF16), _sds((t, KV_LORA), BF16)],
    )(cq, ckv, kr, qn, kvn, wq, wkv, rope_c)


def _flash_fwd(q, k, v, tq, comm=None):
    t = q.shape[1]
    nq = t // tq

    def body(q_ref, k_ref, v_ref, o_ref, lse_ref, s_even, s_odd):
        qi = pl.program_id(1)
        qv = q_ref[0]

        def scores(ki, buf):
            buf[...] = _dot_nt(qv, k_ref[0, pl.ds(pl.multiple_of(ki * tq, tq), tq), :])

        def softmax_pv(ki, buf, carry, diagonal):
            m, acc = carry
            s = buf[...]
            if diagonal:
                s = jnp.where(_row_iota((tq, tq)) >= _lane_iota((tq, tq)), s, NEG)
            m_new = jnp.maximum(m, jnp.max(s, axis=1, keepdims=True))
            p = jnp.exp2(s - m_new)
            alpha = jnp.exp2(m - m_new)
            acc = alpha * acc + _dot(p.astype(BF16), v_ref[0, pl.ds(pl.multiple_of(ki * tq, tq), tq), :])
            return m_new, acc

        def finish(carry):
            m, acc = carry
            l = acc[:, V_HEAD:2 * V_HEAD]
            o_ref[...] = acc[:, 0:V_HEAD] / l
            lse = m + jnp.log(l) * math.log2(math.e)
            lse_ref[0, 0] = jnp.transpose(lse)[0:8, :]

        def two(i, carry):
            scores(2 * i + 1, s_odd)
            carry = softmax_pv(2 * i, s_even, carry, False)
            scores(2 * i + 2, s_even)
            return softmax_pv(2 * i + 1, s_odd, carry, False)

        init = (jnp.full((tq, 1), -jnp.inf, F32), jnp.zeros((tq, 2 * V_HEAD), F32))
        scores(0, s_even)
        carry = lax.fori_loop(0, qi // 2, two, init)

        @pl.when(qi % 2 == 0)
        def _():
            finish(softmax_pv(qi, s_even, carry, True))

        @pl.when(qi % 2 == 1)
        def _():
            scores(qi, s_odd)
            finish(softmax_pv(qi, s_odd, softmax_pv(qi - 1, s_even, carry, False), True))

    head = lambda w: pl.BlockSpec((1, t, w), lambda h, qi: (h, 0, 0))
    return _pcall(
        body, "flash_fwd", (N_HEADS, nq),
        [pl.BlockSpec((1, tq, HEAD_W), lambda h, qi: (h, qi, 0)), head(HEAD_W), head(2 * V_HEAD)],
        [pl.BlockSpec((tq, V_HEAD), lambda h, qi: (qi, h)), pl.BlockSpec((1, 1, 8, tq), lambda h, qi: (h, qi, 0, 0))],
        [_sds((t, D)), _sds((N_HEADS, nq, 8, tq))],
        scratch=[pltpu.VMEM((tq, tq), F32), pltpu.VMEM((tq, tq), F32)],
        comm=comm,
    )(q, k, v)


def _merge_out(x, h, g3, yb, w_out, tm):
    t = x.shape[0]

    def body(x_ref, h_ref, g3_ref, yb_ref, w_ref, h1_ref, mg_ref):
        ya = h_ref[...] * _gelu(g3_ref[:, 0:D])
        merged = (_sigmoid(g3_ref[:, D:2 * D]) * ya + _sigmoid(g3_ref[:, 2 * D:3 * D]) * yb_ref[...]).astype(BF16)
        mg_ref[...] = merged
        h1_ref[...] = x_ref[...] + _dot(merged, w_ref[...])

    return _pcall(
        body, "merge_out", (t // tm,),
        [_rows(tm, D), _rows(tm, D), _rows(tm, 3 * D), _rows(tm, D), _full(D, D)],
        [_rows(tm, D), _rows(tm, D)],
        [_sds((t, D)), _sds((t, D), BF16)],
    )(x, h, g3, yb, w_out)


def _mlp_up(h1, g, w_up, tm):
    t = h1.shape[0]

    def body(h_ref, g_ref, w_ref, u_ref, n2_ref):
        hv = h_ref[...]
        n2 = (hv * _rms_scale(hv) * g_ref[...]).astype(BF16)
        n2_ref[...] = n2
        for c0 in range(0, D_FF, 512):
            u_ref[:, c0:c0 + 512] = _dot(n2, w_ref[:, c0:c0 + 512])

    return _pcall(
        body, "mlp_up", (t // tm,),
        [_rows(tm, D), _full(1, D), _full(D, D_FF)],
        [_rows(tm, D_FF), _rows(tm, D)],
        [_sds((t, D_FF)), _sds((t, D), BF16)],
    )(h1, g, w_up)


def _mlp_down_loss(u, h1, target, w_down, g, tm):
    t = u.shape[0]

    def body(u_ref, h1_ref, tg_ref, w_ref, g_ref, act_ref, dh2_ref, loss_ref, gnf_ref, lacc):
        i = pl.program_id(0)

        @pl.when(i == 0)
        def _():
            lacc[...] = jnp.zeros_like(lacc)
            gnf_ref[...] = jnp.zeros_like(gnf_ref)

        ru = jnp.maximum(u_ref[...], 0.0)
        act = (ru * ru).astype(BF16)
        act_ref[...] = act
        h2 = h1_ref[...] + _dot(act, w_ref[...])
        rs = _rms_scale(h2)
        gv = g_ref[...]
        err = h2 * rs * gv - tg_ref[...]
        lacc[...] += jnp.sum(err * err, axis=0, keepdims=True)
        dx, dgr = _rms_bwd(h2, rs, gv, err * (1.0 / D))
        dh2_ref[...] = dx
        gnf_ref[...] += jnp.sum(dgr, axis=0, keepdims=True)

        @pl.when(i == pl.num_programs(0) - 1)
        def _():
            loss_ref[...] = jnp.broadcast_to(jnp.sum(lacc[...], axis=1, keepdims=True) * (0.5 / D), (8, 128))

    return _pcall(
        body, "mlp_down_loss", (t // tm,),
        [_rows(tm, D_FF), _rows(tm, D), _rows(tm, D), _full(D_FF, D), _full(1, D)],
        [_rows(tm, D_FF), _rows(tm, D), _full(8, 128), _full(1, D)],
        [_sds((t, D_FF), BF16), _sds((t, D)), _sds((8, 128)), _sds((1, D))],
        scratch=[pltpu.VMEM((1, D), F32)],
    )(u, h1, target, w_down, g)


def _matmul_tn(a, g, name, chunked=None, comm=None):
    t, kdim = a.shape
    ndim = g.shape[1]
    tk = min(kdim, 1024)
    tn = ndim if ndim <= 1024 else 1024
    if chunked == "cols":
        assert tk == kdim and tn == ndim // N_CHIPS
    elif chunked == "rows":
        assert tk == kdim // N_CHIPS and tn == ndim
    tt = min(t, 2048)
    nt = t // tt

    def body(a_ref, g_ref, o_ref):
        @pl.when(pl.program_id(2) == 0)
        def _():
            o_ref[...] = jnp.zeros_like(o_ref)

        o_ref[...] += _dot_tn(a_ref[...].astype(BF16), g_ref[...].astype(BF16)).reshape(o_ref.shape)

    if chunked is not None:
        out_spec = pl.BlockSpec((2, None, tk // 2, tn), lambda i, j, s: (0, i + j, 0, 0))
        out_shape = _sds((2, N_CHIPS, tk // 2, tn))
    else:
        out_spec, out_shape = pl.BlockSpec((tk, tn), lambda i, j, s: (i, j)), _sds((kdim, ndim))
    return _pcall(
        body, name, (kdim // tk, ndim // tn, nt),
        [pl.BlockSpec((tt, tk), lambda i, j, s: (s, i)), pl.BlockSpec((tt, tn), lambda i, j, s: (s, j))],
        out_spec, out_shape, comm=comm,
    )(a, g)


def _mlp_bwd_act(dh2, u, w_down, tm):
    t = u.shape[0]

    def body(d_ref, u_ref, w_ref, du_ref):
        db = d_ref[...].astype(BF16)
        for c0 in range(0, D_FF, 512):
            da = _dot_nt(db, w_ref[c0:c0 + 512, :])
            du_ref[:, c0:c0 + 512] = (da * (2.0 * jnp.maximum(u_ref[:, c0:c0 + 512], 0.0))).astype(BF16)

    return _pcall(
        body, "mlp_bwd_act", (t // tm,),
        [_rows(tm, D), _rows(tm, D_FF), _full(D_FF, D)],
        _rows(tm, D_FF), _sds((t, D_FF), BF16),
    )(dh2, u, w_down)


def _mlp_bwd_in(du, dh2, h1, w_up, g, tm):
    t = du.shape[0]

    def body(du_ref, d_ref, h_ref, w_ref, g_ref, dh1_ref, gacc_ref):
        @pl.when(pl.program_id(0) == 0)
        def _():
            gacc_ref[...] = jnp.zeros_like(gacc_ref)

        dn2 = _dot_nt(du_ref[...], w_ref[...])
        hv = h_ref[...]
        dx, dgr = _rms_bwd(hv, _rms_scale(hv), g_ref[...], dn2)
        dh1_ref[...] = d_ref[...] + dx
        gacc_ref[...] += jnp.sum(dgr, axis=0, keepdims=True)

    return _pcall(
        body, "mlp_bwd_in", (t // tm,),
        [_rows(tm, D_FF), _rows(tm, D), _rows(tm, D), _full(D, D_FF), _full(1, D)],
        [_rows(tm, D), _full(1, D)],
        [_sds((t, D)), _sds((1, D))],
    )(du, dh2, h1, w_up, g)


def _merge_bwd(dh1, w_out, g3, h, yb, merged, tm):
    t = dh1.shape[0]

    def body(d_ref, w_ref, g3_ref, h_ref, yb_ref, mg_ref, dg3_ref, dyb_ref, dl_ref, dh_ref, dwo_ref):
        @pl.when(pl.program_id(0) == 0)
        def _():
            dwo_ref[...] = jnp.zeros_like(dwo_ref)

        db = d_ref[...].astype(BF16)
        dwo_ref[...] += _dot_tn(mg_ref[...], db)
        dm = _dot_nt(db, w_ref[...])
        gv = g3_ref[:, 0:D]
        sa = _sigmoid(g3_ref[:, D:2 * D])
        sb = _sigmoid(g3_ref[:, 2 * D:3 * D])
        gel = _gelu(gv)
        hv = h_ref[...]
        ybv = yb_ref[...]
        dya = dm * sa
        dyb = dm * sb
        dg3_ref[:, 0:D] = (dya * hv * _gelu_grad(gv)).astype(BF16)
        dg3_ref[:, D:2 * D] = (dya * (hv * gel) * (1.0 - sa)).astype(BF16)
        dg3_ref[:, 2 * D:3 * D] = (dyb * ybv * (1.0 - sb)).astype(BF16)
        dh_ref[...] = dya * gel
        dyb_ref[...] = dyb.astype(BF16)
        prod = dyb * ybv
        ones = jnp.ones((8, V_HEAD), F32)
        for hh in range(N_HEADS):
            dl_ref[hh] = lax.dot_general(ones, prod[:, hh * V_HEAD:(hh + 1) * V_HEAD], (((1,), (1,)), ((), ())),
                                         precision=lax.Precision.HIGHEST, preferred_element_type=F32)

    return _pcall(
        body, "merge_bwd", (t // tm,),
        [_rows(tm, D), _full(D, D), _rows(tm, 3 * D), _rows(tm, D), _rows(tm, D), _rows(tm, D)],
        [_rows(tm, 3 * D), _rows(tm, D), pl.BlockSpec((N_HEADS, 8, tm), lambda i: (0, 0, i)), _rows(tm, D), _full(D, D)],
        [_sds((t, 3 * D), BF16), _sds((t, D), BF16), _sds((N_HEADS, 8, t)), _sds((t, D)), _sds((D, D))],
    )(dh1, w_out, g3, h, yb, merged)


def _flash_bwd(q, k, v, do, lse, delta, tq, comm=None):
    t = q.shape[1]
    nq = t // tq

    def body(q_ref, k_ref, v_ref, do_ref, lse_ref, dl_ref, dqt_ref, dk_ref, dv_ref):
        ki = pl.program_id(1)

        @pl.when(ki == 0)
        def _():
            dqt_ref[...] = jnp.zeros_like(dqt_ref)

        kblk, vblk = k_ref[0], v_ref[0]
        kt = jnp.transpose(kblk)
        dk_ref[...] = jnp.zeros_like(dk_ref)
        dv_ref[...] = jnp.zeros_like(dv_ref)

        def block(qi, diagonal):
            rows = pl.ds(pl.multiple_of(qi * tq, tq), tq)
            qv, dov = q_ref[0, rows, :], do_ref[rows, :]
            p = jnp.exp2(_dot_nt(kblk, qv) - lse_ref[0, qi, 0:1, :])
            if diagonal:
                p = jnp.where(_lane_iota((tq, tq)) >= _row_iota((tq, tq)), p, 0.0)
            dv_ref[0] += _dot(p.astype(BF16), dov)
            dp = _dot_nt(vblk, dov)
            ds = (p * (dp - dl_ref[0, qi, 0:1, :]) * math.log(2.0)).astype(BF16)
            dk_ref[0] += _dot(ds, qv)
            dqt_ref[0, qi] += _dot(kt, ds)

        block(ki, True)

        def two(i, carry):
            block(ki + 1 + 2 * i, False)
            block(ki + 2 + 2 * i, False)
            return carry

        def one(qi, carry):
            block(qi, False)
            return carry

        pairs = (nq - 1 - ki) // 2
        lax.fori_loop(0, pairs, two, 0)
        lax.fori_loop(ki + 1 + 2 * pairs, nq, one, 0)

    kv_spec = lambda w: pl.BlockSpec((1, tq, w), lambda h, ki: (h, ki, 0))
    stat = pl.BlockSpec((1, nq, 8, tq), lambda h, ki: (h, 0, 0, 0))
    return _pcall(
        body, "flash_bwd", (N_HEADS, nq),
        [pl.BlockSpec((1, t, HEAD_W), lambda h, ki: (h, 0, 0)), kv_spec(HEAD_W), kv_spec(V_HEAD),
         pl.BlockSpec((t, V_HEAD), lambda h, ki: (0, h)), stat, stat],
        [pl.BlockSpec((1, nq, HEAD_W, tq), lambda h, ki: (h, 0, 0, 0)), kv_spec(HEAD_W), kv_spec(V_HEAD)],
        [_sds((N_HEADS, nq, HEAD_W, tq)), _sds((N_HEADS, t, HEAD_W)), _sds((N_HEADS, t, V_HEAD))],
        comm=comm,
    )(q, k, v, do, lse, delta)


def _mla_bwd(dqt, dk, dv, cqn, ckvn, cq, ckv, rope_c, wq, wkv, qn, kvn, tm):
    t = cq.shape[0]

    def body(dq_ref, dk_ref, dv_ref, cqn_ref, ckvn_ref, cq_ref, ckv_ref, c_ref, wq_ref, wkv_ref, qn_ref, kvn_ref,
             dmla_ref, dwq_ref, dwkv_ref, dqn_ref, dkvn_ref):
        @pl.when(pl.program_id(0) == 0)
        def _():
            dwq_ref[...] = jnp.zeros_like(dwq_ref)
            dwkv_ref[...] = jnp.zeros_like(dwkv_ref)
            dqn_ref[...] = jnp.zeros_like(dqn_ref)
            dkvn_ref[...] = jnp.zeros_like(dkvn_ref)

        c = c_ref[...]
        lane = _lane_iota((tm, KR_W))
        cqn, ckvn = cqn_ref[...], ckvn_ref[...]
        dcqn = jnp.zeros((tm, Q_LORA), F32)
        dckvn = jnp.zeros((tm, KV_LORA), F32)
        dkr = jnp.zeros((tm, KR_W), F32)
        for h in range(N_HEADS):
            sl = slice(h * HEAD_W, (h + 1) * HEAD_W)
            dqh = jnp.transpose(dq_ref[h, 0]) * EXP2_SCALE
            droped = jnp.where(lane < 64, dqh[:, 128:256], 0.0)
            dqp = jnp.concatenate([dqh[:, 0:128], _rope_pair(droped) * c], axis=1).astype(BF16)
            dcqn = dcqn + _dot_nt(dqp, wq_ref[:, sl])
            dwq_ref[:, sl] += _dot_tn(cqn, dqp)
            dkr = dkr + jnp.where(lane < 64, dk_ref[h, :, 128:256], 0.0)
            dkvp = jnp.concatenate([dk_ref[h, :, 0:128], dv_ref[h]], axis=1).astype(BF16)
            dckvn = dckvn + _dot_nt(dkvp, wkv_ref[:, sl])
            dwkv_ref[:, sl] += _dot_tn(ckvn, dkvp)
        cqv, ckvv = cq_ref[...], ckv_ref[...]
        dcq, dgq = _rms_bwd(cqv, _rms_scale(cqv), qn_ref[...], dcqn)
        dckv, dgkv = _rms_bwd(ckvv, _rms_scale(ckvv), kvn_ref[...], dckvn)
        dqn_ref[...] += jnp.sum(dgq, axis=0, keepdims=True)
        dkvn_ref[...] += jnp.sum(dgkv, axis=0, keepdims=True)
        dmla_ref[:, 0:256] = dcq.astype(BF16)
        dmla_ref[:, 256:512] = dckv.astype(BF16)
        dmla_ref[:, 512:640] = (_rope_pair(dkr) * c).astype(BF16)

    hb = lambda w: pl.BlockSpec((N_HEADS, tm, w), lambda i: (0, i, 0))
    wide = N_HEADS * HEAD_W
    per_q = dqt.shape[3] // tm
    dq_spec = pl.BlockSpec((N_HEADS, 1, HEAD_W, tm), lambda i: (0, i // per_q, 0, i % per_q))
    return _pcall(
        body, "mla_bwd", (t // tm,),
        [dq_spec, hb(HEAD_W), hb(V_HEAD), _rows(tm, Q_LORA), _rows(tm, KV_LORA), _rows(tm, Q_LORA), _rows(tm, KV_LORA),
         _rows(tm, KR_W), _full(Q_LORA, wide), _full(KV_LORA, wide), _full(1, Q_LORA), _full(1, KV_LORA)],
        [_rows(tm, 640), _full(Q_LORA, wide), _full(KV_LORA, wide), _full(1, Q_LORA), _full(1, KV_LORA)],
        [_sds((t, 640), BF16), _sds((Q_LORA, wide)), _sds((KV_LORA, wide)), _sds((1, Q_LORA)), _sds((1, KV_LORA))],
    )(dqt, dk, dv, cqn, ckvn, cq, ckv, rope_c, wq, wkv, qn, kvn)


def _lru_bwd(dh, xa, h, rx, conv_w, wa, ba, wx, bx, lam, tb, comm=None):
    t = dh.shape[0]
    nb = t // tb

    def body(dh_ref, xa_ref, h_ref, hp_ref, x_ref, cw_ref, wa_ref, ba_ref, wx_ref, bx_ref, lam_ref,
             drx_ref, dcw_ref, dcb_ref, dwa_ref, dba_ref, dwx_ref, dbx_ref, dlam_ref, gc, dxn, tmp, pre_r, pre_i):
        step = pl.program_id(0)
        first_block = step == nb - 1

        @pl.when(step == 0)
        def _():
            gc[...] = jnp.zeros_like(gc)
            dxn[...] = jnp.zeros_like(dxn)
            for ref in (dcw_ref, dcb_ref, dwa_ref, dba_ref, dwx_ref, dbx_ref, dlam_ref):
                ref[...] = jnp.zeros_like(ref)

        xa = xa_ref[...]
        r, gi = _lru_gates(xa, wa_ref, ba_ref[...], wx_ref, bx_ref[...], pre_r, pre_i)
        lamv = lam_ref[...]
        sp = _softplus(-lamv)
        la = (-LRU_C * sp) * r
        a = jnp.exp(la)
        e2 = _expm1(2.0 * la)
        sq = jnp.sqrt(-e2)
        row = _row_iota((tb, D))
        cf = jnp.where(row == tb - 1, 1.0, pltpu.roll(a, tb - 1, 0))
        c3, b3 = _scan_groups_bwd(cf.reshape(tb // 8, 8, D), dh_ref[...].reshape(tb // 8, 8, D))
        carry = gc[...]
        for grp in reversed(range(tb // 8)):
            dg = b3[grp] + c3[grp] * carry
            pre_r[8 * grp:8 * grp + 8, :] = dg
            carry = dg[0:1, :]
        delta = pre_r[...]
        gc[...] = a[0:1, :] * carry
        hv = h_ref[...]
        hr = pltpu.roll(hv, 1, 0)
        tmp[...] = hr
        tmp[0:1, :] = jnp.where(first_block, 0.0, hp_ref[7:8, :])
        hprev = tmp[...]
        ix = gi * xa
        dla = (delta * hprev) * a - (delta * ix) * ((e2 + 1.0) / sq)
        dlam_ref[...] += jnp.sum(dla * r, axis=0, keepdims=True) * (LRU_C * _sigmoid(-lamv))
        dpr = (dla * (-LRU_C * sp)) * r * (1.0 - r)
        dsq = delta * sq
        dpi = (dsq * xa) * gi * (1.0 - gi)
        dba_ref[...] += jnp.sum(dpr, axis=0, keepdims=True)
        dbx_ref[...] += jnp.sum(dpi, axis=0, keepdims=True)
        pre_r[...] = dpr
        pre_i[...] = dpi
        xb = xa.astype(BF16)
        for n in range(RNN_BLOCKS):
            sl = slice(n * RNN_BW, (n + 1) * RNN_BW)
            dprn = pre_r[:, sl].astype(BF16)
            dpin = pre_i[:, sl].astype(BF16)
            dwa_ref[n] += _dot_tn(xb[:, sl], dprn)
            dwx_ref[n] += _dot_tn(xb[:, sl], dpin)
            tmp[:, sl] = _dot_nt(dprn, wa_ref[n]) + _dot_nt(dpin, wx_ref[n])
        dxa = dsq * gi + tmp[...]
        dcb_ref[...] += jnp.sum(dxa, axis=0, keepdims=True)
        xv = x_ref[...]
        drx = cw_ref[3:4, :] * dxa
        dcw_ref[3:4, :] += jnp.sum(dxa * xv, axis=0, keepdims=True)
        row8 = _row_iota((8, D))
        nxt = dxn[...]
        for s in (1, 2, 3):
            dr_ = pltpu.roll(dxa, tb - s, 0)
            tmp[...] = dr_
            tmp[tb - 8:tb, :] = jnp.where(row8 >= 8 - s, pltpu.roll(nxt, 8 - s, 0), dr_[tb - 8:tb, :])
            dxs = tmp[...]
            drx = drx + cw_ref[3 - s:4 - s, :] * dxs
            dcw_ref[3 - s:4 - s, :] += jnp.sum(dxs * xv, axis=0, keepdims=True)
        drx_ref[...] = drx.astype(BF16)
        dxn[...] = dxa[0:8, :]

    rev = pl.BlockSpec((tb, D), lambda i: (nb - 1 - i, 0))
    prev8 = pl.BlockSpec((8, D), lambda i: (jnp.maximum((nb - 1 - i) * (tb // 8) - 1, 0), 0))
    wblk = _full(RNN_BLOCKS, RNN_BW, RNN_BW)
    return _pcall(
        body, "lru_bwd", (nb,),
        [rev, rev, rev, prev8, rev, _full(4, D), wblk, _full(1, D), wblk, _full(1, D), _full(1, D)],
        [rev, _full(4, D), _full(1, D), wblk, _full(1, D), wblk, _full(1, D), _full(1, D)],
        [_sds((t, D), BF16), _sds((4, D)), _sds((1, D)), _sds((RNN_BLOCKS, RNN_BW, RNN_BW)), _sds((1, D)),
         _sds((RNN_BLOCKS, RNN_BW, RNN_BW)), _sds((1, D)), _sds((1, D))],
        scratch=[pltpu.VMEM((1, D), F32), pltpu.VMEM((8, D), F32), pltpu.VMEM((tb, D), F32), pltpu.VMEM((tb, D), F32),
                 pltpu.VMEM((tb, D), F32)],
        comm=comm,
    )(dh, xa, h, h, rx, conv_w, wa, ba, wx, bx, lam)


def _inproj_bwd(x, dh1, drx, dg3, dmla, w1, g, tm, comm=None):
    t = x.shape[0]

    def body(x_ref, d_ref, drx_ref, dg3_ref, dmla_ref, w_ref, g_ref, dx_ref, gacc_ref):
        @pl.when(pl.program_id(0) == 0)
        def _():
            gacc_ref[...] = jnp.zeros_like(gacc_ref)

        dxn = _dot(drx_ref[...], w_ref[0:D, :])
        for c0 in range(0, 3 * D, D):
            dxn = dxn + _dot(dg3_ref[:, c0:c0 + D], w_ref[D + c0:2 * D + c0, :])
        dxn = dxn + _dot(dmla_ref[...], w_ref[4 * D:W1_COLS, :])
        xv = x_ref[...]
        dx, dgr = _rms_bwd(xv, _rms_scale(xv), g_ref[...], dxn)
        dx_ref[...] = d_ref[...] + dx
        gacc_ref[...] += jnp.sum(dgr, axis=0, keepdims=True)

    return _pcall(
        body, "inproj_bwd", (t // tm,),
        [_rows(tm, D), _rows(tm, D), _rows(tm, D), _rows(tm, 3 * D), _rows(tm, 640), _full(W1_COLS, D), _full(1, D)],
        [_rows(tm, D), _full(1, D)],
        [_sds((t, D)), _sds((1, D))],
        comm=comm,
    )(x, dh1, drx, dg3, dmla, w1, g)


def _pcall_indexed(body, name, index, grid, in_specs, out_specs, out_shape):
    call = pl.pallas_call(
        body, name=name, out_shape=out_shape,
        grid_spec=pltpu.PrefetchScalarGridSpec(num_scalar_prefetch=1, grid=grid, in_specs=in_specs, out_specs=out_specs),
        compiler_params=pltpu.CompilerParams(dimension_semantics=("arbitrary",) * len(grid), vmem_limit_bytes=V7X_VMEM_LIMIT))
    return lambda *operands: call(index, *operands)


V7X_STREAM_BLOCK_BYTES = 5 << 19


def _stream_tile(rows, cols):
    return _row_tile(rows, cap=max(256, V7X_STREAM_BLOCK_BYTES // (4 * cols)), mult=16)


def _pair_sum(halves, theirs, core, out_dtype, name):
    _, rows, cols = halves.shape
    tm = _stream_tile(rows, cols)

    def body(c_ref, a_ref, b_ref, o_ref):
        o_ref[...] = (a_ref[0] + b_ref[...]).astype(out_dtype)

    plain = pl.BlockSpec((tm, cols), lambda i, c: (i, 0))
    return _pcall_indexed(body, name, core, (rows // tm,),
                          [pl.BlockSpec((1, tm, cols), lambda i, c: (c[0], i, 0)), plain], plain,
                          _sds((rows, cols), out_dtype))(halves, theirs)


def _chip_sum(parts, recv, chip, name):
    _, rows, cols = parts.shape
    tm = _stream_tile(rows, cols)

    def body(c_ref, a_ref, r_ref, o_ref):
        o_ref[...] = ((a_ref[0].astype(F32) + r_ref[0].astype(F32)) + r_ref[1].astype(F32)) + r_ref[2].astype(F32)

    return _pcall_indexed(body, name, chip, (rows // tm,),
                          [pl.BlockSpec((1, tm, cols), lambda i, c: (c[0], i, 0)),
                           pl.BlockSpec((N_CHIPS - 1, tm, cols), lambda i, c: (0, i, 0))],
                          pl.BlockSpec((tm, cols), lambda i, c: (i, 0)), _sds((rows, cols)))(parts, recv)


def _adam_math(w, gv, m, v):
    mn = ADAM_B1 * m + (1.0 - ADAM_B1) * gv
    vn = ADAM_B2 * v + (1.0 - ADAM_B2) * (gv * gv)
    m_hat = mn / (1.0 - ADAM_B1 ** ADAM_STEP)
    v_hat = vn / (1.0 - ADAM_B2 ** ADAM_STEP)
    return -ADAM_LR * (m_hat / (jnp.sqrt(v_hat) + ADAM_EPS) + ADAM_WD * w), mn, vn


def _adamw(w, g, m, v, name):
    rows, cols = w.shape
    tm = _row_tile(rows)

    def body(w_ref, g_ref, m_ref, v_ref, d_ref, mo_ref, vo_ref):
        d_ref[...], mo_ref[...], vo_ref[...] = _adam_math(w_ref[...], g_ref[...], m_ref[...], v_ref[...])

    spec = _rows(tm, cols)
    return _pcall(body, name, (rows // tm,), [spec] * 4, [spec] * 3, [_sds((rows, cols))] * 3)(w, g, m, v)


def _adamw_halves(w, mine, theirs, m, v, core, name, by_cols=False):
    rows, cols = w.shape
    if by_cols:
        tm, tc = mine.shape[0] // 2, cols // 2
        grid = (rows // tm, 2)
        full = pl.BlockSpec((tm, tc), lambda i, j, c: (i, j))
        half = pl.BlockSpec((tm, tc), lambda i, j, c: (i, 0))
    else:
        tm = _row_tile(rows // 2)
        nh = rows // 2 // tm
        grid = (rows // tm,)
        full = pl.BlockSpec((tm, cols), lambda i, c: (i, 0))
        half = pl.BlockSpec((tm, cols), lambda i, c: (i % nh, 0))

    def body(c_ref, w_ref, a_ref, b_ref, m_ref, v_ref, g_ref, d_ref, mo_ref, vo_ref):
        which = pl.program_id(1) if by_cols else pl.program_id(0) // nh
        gv = jnp.where(which == c_ref[0], a_ref[...], b_ref[...])
        g_ref[...] = gv
        d_ref[...], mo_ref[...], vo_ref[...] = _adam_math(w_ref[...], gv, m_ref[...], v_ref[...])

    return _pcall_indexed(body, name, core, grid, [full, half, half, full, full], [full] * 4,
                          [_sds((rows, cols))] * 4)(w, mine, theirs, m, v)


REL_SIBLING = (0, 0, 1)
REL_CHIPS = ((1, 0, 0), (0, 1, 0), (1, 1, 0))


V7X_DMA_CHUNK_BYTES = 1 << 20


def _split_copy(src, dst, shape, itemsize):
    nbytes = math.prod(shape) * itemsize
    if nbytes <= V7X_DMA_CHUNK_BYTES or len(shape) < 2:
        return [(src, dst)]
    if len(shape) > 2:
        out = []
        for k in range(shape[0]):
            out += _split_copy(src.at[k], dst.at[k], shape[1:], itemsize)
        return out
    rows = shape[0]
    sub = 8 * (4 // itemsize)
    parts = max(1, min(-(-nbytes // V7X_DMA_CHUNK_BYTES), rows // sub))
    while rows % parts or (rows // parts) % sub:
        parts -= 1
    step = rows // parts
    return [(src.at[pl.ds(k * step, step)], dst.at[pl.ds(k * step, step)]) for k in range(parts)]


def _mesh_pos():
    return (lax.axis_index("x"), lax.axis_index("y"), lax.axis_index("c"))


def _make_copy(i, op, sems, pos, src=None, dst=None):
    rel = op[0]
    src, dst = (op[1], op[2]) if src is None else (src, dst)
    send_sems, recv_sems = sems
    if rel is None:
        return pltpu.make_async_copy(src, dst, send_sems.at[i])
    peer = tuple((p + r) % 2 for p, r in zip(pos, rel))
    return pltpu.make_async_remote_copy(src_ref=src, dst_ref=dst, send_sem=send_sems.at[i], recv_sem=recv_sems.at[i],
                                        device_id=peer, device_id_type=MESH_ID)


def _start_copies(ops, sems, pos, base=0):
    for i, op in enumerate(ops):
        for s_piece, d_piece in _split_copy(op[1], op[2], op[1].shape, jnp.dtype(op[1].dtype).itemsize):
            _make_copy(base + i, op, sems, pos, s_piece, d_piece).start()


def _wait_copies(ops, sems, pos, base=0):
    for i, op in enumerate(ops):
        _make_copy(base + i, op, sems, pos).wait()


def _comm(name, ins, out_shapes, n_ops, ops_fn):
    n_in, n_out = len(ins), len(out_shapes)

    def body(*refs):
        in_refs, out_refs = refs[:n_in], refs[n_in:n_in + n_out]
        sems = refs[n_in + n_out:]
        pos = _mesh_pos()
        ops = ops_fn(in_refs, out_refs, pos)
        assert len(ops) == n_ops
        _start_copies(ops, sems, pos)
        _wait_copies(ops, sems, pos)

    hbm = pl.BlockSpec(memory_space=pl.ANY)
    return pl.pallas_call(
        body, name=name, in_specs=[hbm] * n_in, out_specs=[hbm] * n_out, out_shape=list(out_shapes),
        scratch_shapes=[pltpu.SemaphoreType.DMA((n_ops,)), pltpu.SemaphoreType.DMA((n_ops,))],
    )(*ins)


def _chip_of(pos, rel=(0, 0, 0)):
    return 2 * ((pos[0] + rel[0]) % 2) + (pos[1] + rel[1]) % 2


def _gather_chips(shards, chip, name):
    def ops_fn(in_refs, out_refs, pos):
        me = _chip_of(pos)
        return [(rel, src, dst.at[me]) for src, dst in zip(in_refs, out_refs) for rel in REL_CHIPS]

    outs = _comm(name, shards, [_sds((N_CHIPS,) + s.shape, s.dtype) for s in shards], 3 * len(shards), ops_fn)
    return [lax.dynamic_update_index_in_dim(o, s, chip, 0) for o, s in zip(outs, shards)]


def _halved_gather_ops(pos, srcs, dsts, whole):
    me, c = _chip_of(pos), pos[2]
    ici, d2d = [], []
    for a, (src, dst) in enumerate(zip(srcs, dsts)):
        for rel in REL_CHIPS:
            if a in whole:
                ici.append((rel, src, dst.at[me]))
            else:
                ici.append((rel, src.at[c], dst.at[me, c]))
                arrived = dst.at[_chip_of(pos, rel), c]
                d2d.append((REL_SIBLING, arrived, arrived))
    return ici, d2d


def _halved_gather_comm(shards):
    srcs = [s.reshape(2, s.shape[0] // 2, s.shape[1]) for s in shards]

    def ops_fn(in_refs, out_refs, pos):
        return _halved_gather_ops(pos, in_refs, out_refs, ())

    def finish(outs, chip):
        return [lax.dynamic_update_index_in_dim(o, s, chip, 0).reshape((N_CHIPS,) + sh.shape) for o, s, sh in zip(outs, srcs, shards)]

    return (srcs, [_sds((N_CHIPS,) + s.shape, s.dtype) for s in srcs], 6 * len(shards), ops_fn), finish


def _gather_halved(shards, whole, chip, name):
    srcs = [s if a in whole or s.ndim == 3 else s.reshape(2, s.shape[0] // 2, s.shape[1]) for a, s in enumerate(shards)]
    n_sh = len(shards)
    n_ici, n_d2d = 3 * n_sh, 3 * (n_sh - len(whole))

    def body(*refs):
        in_refs, out_refs, sems = refs[:n_sh], refs[n_sh:2 * n_sh], refs[2 * n_sh:]
        pos = _mesh_pos()
        ici, d2d = _halved_gather_ops(pos, in_refs, out_refs, whole)
        _start_copies(ici, sems, pos)
        _wait_copies(ici, sems, pos)
        _start_copies(d2d, sems, pos, base=n_ici)
        _wait_copies(d2d, sems, pos, base=n_ici)

    hbm = pl.BlockSpec(memory_space=pl.ANY)
    outs = pl.pallas_call(
        body, name=name, in_specs=[hbm] * n_sh, out_specs=[hbm] * n_sh,
        out_shape=[_sds((N_CHIPS,) + s.shape, s.dtype) for s in srcs],
        scratch_shapes=[pltpu.SemaphoreType.DMA((n_ici + n_d2d,)), pltpu.SemaphoreType.DMA((n_ici + n_d2d,))],
    )(*srcs)
    return [lax.dynamic_update_index_in_dim(o, s, chip, 0).reshape((N_CHIPS,) + sh.shape)
            for o, s, sh in zip(outs, srcs, shards)]


def _split_comm(gs):
    def ops_fn(in_refs, out_refs, pos):
        return [(REL_SIBLING, src.at[1 - pos[2]], dst) for src, dst in zip(in_refs, out_refs)]

    return gs, [_sds(g.shape[1:], g.dtype) for g in gs], len(gs), ops_fn


def _exchange_comm(ps):
    def ops_fn(in_refs, out_refs, pos):
        return [(rel, src.at[_chip_of(pos, rel)], dst.at[j])
                for src, dst in zip(in_refs, out_refs) for j, rel in enumerate(REL_CHIPS)]

    return ps, [_sds((N_CHIPS - 1,) + p.shape[1:], p.dtype) for p in ps], 3 * len(ps), ops_fn


def _join_comm(hs):
    def ops_fn(in_refs, out_refs, pos):
        return [(REL_SIBLING, src, dst) for src, dst in zip(in_refs, out_refs)]

    return hs, [_sds(h.shape, h.dtype) for h in hs], len(hs), ops_fn


def _run_comm(name, comm):
    ins, shapes, n_ops, ops_fn = comm
    return _comm(name, ins, shapes, n_ops, ops_fn)


def _rope_table(t):
    pos = np.arange(t, dtype=np.float32)
    inv_freq = (np.float32(1.0) / (np.float32(ROPE_THETA) ** (np.arange(0, QK_ROPE, 2, dtype=np.float32) / np.float32(QK_ROPE)))).astype(np.float32)
    ang = (pos[:, None] * inv_freq[None, :]).astype(np.float32)
    return np.concatenate([np.cos(ang), np.cos(ang), np.sin(ang), np.sin(ang)], axis=-1).astype(np.float32)


def _rot_cols(w):
    return jnp.concatenate([-w[..., 32:], w[..., :32]], axis=-1)


def _unrot_cols(dw):
    return jnp.concatenate([dw[..., 32:], -dw[..., :32]], axis=-1)


IN_OFFS = (0, 1024, 2048, 2304, 2560, 2624, 3648, 4672)


def _w1t_from_w_in_t(wt):
    seg = [wt[IN_OFFS[i]:IN_OFFS[i + 1]] for i in range(7)]
    rnn_x, rnn_gate, cq, ckv, kr, ga, gb = seg
    return jnp.concatenate([rnn_x, rnn_gate, ga, gb, cq, ckv, kr, _rot_cols(kr.T).T], axis=0)


def _w_in_t_grad_from_parts(d_rx, d_g3, d_mla):
    kr = d_mla[512:576] + _unrot_cols(d_mla[576:640].T).T
    return jnp.concatenate([d_rx, d_g3[0:D], d_mla[0:512], kr, d_g3[D:3 * D]], axis=0)


def _wq_from_w_uq(w):
    w3 = w.reshape(Q_LORA, N_HEADS, QK_NOPE + QK_ROPE)
    rope = w3[..., QK_NOPE:]
    return jnp.concatenate([w3[..., :QK_NOPE], rope, _rot_cols(rope)], axis=-1).reshape(Q_LORA, N_HEADS * HEAD_W)


def _w_uq_grad_from_wq(dw):
    d3 = dw.reshape(Q_LORA, N_HEADS, HEAD_W)
    rope = d3[..., 128:192] + _unrot_cols(d3[..., 192:256])
    return jnp.concatenate([d3[..., :128], rope], axis=-1).reshape(Q_LORA, N_HEADS * (QK_NOPE + QK_ROPE))


def _cols_from_chunks(g):
    return g.transpose(1, 0, 2).reshape(g.shape[1], N_CHIPS * g.shape[2])


def _halves_of_col_chunks(dw):
    r, c4 = dw.shape
    return dw.reshape(2, r // 2, N_CHIPS, c4 // N_CHIPS).transpose(0, 2, 1, 3)


def _halves_of_row_chunks(dw):
    r4, c = dw.shape
    return dw.reshape(N_CHIPS, 2, r4 // (2 * N_CHIPS), c).transpose(1, 0, 2, 3)


def kernel(x, norm_mix, w_in, conv_w, conv_b, lru_wa, lru_ba, lru_wx, lru_bx, lru_lambda, q_norm, w_uq, kv_norm, w_ukv, w_out, norm_mlp, w_up, w_down, norm_final, loss_target, m_norm_mix, m_w_in, m_conv_w, m_conv_b, m_lru_wa, m_lru_ba, m_lru_wx, m_lru_bx, m_lru_lambda, m_q_norm, m_w_uq, m_kv_norm, m_w_ukv, m_w_out, m_norm_mlp, m_w_up, m_w_down, m_norm_final, v_norm_mix, v_w_in, v_conv_w, v_conv_b, v_lru_wa, v_lru_ba, v_lru_wx, v_lru_bx, v_lru_lambda, v_q_norm, v_w_uq, v_kv_norm, v_w_ukv, v_w_out, v_norm_mlp, v_w_up, v_w_down, v_norm_final):
    t = x.shape[1]
    tm = min(512, t)
    tb = min(256, t)
    tq = min(512, max(tm, t // 4))
    x2 = x[0]
    target = loss_target[0]
    chip = 2 * lax.axis_index("x") + lax.axis_index("y")
    core = lax.axis_index("c")
    chip_ix, core_ix = chip.reshape(1).astype(jnp.int32), core.reshape(1).astype(jnp.int32)
    row = lambda p: p.reshape(1, -1)

    big_shards = (w_in, w_uq, w_ukv, w_out, w_up, w_down)
    w_in_t = w_in.T
    col_halves = lambda z: jnp.stack([z[:, :z.shape[1] // 2], z[:, z.shape[1] // 2:]])
    w_in_g, conv_w_g = _gather_halved([col_halves(w_in_t.astype(BF16)), conv_w], (1,), chip, "weight_gather_first")
    w1 = _w1t_from_w_in_t(jnp.concatenate([w_in_g[:, 0], w_in_g[:, 1]], axis=-1).reshape(IN_OFFS[-1], D))
    conv_w_f = _cols_from_chunks(conv_w_g)
    wa_b, wx_b = lru_wa.astype(BF16), lru_wx.astype(BF16)

    rope_c = jnp.asarray(_rope_table(t))

    comm_a, finish_a = _halved_gather_comm([w.astype(BF16) for w in (w_uq, w_ukv, w_out)])
    (xn, rx, g3, cq, ckv, kr), gathered = _inproj(x2, row(norm_mix), w1, tm, comm=comm_a)
    g_uq, g_ukv, g_out = finish_a(gathered, chip)
    wq = _wq_from_w_uq(_cols_from_chunks(g_uq))
    wkv = _cols_from_chunks(g_ukv)
    w_out_f = g_out.reshape(D, D)
    comm_b, finish_b = _halved_gather_comm([w_up.astype(BF16)])
    (h, xa), gathered = _lru_fwd(rx, conv_w_f, row(conv_b), wa_b, row(lru_ba), wx_b, row(lru_bx), row(lru_lambda), tb, comm=comm_b)
    w_up_f = _cols_from_chunks(finish_b(gathered, chip)[0])
    q, k, v, cqn, ckvn = _mla_proj(cq, ckv, kr, row(q_norm), row(kv_norm), wq, wkv, rope_c, tm)
    nq = t // tq
    comm_c, finish_c = _halved_gather_comm([w_down.astype(BF16)])
    (yb, lse), gathered = _flash_fwd(q, k, v, tq, comm=comm_c)
    w_down_f = finish_c(gathered, chip)[0].reshape(D_FF, D)
    h1, merged = _merge_out(x2, h, g3, yb, w_out_f, tm)
    u, n2 = _mlp_up(h1, row(norm_mlp), w_up_f, tm)
    act, dh2, loss_blk, g_norm_final = _mlp_down_loss(u, h1, target, w_down_f, row(norm_final), tm)

    g_w_down = _matmul_tn(act, dh2, "grad_w_down", "rows")
    du = _mlp_bwd_act(dh2, u, w_down_f, tm)
    dh1, g_norm_mlp = _mlp_bwd_in(du, dh2, h1, w_up_f, row(norm_mlp), tm)
    g_w_up = _matmul_tn(n2, du, "grad_w_up", "cols")
    dg3, dyb, delta, dh, g_w_out = _merge_bwd(dh1, w_out_f, g3, h, yb, merged, tm)
    delta = delta.reshape(N_HEADS, 8, nq, tq).swapaxes(1, 2)
    def pair_sums(hvs, theirs, dtypes, tag):
        return [_pair_sum(hv.reshape(2, -1, hv.shape[-1]), r.reshape(-1, r.shape[-1]), core_ix, dt, f"grad_pair_sum_{tag}{a}").reshape(r.shape)
                for a, (hv, r, dt) in enumerate(zip(hvs, theirs, dtypes))]

    def chip_sums(parts, received, tag):
        return [_chip_sum(p, r, chip_ix, f"grad_chip_sum_{tag}{a}") for a, (p, r) in enumerate(zip(parts, received))]

    early = [_halves_of_row_chunks(g_w_out), g_w_up, g_w_down]
    (dq, dk, dv), early_theirs = _flash_bwd(q, k, v, dyb, lse, delta, tq, comm=_split_comm(early))
    early_parts = pair_sums(early, early_theirs, [BF16] * 3, "early")
    dmla, g_wq, g_wkv, g_q_norm, g_kv_norm = _mla_bwd(dq, dk, dv, cqn, ckvn, cq, ckv, rope_c, wq, wkv, row(q_norm), row(kv_norm), tm)
    (drx, g_conv_w, g_conv_b, g_wa, g_ba, g_wx, g_bx, g_lam), early_received = _lru_bwd(
        dh, xa, h, rx, conv_w_f, wa_b, row(lru_ba), wx_b, row(lru_bx), row(lru_lambda), tb, comm=_exchange_comm(early_parts))
    early_reduced = chip_sums(early_parts, early_received, "early")
    grad_x, g_norm_mix = _inproj_bwd(x2, dh1, drx, dg3, dmla, w1, row(norm_mix), tm)
    g_w_in_gates, early_sibling = _matmul_tn(dg3, xn, "grad_w_in_gates", comm=_join_comm(early_reduced))
    g_w_in_t = _w_in_t_grad_from_parts(_matmul_tn(drx, xn, "grad_w_in_rx"), g_w_in_gates, _matmul_tn(dmla, xn, "grad_w_in_mla"))
    g_w_uq = _w_uq_grad_from_wq(g_wq)

    smalls = (g_norm_mix, g_conv_b, g_wa, g_ba, g_wx, g_bx, g_lam, g_q_norm, g_kv_norm, g_norm_mlp, g_norm_final, g_conv_w)
    s_flat = jnp.concatenate([s.reshape(-1) for s in smalls] + [loss_blk[0, 0:1], jnp.zeros((S_LEN - N_SMALL - CONVW_SIZE - 1,), F32)])
    late = [col_halves(g_w_in_t).reshape(2, N_CHIPS, IN_OFFS[-1] // N_CHIPS, D // 2), _halves_of_col_chunks(g_w_uq), _halves_of_col_chunks(g_wkv),
            s_flat.reshape(N_CHIPS, 2, S_ROWS_HALF, 128).transpose(1, 0, 2, 3)]
    late_theirs = _run_comm("grad_sibling_split", _split_comm(late))
    late_parts = pair_sums(late, late_theirs, [BF16] * 3 + [F32], "late")
    late_reduced = chip_sums(late_parts, _run_comm("grad_chip_exchange", _exchange_comm(late_parts)), "late")
    late_sibling = _run_comm("grad_sibling_join", _join_comm(late_reduced))
    reduced = late_reduced[:3] + early_reduced
    reduced_sibling = list(late_sibling[:3]) + list(early_sibling)
    s_mine, s_theirs = late_reduced[3], late_sibling[3]
    s_chunk = jnp.where(core == 0, jnp.concatenate([s_mine, s_theirs]), jnp.concatenate([s_theirs, s_mine]))
    s_all = _gather_chips([s_chunk], chip, "small_grad_gather")[0].reshape(-1)

    small_grads = []
    off = 0
    for shp, n in zip(SMALL_SHAPES, SMALL_SIZES):
        small_grads.append(s_all[off:off + n].reshape(shp))
        off += n
    g_conv_w_mine = lax.dynamic_slice_in_dim(s_all[off:off + CONVW_SIZE].reshape(4, D), chip * (D // N_CHIPS), D // N_CHIPS, axis=1)
    loss = s_all[off + CONVW_SIZE]

    big_m = (m_w_in, m_w_uq, m_w_ukv, m_w_out, m_w_up, m_w_down)
    big_v = (v_w_in, v_w_uq, v_w_ukv, v_w_out, v_w_up, v_w_down)
    big_names = ("w_in", "w_uq", "w_ukv", "w_out", "w_up", "w_down")
    big_upd = [_adamw_halves(w, gm, gt, m, v, core_ix, "adamw_" + n)
               for w, gm, gt, m, v, n in zip(big_shards[1:], reduced[1:], reduced_sibling[1:], big_m[1:], big_v[1:], big_names[1:])]
    w_in_upd = _adamw_halves(w_in_t, reduced[0], reduced_sibling[0], m_w_in.T, v_w_in.T, core_ix, "adamw_w_in", by_cols=True)
    big_upd = [[u.T for u in w_in_upd]] + big_upd

    small_w = (norm_mix, conv_b, lru_wa, lru_ba, lru_wx, lru_bx, lru_lambda, q_norm, kv_norm, norm_mlp, norm_final)
    small_m = (m_norm_mix, m_conv_b, m_lru_wa, m_lru_ba, m_lru_wx, m_lru_bx, m_lru_lambda, m_q_norm, m_kv_norm, m_norm_mlp, m_norm_final)
    small_v = (v_norm_mix, v_conv_b, v_lru_wa, v_lru_ba, v_lru_wx, v_lru_bx, v_lru_lambda, v_q_norm, v_kv_norm, v_norm_mlp, v_norm_final)

    def pack(items, last, fill):
        flat = jnp.concatenate([i.reshape(-1) for i in items] + [last.reshape(-1)])
        return jnp.concatenate([flat, jnp.full((PACK_ROWS * 128 - flat.shape[0],), fill, F32)]).reshape(PACK_ROWS, 128)

    packed = _adamw(pack(small_w, conv_w, 0.0), pack(small_grads, g_conv_w_mine, 0.0), pack(small_m, m_conv_w, 0.0),
                    pack(small_v, v_conv_w, 1.0), "adamw_small")

    def unpack(p):
        flat = p.reshape(-1)
        outs, o = [], 0
        for shp, n in zip(SMALL_SHAPES, SMALL_SIZES):
            outs.append(flat[o:o + n].reshape(shp))
            o += n
        return outs, flat[o:o + CONVW_SIZE // N_CHIPS].reshape(4, D // N_CHIPS)

    order = ("norm_mix", "w_in", "conv_w", "conv_b", "lru_wa", "lru_ba", "lru_wx", "lru_bx", "lru_lambda", "q_norm", "w_uq", "kv_norm",
             "w_ukv", "w_out", "norm_mlp", "w_up", "w_down", "norm_final")

    def assemble(small_list, conv_w_item, big_list):
        table = dict(zip(SMALL_NAMES, small_list))
        table["conv_w"] = conv_w_item
        table.update(zip(big_names, big_list))
        return [table[n] for n in order]

    outs = [loss, grad_x.reshape(1, t, D)]
    outs += assemble(small_grads, g_conv_w_mine, [b[0] for b in big_upd])
    for j in range(3):
        sm, cw = unpack(packed[j])
        outs += assemble(sm, cw, [b[j + 1] for b in big_upd])
    return tuple(outs)
```

```python
import functools
import math

import jax
import jax.numpy as jnp
import numpy as np
from jax import lax
from jax.experimental import pallas as pl
from jax.experimental.pallas import tpu as pltpu

F32 = jnp.float32
BF16 = jnp.bfloat16

D = 1024
N_HEADS = 8
QK_NOPE = 128
QK_ROPE = 64
V_HEAD = 128
Q_LORA = 256
KV_LORA = 256
D_FF = 4096
RNN_BLOCKS = 8
RNN_BW = 128
LRU_C = 8.0
EPS = 1e-6
ROPE_THETA = 10000.0
HEAD_W = 256
KR_W = 128
W1_COLS = 4 * D + Q_LORA + KV_LORA + KR_W
SM_SCALE = (QK_NOPE + QK_ROPE) ** -0.5
EXP2_SCALE = SM_SCALE * math.log2(math.e)
NEG = float(jnp.finfo(jnp.float32).min)

ADAM_LR = 0.001
ADAM_B1 = 0.9
ADAM_B2 = 0.999
ADAM_EPS = 1e-08
ADAM_WD = 0.01
ADAM_STEP = 10

N_CHIPS = 4
V7X_VMEM_LIMIT = 56 * 1024 * 1024
MESH_ID = pl.DeviceIdType.MESH

SMALL_NAMES = ("norm_mix", "conv_b", "lru_wa", "lru_ba", "lru_wx", "lru_bx", "lru_lambda", "q_norm", "kv_norm", "norm_mlp", "norm_final")
SMALL_SHAPES = ((D,), (D,), (RNN_BLOCKS, RNN_BW, RNN_BW), (RNN_BLOCKS, RNN_BW), (RNN_BLOCKS, RNN_BW, RNN_BW), (RNN_BLOCKS, RNN_BW), (D,),
                (Q_LORA,), (KV_LORA,), (D,), (D,))
SMALL_SIZES = tuple(math.prod(s) for s in SMALL_SHAPES)
N_SMALL = sum(SMALL_SIZES)
CONVW_SIZE = 4 * D
S_LEN = -(-(N_SMALL + CONVW_SIZE) // 8192) * 8192
S_ROWS_HALF = S_LEN // (N_CHIPS * 2 * 128)
PACK_ROWS = -(-(N_SMALL + CONVW_SIZE // N_CHIPS) // (256 * 128)) * 256


def _pcall(body, name, grid, in_specs, out_specs, out_shape, scratch=(), comm=None):
    params = pltpu.CompilerParams(dimension_semantics=("arbitrary",) * len(grid), vmem_limit_bytes=V7X_VMEM_LIMIT)
    if comm is None:
        return pl.pallas_call(body, name=name, grid=grid, in_specs=in_specs, out_specs=out_specs, out_shape=out_shape,
                              scratch_shapes=list(scratch), compiler_params=params)
    c_ins, c_shapes, n_ops, ops_fn = comm
    single = not isinstance(out_specs, (list, tuple))
    out_specs, out_shape = ([out_specs], [out_shape]) if single else (list(out_specs), list(out_shape))
    n_in, n_out, n_sc, n_ci, n_co = len(in_specs), len(out_specs), len(scratch), len(c_ins), len(c_shapes)

    def wrapped(*refs):
        ins, refs = refs[:n_in], refs[n_in:]
        c_in_refs, refs = refs[:n_ci], refs[n_ci:]
        outs, refs = refs[:n_out], refs[n_out:]
        c_out_refs, refs = refs[:n_co], refs[n_co:]
        own_scratch, sems = refs[:n_sc], refs[n_sc:]
        pos = _mesh_pos()
        ops = ops_fn(c_in_refs, c_out_refs, pos)
        ops, then = ops if isinstance(ops, tuple) else (ops, [])
        assert len(ops) + len(then) == n_ops
        first, last = True, True
        for d, n in enumerate(grid):
            first = first & (pl.program_id(d) == 0)
            last = last & (pl.program_id(d) == n - 1)

        @pl.when(first)
        def _():
            _start_copies(ops, sems, pos)

        body(*ins, *outs, *own_scratch)

        @pl.when(last)
        def _():
            _wait_copies(ops, sems, pos)
            _start_copies(then, sems, pos, base=len(ops))
            _wait_copies(then, sems, pos, base=len(ops))

    hbm = pl.BlockSpec(memory_space=pl.ANY)
    call = pl.pallas_call(
        wrapped, name=name, grid=grid, in_specs=list(in_specs) + [hbm] * n_ci, out_specs=out_specs + [hbm] * n_co,
        out_shape=out_shape + list(c_shapes),
        scratch_shapes=list(scratch) + [pltpu.SemaphoreType.DMA((n_ops,)), pltpu.SemaphoreType.DMA((n_ops,))],
        compiler_params=params)

    def run(*operands):
        res = call(*operands, *c_ins)
        own = res[0] if single else res[:n_out]
        return own, res[n_out:]

    return run


def _rows(tm, w):
    return pl.BlockSpec((tm, w), lambda i: (i, 0))


def _full(*shape):
    return pl.BlockSpec(shape, lambda *_: (0,) * len(shape))


def _sds(shape, dtype=F32):
    return jax.ShapeDtypeStruct(shape, dtype)


def _row_tile(rows, cap=256, mult=8):
    t = min(rows, cap)
    while rows % t or t % mult:
        t -= 1
    return t


def _dot(a, b):
    return jnp.dot(a, b, preferred_element_type=F32)


def _dot_nt(a, b):
    return lax.dot_general(a, b, (((1,), (1,)), ((), ())), preferred_element_type=F32)


def _dot_tn(a, b):
    return lax.dot_general(a, b, (((0,), (0,)), ((), ())), preferred_element_type=F32)


def _sigmoid(x):
    return 1.0 / (1.0 + jnp.exp(-x))


_GELU_C = math.sqrt(2.0 / math.pi)


def _gelu(x):
    return x * (0.5 * (1.0 + jnp.tanh(_GELU_C * (x + 0.044715 * (x * x * x)))))


def _gelu_grad(x):
    t = jnp.tanh(_GELU_C * (x + 0.044715 * (x * x * x)))
    cdf = 0.5 * (1.0 + t)
    return cdf + x * (0.5 * (1.0 - t * t) * _GELU_C * (1.0 + 3.0 * 0.044715 * (x * x)))


def _rms_scale(x):
    return lax.rsqrt(jnp.mean(x * x, axis=-1, keepdims=True) + EPS)


def _rms_bwd(x, rs, g, dy):
    gdy = dy * g
    dx = rs * gdy - x * ((rs * rs * rs) * jnp.mean(gdy * x, axis=-1, keepdims=True))
    return dx, dy * (x * rs)


def _log1p(e):
    u = 1.0 + e
    d = u - 1.0
    return jnp.where(d == 0.0, e, jnp.log(u) * (e / jnp.where(d == 0.0, 1.0, d)))


def _softplus(y):
    return jnp.maximum(y, 0.0) + _log1p(jnp.exp(-jnp.abs(y)))


def _expm1(x):
    u = jnp.exp(x)
    lu = jnp.log(u)
    safe = jnp.where((u == 1.0) | (u == 0.0), 1.0, lu)
    return jnp.where(u == 1.0, x, jnp.where(u == 0.0, -1.0, (u - 1.0) * (x / safe)))


def _row_iota(shape):
    return lax.broadcasted_iota(jnp.int32, shape, 0)


def _lane_iota(shape):
    return lax.broadcasted_iota(jnp.int32, shape, 1)


def _scan_groups_fwd(a, b):
    sub = lax.broadcasted_iota(jnp.int32, a.shape, 1)
    for sh in (1, 2, 4):
        m = sub >= sh
        b = jnp.where(m, a * pltpu.roll(b, sh, 1) + b, b)
        a = jnp.where(m, a * pltpu.roll(a, sh, 1), a)
    return a, b


def _scan_groups_bwd(c, b):
    sub = lax.broadcasted_iota(jnp.int32, c.shape, 1)
    for sh in (1, 2, 4):
        m = sub < 8 - sh
        b = jnp.where(m, b + c * pltpu.roll(b, 8 - sh, 1), b)
        c = jnp.where(m, c * pltpu.roll(c, 8 - sh, 1), c)
    return c, b


def _rope_pair(gc):
    return gc + pltpu.roll(gc, 64, 1)


def _inproj(x, g, w1, tm, comm=None):
    t = x.shape[0]
    widths = (D, 3 * D, Q_LORA, KV_LORA, KR_W)

    def body(x_ref, g_ref, w_ref, xn_ref, rx_ref, g3_ref, cq_ref, ckv_ref, kr_ref):
        xv = x_ref[...]
        xn = (xv * _rms_scale(xv) * g_ref[...]).astype(BF16)
        xn_ref[...] = xn
        col = 0
        for ref, w in zip((rx_ref, g3_ref, cq_ref, ckv_ref, kr_ref), widths):
            for c0 in range(0, w, 512):
                cw = min(512, w - c0)
                ref[:, c0:c0 + cw] = _dot_nt(xn, w_ref[col + c0:col + c0 + cw, :])
            col += w

    return _pcall(
        body, "inproj", (t // tm,),
        [_rows(tm, D), _full(1, D), _full(W1_COLS, D)],
        [_rows(tm, D)] + [_rows(tm, w) for w in widths],
        [_sds((t, D), BF16)] + [_sds((t, w)) for w in widths],
        comm=comm,
    )(x, g, w1)


def _lru_gates(xa, wa_ref, ba, wx_ref, bx, pre_r, pre_i):
    xb = xa.astype(BF16)
    for n in range(RNN_BLOCKS):
        sl = slice(n * RNN_BW, (n + 1) * RNN_BW)
        pre_r[:, sl] = _dot(xb[:, sl], wa_ref[n])
        pre_i[:, sl] = _dot(xb[:, sl], wx_ref[n])
    r = _sigmoid(pre_r[...] + ba)
    i = _sigmoid(pre_i[...] + bx)
    return r, i


def _lru_fwd(rx, conv_w, conv_b, wa, ba, wx, bx, lam, tb, comm=None):
    t = rx.shape[0]
    nb = t // tb

    def body(x_ref, xp_ref, cw_ref, cb_ref, wa_ref, ba_ref, wx_ref, bx_ref, lam_ref, h_ref, xa_ref, hc, tmp, pre_r, pre_i):
        i_blk = pl.program_id(0)

        @pl.when(i_blk == 0)
        def _():
            hc[...] = jnp.zeros_like(hc)

        xv = x_ref[...]
        xp = jnp.where(i_blk > 0, xp_ref[...], 0.0)
        row8 = _row_iota((8, D))
        xa = cb_ref[...] + cw_ref[3:4, :] * xv
        for s in (1, 2, 3):
            xr = pltpu.roll(xv, s, 0)
            tmp[...] = xr
            tmp[0:8, :] = jnp.where(row8 < s, pltpu.roll(xp, s, 0), xr[0:8, :])
            xa = xa + cw_ref[3 - s:4 - s, :] * tmp[...]
        xa_ref[...] = xa
        r, gi = _lru_gates(xa, wa_ref, ba_ref[...], wx_ref, bx_ref[...], pre_r, pre_i)
        la = (-LRU_C * _softplus(-lam_ref[...])) * r
        a = jnp.exp(la)
        b = jnp.sqrt(-_expm1(2.0 * la)) * (gi * xa)
        a3, b3 = _scan_groups_fwd(a.reshape(tb // 8, 8, D), b.reshape(tb // 8, 8, D))
        carry = hc[...]
        for grp in range(tb // 8):
            hg = a3[grp] * carry + b3[grp]
            h_ref[8 * grp:8 * grp + 8, :] = hg
            carry = hg[7:8, :]
        hc[...] = carry

    prev8 = pl.BlockSpec((8, D), lambda i: (jnp.maximum(i * (tb // 8) - 1, 0), 0))
    return _pcall(
        body, "lru_fwd", (nb,),
        [_rows(tb, D), prev8, _full(4, D), _full(1, D), _full(RNN_BLOCKS, RNN_BW, RNN_BW), _full(1, D),
         _full(RNN_BLOCKS, RNN_BW, RNN_BW), _full(1, D), _full(1, D)],
        [_rows(tb, D), _rows(tb, D)],
        [_sds((t, D)), _sds((t, D))],
        scratch=[pltpu.VMEM((1, D), F32), pltpu.VMEM((tb, D), F32), pltpu.VMEM((tb, D), F32), pltpu.VMEM((tb, D), F32)],
        comm=comm,
    )(rx, rx, conv_w, conv_b, wa, ba, wx, bx, lam)


def _mla_proj(cq, ckv, kr, qn, kvn, wq, wkv, rope_c, tm):
    t = cq.shape[0]

    def body(cq_ref, ckv_ref, kr_ref, qn_ref, kvn_ref, wq_ref, wkv_ref, c_ref, q_ref, k_ref, v_ref, cqn_ref, ckvn_ref):
        cqv = cq_ref[...]
        cqn = (cqv * _rms_scale(cqv) * qn_ref[...]).astype(BF16)
        ckvv = ckv_ref[...]
        ckvn = (ckvv * _rms_scale(ckvv) * kvn_ref[...]).astype(BF16)
        cqn_ref[...] = cqn
        ckvn_ref[...] = ckvn
        c = c_ref[...]
        lane = _lane_iota((tm, KR_W))
        kro = jnp.where(lane < 64, _rope_pair(kr_ref[...] * c), 0.0).astype(BF16)
        for h in range(N_HEADS):
            sl = slice(h * HEAD_W, (h + 1) * HEAD_W)
            qh = _dot(cqn, wq_ref[:, sl])
            q_ref[h, :, 0:128] = (qh[:, 0:128] * EXP2_SCALE).astype(BF16)
            q_ref[h, :, 128:256] = (_rope_pair(qh[:, 128:256] * c) * EXP2_SCALE).astype(BF16)
            kvh = _dot(ckvn, wkv_ref[:, sl])
            k_ref[h, :, 0:128] = kvh[:, 0:128].astype(BF16)
            k_ref[h, :, 128:256] = kro
            v_ref[h, :, 0:V_HEAD] = kvh[:, 128:256].astype(BF16)
            v_ref[h, :, V_HEAD:2 * V_HEAD] = jnp.ones((tm, V_HEAD), BF16)

    hb = lambda w: pl.BlockSpec((N_HEADS, tm, w), lambda i: (0, i, 0))
    return _pcall(
        body, "mla_proj", (t // tm,),
        [_rows(tm, Q_LORA), _rows(tm, KV_LORA), _rows(tm, KR_W), _full(1, Q_LORA), _full(1, KV_LORA),
         _full(Q_LORA, N_HEADS * HEAD_W), _full(KV_LORA, N_HEADS * HEAD_W), _rows(tm, KR_W)],
        [hb(HEAD_W), hb(HEAD_W), hb(2 * V_HEAD), _rows(tm, Q_LORA), _rows(tm, KV_LORA)],
        [_sds((N_HEADS, t, HEAD_W), BF16), _sds((N_HEADS, t, HEAD_W), BF16), _sds((N_HEADS, t, 2 * V_HEAD), BF16),
         _sds((t, Q_LORA), BF16), _sds((t, KV_LORA), BF16)],
    )(cq, ckv, kr, qn, kvn, wq, wkv, rope_c)


def _flash_fwd(q, k, v, tq, comm=None):
    t = q.shape[1]
    nq = t // tq

    def body(q_ref, k_ref, v_ref, o_ref, lse_ref, s_even, s_odd):
        qi = pl.program_id(1)
        qv = q_ref[0]

        def scores(ki, buf):
            buf[...] = _dot_nt(qv, k_ref[0, pl.ds(pl.multiple_of(ki * tq, tq), tq), :])

        def softmax_pv(ki, buf, carry, diagonal):
            m, acc = carry
            s = buf[...]
            if diagonal:
                s = jnp.where(_row_iota((tq, tq)) >= _lane_iota((tq, tq)), s, NEG)
            m_new = jnp.maximum(m, jnp.max(s, axis=1, keepdims=True))
            p = jnp.exp2(s - m_new)
            alpha = jnp.exp2(m - m_new)
            acc = alpha * acc + _dot(p.astype(BF16), v_ref[0, pl.ds(pl.multiple_of(ki * tq, tq), tq), :])
            return m_new, acc

        def finish(carry):
            m, acc = carry
            l = acc[:, V_HEAD:2 * V_HEAD]
            o_ref[...] = acc[:, 0:V_HEAD] / l
            lse = m + jnp.log(l) * math.log2(math.e)
            lse_ref[0, 0] = jnp.transpose(lse)[0:8, :]

        def two(i, carry):
            scores(2 * i + 1, s_odd)
            carry = softmax_pv(2 * i, s_even, carry, False)
            scores(2 * i + 2, s_even)
            return softmax_pv(2 * i + 1, s_odd, carry, False)

        init = (jnp.full((tq, 1), -jnp.inf, F32), jnp.zeros((tq, 2 * V_HEAD), F32))
        scores(0, s_even)
        carry = lax.fori_loop(0, qi // 2, two, init)

        @pl.when(qi % 2 == 0)
        def _():
            finish(softmax_pv(qi, s_even, carry, True))

        @pl.when(qi % 2 == 1)
        def _():
            scores(qi, s_odd)
            finish(softmax_pv(qi, s_odd, softmax_pv(qi - 1, s_even, carry, False), True))

    head = lambda w: pl.BlockSpec((1, t, w), lambda h, qi: (h, 0, 0))
    return _pcall(
        body, "flash_fwd", (N_HEADS, nq),
        [pl.BlockSpec((1, tq, HEAD_W), lambda h, qi: (h, qi, 0)), head(HEAD_W), head(2 * V_HEAD)],
        [pl.BlockSpec((tq, V_HEAD), lambda h, qi: (qi, h)), pl.BlockSpec((1, 1, 8, tq), lambda h, qi: (h, qi, 0, 0))],
        [_sds((t, D)), _sds((N_HEADS, nq, 8, tq))],
        scratch=[pltpu.VMEM((tq, tq), F32), pltpu.VMEM((tq, tq), F32)],
        comm=comm,
    )(q, k, v)


def _merge_out(x, h, g3, yb, w_out, tm):
    t = x.shape[0]

    def body(x_ref, h_ref, g3_ref, yb_ref, w_ref, h1_ref, mg_ref):
        ya = h_ref[...] * _gelu(g3_ref[:, 0:D])
        merged = (_sigmoid(g3_ref[:, D:2 * D]) * ya + _sigmoid(g3_ref[:, 2 * D:3 * D]) * yb_ref[...]).astype(BF16)
        mg_ref[...] = merged
        h1_ref[...] = x_ref[...] + _dot(merged, w_ref[...])

    return _pcall(
        body, "merge_out", (t // tm,),
        [_rows(tm, D), _rows(tm, D), _rows(tm, 3 * D), _rows(tm, D), _full(D, D)],
        [_rows(tm, D), _rows(tm, D)],
        [_sds((t, D)), _sds((t, D), BF16)],
    )(x, h, g3, yb, w_out)


def _mlp_up(h1, g, w_up, tm):
    t = h1.shape[0]

    def body(h_ref, g_ref, w_ref, u_ref, n2_ref):
        hv = h_ref[...]
        n2 = (hv * _rms_scale(hv) * g_ref[...]).astype(BF16)
        n2_ref[...] = n2
        for c0 in range(0, D_FF, 512):
            u_ref[:, c0:c0 + 512] = _dot(n2, w_ref[:, c0:c0 + 512])

    return _pcall(
        body, "mlp_up", (t // tm,),
        [_rows(tm, D), _full(1, D), _full(D, D_FF)],
        [_rows(tm, D_FF), _rows(tm, D)],
        [_sds((t, D_FF)), _sds((t, D), BF16)],
    )(h1, g, w_up)


def _mlp_down_loss(u, h1, target, w_down, g, tm):
    t = u.shape[0]

    def body(u_ref, h1_ref, tg_ref, w_ref, g_ref, act_ref, dh2_ref, loss_ref, gnf_ref, lacc):
        i = pl.program_id(0)

        @pl.when(i == 0)
        def _():
            lacc[...] = jnp.zeros_like(lacc)
            gnf_ref[...] = jnp.zeros_like(gnf_ref)

        ru = jnp.maximum(u_ref[...], 0.0)
        act = (ru * ru).astype(BF16)
        act_ref[...] = act
        h2 = h1_ref[...] + _dot(act, w_ref[...])
        rs = _rms_scale(h2)
        gv = g_ref[...]
        err = h2 * rs * gv - tg_ref[...]
        lacc[...] += jnp.sum(err * err, axis=0, keepdims=True)
        dx, dgr = _rms_bwd(h2, rs, gv, err * (1.0 / D))
        dh2_ref[...] = dx
        gnf_ref[...] += jnp.sum(dgr, axis=0, keepdims=True)

        @pl.when(i == pl.num_programs(0) - 1)
        def _():
            loss_ref[...] = jnp.broadcast_to(jnp.sum(lacc[...], axis=1, keepdims=True) * (0.5 / D), (8, 128))

    return _pcall(
        body, "mlp_down_loss", (t // tm,),
        [_rows(tm, D_FF), _rows(tm, D), _rows(tm, D), _full(D_FF, D), _full(1, D)],
        [_rows(tm, D_FF), _rows(tm, D), _full(8, 128), _full(1, D)],
        [_sds((t, D_FF), BF16), _sds((t, D)), _sds((8, 128)), _sds((1, D))],
        scratch=[pltpu.VMEM((1, D), F32)],
    )(u, h1, target, w_down, g)


def _matmul_tn(a, g, name, chunked=None, comm=None):
    t, kdim = a.shape
    ndim = g.shape[1]
    tk = min(kdim, 1024)
    tn = ndim if ndim <= 1024 else 1024
    if chunked == "cols":
        assert tk == kdim and tn == ndim // N_CHIPS
    elif chunked == "rows":
        assert tk == kdim // N_CHIPS and tn == ndim
    tt = min(t, 2048)
    nt = t // tt

    def body(a_ref, g_ref, o_ref):
        @pl.when(pl.program_id(2) == 0)
        def _():
            o_ref[...] = jnp.zeros_like(o_ref)

        o_ref[...] += _dot_tn(a_ref[...].astype(BF16), g_ref[...].astype(BF16)).reshape(o_ref.shape)

    if chunked is not None:
        out_spec = pl.BlockSpec((2, None, tk // 2, tn), lambda i, j, s: (0, i + j, 0, 0))
        out_shape = _sds((2, N_CHIPS, tk // 2, tn))
    else:
        out_spec, out_shape = pl.BlockSpec((tk, tn), lambda i, j, s: (i, j)), _sds((kdim, ndim))
    return _pcall(
        body, name, (kdim // tk, ndim // tn, nt),
        [pl.BlockSpec((tt, tk), lambda i, j, s: (s, i)), pl.BlockSpec((tt, tn), lambda i, j, s: (s, j))],
        out_spec, out_shape, comm=comm,
    )(a, g)


def _mlp_bwd_act(dh2, u, w_down, tm):
    t = u.shape[0]

    def body(d_ref, u_ref, w_ref, du_ref):
        db = d_ref[...].astype(BF16)
        for c0 in range(0, D_FF, 512):
            da = _dot_nt(db, w_ref[c0:c0 + 512, :])
            du_ref[:, c0:c0 + 512] = (da * (2.0 * jnp.maximum(u_ref[:, c0:c0 + 512], 0.0))).astype(BF16)

    return _pcall(
        body, "mlp_bwd_act", (t // tm,),
        [_rows(tm, D), _rows(tm, D_FF), _full(D_FF, D)],
        _rows(tm, D_FF), _sds((t, D_FF), BF16),
    )(dh2, u, w_down)


def _mlp_bwd_in(du, dh2, h1, w_up, g, tm):
    t = du.shape[0]

    def body(du_ref, d_ref, h_ref, w_ref, g_ref, dh1_ref, gacc_ref):
        @pl.when(pl.program_id(0) == 0)
        def _():
            gacc_ref[...] = jnp.zeros_like(gacc_ref)

        dn2 = _dot_nt(du_ref[...], w_ref[...])
        hv = h_ref[...]
        dx, dgr = _rms_bwd(hv, _rms_scale(hv), g_ref[...], dn2)
        dh1_ref[...] = d_ref[...] + dx
        gacc_ref[...] += jnp.sum(dgr, axis=0, keepdims=True)

    return _pcall(
        body, "mlp_bwd_in", (t // tm,),
        [_rows(tm, D_FF), _rows(tm, D), _rows(tm, D), _full(D, D_FF), _full(1, D)],
        [_rows(tm, D), _full(1, D)],
        [_sds((t, D)), _sds((1, D))],
    )(du, dh2, h1, w_up, g)


def _merge_bwd(dh1, w_out, g3, h, yb, merged, tm):
    t = dh1.shape[0]

    def body(d_ref, w_ref, g3_ref, h_ref, yb_ref, mg_ref, dg3_ref, dyb_ref, dl_ref, dh_ref, dwo_ref):
        @pl.when(pl.program_id(0) == 0)
        def _():
            dwo_ref[...] = jnp.zeros_like(dwo_ref)

        db = d_ref[...].astype(BF16)
        dwo_ref[...] += _dot_tn(mg_ref[...], db)
        dm = _dot_nt(db, w_ref[...])
        gv = g3_ref[:, 0:D]
        sa = _sigmoid(g3_ref[:, D:2 * D])
        sb = _sigmoid(g3_ref[:, 2 * D:3 * D])
        gel = _gelu(gv)
        hv = h_ref[...]
        ybv = yb_ref[...]
        dya = dm * sa
        dyb = dm * sb
        dg3_ref[:, 0:D] = (dya * hv * _gelu_grad(gv)).astype(BF16)
        dg3_ref[:, D:2 * D] = (dya * (hv * gel) * (1.0 - sa)).astype(BF16)
        dg3_ref[:, 2 * D:3 * D] = (dyb * ybv * (1.0 - sb)).astype(BF16)
        dh_ref[...] = dya * gel
        dyb_ref[...] = dyb.astype(BF16)
        prod = dyb * ybv
        ones = jnp.ones((8, V_HEAD), F32)
        for hh in range(N_HEADS):
            dl_ref[hh] = lax.dot_general(ones, prod[:, hh * V_HEAD:(hh + 1) * V_HEAD], (((1,), (1,)), ((), ())),
                                         precision=lax.Precision.HIGHEST, preferred_element_type=F32)

    return _pcall(
        body, "merge_bwd", (t // tm,),
        [_rows(tm, D), _full(D, D), _rows(tm, 3 * D), _rows(tm, D), _rows(tm, D), _rows(tm, D)],
        [_rows(tm, 3 * D), _rows(tm, D), pl.BlockSpec((N_HEADS, 8, tm), lambda i: (0, 0, i)), _rows(tm, D), _full(D, D)],
        [_sds((t, 3 * D), BF16), _sds((t, D), BF16), _sds((N_HEADS, 8, t)), _sds((t, D)), _sds((D, D))],
    )(dh1, w_out, g3, h, yb, merged)


def _flash_bwd(q, k, v, do, lse, delta, tq, comm=None):
    t = q.shape[1]
    nq = t // tq

    def body(q_ref, k_ref, v_ref, do_ref, lse_ref, dl_ref, dqt_ref, dk_ref, dv_ref):
        ki = pl.program_id(1)

        @pl.when(ki == 0)
        def _():
            dqt_ref[...] = jnp.zeros_like(dqt_ref)

        kblk, vblk = k_ref[0], v_ref[0]
        kt = jnp.transpose(kblk)
        dk_ref[...] = jnp.zeros_like(dk_ref)
        dv_ref[...] = jnp.zeros_like(dv_ref)

        def block(qi, diagonal):
            rows = pl.ds(pl.multiple_of(qi * tq, tq), tq)
            qv, dov = q_ref[0, rows, :], do_ref[rows, :]
            p = jnp.exp2(_dot_nt(kblk, qv) - lse_ref[0, qi, 0:1, :])
            if diagonal:
                p = jnp.where(_lane_iota((tq, tq)) >= _row_iota((tq, tq)), p, 0.0)
            dv_ref[0] += _dot(p.astype(BF16), dov)
            dp = _dot_nt(vblk, dov)
            ds = (p * (dp - dl_ref[0, qi, 0:1, :]) * math.log(2.0)).astype(BF16)
            dk_ref[0] += _dot(ds, qv)
            dqt_ref[0, qi] += _dot(kt, ds)

        block(ki, True)

        def two(i, carry):
            block(ki + 1 + 2 * i, False)
            block(ki + 2 + 2 * i, False)
            return carry

        def one(qi, carry):
            block(qi, False)
            return carry

        pairs = (nq - 1 - ki) // 2
        lax.fori_loop(0, pairs, two, 0)
        lax.fori_loop(ki + 1 + 2 * pairs, nq, one, 0)

    kv_spec = lambda w: pl.BlockSpec((1, tq, w), lambda h, ki: (h, ki, 0))
    stat = pl.BlockSpec((1, nq, 8, tq), lambda h, ki: (h, 0, 0, 0))
    return _pcall(
        body, "flash_bwd", (N_HEADS, nq),
        [pl.BlockSpec((1, t, HEAD_W), lambda h, ki: (h, 0, 0)), kv_spec(HEAD_W), kv_spec(V_HEAD),
         pl.BlockSpec((t, V_HEAD), lambda h, ki: (0, h)), stat, stat],
        [pl.BlockSpec((1, nq, HEAD_W, tq), lambda h, ki: (h, 0, 0, 0)), kv_spec(HEAD_W), kv_spec(V_HEAD)],
        [_sds((N_HEADS, nq, HEAD_W, tq)), _sds((N_HEADS, t, HEAD_W)), _sds((N_HEADS, t, V_HEAD))],
        comm=comm,
    )(q, k, v, do, lse, delta)


def _mla_bwd(dqt, dk, dv, cqn, ckvn, cq, ckv, rope_c, wq, wkv, qn, kvn, tm):
    t = cq.shape[0]

    def body(dq_ref, dk_ref, dv_ref, cqn_ref, ckvn_ref, cq_ref, ckv_ref, c_ref, wq_ref, wkv_ref, qn_ref, kvn_ref,
             dmla_ref, dwq_ref, dwkv_ref, dqn_ref, dkvn_ref):
        @pl.when(pl.program_id(0) == 0)
        def _():
            dwq_ref[...] = jnp.zeros_like(dwq_ref)
            dwkv_ref[...] = jnp.zeros_like(dwkv_ref)
            dqn_ref[...] = jnp.zeros_like(dqn_ref)
            dkvn_ref[...] = jnp.zeros_like(dkvn_ref)

        c = c_ref[...]
        lane = _lane_iota((tm, KR_W))
        cqn, ckvn = cqn_ref[...], ckvn_ref[...]
        dcqn = jnp.zeros((tm, Q_LORA), F32)
        dckvn = jnp.zeros((tm, KV_LORA), F32)
        dkr = jnp.zeros((tm, KR_W), F32)
        for h in range(N_HEADS):
            sl = slice(h * HEAD_W, (h + 1) * HEAD_W)
            dqh = jnp.transpose(dq_ref[h, 0]) * EXP2_SCALE
            droped = jnp.where(lane < 64, dqh[:, 128:256], 0.0)
            dqp = jnp.concatenate([dqh[:, 0:128], _rope_pair(droped) * c], axis=1).astype(BF16)
            dcqn = dcqn + _dot_nt(dqp, wq_ref[:, sl])
            dwq_ref[:, sl] += _dot_tn(cqn, dqp)
            dkr = dkr + jnp.where(lane < 64, dk_ref[h, :, 128:256], 0.0)
            dkvp = jnp.concatenate([dk_ref[h, :, 0:128], dv_ref[h]], axis=1).astype(BF16)
            dckvn = dckvn + _dot_nt(dkvp, wkv_ref[:, sl])
            dwkv_ref[:, sl] += _dot_tn(ckvn, dkvp)
        cqv, ckvv = cq_ref[...], ckv_ref[...]
        dcq, dgq = _rms_bwd(cqv, _rms_scale(cqv), qn_ref[...], dcqn)
        dckv, dgkv = _rms_bwd(ckvv, _rms_scale(ckvv), kvn_ref[...], dckvn)
        dqn_ref[...] += jnp.sum(dgq, axis=0, keepdims=True)
        dkvn_ref[...] += jnp.sum(dgkv, axis=0, keepdims=True)
        dmla_ref[:, 0:256] = dcq.astype(BF16)
        dmla_ref[:, 256:512] = dckv.astype(BF16)
        dmla_ref[:, 512:640] = (_rope_pair(dkr) * c).astype(BF16)

    hb = lambda w: pl.BlockSpec((N_HEADS, tm, w), lambda i: (0, i, 0))
    wide = N_HEADS * HEAD_W
    per_q = dqt.shape[3] // tm
    dq_spec = pl.BlockSpec((N_HEADS, 1, HEAD_W, tm), lambda i: (0, i // per_q, 0, i % per_q))
    return _pcall(
        body, "mla_bwd", (t // tm,),
        [dq_spec, hb(HEAD_W), hb(V_HEAD), _rows(tm, Q_LORA), _rows(tm, KV_LORA), _rows(tm, Q_LORA), _rows(tm, KV_LORA),
         _rows(tm, KR_W), _full(Q_LORA, wide), _full(KV_LORA, wide), _full(1, Q_LORA), _full(1, KV_LORA)],
        [_rows(tm, 640), _full(Q_LORA, wide), _full(KV_LORA, wide), _full(1, Q_LORA), _full(1, KV_LORA)],
        [_sds((t, 640), BF16), _sds((Q_LORA, wide)), _sds((KV_LORA, wide)), _sds((1, Q_LORA)), _sds((1, KV_LORA))],
    )(dqt, dk, dv, cqn, ckvn, cq, ckv, rope_c, wq, wkv, qn, kvn)


def _lru_bwd(dh, xa, h, rx, conv_w, wa, ba, wx, bx, lam, tb, comm=None):
    t = dh.shape[0]
    nb = t // tb

    def body(dh_ref, xa_ref, h_ref, hp_ref, x_ref, cw_ref, wa_ref, ba_ref, wx_ref, bx_ref, lam_ref,
             drx_ref, dcw_ref, dcb_ref, dwa_ref, dba_ref, dwx_ref, dbx_ref, dlam_ref, gc, dxn, tmp, pre_r, pre_i):
        step = pl.program_id(0)
        first_block = step == nb - 1

        @pl.when(step == 0)
        def _():
            gc[...] = jnp.zeros_like(gc)
            dxn[...] = jnp.zeros_like(dxn)
            for ref in (dcw_ref, dcb_ref, dwa_ref, dba_ref, dwx_ref, dbx_ref, dlam_ref):
                ref[...] = jnp.zeros_like(ref)

        xa = xa_ref[...]
        r, gi = _lru_gates(xa, wa_ref, ba_ref[...], wx_ref, bx_ref[...], pre_r, pre_i)
        lamv = lam_ref[...]
        sp = _softplus(-lamv)
        la = (-LRU_C * sp) * r
        a = jnp.exp(la)
        e2 = _expm1(2.0 * la)
        sq = jnp.sqrt(-e2)
        row = _row_iota((tb, D))
        cf = jnp.where(row == tb - 1, 1.0, pltpu.roll(a, tb - 1, 0))
        c3, b3 = _scan_groups_bwd(cf.reshape(tb // 8, 8, D), dh_ref[...].reshape(tb // 8, 8, D))
        carry = gc[...]
        for grp in reversed(range(tb // 8)):
            dg = b3[grp] + c3[grp] * carry
            pre_r[8 * grp:8 * grp + 8, :] = dg
            carry = dg[0:1, :]
        delta = pre_r[...]
        gc[...] = a[0:1, :] * carry
        hv = h_ref[...]
        hr = pltpu.roll(hv, 1, 0)
        tmp[...] = hr
        tmp[0:1, :] = jnp.where(first_block, 0.0, hp_ref[7:8, :])
        hprev = tmp[...]
        ix = gi * xa
        dla = (delta * hprev) * a - (delta * ix) * ((e2 + 1.0) / sq)
        dlam_ref[...] += jnp.sum(dla * r, axis=0, keepdims=True) * (LRU_C * _sigmoid(-lamv))
        dpr = (dla * (-LRU_C * sp)) * r * (1.0 - r)
        dsq = delta * sq
        dpi = (dsq * xa) * gi * (1.0 - gi)
        dba_ref[...] += jnp.sum(dpr, axis=0, keepdims=True)
        dbx_ref[...] += jnp.sum(dpi, axis=0, keepdims=True)
        pre_r[...] = dpr
        pre_i[...] = dpi
        xb = xa.astype(BF16)
        for n in range(RNN_BLOCKS):
            sl = slice(n * RNN_BW, (n + 1) * RNN_BW)
            dprn = pre_r[:, sl].astype(BF16)
            dpin = pre_i[:, sl].astype(BF16)
            dwa_ref[n] += _dot_tn(xb[:, sl], dprn)
            dwx_ref[n] += _dot_tn(xb[:, sl], dpin)
            tmp[:, sl] = _dot_nt(dprn, wa_ref[n]) + _dot_nt(dpin, wx_ref[n])
        dxa = dsq * gi + tmp[...]
        dcb_ref[...] += jnp.sum(dxa, axis=0, keepdims=True)
        xv = x_ref[...]
        drx = cw_ref[3:4, :] * dxa
        dcw_ref[3:4, :] += jnp.sum(dxa * xv, axis=0, keepdims=True)
        row8 = _row_iota((8, D))
        nxt = dxn[...]
        for s in (1, 2, 3):
            dr_ = pltpu.roll(dxa, tb - s, 0)
            tmp[...] = dr_
            tmp[tb - 8:tb, :] = jnp.where(row8 >= 8 - s, pltpu.roll(nxt, 8 - s, 0), dr_[tb - 8:tb, :])
            dxs = tmp[...]
            drx = drx + cw_ref[3 - s:4 - s, :] * dxs
            dcw_ref[3 - s:4 - s, :] += jnp.sum(dxs * xv, axis=0, keepdims=True)
        drx_ref[...] = drx.astype(BF16)
        dxn[...] = dxa[0:8, :]

    rev = pl.BlockSpec((tb, D), lambda i: (nb - 1 - i, 0))
    prev8 = pl.BlockSpec((8, D), lambda i: (jnp.maximum((nb - 1 - i) * (tb // 8) - 1, 0), 0))
    wblk = _full(RNN_BLOCKS, RNN_BW, RNN_BW)
    return _pcall(
        body, "lru_bwd", (nb,),
        [rev, rev, rev, prev8, rev, _full(4, D), wblk, _full(1, D), wblk, _full(1, D), _full(1, D)],
        [rev, _full(4, D), _full(1, D), wblk, _full(1, D), wblk, _full(1, D), _full(1, D)],
        [_sds((t, D), BF16), _sds((4, D)), _sds((1, D)), _sds((RNN_BLOCKS, RNN_BW, RNN_BW)), _sds((1, D)),
         _sds((RNN_BLOCKS, RNN_BW, RNN_BW)), _sds((1, D)), _sds((1, D))],
        scratch=[pltpu.VMEM((1, D), F32), pltpu.VMEM((8, D), F32), pltpu.VMEM((tb, D), F32), pltpu.VMEM((tb, D), F32),
                 pltpu.VMEM((tb, D), F32)],
        comm=comm,
    )(dh, xa, h, h, rx, conv_w, wa, ba, wx, bx, lam)


def _inproj_bwd(x, dh1, drx, dg3, dmla, w1, g, tm, comm=None):
    t = x.shape[0]

    def body(x_ref, d_ref, drx_ref, dg3_ref, dmla_ref, w_ref, g_ref, dx_ref, gacc_ref):
        @pl.when(pl.program_id(0) == 0)
        def _():
            gacc_ref[...] = jnp.zeros_like(gacc_ref)

        dxn = _dot(drx_ref[...], w_ref[0:D, :])
        for c0 in range(0, 3 * D, D):
            dxn = dxn + _dot(dg3_ref[:, c0:c0 + D], w_ref[D + c0:2 * D + c0, :])
        dxn = dxn + _dot(dmla_ref[...], w_ref[4 * D:W1_COLS, :])
        xv = x_ref[...]
        dx, dgr = _rms_bwd(xv, _rms_scale(xv), g_ref[...], dxn)
        dx_ref[...] = d_ref[...] + dx
        gacc_ref[...] += jnp.sum(dgr, axis=0, keepdims=True)

    return _pcall(
        body, "inproj_bwd", (t // tm,),
        [_rows(tm, D), _rows(tm, D), _rows(tm, D), _rows(tm, 3 * D), _rows(tm, 640), _full(W1_COLS, D), _full(1, D)],
        [_rows(tm, D), _full(1, D)],
        [_sds((t, D)), _sds((1, D))],
        comm=comm,
    )(x, dh1, drx, dg3, dmla, w1, g)


def _pcall_indexed(body, name, index, grid, in_specs, out_specs, out_shape):
    call = pl.pallas_call(
        body, name=name, out_shape=out_shape,
        grid_spec=pltpu.PrefetchScalarGridSpec(num_scalar_prefetch=1, grid=grid, in_specs=in_specs, out_specs=out_specs),
        compiler_params=pltpu.CompilerParams(dimension_semantics=("arbitrary",) * len(grid), vmem_limit_bytes=V7X_VMEM_LIMIT))
    return lambda *operands: call(index, *operands)


V7X_STREAM_BLOCK_BYTES = 5 << 19


def _stream_tile(rows, cols):
    return _row_tile(rows, cap=max(256, V7X_STREAM_BLOCK_BYTES // (4 * cols)), mult=16)


def _pair_sum(halves, theirs, core, out_dtype, name):
    _, rows, cols = halves.shape
    tm = _stream_tile(rows, cols)

    def body(c_ref, a_ref, b_ref, o_ref):
        o_ref[...] = (a_ref[0] + b_ref[...]).astype(out_dtype)

    plain = pl.BlockSpec((tm, cols), lambda i, c: (i, 0))
    return _pcall_indexed(body, name, core, (rows // tm,),
                          [pl.BlockSpec((1, tm, cols), lambda i, c: (c[0], i, 0)), plain], plain,
                          _sds((rows, cols), out_dtype))(halves, theirs)


def _chip_sum(parts, recv, chip, name):
    _, rows, cols = parts.shape
    tm = _stream_tile(rows, cols)

    def body(c_ref, a_ref, r_ref, o_ref):
        o_ref[...] = ((a_ref[0].astype(F32) + r_ref[0].astype(F32)) + r_ref[1].astype(F32)) + r_ref[2].astype(F32)

    return _pcall_indexed(body, name, chip, (rows // tm,),
                          [pl.BlockSpec((1, tm, cols), lambda i, c: (c[0], i, 0)),
                           pl.BlockSpec((N_CHIPS - 1, tm, cols), lambda i, c: (0, i, 0))],
                          pl.BlockSpec((tm, cols), lambda i, c: (i, 0)), _sds((rows, cols)))(parts, recv)


def _adam_math(w, gv, m, v):
    mn = ADAM_B1 * m + (1.0 - ADAM_B1) * gv
    vn = ADAM_B2 * v + (1.0 - ADAM_B2) * (gv * gv)
    m_hat = mn / (1.0 - ADAM_B1 ** ADAM_STEP)
    v_hat = vn / (1.0 - ADAM_B2 ** ADAM_STEP)
    return -ADAM_LR * (m_hat / (jnp.sqrt(v_hat) + ADAM_EPS) + ADAM_WD * w), mn, vn


def _adamw(w, g, m, v, name):
    rows, cols = w.shape
    tm = _row_tile(rows)

    def body(w_ref, g_ref, m_ref, v_ref, d_ref, mo_ref, vo_ref):
        d_ref[...], mo_ref[...], vo_ref[...] = _adam_math(w_ref[...], g_ref[...], m_ref[...], v_ref[...])

    spec = _rows(tm, cols)
    return _pcall(body, name, (rows // tm,), [spec] * 4, [spec] * 3, [_sds((rows, cols))] * 3)(w, g, m, v)


def _adamw_halves(w, mine, theirs, m, v, core, name, by_cols=False):
    rows, cols = w.shape
    if by_cols:
        tm, tc = mine.shape[0] // 2, cols // 2
        grid = (rows // tm, 2)
        full = pl.BlockSpec((tm, tc), lambda i, j, c: (i, j))
        half = pl.BlockSpec((tm, tc), lambda i, j, c: (i, 0))
    else:
        tm = _row_tile(rows // 2)
        nh = rows // 2 // tm
        grid = (rows // tm,)
        full = pl.BlockSpec((tm, cols), lambda i, c: (i, 0))
        half = pl.BlockSpec((tm, cols), lambda i, c: (i % nh, 0))

    def body(c_ref, w_ref, a_ref, b_ref, m_ref, v_ref, g_ref, d_ref, mo_ref, vo_ref):
        which = pl.program_id(1) if by_cols else pl.program_id(0) // nh
        gv = jnp.where(which == c_ref[0], a_ref[...], b_ref[...])
        g_ref[...] = gv
        d_ref[...], mo_ref[...], vo_ref[...] = _adam_math(w_ref[...], gv, m_ref[...], v_ref[...])

    return _pcall_indexed(body, name, core, grid, [full, half, half, full, full], [full] * 4,
                          [_sds((rows, cols))] * 4)(w, mine, theirs, m, v)


REL_SIBLING = (0, 0, 1)
REL_CHIPS = ((1, 0, 0), (0, 1, 0), (1, 1, 0))


V7X_DMA_CHUNK_BYTES = 1 << 20


def _split_copy(src, dst, shape, itemsize):
    nbytes = math.prod(shape) * itemsize
    if nbytes <= V7X_DMA_CHUNK_BYTES or len(shape) < 2:
        return [(src, dst)]
    if len(shape) > 2:
        out = []
        for k in range(shape[0]):
            out += _split_copy(src.at[k], dst.at[k], shape[1:], itemsize)
        return out
    rows = shape[0]
    sub = 8 * (4 // itemsize)
    parts = max(1, min(-(-nbytes // V7X_DMA_CHUNK_BYTES), rows // sub))
    while rows % parts or (rows // parts) % sub:
        parts -= 1
    step = rows // parts
    return [(src.at[pl.ds(k * step, step)], dst.at[pl.ds(k * step, step)]) for k in range(parts)]


def _mesh_pos():
    return (lax.axis_index("x"), lax.axis_index("y"), lax.axis_index("c"))


def _make_copy(i, op, sems, pos, src=None, dst=None):
    rel = op[0]
    src, dst = (op[1], op[2]) if src is None else (src, dst)
    send_sems, recv_sems = sems
    if rel is None:
        return pltpu.make_async_copy(src, dst, send_sems.at[i])
    peer = tuple((p + r) % 2 for p, r in zip(pos, rel))
    return pltpu.make_async_remote_copy(src_ref=src, dst_ref=dst, send_sem=send_sems.at[i], recv_sem=recv_sems.at[i],
                                        device_id=peer, device_id_type=MESH_ID)


def _start_copies(ops, sems, pos, base=0):
    for i, op in enumerate(ops):
        for s_piece, d_piece in _split_copy(op[1], op[2], op[1].shape, jnp.dtype(op[1].dtype).itemsize):
            _make_copy(base + i, op, sems, pos, s_piece, d_piece).start()


def _wait_copies(ops, sems, pos, base=0):
    for i, op in enumerate(ops):
        _make_copy(base + i, op, sems, pos).wait()


def _comm(name, ins, out_shapes, n_ops, ops_fn):
    n_in, n_out = len(ins), len(out_shapes)

    def body(*refs):
        in_refs, out_refs = refs[:n_in], refs[n_in:n_in + n_out]
        sems = refs[n_in + n_out:]
        pos = _mesh_pos()
        ops = ops_fn(in_refs, out_refs, pos)
        assert len(ops) == n_ops
        _start_copies(ops, sems, pos)
        _wait_copies(ops, sems, pos)

    hbm = pl.BlockSpec(memory_space=pl.ANY)
    return pl.pallas_call(
        body, name=name, in_specs=[hbm] * n_in, out_specs=[hbm] * n_out, out_shape=list(out_shapes),
        scratch_shapes=[pltpu.SemaphoreType.DMA((n_ops,)), pltpu.SemaphoreType.DMA((n_ops,))],
    )(*ins)


def _chip_of(pos, rel=(0, 0, 0)):
    return 2 * ((pos[0] + rel[0]) % 2) + (pos[1] + rel[1]) % 2


def _gather_chips(shards, chip, name):
    def ops_fn(in_refs, out_refs, pos):
        me = _chip_of(pos)
        return [(rel, src, dst.at[me]) for src, dst in zip(in_refs, out_refs) for rel in REL_CHIPS]

    outs = _comm(name, shards, [_sds((N_CHIPS,) + s.shape, s.dtype) for s in shards], 3 * len(shards), ops_fn)
    return [lax.dynamic_update_index_in_dim(o, s, chip, 0) for o, s in zip(outs, shards)]


def _halved_gather_ops(pos, srcs, dsts, whole):
    me, c = _chip_of(pos), pos[2]
    ici, d2d = [], []
    for a, (src, dst) in enumerate(zip(srcs, dsts)):
        for rel in REL_CHIPS:
            if a in whole:
                ici.append((rel, src, dst.at[me]))
            else:
                ici.append((rel, src.at[c], dst.at[me, c]))
                arrived = dst.at[_chip_of(pos, rel), c]
                d2d.append((REL_SIBLING, arrived, arrived))
    return ici, d2d


def _halved_gather_comm(shards):
    srcs = [s.reshape(2, s.shape[0] // 2, s.shape[1]) for s in shards]

    def ops_fn(in_refs, out_refs, pos):
        return _halved_gather_ops(pos, in_refs, out_refs, ())

    def finish(outs, chip):
        return [lax.dynamic_update_index_in_dim(o, s, chip, 0).reshape((N_CHIPS,) + sh.shape) for o, s, sh in zip(outs, srcs, shards)]

    return (srcs, [_sds((N_CHIPS,) + s.shape, s.dtype) for s in srcs], 6 * len(shards), ops_fn), finish


def _gather_halved(shards, whole, chip, name):
    srcs = [s if a in whole or s.ndim == 3 else s.reshape(2, s.shape[0] // 2, s.shape[1]) for a, s in enumerate(shards)]
    n_sh = len(shards)
    n_ici, n_d2d = 3 * n_sh, 3 * (n_sh - len(whole))

    def body(*refs):
        in_refs, out_refs, sems = refs[:n_sh], refs[n_sh:2 * n_sh], refs[2 * n_sh:]
        pos = _mesh_pos()
        ici, d2d = _halved_gather_ops(pos, in_refs, out_refs, whole)
        _start_copies(ici, sems, pos)
        _wait_copies(ici, sems, pos)
        _start_copies(d2d, sems, pos, base=n_ici)
        _wait_copies(d2d, sems, pos, base=n_ici)

    hbm = pl.BlockSpec(memory_space=pl.ANY)
    outs = pl.pallas_call(
        body, name=name, in_specs=[hbm] * n_sh, out_specs=[hbm] * n_sh,
        out_shape=[_sds((N_CHIPS,) + s.shape, s.dtype) for s in srcs],
        scratch_shapes=[pltpu.SemaphoreType.DMA((n_ici + n_d2d,)), pltpu.SemaphoreType.DMA((n_ici + n_d2d,))],
    )(*srcs)
    return [lax.dynamic_update_index_in_dim(o, s, chip, 0).reshape((N_CHIPS,) + sh.shape)
            for o, s, sh in zip(outs, srcs, shards)]


def _split_comm(gs):
    def ops_fn(in_refs, out_refs, pos):
        return [(REL_SIBLING, src.at[1 - pos[2]], dst) for src, dst in zip(in_refs, out_refs)]

    return gs, [_sds(g.shape[1:], g.dtype) for g in gs], len(gs), ops_fn


def _exchange_comm(ps):
    def ops_fn(in_refs, out_refs, pos):
        return [(rel, src.at[_chip_of(pos, rel)], dst.at[j])
                for src, dst in zip(in_refs, out_refs) for j, rel in enumerate(REL_CHIPS)]

    return ps, [_sds((N_CHIPS - 1,) + p.shape[1:], p.dtype) for p in ps], 3 * len(ps), ops_fn


def _join_comm(hs):
    def ops_fn(in_refs, out_refs, pos):
        return [(REL_SIBLING, src, dst) for src, dst in zip(in_refs, out_refs)]

    return hs, [_sds(h.shape, h.dtype) for h in hs], len(hs), ops_fn


def _run_comm(name, comm):
    ins, shapes, n_ops, ops_fn = comm
    return _comm(name, ins, shapes, n_ops, ops_fn)


def _rope_table(t):
    pos = np.arange(t, dtype=np.float32)
    inv_freq = (np.float32(1.0) / (np.float32(ROPE_THETA) ** (np.arange(0, QK_ROPE, 2, dtype=np.float32) / np.float32(QK_ROPE)))).astype(np.float32)
    ang = (pos[:, None] * inv_freq[None, :]).astype(np.float32)
    return np.concatenate([np.cos(ang), np.cos(ang), np.sin(ang), np.sin(ang)], axis=-1).astype(np.float32)


def _rot_cols(w):
    return jnp.concatenate([-w[..., 32:], w[..., :32]], axis=-1)


def _unrot_cols(dw):
    return jnp.concatenate([dw[..., 32:], -dw[..., :32]], axis=-1)


IN_OFFS = (0, 1024, 2048, 2304, 2560, 2624, 3648, 4672)


def _w1t_from_w_in_t(wt):
    seg = [wt[IN_OFFS[i]:IN_OFFS[i + 1]] for i in range(7)]
    rnn_x, rnn_gate, cq, ckv, kr, ga, gb = seg
    return jnp.concatenate([rnn_x, rnn_gate, ga, gb, cq, ckv, kr, _rot_cols(kr.T).T], axis=0)


def _w_in_t_grad_from_parts(d_rx, d_g3, d_mla):
    kr = d_mla[512:576] + _unrot_cols(d_mla[576:640].T).T
    return jnp.concatenate([d_rx, d_g3[0:D], d_mla[0:512], kr, d_g3[D:3 * D]], axis=0)


def _wq_from_w_uq(w):
    w3 = w.reshape(Q_LORA, N_HEADS, QK_NOPE + QK_ROPE)
    rope = w3[..., QK_NOPE:]
    return jnp.concatenate([w3[..., :QK_NOPE], rope, _rot_cols(rope)], axis=-1).reshape(Q_LORA, N_HEADS * HEAD_W)


def _w_uq_grad_from_wq(dw):
    d3 = dw.reshape(Q_LORA, N_HEADS, HEAD_W)
    rope = d3[..., 128:192] + _unrot_cols(d3[..., 192:256])
    return jnp.concatenate([d3[..., :128], rope], axis=-1).reshape(Q_LORA, N_HEADS * (QK_NOPE + QK_ROPE))


def _cols_from_chunks(g):
    return g.transpose(1, 0, 2).reshape(g.shape[1], N_CHIPS * g.shape[2])


def _halves_of_col_chunks(dw):
    r, c4 = dw.shape
    return dw.reshape(2, r // 2, N_CHIPS, c4 // N_CHIPS).transpose(0, 2, 1, 3)


def _halves_of_row_chunks(dw):
    r4, c = dw.shape
    return dw.reshape(N_CHIPS, 2, r4 // (2 * N_CHIPS), c).transpose(1, 0, 2, 3)


def kernel(x, norm_mix, w_in, conv_w, conv_b, lru_wa, lru_ba, lru_wx, lru_bx, lru_lambda, q_norm, w_uq, kv_norm, w_ukv, w_out, norm_mlp, w_up, w_down, norm_final, loss_target, m_norm_mix, m_w_in, m_conv_w, m_conv_b, m_lru_wa, m_lru_ba, m_lru_wx, m_lru_bx, m_lru_lambda, m_q_norm, m_w_uq, m_kv_norm, m_w_ukv, m_w_out, m_norm_mlp, m_w_up, m_w_down, m_norm_final, v_norm_mix, v_w_in, v_conv_w, v_conv_b, v_lru_wa, v_lru_ba, v_lru_wx, v_lru_bx, v_lru_lambda, v_q_norm, v_w_uq, v_kv_norm, v_w_ukv, v_w_out, v_norm_mlp, v_w_up, v_w_down, v_norm_final):
    t = x.shape[1]
    tm = min(512, t)
    tb = min(256, t)
    tq = min(512, max(tm, t // 4))
    x2 = x[0]
    target = loss_target[0]
    chip = 2 * lax.axis_index("x") + lax.axis_index("y")
    core = lax.axis_index("c")
    chip_ix, core_ix = chip.reshape(1).astype(jnp.int32), core.reshape(1).astype(jnp.int32)
    row = lambda p: p.reshape(1, -1)

    big_shards = (w_in, w_uq, w_ukv, w_out, w_up, w_down)
    w_in_t = w_in.T
    col_halves = lambda z: jnp.stack([z[:, :z.shape[1] // 2], z[:, z.shape[1] // 2:]])
    w_in_g, conv_w_g = _gather_halved([col_halves(w_in_t.astype(BF16)), conv_w], (1,), chip, "weight_gather_first")
    w1 = _w1t_from_w_in_t(jnp.concatenate([w_in_g[:, 0], w_in_g[:, 1]], axis=-1).reshape(IN_OFFS[-1], D))
    conv_w_f = _cols_from_chunks(conv_w_g)
    wa_b, wx_b = lru_wa.astype(BF16), lru_wx.astype(BF16)

    rope_c = jnp.asarray(_rope_table(t))

    comm_a, finish_a = _halved_gather_comm([w.astype(BF16) for w in (w_uq, w_ukv, w_out)])
    (xn, rx, g3, cq, ckv, kr), gathered = _inproj(x2, row(norm_mix), w1, tm, comm=comm_a)
    g_uq, g_ukv, g_out = finish_a(gathered, chip)
    wq = _wq_from_w_uq(_cols_from_chunks(g_uq))
    wkv = _cols_from_chunks(g_ukv)
    w_out_f = g_out.reshape(D, D)
    comm_b, finish_b = _halved_gather_comm([w.astype(BF16) for w in (w_up, w_down)])
    (h, xa), gathered = _lru_fwd(rx, conv_w_f, row(conv_b), wa_b, row(lru_ba), wx_b, row(lru_bx), row(lru_lambda), tb, comm=comm_b)
    g_up, g_down = finish_b(gathered, chip)
    w_up_f = _cols_from_chunks(g_up)
    w_down_f = g_down.reshape(D_FF, D)
    q, k, v, cqn, ckvn = _mla_proj(cq, ckv, kr, row(q_norm), row(kv_norm), wq, wkv, rope_c, tm)
    nq = t // tq
    yb, lse = _flash_fwd(q, k, v, tq)
    h1, merged = _merge_out(x2, h, g3, yb, w_out_f, tm)
    u, n2 = _mlp_up(h1, row(norm_mlp), w_up_f, tm)
    act, dh2, loss_blk, g_norm_final = _mlp_down_loss(u, h1, target, w_down_f, row(norm_final), tm)

    g_w_down = _matmul_tn(act, dh2, "grad_w_down", "rows")
    du = _mlp_bwd_act(dh2, u, w_down_f, tm)
    dh1, g_norm_mlp = _mlp_bwd_in(du, dh2, h1, w_up_f, row(norm_mlp), tm)
    g_w_up = _matmul_tn(n2, du, "grad_w_up", "cols")
    dg3, dyb, delta, dh, g_w_out = _merge_bwd(dh1, w_out_f, g3, h, yb, merged, tm)
    delta = delta.reshape(N_HEADS, 8, nq, tq).swapaxes(1, 2)
    def pair_sums(hvs, theirs, dtypes, tag):
        return [_pair_sum(hv.reshape(2, -1, hv.shape[-1]), r.reshape(-1, r.shape[-1]), core_ix, dt, f"grad_pair_sum_{tag}{a}").reshape(r.shape)
                for a, (hv, r, dt) in enumerate(zip(hvs, theirs, dtypes))]

    def chip_sums(parts, received, tag):
        return [_chip_sum(p, r, chip_ix, f"grad_chip_sum_{tag}{a}") for a, (p, r) in enumerate(zip(parts, received))]

    early = [_halves_of_row_chunks(g_w_out), g_w_up, g_w_down]
    (dq, dk, dv), early_theirs = _flash_bwd(q, k, v, dyb, lse, delta, tq, comm=_split_comm(early))
    early_parts = pair_sums(early, early_theirs, [BF16] * 3, "early")
    dmla, g_wq, g_wkv, g_q_norm, g_kv_norm = _mla_bwd(dq, dk, dv, cqn, ckvn, cq, ckv, rope_c, wq, wkv, row(q_norm), row(kv_norm), tm)
    (drx, g_conv_w, g_conv_b, g_wa, g_ba, g_wx, g_bx, g_lam), early_received = _lru_bwd(
        dh, xa, h, rx, conv_w_f, wa_b, row(lru_ba), wx_b, row(lru_bx), row(lru_lambda), tb, comm=_exchange_comm(early_parts))
    early_reduced = chip_sums(early_parts, early_received, "early")
    grad_x, g_norm_mix = _inproj_bwd(x2, dh1, drx, dg3, dmla, w1, row(norm_mix), tm)
    g_w_in_gates, early_sibling = _matmul_tn(dg3, xn, "grad_w_in_gates", comm=_join_comm(early_reduced))
    g_w_in_t = _w_in_t_grad_from_parts(_matmul_tn(drx, xn, "grad_w_in_rx"), g_w_in_gates, _matmul_tn(dmla, xn, "grad_w_in_mla"))
    g_w_uq = _w_uq_grad_from_wq(g_wq)

    smalls = (g_norm_mix, g_conv_b, g_wa, g_ba, g_wx, g_bx, g_lam, g_q_norm, g_kv_norm, g_norm_mlp, g_norm_final, g_conv_w)
    s_flat = jnp.concatenate([s.reshape(-1) for s in smalls] + [loss_blk[0, 0:1], jnp.zeros((S_LEN - N_SMALL - CONVW_SIZE - 1,), F32)])
    g_w_in_halves = lax.optimization_barrier(col_halves(g_w_in_t))
    late = [g_w_in_halves.reshape(2, N_CHIPS, IN_OFFS[-1] // N_CHIPS, D // 2), _halves_of_col_chunks(g_w_uq), _halves_of_col_chunks(g_wkv),
            s_flat.reshape(N_CHIPS, 2, S_ROWS_HALF, 128).transpose(1, 0, 2, 3)]
    late_theirs = _run_comm("grad_sibling_split", _split_comm(late))
    late_parts = pair_sums(late, late_theirs, [BF16] * 3 + [F32], "late")
    late_reduced = chip_sums(late_parts, _run_comm("grad_chip_exchange", _exchange_comm(late_parts)), "late")
    late_sibling = _run_comm("grad_sibling_join", _join_comm(late_reduced))
    reduced = late_reduced[:3] + early_reduced
    reduced_sibling = list(late_sibling[:3]) + list(early_sibling)
    s_mine, s_theirs = late_reduced[3], late_sibling[3]
    s_chunk = jnp.where(core == 0, jnp.concatenate([s_mine, s_theirs]), jnp.concatenate([s_theirs, s_mine]))
    s_all = _gather_chips([s_chunk], chip, "small_grad_gather")[0].reshape(-1)

    small_grads = []
    off = 0
    for shp, n in zip(SMALL_SHAPES, SMALL_SIZES):
        small_grads.append(s_all[off:off + n].reshape(shp))
        off += n
    g_conv_w_mine = lax.dynamic_slice_in_dim(s_all[off:off + CONVW_SIZE].reshape(4, D), chip * (D // N_CHIPS), D // N_CHIPS, axis=1)
    loss = s_all[off + CONVW_SIZE]

    big_m = (m_w_in, m_w_uq, m_w_ukv, m_w_out, m_w_up, m_w_down)
    big_v = (v_w_in, v_w_uq, v_w_ukv, v_w_out, v_w_up, v_w_down)
    big_names = ("w_in", "w_uq", "w_ukv", "w_out", "w_up", "w_down")
    big_upd = [_adamw_halves(w, gm, gt, m, v, core_ix, "adamw_" + n)
               for w, gm, gt, m, v, n in zip(big_shards[1:], reduced[1:], reduced_sibling[1:], big_m[1:], big_v[1:], big_names[1:])]
    w_in_upd = _adamw_halves(w_in_t, reduced[0], reduced_sibling[0], m_w_in.T, v_w_in.T, core_ix, "adamw_w_in", by_cols=True)
    big_upd = [[u.T for u in w_in_upd]] + big_upd

    small_w = (norm_mix, conv_b, lru_wa, lru_ba, lru_wx, lru_bx, lru_lambda, q_norm, kv_norm, norm_mlp, norm_final)
    small_m = (m_norm_mix, m_conv_b, m_lru_wa, m_lru_ba, m_lru_wx, m_lru_bx, m_lru_lambda, m_q_norm, m_kv_norm, m_norm_mlp, m_norm_final)
    small_v = (v_norm_mix, v_conv_b, v_lru_wa, v_lru_ba, v_lru_wx, v_lru_bx, v_lru_lambda, v_q_norm, v_kv_norm, v_norm_mlp, v_norm_final)

    def pack(items, last, fill):
        flat = jnp.concatenate([i.reshape(-1) for i in items] + [last.reshape(-1)])
        return jnp.concatenate([flat, jnp.full((PACK_ROWS * 128 - flat.shape[0],), fill, F32)]).reshape(PACK_ROWS, 128)

    packed = _adamw(pack(small_w, conv_w, 0.0), pack(small_grads, g_conv_w_mine, 0.0), pack(small_m, m_conv_w, 0.0),
                    pack(small_v, v_conv_w, 1.0), "adamw_small")

    def unpack(p):
        flat = p.reshape(-1)
        outs, o = [], 0
        for shp, n in zip(SMALL_SHAPES, SMALL_SIZES):
            outs.append(flat[o:o + n].reshape(shp))
            o += n
        return outs, flat[o:o + CONVW_SIZE // N_CHIPS].reshape(4, D // N_CHIPS)

    order = ("norm_mix", "w_in", "conv_w", "conv_b", "lru_wa", "lru_ba", "lru_wx", "lru_bx", "lru_lambda", "q_norm", "w_uq", "kv_norm",
             "w_ukv", "w_out", "norm_mlp", "w_up", "w_down", "norm_final")

    def assemble(small_list, conv_w_item, big_list):
        table = dict(zip(SMALL_NAMES, small_list))
        table["conv_w"] = conv_w_item
        table.update(zip(big_names, big_list))
        return [table[n] for n in order]

    outs = [loss, grad_x.reshape(1, t, D)]
    outs += assemble(small_grads, g_conv_w_mine, [b[0] for b in big_upd])
    for j in range(3):
        sm, cw = unpack(packed[j])
        outs += assemble(sm, cw, [b[j + 1] for b in big_upd])
    return tuple(outs)
```

```python
import functools
import math

import jax
import jax.numpy as jnp
import numpy as np
from jax import lax
from jax.experimental import pallas as pl
from jax.experimental.pallas import tpu as pltpu

F32 = jnp.float32
BF16 = jnp.bfloat16

D = 1024
N_HEADS = 8
QK_NOPE = 128
QK_ROPE = 64
V_HEAD = 128
Q_LORA = 256
KV_LORA = 256
D_FF = 4096
RNN_BLOCKS = 8
RNN_BW = 128
LRU_C = 8.0
EPS = 1e-6
ROPE_THETA = 10000.0
HEAD_W = 256
KR_W = 128
W1_COLS = 4 * D + Q_LORA + KV_LORA + KR_W
SM_SCALE = (QK_NOPE + QK_ROPE) ** -0.5
EXP2_SCALE = SM_SCALE * math.log2(math.e)
NEG = float(jnp.finfo(jnp.float32).min)

ADAM_LR = 0.001
ADAM_B1 = 0.9
ADAM_B2 = 0.999
ADAM_EPS = 1e-08
ADAM_WD = 0.01
ADAM_STEP = 10

N_CHIPS = 4
V7X_VMEM_LIMIT = 56 * 1024 * 1024
MESH_ID = pl.DeviceIdType.MESH

SMALL_NAMES = ("norm_mix", "conv_b", "lru_wa", "lru_ba", "lru_wx", "lru_bx", "lru_lambda", "q_norm", "kv_norm", "norm_mlp", "norm_final")
SMALL_SHAPES = ((D,), (D,), (RNN_BLOCKS, RNN_BW, RNN_BW), (RNN_BLOCKS, RNN_BW), (RNN_BLOCKS, RNN_BW, RNN_BW), (RNN_BLOCKS, RNN_BW), (D,),
                (Q_LORA,), (KV_LORA,), (D,), (D,))
SMALL_SIZES = tuple(math.prod(s) for s in SMALL_SHAPES)
N_SMALL = sum(SMALL_SIZES)
CONVW_SIZE = 4 * D
S_LEN = -(-(N_SMALL + CONVW_SIZE) // 8192) * 8192
S_ROWS_HALF = S_LEN // (N_CHIPS * 2 * 128)
PACK_ROWS = -(-(N_SMALL + CONVW_SIZE // N_CHIPS) // (256 * 128)) * 256


def _pcall(body, name, grid, in_specs, out_specs, out_shape, scratch=(), comm=None):
    params = pltpu.CompilerParams(dimension_semantics=("arbitrary",) * len(grid), vmem_limit_bytes=V7X_VMEM_LIMIT)
    if comm is None:
        return pl.pallas_call(body, name=name, grid=grid, in_specs=in_specs, out_specs=out_specs, out_shape=out_shape,
                              scratch_shapes=list(scratch), compiler_params=params)
    c_ins, c_shapes, n_ops, ops_fn = comm
    single = not isinstance(out_specs, (list, tuple))
    out_specs, out_shape = ([out_specs], [out_shape]) if single else (list(out_specs), list(out_shape))
    n_in, n_out, n_sc, n_ci, n_co = len(in_specs), len(out_specs), len(scratch), len(c_ins), len(c_shapes)

    def wrapped(*refs):
        ins, refs = refs[:n_in], refs[n_in:]
        c_in_refs, refs = refs[:n_ci], refs[n_ci:]
        outs, refs = refs[:n_out], refs[n_out:]
        c_out_refs, refs = refs[:n_co], refs[n_co:]
        own_scratch, sems = refs[:n_sc], refs[n_sc:]
        pos = _mesh_pos()
        ops = ops_fn(c_in_refs, c_out_refs, pos)
        ops, then = ops if isinstance(ops, tuple) else (ops, [])
        assert len(ops) + len(then) == n_ops
        first, last = True, True
        for d, n in enumerate(grid):
            first = first & (pl.program_id(d) == 0)
            last = last & (pl.program_id(d) == n - 1)

        @pl.when(first)
        def _():
            _start_copies(ops, sems, pos)

        body(*ins, *outs, *own_scratch)

        @pl.when(last)
        def _():
            _wait_copies(ops, sems, pos)
            _start_copies(then, sems, pos, base=len(ops))
            _wait_copies(then, sems, pos, base=len(ops))

    hbm = pl.BlockSpec(memory_space=pl.ANY)
    call = pl.pallas_call(
        wrapped, name=name, grid=grid, in_specs=list(in_specs) + [hbm] * n_ci, out_specs=out_specs + [hbm] * n_co,
        out_shape=out_shape + list(c_shapes),
        scratch_shapes=list(scratch) + [pltpu.SemaphoreType.DMA((n_ops,)), pltpu.SemaphoreType.DMA((n_ops,))],
        compiler_params=params)

    def run(*operands):
        res = call(*operands, *c_ins)
        own = res[0] if single else res[:n_out]
        return own, res[n_out:]

    return run


def _rows(tm, w):
    return pl.BlockSpec((tm, w), lambda i: (i, 0))


def _full(*shape):
    return pl.BlockSpec(shape, lambda *_: (0,) * len(shape))


def _sds(shape, dtype=F32):
    return jax.ShapeDtypeStruct(shape, dtype)


def _row_tile(rows, cap=256, mult=8):
    t = min(rows, cap)
    while rows % t or t % mult:
        t -= 1
    return t


def _dot(a, b):
    return jnp.dot(a, b, preferred_element_type=F32)


def _dot_nt(a, b):
    return lax.dot_general(a, b, (((1,), (1,)), ((), ())), preferred_element_type=F32)


def _dot_tn(a, b):
    return lax.dot_general(a, b, (((0,), (0,)), ((), ())), preferred_element_type=F32)


def _sigmoid(x):
    return 1.0 / (1.0 + jnp.exp(-x))


_GELU_C = math.sqrt(2.0 / math.pi)


def _gelu(x):
    return x * (0.5 * (1.0 + jnp.tanh(_GELU_C * (x + 0.044715 * (x * x * x)))))


def _gelu_grad(x):
    t = jnp.tanh(_GELU_C * (x + 0.044715 * (x * x * x)))
    cdf = 0.5 * (1.0 + t)
    return cdf + x * (0.5 * (1.0 - t * t) * _GELU_C * (1.0 + 3.0 * 0.044715 * (x * x)))


def _rms_scale(x):
    return lax.rsqrt(jnp.mean(x * x, axis=-1, keepdims=True) + EPS)


def _rms_bwd(x, rs, g, dy):
    gdy = dy * g
    dx = rs * gdy - x * ((rs * rs * rs) * jnp.mean(gdy * x, axis=-1, keepdims=True))
    return dx, dy * (x * rs)


def _log1p(e):
    u = 1.0 + e
    d = u - 1.0
    return jnp.where(d == 0.0, e, jnp.log(u) * (e / jnp.where(d == 0.0, 1.0, d)))


def _softplus(y):
    return jnp.maximum(y, 0.0) + _log1p(jnp.exp(-jnp.abs(y)))


def _expm1(x):
    u = jnp.exp(x)
    lu = jnp.log(u)
    safe = jnp.where((u == 1.0) | (u == 0.0), 1.0, lu)
    return jnp.where(u == 1.0, x, jnp.where(u == 0.0, -1.0, (u - 1.0) * (x / safe)))


def _row_iota(shape):
    return lax.broadcasted_iota(jnp.int32, shape, 0)


def _lane_iota(shape):
    return lax.broadcasted_iota(jnp.int32, shape, 1)


def _scan_groups_fwd(a, b):
    sub = lax.broadcasted_iota(jnp.int32, a.shape, 1)
    for sh in (1, 2, 4):
        m = sub >= sh
        b = jnp.where(m, a * pltpu.roll(b, sh, 1) + b, b)
        a = jnp.where(m, a * pltpu.roll(a, sh, 1), a)
    return a, b


def _scan_groups_bwd(c, b):
    sub = lax.broadcasted_iota(jnp.int32, c.shape, 1)
    for sh in (1, 2, 4):
        m = sub < 8 - sh
        b = jnp.where(m, b + c * pltpu.roll(b, 8 - sh, 1), b)
        c = jnp.where(m, c * pltpu.roll(c, 8 - sh, 1), c)
    return c, b


def _rope_pair(gc):
    return gc + pltpu.roll(gc, 64, 1)


def _inproj(x, g, w1, tm, comm=None):
    t = x.shape[0]
    widths = (D, 3 * D, Q_LORA, KV_LORA, KR_W)

    def body(x_ref, g_ref, w_ref, xn_ref, rx_ref, g3_ref, cq_ref, ckv_ref, kr_ref):
        xv = x_ref[...]
        xn = (xv * _rms_scale(xv) * g_ref[...]).astype(BF16)
        xn_ref[...] = xn
        col = 0
        for ref, w in zip((rx_ref, g3_ref, cq_ref, ckv_ref, kr_ref), widths):
            for c0 in range(0, w, 512):
                cw = min(512, w - c0)
                ref[:, c0:c0 + cw] = _dot_nt(xn, w_ref[col + c0:col + c0 + cw, :])
            col += w

    return _pcall(
        body, "inproj", (t // tm,),
        [_rows(tm, D), _full(1, D), _full(W1_COLS, D)],
        [_rows(tm, D)] + [_rows(tm, w) for w in widths],
        [_sds((t, D), BF16)] + [_sds((t, w)) for w in widths],
        comm=comm,
    )(x, g, w1)


def _lru_gates(xa, wa_ref, ba, wx_ref, bx, pre_r, pre_i):
    xb = xa.astype(BF16)
    for n in range(RNN_BLOCKS):
        sl = slice(n * RNN_BW, (n + 1) * RNN_BW)
        pre_r[:, sl] = _dot(xb[:, sl], wa_ref[n])
        pre_i[:, sl] = _dot(xb[:, sl], wx_ref[n])
    r = _sigmoid(pre_r[...] + ba)
    i = _sigmoid(pre_i[...] + bx)
    return r, i


def _lru_fwd(rx, conv_w, conv_b, wa, ba, wx, bx, lam, tb, comm=None):
    t = rx.shape[0]
    nb = t // tb

    def body(x_ref, xp_ref, cw_ref, cb_ref, wa_ref, ba_ref, wx_ref, bx_ref, lam_ref, h_ref, xa_ref, hc, tmp, pre_r, pre_i):
        i_blk = pl.program_id(0)

        @pl.when(i_blk == 0)
        def _():
            hc[...] = jnp.zeros_like(hc)

        xv = x_ref[...]
        xp = jnp.where(i_blk > 0, xp_ref[...], 0.0)
        row8 = _row_iota((8, D))
        xa = cb_ref[...] + cw_ref[3:4, :] * xv
        for s in (1, 2, 3):
            xr = pltpu.roll(xv, s, 0)
            tmp[...] = xr
            tmp[0:8, :] = jnp.where(row8 < s, pltpu.roll(xp, s, 0), xr[0:8, :])
            xa = xa + cw_ref[3 - s:4 - s, :] * tmp[...]
        xa_ref[...] = xa
        r, gi = _lru_gates(xa, wa_ref, ba_ref[...], wx_ref, bx_ref[...], pre_r, pre_i)
        la = (-LRU_C * _softplus(-lam_ref[...])) * r
        a = jnp.exp(la)
        b = jnp.sqrt(-_expm1(2.0 * la)) * (gi * xa)
        a3, b3 = _scan_groups_fwd(a.reshape(tb // 8, 8, D), b.reshape(tb // 8, 8, D))
        carry = hc[...]
        for grp in range(tb // 8):
            hg = a3[grp] * carry + b3[grp]
            h_ref[8 * grp:8 * grp + 8, :] = hg
            carry = hg[7:8, :]
        hc[...] = carry

    prev8 = pl.BlockSpec((8, D), lambda i: (jnp.maximum(i * (tb // 8) - 1, 0), 0))
    return _pcall(
        body, "lru_fwd", (nb,),
        [_rows(tb, D), prev8, _full(4, D), _full(1, D), _full(RNN_BLOCKS, RNN_BW, RNN_BW), _full(1, D),
         _full(RNN_BLOCKS, RNN_BW, RNN_BW), _full(1, D), _full(1, D)],
        [_rows(tb, D), _rows(tb, D)],
        [_sds((t, D)), _sds((t, D))],
        scratch=[pltpu.VMEM((1, D), F32), pltpu.VMEM((tb, D), F32), pltpu.VMEM((tb, D), F32), pltpu.VMEM((tb, D), F32)],
        comm=comm,
    )(rx, rx, conv_w, conv_b, wa, ba, wx, bx, lam)


def _mla_proj(cq, ckv, kr, qn, kvn, wq, wkv, rope_c, tm):
    t = cq.shape[0]

    def body(cq_ref, ckv_ref, kr_ref, qn_ref, kvn_ref, wq_ref, wkv_ref, c_ref, q_ref, k_ref, v_ref, cqn_ref, ckvn_ref):
        cqv = cq_ref[...]
        cqn = (cqv * _rms_scale(cqv) * qn_ref[...]).astype(BF16)
        ckvv = ckv_ref[...]
        ckvn = (ckvv * _rms_scale(ckvv) * kvn_ref[...]).astype(BF16)
        cqn_ref[...] = cqn
        ckvn_ref[...] = ckvn
        c = c_ref[...]
        lane = _lane_iota((tm, KR_W))
        kro = jnp.where(lane < 64, _rope_pair(kr_ref[...] * c), 0.0).astype(BF16)
        for h in range(N_HEADS):
            sl = slice(h * HEAD_W, (h + 1) * HEAD_W)
            qh = _dot(cqn, wq_ref[:, sl])
            q_ref[h, :, 0:128] = (qh[:, 0:128] * EXP2_SCALE).astype(BF16)
            q_ref[h, :, 128:256] = (_rope_pair(qh[:, 128:256] * c) * EXP2_SCALE).astype(BF16)
            kvh = _dot(ckvn, wkv_ref[:, sl])
            k_ref[h, :, 0:128] = kvh[:, 0:128].astype(BF16)
            k_ref[h, :, 128:256] = kro
            v_ref[h, :, 0:V_HEAD] = kvh[:, 128:256].astype(BF16)
            v_ref[h, :, V_HEAD:2 * V_HEAD] = jnp.ones((tm, V_HEAD), BF16)

    hb = lambda w: pl.BlockSpec((N_HEADS, tm, w), lambda i: (0, i, 0))
    return _pcall(
        body, "mla_proj", (t // tm,),
        [_rows(tm, Q_LORA), _rows(tm, KV_LORA), _rows(tm, KR_W), _full(1, Q_LORA), _full(1, KV_LORA),
         _full(Q_LORA, N_HEADS * HEAD_W), _full(KV_LORA, N_HEADS * HEAD_W), _rows(tm, KR_W)],
        [hb(HEAD_W), hb(HEAD_W), hb(2 * V_HEAD), _rows(tm, Q_LORA), _rows(tm, KV_LORA)],
        [_sds((N_HEADS, t, HEAD_W), BF16), _sds((N_HEADS, t, HEAD_W), BF16), _sds((N_HEADS, t, 2 * V_HEAD), BF16),
         _sds((t, Q_LORA), BF16), _sds((t, KV_LORA), BF16)],
    )(cq, ckv, kr, qn, kvn, wq, wkv, rope_c)


def _flash_fwd(q, k, v, tq, comm=None):
    t = q.shape[1]
    nq = t // tq

    def body(q_ref, k_ref, v_ref, o_ref, lse_ref, s_even, s_odd):
        qi = pl.program_id(1)
        qv = q_ref[0]

        def scores(ki, buf):
            buf[...] = _dot_nt(qv, k_ref[0, pl.ds(pl.multiple_of(ki * tq, tq), tq), :])

        def softmax_pv(ki, buf, carry, diagonal):
            m, acc = carry
            s = buf[...]
            if diagonal:
                s = jnp.where(_row_iota((tq, tq)) >= _lane_iota((tq, tq)), s, NEG)
            m_new = jnp.maximum(m, jnp.max(s, axis=1, keepdims=True))
            p = jnp.exp2(s - m_new)
            alpha = jnp.exp2(m - m_new)
            acc = alpha * acc + _dot(p.astype(BF16), v_ref[0, pl.ds(pl.multiple_of(ki * tq, tq), tq), :])
            return m_new, acc

        def finish(carry):
            m, acc = carry
            l = acc[:, V_HEAD:2 * V_HEAD]
            o_ref[...] = acc[:, 0:V_HEAD] / l
            lse = m + jnp.log(l) * math.log2(math.e)
            lse_ref[0, 0] = jnp.transpose(lse)[0:8, :]

        def two(i, carry):
            scores(2 * i + 1, s_odd)
            carry = softmax_pv(2 * i, s_even, carry, False)
            scores(2 * i + 2, s_even)
            return softmax_pv(2 * i + 1, s_odd, carry, False)

        init = (jnp.full((tq, 1), -jnp.inf, F32), jnp.zeros((tq, 2 * V_HEAD), F32))
        scores(0, s_even)
        carry = lax.fori_loop(0, qi // 2, two, init)

        @pl.when(qi % 2 == 0)
        def _():
            finish(softmax_pv(qi, s_even, carry, True))

        @pl.when(qi % 2 == 1)
        def _():
            scores(qi, s_odd)
            finish(softmax_pv(qi, s_odd, softmax_pv(qi - 1, s_even, carry, False), True))

    head = lambda w: pl.BlockSpec((1, t, w), lambda h, qi: (h, 0, 0))
    return _pcall(
        body, "flash_fwd", (N_HEADS, nq),
        [pl.BlockSpec((1, tq, HEAD_W), lambda h, qi: (h, qi, 0)), head(HEAD_W), head(2 * V_HEAD)],
        [pl.BlockSpec((tq, V_HEAD), lambda h, qi: (qi, h)), pl.BlockSpec((1, 1, 8, tq), lambda h, qi: (h, qi, 0, 0))],
        [_sds((t, D)), _sds((N_HEADS, nq, 8, tq))],
        scratch=[pltpu.VMEM((tq, tq), F32), pltpu.VMEM((tq, tq), F32)],
        comm=comm,
    )(q, k, v)


def _merge_out(x, h, g3, yb, w_out, tm):
    t = x.shape[0]

    def body(x_ref, h_ref, g3_ref, yb_ref, w_ref, h1_ref, mg_ref):
        ya = h_ref[...] * _gelu(g3_ref[:, 0:D])
        merged = (_sigmoid(g3_ref[:, D:2 * D]) * ya + _sigmoid(g3_ref[:, 2 * D:3 * D]) * yb_ref[...]).astype(BF16)
        mg_ref[...] = merged
        h1_ref[...] = x_ref[...] + _dot(merged, w_ref[...])

    return _pcall(
        body, "merge_out", (t // tm,),
        [_rows(tm, D), _rows(tm, D), _rows(tm, 3 * D), _rows(tm, D), _full(D, D)],
        [_rows(tm, D), _rows(tm, D)],
        [_sds((t, D)), _sds((t, D), BF16)],
    )(x, h, g3, yb, w_out)


def _mlp_up(h1, g, w_up, tm):
    t = h1.shape[0]

    def body(h_ref, g_ref, w_ref, u_ref, n2_ref):
        hv = h_ref[...]
        n2 = (hv * _rms_scale(hv) * g_ref[...]).astype(BF16)
        n2_ref[...] = n2
        for c0 in range(0, D_FF, 512):
            u_ref[:, c0:c0 + 512] = _dot(n2, w_ref[:, c0:c0 + 512])

    return _pcall(
        body, "mlp_up", (t // tm,),
        [_rows(tm, D), _full(1, D), _full(D, D_FF)],
        [_rows(tm, D_FF), _rows(tm, D)],
        [_sds((t, D_FF)), _sds((t, D), BF16)],
    )(h1, g, w_up)


def _mlp_down_loss(u, h1, target, w_down, g, tm):
    t = u.shape[0]

    def body(u_ref, h1_ref, tg_ref, w_ref, g_ref, act_ref, dh2_ref, loss_ref, gnf_ref, lacc):
        i = pl.program_id(0)

        @pl.when(i == 0)
        def _():
            lacc[...] = jnp.zeros_like(lacc)
            gnf_ref[...] = jnp.zeros_like(gnf_ref)

        ru = jnp.maximum(u_ref[...], 0.0)
        act = (ru * ru).astype(BF16)
        act_ref[...] = act
        h2 = h1_ref[...] + _dot(act, w_ref[...])
        rs = _rms_scale(h2)
        gv = g_ref[...]
        err = h2 * rs * gv - tg_ref[...]
        lacc[...] += jnp.sum(err * err, axis=0, keepdims=True)
        dx, dgr = _rms_bwd(h2, rs, gv, err * (1.0 / D))
        dh2_ref[...] = dx
        gnf_ref[...] += jnp.sum(dgr, axis=0, keepdims=True)

        @pl.when(i == pl.num_programs(0) - 1)
        def _():
            loss_ref[...] = jnp.broadcast_to(jnp.sum(lacc[...], axis=1, keepdims=True) * (0.5 / D), (8, 128))

    return _pcall(
        body, "mlp_down_loss", (t // tm,),
        [_rows(tm, D_FF), _rows(tm, D), _rows(tm, D), _full(D_FF, D), _full(1, D)],
        [_rows(tm, D_FF), _rows(tm, D), _full(8, 128), _full(1, D)],
        [_sds((t, D_FF), BF16), _sds((t, D)), _sds((8, 128)), _sds((1, D))],
        scratch=[pltpu.VMEM((1, D), F32)],
    )(u, h1, target, w_down, g)


def _matmul_tn(a, g, name, chunked=None, comm=None):
    t, kdim = a.shape
    ndim = g.shape[1]
    tk = min(kdim, 1024)
    tn = ndim if ndim <= 1024 else 1024
    if chunked == "cols":
        assert tk == kdim and tn == ndim // N_CHIPS
    elif chunked == "rows":
        assert tk == kdim // N_CHIPS and tn == ndim
    tt = min(t, 4096 if a.dtype == BF16 and g.dtype == BF16 else 2048)
    nt = t // tt

    def body(a_ref, g_ref, o_ref):
        @pl.when(pl.program_id(2) == 0)
        def _():
            o_ref[...] = jnp.zeros_like(o_ref)

        o_ref[...] += _dot_tn(a_ref[...].astype(BF16), g_ref[...].astype(BF16)).reshape(o_ref.shape)

    if chunked is not None:
        out_spec = pl.BlockSpec((2, None, tk // 2, tn), lambda i, j, s: (0, i + j, 0, 0))
        out_shape = _sds((2, N_CHIPS, tk // 2, tn))
    else:
        out_spec, out_shape = pl.BlockSpec((tk, tn), lambda i, j, s: (i, j)), _sds((kdim, ndim))
    return _pcall(
        body, name, (kdim // tk, ndim // tn, nt),
        [pl.BlockSpec((tt, tk), lambda i, j, s: (s, i)), pl.BlockSpec((tt, tn), lambda i, j, s: (s, j))],
        out_spec, out_shape, comm=comm,
    )(a, g)


def _mlp_bwd_act(dh2, u, w_down, tm):
    t = u.shape[0]

    def body(d_ref, u_ref, w_ref, du_ref):
        db = d_ref[...].astype(BF16)
        for c0 in range(0, D_FF, 512):
            da = _dot_nt(db, w_ref[c0:c0 + 512, :])
            du_ref[:, c0:c0 + 512] = (da * (2.0 * jnp.maximum(u_ref[:, c0:c0 + 512], 0.0))).astype(BF16)

    return _pcall(
        body, "mlp_bwd_act", (t // tm,),
        [_rows(tm, D), _rows(tm, D_FF), _full(D_FF, D)],
        _rows(tm, D_FF), _sds((t, D_FF), BF16),
    )(dh2, u, w_down)


def _mlp_bwd_in(du, dh2, h1, w_up, g, tm):
    t = du.shape[0]

    def body(du_ref, d_ref, h_ref, w_ref, g_ref, dh1_ref, gacc_ref):
        @pl.when(pl.program_id(0) == 0)
        def _():
            gacc_ref[...] = jnp.zeros_like(gacc_ref)

        dn2 = _dot_nt(du_ref[...], w_ref[...])
        hv = h_ref[...]
        dx, dgr = _rms_bwd(hv, _rms_scale(hv), g_ref[...], dn2)
        dh1_ref[...] = d_ref[...] + dx
        gacc_ref[...] += jnp.sum(dgr, axis=0, keepdims=True)

    return _pcall(
        body, "mlp_bwd_in", (t // tm,),
        [_rows(tm, D_FF), _rows(tm, D), _rows(tm, D), _full(D, D_FF), _full(1, D)],
        [_rows(tm, D), _full(1, D)],
        [_sds((t, D)), _sds((1, D))],
    )(du, dh2, h1, w_up, g)


def _merge_bwd(dh1, w_out, g3, h, yb, merged, tm):
    t = dh1.shape[0]

    def body(d_ref, w_ref, g3_ref, h_ref, yb_ref, mg_ref, dg3_ref, dyb_ref, dl_ref, dh_ref, dwo_ref):
        @pl.when(pl.program_id(0) == 0)
        def _():
            dwo_ref[...] = jnp.zeros_like(dwo_ref)

        db = d_ref[...].astype(BF16)
        dwo_ref[...] += _dot_tn(mg_ref[...], db)
        dm = _dot_nt(db, w_ref[...])
        gv = g3_ref[:, 0:D]
        sa = _sigmoid(g3_ref[:, D:2 * D])
        sb = _sigmoid(g3_ref[:, 2 * D:3 * D])
        gel = _gelu(gv)
        hv = h_ref[...]
        ybv = yb_ref[...]
        dya = dm * sa
        dyb = dm * sb
        dg3_ref[:, 0:D] = (dya * hv * _gelu_grad(gv)).astype(BF16)
        dg3_ref[:, D:2 * D] = (dya * (hv * gel) * (1.0 - sa)).astype(BF16)
        dg3_ref[:, 2 * D:3 * D] = (dyb * ybv * (1.0 - sb)).astype(BF16)
        dh_ref[...] = dya * gel
        dyb_ref[...] = dyb.astype(BF16)
        prod = dyb * ybv
        ones = jnp.ones((8, V_HEAD), F32)
        for hh in range(N_HEADS):
            dl_ref[hh] = lax.dot_general(ones, prod[:, hh * V_HEAD:(hh + 1) * V_HEAD], (((1,), (1,)), ((), ())),
                                         precision=lax.Precision.HIGHEST, preferred_element_type=F32)

    return _pcall(
        body, "merge_bwd", (t // tm,),
        [_rows(tm, D), _full(D, D), _rows(tm, 3 * D), _rows(tm, D), _rows(tm, D), _rows(tm, D)],
        [_rows(tm, 3 * D), _rows(tm, D), pl.BlockSpec((N_HEADS, 8, tm), lambda i: (0, 0, i)), _rows(tm, D), _full(D, D)],
        [_sds((t, 3 * D), BF16), _sds((t, D), BF16), _sds((N_HEADS, 8, t)), _sds((t, D)), _sds((D, D))],
    )(dh1, w_out, g3, h, yb, merged)


def _flash_bwd(q, k, v, do, lse, delta, tq, comm=None):
    t = q.shape[1]
    nq = t // tq

    def body(q_ref, k_ref, v_ref, do_ref, lse_ref, dl_ref, dqt_ref, dk_ref, dv_ref):
        ki = pl.program_id(1)

        @pl.when(ki == 0)
        def _():
            dqt_ref[...] = jnp.zeros_like(dqt_ref)

        kblk, vblk = k_ref[0], v_ref[0]
        kt = jnp.transpose(kblk)
        dk_ref[...] = jnp.zeros_like(dk_ref)
        dv_ref[...] = jnp.zeros_like(dv_ref)

        def block(qi, diagonal):
            rows = pl.ds(pl.multiple_of(qi * tq, tq), tq)
            qv, dov = q_ref[0, rows, :], do_ref[rows, :]
            p = jnp.exp2(_dot_nt(kblk, qv) - lse_ref[0, qi, 0:1, :])
            if diagonal:
                p = jnp.where(_lane_iota((tq, tq)) >= _row_iota((tq, tq)), p, 0.0)
            dv_ref[0] += _dot(p.astype(BF16), dov)
            dp = _dot_nt(vblk, dov)
            ds = (p * (dp - dl_ref[0, qi, 0:1, :]) * math.log(2.0)).astype(BF16)
            dk_ref[0] += _dot(ds, qv)
            dqt_ref[0, qi] += _dot(kt, ds)

        block(ki, True)

        def two(i, carry):
            block(ki + 1 + 2 * i, False)
            block(ki + 2 + 2 * i, False)
            return carry

        def one(qi, carry):
            block(qi, False)
            return carry

        pairs = (nq - 1 - ki) // 2
        lax.fori_loop(0, pairs, two, 0)
        lax.fori_loop(ki + 1 + 2 * pairs, nq, one, 0)

    kv_spec = lambda w: pl.BlockSpec((1, tq, w), lambda h, ki: (h, ki, 0))
    stat = pl.BlockSpec((1, nq, 8, tq), lambda h, ki: (h, 0, 0, 0))
    return _pcall(
        body, "flash_bwd", (N_HEADS, nq),
        [pl.BlockSpec((1, t, HEAD_W), lambda h, ki: (h, 0, 0)), kv_spec(HEAD_W), kv_spec(V_HEAD),
         pl.BlockSpec((t, V_HEAD), lambda h, ki: (0, h)), stat, stat],
        [pl.BlockSpec((1, nq, HEAD_W, tq), lambda h, ki: (h, 0, 0, 0)), kv_spec(HEAD_W), kv_spec(V_HEAD)],
        [_sds((N_HEADS, nq, HEAD_W, tq)), _sds((N_HEADS, t, HEAD_W)), _sds((N_HEADS, t, V_HEAD))],
        comm=comm,
    )(q, k, v, do, lse, delta)


def _mla_bwd(dqt, dk, dv, cqn, ckvn, cq, ckv, rope_c, wq, wkv, qn, kvn, tm):
    t = cq.shape[0]

    def body(dq_ref, dk_ref, dv_ref, cqn_ref, ckvn_ref, cq_ref, ckv_ref, c_ref, wq_ref, wkv_ref, qn_ref, kvn_ref,
             dmla_ref, dwq_ref, dwkv_ref, dqn_ref, dkvn_ref):
        @pl.when(pl.program_id(0) == 0)
        def _():
            dwq_ref[...] = jnp.zeros_like(dwq_ref)
            dwkv_ref[...] = jnp.zeros_like(dwkv_ref)
            dqn_ref[...] = jnp.zeros_like(dqn_ref)
            dkvn_ref[...] = jnp.zeros_like(dkvn_ref)

        c = c_ref[...]
        lane = _lane_iota((tm, KR_W))
        cqn, ckvn = cqn_ref[...], ckvn_ref[...]
        dcqn = jnp.zeros((tm, Q_LORA), F32)
        dckvn = jnp.zeros((tm, KV_LORA), F32)
        dkr = jnp.zeros((tm, KR_W), F32)
        for h in range(N_HEADS):
            sl = slice(h * HEAD_W, (h + 1) * HEAD_W)
            dqh = jnp.transpose(dq_ref[h, 0]) * EXP2_SCALE
            droped = jnp.where(lane < 64, dqh[:, 128:256], 0.0)
            dqp = jnp.concatenate([dqh[:, 0:128], _rope_pair(droped) * c], axis=1).astype(BF16)
            dcqn = dcqn + _dot_nt(dqp, wq_ref[:, sl])
            dwq_ref[:, sl] += _dot_tn(cqn, dqp)
            dkr = dkr + jnp.where(lane < 64, dk_ref[h, :, 128:256], 0.0)
            dkvp = jnp.concatenate([dk_ref[h, :, 0:128], dv_ref[h]], axis=1).astype(BF16)
            dckvn = dckvn + _dot_nt(dkvp, wkv_ref[:, sl])
            dwkv_ref[:, sl] += _dot_tn(ckvn, dkvp)
        cqv, ckvv = cq_ref[...], ckv_ref[...]
        dcq, dgq = _rms_bwd(cqv, _rms_scale(cqv), qn_ref[...], dcqn)
        dckv, dgkv = _rms_bwd(ckvv, _rms_scale(ckvv), kvn_ref[...], dckvn)
        dqn_ref[...] += jnp.sum(dgq, axis=0, keepdims=True)
        dkvn_ref[...] += jnp.sum(dgkv, axis=0, keepdims=True)
        dmla_ref[:, 0:256] = dcq.astype(BF16)
        dmla_ref[:, 256:512] = dckv.astype(BF16)
        dmla_ref[:, 512:640] = (_rope_pair(dkr) * c).astype(BF16)

    hb = lambda w: pl.BlockSpec((N_HEADS, tm, w), lambda i: (0, i, 0))
    wide = N_HEADS * HEAD_W
    per_q = dqt.shape[3] // tm
    dq_spec = pl.BlockSpec((N_HEADS, 1, HEAD_W, tm), lambda i: (0, i // per_q, 0, i % per_q))
    return _pcall(
        body, "mla_bwd", (t // tm,),
        [dq_spec, hb(HEAD_W), hb(V_HEAD), _rows(tm, Q_LORA), _rows(tm, KV_LORA), _rows(tm, Q_LORA), _rows(tm, KV_LORA),
         _rows(tm, KR_W), _full(Q_LORA, wide), _full(KV_LORA, wide), _full(1, Q_LORA), _full(1, KV_LORA)],
        [_rows(tm, 640), _full(Q_LORA, wide), _full(KV_LORA, wide), _full(1, Q_LORA), _full(1, KV_LORA)],
        [_sds((t, 640), BF16), _sds((Q_LORA, wide)), _sds((KV_LORA, wide)), _sds((1, Q_LORA)), _sds((1, KV_LORA))],
    )(dqt, dk, dv, cqn, ckvn, cq, ckv, rope_c, wq, wkv, qn, kvn)


def _lru_bwd(dh, xa, h, rx, conv_w, wa, ba, wx, bx, lam, tb, comm=None):
    t = dh.shape[0]
    nb = t // tb

    def body(dh_ref, xa_ref, h_ref, hp_ref, x_ref, cw_ref, wa_ref, ba_ref, wx_ref, bx_ref, lam_ref,
             drx_ref, dcw_ref, dcb_ref, dwa_ref, dba_ref, dwx_ref, dbx_ref, dlam_ref, gc, dxn, tmp, pre_r, pre_i):
        step = pl.program_id(0)
        first_block = step == nb - 1

        @pl.when(step == 0)
        def _():
            gc[...] = jnp.zeros_like(gc)
            dxn[...] = jnp.zeros_like(dxn)
            for ref in (dcw_ref, dcb_ref, dwa_ref, dba_ref, dwx_ref, dbx_ref, dlam_ref):
                ref[...] = jnp.zeros_like(ref)

        xa = xa_ref[...]
        r, gi = _lru_gates(xa, wa_ref, ba_ref[...], wx_ref, bx_ref[...], pre_r, pre_i)
        lamv = lam_ref[...]
        sp = _softplus(-lamv)
        la = (-LRU_C * sp) * r
        a = jnp.exp(la)
        e2 = _expm1(2.0 * la)
        sq = jnp.sqrt(-e2)
        row = _row_iota((tb, D))
        cf = jnp.where(row == tb - 1, 1.0, pltpu.roll(a, tb - 1, 0))
        c3, b3 = _scan_groups_bwd(cf.reshape(tb // 8, 8, D), dh_ref[...].reshape(tb // 8, 8, D))
        carry = gc[...]
        for grp in reversed(range(tb // 8)):
            dg = b3[grp] + c3[grp] * carry
            pre_r[8 * grp:8 * grp + 8, :] = dg
            carry = dg[0:1, :]
        delta = pre_r[...]
        gc[...] = a[0:1, :] * carry
        hv = h_ref[...]
        hr = pltpu.roll(hv, 1, 0)
        tmp[...] = hr
        tmp[0:1, :] = jnp.where(first_block, 0.0, hp_ref[7:8, :])
        hprev = tmp[...]
        ix = gi * xa
        dla = (delta * hprev) * a - (delta * ix) * ((e2 + 1.0) / sq)
        dlam_ref[...] += jnp.sum(dla * r, axis=0, keepdims=True) * (LRU_C * _sigmoid(-lamv))
        dpr = (dla * (-LRU_C * sp)) * r * (1.0 - r)
        dsq = delta * sq
        dpi = (dsq * xa) * gi * (1.0 - gi)
        dba_ref[...] += jnp.sum(dpr, axis=0, keepdims=True)
        dbx_ref[...] += jnp.sum(dpi, axis=0, keepdims=True)
        pre_r[...] = dpr
        pre_i[...] = dpi
        xb = xa.astype(BF16)
        for n in range(RNN_BLOCKS):
            sl = slice(n * RNN_BW, (n + 1) * RNN_BW)
            dprn = pre_r[:, sl].astype(BF16)
            dpin = pre_i[:, sl].astype(BF16)
            dwa_ref[n] += _dot_tn(xb[:, sl], dprn)
            dwx_ref[n] += _dot_tn(xb[:, sl], dpin)
            tmp[:, sl] = _dot_nt(dprn, wa_ref[n]) + _dot_nt(dpin, wx_ref[n])
        dxa = dsq * gi + tmp[...]
        dcb_ref[...] += jnp.sum(dxa, axis=0, keepdims=True)
        xv = x_ref[...]
        drx = cw_ref[3:4, :] * dxa
        dcw_ref[3:4, :] += jnp.sum(dxa * xv, axis=0, keepdims=True)
        row8 = _row_iota((8, D))
        nxt = dxn[...]
        for s in (1, 2, 3):
            dr_ = pltpu.roll(dxa, tb - s, 0)
            tmp[...] = dr_
            tmp[tb - 8:tb, :] = jnp.where(row8 >= 8 - s, pltpu.roll(nxt, 8 - s, 0), dr_[tb - 8:tb, :])
            dxs = tmp[...]
            drx = drx + cw_ref[3 - s:4 - s, :] * dxs
            dcw_ref[3 - s:4 - s, :] += jnp.sum(dxs * xv, axis=0, keepdims=True)
        drx_ref[...] = drx.astype(BF16)
        dxn[...] = dxa[0:8, :]

    rev = pl.BlockSpec((tb, D), lambda i: (nb - 1 - i, 0))
    prev8 = pl.BlockSpec((8, D), lambda i: (jnp.maximum((nb - 1 - i) * (tb // 8) - 1, 0), 0))
    wblk = _full(RNN_BLOCKS, RNN_BW, RNN_BW)
    return _pcall(
        body, "lru_bwd", (nb,),
        [rev, rev, rev, prev8, rev, _full(4, D), wblk, _full(1, D), wblk, _full(1, D), _full(1, D)],
        [rev, _full(4, D), _full(1, D), wblk, _full(1, D), wblk, _full(1, D), _full(1, D)],
        [_sds((t, D), BF16), _sds((4, D)), _sds((1, D)), _sds((RNN_BLOCKS, RNN_BW, RNN_BW)), _sds((1, D)),
         _sds((RNN_BLOCKS, RNN_BW, RNN_BW)), _sds((1, D)), _sds((1, D))],
        scratch=[pltpu.VMEM((1, D), F32), pltpu.VMEM((8, D), F32), pltpu.VMEM((tb, D), F32), pltpu.VMEM((tb, D), F32),
                 pltpu.VMEM((tb, D), F32)],
        comm=comm,
    )(dh, xa, h, h, rx, conv_w, wa, ba, wx, bx, lam)


def _inproj_bwd(x, dh1, drx, dg3, dmla, w1, g, tm, comm=None):
    t = x.shape[0]

    def body(x_ref, d_ref, drx_ref, dg3_ref, dmla_ref, w_ref, g_ref, dx_ref, gacc_ref):
        @pl.when(pl.program_id(0) == 0)
        def _():
            gacc_ref[...] = jnp.zeros_like(gacc_ref)

        dxn = _dot(drx_ref[...], w_ref[0:D, :])
        for c0 in range(0, 3 * D, D):
            dxn = dxn + _dot(dg3_ref[:, c0:c0 + D], w_ref[D + c0:2 * D + c0, :])
        dxn = dxn + _dot(dmla_ref[...], w_ref[4 * D:W1_COLS, :])
        xv = x_ref[...]
        dx, dgr = _rms_bwd(xv, _rms_scale(xv), g_ref[...], dxn)
        dx_ref[...] = d_ref[...] + dx
        gacc_ref[...] += jnp.sum(dgr, axis=0, keepdims=True)

    return _pcall(
        body, "inproj_bwd", (t // tm,),
        [_rows(tm, D), _rows(tm, D), _rows(tm, D), _rows(tm, 3 * D), _rows(tm, 640), _full(W1_COLS, D), _full(1, D)],
        [_rows(tm, D), _full(1, D)],
        [_sds((t, D)), _sds((1, D))],
        comm=comm,
    )(x, dh1, drx, dg3, dmla, w1, g)


def _pcall_indexed(body, name, index, grid, in_specs, out_specs, out_shape):
    call = pl.pallas_call(
        body, name=name, out_shape=out_shape,
        grid_spec=pltpu.PrefetchScalarGridSpec(num_scalar_prefetch=1, grid=grid, in_specs=in_specs, out_specs=out_specs),
        compiler_params=pltpu.CompilerParams(dimension_semantics=("arbitrary",) * len(grid), vmem_limit_bytes=V7X_VMEM_LIMIT))
    return lambda *operands: call(index, *operands)


V7X_STREAM_BLOCK_BYTES = 5 << 19


def _stream_tile(rows, cols):
    return _row_tile(rows, cap=max(256, V7X_STREAM_BLOCK_BYTES // (4 * cols)), mult=16)


def _pair_sum(halves, theirs, core, out_dtype, name):
    _, rows, cols = halves.shape
    tm = _stream_tile(rows, cols)

    def body(c_ref, a_ref, b_ref, o_ref):
        o_ref[...] = (a_ref[0] + b_ref[...]).astype(out_dtype)

    plain = pl.BlockSpec((tm, cols), lambda i, c: (i, 0))
    return _pcall_indexed(body, name, core, (rows // tm,),
                          [pl.BlockSpec((1, tm, cols), lambda i, c: (c[0], i, 0)), plain], plain,
                          _sds((rows, cols), out_dtype))(halves, theirs)


def _chip_sum(parts, recv, chip, name):
    _, rows, cols = parts.shape
    tm = _stream_tile(rows, cols)

    def body(c_ref, a_ref, r_ref, o_ref):
        o_ref[...] = ((a_ref[0].astype(F32) + r_ref[0].astype(F32)) + r_ref[1].astype(F32)) + r_ref[2].astype(F32)

    return _pcall_indexed(body, name, chip, (rows // tm,),
                          [pl.BlockSpec((1, tm, cols), lambda i, c: (c[0], i, 0)),
                           pl.BlockSpec((N_CHIPS - 1, tm, cols), lambda i, c: (0, i, 0))],
                          pl.BlockSpec((tm, cols), lambda i, c: (i, 0)), _sds((rows, cols)))(parts, recv)


def _adam_math(w, gv, m, v):
    mn = ADAM_B1 * m + (1.0 - ADAM_B1) * gv
    vn = ADAM_B2 * v + (1.0 - ADAM_B2) * (gv * gv)
    m_hat = mn / (1.0 - ADAM_B1 ** ADAM_STEP)
    v_hat = vn / (1.0 - ADAM_B2 ** ADAM_STEP)
    return -ADAM_LR * (m_hat / (jnp.sqrt(v_hat) + ADAM_EPS) + ADAM_WD * w), mn, vn


def _adamw(w, g, m, v, name):
    rows, cols = w.shape
    tm = _row_tile(rows)

    def body(w_ref, g_ref, m_ref, v_ref, d_ref, mo_ref, vo_ref):
        d_ref[...], mo_ref[...], vo_ref[...] = _adam_math(w_ref[...], g_ref[...], m_ref[...], v_ref[...])

    spec = _rows(tm, cols)
    return _pcall(body, name, (rows // tm,), [spec] * 4, [spec] * 3, [_sds((rows, cols))] * 3)(w, g, m, v)


def _adamw_halves(w, mine, theirs, m, v, core, name, by_cols=False):
    rows, cols = w.shape
    if by_cols:
        tm, tc = mine.shape[0] // 2, cols // 2
        grid = (rows // tm, 2)
        full = pl.BlockSpec((tm, tc), lambda i, j, c: (i, j))
        half = pl.BlockSpec((tm, tc), lambda i, j, c: (i, 0))
    else:
        tm = _row_tile(rows // 2)
        nh = rows // 2 // tm
        grid = (rows // tm,)
        full = pl.BlockSpec((tm, cols), lambda i, c: (i, 0))
        half = pl.BlockSpec((tm, cols), lambda i, c: (i % nh, 0))

    def body(c_ref, w_ref, a_ref, b_ref, m_ref, v_ref, g_ref, d_ref, mo_ref, vo_ref):
        which = pl.program_id(1) if by_cols else pl.program_id(0) // nh
        gv = jnp.where(which == c_ref[0], a_ref[...], b_ref[...])
        g_ref[...] = gv
        d_ref[...], mo_ref[...], vo_ref[...] = _adam_math(w_ref[...], gv, m_ref[...], v_ref[...])

    return _pcall_indexed(body, name, core, grid, [full, half, half, full, full], [full] * 4,
                          [_sds((rows, cols))] * 4)(w, mine, theirs, m, v)


REL_SIBLING = (0, 0, 1)
REL_CHIPS = ((1, 0, 0), (0, 1, 0), (1, 1, 0))


V7X_DMA_CHUNK_BYTES = 1 << 20


def _split_copy(src, dst, shape, itemsize):
    nbytes = math.prod(shape) * itemsize
    if nbytes <= V7X_DMA_CHUNK_BYTES or len(shape) < 2:
        return [(src, dst)]
    if len(shape) > 2:
        out = []
        for k in range(shape[0]):
            out += _split_copy(src.at[k], dst.at[k], shape[1:], itemsize)
        return out
    rows = shape[0]
    sub = 8 * (4 // itemsize)
    parts = max(1, min(-(-nbytes // V7X_DMA_CHUNK_BYTES), rows // sub))
    while rows % parts or (rows // parts) % sub:
        parts -= 1
    step = rows // parts
    return [(src.at[pl.ds(k * step, step)], dst.at[pl.ds(k * step, step)]) for k in range(parts)]


def _mesh_pos():
    return (lax.axis_index("x"), lax.axis_index("y"), lax.axis_index("c"))


def _make_copy(i, op, sems, pos, src=None, dst=None):
    rel = op[0]
    src, dst = (op[1], op[2]) if src is None else (src, dst)
    send_sems, recv_sems = sems
    if rel is None:
        return pltpu.make_async_copy(src, dst, send_sems.at[i])
    peer = tuple((p + r) % 2 for p, r in zip(pos, rel))
    return pltpu.make_async_remote_copy(src_ref=src, dst_ref=dst, send_sem=send_sems.at[i], recv_sem=recv_sems.at[i],
                                        device_id=peer, device_id_type=MESH_ID)


def _start_copies(ops, sems, pos, base=0):
    for i, op in enumerate(ops):
        for s_piece, d_piece in _split_copy(op[1], op[2], op[1].shape, jnp.dtype(op[1].dtype).itemsize):
            _make_copy(base + i, op, sems, pos, s_piece, d_piece).start()


def _wait_copies(ops, sems, pos, base=0):
    for i, op in enumerate(ops):
        _make_copy(base + i, op, sems, pos).wait()


def _comm(name, ins, out_shapes, n_ops, ops_fn):
    n_in, n_out = len(ins), len(out_shapes)

    def body(*refs):
        in_refs, out_refs = refs[:n_in], refs[n_in:n_in + n_out]
        sems = refs[n_in + n_out:]
        pos = _mesh_pos()
        ops = ops_fn(in_refs, out_refs, pos)
        assert len(ops) == n_ops
        _start_copies(ops, sems, pos)
        _wait_copies(ops, sems, pos)

    hbm = pl.BlockSpec(memory_space=pl.ANY)
    return pl.pallas_call(
        body, name=name, in_specs=[hbm] * n_in, out_specs=[hbm] * n_out, out_shape=list(out_shapes),
        scratch_shapes=[pltpu.SemaphoreType.DMA((n_ops,)), pltpu.SemaphoreType.DMA((n_ops,))],
    )(*ins)


def _chip_of(pos, rel=(0, 0, 0)):
    return 2 * ((pos[0] + rel[0]) % 2) + (pos[1] + rel[1]) % 2


def _halved_gather_ops(pos, srcs, dsts, whole):
    me, c = _chip_of(pos), pos[2]
    ici, d2d = [], []
    for a, (src, dst) in enumerate(zip(srcs, dsts)):
        for rel in REL_CHIPS:
            if a in whole:
                ici.append((rel, src, dst.at[me]))
            else:
                ici.append((rel, src.at[c], dst.at[me, c]))
                arrived = dst.at[_chip_of(pos, rel), c]
                d2d.append((REL_SIBLING, arrived, arrived))
    return ici, d2d


def _halved_gather_comm(shards):
    srcs = [s.reshape(2, s.shape[0] // 2, s.shape[1]) for s in shards]

    def ops_fn(in_refs, out_refs, pos):
        return _halved_gather_ops(pos, in_refs, out_refs, ())

    def finish(outs, chip):
        return [lax.dynamic_update_index_in_dim(o, s, chip, 0).reshape((N_CHIPS,) + sh.shape) for o, s, sh in zip(outs, srcs, shards)]

    return (srcs, [_sds((N_CHIPS,) + s.shape, s.dtype) for s in srcs], 6 * len(shards), ops_fn), finish


def _gather_halved(shards, whole, chip, name):
    srcs = [s if a in whole or s.ndim == 3 else s.reshape(2, s.shape[0] // 2, s.shape[1]) for a, s in enumerate(shards)]
    n_sh = len(shards)
    n_ici, n_d2d = 3 * n_sh, 3 * (n_sh - len(whole))

    def body(*refs):
        in_refs, out_refs, sems = refs[:n_sh], refs[n_sh:2 * n_sh], refs[2 * n_sh:]
        pos = _mesh_pos()
        ici, d2d = _halved_gather_ops(pos, in_refs, out_refs, whole)
        _start_copies(ici, sems, pos)
        _wait_copies(ici, sems, pos)
        _start_copies(d2d, sems, pos, base=n_ici)
        _wait_copies(d2d, sems, pos, base=n_ici)

    hbm = pl.BlockSpec(memory_space=pl.ANY)
    outs = pl.pallas_call(
        body, name=name, in_specs=[hbm] * n_sh, out_specs=[hbm] * n_sh,
        out_shape=[_sds((N_CHIPS,) + s.shape, s.dtype) for s in srcs],
        scratch_shapes=[pltpu.SemaphoreType.DMA((n_ici + n_d2d,)), pltpu.SemaphoreType.DMA((n_ici + n_d2d,))],
    )(*srcs)
    return [lax.dynamic_update_index_in_dim(o, s, chip, 0).reshape((N_CHIPS,) + sh.shape)
            for o, s, sh in zip(outs, srcs, shards)]


def _split_comm(gs):
    def ops_fn(in_refs, out_refs, pos):
        return [(REL_SIBLING, src.at[1 - pos[2]], dst) for src, dst in zip(in_refs, out_refs)]

    return gs, [_sds(g.shape[1:], g.dtype) for g in gs], len(gs), ops_fn


def _exchange_comm(ps):
    def ops_fn(in_refs, out_refs, pos):
        return [(rel, src.at[_chip_of(pos, rel)], dst.at[j])
                for src, dst in zip(in_refs, out_refs) for j, rel in enumerate(REL_CHIPS)]

    return ps, [_sds((N_CHIPS - 1,) + p.shape[1:], p.dtype) for p in ps], 3 * len(ps), ops_fn


REL_OTHERS = tuple((dx, dy, dc) for dx in (0, 1) for dy in (0, 1) for dc in (0, 1) if dx + dy + dc)


def _join_comm(hs, piece=None):
    n = len(hs)

    def ops_fn(in_refs, out_refs, pos):
        ops = [(REL_SIBLING, src, dst) for src, dst in zip(in_refs[:n], out_refs[:n])]
        if piece is not None:
            ops += [(rel, in_refs[n], out_refs[n].at[2 * _chip_of(pos) + pos[2]]) for rel in REL_OTHERS]
        return ops

    ins, shapes = list(hs), [_sds(h.shape, h.dtype) for h in hs]
    if piece is not None:
        ins, shapes = ins + [piece], shapes + [_sds((2 * N_CHIPS,) + piece.shape, piece.dtype)]
    return ins, shapes, n + (len(REL_OTHERS) if piece is not None else 0), ops_fn


def _run_comm(name, comm):
    ins, shapes, n_ops, ops_fn = comm
    return _comm(name, ins, shapes, n_ops, ops_fn)


def _rope_table(t):
    pos = np.arange(t, dtype=np.float32)
    inv_freq = (np.float32(1.0) / (np.float32(ROPE_THETA) ** (np.arange(0, QK_ROPE, 2, dtype=np.float32) / np.float32(QK_ROPE)))).astype(np.float32)
    ang = (pos[:, None] * inv_freq[None, :]).astype(np.float32)
    return np.concatenate([np.cos(ang), np.cos(ang), np.sin(ang), np.sin(ang)], axis=-1).astype(np.float32)


def _rot_cols(w):
    return jnp.concatenate([-w[..., 32:], w[..., :32]], axis=-1)


def _unrot_cols(dw):
    return jnp.concatenate([dw[..., 32:], -dw[..., :32]], axis=-1)


IN_OFFS = (0, 1024, 2048, 2304, 2560, 2624, 3648, 4672)


def _w1t_from_w_in_t(wt):
    seg = [wt[IN_OFFS[i]:IN_OFFS[i + 1]] for i in range(7)]
    rnn_x, rnn_gate, cq, ckv, kr, ga, gb = seg
    return jnp.concatenate([rnn_x, rnn_gate, ga, gb, cq, ckv, kr, _rot_cols(kr.T).T], axis=0)


def _w_in_t_grad_from_parts(d_rx, d_g3, d_mla):
    kr = d_mla[512:576] + _unrot_cols(d_mla[576:640].T).T
    return jnp.concatenate([d_rx, d_g3[0:D], d_mla[0:512], kr, d_g3[D:3 * D]], axis=0)


def _wq_from_w_uq(w):
    w3 = w.reshape(Q_LORA, N_HEADS, QK_NOPE + QK_ROPE)
    rope = w3[..., QK_NOPE:]
    return jnp.concatenate([w3[..., :QK_NOPE], rope, _rot_cols(rope)], axis=-1).reshape(Q_LORA, N_HEADS * HEAD_W)


def _w_uq_grad_from_wq(dw):
    d3 = dw.reshape(Q_LORA, N_HEADS, HEAD_W)
    rope = d3[..., 128:192] + _unrot_cols(d3[..., 192:256])
    return jnp.concatenate([d3[..., :128], rope], axis=-1).reshape(Q_LORA, N_HEADS * (QK_NOPE + QK_ROPE))


def _cols_from_chunks(g):
    return g.transpose(1, 0, 2).reshape(g.shape[1], N_CHIPS * g.shape[2])


def _halves_of_col_chunks(dw):
    r, c4 = dw.shape
    return dw.reshape(2, r // 2, N_CHIPS, c4 // N_CHIPS).transpose(0, 2, 1, 3)


def _halves_of_row_chunks(dw):
    r4, c = dw.shape
    return dw.reshape(N_CHIPS, 2, r4 // (2 * N_CHIPS), c).transpose(1, 0, 2, 3)


def kernel(x, norm_mix, w_in, conv_w, conv_b, lru_wa, lru_ba, lru_wx, lru_bx, lru_lambda, q_norm, w_uq, kv_norm, w_ukv, w_out, norm_mlp, w_up, w_down, norm_final, loss_target, m_norm_mix, m_w_in, m_conv_w, m_conv_b, m_lru_wa, m_lru_ba, m_lru_wx, m_lru_bx, m_lru_lambda, m_q_norm, m_w_uq, m_kv_norm, m_w_ukv, m_w_out, m_norm_mlp, m_w_up, m_w_down, m_norm_final, v_norm_mix, v_w_in, v_conv_w, v_conv_b, v_lru_wa, v_lru_ba, v_lru_wx, v_lru_bx, v_lru_lambda, v_q_norm, v_w_uq, v_kv_norm, v_w_ukv, v_w_out, v_norm_mlp, v_w_up, v_w_down, v_norm_final):
    t = x.shape[1]
    tm = min(512, t)
    tb = min(256, t)
    tq = min(512, max(tm, t // 4))
    x2 = x[0]
    target = loss_target[0]
    chip = 2 * lax.axis_index("x") + lax.axis_index("y")
    core = lax.axis_index("c")
    chip_ix, core_ix = chip.reshape(1).astype(jnp.int32), core.reshape(1).astype(jnp.int32)
    row = lambda p: p.reshape(1, -1)

    big_shards = (w_in, w_uq, w_ukv, w_out, w_up, w_down)
    w_in_t = w_in.T
    col_halves = lambda z: jnp.stack([z[:, :z.shape[1] // 2], z[:, z.shape[1] // 2:]])
    w_in_g, conv_w_g = _gather_halved([col_halves(w_in_t.astype(BF16)), conv_w], (1,), chip, "weight_gather_first")
    w1 = _w1t_from_w_in_t(jnp.concatenate([w_in_g[:, 0], w_in_g[:, 1]], axis=-1).reshape(IN_OFFS[-1], D))
    conv_w_f = _cols_from_chunks(conv_w_g)
    wa_b, wx_b = lru_wa.astype(BF16), lru_wx.astype(BF16)

    rope_c = jnp.asarray(_rope_table(t))

    comm_a, finish_a = _halved_gather_comm([w.astype(BF16) for w in (w_uq, w_ukv, w_out)])
    (xn, rx, g3, cq, ckv, kr), gathered = _inproj(x2, row(norm_mix), w1, tm, comm=comm_a)
    g_uq, g_ukv, g_out = finish_a(gathered, chip)
    wq = _wq_from_w_uq(_cols_from_chunks(g_uq))
    wkv = _cols_from_chunks(g_ukv)
    w_out_f = g_out.reshape(D, D)
    comm_b, finish_b = _halved_gather_comm([w.astype(BF16) for w in (w_up, w_down)])
    (h, xa), gathered = _lru_fwd(rx, conv_w_f, row(conv_b), wa_b, row(lru_ba), wx_b, row(lru_bx), row(lru_lambda), tb, comm=comm_b)
    g_up, g_down = finish_b(gathered, chip)
    w_up_f = _cols_from_chunks(g_up)
    w_down_f = g_down.reshape(D_FF, D)
    q, k, v, cqn, ckvn = _mla_proj(cq, ckv, kr, row(q_norm), row(kv_norm), wq, wkv, rope_c, tm)
    nq = t // tq
    yb, lse = _flash_fwd(q, k, v, tq)
    h1, merged = _merge_out(x2, h, g3, yb, w_out_f, tm)
    u, n2 = _mlp_up(h1, row(norm_mlp), w_up_f, tm)
    act, dh2, loss_blk, g_norm_final = _mlp_down_loss(u, h1, target, w_down_f, row(norm_final), tm)

    g_w_down = _matmul_tn(act, dh2, "grad_w_down", "rows")
    du = _mlp_bwd_act(dh2, u, w_down_f, tm)
    dh1, g_norm_mlp = _mlp_bwd_in(du, dh2, h1, w_up_f, row(norm_mlp), tm)
    g_w_up = _matmul_tn(n2, du, "grad_w_up", "cols")
    dg3, dyb, delta, dh, g_w_out = _merge_bwd(dh1, w_out_f, g3, h, yb, merged, tm)
    delta = delta.reshape(N_HEADS, 8, nq, tq).swapaxes(1, 2)
    def pair_sums(hvs, theirs, dtypes, tag):
        return [_pair_sum(hv.reshape(2, -1, hv.shape[-1]), r.reshape(-1, r.shape[-1]), core_ix, dt, f"grad_pair_sum_{tag}{a}").reshape(r.shape)
                for a, (hv, r, dt) in enumerate(zip(hvs, theirs, dtypes))]

    def chip_sums(parts, received, tag):
        return [_chip_sum(p, r, chip_ix, f"grad_chip_sum_{tag}{a}") for a, (p, r) in enumerate(zip(parts, received))]

    early = [_halves_of_row_chunks(g_w_out), g_w_up, g_w_down]
    (dq, dk, dv), early_theirs = _flash_bwd(q, k, v, dyb, lse, delta, tq, comm=_split_comm(early))
    early_parts = pair_sums(early, early_theirs, [BF16] * 3, "early")
    dmla, g_wq, g_wkv, g_q_norm, g_kv_norm = _mla_bwd(dq, dk, dv, cqn, ckvn, cq, ckv, rope_c, wq, wkv, row(q_norm), row(kv_norm), tm)
    (drx, g_conv_w, g_conv_b, g_wa, g_ba, g_wx, g_bx, g_lam), early_received = _lru_bwd(
        dh, xa, h, rx, conv_w_f, wa_b, row(lru_ba), wx_b, row(lru_bx), row(lru_lambda), tb, comm=_exchange_comm(early_parts))
    early_reduced = chip_sums(early_parts, early_received, "early")
    grad_x, g_norm_mix = _inproj_bwd(x2, dh1, drx, dg3, dmla, w1, row(norm_mix), tm)
    g_w_in_gates, early_sibling = _matmul_tn(dg3, xn, "grad_w_in_gates", comm=_join_comm(early_reduced))
    g_w_in_t = _w_in_t_grad_from_parts(_matmul_tn(drx, xn, "grad_w_in_rx"), g_w_in_gates, _matmul_tn(dmla, xn, "grad_w_in_mla"))
    g_w_uq = _w_uq_grad_from_wq(g_wq)

    smalls = (g_norm_mix, g_conv_b, g_wa, g_ba, g_wx, g_bx, g_lam, g_q_norm, g_kv_norm, g_norm_mlp, g_norm_final, g_conv_w)
    s_flat = jnp.concatenate([s.reshape(-1) for s in smalls] + [loss_blk[0, 0:1], jnp.zeros((S_LEN - N_SMALL - CONVW_SIZE - 1,), F32)])
    g_w_in_halves = lax.optimization_barrier(col_halves(g_w_in_t))
    late = [g_w_in_halves.reshape(2, N_CHIPS, IN_OFFS[-1] // N_CHIPS, D // 2), _halves_of_col_chunks(g_w_uq), _halves_of_col_chunks(g_wkv),
            s_flat.reshape(N_CHIPS, 2, S_ROWS_HALF, 128).transpose(1, 0, 2, 3)]
    late_theirs = _run_comm("grad_sibling_split", _split_comm(late))
    late_parts = pair_sums(late, late_theirs, [BF16] * 3 + [F32], "late")
    late_reduced = chip_sums(late_parts, _run_comm("grad_chip_exchange", _exchange_comm(late_parts)), "late")
    late_sibling = _run_comm("grad_sibling_join", _join_comm(late_reduced[:3], late_reduced[3]))
    reduced = late_reduced[:3] + early_reduced
    reduced_sibling = list(late_sibling[:3]) + list(early_sibling)
    s_all = lax.dynamic_update_index_in_dim(late_sibling[3], late_reduced[3], 2 * chip + core, 0).reshape(-1)

    small_grads = []
    off = 0
    for shp, n in zip(SMALL_SHAPES, SMALL_SIZES):
        small_grads.append(s_all[off:off + n].reshape(shp))
        off += n
    g_conv_w_mine = lax.dynamic_slice_in_dim(s_all[off:off + CONVW_SIZE].reshape(4, D), chip * (D // N_CHIPS), D // N_CHIPS, axis=1)
    loss = s_all[off + CONVW_SIZE]

    big_m = (m_w_in, m_w_uq, m_w_ukv, m_w_out, m_w_up, m_w_down)
    big_v = (v_w_in, v_w_uq, v_w_ukv, v_w_out, v_w_up, v_w_down)
    big_names = ("w_in", "w_uq", "w_ukv", "w_out", "w_up", "w_down")
    big_upd = [_adamw_halves(w, gm, gt, m, v, core_ix, "adamw_" + n)
               for w, gm, gt, m, v, n in zip(big_shards[1:], reduced[1:], reduced_sibling[1:], big_m[1:], big_v[1:], big_names[1:])]
    w_in_upd = _adamw_halves(w_in_t, reduced[0], reduced_sibling[0], m_w_in.T, v_w_in.T, core_ix, "adamw_w_in", by_cols=True)
    big_upd = [[u.T for u in w_in_upd]] + big_upd

    small_w = (norm_mix, conv_b, lru_wa, lru_ba, lru_wx, lru_bx, lru_lambda, q_norm, kv_norm, norm_mlp, norm_final)
    small_m = (m_norm_mix, m_conv_b, m_lru_wa, m_lru_ba, m_lru_wx, m_lru_bx, m_lru_lambda, m_q_norm, m_kv_norm, m_norm_mlp, m_norm_final)
    small_v = (v_norm_mix, v_conv_b, v_lru_wa, v_lru_ba, v_lru_wx, v_lru_bx, v_lru_lambda, v_q_norm, v_kv_norm, v_norm_mlp, v_norm_final)

    def pack(items, last, fill):
        flat = jnp.concatenate([i.reshape(-1) for i in items] + [last.reshape(-1)])
        return jnp.concatenate([flat, jnp.full((PACK_ROWS * 128 - flat.shape[0],), fill, F32)]).reshape(PACK_ROWS, 128)

    packed = _adamw(pack(small_w, conv_w, 0.0), pack(small_grads, g_conv_w_mine, 0.0), pack(small_m, m_conv_w, 0.0),
                    pack(small_v, v_conv_w, 1.0), "adamw_small")

    def unpack(p):
        flat = p.reshape(-1)
        outs, o = [], 0
        for shp, n in zip(SMALL_SHAPES, SMALL_SIZES):
            outs.append(flat[o:o + n].reshape(shp))
            o += n
        return outs, flat[o:o + CONVW_SIZE // N_CHIPS].reshape(4, D // N_CHIPS)

    order = ("norm_mix", "w_in", "conv_w", "conv_b", "lru_wa", "lru_ba", "lru_wx", "lru_bx", "lru_lambda", "q_norm", "w_uq", "kv_norm",
             "w_ukv", "w_out", "norm_mlp", "w_up", "w_down", "norm_final")

    def assemble(small_list, conv_w_item, big_list):
        table = dict(zip(SMALL_NAMES, small_list))
        table["conv_w"] = conv_w_item
        table.update(zip(big_names, big_list))
        return [table[n] for n in order]

    outs = [loss, grad_x.reshape(1, t, D)]
    outs += assemble(small_grads, g_conv_w_mine, [b[0] for b in big_upd])
    for j in range(3):
        sm, cw = unpack(packed[j])
        outs += assemble(sm, cw, [b[j + 1] for b in big_upd])
    return tuple(outs)
```

```python
import math

import jax
import jax.numpy as jnp
import numpy as np
from jax import lax
from jax.experimental import pallas as pl
from jax.experimental.pallas import tpu as pltpu

F32 = jnp.float32
BF16 = jnp.bfloat16

D = 1024
N_HEADS = 8
QK_NOPE = 128
QK_ROPE = 64
V_HEAD = 128
Q_LORA = 256
KV_LORA = 256
D_FF = 4096
RNN_BLOCKS = 8
RNN_BW = 128
LRU_C = 8.0
EPS = 1e-6
ROPE_THETA = 10000.0
HEAD_W = 256
KR_W = 128
W1_COLS = 4 * D + Q_LORA + KV_LORA + KR_W
SM_SCALE = (QK_NOPE + QK_ROPE) ** -0.5
EXP2_SCALE = SM_SCALE * math.log2(math.e)
NEG = float(jnp.finfo(jnp.float32).min)

ADAM_LR = 0.001
ADAM_B1 = 0.9
ADAM_B2 = 0.999
ADAM_EPS = 1e-08
ADAM_WD = 0.01
ADAM_STEP = 10

N_CHIPS = 4
V7X_VMEM_LIMIT = 56 * 1024 * 1024
MESH_ID = pl.DeviceIdType.MESH

SMALL_NAMES = ("norm_mix", "conv_b", "lru_wa", "lru_ba", "lru_wx", "lru_bx", "lru_lambda", "q_norm", "kv_norm", "norm_mlp", "norm_final")
SMALL_SHAPES = ((D,), (D,), (RNN_BLOCKS, RNN_BW, RNN_BW), (RNN_BLOCKS, RNN_BW), (RNN_BLOCKS, RNN_BW, RNN_BW), (RNN_BLOCKS, RNN_BW), (D,),
                (Q_LORA,), (KV_LORA,), (D,), (D,))
SMALL_SIZES = tuple(math.prod(s) for s in SMALL_SHAPES)
N_SMALL = sum(SMALL_SIZES)
CONVW_SIZE = 4 * D
S_LEN = -(-(N_SMALL + CONVW_SIZE) // 8192) * 8192
S_ROWS_HALF = S_LEN // (N_CHIPS * 2 * 128)
PACK_ROWS = -(-(N_SMALL + CONVW_SIZE // N_CHIPS) // (256 * 128)) * 256


def _pcall(body, name, grid, in_specs, out_specs, out_shape, scratch=(), comm=None):
    params = pltpu.CompilerParams(dimension_semantics=("arbitrary",) * len(grid), vmem_limit_bytes=V7X_VMEM_LIMIT)
    if comm is None:
        return pl.pallas_call(body, name=name, grid=grid, in_specs=in_specs, out_specs=out_specs, out_shape=out_shape,
                              scratch_shapes=list(scratch), compiler_params=params)
    c_ins, c_shapes, n_ops, ops_fn = comm
    single = not isinstance(out_specs, (list, tuple))
    out_specs, out_shape = ([out_specs], [out_shape]) if single else (list(out_specs), list(out_shape))
    n_in, n_out, n_sc, n_ci, n_co = len(in_specs), len(out_specs), len(scratch), len(c_ins), len(c_shapes)

    def wrapped(*refs):
        ins, refs = refs[:n_in], refs[n_in:]
        c_in_refs, refs = refs[:n_ci], refs[n_ci:]
        outs, refs = refs[:n_out], refs[n_out:]
        c_out_refs, refs = refs[:n_co], refs[n_co:]
        own_scratch, sems = refs[:n_sc], refs[n_sc:]
        pos = _mesh_pos()
        ops = ops_fn(c_in_refs, c_out_refs, pos)
        ops, then = ops if isinstance(ops, tuple) else (ops, [])
        assert len(ops) + len(then) == n_ops
        first, last = True, True
        for d, n in enumerate(grid):
            first = first & (pl.program_id(d) == 0)
            last = last & (pl.program_id(d) == n - 1)

        @pl.when(first)
        def _():
            _start_copies(ops, sems, pos)

        body(*ins, *outs, *own_scratch)

        @pl.when(last)
        def _():
            _wait_copies(ops, sems, pos)
            _start_copies(then, sems, pos, base=len(ops))
            _wait_copies(then, sems, pos, base=len(ops))

    hbm = pl.BlockSpec(memory_space=pl.ANY)
    call = pl.pallas_call(
        wrapped, name=name, grid=grid, in_specs=list(in_specs) + [hbm] * n_ci, out_specs=out_specs + [hbm] * n_co,
        out_shape=out_shape + list(c_shapes),
        scratch_shapes=list(scratch) + [pltpu.SemaphoreType.DMA((n_ops,)), pltpu.SemaphoreType.DMA((n_ops,))],
        compiler_params=params)

    def run(*operands):
        res = call(*operands, *c_ins)
        own = res[0] if single else res[:n_out]
        return own, res[n_out:]

    return run


def _rows(tm, w):
    return pl.BlockSpec((tm, w), lambda i: (i, 0))


def _full(*shape):
    return pl.BlockSpec(shape, lambda *_: (0,) * len(shape))


def _sds(shape, dtype=F32):
    return jax.ShapeDtypeStruct(shape, dtype)


def _row_tile(rows, cap=256, mult=8):
    t = min(rows, cap)
    while rows % t or t % mult:
        t -= 1
    return t


def _dot(a, b):
    return jnp.dot(a, b, preferred_element_type=F32)


def _dot_nt(a, b):
    return lax.dot_general(a, b, (((1,), (1,)), ((), ())), preferred_element_type=F32)


def _dot_tn(a, b):
    return lax.dot_general(a, b, (((0,), (0,)), ((), ())), preferred_element_type=F32)


def _sigmoid(x):
    return 1.0 / (1.0 + jnp.exp(-x))


_GELU_C = math.sqrt(2.0 / math.pi)


def _gelu(x):
    return x * (0.5 * (1.0 + jnp.tanh(_GELU_C * (x + 0.044715 * (x * x * x)))))


def _gelu_grad(x):
    t = jnp.tanh(_GELU_C * (x + 0.044715 * (x * x * x)))
    cdf = 0.5 * (1.0 + t)
    return cdf + x * (0.5 * (1.0 - t * t) * _GELU_C * (1.0 + 3.0 * 0.044715 * (x * x)))


def _rms_scale(x):
    return lax.rsqrt(jnp.mean(x * x, axis=-1, keepdims=True) + EPS)


def _rms_bwd(x, rs, g, dy):
    gdy = dy * g
    dx = rs * gdy - x * ((rs * rs * rs) * jnp.mean(gdy * x, axis=-1, keepdims=True))
    return dx, dy * (x * rs)


def _log1p(e):
    u = 1.0 + e
    d = u - 1.0
    return jnp.where(d == 0.0, e, jnp.log(u) * (e / jnp.where(d == 0.0, 1.0, d)))


def _softplus(y):
    return jnp.maximum(y, 0.0) + _log1p(jnp.exp(-jnp.abs(y)))


def _expm1(x):
    u = jnp.exp(x)
    lu = jnp.log(u)
    safe = jnp.where((u == 1.0) | (u == 0.0), 1.0, lu)
    return jnp.where(u == 1.0, x, jnp.where(u == 0.0, -1.0, (u - 1.0) * (x / safe)))


def _row_iota(shape):
    return lax.broadcasted_iota(jnp.int32, shape, 0)


def _lane_iota(shape):
    return lax.broadcasted_iota(jnp.int32, shape, 1)


def _scan_groups_fwd(a, b):
    sub = lax.broadcasted_iota(jnp.int32, a.shape, 1)
    for sh in (1, 2, 4):
        m = sub >= sh
        b = jnp.where(m, a * pltpu.roll(b, sh, 1) + b, b)
        a = jnp.where(m, a * pltpu.roll(a, sh, 1), a)
    return a, b


def _scan_groups_bwd(c, b):
    sub = lax.broadcasted_iota(jnp.int32, c.shape, 1)
    for sh in (1, 2, 4):
        m = sub < 8 - sh
        b = jnp.where(m, b + c * pltpu.roll(b, 8 - sh, 1), b)
        c = jnp.where(m, c * pltpu.roll(c, 8 - sh, 1), c)
    return c, b


def _rope_pair(gc):
    return gc + pltpu.roll(gc, 64, 1)


def _inproj(x, g, w1, tm, comm=None):
    t = x.shape[0]
    widths = (D, 3 * D, Q_LORA, KV_LORA, KR_W)

    def body(x_ref, g_ref, w_ref, xn_ref, rx_ref, g3_ref, cq_ref, ckv_ref, kr_ref):
        xv = x_ref[...]
        xn = (xv * _rms_scale(xv) * g_ref[...]).astype(BF16)
        xn_ref[...] = xn
        col = 0
        for ref, w in zip((rx_ref, g3_ref, cq_ref, ckv_ref, kr_ref), widths):
            for c0 in range(0, w, 512):
                cw = min(512, w - c0)
                ref[:, c0:c0 + cw] = _dot_nt(xn, w_ref[col + c0:col + c0 + cw, :])
            col += w

    return _pcall(
        body, "inproj", (t // tm,),
        [_rows(tm, D), _full(1, D), _full(W1_COLS, D)],
        [_rows(tm, D)] + [_rows(tm, w) for w in widths],
        [_sds((t, D), BF16)] + [_sds((t, w)) for w in widths],
        comm=comm,
    )(x, g, w1)


def _lru_gates(xa, wa_ref, ba, wx_ref, bx, pre_r, pre_i):
    xb = xa.astype(BF16)
    for n in range(RNN_BLOCKS):
        sl = slice(n * RNN_BW, (n + 1) * RNN_BW)
        pre_r[:, sl] = _dot(xb[:, sl], wa_ref[n])
        pre_i[:, sl] = _dot(xb[:, sl], wx_ref[n])
    r = _sigmoid(pre_r[...] + ba)
    i = _sigmoid(pre_i[...] + bx)
    return r, i


def _lru_fwd(rx, conv_w, conv_b, wa, ba, wx, bx, lam, tb, comm=None):
    t = rx.shape[0]
    nb = t // tb

    def body(x_ref, xp_ref, cw_ref, cb_ref, wa_ref, ba_ref, wx_ref, bx_ref, lam_ref, h_ref, xa_ref, hc, tmp, pre_r, pre_i):
        i_blk = pl.program_id(0)

        @pl.when(i_blk == 0)
        def _():
            hc[...] = jnp.zeros_like(hc)

        xv = x_ref[...]
        xp = jnp.where(i_blk > 0, xp_ref[...], 0.0)
        row8 = _row_iota((8, D))
        xa = cb_ref[...] + cw_ref[3:4, :] * xv
        for s in (1, 2, 3):
            xr = pltpu.roll(xv, s, 0)
            tmp[...] = xr
            tmp[0:8, :] = jnp.where(row8 < s, pltpu.roll(xp, s, 0), xr[0:8, :])
            xa = xa + cw_ref[3 - s:4 - s, :] * tmp[...]
        xa_ref[...] = xa
        r, gi = _lru_gates(xa, wa_ref, ba_ref[...], wx_ref, bx_ref[...], pre_r, pre_i)
        la = (-LRU_C * _softplus(-lam_ref[...])) * r
        a = jnp.exp(la)
        b = jnp.sqrt(-_expm1(2.0 * la)) * (gi * xa)
        a3, b3 = _scan_groups_fwd(a.reshape(tb // 8, 8, D), b.reshape(tb // 8, 8, D))
        carry = hc[...]
        for grp in range(tb // 8):
            hg = a3[grp] * carry + b3[grp]
            h_ref[8 * grp:8 * grp + 8, :] = hg
            carry = hg[7:8, :]
        hc[...] = carry

    prev8 = pl.BlockSpec((8, D), lambda i: (jnp.maximum(i * (tb // 8) - 1, 0), 0))
    return _pcall(
        body, "lru_fwd", (nb,),
        [_rows(tb, D), prev8, _full(4, D), _full(1, D), _full(RNN_BLOCKS, RNN_BW, RNN_BW), _full(1, D),
         _full(RNN_BLOCKS, RNN_BW, RNN_BW), _full(1, D), _full(1, D)],
        [_rows(tb, D), _rows(tb, D)],
        [_sds((t, D)), _sds((t, D))],
        scratch=[pltpu.VMEM((1, D), F32), pltpu.VMEM((tb, D), F32), pltpu.VMEM((tb, D), F32), pltpu.VMEM((tb, D), F32)],
        comm=comm,
    )(rx, rx, conv_w, conv_b, wa, ba, wx, bx, lam)


def _mla_proj(cq, ckv, kr, qn, kvn, wq, wkv, rope_c, tm):
    t = cq.shape[0]

    def body(cq_ref, ckv_ref, kr_ref, qn_ref, kvn_ref, wq_ref, wkv_ref, c_ref, q_ref, k_ref, v_ref, cqn_ref, ckvn_ref):
        cqv = cq_ref[...]
        cqn = (cqv * _rms_scale(cqv) * qn_ref[...]).astype(BF16)
        ckvv = ckv_ref[...]
        ckvn = (ckvv * _rms_scale(ckvv) * kvn_ref[...]).astype(BF16)
        cqn_ref[...] = cqn
        ckvn_ref[...] = ckvn
        c = c_ref[...]
        lane = _lane_iota((tm, KR_W))
        kro = jnp.where(lane < 64, _rope_pair(kr_ref[...] * c), 0.0).astype(BF16)
        for h in range(N_HEADS):
            sl = slice(h * HEAD_W, (h + 1) * HEAD_W)
            qh = _dot(cqn, wq_ref[:, sl])
            q_ref[h, :, 0:128] = (qh[:, 0:128] * EXP2_SCALE).astype(BF16)
            q_ref[h, :, 128:256] = (_rope_pair(qh[:, 128:256] * c) * EXP2_SCALE).astype(BF16)
            kvh = _dot(ckvn, wkv_ref[:, sl])
            k_ref[h, :, 0:128] = kvh[:, 0:128].astype(BF16)
            k_ref[h, :, 128:256] = kro
            v_ref[h, :, 0:V_HEAD] = kvh[:, 128:256].astype(BF16)
            v_ref[h, :, V_HEAD:2 * V_HEAD] = jnp.ones((tm, V_HEAD), BF16)

    hb = lambda w: pl.BlockSpec((N_HEADS, tm, w), lambda i: (0, i, 0))
    return _pcall(
        body, "mla_proj", (t // tm,),
        [_rows(tm, Q_LORA), _rows(tm, KV_LORA), _rows(tm, KR_W), _full(1, Q_LORA), _full(1, KV_LORA),
         _full(Q_LORA, N_HEADS * HEAD_W), _full(KV_LORA, N_HEADS * HEAD_W), _rows(tm, KR_W)],
        [hb(HEAD_W), hb(HEAD_W), hb(2 * V_HEAD), _rows(tm, Q_LORA), _rows(tm, KV_LORA)],
        [_sds((N_HEADS, t, HEAD_W), BF16), _sds((N_HEADS, t, HEAD_W), BF16), _sds((N_HEADS, t, 2 * V_HEAD), BF16),
         _sds((t, Q_LORA), BF16), _sds((t, KV_LORA), BF16)],
    )(cq, ckv, kr, qn, kvn, wq, wkv, rope_c)


def _flash_fwd(q, k, v, tq, comm=None):
    t = q.shape[1]
    nq = t // tq

    def body(q_ref, k_ref, v_ref, o_ref, lse_ref, s_even, s_odd):
        qi = pl.program_id(1)
        qv = q_ref[0]

        def scores(ki, buf):
            buf[...] = _dot_nt(qv, k_ref[0, pl.ds(pl.multiple_of(ki * tq, tq), tq), :])

        def softmax_pv(ki, buf, carry, diagonal):
            m, acc = carry
            s = buf[...]
            if diagonal:
                s = jnp.where(_row_iota((tq, tq)) >= _lane_iota((tq, tq)), s, NEG)
            m_new = jnp.maximum(m, jnp.max(s, axis=1, keepdims=True))
            p = jnp.exp2(s - m_new)
            alpha = jnp.exp2(m - m_new)
            acc = alpha * acc + _dot(p.astype(BF16), v_ref[0, pl.ds(pl.multiple_of(ki * tq, tq), tq), :])
            return m_new, acc

        def finish(carry):
            m, acc = carry
            l = acc[:, V_HEAD:2 * V_HEAD]
            o_ref[...] = acc[:, 0:V_HEAD] / l
            lse = m + jnp.log(l) * math.log2(math.e)
            lse_ref[0, 0] = jnp.transpose(lse)[0:8, :]

        def two(i, carry):
            scores(2 * i + 1, s_odd)
            carry = softmax_pv(2 * i, s_even, carry, False)
            scores(2 * i + 2, s_even)
            return softmax_pv(2 * i + 1, s_odd, carry, False)

        init = (jnp.full((tq, 1), -jnp.inf, F32), jnp.zeros((tq, 2 * V_HEAD), F32))
        scores(0, s_even)
        carry = lax.fori_loop(0, qi // 2, two, init)

        @pl.when(qi % 2 == 0)
        def _():
            finish(softmax_pv(qi, s_even, carry, True))

        @pl.when(qi % 2 == 1)
        def _():
            scores(qi, s_odd)
            finish(softmax_pv(qi, s_odd, softmax_pv(qi - 1, s_even, carry, False), True))

    head = lambda w: pl.BlockSpec((1, t, w), lambda h, qi: (h, 0, 0))
    return _pcall(
        body, "flash_fwd", (N_HEADS, nq),
        [pl.BlockSpec((1, tq, HEAD_W), lambda h, qi: (h, qi, 0)), head(HEAD_W), head(2 * V_HEAD)],
        [pl.BlockSpec((tq, V_HEAD), lambda h, qi: (qi, h)), pl.BlockSpec((1, 1, 8, tq), lambda h, qi: (h, qi, 0, 0))],
        [_sds((t, D)), _sds((N_HEADS, nq, 8, tq))],
        scratch=[pltpu.VMEM((tq, tq), F32), pltpu.VMEM((tq, tq), F32)],
        comm=comm,
    )(q, k, v)


def _merge_out(x, h, g3, yb, w_out, tm):
    t = x.shape[0]

    def body(x_ref, h_ref, g3_ref, yb_ref, w_ref, h1_ref, mg_ref):
        ya = h_ref[...] * _gelu(g3_ref[:, 0:D])
        merged = (_sigmoid(g3_ref[:, D:2 * D]) * ya + _sigmoid(g3_ref[:, 2 * D:3 * D]) * yb_ref[...]).astype(BF16)
        mg_ref[...] = merged
        h1_ref[...] = x_ref[...] + _dot(merged, w_ref[...])

    return _pcall(
        body, "merge_out", (t // tm,),
        [_rows(tm, D), _rows(tm, D), _rows(tm, 3 * D), _rows(tm, D), _full(D, D)],
        [_rows(tm, D), _rows(tm, D)],
        [_sds((t, D)), _sds((t, D), BF16)],
    )(x, h, g3, yb, w_out)


def _mlp_up(h1, g, w_up, tm):
    t = h1.shape[0]

    def body(h_ref, g_ref, w_ref, u_ref, n2_ref):
        hv = h_ref[...]
        n2 = (hv * _rms_scale(hv) * g_ref[...]).astype(BF16)
        n2_ref[...] = n2
        for c0 in range(0, D_FF, 512):
            u_ref[:, c0:c0 + 512] = _dot(n2, w_ref[:, c0:c0 + 512])

    return _pcall(
        body, "mlp_up", (t // tm,),
        [_rows(tm, D), _full(1, D), _full(D, D_FF)],
        [_rows(tm, D_FF), _rows(tm, D)],
        [_sds((t, D_FF)), _sds((t, D), BF16)],
    )(h1, g, w_up)


def _mlp_down_loss(u, h1, target, w_down, g, tm):
    t = u.shape[0]

    def body(u_ref, h1_ref, tg_ref, w_ref, g_ref, act_ref, dh2_ref, loss_ref, gnf_ref, lacc):
        i = pl.program_id(0)

        @pl.when(i == 0)
        def _():
            lacc[...] = jnp.zeros_like(lacc)
            gnf_ref[...] = jnp.zeros_like(gnf_ref)

        ru = jnp.maximum(u_ref[...], 0.0)
        act = (ru * ru).astype(BF16)
        act_ref[...] = act
        h2 = h1_ref[...] + _dot(act, w_ref[...])
        rs = _rms_scale(h2)
        gv = g_ref[...]
        err = h2 * rs * gv - tg_ref[...]
        lacc[...] += jnp.sum(err * err, axis=0, keepdims=True)
        dx, dgr = _rms_bwd(h2, rs, gv, err * (1.0 / D))
        dh2_ref[...] = dx
        gnf_ref[...] += jnp.sum(dgr, axis=0, keepdims=True)

        @pl.when(i == pl.num_programs(0) - 1)
        def _():
            loss_ref[...] = jnp.broadcast_to(jnp.sum(lacc[...], axis=1, keepdims=True) * (0.5 / D), (8, 128))

    return _pcall(
        body, "mlp_down_loss", (t // tm,),
        [_rows(tm, D_FF), _rows(tm, D), _rows(tm, D), _full(D_FF, D), _full(1, D)],
        [_rows(tm, D_FF), _rows(tm, D), _full(8, 128), _full(1, D)],
        [_sds((t, D_FF), BF16), _sds((t, D)), _sds((8, 128)), _sds((1, D))],
        scratch=[pltpu.VMEM((1, D), F32)],
    )(u, h1, target, w_down, g)


def _matmul_tn(a, g, name, chunked=None, comm=None):
    t, kdim = a.shape
    ndim = g.shape[1]
    tk = min(kdim, 1024)
    tn = ndim if ndim <= 1024 else 1024
    if chunked == "cols":
        assert tk == kdim and tn == ndim // N_CHIPS
    elif chunked == "rows":
        assert tk == kdim // N_CHIPS and tn == ndim
    tt = min(t, 4096 if a.dtype == BF16 and g.dtype == BF16 else 2048)
    nt = t // tt

    def body(a_ref, g_ref, o_ref):
        @pl.when(pl.program_id(2) == 0)
        def _():
            o_ref[...] = jnp.zeros_like(o_ref)

        o_ref[...] += _dot_tn(a_ref[...].astype(BF16), g_ref[...].astype(BF16)).reshape(o_ref.shape)

    if chunked is not None:
        out_spec = pl.BlockSpec((2, None, tk // 2, tn), lambda i, j, s: (0, i + j, 0, 0))
        out_shape = _sds((2, N_CHIPS, tk // 2, tn))
    else:
        out_spec, out_shape = pl.BlockSpec((tk, tn), lambda i, j, s: (i, j)), _sds((kdim, ndim))
    return _pcall(
        body, name, (kdim // tk, ndim // tn, nt),
        [pl.BlockSpec((tt, tk), lambda i, j, s: (s, i)), pl.BlockSpec((tt, tn), lambda i, j, s: (s, j))],
        out_spec, out_shape, comm=comm,
    )(a, g)


def _mlp_bwd_act(dh2, u, w_down, tm):
    t = u.shape[0]

    def body(d_ref, u_ref, w_ref, du_ref):
        db = d_ref[...].astype(BF16)
        for c0 in range(0, D_FF, 512):
            da = _dot_nt(db, w_ref[c0:c0 + 512, :])
            du_ref[:, c0:c0 + 512] = (da * (2.0 * jnp.maximum(u_ref[:, c0:c0 + 512], 0.0))).astype(BF16)

    return _pcall(
        body, "mlp_bwd_act", (t // tm,),
        [_rows(tm, D), _rows(tm, D_FF), _full(D_FF, D)],
        _rows(tm, D_FF), _sds((t, D_FF), BF16),
    )(dh2, u, w_down)


def _mlp_bwd_in(du, dh2, h1, w_up, g, tm):
    t = du.shape[0]

    def body(du_ref, d_ref, h_ref, w_ref, g_ref, dh1_ref, gacc_ref):
        @pl.when(pl.program_id(0) == 0)
        def _():
            gacc_ref[...] = jnp.zeros_like(gacc_ref)

        dn2 = _dot_nt(du_ref[...], w_ref[...])
        hv = h_ref[...]
        dx, dgr = _rms_bwd(hv, _rms_scale(hv), g_ref[...], dn2)
        dh1_ref[...] = d_ref[...] + dx
        gacc_ref[...] += jnp.sum(dgr, axis=0, keepdims=True)

    return _pcall(
        body, "mlp_bwd_in", (t // tm,),
        [_rows(tm, D_FF), _rows(tm, D), _rows(tm, D), _full(D, D_FF), _full(1, D)],
        [_rows(tm, D), _full(1, D)],
        [_sds((t, D)), _sds((1, D))],
    )(du, dh2, h1, w_up, g)


def _merge_bwd(dh1, w_out, g3, h, yb, merged, tm):
    t = dh1.shape[0]

    def body(d_ref, w_ref, g3_ref, h_ref, yb_ref, mg_ref, dg3_ref, dyb_ref, dl_ref, dh_ref, dwo_ref):
        @pl.when(pl.program_id(0) == 0)
        def _():
            dwo_ref[...] = jnp.zeros_like(dwo_ref)

        db = d_ref[...].astype(BF16)
        dwo_ref[...] += _dot_tn(mg_ref[...], db)
        dm = _dot_nt(db, w_ref[...])
        gv = g3_ref[:, 0:D]
        sa = _sigmoid(g3_ref[:, D:2 * D])
        sb = _sigmoid(g3_ref[:, 2 * D:3 * D])
        gel = _gelu(gv)
        hv = h_ref[...]
        ybv = yb_ref[...]
        dya = dm * sa
        dyb = dm * sb
        dg3_ref[:, 0:D] = (dya * hv * _gelu_grad(gv)).astype(BF16)
        dg3_ref[:, D:2 * D] = (dya * (hv * gel) * (1.0 - sa)).astype(BF16)
        dg3_ref[:, 2 * D:3 * D] = (dyb * ybv * (1.0 - sb)).astype(BF16)
        dh_ref[...] = dya * gel
        dyb_ref[...] = dyb.astype(BF16)
        prod = dyb * ybv
        ones = jnp.ones((8, V_HEAD), F32)
        for hh in range(N_HEADS):
            dl_ref[hh] = lax.dot_general(ones, prod[:, hh * V_HEAD:(hh + 1) * V_HEAD], (((1,), (1,)), ((), ())),
                                         precision=lax.Precision.HIGHEST, preferred_element_type=F32)

    return _pcall(
        body, "merge_bwd", (t // tm,),
        [_rows(tm, D), _full(D, D), _rows(tm, 3 * D), _rows(tm, D), _rows(tm, D), _rows(tm, D)],
        [_rows(tm, 3 * D), _rows(tm, D), pl.BlockSpec((N_HEADS, 8, tm), lambda i: (0, 0, i)), _rows(tm, D), _full(D, D)],
        [_sds((t, 3 * D), BF16), _sds((t, D), BF16), _sds((N_HEADS, 8, t)), _sds((t, D)), _sds((D, D))],
    )(dh1, w_out, g3, h, yb, merged)


def _flash_bwd(q, k, v, do, lse, delta, tq, comm=None):
    t = q.shape[1]
    nq = t // tq

    def body(q_ref, k_ref, v_ref, do_ref, lse_ref, dl_ref, dqt_ref, dk_ref, dv_ref):
        ki = pl.program_id(1)

        @pl.when(ki == 0)
        def _():
            dqt_ref[...] = jnp.zeros_like(dqt_ref)

        kblk, vblk = k_ref[0], v_ref[0]
        kt = jnp.transpose(kblk)
        dk_ref[...] = jnp.zeros_like(dk_ref)
        dv_ref[...] = jnp.zeros_like(dv_ref)

        def block(qi, diagonal):
            rows = pl.ds(pl.multiple_of(qi * tq, tq), tq)
            qv, dov = q_ref[0, rows, :], do_ref[rows, :]
            p = jnp.exp2(_dot_nt(kblk, qv) - lse_ref[0, qi, 0:1, :])
            if diagonal:
                p = jnp.where(_lane_iota((tq, tq)) >= _row_iota((tq, tq)), p, 0.0)
            dv_ref[0] += _dot(p.astype(BF16), dov)
            dp = _dot_nt(vblk, dov)
            ds = (p * (dp - dl_ref[0, qi, 0:1, :]) * math.log(2.0)).astype(BF16)
            dk_ref[0] += _dot(ds, qv)
            dqt_ref[0, qi] += _dot(kt, ds)

        block(ki, True)

        def two(i, carry):
            block(ki + 1 + 2 * i, False)
            block(ki + 2 + 2 * i, False)
            return carry

        def one(qi, carry):
            block(qi, False)
            return carry

        pairs = (nq - 1 - ki) // 2
        lax.fori_loop(0, pairs, two, 0)
        lax.fori_loop(ki + 1 + 2 * pairs, nq, one, 0)

    kv_spec = lambda w: pl.BlockSpec((1, tq, w), lambda h, ki: (h, ki, 0))
    stat = pl.BlockSpec((1, nq, 8, tq), lambda h, ki: (h, 0, 0, 0))
    return _pcall(
        body, "flash_bwd", (N_HEADS, nq),
        [pl.BlockSpec((1, t, HEAD_W), lambda h, ki: (h, 0, 0)), kv_spec(HEAD_W), kv_spec(V_HEAD),
         pl.BlockSpec((t, V_HEAD), lambda h, ki: (0, h)), stat, stat],
        [pl.BlockSpec((1, nq, HEAD_W, tq), lambda h, ki: (h, 0, 0, 0)), kv_spec(HEAD_W), kv_spec(V_HEAD)],
        [_sds((N_HEADS, nq, HEAD_W, tq)), _sds((N_HEADS, t, HEAD_W)), _sds((N_HEADS, t, V_HEAD))],
        comm=comm,
    )(q, k, v, do, lse, delta)


def _mla_bwd(dqt, dk, dv, cqn, ckvn, cq, ckv, rope_c, wq, wkv, qn, kvn, tm):
    t = cq.shape[0]

    def body(dq_ref, dk_ref, dv_ref, cqn_ref, ckvn_ref, cq_ref, ckv_ref, c_ref, wq_ref, wkv_ref, qn_ref, kvn_ref,
             dmla_ref, dwq_ref, dwkv_ref, dqn_ref, dkvn_ref):
        @pl.when(pl.program_id(0) == 0)
        def _():
            dwq_ref[...] = jnp.zeros_like(dwq_ref)
            dwkv_ref[...] = jnp.zeros_like(dwkv_ref)
            dqn_ref[...] = jnp.zeros_like(dqn_ref)
            dkvn_ref[...] = jnp.zeros_like(dkvn_ref)

        c = c_ref[...]
        lane = _lane_iota((tm, KR_W))
        cqn, ckvn = cqn_ref[...], ckvn_ref[...]
        dcqn = jnp.zeros((tm, Q_LORA), F32)
        dckvn = jnp.zeros((tm, KV_LORA), F32)
        dkr = jnp.zeros((tm, KR_W), F32)
        for h in range(N_HEADS):
            sl = slice(h * HEAD_W, (h + 1) * HEAD_W)
            dqh = jnp.transpose(dq_ref[h, 0]) * EXP2_SCALE
            droped = jnp.where(lane < 64, dqh[:, 128:256], 0.0)
            dqp = jnp.concatenate([dqh[:, 0:128], _rope_pair(droped) * c], axis=1).astype(BF16)
            dcqn = dcqn + _dot_nt(dqp, wq_ref[:, sl])
            dwq_ref[:, sl] += _dot_tn(cqn, dqp)
            dkr = dkr + jnp.where(lane < 64, dk_ref[h, :, 128:256], 0.0)
            dkvp = jnp.concatenate([dk_ref[h, :, 0:128], dv_ref[h]], axis=1).astype(BF16)
            dckvn = dckvn + _dot_nt(dkvp, wkv_ref[:, sl])
            dwkv_ref[:, sl] += _dot_tn(ckvn, dkvp)
        cqv, ckvv = cq_ref[...], ckv_ref[...]
        dcq, dgq = _rms_bwd(cqv, _rms_scale(cqv), qn_ref[...], dcqn)
        dckv, dgkv = _rms_bwd(ckvv, _rms_scale(ckvv), kvn_ref[...], dckvn)
        dqn_ref[...] += jnp.sum(dgq, axis=0, keepdims=True)
        dkvn_ref[...] += jnp.sum(dgkv, axis=0, keepdims=True)
        dmla_ref[:, 0:256] = dcq.astype(BF16)
        dmla_ref[:, 256:512] = dckv.astype(BF16)
        dmla_ref[:, 512:640] = (_rope_pair(dkr) * c).astype(BF16)

    hb = lambda w: pl.BlockSpec((N_HEADS, tm, w), lambda i: (0, i, 0))
    wide = N_HEADS * HEAD_W
    per_q = dqt.shape[3] // tm
    dq_spec = pl.BlockSpec((N_HEADS, 1, HEAD_W, tm), lambda i: (0, i // per_q, 0, i % per_q))
    return _pcall(
        body, "mla_bwd", (t // tm,),
        [dq_spec, hb(HEAD_W), hb(V_HEAD), _rows(tm, Q_LORA), _rows(tm, KV_LORA), _rows(tm, Q_LORA), _rows(tm, KV_LORA),
         _rows(tm, KR_W), _full(Q_LORA, wide), _full(KV_LORA, wide), _full(1, Q_LORA), _full(1, KV_LORA)],
        [_rows(tm, 640), _full(Q_LORA, wide), _full(KV_LORA, wide), _full(1, Q_LORA), _full(1, KV_LORA)],
        [_sds((t, 640), BF16), _sds((Q_LORA, wide)), _sds((KV_LORA, wide)), _sds((1, Q_LORA)), _sds((1, KV_LORA))],
    )(dqt, dk, dv, cqn, ckvn, cq, ckv, rope_c, wq, wkv, qn, kvn)


def _lru_bwd(dh, xa, h, rx, conv_w, wa, ba, wx, bx, lam, tb, comm=None):
    t = dh.shape[0]
    nb = t // tb

    def body(dh_ref, xa_ref, h_ref, hp_ref, x_ref, cw_ref, wa_ref, ba_ref, wx_ref, bx_ref, lam_ref,
             drx_ref, dcw_ref, dcb_ref, dwa_ref, dba_ref, dwx_ref, dbx_ref, dlam_ref, gc, dxn, tmp, pre_r, pre_i):
        step = pl.program_id(0)
        first_block = step == nb - 1

        @pl.when(step == 0)
        def _():
            gc[...] = jnp.zeros_like(gc)
            dxn[...] = jnp.zeros_like(dxn)
            for ref in (dcw_ref, dcb_ref, dwa_ref, dba_ref, dwx_ref, dbx_ref, dlam_ref):
                ref[...] = jnp.zeros_like(ref)

        xa = xa_ref[...]
        r, gi = _lru_gates(xa, wa_ref, ba_ref[...], wx_ref, bx_ref[...], pre_r, pre_i)
        lamv = lam_ref[...]
        sp = _softplus(-lamv)
        la = (-LRU_C * sp) * r
        a = jnp.exp(la)
        e2 = _expm1(2.0 * la)
        sq = jnp.sqrt(-e2)
        row = _row_iota((tb, D))
        cf = jnp.where(row == tb - 1, 1.0, pltpu.roll(a, tb - 1, 0))
        c3, b3 = _scan_groups_bwd(cf.reshape(tb // 8, 8, D), dh_ref[...].reshape(tb // 8, 8, D))
        carry = gc[...]
        for grp in reversed(range(tb // 8)):
            dg = b3[grp] + c3[grp] * carry
            pre_r[8 * grp:8 * grp + 8, :] = dg
            carry = dg[0:1, :]
        delta = pre_r[...]
        gc[...] = a[0:1, :] * carry
        hv = h_ref[...]
        hr = pltpu.roll(hv, 1, 0)
        tmp[...] = hr
        tmp[0:1, :] = jnp.where(first_block, 0.0, hp_ref[7:8, :])
        hprev = tmp[...]
        ix = gi * xa
        dla = (delta * hprev) * a - (delta * ix) * ((e2 + 1.0) / sq)
        dlam_ref[...] += jnp.sum(dla * r, axis=0, keepdims=True) * (LRU_C * _sigmoid(-lamv))
        dpr = (dla * (-LRU_C * sp)) * r * (1.0 - r)
        dsq = delta * sq
        dpi = (dsq * xa) * gi * (1.0 - gi)
        dba_ref[...] += jnp.sum(dpr, axis=0, keepdims=True)
        dbx_ref[...] += jnp.sum(dpi, axis=0, keepdims=True)
        pre_r[...] = dpr
        pre_i[...] = dpi
        xb = xa.astype(BF16)
        for n in range(RNN_BLOCKS):
            sl = slice(n * RNN_BW, (n + 1) * RNN_BW)
            dprn = pre_r[:, sl].astype(BF16)
            dpin = pre_i[:, sl].astype(BF16)
            dwa_ref[n] += _dot_tn(xb[:, sl], dprn)
            dwx_ref[n] += _dot_tn(xb[:, sl], dpin)
            tmp[:, sl] = _dot_nt(dprn, wa_ref[n]) + _dot_nt(dpin, wx_ref[n])
        dxa = dsq * gi + tmp[...]
        dcb_ref[...] += jnp.sum(dxa, axis=0, keepdims=True)
        xv = x_ref[...]
        drx = cw_ref[3:4, :] * dxa
        dcw_ref[3:4, :] += jnp.sum(dxa * xv, axis=0, keepdims=True)
        row8 = _row_iota((8, D))
        nxt = dxn[...]
        for s in (1, 2, 3):
            dr_ = pltpu.roll(dxa, tb - s, 0)
            tmp[...] = dr_
            tmp[tb - 8:tb, :] = jnp.where(row8 >= 8 - s, pltpu.roll(nxt, 8 - s, 0), dr_[tb - 8:tb, :])
            dxs = tmp[...]
            drx = drx + cw_ref[3 - s:4 - s, :] * dxs
            dcw_ref[3 - s:4 - s, :] += jnp.sum(dxs * xv, axis=0, keepdims=True)
        drx_ref[...] = drx.astype(BF16)
        dxn[...] = dxa[0:8, :]

    rev = pl.BlockSpec((tb, D), lambda i: (nb - 1 - i, 0))
    prev8 = pl.BlockSpec((8, D), lambda i: (jnp.maximum((nb - 1 - i) * (tb // 8) - 1, 0), 0))
    wblk = _full(RNN_BLOCKS, RNN_BW, RNN_BW)
    return _pcall(
        body, "lru_bwd", (nb,),
        [rev, rev, rev, prev8, rev, _full(4, D), wblk, _full(1, D), wblk, _full(1, D), _full(1, D)],
        [rev, _full(4, D), _full(1, D), wblk, _full(1, D), wblk, _full(1, D), _full(1, D)],
        [_sds((t, D), BF16), _sds((4, D)), _sds((1, D)), _sds((RNN_BLOCKS, RNN_BW, RNN_BW)), _sds((1, D)),
         _sds((RNN_BLOCKS, RNN_BW, RNN_BW)), _sds((1, D)), _sds((1, D))],
        scratch=[pltpu.VMEM((1, D), F32), pltpu.VMEM((8, D), F32), pltpu.VMEM((tb, D), F32), pltpu.VMEM((tb, D), F32),
                 pltpu.VMEM((tb, D), F32)],
        comm=comm,
    )(dh, xa, h, h, rx, conv_w, wa, ba, wx, bx, lam)


def _inproj_bwd(x, dh1, drx, dg3, dmla, w1, g, tm, comm=None):
    t = x.shape[0]

    def body(x_ref, d_ref, drx_ref, dg3_ref, dmla_ref, w_ref, g_ref, dx_ref, gacc_ref):
        @pl.when(pl.program_id(0) == 0)
        def _():
            gacc_ref[...] = jnp.zeros_like(gacc_ref)

        dxn = _dot(drx_ref[...], w_ref[0:D, :])
        for c0 in range(0, 3 * D, D):
            dxn = dxn + _dot(dg3_ref[:, c0:c0 + D], w_ref[D + c0:2 * D + c0, :])
        dxn = dxn + _dot(dmla_ref[...], w_ref[4 * D:W1_COLS, :])
        xv = x_ref[...]
        dx, dgr = _rms_bwd(xv, _rms_scale(xv), g_ref[...], dxn)
        dx_ref[...] = d_ref[...] + dx
        gacc_ref[...] += jnp.sum(dgr, axis=0, keepdims=True)

    return _pcall(
        body, "inproj_bwd", (t // tm,),
        [_rows(tm, D), _rows(tm, D), _rows(tm, D), _rows(tm, 3 * D), _rows(tm, 640), _full(W1_COLS, D), _full(1, D)],
        [_rows(tm, D), _full(1, D)],
        [_sds((t, D)), _sds((1, D))],
        comm=comm,
    )(x, dh1, drx, dg3, dmla, w1, g)


def _pcall_indexed(body, name, index, grid, in_specs, out_specs, out_shape):
    call = pl.pallas_call(
        body, name=name, out_shape=out_shape,
        grid_spec=pltpu.PrefetchScalarGridSpec(num_scalar_prefetch=1, grid=grid, in_specs=in_specs, out_specs=out_specs),
        compiler_params=pltpu.CompilerParams(dimension_semantics=("arbitrary",) * len(grid), vmem_limit_bytes=V7X_VMEM_LIMIT))
    return lambda *operands: call(index, *operands)


V7X_STREAM_BLOCK_BYTES = 5 << 19


def _stream_tile(rows, cols):
    return _row_tile(rows, cap=max(256, V7X_STREAM_BLOCK_BYTES // (4 * cols)), mult=16)


def _pair_sum(halves, theirs, core, out_dtype, name):
    _, rows, cols = halves.shape
    tm = _stream_tile(rows, cols)

    def body(c_ref, a_ref, b_ref, o_ref):
        o_ref[...] = (a_ref[0] + b_ref[...]).astype(out_dtype)

    plain = pl.BlockSpec((tm, cols), lambda i, c: (i, 0))
    return _pcall_indexed(body, name, core, (rows // tm,),
                          [pl.BlockSpec((1, tm, cols), lambda i, c: (c[0], i, 0)), plain], plain,
                          _sds((rows, cols), out_dtype))(halves, theirs)


def _chip_sum(parts, recv, chip, name):
    _, rows, cols = parts.shape
    tm = _stream_tile(rows, cols)

    def body(c_ref, a_ref, r_ref, o_ref):
        o_ref[...] = ((a_ref[0].astype(F32) + r_ref[0].astype(F32)) + r_ref[1].astype(F32)) + r_ref[2].astype(F32)

    return _pcall_indexed(body, name, chip, (rows // tm,),
                          [pl.BlockSpec((1, tm, cols), lambda i, c: (c[0], i, 0)),
                           pl.BlockSpec((N_CHIPS - 1, tm, cols), lambda i, c: (0, i, 0))],
                          pl.BlockSpec((tm, cols), lambda i, c: (i, 0)), _sds((rows, cols)))(parts, recv)


def _adam_math(w, gv, m, v):
    mn = ADAM_B1 * m + (1.0 - ADAM_B1) * gv
    vn = ADAM_B2 * v + (1.0 - ADAM_B2) * (gv * gv)
    m_hat = mn / (1.0 - ADAM_B1 ** ADAM_STEP)
    v_hat = vn / (1.0 - ADAM_B2 ** ADAM_STEP)
    return -ADAM_LR * (m_hat / (jnp.sqrt(v_hat) + ADAM_EPS) + ADAM_WD * w), mn, vn


def _adamw(w, g, m, v, name):
    rows, cols = w.shape
    tm = _row_tile(rows)

    def body(w_ref, g_ref, m_ref, v_ref, d_ref, mo_ref, vo_ref):
        d_ref[...], mo_ref[...], vo_ref[...] = _adam_math(w_ref[...], g_ref[...], m_ref[...], v_ref[...])

    spec = _rows(tm, cols)
    return _pcall(body, name, (rows // tm,), [spec] * 4, [spec] * 3, [_sds((rows, cols))] * 3)(w, g, m, v)


def _adamw_halves(w, mine, theirs, m, v, core, name, by_cols=False):
    rows, cols = w.shape
    if by_cols:
        tm, tc = mine.shape[0] // 2, cols // 2
        grid = (rows // tm, 2)
        full = pl.BlockSpec((tm, tc), lambda i, j, c: (i, j))
        half = pl.BlockSpec((tm, tc), lambda i, j, c: (i, 0))
    else:
        tm = _row_tile(rows // 2)
        nh = rows // 2 // tm
        grid = (rows // tm,)
        full = pl.BlockSpec((tm, cols), lambda i, c: (i, 0))
        half = pl.BlockSpec((tm, cols), lambda i, c: (i % nh, 0))

    def body(c_ref, w_ref, a_ref, b_ref, m_ref, v_ref, g_ref, d_ref, mo_ref, vo_ref):
        which = pl.program_id(1) if by_cols else pl.program_id(0) // nh
        gv = jnp.where(which == c_ref[0], a_ref[...], b_ref[...])
        g_ref[...] = gv
        d_ref[...], mo_ref[...], vo_ref[...] = _adam_math(w_ref[...], gv, m_ref[...], v_ref[...])

    return _pcall_indexed(body, name, core, grid, [full, half, half, full, full], [full] * 4,
                          [_sds((rows, cols))] * 4)(w, mine, theirs, m, v)


REL_SIBLING = (0, 0, 1)
REL_CHIPS = ((1, 0, 0), (0, 1, 0), (1, 1, 0))


V7X_DMA_CHUNK_BYTES = 1 << 20


def _split_copy(src, dst, shape, itemsize):
    nbytes = math.prod(shape) * itemsize
    if nbytes <= V7X_DMA_CHUNK_BYTES or len(shape) < 2:
        return [(src, dst)]
    if len(shape) > 2:
        out = []
        for k in range(shape[0]):
            out += _split_copy(src.at[k], dst.at[k], shape[1:], itemsize)
        return out
    rows = shape[0]
    sub = 8 * (4 // itemsize)
    parts = max(1, min(-(-nbytes // V7X_DMA_CHUNK_BYTES), rows // sub))
    while rows % parts or (rows // parts) % sub:
        parts -= 1
    step = rows // parts
    return [(src.at[pl.ds(k * step, step)], dst.at[pl.ds(k * step, step)]) for k in range(parts)]


def _mesh_pos():
    return (lax.axis_index("x"), lax.axis_index("y"), lax.axis_index("c"))


def _make_copy(i, op, sems, pos, src=None, dst=None):
    rel = op[0]
    src, dst = (op[1], op[2]) if src is None else (src, dst)
    send_sems, recv_sems = sems
    if rel is None:
        return pltpu.make_async_copy(src, dst, send_sems.at[i])
    peer = tuple((p + r) % 2 for p, r in zip(pos, rel))
    return pltpu.make_async_remote_copy(src_ref=src, dst_ref=dst, send_sem=send_sems.at[i], recv_sem=recv_sems.at[i],
                                        device_id=peer, device_id_type=MESH_ID)


def _start_copies(ops, sems, pos, base=0):
    for i, op in enumerate(ops):
        for s_piece, d_piece in _split_copy(op[1], op[2], op[1].shape, jnp.dtype(op[1].dtype).itemsize):
            _make_copy(base + i, op, sems, pos, s_piece, d_piece).start()


def _wait_copies(ops, sems, pos, base=0):
    for i, op in enumerate(ops):
        _make_copy(base + i, op, sems, pos).wait()


def _comm(name, ins, out_shapes, n_ops, ops_fn):
    n_in, n_out = len(ins), len(out_shapes)

    def body(*refs):
        in_refs, out_refs = refs[:n_in], refs[n_in:n_in + n_out]
        sems = refs[n_in + n_out:]
        pos = _mesh_pos()
        ops = ops_fn(in_refs, out_refs, pos)
        assert len(ops) == n_ops
        _start_copies(ops, sems, pos)
        _wait_copies(ops, sems, pos)

    hbm = pl.BlockSpec(memory_space=pl.ANY)
    return pl.pallas_call(
        body, name=name, in_specs=[hbm] * n_in, out_specs=[hbm] * n_out, out_shape=list(out_shapes),
        scratch_shapes=[pltpu.SemaphoreType.DMA((n_ops,)), pltpu.SemaphoreType.DMA((n_ops,))],
    )(*ins)


def _chip_of(pos, rel=(0, 0, 0)):
    return 2 * ((pos[0] + rel[0]) % 2) + (pos[1] + rel[1]) % 2


def _halved_gather_ops(pos, srcs, dsts, whole):
    me, c = _chip_of(pos), pos[2]
    ici, d2d = [], []
    for a, (src, dst) in enumerate(zip(srcs, dsts)):
        for rel in REL_CHIPS:
            if a in whole:
                ici.append((rel, src, dst.at[me]))
            else:
                ici.append((rel, src.at[c], dst.at[me, c]))
                arrived = dst.at[_chip_of(pos, rel), c]
                d2d.append((REL_SIBLING, arrived, arrived))
    return ici, d2d


def _halved_gather_comm(shards):
    srcs = [s.reshape(2, s.shape[0] // 2, s.shape[1]) for s in shards]

    def ops_fn(in_refs, out_refs, pos):
        return _halved_gather_ops(pos, in_refs, out_refs, ())

    def finish(outs, chip):
        return [lax.dynamic_update_index_in_dim(o, s, chip, 0).reshape((N_CHIPS,) + sh.shape) for o, s, sh in zip(outs, srcs, shards)]

    return (srcs, [_sds((N_CHIPS,) + s.shape, s.dtype) for s in srcs], 6 * len(shards), ops_fn), finish


def _gather_halved(shards, whole, chip, name):
    srcs = [s if a in whole or s.ndim == 3 else s.reshape(2, s.shape[0] // 2, s.shape[1]) for a, s in enumerate(shards)]
    n_sh = len(shards)
    n_ici, n_d2d = 3 * n_sh, 3 * (n_sh - len(whole))

    def body(*refs):
        in_refs, out_refs, sems = refs[:n_sh], refs[n_sh:2 * n_sh], refs[2 * n_sh:]
        pos = _mesh_pos()
        ici, d2d = _halved_gather_ops(pos, in_refs, out_refs, whole)
        _start_copies(ici, sems, pos)
        _wait_copies(ici, sems, pos)
        _start_copies(d2d, sems, pos, base=n_ici)
        _wait_copies(d2d, sems, pos, base=n_ici)

    hbm = pl.BlockSpec(memory_space=pl.ANY)
    outs = pl.pallas_call(
        body, name=name, in_specs=[hbm] * n_sh, out_specs=[hbm] * n_sh,
        out_shape=[_sds((N_CHIPS,) + s.shape, s.dtype) for s in srcs],
        scratch_shapes=[pltpu.SemaphoreType.DMA((n_ici + n_d2d,)), pltpu.SemaphoreType.DMA((n_ici + n_d2d,))],
    )(*srcs)
    return [lax.dynamic_update_index_in_dim(o, s, chip, 0).reshape((N_CHIPS,) + sh.shape)
            for o, s, sh in zip(outs, srcs, shards)]


def _split_comm(gs):
    def ops_fn(in_refs, out_refs, pos):
        return [(REL_SIBLING, src.at[1 - pos[2]], dst) for src, dst in zip(in_refs, out_refs)]

    return gs, [_sds(g.shape[1:], g.dtype) for g in gs], len(gs), ops_fn


def _exchange_comm(ps):
    def ops_fn(in_refs, out_refs, pos):
        return [(rel, src.at[_chip_of(pos, rel)], dst.at[j])
                for src, dst in zip(in_refs, out_refs) for j, rel in enumerate(REL_CHIPS)]

    return ps, [_sds((N_CHIPS - 1,) + p.shape[1:], p.dtype) for p in ps], 3 * len(ps), ops_fn


REL_OTHERS = tuple((dx, dy, dc) for dx in (0, 1) for dy in (0, 1) for dc in (0, 1) if dx + dy + dc)


def _join_comm(hs, piece=None):
    n = len(hs)

    def ops_fn(in_refs, out_refs, pos):
        ops = [(REL_SIBLING, src, dst) for src, dst in zip(in_refs[:n], out_refs[:n])]
        if piece is not None:
            ops += [(rel, in_refs[n], out_refs[n].at[2 * _chip_of(pos) + pos[2]]) for rel in REL_OTHERS]
        return ops

    ins, shapes = list(hs), [_sds(h.shape, h.dtype) for h in hs]
    if piece is not None:
        ins, shapes = ins + [piece], shapes + [_sds((2 * N_CHIPS,) + piece.shape, piece.dtype)]
    return ins, shapes, n + (len(REL_OTHERS) if piece is not None else 0), ops_fn


def _run_comm(name, comm):
    ins, shapes, n_ops, ops_fn = comm
    return _comm(name, ins, shapes, n_ops, ops_fn)


def _rope_table(t):
    pos = np.arange(t, dtype=np.float32)
    inv_freq = (np.float32(1.0) / (np.float32(ROPE_THETA) ** (np.arange(0, QK_ROPE, 2, dtype=np.float32) / np.float32(QK_ROPE)))).astype(np.float32)
    ang = (pos[:, None] * inv_freq[None, :]).astype(np.float32)
    return np.concatenate([np.cos(ang), np.cos(ang), np.sin(ang), np.sin(ang)], axis=-1).astype(np.float32)


def _rot_cols(w):
    return jnp.concatenate([-w[..., 32:], w[..., :32]], axis=-1)


def _unrot_cols(dw):
    return jnp.concatenate([dw[..., 32:], -dw[..., :32]], axis=-1)


IN_OFFS = (0, 1024, 2048, 2304, 2560, 2624, 3648, 4672)


def _w1t_from_w_in_t(wt):
    seg = [wt[IN_OFFS[i]:IN_OFFS[i + 1]] for i in range(7)]
    rnn_x, rnn_gate, cq, ckv, kr, ga, gb = seg
    return jnp.concatenate([rnn_x, rnn_gate, ga, gb, cq, ckv, kr, _rot_cols(kr.T).T], axis=0)


def _w_in_t_grad_from_parts(d_rx, d_g3, d_mla):
    kr = d_mla[512:576] + _unrot_cols(d_mla[576:640].T).T
    return jnp.concatenate([d_rx, d_g3[0:D], d_mla[0:512], kr, d_g3[D:3 * D]], axis=0)


def _wq_from_w_uq(w):
    w3 = w.reshape(Q_LORA, N_HEADS, QK_NOPE + QK_ROPE)
    rope = w3[..., QK_NOPE:]
    return jnp.concatenate([w3[..., :QK_NOPE], rope, _rot_cols(rope)], axis=-1).reshape(Q_LORA, N_HEADS * HEAD_W)


def _w_uq_grad_from_wq(dw):
    d3 = dw.reshape(Q_LORA, N_HEADS, HEAD_W)
    rope = d3[..., 128:192] + _unrot_cols(d3[..., 192:256])
    return jnp.concatenate([d3[..., :128], rope], axis=-1).reshape(Q_LORA, N_HEADS * (QK_NOPE + QK_ROPE))


def _cols_from_chunks(g):
    return g.transpose(1, 0, 2).reshape(g.shape[1], N_CHIPS * g.shape[2])


def _halves_of_col_chunks(dw):
    r, c4 = dw.shape
    return dw.reshape(2, r // 2, N_CHIPS, c4 // N_CHIPS).transpose(0, 2, 1, 3)


def _halves_of_row_chunks(dw):
    r4, c = dw.shape
    return dw.reshape(N_CHIPS, 2, r4 // (2 * N_CHIPS), c).transpose(1, 0, 2, 3)


def kernel(x, norm_mix, w_in, conv_w, conv_b, lru_wa, lru_ba, lru_wx, lru_bx, lru_lambda, q_norm, w_uq, kv_norm, w_ukv, w_out, norm_mlp, w_up, w_down, norm_final, loss_target, m_norm_mix, m_w_in, m_conv_w, m_conv_b, m_lru_wa, m_lru_ba, m_lru_wx, m_lru_bx, m_lru_lambda, m_q_norm, m_w_uq, m_kv_norm, m_w_ukv, m_w_out, m_norm_mlp, m_w_up, m_w_down, m_norm_final, v_norm_mix, v_w_in, v_conv_w, v_conv_b, v_lru_wa, v_lru_ba, v_lru_wx, v_lru_bx, v_lru_lambda, v_q_norm, v_w_uq, v_kv_norm, v_w_ukv, v_w_out, v_norm_mlp, v_w_up, v_w_down, v_norm_final):
    t = x.shape[1]
    tm = min(512, t)
    tb = min(256, t)
    tq = min(512, max(tm, t // 4))
    x2 = x[0]
    target = loss_target[0]
    chip = 2 * lax.axis_index("x") + lax.axis_index("y")
    core = lax.axis_index("c")
    chip_ix, core_ix = chip.reshape(1).astype(jnp.int32), core.reshape(1).astype(jnp.int32)
    row = lambda p: p.reshape(1, -1)

    big_shards = (w_in, w_uq, w_ukv, w_out, w_up, w_down)
    w_in_t = w_in.T
    col_halves = lambda z: jnp.stack([z[:, :z.shape[1] // 2], z[:, z.shape[1] // 2:]])
    w_in_g, conv_w_g = _gather_halved([col_halves(w_in_t.astype(BF16)), conv_w], (1,), chip, "weight_gather_first")
    w1 = _w1t_from_w_in_t(jnp.concatenate([w_in_g[:, 0], w_in_g[:, 1]], axis=-1).reshape(IN_OFFS[-1], D))
    conv_w_f = _cols_from_chunks(conv_w_g)
    wa_b, wx_b = lru_wa.astype(BF16), lru_wx.astype(BF16)

    rope_c = jnp.asarray(_rope_table(t))

    comm_a, finish_a = _halved_gather_comm([w.astype(BF16) for w in (w_uq, w_ukv, w_out, w_down)])
    (xn, rx, g3, cq, ckv, kr), gathered = _inproj(x2, row(norm_mix), w1, tm, comm=comm_a)
    g_uq, g_ukv, g_out, g_down = finish_a(gathered, chip)
    wq = _wq_from_w_uq(_cols_from_chunks(g_uq))
    wkv = _cols_from_chunks(g_ukv)
    w_out_f = g_out.reshape(D, D)
    w_down_f = g_down.reshape(D_FF, D)
    comm_b, finish_b = _halved_gather_comm([w_up.astype(BF16)])
    (h, xa), gathered = _lru_fwd(rx, conv_w_f, row(conv_b), wa_b, row(lru_ba), wx_b, row(lru_bx), row(lru_lambda), tb, comm=comm_b)
    w_up_f = _cols_from_chunks(finish_b(gathered, chip)[0])
    q, k, v, cqn, ckvn = _mla_proj(cq, ckv, kr, row(q_norm), row(kv_norm), wq, wkv, rope_c, tm)
    nq = t // tq
    yb, lse = _flash_fwd(q, k, v, tq)
    h1, merged = _merge_out(x2, h, g3, yb, w_out_f, tm)
    u, n2 = _mlp_up(h1, row(norm_mlp), w_up_f, tm)
    act, dh2, loss_blk, g_norm_final = _mlp_down_loss(u, h1, target, w_down_f, row(norm_final), tm)

    g_w_down = _matmul_tn(act, dh2, "grad_w_down", "rows")
    du = _mlp_bwd_act(dh2, u, w_down_f, tm)
    dh1, g_norm_mlp = _mlp_bwd_in(du, dh2, h1, w_up_f, row(norm_mlp), tm)
    g_w_up = _matmul_tn(n2, du, "grad_w_up", "cols")
    dg3, dyb, delta, dh, g_w_out = _merge_bwd(dh1, w_out_f, g3, h, yb, merged, tm)
    delta = delta.reshape(N_HEADS, 8, nq, tq).swapaxes(1, 2)
    def pair_sums(hvs, theirs, dtypes, tag):
        return [_pair_sum(hv.reshape(2, -1, hv.shape[-1]), r.reshape(-1, r.shape[-1]), core_ix, dt, f"grad_pair_sum_{tag}{a}").reshape(r.shape)
                for a, (hv, r, dt) in enumerate(zip(hvs, theirs, dtypes))]

    def chip_sums(parts, received, tag):
        return [_chip_sum(p, r, chip_ix, f"grad_chip_sum_{tag}{a}") for a, (p, r) in enumerate(zip(parts, received))]

    early = [_halves_of_row_chunks(g_w_out), g_w_up, g_w_down]
    (dq, dk, dv), early_theirs = _flash_bwd(q, k, v, dyb, lse, delta, tq, comm=_split_comm(early))
    early_parts = pair_sums(early, early_theirs, [BF16] * 3, "early")
    dmla, g_wq, g_wkv, g_q_norm, g_kv_norm = _mla_bwd(dq, dk, dv, cqn, ckvn, cq, ckv, rope_c, wq, wkv, row(q_norm), row(kv_norm), tm)
    (drx, g_conv_w, g_conv_b, g_wa, g_ba, g_wx, g_bx, g_lam), early_received = _lru_bwd(
        dh, xa, h, rx, conv_w_f, wa_b, row(lru_ba), wx_b, row(lru_bx), row(lru_lambda), tb, comm=_exchange_comm(early_parts))
    early_reduced = chip_sums(early_parts, early_received, "early")
    grad_x, g_norm_mix = _inproj_bwd(x2, dh1, drx, dg3, dmla, w1, row(norm_mix), tm)
    g_w_in_gates, early_sibling = _matmul_tn(dg3, xn, "grad_w_in_gates", comm=_join_comm(early_reduced))
    g_w_in_t = _w_in_t_grad_from_parts(_matmul_tn(drx, xn, "grad_w_in_rx"), g_w_in_gates, _matmul_tn(dmla, xn, "grad_w_in_mla"))
    g_w_uq = _w_uq_grad_from_wq(g_wq)

    smalls = (g_norm_mix, g_conv_b, g_wa, g_ba, g_wx, g_bx, g_lam, g_q_norm, g_kv_norm, g_norm_mlp, g_norm_final, g_conv_w)
    s_flat = jnp.concatenate([s.reshape(-1) for s in smalls] + [loss_blk[0, 0:1], jnp.zeros((S_LEN - N_SMALL - CONVW_SIZE - 1,), F32)])
    g_w_in_halves = lax.optimization_barrier(col_halves(g_w_in_t))
    late = [g_w_in_halves.reshape(2, N_CHIPS, IN_OFFS[-1] // N_CHIPS, D // 2), _halves_of_col_chunks(g_w_uq), _halves_of_col_chunks(g_wkv),
            s_flat.reshape(N_CHIPS, 2, S_ROWS_HALF, 128).transpose(1, 0, 2, 3)]
    late_theirs = _run_comm("grad_sibling_split", _split_comm(late))
    late_parts = pair_sums(late, late_theirs, [BF16] * 3 + [F32], "late")
    late_reduced = chip_sums(late_parts, _run_comm("grad_chip_exchange", _exchange_comm(late_parts)), "late")
    late_sibling = _run_comm("grad_sibling_join", _join_comm(late_reduced[:3], late_reduced[3]))
    reduced = late_reduced[:3] + early_reduced
    reduced_sibling = list(late_sibling[:3]) + list(early_sibling)
    s_all = lax.dynamic_update_index_in_dim(late_sibling[3], late_reduced[3], 2 * chip + core, 0).reshape(-1)

    small_grads = []
    off = 0
    for shp, n in zip(SMALL_SHAPES, SMALL_SIZES):
        small_grads.append(s_all[off:off + n].reshape(shp))
        off += n
    g_conv_w_mine = lax.dynamic_slice_in_dim(s_all[off:off + CONVW_SIZE].reshape(4, D), chip * (D // N_CHIPS), D // N_CHIPS, axis=1)
    loss = s_all[off + CONVW_SIZE]

    big_m = (m_w_in, m_w_uq, m_w_ukv, m_w_out, m_w_up, m_w_down)
    big_v = (v_w_in, v_w_uq, v_w_ukv, v_w_out, v_w_up, v_w_down)
    big_names = ("w_in", "w_uq", "w_ukv", "w_out", "w_up", "w_down")
    big_upd = [_adamw_halves(w, gm, gt, m, v, core_ix, "adamw_" + n)
               for w, gm, gt, m, v, n in zip(big_shards[1:], reduced[1:], reduced_sibling[1:], big_m[1:], big_v[1:], big_names[1:])]
    w_in_upd = _adamw_halves(w_in_t, reduced[0], reduced_sibling[0], m_w_in.T, v_w_in.T, core_ix, "adamw_w_in", by_cols=True)
    big_upd = [[u.T for u in w_in_upd]] + big_upd

    small_w = (norm_mix, conv_b, lru_wa, lru_ba, lru_wx, lru_bx, lru_lambda, q_norm, kv_norm, norm_mlp, norm_final)
    small_m = (m_norm_mix, m_conv_b, m_lru_wa, m_lru_ba, m_lru_wx, m_lru_bx, m_lru_lambda, m_q_norm, m_kv_norm, m_norm_mlp, m_norm_final)
    small_v = (v_norm_mix, v_conv_b, v_lru_wa, v_lru_ba, v_lru_wx, v_lru_bx, v_lru_lambda, v_q_norm, v_kv_norm, v_norm_mlp, v_norm_final)

    def pack(items, last, fill):
        flat = jnp.concatenate([i.reshape(-1) for i in items] + [last.reshape(-1)])
        return jnp.concatenate([flat, jnp.full((PACK_ROWS * 128 - flat.shape[0],), fill, F32)]).reshape(PACK_ROWS, 128)

    packed = _adamw(pack(small_w, conv_w, 0.0), pack(small_grads, g_conv_w_mine, 0.0), pack(small_m, m_conv_w, 0.0),
                    pack(small_v, v_conv_w, 1.0), "adamw_small")

    def unpack(p):
        flat = p.reshape(-1)
        outs, o = [], 0
        for shp, n in zip(SMALL_SHAPES, SMALL_SIZES):
            outs.append(flat[o:o + n].reshape(shp))
            o += n
        return outs, flat[o:o + CONVW_SIZE // N_CHIPS].reshape(4, D // N_CHIPS)

    order = ("norm_mix", "w_in", "conv_w", "conv_b", "lru_wa", "lru_ba", "lru_wx", "lru_bx", "lru_lambda", "q_norm", "w_uq", "kv_norm",
             "w_ukv", "w_out", "norm_mlp", "w_up", "w_down", "norm_final")

    def assemble(small_list, conv_w_item, big_list):
        table = dict(zip(SMALL_NAMES, small_list))
        table["conv_w"] = conv_w_item
        table.update(zip(big_names, big_list))
        return [table[n] for n in order]

    outs = [loss, grad_x.reshape(1, t, D)]
    outs += assemble(small_grads, g_conv_w_mine, [b[0] for b in big_upd])
    for j in range(3):
        sm, cw = unpack(packed[j])
        outs += assemble(sm, cw, [b[j + 1] for b in big_upd])
    return tuple(outs)
```

```python
import functools
import math

import jax
import jax.numpy as jnp
import numpy as np
from jax import lax
from jax.experimental import pallas as pl
from jax.experimental.pallas import tpu as pltpu

F32 = jnp.float32
BF16 = jnp.bfloat16

D = 1024
N_HEADS = 8
QK_NOPE = 128
QK_ROPE = 64
V_HEAD = 128
Q_LORA = 256
KV_LORA = 256
D_FF = 4096
RNN_BLOCKS = 8
RNN_BW = 128
LRU_C = 8.0
EPS = 1e-6
ROPE_THETA = 10000.0
HEAD_W = 256
KR_W = 128
W1_COLS = 4 * D + Q_LORA + KV_LORA + KR_W
SM_SCALE = (QK_NOPE + QK_ROPE) ** -0.5
EXP2_SCALE = SM_SCALE * math.log2(math.e)
NEG = float(jnp.finfo(jnp.float32).min)

ADAM_LR = 0.001
ADAM_B1 = 0.9
ADAM_B2 = 0.999
ADAM_EPS = 1e-08
ADAM_WD = 0.01
ADAM_STEP = 10

N_CHIPS = 4
V7X_VMEM_LIMIT = 56 * 1024 * 1024
MESH_ID = pl.DeviceIdType.MESH

SMALL_NAMES = ("norm_mix", "conv_b", "lru_wa", "lru_ba", "lru_wx", "lru_bx", "lru_lambda", "q_norm", "kv_norm", "norm_mlp", "norm_final")
SMALL_SHAPES = ((D,), (D,), (RNN_BLOCKS, RNN_BW, RNN_BW), (RNN_BLOCKS, RNN_BW), (RNN_BLOCKS, RNN_BW, RNN_BW), (RNN_BLOCKS, RNN_BW), (D,),
                (Q_LORA,), (KV_LORA,), (D,), (D,))
SMALL_SIZES = tuple(math.prod(s) for s in SMALL_SHAPES)
N_SMALL = sum(SMALL_SIZES)
CONVW_SIZE = 4 * D
S_LEN = -(-(N_SMALL + CONVW_SIZE) // 8192) * 8192
S_ROWS_HALF = S_LEN // (N_CHIPS * 2 * 128)
PACK_ROWS = -(-(N_SMALL + CONVW_SIZE // N_CHIPS) // (256 * 128)) * 256


def _pcall(body, name, grid, in_specs, out_specs, out_shape, scratch=(), comm=None):
    params = pltpu.CompilerParams(dimension_semantics=("arbitrary",) * len(grid), vmem_limit_bytes=V7X_VMEM_LIMIT)
    if comm is None:
        return pl.pallas_call(body, name=name, grid=grid, in_specs=in_specs, out_specs=out_specs, out_shape=out_shape,
                              scratch_shapes=list(scratch), compiler_params=params)
    c_ins, c_shapes, n_ops, ops_fn = comm
    single = not isinstance(out_specs, (list, tuple))
    out_specs, out_shape = ([out_specs], [out_shape]) if single else (list(out_specs), list(out_shape))
    n_in, n_out, n_sc, n_ci, n_co = len(in_specs), len(out_specs), len(scratch), len(c_ins), len(c_shapes)

    def wrapped(*refs):
        ins, refs = refs[:n_in], refs[n_in:]
        c_in_refs, refs = refs[:n_ci], refs[n_ci:]
        outs, refs = refs[:n_out], refs[n_out:]
        c_out_refs, refs = refs[:n_co], refs[n_co:]
        own_scratch, sems = refs[:n_sc], refs[n_sc:]
        pos = _mesh_pos()
        ops = ops_fn(c_in_refs, c_out_refs, pos)
        ops, then = ops if isinstance(ops, tuple) else (ops, [])
        assert len(ops) + len(then) == n_ops
        first, last = True, True
        for d, n in enumerate(grid):
            first = first & (pl.program_id(d) == 0)
            last = last & (pl.program_id(d) == n - 1)

        @pl.when(first)
        def _():
            _start_copies(ops, sems, pos)

        body(*ins, *outs, *own_scratch)

        @pl.when(last)
        def _():
            _wait_copies(ops, sems, pos)
            _start_copies(then, sems, pos, base=len(ops))
            _wait_copies(then, sems, pos, base=len(ops))

    hbm = pl.BlockSpec(memory_space=pl.ANY)
    call = pl.pallas_call(
        wrapped, name=name, grid=grid, in_specs=list(in_specs) + [hbm] * n_ci, out_specs=out_specs + [hbm] * n_co,
        out_shape=out_shape + list(c_shapes),
        scratch_shapes=list(scratch) + [pltpu.SemaphoreType.DMA((n_ops,)), pltpu.SemaphoreType.DMA((n_ops,))],
        compiler_params=params)

    def run(*operands):
        res = call(*operands, *c_ins)
        own = res[0] if single else res[:n_out]
        return own, res[n_out:]

    return run


def _rows(tm, w):
    return pl.BlockSpec((tm, w), lambda i: (i, 0))


def _full(*shape):
    return pl.BlockSpec(shape, lambda *_: (0,) * len(shape))


def _sds(shape, dtype=F32):
    return jax.ShapeDtypeStruct(shape, dtype)


def _row_tile(rows, cap=256, mult=8):
    t = min(rows, cap)
    while rows % t or t % mult:
        t -= 1
    return t


def _dot(a, b):
    return jnp.dot(a, b, preferred_element_type=F32)


def _dot_nt(a, b):
    return lax.dot_general(a, b, (((1,), (1,)), ((), ())), preferred_element_type=F32)


def _dot_tn(a, b):
    return lax.dot_general(a, b, (((0,), (0,)), ((), ())), preferred_element_type=F32)


def _sigmoid(x):
    return 1.0 / (1.0 + jnp.exp(-x))


_GELU_C = math.sqrt(2.0 / math.pi)


def _gelu(x):
    return x * (0.5 * (1.0 + jnp.tanh(_GELU_C * (x + 0.044715 * (x * x * x)))))


def _gelu_grad(x):
    t = jnp.tanh(_GELU_C * (x + 0.044715 * (x * x * x)))
    cdf = 0.5 * (1.0 + t)
    return cdf + x * (0.5 * (1.0 - t * t) * _GELU_C * (1.0 + 3.0 * 0.044715 * (x * x)))


def _rms_scale(x):
    return lax.rsqrt(jnp.mean(x * x, axis=-1, keepdims=True) + EPS)


def _rms_bwd(x, rs, g, dy):
    gdy = dy * g
    dx = rs * gdy - x * ((rs * rs * rs) * jnp.mean(gdy * x, axis=-1, keepdims=True))
    return dx, dy * (x * rs)


def _log1p(e):
    u = 1.0 + e
    d = u - 1.0
    return jnp.where(d == 0.0, e, jnp.log(u) * (e / jnp.where(d == 0.0, 1.0, d)))


def _softplus(y):
    return jnp.maximum(y, 0.0) + _log1p(jnp.exp(-jnp.abs(y)))


def _expm1(x):
    u = jnp.exp(x)
    lu = jnp.log(u)
    safe = jnp.where((u == 1.0) | (u == 0.0), 1.0, lu)
    return jnp.where(u == 1.0, x, jnp.where(u == 0.0, -1.0, (u - 1.0) * (x / safe)))


def _row_iota(shape):
    return lax.broadcasted_iota(jnp.int32, shape, 0)


def _lane_iota(shape):
    return lax.broadcasted_iota(jnp.int32, shape, 1)


def _scan_groups_fwd(a, b):
    sub = lax.broadcasted_iota(jnp.int32, a.shape, 1)
    for sh in (1, 2, 4):
        m = sub >= sh
        b = jnp.where(m, a * pltpu.roll(b, sh, 1) + b, b)
        a = jnp.where(m, a * pltpu.roll(a, sh, 1), a)
    return a, b


def _scan_groups_bwd(c, b):
    sub = lax.broadcasted_iota(jnp.int32, c.shape, 1)
    for sh in (1, 2, 4):
        m = sub < 8 - sh
        b = jnp.where(m, b + c * pltpu.roll(b, 8 - sh, 1), b)
        c = jnp.where(m, c * pltpu.roll(c, 8 - sh, 1), c)
    return c, b


def _rope_pair(gc):
    return gc + pltpu.roll(gc, 64, 1)


def _inproj(x, g, w1, tm, comm=None):
    t = x.shape[0]
    widths = (D, 3 * D, Q_LORA, KV_LORA, KR_W)

    def body(x_ref, g_ref, w_ref, xn_ref, rx_ref, g3_ref, cq_ref, ckv_ref, kr_ref):
        xv = x_ref[...]
        xn = (xv * _rms_scale(xv) * g_ref[...]).astype(BF16)
        xn_ref[...] = xn
        col = 0
        for ref, w in zip((rx_ref, g3_ref, cq_ref, ckv_ref, kr_ref), widths):
            for c0 in range(0, w, 512):
                cw = min(512, w - c0)
                ref[:, c0:c0 + cw] = _dot_nt(xn, w_ref[col + c0:col + c0 + cw, :])
            col += w

    return _pcall(
        body, "inproj", (t // tm,),
        [_rows(tm, D), _full(1, D), _full(W1_COLS, D)],
        [_rows(tm, D)] + [_rows(tm, w) for w in widths],
        [_sds((t, D), BF16)] + [_sds((t, w)) for w in widths],
        comm=comm,
    )(x, g, w1)


def _lru_gates(xa, wa_ref, ba, wx_ref, bx, pre_r, pre_i):
    xb = xa.astype(BF16)
    for n in range(RNN_BLOCKS):
        sl = slice(n * RNN_BW, (n + 1) * RNN_BW)
        pre_r[:, sl] = _dot(xb[:, sl], wa_ref[n])
        pre_i[:, sl] = _dot(xb[:, sl], wx_ref[n])
    r = _sigmoid(pre_r[...] + ba)
    i = _sigmoid(pre_i[...] + bx)
    return r, i


def _lru_fwd(rx, conv_w, conv_b, wa, ba, wx, bx, lam, tb, comm=None):
    t = rx.shape[0]
    nb = t // tb

    def body(x_ref, xp_ref, cw_ref, cb_ref, wa_ref, ba_ref, wx_ref, bx_ref, lam_ref, h_ref, xa_ref, hc, tmp, pre_r, pre_i):
        i_blk = pl.program_id(0)

        @pl.when(i_blk == 0)
        def _():
            hc[...] = jnp.zeros_like(hc)

        xv = x_ref[...]
        xp = jnp.where(i_blk > 0, xp_ref[...], 0.0)
        row8 = _row_iota((8, D))
        xa = cb_ref[...] + cw_ref[3:4, :] * xv
        for s in (1, 2, 3):
            xr = pltpu.roll(xv, s, 0)
            tmp[...] = xr
            tmp[0:8, :] = jnp.where(row8 < s, pltpu.roll(xp, s, 0), xr[0:8, :])
            xa = xa + cw_ref[3 - s:4 - s, :] * tmp[...]
        xa_ref[...] = xa
        r, gi = _lru_gates(xa, wa_ref, ba_ref[...], wx_ref, bx_ref[...], pre_r, pre_i)
        la = (-LRU_C * _softplus(-lam_ref[...])) * r
        a = jnp.exp(la)
        b = jnp.sqrt(-_expm1(2.0 * la)) * (gi * xa)
        a3, b3 = _scan_groups_fwd(a.reshape(tb // 8, 8, D), b.reshape(tb // 8, 8, D))
        carry = hc[...]
        for grp in range(tb // 8):
            hg = a3[grp] * carry + b3[grp]
            h_ref[8 * grp:8 * grp + 8, :] = hg
            carry = hg[7:8, :]
        hc[...] = carry

    prev8 = pl.BlockSpec((8, D), lambda i: (jnp.maximum(i * (tb // 8) - 1, 0), 0))
    return _pcall(
        body, "lru_fwd", (nb,),
        [_rows(tb, D), prev8, _full(4, D), _full(1, D), _full(RNN_BLOCKS, RNN_BW, RNN_BW), _full(1, D),
         _full(RNN_BLOCKS, RNN_BW, RNN_BW), _full(1, D), _full(1, D)],
        [_rows(tb, D), _rows(tb, D)],
        [_sds((t, D)), _sds((t, D))],
        scratch=[pltpu.VMEM((1, D), F32), pltpu.VMEM((tb, D), F32), pltpu.VMEM((tb, D), F32), pltpu.VMEM((tb, D), F32)],
        comm=comm,
    )(rx, rx, conv_w, conv_b, wa, ba, wx, bx, lam)


def _mla_proj(cq, ckv, kr, qn, kvn, wq, wkv, rope_c, tm):
    t = cq.shape[0]

    def body(cq_ref, ckv_ref, kr_ref, qn_ref, kvn_ref, wq_ref, wkv_ref, c_ref, q_ref, k_ref, v_ref, cqn_ref, ckvn_ref):
        cqv = cq_ref[...]
        cqn = (cqv * _rms_scale(cqv) * qn_ref[...]).astype(BF16)
        ckvv = ckv_ref[...]
        ckvn = (ckvv * _rms_scale(ckvv) * kvn_ref[...]).astype(BF16)
        cqn_ref[...] = cqn
        ckvn_ref[...] = ckvn
        c = c_ref[...]
        lane = _lane_iota((tm, KR_W))
        kro = jnp.where(lane < 64, _rope_pair(kr_ref[...] * c), 0.0).astype(BF16)
        for h in range(N_HEADS):
            sl = slice(h * HEAD_W, (h + 1) * HEAD_W)
            qh = _dot(cqn, wq_ref[:, sl])
            q_ref[h, :, 0:128] = (qh[:, 0:128] * EXP2_SCALE).astype(BF16)
            q_ref[h, :, 128:256] = (_rope_pair(qh[:, 128:256] * c) * EXP2_SCALE).astype(BF16)
            kvh = _dot(ckvn, wkv_ref[:, sl])
            k_ref[h, :, 0:128] = kvh[:, 0:128].astype(BF16)
            k_ref[h, :, 128:256] = kro
            v_ref[h, :, 0:V_HEAD] = kvh[:, 128:256].astype(BF16)
            v_ref[h, :, V_HEAD:2 * V_HEAD] = jnp.ones((tm, V_HEAD), BF16)

    hb = lambda w: pl.BlockSpec((N_HEADS, tm, w), lambda i: (0, i, 0))
    return _pcall(
        body, "mla_proj", (t // tm,),
        [_rows(tm, Q_LORA), _rows(tm, KV_LORA), _rows(tm, KR_W), _full(1, Q_LORA), _full(1, KV_LORA),
         _full(Q_LORA, N_HEADS * HEAD_W), _full(KV_LORA, N_HEADS * HEAD_W), _rows(tm, KR_W)],
        [hb(HEAD_W), hb(HEAD_W), hb(2 * V_HEAD), _rows(tm, Q_LORA), _rows(tm, KV_LORA)],
        [_sds((N_HEADS, t, HEAD_W), BF16), _sds((N_HEADS, t, HEAD_W), BF16), _sds((N_HEADS, t, 2 * V_HEAD), BF16),
         _sds((t, Q_LORA), BF16), _sds((t, KV_LORA), BF16)],
    )(cq, ckv, kr, qn, kvn, wq, wkv, rope_c)


def _flash_fwd(q, k, v, tq, comm=None):
    t = q.shape[1]
    nq = t // tq

    def body(q_ref, k_ref, v_ref, o_ref, lse_ref, s_even, s_odd):
        qi = pl.program_id(1)
        qv = q_ref[0]

        def scores(ki, buf):
            buf[...] = _dot_nt(qv, k_ref[0, pl.ds(pl.multiple_of(ki * tq, tq), tq), :])

        def softmax_pv(ki, buf, carry, diagonal):
            m, acc = carry
            s = buf[...]
            if diagonal:
                s = jnp.where(_row_iota((tq, tq)) >= _lane_iota((tq, tq)), s, NEG)
            m_new = jnp.maximum(m, jnp.max(s, axis=1, keepdims=True))
            p = jnp.exp2(s - m_new)
            alpha = jnp.exp2(m - m_new)
            acc = alpha * acc + _dot(p.astype(BF16), v_ref[0, pl.ds(pl.multiple_of(ki * tq, tq), tq), :])
            return m_new, acc

        def finish(carry):
            m, acc = carry
            l = acc[:, V_HEAD:2 * V_HEAD]
            o_ref[...] = acc[:, 0:V_HEAD] / l
            lse = m + jnp.log(l) * math.log2(math.e)
            lse_ref[0, 0] = jnp.transpose(lse)[0:8, :]

        def two(i, carry):
            scores(2 * i + 1, s_odd)
            carry = softmax_pv(2 * i, s_even, carry, False)
            scores(2 * i + 2, s_even)
            return softmax_pv(2 * i + 1, s_odd, carry, False)

        init = (jnp.full((tq, 1), -jnp.inf, F32), jnp.zeros((tq, 2 * V_HEAD), F32))
        scores(0, s_even)
        carry = lax.fori_loop(0, qi // 2, two, init)

        @pl.when(qi % 2 == 0)
        def _():
            finish(softmax_pv(qi, s_even, carry, True))

        @pl.when(qi % 2 == 1)
        def _():
            scores(qi, s_odd)
            finish(softmax_pv(qi, s_odd, softmax_pv(qi - 1, s_even, carry, False), True))

    head = lambda w: pl.BlockSpec((1, t, w), lambda h, qi: (h, 0, 0))
    return _pcall(
        body, "flash_fwd", (N_HEADS, nq),
        [pl.BlockSpec((1, tq, HEAD_W), lambda h, qi: (h, qi, 0)), head(HEAD_W), head(2 * V_HEAD)],
        [pl.BlockSpec((tq, V_HEAD), lambda h, qi: (qi, h)), pl.BlockSpec((1, 1, 8, tq), lambda h, qi: (h, qi, 0, 0))],
        [_sds((t, D)), _sds((N_HEADS, nq, 8, tq))],
        scratch=[pltpu.VMEM((tq, tq), F32), pltpu.VMEM((tq, tq), F32)],
        comm=comm,
    )(q, k, v)


def _merge_out(x, h, g3, yb, w_out, tm):
    t = x.shape[0]

    def body(x_ref, h_ref, g3_ref, yb_ref, w_ref, h1_ref, mg_ref):
        ya = h_ref[...] * _gelu(g3_ref[:, 0:D])
        merged = (_sigmoid(g3_ref[:, D:2 * D]) * ya + _sigmoid(g3_ref[:, 2 * D:3 * D]) * yb_ref[...]).astype(BF16)
        mg_ref[...] = merged
        h1_ref[...] = x_ref[...] + _dot(merged, w_ref[...])

    return _pcall(
        body, "merge_out", (t // tm,),
        [_rows(tm, D), _rows(tm, D), _rows(tm, 3 * D), _rows(tm, D), _full(D, D)],
        [_rows(tm, D), _rows(tm, D)],
        [_sds((t, D)), _sds((t, D), BF16)],
    )(x, h, g3, yb, w_out)


def _mlp_up(h1, g, w_up, tm):
    t = h1.shape[0]

    def body(h_ref, g_ref, w_ref, u_ref, n2_ref):
        hv = h_ref[...]
        n2 = (hv * _rms_scale(hv) * g_ref[...]).astype(BF16)
        n2_ref[...] = n2
        for c0 in range(0, D_FF, 512):
            u_ref[:, c0:c0 + 512] = _dot(n2, w_ref[:, c0:c0 + 512])

    return _pcall(
        body, "mlp_up", (t // tm,),
        [_rows(tm, D), _full(1, D), _full(D, D_FF)],
        [_rows(tm, D_FF), _rows(tm, D)],
        [_sds((t, D_FF)), _sds((t, D), BF16)],
    )(h1, g, w_up)


def _mlp_down_loss(u, h1, target, w_down, g, tm):
    t = u.shape[0]

    def body(u_ref, h1_ref, tg_ref, w_ref, g_ref, act_ref, dh2_ref, loss_ref, gnf_ref, lacc):
        i = pl.program_id(0)

        @pl.when(i == 0)
        def _():
            lacc[...] = jnp.zeros_like(lacc)
            gnf_ref[...] = jnp.zeros_like(gnf_ref)

        ru = jnp.maximum(u_ref[...], 0.0)
        act = (ru * ru).astype(BF16)
        act_ref[...] = act
        h2 = h1_ref[...] + _dot(act, w_ref[...])
        rs = _rms_scale(h2)
        gv = g_ref[...]
        err = h2 * rs * gv - tg_ref[...]
        lacc[...] += jnp.sum(err * err, axis=0, keepdims=True)
        dx, dgr = _rms_bwd(h2, rs, gv, err * (1.0 / D))
        dh2_ref[...] = dx
        gnf_ref[...] += jnp.sum(dgr, axis=0, keepdims=True)

        @pl.when(i == pl.num_programs(0) - 1)
        def _():
            loss_ref[...] = jnp.broadcast_to(jnp.sum(lacc[...], axis=1, keepdims=True) * (0.5 / D), (8, 128))

    return _pcall(
        body, "mlp_down_loss", (t // tm,),
        [_rows(tm, D_FF), _rows(tm, D), _rows(tm, D), _full(D_FF, D), _full(1, D)],
        [_rows(tm, D_FF), _rows(tm, D), _full(8, 128), _full(1, D)],
        [_sds((t, D_FF), BF16), _sds((t, D)), _sds((8, 128)), _sds((1, D))],
        scratch=[pltpu.VMEM((1, D), F32)],
    )(u, h1, target, w_down, g)


def _matmul_tn(a, g, name, chunked=None, comm=None):
    t, kdim = a.shape
    ndim = g.shape[1]
    tk = min(kdim, 1024)
    tn = ndim if ndim <= 1024 else 1024
    if chunked == "cols":
        assert tk == kdim and tn == ndim // N_CHIPS
    elif chunked == "rows":
        assert tk == kdim // N_CHIPS and tn == ndim
    tt = min(t, 4096 if a.dtype == BF16 and g.dtype == BF16 else 2048)
    nt = t // tt

    def body(a_ref, g_ref, o_ref):
        @pl.when(pl.program_id(2) == 0)
        def _():
            o_ref[...] = jnp.zeros_like(o_ref)

        o_ref[...] += _dot_tn(a_ref[...].astype(BF16), g_ref[...].astype(BF16)).reshape(o_ref.shape)

    if chunked is not None:
        out_spec = pl.BlockSpec((2, None, tk // 2, tn), lambda i, j, s: (0, i + j, 0, 0))
        out_shape = _sds((2, N_CHIPS, tk // 2, tn))
    else:
        out_spec, out_shape = pl.BlockSpec((tk, tn), lambda i, j, s: (i, j)), _sds((kdim, ndim))
    return _pcall(
        body, name, (kdim // tk, ndim // tn, nt),
        [pl.BlockSpec((tt, tk), lambda i, j, s: (s, i)), pl.BlockSpec((tt, tn), lambda i, j, s: (s, j))],
        out_spec, out_shape, comm=comm,
    )(a, g)


def _mlp_bwd_act(dh2, u, w_down, tm):
    t = u.shape[0]

    def body(d_ref, u_ref, w_ref, du_ref):
        db = d_ref[...].astype(BF16)
        for c0 in range(0, D_FF, 512):
            da = _dot_nt(db, w_ref[c0:c0 + 512, :])
            du_ref[:, c0:c0 + 512] = (da * (2.0 * jnp.maximum(u_ref[:, c0:c0 + 512], 0.0))).astype(BF16)

    return _pcall(
        body, "mlp_bwd_act", (t // tm,),
        [_rows(tm, D), _rows(tm, D_FF), _full(D_FF, D)],
        _rows(tm, D_FF), _sds((t, D_FF), BF16),
    )(dh2, u, w_down)


def _mlp_bwd_in(du, dh2, h1, w_up, g, tm):
    t = du.shape[0]

    def body(du_ref, d_ref, h_ref, w_ref, g_ref, dh1_ref, gacc_ref):
        @pl.when(pl.program_id(0) == 0)
        def _():
            gacc_ref[...] = jnp.zeros_like(gacc_ref)

        dn2 = _dot_nt(du_ref[...], w_ref[...])
        hv = h_ref[...]
        dx, dgr = _rms_bwd(hv, _rms_scale(hv), g_ref[...], dn2)
        dh1_ref[...] = d_ref[...] + dx
        gacc_ref[...] += jnp.sum(dgr, axis=0, keepdims=True)

    return _pcall(
        body, "mlp_bwd_in", (t // tm,),
        [_rows(tm, D_FF), _rows(tm, D), _rows(tm, D), _full(D, D_FF), _full(1, D)],
        [_rows(tm, D), _full(1, D)],
        [_sds((t, D)), _sds((1, D))],
    )(du, dh2, h1, w_up, g)


def _merge_bwd(dh1, w_out, g3, h, yb, merged, tm):
    t = dh1.shape[0]

    def body(d_ref, w_ref, g3_ref, h_ref, yb_ref, mg_ref, dg3_ref, dyb_ref, dl_ref, dh_ref, dwo_ref):
        @pl.when(pl.program_id(0) == 0)
        def _():
            dwo_ref[...] = jnp.zeros_like(dwo_ref)

        db = d_ref[...].astype(BF16)
        dwo_ref[...] += _dot_tn(mg_ref[...], db)
        dm = _dot_nt(db, w_ref[...])
        gv = g3_ref[:, 0:D]
        sa = _sigmoid(g3_ref[:, D:2 * D])
        sb = _sigmoid(g3_ref[:, 2 * D:3 * D])
        gel = _gelu(gv)
        hv = h_ref[...]
        ybv = yb_ref[...]
        dya = dm * sa
        dyb = dm * sb
        dg3_ref[:, 0:D] = (dya * hv * _gelu_grad(gv)).astype(BF16)
        dg3_ref[:, D:2 * D] = (dya * (hv * gel) * (1.0 - sa)).astype(BF16)
        dg3_ref[:, 2 * D:3 * D] = (dyb * ybv * (1.0 - sb)).astype(BF16)
        dh_ref[...] = dya * gel
        dyb_ref[...] = dyb.astype(BF16)
        prod = dyb * ybv
        ones = jnp.ones((8, V_HEAD), F32)
        for hh in range(N_HEADS):
            dl_ref[hh] = lax.dot_general(ones, prod[:, hh * V_HEAD:(hh + 1) * V_HEAD], (((1,), (1,)), ((), ())),
                                         precision=lax.Precision.HIGHEST, preferred_element_type=F32)

    return _pcall(
        body, "merge_bwd", (t // tm,),
        [_rows(tm, D), _full(D, D), _rows(tm, 3 * D), _rows(tm, D), _rows(tm, D), _rows(tm, D)],
        [_rows(tm, 3 * D), _rows(tm, D), pl.BlockSpec((N_HEADS, 8, tm), lambda i: (0, 0, i)), _rows(tm, D), _full(D, D)],
        [_sds((t, 3 * D), BF16), _sds((t, D), BF16), _sds((N_HEADS, 8, t)), _sds((t, D)), _sds((D, D))],
    )(dh1, w_out, g3, h, yb, merged)


def _flash_bwd(q, k, v, do, lse, delta, tq, comm=None):
    t = q.shape[1]
    nq = t // tq

    def body(q_ref, k_ref, v_ref, do_ref, lse_ref, dl_ref, dqt_ref, dk_ref, dv_ref):
        ki = pl.program_id(1)

        @pl.when(ki == 0)
        def _():
            dqt_ref[...] = jnp.zeros_like(dqt_ref)

        kblk, vblk = k_ref[0], v_ref[0]
        kt = jnp.transpose(kblk)

        def block(qi, diagonal):
            rows = pl.ds(pl.multiple_of(qi * tq, tq), tq)
            qv, dov = q_ref[0, rows, :], do_ref[rows, :]
            p = jnp.exp2(_dot_nt(kblk, qv) - lse_ref[0, qi, 0:1, :])
            if diagonal:
                p = jnp.where(_lane_iota((tq, tq)) >= _row_iota((tq, tq)), p, 0.0)
            dv = _dot(p.astype(BF16), dov)
            dp = _dot_nt(vblk, dov)
            ds = (p * (dp - dl_ref[0, qi, 0:1, :]) * math.log(2.0)).astype(BF16)
            dk = _dot(ds, qv)
            if diagonal:
                dv_ref[0], dk_ref[0] = dv, dk
            else:
                dv_ref[0] += dv
                dk_ref[0] += dk
            dqt_ref[0, qi] += _dot(kt, ds)

        block(ki, True)

        def two(i, carry):
            block(ki + 1 + 2 * i, False)
            block(ki + 2 + 2 * i, False)
            return carry

        def one(qi, carry):
            block(qi, False)
            return carry

        pairs = (nq - 1 - ki) // 2
        lax.fori_loop(0, pairs, two, 0)
        lax.fori_loop(ki + 1 + 2 * pairs, nq, one, 0)

    kv_spec = lambda w: pl.BlockSpec((1, tq, w), lambda h, ki: (h, ki, 0))
    stat = pl.BlockSpec((1, nq, 8, tq), lambda h, ki: (h, 0, 0, 0))
    return _pcall(
        body, "flash_bwd", (N_HEADS, nq),
        [pl.BlockSpec((1, t, HEAD_W), lambda h, ki: (h, 0, 0)), kv_spec(HEAD_W), kv_spec(V_HEAD),
         pl.BlockSpec((t, V_HEAD), lambda h, ki: (0, h)), stat, stat],
        [pl.BlockSpec((1, nq, HEAD_W, tq), lambda h, ki: (h, 0, 0, 0)), kv_spec(HEAD_W), kv_spec(V_HEAD)],
        [_sds((N_HEADS, nq, HEAD_W, tq)), _sds((N_HEADS, t, HEAD_W)), _sds((N_HEADS, t, V_HEAD))],
        comm=comm,
    )(q, k, v, do, lse, delta)


def _mla_bwd(dqt, dk, dv, cqn, ckvn, cq, ckv, rope_c, wq, wkv, qn, kvn, tm):
    t = cq.shape[0]

    def body(dq_ref, dk_ref, dv_ref, cqn_ref, ckvn_ref, cq_ref, ckv_ref, c_ref, wq_ref, wkv_ref, qn_ref, kvn_ref,
             dmla_ref, dwq_ref, dwkv_ref, dqn_ref, dkvn_ref):
        @pl.when(pl.program_id(0) == 0)
        def _():
            dwq_ref[...] = jnp.zeros_like(dwq_ref)
            dwkv_ref[...] = jnp.zeros_like(dwkv_ref)
            dqn_ref[...] = jnp.zeros_like(dqn_ref)
            dkvn_ref[...] = jnp.zeros_like(dkvn_ref)

        c = c_ref[...]
        lane = _lane_iota((tm, KR_W))
        cqn, ckvn = cqn_ref[...], ckvn_ref[...]
        dcqn = jnp.zeros((tm, Q_LORA), F32)
        dckvn = jnp.zeros((tm, KV_LORA), F32)
        dkr = jnp.zeros((tm, KR_W), F32)
        for h in range(N_HEADS):
            sl = slice(h * HEAD_W, (h + 1) * HEAD_W)
            dqh = jnp.transpose(dq_ref[h, 0]) * EXP2_SCALE
            droped = jnp.where(lane < 64, dqh[:, 128:256], 0.0)
            dqp = jnp.concatenate([dqh[:, 0:128], _rope_pair(droped) * c], axis=1).astype(BF16)
            dcqn = dcqn + _dot_nt(dqp, wq_ref[:, sl])
            dwq_ref[:, sl] += _dot_tn(cqn, dqp)
            dkr = dkr + jnp.where(lane < 64, dk_ref[h, :, 128:256], 0.0)
            dkvp = jnp.concatenate([dk_ref[h, :, 0:128], dv_ref[h]], axis=1).astype(BF16)
            dckvn = dckvn + _dot_nt(dkvp, wkv_ref[:, sl])
            dwkv_ref[:, sl] += _dot_tn(ckvn, dkvp)
        cqv, ckvv = cq_ref[...], ckv_ref[...]
        dcq, dgq = _rms_bwd(cqv, _rms_scale(cqv), qn_ref[...], dcqn)
        dckv, dgkv = _rms_bwd(ckvv, _rms_scale(ckvv), kvn_ref[...], dckvn)
        dqn_ref[...] += jnp.sum(dgq, axis=0, keepdims=True)
        dkvn_ref[...] += jnp.sum(dgkv, axis=0, keepdims=True)
        dmla_ref[:, 0:256] = dcq.astype(BF16)
        dmla_ref[:, 256:512] = dckv.astype(BF16)
        dmla_ref[:, 512:640] = (_rope_pair(dkr) * c).astype(BF16)

    hb = lambda w: pl.BlockSpec((N_HEADS, tm, w), lambda i: (0, i, 0))
    wide = N_HEADS * HEAD_W
    per_q = dqt.shape[3] // tm
    dq_spec = pl.BlockSpec((N_HEADS, 1, HEAD_W, tm), lambda i: (0, i // per_q, 0, i % per_q))
    return _pcall(
        body, "mla_bwd", (t // tm,),
        [dq_spec, hb(HEAD_W), hb(V_HEAD), _rows(tm, Q_LORA), _rows(tm, KV_LORA), _rows(tm, Q_LORA), _rows(tm, KV_LORA),
         _rows(tm, KR_W), _full(Q_LORA, wide), _full(KV_LORA, wide), _full(1, Q_LORA), _full(1, KV_LORA)],
        [_rows(tm, 640), _full(Q_LORA, wide), _full(KV_LORA, wide), _full(1, Q_LORA), _full(1, KV_LORA)],
        [_sds((t, 640), BF16), _sds((Q_LORA, wide)), _sds((KV_LORA, wide)), _sds((1, Q_LORA)), _sds((1, KV_LORA))],
    )(dqt, dk, dv, cqn, ckvn, cq, ckv, rope_c, wq, wkv, qn, kvn)


def _lru_bwd(dh, xa, h, rx, conv_w, wa, ba, wx, bx, lam, tb, comm=None):
    t = dh.shape[0]
    nb = t // tb

    def body(dh_ref, xa_ref, h_ref, hp_ref, x_ref, cw_ref, wa_ref, ba_ref, wx_ref, bx_ref, lam_ref,
             drx_ref, dcw_ref, dcb_ref, dwa_ref, dba_ref, dwx_ref, dbx_ref, dlam_ref, gc, dxn, tmp, pre_r, pre_i):
        step = pl.program_id(0)
        first_block = step == nb - 1

        @pl.when(step == 0)
        def _():
            gc[...] = jnp.zeros_like(gc)
            dxn[...] = jnp.zeros_like(dxn)
            for ref in (dcw_ref, dcb_ref, dwa_ref, dba_ref, dwx_ref, dbx_ref, dlam_ref):
                ref[...] = jnp.zeros_like(ref)

        xa = xa_ref[...]
        r, gi = _lru_gates(xa, wa_ref, ba_ref[...], wx_ref, bx_ref[...], pre_r, pre_i)
        lamv = lam_ref[...]
        sp = _softplus(-lamv)
        la = (-LRU_C * sp) * r
        a = jnp.exp(la)
        e2 = _expm1(2.0 * la)
        sq = jnp.sqrt(-e2)
        row = _row_iota((tb, D))
        cf = jnp.where(row == tb - 1, 1.0, pltpu.roll(a, tb - 1, 0))
        c3, b3 = _scan_groups_bwd(cf.reshape(tb // 8, 8, D), dh_ref[...].reshape(tb // 8, 8, D))
        carry = gc[...]
        for grp in reversed(range(tb // 8)):
            dg = b3[grp] + c3[grp] * carry
            pre_r[8 * grp:8 * grp + 8, :] = dg
            carry = dg[0:1, :]
        delta = pre_r[...]
        gc[...] = a[0:1, :] * carry
        hv = h_ref[...]
        hr = pltpu.roll(hv, 1, 0)
        tmp[...] = hr
        tmp[0:1, :] = jnp.where(first_block, 0.0, hp_ref[7:8, :])
        hprev = tmp[...]
        ix = gi * xa
        dla = (delta * hprev) * a - (delta * ix) * ((e2 + 1.0) / sq)
        dlam_ref[...] += jnp.sum(dla * r, axis=0, keepdims=True) * (LRU_C * _sigmoid(-lamv))
        dpr = (dla * (-LRU_C * sp)) * r * (1.0 - r)
        dsq = delta * sq
        dpi = (dsq * xa) * gi * (1.0 - gi)
        dba_ref[...] += jnp.sum(dpr, axis=0, keepdims=True)
        dbx_ref[...] += jnp.sum(dpi, axis=0, keepdims=True)
        pre_r[...] = dpr
        pre_i[...] = dpi
        xb = xa.astype(BF16)
        for n in range(RNN_BLOCKS):
            sl = slice(n * RNN_BW, (n + 1) * RNN_BW)
            dprn = pre_r[:, sl].astype(BF16)
            dpin = pre_i[:, sl].astype(BF16)
            dwa_ref[n] += _dot_tn(xb[:, sl], dprn)
            dwx_ref[n] += _dot_tn(xb[:, sl], dpin)
            tmp[:, sl] = _dot_nt(dprn, wa_ref[n]) + _dot_nt(dpin, wx_ref[n])
        dxa = dsq * gi + tmp[...]
        dcb_ref[...] += jnp.sum(dxa, axis=0, keepdims=True)
        xv = x_ref[...]
        drx = cw_ref[3:4, :] * dxa
        dcw_ref[3:4, :] += jnp.sum(dxa * xv, axis=0, keepdims=True)
        row8 = _row_iota((8, D))
        nxt = dxn[...]
        for s in (1, 2, 3):
            dr_ = pltpu.roll(dxa, tb - s, 0)
            tmp[...] = dr_
            tmp[tb - 8:tb, :] = jnp.where(row8 >= 8 - s, pltpu.roll(nxt, 8 - s, 0), dr_[tb - 8:tb, :])
            dxs = tmp[...]
            drx = drx + cw_ref[3 - s:4 - s, :] * dxs
            dcw_ref[3 - s:4 - s, :] += jnp.sum(dxs * xv, axis=0, keepdims=True)
        drx_ref[...] = drx.astype(BF16)
        dxn[...] = dxa[0:8, :]

    rev = pl.BlockSpec((tb, D), lambda i: (nb - 1 - i, 0))
    prev8 = pl.BlockSpec((8, D), lambda i: (jnp.maximum((nb - 1 - i) * (tb // 8) - 1, 0), 0))
    wblk = _full(RNN_BLOCKS, RNN_BW, RNN_BW)
    return _pcall(
        body, "lru_bwd", (nb,),
        [rev, rev, rev, prev8, rev, _full(4, D), wblk, _full(1, D), wblk, _full(1, D), _full(1, D)],
        [rev, _full(4, D), _full(1, D), wblk, _full(1, D), wblk, _full(1, D), _full(1, D)],
        [_sds((t, D), BF16), _sds((4, D)), _sds((1, D)), _sds((RNN_BLOCKS, RNN_BW, RNN_BW)), _sds((1, D)),
         _sds((RNN_BLOCKS, RNN_BW, RNN_BW)), _sds((1, D)), _sds((1, D))],
        scratch=[pltpu.VMEM((1, D), F32), pltpu.VMEM((8, D), F32), pltpu.VMEM((tb, D), F32), pltpu.VMEM((tb, D), F32),
                 pltpu.VMEM((tb, D), F32)],
        comm=comm,
    )(dh, xa, h, h, rx, conv_w, wa, ba, wx, bx, lam)


def _inproj_bwd(x, dh1, drx, dg3, dmla, w1, g, tm, comm=None):
    t = x.shape[0]

    def body(x_ref, d_ref, drx_ref, dg3_ref, dmla_ref, w_ref, g_ref, dx_ref, gacc_ref):
        @pl.when(pl.program_id(0) == 0)
        def _():
            gacc_ref[...] = jnp.zeros_like(gacc_ref)

        dxn = _dot(drx_ref[...], w_ref[0:D, :])
        for c0 in range(0, 3 * D, D):
            dxn = dxn + _dot(dg3_ref[:, c0:c0 + D], w_ref[D + c0:2 * D + c0, :])
        dxn = dxn + _dot(dmla_ref[...], w_ref[4 * D:W1_COLS, :])
        xv = x_ref[...]
        dx, dgr = _rms_bwd(xv, _rms_scale(xv), g_ref[...], dxn)
        dx_ref[...] = d_ref[...] + dx
        gacc_ref[...] += jnp.sum(dgr, axis=0, keepdims=True)

    return _pcall(
        body, "inproj_bwd", (t // tm,),
        [_rows(tm, D), _rows(tm, D), _rows(tm, D), _rows(tm, 3 * D), _rows(tm, 640), _full(W1_COLS, D), _full(1, D)],
        [_rows(tm, D), _full(1, D)],
        [_sds((t, D)), _sds((1, D))],
        comm=comm,
    )(x, dh1, drx, dg3, dmla, w1, g)


def _pcall_indexed(body, name, index, grid, in_specs, out_specs, out_shape):
    call = pl.pallas_call(
        body, name=name, out_shape=out_shape,
        grid_spec=pltpu.PrefetchScalarGridSpec(num_scalar_prefetch=1, grid=grid, in_specs=in_specs, out_specs=out_specs),
        compiler_params=pltpu.CompilerParams(dimension_semantics=("arbitrary",) * len(grid), vmem_limit_bytes=V7X_VMEM_LIMIT))
    return lambda *operands: call(index, *operands)


V7X_STREAM_BLOCK_BYTES = 5 << 19


def _stream_tile(rows, cols):
    return _row_tile(rows, cap=max(256, V7X_STREAM_BLOCK_BYTES // (4 * cols)), mult=16)


def _pair_sum(halves, theirs, core, out_dtype, name):
    _, rows, cols = halves.shape
    tm = _stream_tile(rows, cols)

    def body(c_ref, a_ref, b_ref, o_ref):
        o_ref[...] = (a_ref[0] + b_ref[...]).astype(out_dtype)

    plain = pl.BlockSpec((tm, cols), lambda i, c: (i, 0))
    return _pcall_indexed(body, name, core, (rows // tm,),
                          [pl.BlockSpec((1, tm, cols), lambda i, c: (c[0], i, 0)), plain], plain,
                          _sds((rows, cols), out_dtype))(halves, theirs)


def _chip_sum(parts, recv, chip, name):
    _, rows, cols = parts.shape
    tm = _stream_tile(rows, cols)

    def body(c_ref, a_ref, r_ref, o_ref):
        o_ref[...] = ((a_ref[0].astype(F32) + r_ref[0].astype(F32)) + r_ref[1].astype(F32)) + r_ref[2].astype(F32)

    return _pcall_indexed(body, name, chip, (rows // tm,),
                          [pl.BlockSpec((1, tm, cols), lambda i, c: (c[0], i, 0)),
                           pl.BlockSpec((N_CHIPS - 1, tm, cols), lambda i, c: (0, i, 0))],
                          pl.BlockSpec((tm, cols), lambda i, c: (i, 0)), _sds((rows, cols)))(parts, recv)


def _adam_math(w, gv, m, v):
    mn = ADAM_B1 * m + (1.0 - ADAM_B1) * gv
    vn = ADAM_B2 * v + (1.0 - ADAM_B2) * (gv * gv)
    m_hat = mn / (1.0 - ADAM_B1 ** ADAM_STEP)
    v_hat = vn / (1.0 - ADAM_B2 ** ADAM_STEP)
    return -ADAM_LR * (m_hat / (jnp.sqrt(v_hat) + ADAM_EPS) + ADAM_WD * w), mn, vn


def _adamw(w, g, m, v, name):
    rows, cols = w.shape
    tm = _row_tile(rows)

    def body(w_ref, g_ref, m_ref, v_ref, d_ref, mo_ref, vo_ref):
        d_ref[...], mo_ref[...], vo_ref[...] = _adam_math(w_ref[...], g_ref[...], m_ref[...], v_ref[...])

    spec = _rows(tm, cols)
    return _pcall(body, name, (rows // tm,), [spec] * 4, [spec] * 3, [_sds((rows, cols))] * 3)(w, g, m, v)


def _adamw_halves(w, mine, theirs, m, v, core, name, by_cols=False):
    rows, cols = w.shape
    if by_cols:
        tm, tc = mine.shape[0] // 2, cols // 2
        grid = (rows // tm, 2)
        full = pl.BlockSpec((tm, tc), lambda i, j, c: (i, j))
        half = pl.BlockSpec((tm, tc), lambda i, j, c: (i, 0))
    else:
        tm = _row_tile(rows // 2)
        nh = rows // 2 // tm
        grid = (rows // tm,)
        full = pl.BlockSpec((tm, cols), lambda i, c: (i, 0))
        half = pl.BlockSpec((tm, cols), lambda i, c: (i % nh, 0))

    def body(c_ref, w_ref, a_ref, b_ref, m_ref, v_ref, g_ref, d_ref, mo_ref, vo_ref):
        which = pl.program_id(1) if by_cols else pl.program_id(0) // nh
        gv = jnp.where(which == c_ref[0], a_ref[...], b_ref[...])
        g_ref[...] = gv
        d_ref[...], mo_ref[...], vo_ref[...] = _adam_math(w_ref[...], gv, m_ref[...], v_ref[...])

    return _pcall_indexed(body, name, core, grid, [full, half, half, full, full], [full] * 4,
                          [_sds((rows, cols))] * 4)(w, mine, theirs, m, v)


REL_SIBLING = (0, 0, 1)
REL_CHIPS = ((1, 0, 0), (0, 1, 0), (1, 1, 0))


V7X_DMA_CHUNK_BYTES = 1 << 20


def _split_copy(src, dst, shape, itemsize):
    nbytes = math.prod(shape) * itemsize
    if nbytes <= V7X_DMA_CHUNK_BYTES or len(shape) < 2:
        return [(src, dst)]
    if len(shape) > 2:
        out = []
        for k in range(shape[0]):
            out += _split_copy(src.at[k], dst.at[k], shape[1:], itemsize)
        return out
    rows = shape[0]
    sub = 8 * (4 // itemsize)
    parts = max(1, min(-(-nbytes // V7X_DMA_CHUNK_BYTES), rows // sub))
    while rows % parts or (rows // parts) % sub:
        parts -= 1
    step = rows // parts
    return [(src.at[pl.ds(k * step, step)], dst.at[pl.ds(k * step, step)]) for k in range(parts)]


def _mesh_pos():
    return (lax.axis_index("x"), lax.axis_index("y"), lax.axis_index("c"))


def _make_copy(i, op, sems, pos, src=None, dst=None):
    rel = op[0]
    src, dst = (op[1], op[2]) if src is None else (src, dst)
    send_sems, recv_sems = sems
    if rel is None:
        return pltpu.make_async_copy(src, dst, send_sems.at[i])
    peer = tuple((p + r) % 2 for p, r in zip(pos, rel))
    return pltpu.make_async_remote_copy(src_ref=src, dst_ref=dst, send_sem=send_sems.at[i], recv_sem=recv_sems.at[i],
                                        device_id=peer, device_id_type=MESH_ID)


def _start_copies(ops, sems, pos, base=0):
    for i, op in enumerate(ops):
        for s_piece, d_piece in _split_copy(op[1], op[2], op[1].shape, jnp.dtype(op[1].dtype).itemsize):
            _make_copy(base + i, op, sems, pos, s_piece, d_piece).start()


def _wait_copies(ops, sems, pos, base=0):
    for i, op in enumerate(ops):
        _make_copy(base + i, op, sems, pos).wait()


def _comm(name, ins, out_shapes, n_ops, ops_fn):
    n_in, n_out = len(ins), len(out_shapes)

    def body(*refs):
        in_refs, out_refs = refs[:n_in], refs[n_in:n_in + n_out]
        sems = refs[n_in + n_out:]
        pos = _mesh_pos()
        ops = ops_fn(in_refs, out_refs, pos)
        assert len(ops) == n_ops
        _start_copies(ops, sems, pos)
        _wait_copies(ops, sems, pos)

    hbm = pl.BlockSpec(memory_space=pl.ANY)
    return pl.pallas_call(
        body, name=name, in_specs=[hbm] * n_in, out_specs=[hbm] * n_out, out_shape=list(out_shapes),
        scratch_shapes=[pltpu.SemaphoreType.DMA((n_ops,)), pltpu.SemaphoreType.DMA((n_ops,))],
    )(*ins)


def _chip_of(pos, rel=(0, 0, 0)):
    return 2 * ((pos[0] + rel[0]) % 2) + (pos[1] + rel[1]) % 2


def _halved_gather_ops(pos, srcs, dsts, whole):
    me, c = _chip_of(pos), pos[2]
    ici, d2d = [], []
    for a, (src, dst) in enumerate(zip(srcs, dsts)):
        for rel in REL_CHIPS:
            if a in whole:
                ici.append((rel, src, dst.at[me]))
            else:
                ici.append((rel, src.at[c], dst.at[me, c]))
                arrived = dst.at[_chip_of(pos, rel), c]
                d2d.append((REL_SIBLING, arrived, arrived))
    return ici, d2d


def _halved_gather_comm(shards):
    srcs = [s.reshape(2, s.shape[0] // 2, s.shape[1]) for s in shards]

    def ops_fn(in_refs, out_refs, pos):
        return _halved_gather_ops(pos, in_refs, out_refs, ())

    def finish(outs, chip):
        return [lax.dynamic_update_index_in_dim(o, s, chip, 0).reshape((N_CHIPS,) + sh.shape) for o, s, sh in zip(outs, srcs, shards)]

    return (srcs, [_sds((N_CHIPS,) + s.shape, s.dtype) for s in srcs], 6 * len(shards), ops_fn), finish


def _gather_halved(shards, whole, chip, name):
    srcs = [s if a in whole or s.ndim == 3 else s.reshape(2, s.shape[0] // 2, s.shape[1]) for a, s in enumerate(shards)]
    n_sh = len(shards)
    n_ici, n_d2d = 3 * n_sh, 3 * (n_sh - len(whole))

    def body(*refs):
        in_refs, out_refs, sems = refs[:n_sh], refs[n_sh:2 * n_sh], refs[2 * n_sh:]
        pos = _mesh_pos()
        ici, d2d = _halved_gather_ops(pos, in_refs, out_refs, whole)
        _start_copies(ici, sems, pos)
        _wait_copies(ici, sems, pos)
        _start_copies(d2d, sems, pos, base=n_ici)
        _wait_copies(d2d, sems, pos, base=n_ici)

    hbm = pl.BlockSpec(memory_space=pl.ANY)
    outs = pl.pallas_call(
        body, name=name, in_specs=[hbm] * n_sh, out_specs=[hbm] * n_sh,
        out_shape=[_sds((N_CHIPS,) + s.shape, s.dtype) for s in srcs],
        scratch_shapes=[pltpu.SemaphoreType.DMA((n_ici + n_d2d,)), pltpu.SemaphoreType.DMA((n_ici + n_d2d,))],
    )(*srcs)
    return [lax.dynamic_update_index_in_dim(o, s, chip, 0).reshape((N_CHIPS,) + sh.shape)
            for o, s, sh in zip(outs, srcs, shards)]


def _split_comm(gs):
    def ops_fn(in_refs, out_refs, pos):
        return [(REL_SIBLING, src.at[1 - pos[2]], dst) for src, dst in zip(in_refs, out_refs)]

    return gs, [_sds(g.shape[1:], g.dtype) for g in gs], len(gs), ops_fn


def _exchange_comm(ps):
    def ops_fn(in_refs, out_refs, pos):
        return [(rel, src.at[_chip_of(pos, rel)], dst.at[j])
                for src, dst in zip(in_refs, out_refs) for j, rel in enumerate(REL_CHIPS)]

    return ps, [_sds((N_CHIPS - 1,) + p.shape[1:], p.dtype) for p in ps], 3 * len(ps), ops_fn


REL_OTHERS = tuple((dx, dy, dc) for dx in (0, 1) for dy in (0, 1) for dc in (0, 1) if dx + dy + dc)


def _join_comm(hs, piece=None):
    n = len(hs)

    def ops_fn(in_refs, out_refs, pos):
        ops = [(REL_SIBLING, src, dst) for src, dst in zip(in_refs[:n], out_refs[:n])]
        if piece is not None:
            ops += [(rel, in_refs[n], out_refs[n].at[2 * _chip_of(pos) + pos[2]]) for rel in REL_OTHERS]
        return ops

    ins, shapes = list(hs), [_sds(h.shape, h.dtype) for h in hs]
    if piece is not None:
        ins, shapes = ins + [piece], shapes + [_sds((2 * N_CHIPS,) + piece.shape, piece.dtype)]
    return ins, shapes, n + (len(REL_OTHERS) if piece is not None else 0), ops_fn


def _run_comm(name, comm):
    ins, shapes, n_ops, ops_fn = comm
    return _comm(name, ins, shapes, n_ops, ops_fn)


def _rope_table(t):
    pos = np.arange(t, dtype=np.float32)
    inv_freq = (np.float32(1.0) / (np.float32(ROPE_THETA) ** (np.arange(0, QK_ROPE, 2, dtype=np.float32) / np.float32(QK_ROPE)))).astype(np.float32)
    ang = (pos[:, None] * inv_freq[None, :]).astype(np.float32)
    return np.concatenate([np.cos(ang), np.cos(ang), np.sin(ang), np.sin(ang)], axis=-1).astype(np.float32)


def _rot_cols(w):
    return jnp.concatenate([-w[..., 32:], w[..., :32]], axis=-1)


def _unrot_cols(dw):
    return jnp.concatenate([dw[..., 32:], -dw[..., :32]], axis=-1)


IN_OFFS = (0, 1024, 2048, 2304, 2560, 2624, 3648, 4672)


def _w1t_from_w_in_t(wt):
    seg = [wt[IN_OFFS[i]:IN_OFFS[i + 1]] for i in range(7)]
    rnn_x, rnn_gate, cq, ckv, kr, ga, gb = seg
    return jnp.concatenate([rnn_x, rnn_gate, ga, gb, cq, ckv, kr, _rot_cols(kr.T).T], axis=0)


def _w_in_t_grad_from_parts(d_rx, d_g3, d_mla):
    kr = d_mla[512:576] + _unrot_cols(d_mla[576:640].T).T
    return jnp.concatenate([d_rx, d_g3[0:D], d_mla[0:512], kr, d_g3[D:3 * D]], axis=0)


def _wq_from_w_uq(w):
    w3 = w.reshape(Q_LORA, N_HEADS, QK_NOPE + QK_ROPE)
    rope = w3[..., QK_NOPE:]
    return jnp.concatenate([w3[..., :QK_NOPE], rope, _rot_cols(rope)], axis=-1).reshape(Q_LORA, N_HEADS * HEAD_W)


def _w_uq_grad_from_wq(dw):
    d3 = dw.reshape(Q_LORA, N_HEADS, HEAD_W)
    rope = d3[..., 128:192] + _unrot_cols(d3[..., 192:256])
    return jnp.concatenate([d3[..., :128], rope], axis=-1).reshape(Q_LORA, N_HEADS * (QK_NOPE + QK_ROPE))


def _cols_from_chunks(g):
    return g.transpose(1, 0, 2).reshape(g.shape[1], N_CHIPS * g.shape[2])


def _halves_of_col_chunks(dw):
    r, c4 = dw.shape
    return dw.reshape(2, r // 2, N_CHIPS, c4 // N_CHIPS).transpose(0, 2, 1, 3)


def _halves_of_row_chunks(dw):
    r4, c = dw.shape
    return dw.reshape(N_CHIPS, 2, r4 // (2 * N_CHIPS), c).transpose(1, 0, 2, 3)


def kernel(x, norm_mix, w_in, conv_w, conv_b, lru_wa, lru_ba, lru_wx, lru_bx, lru_lambda, q_norm, w_uq, kv_norm, w_ukv, w_out, norm_mlp, w_up, w_down, norm_final, loss_target, m_norm_mix, m_w_in, m_conv_w, m_conv_b, m_lru_wa, m_lru_ba, m_lru_wx, m_lru_bx, m_lru_lambda, m_q_norm, m_w_uq, m_kv_norm, m_w_ukv, m_w_out, m_norm_mlp, m_w_up, m_w_down, m_norm_final, v_norm_mix, v_w_in, v_conv_w, v_conv_b, v_lru_wa, v_lru_ba, v_lru_wx, v_lru_bx, v_lru_lambda, v_q_norm, v_w_uq, v_kv_norm, v_w_ukv, v_w_out, v_norm_mlp, v_w_up, v_w_down, v_norm_final):
    t = x.shape[1]
    tm = min(512, t)
    tb = min(256, t)
    tq = min(512, max(tm, t // 4))
    x2 = x[0]
    target = loss_target[0]
    chip = 2 * lax.axis_index("x") + lax.axis_index("y")
    core = lax.axis_index("c")
    chip_ix, core_ix = chip.reshape(1).astype(jnp.int32), core.reshape(1).astype(jnp.int32)
    row = lambda p: p.reshape(1, -1)

    big_shards = (w_in, w_uq, w_ukv, w_out, w_up, w_down)
    w_in_t = w_in.T
    col_halves = lambda z: jnp.stack([z[:, :z.shape[1] // 2], z[:, z.shape[1] // 2:]])
    w_in_g, conv_w_g = _gather_halved([col_halves(w_in_t.astype(BF16)), conv_w], (1,), chip, "weight_gather_first")
    w1 = _w1t_from_w_in_t(jnp.concatenate([w_in_g[:, 0], w_in_g[:, 1]], axis=-1).reshape(IN_OFFS[-1], D))
    conv_w_f = _cols_from_chunks(conv_w_g)
    wa_b, wx_b = lru_wa.astype(BF16), lru_wx.astype(BF16)

    rope_c = jnp.asarray(_rope_table(t))

    comm_a, finish_a = _halved_gather_comm([w.astype(BF16) for w in (w_uq, w_ukv, w_out)])
    (xn, rx, g3, cq, ckv, kr), gathered = _inproj(x2, row(norm_mix), w1, tm, comm=comm_a)
    g_uq, g_ukv, g_out = finish_a(gathered, chip)
    wq = _wq_from_w_uq(_cols_from_chunks(g_uq))
    wkv = _cols_from_chunks(g_ukv)
    w_out_f = g_out.reshape(D, D)
    comm_b, finish_b = _halved_gather_comm([w.astype(BF16) for w in (w_up, w_down)])
    (h, xa), gathered = _lru_fwd(rx, conv_w_f, row(conv_b), wa_b, row(lru_ba), wx_b, row(lru_bx), row(lru_lambda), tb, comm=comm_b)
    g_up, g_down = finish_b(gathered, chip)
    w_up_f = _cols_from_chunks(g_up)
    w_down_f = g_down.reshape(D_FF, D)
    q, k, v, cqn, ckvn = _mla_proj(cq, ckv, kr, row(q_norm), row(kv_norm), wq, wkv, rope_c, tm)
    nq = t // tq
    yb, lse = _flash_fwd(q, k, v, tq)
    h1, merged = _merge_out(x2, h, g3, yb, w_out_f, tm)
    u, n2 = _mlp_up(h1, row(norm_mlp), w_up_f, tm)
    act, dh2, loss_blk, g_norm_final = _mlp_down_loss(u, h1, target, w_down_f, row(norm_final), tm)

    g_w_down = _matmul_tn(act, dh2, "grad_w_down", "rows")
    du = _mlp_bwd_act(dh2, u, w_down_f, tm)
    dh1, g_norm_mlp = _mlp_bwd_in(du, dh2, h1, w_up_f, row(norm_mlp), tm)
    g_w_up = _matmul_tn(n2, du, "grad_w_up", "cols")
    dg3, dyb, delta, dh, g_w_out = _merge_bwd(dh1, w_out_f, g3, h, yb, merged, tm)
    delta = delta.reshape(N_HEADS, 8, nq, tq).swapaxes(1, 2)
    def pair_sums(hvs, theirs, dtypes, tag):
        return [_pair_sum(hv.reshape(2, -1, hv.shape[-1]), r.reshape(-1, r.shape[-1]), core_ix, dt, f"grad_pair_sum_{tag}{a}").reshape(r.shape)
                for a, (hv, r, dt) in enumerate(zip(hvs, theirs, dtypes))]

    def chip_sums(parts, received, tag):
        return [_chip_sum(p, r, chip_ix, f"grad_chip_sum_{tag}{a}") for a, (p, r) in enumerate(zip(parts, received))]

    early = [_halves_of_row_chunks(g_w_out), g_w_up, g_w_down]
    (dq, dk, dv), early_theirs = _flash_bwd(q, k, v, dyb, lse, delta, tq, comm=_split_comm(early))
    early_parts = pair_sums(early, early_theirs, [BF16] * 3, "early")
    dmla, g_wq, g_wkv, g_q_norm, g_kv_norm = _mla_bwd(dq, dk, dv, cqn, ckvn, cq, ckv, rope_c, wq, wkv, row(q_norm), row(kv_norm), tm)
    (drx, g_conv_w, g_conv_b, g_wa, g_ba, g_wx, g_bx, g_lam), early_received = _lru_bwd(
        dh, xa, h, rx, conv_w_f, wa_b, row(lru_ba), wx_b, row(lru_bx), row(lru_lambda), tb, comm=_exchange_comm(early_parts))
    early_reduced = chip_sums(early_parts, early_received, "early")
    grad_x, g_norm_mix = _inproj_bwd(x2, dh1, drx, dg3, dmla, w1, row(norm_mix), tm)
    g_w_in_gates, early_sibling = _matmul_tn(dg3, xn, "grad_w_in_gates", comm=_join_comm(early_reduced))
    g_w_in_t = _w_in_t_grad_from_parts(_matmul_tn(drx, xn, "grad_w_in_rx"), g_w_in_gates, _matmul_tn(dmla, xn, "grad_w_in_mla"))
    g_w_uq = _w_uq_grad_from_wq(g_wq)

    smalls = (g_norm_mix, g_conv_b, g_wa, g_ba, g_wx, g_bx, g_lam, g_q_norm, g_kv_norm, g_norm_mlp, g_norm_final, g_conv_w)
    s_flat = jnp.concatenate([s.reshape(-1) for s in smalls] + [loss_blk[0, 0:1], jnp.zeros((S_LEN - N_SMALL - CONVW_SIZE - 1,), F32)])
    g_w_in_halves = lax.optimization_barrier(col_halves(g_w_in_t))
    late = [g_w_in_halves.reshape(2, N_CHIPS, IN_OFFS[-1] // N_CHIPS, D // 2), _halves_of_col_chunks(g_w_uq), _halves_of_col_chunks(g_wkv),
            s_flat.reshape(N_CHIPS, 2, S_ROWS_HALF, 128).transpose(1, 0, 2, 3)]
    late_theirs = _run_comm("grad_sibling_split", _split_comm(late))
    late_parts = pair_sums(late, late_theirs, [BF16] * 3 + [F32], "late")
    late_reduced = chip_sums(late_parts, _run_comm("grad_chip_exchange", _exchange_comm(late_parts)), "late")
    late_sibling = _run_comm("grad_sibling_join", _join_comm(late_reduced[:3], late_reduced[3]))
    reduced = late_reduced[:3] + early_reduced
    reduced_sibling = list(late_sibling[:3]) + list(early_sibling)
    s_all = lax.dynamic_update_index_in_dim(late_sibling[3], late_reduced[3], 2 * chip + core, 0).reshape(-1)

    small_grads = []
    off = 0
    for shp, n in zip(SMALL_SHAPES, SMALL_SIZES):
        small_grads.append(s_all[off:off + n].reshape(shp))
        off += n
    g_conv_w_mine = lax.dynamic_slice_in_dim(s_all[off:off + CONVW_SIZE].reshape(4, D), chip * (D // N_CHIPS), D // N_CHIPS, axis=1)
    loss = s_all[off + CONVW_SIZE]

    big_m = (m_w_in, m_w_uq, m_w_ukv, m_w_out, m_w_up, m_w_down)
    big_v = (v_w_in, v_w_uq, v_w_ukv, v_w_out, v_w_up, v_w_down)
    big_names = ("w_in", "w_uq", "w_ukv", "w_out", "w_up", "w_down")
    big_upd = [_adamw_halves(w, gm, gt, m, v, core_ix, "adamw_" + n)
               for w, gm, gt, m, v, n in zip(big_shards[1:], reduced[1:], reduced_sibling[1:], big_m[1:], big_v[1:], big_names[1:])]
    w_in_upd = _adamw_halves(w_in_t, reduced[0], reduced_sibling[0], m_w_in.T, v_w_in.T, core_ix, "adamw_w_in", by_cols=True)
    big_upd = [[u.T for u in w_in_upd]] + big_upd

    small_w = (norm_mix, conv_b, lru_wa, lru_ba, lru_wx, lru_bx, lru_lambda, q_norm, kv_norm, norm_mlp, norm_final)
    small_m = (m_norm_mix, m_conv_b, m_lru_wa, m_lru_ba, m_lru_wx, m_lru_bx, m_lru_lambda, m_q_norm, m_kv_norm, m_norm_mlp, m_norm_final)
    small_v = (v_norm_mix, v_conv_b, v_lru_wa, v_lru_ba, v_lru_wx, v_lru_bx, v_lru_lambda, v_q_norm, v_kv_norm, v_norm_mlp, v_norm_final)

    def pack(items, last, fill):
        flat = jnp.concatenate([i.reshape(-1) for i in items] + [last.reshape(-1)])
        return jnp.concatenate([flat, jnp.full((PACK_ROWS * 128 - flat.shape[0],), fill, F32)]).reshape(PACK_ROWS, 128)

    packed = _adamw(pack(small_w, conv_w, 0.0), pack(small_grads, g_conv_w_mine, 0.0), pack(small_m, m_conv_w, 0.0),
                    pack(small_v, v_conv_w, 1.0), "adamw_small")

    def unpack(p):
        flat = p.reshape(-1)
        outs, o = [], 0
        for shp, n in zip(SMALL_SHAPES, SMALL_SIZES):
            outs.append(flat[o:o + n].reshape(shp))
            o += n
        return outs, flat[o:o + CONVW_SIZE // N_CHIPS].reshape(4, D // N_CHIPS)

    order = ("norm_mix", "w_in", "conv_w", "conv_b", "lru_wa", "lru_ba", "lru_wx", "lru_bx", "lru_lambda", "q_norm", "w_uq", "kv_norm",
             "w_ukv", "w_out", "norm_mlp", "w_up", "w_down", "norm_final")

    def assemble(small_list, conv_w_item, big_list):
        table = dict(zip(SMALL_NAMES, small_list))
        table["conv_w"] = conv_w_item
        table.update(zip(big_names, big_list))
        return [table[n] for n in order]

    outs = [loss, grad_x.reshape(1, t, D)]
    outs += assemble(small_grads, g_conv_w_mine, [b[0] for b in big_upd])
    for j in range(3):
        sm, cw = unpack(packed[j])
        outs += assemble(sm, cw, [b[j + 1] for b in big_upd])
    return tuple(outs)
```

```python
import functools
import math

import jax
import jax.numpy as jnp
import numpy as np
from jax import lax
from jax.experimental import pallas as pl
from jax.experimental.pallas import tpu as pltpu

F32 = jnp.float32
BF16 = jnp.bfloat16

D = 1024
N_HEADS = 8
QK_NOPE = 128
QK_ROPE = 64
V_HEAD = 128
Q_LORA = 256
KV_LORA = 256
D_FF = 4096
RNN_BLOCKS = 8
RNN_BW = 128
LRU_C = 8.0
EPS = 1e-6
ROPE_THETA = 10000.0
HEAD_W = 256
KR_W = 128
W1_COLS = 4 * D + Q_LORA + KV_LORA + KR_W
SM_SCALE = (QK_NOPE + QK_ROPE) ** -0.5
EXP2_SCALE = SM_SCALE * math.log2(math.e)
NEG = float(jnp.finfo(jnp.float32).min)

ADAM_LR = 0.001
ADAM_B1 = 0.9
ADAM_B2 = 0.999
ADAM_EPS = 1e-08
ADAM_WD = 0.01
ADAM_STEP = 10

N_CHIPS = 4
V7X_VMEM_LIMIT = 56 * 1024 * 1024
MESH_ID = pl.DeviceIdType.MESH

SMALL_NAMES = ("norm_mix", "conv_b", "lru_wa", "lru_ba", "lru_wx", "lru_bx", "lru_lambda", "q_norm", "kv_norm", "norm_mlp", "norm_final")
SMALL_SHAPES = ((D,), (D,), (RNN_BLOCKS, RNN_BW, RNN_BW), (RNN_BLOCKS, RNN_BW), (RNN_BLOCKS, RNN_BW, RNN_BW), (RNN_BLOCKS, RNN_BW), (D,),
                (Q_LORA,), (KV_LORA,), (D,), (D,))
SMALL_SIZES = tuple(math.prod(s) for s in SMALL_SHAPES)
N_SMALL = sum(SMALL_SIZES)
CONVW_SIZE = 4 * D
S_LEN = -(-(N_SMALL + CONVW_SIZE) // 8192) * 8192
S_ROWS_HALF = S_LEN // (N_CHIPS * 2 * 128)
PACK_ROWS = -(-(N_SMALL + CONVW_SIZE // N_CHIPS) // (256 * 128)) * 256


def _pcall(body, name, grid, in_specs, out_specs, out_shape, scratch=(), comm=None):
    params = pltpu.CompilerParams(dimension_semantics=("arbitrary",) * len(grid), vmem_limit_bytes=V7X_VMEM_LIMIT)
    if comm is None:
        return pl.pallas_call(body, name=name, grid=grid, in_specs=in_specs, out_specs=out_specs, out_shape=out_shape,
                              scratch_shapes=list(scratch), compiler_params=params)
    c_ins, c_shapes, n_ops, ops_fn = comm
    single = not isinstance(out_specs, (list, tuple))
    out_specs, out_shape = ([out_specs], [out_shape]) if single else (list(out_specs), list(out_shape))
    n_in, n_out, n_sc, n_ci, n_co = len(in_specs), len(out_specs), len(scratch), len(c_ins), len(c_shapes)

    def wrapped(*refs):
        ins, refs = refs[:n_in], refs[n_in:]
        c_in_refs, refs = refs[:n_ci], refs[n_ci:]
        outs, refs = refs[:n_out], refs[n_out:]
        c_out_refs, refs = refs[:n_co], refs[n_co:]
        own_scratch, sems = refs[:n_sc], refs[n_sc:]
        pos = _mesh_pos()
        ops = ops_fn(c_in_refs, c_out_refs, pos)
        ops, then = ops if isinstance(ops, tuple) else (ops, [])
        assert len(ops) + len(then) == n_ops
        first, last = True, True
        for d, n in enumerate(grid):
            first = first & (pl.program_id(d) == 0)
            last = last & (pl.program_id(d) == n - 1)

        @pl.when(first)
        def _():
            _start_copies(ops, sems, pos)

        body(*ins, *outs, *own_scratch)

        @pl.when(last)
        def _():
            _wait_copies(ops, sems, pos)
            _start_copies(then, sems, pos, base=len(ops))
            _wait_copies(then, sems, pos, base=len(ops))

    hbm = pl.BlockSpec(memory_space=pl.ANY)
    call = pl.pallas_call(
        wrapped, name=name, grid=grid, in_specs=list(in_specs) + [hbm] * n_ci, out_specs=out_specs + [hbm] * n_co,
        out_shape=out_shape + list(c_shapes),
        scratch_shapes=list(scratch) + [pltpu.SemaphoreType.DMA((n_ops,)), pltpu.SemaphoreType.DMA((n_ops,))],
        compiler_params=params)

    def run(*operands):
        res = call(*operands, *c_ins)
        own = res[0] if single else res[:n_out]
        return own, res[n_out:]

    return run


def _rows(tm, w):
    return pl.BlockSpec((tm, w), lambda i: (i, 0))


def _full(*shape):
    return pl.BlockSpec(shape, lambda *_: (0,) * len(shape))


def _sds(shape, dtype=F32):
    return jax.ShapeDtypeStruct(shape, dtype)


def _row_tile(rows, cap=256, mult=8):
    t = min(rows, cap)
    while rows % t or t % mult:
        t -= 1
    return t


def _dot(a, b):
    return jnp.dot(a, b, preferred_element_type=F32)


def _dot_nt(a, b):
    return lax.dot_general(a, b, (((1,), (1,)), ((), ())), preferred_element_type=F32)


def _dot_tn(a, b):
    return lax.dot_general(a, b, (((0,), (0,)), ((), ())), preferred_element_type=F32)


def _sigmoid(x):
    return 1.0 / (1.0 + jnp.exp(-x))


_GELU_C = math.sqrt(2.0 / math.pi)


def _gelu(x):
    return x * (0.5 * (1.0 + jnp.tanh(_GELU_C * (x + 0.044715 * (x * x * x)))))


def _gelu_grad(x):
    t = jnp.tanh(_GELU_C * (x + 0.044715 * (x * x * x)))
    cdf = 0.5 * (1.0 + t)
    return cdf + x * (0.5 * (1.0 - t * t) * _GELU_C * (1.0 + 3.0 * 0.044715 * (x * x)))


def _rms_scale(x):
    return lax.rsqrt(jnp.mean(x * x, axis=-1, keepdims=True) + EPS)


def _rms_bwd(x, rs, g, dy):
    gdy = dy * g
    dx = rs * gdy - x * ((rs * rs * rs) * jnp.mean(gdy * x, axis=-1, keepdims=True))
    return dx, dy * (x * rs)


def _log1p(e):
    u = 1.0 + e
    d = u - 1.0
    return jnp.where(d == 0.0, e, jnp.log(u) * (e / jnp.where(d == 0.0, 1.0, d)))


def _softplus(y):
    return jnp.maximum(y, 0.0) + _log1p(jnp.exp(-jnp.abs(y)))


def _expm1(x):
    u = jnp.exp(x)
    lu = jnp.log(u)
    safe = jnp.where((u == 1.0) | (u == 0.0), 1.0, lu)
    return jnp.where(u == 1.0, x, jnp.where(u == 0.0, -1.0, (u - 1.0) * (x / safe)))


def _row_iota(shape):
    return lax.broadcasted_iota(jnp.int32, shape, 0)


def _lane_iota(shape):
    return lax.broadcasted_iota(jnp.int32, shape, 1)


def _scan_groups_fwd(a, b):
    sub = lax.broadcasted_iota(jnp.int32, a.shape, 1)
    for sh in (1, 2, 4):
        m = sub >= sh
        b = jnp.where(m, a * pltpu.roll(b, sh, 1) + b, b)
        a = jnp.where(m, a * pltpu.roll(a, sh, 1), a)
    return a, b


def _scan_groups_bwd(c, b):
    sub = lax.broadcasted_iota(jnp.int32, c.shape, 1)
    for sh in (1, 2, 4):
        m = sub < 8 - sh
        b = jnp.where(m, b + c * pltpu.roll(b, 8 - sh, 1), b)
        c = jnp.where(m, c * pltpu.roll(c, 8 - sh, 1), c)
    return c, b


def _rope_pair(gc):
    return gc + pltpu.roll(gc, 64, 1)


def _inproj(x, g, w1, tm, comm=None):
    t = x.shape[0]
    widths = (D, 3 * D, Q_LORA, KV_LORA, KR_W)

    def body(x_ref, g_ref, w_ref, xn_ref, rx_ref, g3_ref, cq_ref, ckv_ref, kr_ref):
        xv = x_ref[...]
        xn = (xv * _rms_scale(xv) * g_ref[...]).astype(BF16)
        xn_ref[...] = xn
        col = 0
        for ref, w in zip((rx_ref, g3_ref, cq_ref, ckv_ref, kr_ref), widths):
            for c0 in range(0, w, 512):
                cw = min(512, w - c0)
                ref[:, c0:c0 + cw] = _dot_nt(xn, w_ref[col + c0:col + c0 + cw, :])
            col += w

    return _pcall(
        body, "inproj", (t // tm,),
        [_rows(tm, D), _full(1, D), _full(W1_COLS, D)],
        [_rows(tm, D)] + [_rows(tm, w) for w in widths],
        [_sds((t, D), BF16)] + [_sds((t, w)) for w in widths],
        comm=comm,
    )(x, g, w1)


def _lru_gates(xa, wa_ref, ba, wx_ref, bx, pre_r, pre_i):
    xb = xa.astype(BF16)
    for n in range(RNN_BLOCKS):
        sl = slice(n * RNN_BW, (n + 1) * RNN_BW)
        pre_r[:, sl] = _dot(xb[:, sl], wa_ref[n])
        pre_i[:, sl] = _dot(xb[:, sl], wx_ref[n])
    r = _sigmoid(pre_r[...] + ba)
    i = _sigmoid(pre_i[...] + bx)
    return r, i


def _lru_fwd(rx, conv_w, conv_b, wa, ba, wx, bx, lam, tb, comm=None):
    t = rx.shape[0]
    nb = t // tb

    def body(x_ref, xp_ref, cw_ref, cb_ref, wa_ref, ba_ref, wx_ref, bx_ref, lam_ref, h_ref, xa_ref, hc, tmp, pre_r, pre_i):
        i_blk = pl.program_id(0)

        @pl.when(i_blk == 0)
        def _():
            hc[...] = jnp.zeros_like(hc)

        xv = x_ref[...]
        xp = jnp.where(i_blk > 0, xp_ref[...], 0.0)
        row8 = _row_iota((8, D))
        xa = cb_ref[...] + cw_ref[3:4, :] * xv
        for s in (1, 2, 3):
            xr = pltpu.roll(xv, s, 0)
            tmp[...] = xr
            tmp[0:8, :] = jnp.where(row8 < s, pltpu.roll(xp, s, 0), xr[0:8, :])
            xa = xa + cw_ref[3 - s:4 - s, :] * tmp[...]
        xa_ref[...] = xa
        r, gi = _lru_gates(xa, wa_ref, ba_ref[...], wx_ref, bx_ref[...], pre_r, pre_i)
        la = (-LRU_C * _softplus(-lam_ref[...])) * r
        a = jnp.exp(la)
        b = jnp.sqrt(-_expm1(2.0 * la)) * (gi * xa)
        a3, b3 = _scan_groups_fwd(a.reshape(tb // 8, 8, D), b.reshape(tb // 8, 8, D))
        carry = hc[...]
        for grp in range(tb // 8):
            hg = a3[grp] * carry + b3[grp]
            h_ref[8 * grp:8 * grp + 8, :] = hg
            carry = hg[7:8, :]
        hc[...] = carry

    prev8 = pl.BlockSpec((8, D), lambda i: (jnp.maximum(i * (tb // 8) - 1, 0), 0))
    return _pcall(
        body, "lru_fwd", (nb,),
        [_rows(tb, D), prev8, _full(4, D), _full(1, D), _full(RNN_BLOCKS, RNN_BW, RNN_BW), _full(1, D),
         _full(RNN_BLOCKS, RNN_BW, RNN_BW), _full(1, D), _full(1, D)],
        [_rows(tb, D), _rows(tb, D)],
        [_sds((t, D)), _sds((t, D))],
        scratch=[pltpu.VMEM((1, D), F32), pltpu.VMEM((tb, D), F32), pltpu.VMEM((tb, D), F32), pltpu.VMEM((tb, D), F32)],
        comm=comm,
    )(rx, rx, conv_w, conv_b, wa, ba, wx, bx, lam)


def _mla_proj(cq, ckv, kr, qn, kvn, wq, wkv, rope_c, tm):
    t = cq.shape[0]

    def body(cq_ref, ckv_ref, kr_ref, qn_ref, kvn_ref, wq_ref, wkv_ref, c_ref, q_ref, k_ref, v_ref, cqn_ref, ckvn_ref):
        cqv = cq_ref[...]
        cqn = (cqv * _rms_scale(cqv) * qn_ref[...]).astype(BF16)
        ckvv = ckv_ref[...]
        ckvn = (ckvv * _rms_scale(ckvv) * kvn_ref[...]).astype(BF16)
        cqn_ref[...] = cqn
        ckvn_ref[...] = ckvn
        c = c_ref[...]
        lane = _lane_iota((tm, KR_W))
        kro = jnp.where(lane < 64, _rope_pair(kr_ref[...] * c), 0.0).astype(BF16)
        for h in range(N_HEADS):
            sl = slice(h * HEAD_W, (h + 1) * HEAD_W)
            qh = _dot(cqn, wq_ref[:, sl])
            q_ref[h, :, 0:128] = (qh[:, 0:128] * EXP2_SCALE).astype(BF16)
            q_ref[h, :, 128:256] = (_rope_pair(qh[:, 128:256] * c) * EXP2_SCALE).astype(BF16)
            kvh = _dot(ckvn, wkv_ref[:, sl])
            k_ref[h, :, 0:128] = kvh[:, 0:128].astype(BF16)
            k_ref[h, :, 128:256] = kro
            v_ref[h, :, 0:V_HEAD] = kvh[:, 128:256].astype(BF16)
            v_ref[h, :, V_HEAD:2 * V_HEAD] = jnp.ones((tm, V_HEAD), BF16)

    hb = lambda w: pl.BlockSpec((N_HEADS, tm, w), lambda i: (0, i, 0))
    return _pcall(
        body, "mla_proj", (t // tm,),
        [_rows(tm, Q_LORA), _rows(tm, KV_LORA), _rows(tm, KR_W), _full(1, Q_LORA), _full(1, KV_LORA),
         _full(Q_LORA, N_HEADS * HEAD_W), _full(KV_LORA, N_HEADS * HEAD_W), _rows(tm, KR_W)],
        [hb(HEAD_W), hb(HEAD_W), hb(2 * V_HEAD), _rows(tm, Q_LORA), _rows(tm, KV_LORA)],
        [_sds((N_HEADS, t, HEAD_W), BF16), _sds((N_HEADS, t, HEAD_W), BF16), _sds((N_HEADS, t, 2 * V_HEAD), BF16),
         _sds((t, Q_LORA), BF16), _sds((t, KV_LORA), BF16)],
    )(cq, ckv, kr, qn, kvn, wq, wkv, rope_c)


def _flash_fwd(q, k, v, tq, comm=None):
    t = q.shape[1]
    nq = t // tq

    def body(q_ref, k_ref, v_ref, o_ref, lse_ref, s_even, s_odd):
        qi = pl.program_id(1)
        qv = q_ref[0]

        def scores(ki, buf):
            buf[...] = _dot_nt(qv, k_ref[0, pl.ds(pl.multiple_of(ki * tq, tq), tq), :])

        def softmax_pv(ki, buf, carry, diagonal):
            m, acc = carry
            s = buf[...]
            if diagonal:
                s = jnp.where(_row_iota((tq, tq)) >= _lane_iota((tq, tq)), s, NEG)
            m_new = jnp.maximum(m, jnp.max(s, axis=1, keepdims=True))
            p = jnp.exp2(s - m_new)
            alpha = jnp.exp2(m - m_new)
            acc = alpha * acc + _dot(p.astype(BF16), v_ref[0, pl.ds(pl.multiple_of(ki * tq, tq), tq), :])
            return m_new, acc

        def finish(carry):
            m, acc = carry
            l = acc[:, V_HEAD:2 * V_HEAD]
            o_ref[...] = acc[:, 0:V_HEAD] / l
            lse = m + jnp.log(l) * math.log2(math.e)
            lse_ref[0, 0] = jnp.transpose(lse)[0:8, :]

        def two(i, carry):
            scores(2 * i + 1, s_odd)
            carry = softmax_pv(2 * i, s_even, carry, False)
            scores(2 * i + 2, s_even)
            return softmax_pv(2 * i + 1, s_odd, carry, False)

        init = (jnp.full((tq, 1), -jnp.inf, F32), jnp.zeros((tq, 2 * V_HEAD), F32))
        scores(0, s_even)
        carry = lax.fori_loop(0, qi // 2, two, init)

        @pl.when(qi % 2 == 0)
        def _():
            finish(softmax_pv(qi, s_even, carry, True))

        @pl.when(qi % 2 == 1)
        def _():
            scores(qi, s_odd)
            finish(softmax_pv(qi, s_odd, softmax_pv(qi - 1, s_even, carry, False), True))

    head = lambda w: pl.BlockSpec((1, t, w), lambda h, qi: (h, 0, 0))
    return _pcall(
        body, "flash_fwd", (N_HEADS, nq),
        [pl.BlockSpec((1, tq, HEAD_W), lambda h, qi: (h, qi, 0)), head(HEAD_W), head(2 * V_HEAD)],
        [pl.BlockSpec((tq, V_HEAD), lambda h, qi: (qi, h)), pl.BlockSpec((1, 1, 8, tq), lambda h, qi: (h, qi, 0, 0))],
        [_sds((t, D)), _sds((N_HEADS, nq, 8, tq))],
        scratch=[pltpu.VMEM((tq, tq), F32), pltpu.VMEM((tq, tq), F32)],
        comm=comm,
    )(q, k, v)


def _merge_out(x, h, g3, yb, w_out, tm):
    t = x.shape[0]

    def body(x_ref, h_ref, g3_ref, yb_ref, w_ref, h1_ref, mg_ref):
        ya = h_ref[...] * _gelu(g3_ref[:, 0:D])
        merged = (_sigmoid(g3_ref[:, D:2 * D]) * ya + _sigmoid(g3_ref[:, 2 * D:3 * D]) * yb_ref[...]).astype(BF16)
        mg_ref[...] = merged
        h1_ref[...] = x_ref[...] + _dot(merged, w_ref[...])

    return _pcall(
        body, "merge_out", (t // tm,),
        [_rows(tm, D), _rows(tm, D), _rows(tm, 3 * D), _rows(tm, D), _full(D, D)],
        [_rows(tm, D), _rows(tm, D)],
        [_sds((t, D)), _sds((t, D), BF16)],
    )(x, h, g3, yb, w_out)


def _mlp_up(h1, g, w_up, tm):
    t = h1.shape[0]

    def body(h_ref, g_ref, w_ref, u_ref, n2_ref):
        hv = h_ref[...]
        n2 = (hv * _rms_scale(hv) * g_ref[...]).astype(BF16)
        n2_ref[...] = n2
        for c0 in range(0, D_FF, 512):
            u_ref[:, c0:c0 + 512] = _dot(n2, w_ref[:, c0:c0 + 512])

    return _pcall(
        body, "mlp_up", (t // tm,),
        [_rows(tm, D), _full(1, D), _full(D, D_FF)],
        [_rows(tm, D_FF), _rows(tm, D)],
        [_sds((t, D_FF)), _sds((t, D), BF16)],
    )(h1, g, w_up)


def _mlp_down_loss(u, h1, target, w_down, g, tm):
    t = u.shape[0]

    def body(u_ref, h1_ref, tg_ref, w_ref, g_ref, act_ref, dh2_ref, loss_ref, gnf_ref, lacc):
        i = pl.program_id(0)

        @pl.when(i == 0)
        def _():
            lacc[...] = jnp.zeros_like(lacc)
            gnf_ref[...] = jnp.zeros_like(gnf_ref)

        ru = jnp.maximum(u_ref[...], 0.0)
        act = (ru * ru).astype(BF16)
        act_ref[...] = act
        h2 = h1_ref[...] + _dot(act, w_ref[...])
        rs = _rms_scale(h2)
        gv = g_ref[...]
        err = h2 * rs * gv - tg_ref[...]
        lacc[...] += jnp.sum(err * err, axis=0, keepdims=True)
        dx, dgr = _rms_bwd(h2, rs, gv, err * (1.0 / D))
        dh2_ref[...] = dx
        gnf_ref[...] += jnp.sum(dgr, axis=0, keepdims=True)

        @pl.when(i == pl.num_programs(0) - 1)
        def _():
            loss_ref[...] = jnp.broadcast_to(jnp.sum(lacc[...], axis=1, keepdims=True) * (0.5 / D), (8, 128))

    return _pcall(
        body, "mlp_down_loss", (t // tm,),
        [_rows(tm, D_FF), _rows(tm, D), _rows(tm, D), _full(D_FF, D), _full(1, D)],
        [_rows(tm, D_FF), _rows(tm, D), _full(8, 128), _full(1, D)],
        [_sds((t, D_FF), BF16), _sds((t, D)), _sds((8, 128)), _sds((1, D))],
        scratch=[pltpu.VMEM((1, D), F32)],
    )(u, h1, target, w_down, g)


def _matmul_tn(a, g, name, chunked=None, comm=None):
    t, kdim = a.shape
    ndim = g.shape[1]
    tk = min(kdim, 1024)
    tn = ndim if ndim <= 1024 else 1024
    if chunked == "cols":
        assert tk == kdim and tn == ndim // N_CHIPS
    elif chunked == "rows":
        assert tk == kdim // N_CHIPS and tn == ndim
    tt = min(t, 4096 if a.dtype == BF16 and g.dtype == BF16 else 2048)
    nt = t // tt

    def body(a_ref, g_ref, o_ref):
        part = _dot_tn(a_ref[...].astype(BF16), g_ref[...].astype(BF16)).reshape(o_ref.shape)

        @pl.when(pl.program_id(2) == 0)
        def _():
            o_ref[...] = part

        @pl.when(pl.program_id(2) > 0)
        def _():
            o_ref[...] += part

    if chunked is not None:
        out_spec = pl.BlockSpec((2, None, tk // 2, tn), lambda i, j, s: (0, i + j, 0, 0))
        out_shape = _sds((2, N_CHIPS, tk // 2, tn))
    else:
        out_spec, out_shape = pl.BlockSpec((tk, tn), lambda i, j, s: (i, j)), _sds((kdim, ndim))
    return _pcall(
        body, name, (kdim // tk, ndim // tn, nt),
        [pl.BlockSpec((tt, tk), lambda i, j, s: (s, i)), pl.BlockSpec((tt, tn), lambda i, j, s: (s, j))],
        out_spec, out_shape, comm=comm,
    )(a, g)


def _mlp_bwd_act(dh2, u, w_down, tm):
    t = u.shape[0]

    def body(d_ref, u_ref, w_ref, du_ref):
        db = d_ref[...].astype(BF16)
        for c0 in range(0, D_FF, 512):
            da = _dot_nt(db, w_ref[c0:c0 + 512, :])
            du_ref[:, c0:c0 + 512] = (da * (2.0 * jnp.maximum(u_ref[:, c0:c0 + 512], 0.0))).astype(BF16)

    return _pcall(
        body, "mlp_bwd_act", (t // tm,),
        [_rows(tm, D), _rows(tm, D_FF), _full(D_FF, D)],
        _rows(tm, D_FF), _sds((t, D_FF), BF16),
    )(dh2, u, w_down)


def _mlp_bwd_in(du, dh2, h1, w_up, g, tm):
    t = du.shape[0]

    def body(du_ref, d_ref, h_ref, w_ref, g_ref, dh1_ref, gacc_ref):
        @pl.when(pl.program_id(0) == 0)
        def _():
            gacc_ref[...] = jnp.zeros_like(gacc_ref)

        dn2 = _dot_nt(du_ref[...], w_ref[...])
        hv = h_ref[...]
        dx, dgr = _rms_bwd(hv, _rms_scale(hv), g_ref[...], dn2)
        dh1_ref[...] = d_ref[...] + dx
        gacc_ref[...] += jnp.sum(dgr, axis=0, keepdims=True)

    return _pcall(
        body, "mlp_bwd_in", (t // tm,),
        [_rows(tm, D_FF), _rows(tm, D), _rows(tm, D), _full(D, D_FF), _full(1, D)],
        [_rows(tm, D), _full(1, D)],
        [_sds((t, D)), _sds((1, D))],
    )(du, dh2, h1, w_up, g)


def _merge_bwd(dh1, w_out, g3, h, yb, merged, tm):
    t = dh1.shape[0]

    def body(d_ref, w_ref, g3_ref, h_ref, yb_ref, mg_ref, dg3_ref, dyb_ref, dl_ref, dh_ref, dwo_ref):
        @pl.when(pl.program_id(0) == 0)
        def _():
            dwo_ref[...] = jnp.zeros_like(dwo_ref)

        db = d_ref[...].astype(BF16)
        dwo_ref[...] += _dot_tn(mg_ref[...], db)
        dm = _dot_nt(db, w_ref[...])
        gv = g3_ref[:, 0:D]
        sa = _sigmoid(g3_ref[:, D:2 * D])
        sb = _sigmoid(g3_ref[:, 2 * D:3 * D])
        gel = _gelu(gv)
        hv = h_ref[...]
        ybv = yb_ref[...]
        dya = dm * sa
        dyb = dm * sb
        dg3_ref[:, 0:D] = (dya * hv * _gelu_grad(gv)).astype(BF16)
        dg3_ref[:, D:2 * D] = (dya * (hv * gel) * (1.0 - sa)).astype(BF16)
        dg3_ref[:, 2 * D:3 * D] = (dyb * ybv * (1.0 - sb)).astype(BF16)
        dh_ref[...] = dya * gel
        dyb_ref[...] = dyb.astype(BF16)
        prod = dyb * ybv
        ones = jnp.ones((8, V_HEAD), F32)
        for hh in range(N_HEADS):
            dl_ref[hh] = lax.dot_general(ones, prod[:, hh * V_HEAD:(hh + 1) * V_HEAD], (((1,), (1,)), ((), ())),
                                         precision=lax.Precision.HIGHEST, preferred_element_type=F32)

    return _pcall(
        body, "merge_bwd", (t // tm,),
        [_rows(tm, D), _full(D, D), _rows(tm, 3 * D), _rows(tm, D), _rows(tm, D), _rows(tm, D)],
        [_rows(tm, 3 * D), _rows(tm, D), pl.BlockSpec((N_HEADS, 8, tm), lambda i: (0, 0, i)), _rows(tm, D), _full(D, D)],
        [_sds((t, 3 * D), BF16), _sds((t, D), BF16), _sds((N_HEADS, 8, t)), _sds((t, D)), _sds((D, D))],
    )(dh1, w_out, g3, h, yb, merged)


def _flash_bwd(q, k, v, do, lse, delta, tq, comm=None):
    t = q.shape[1]
    nq = t // tq

    def body(q_ref, k_ref, v_ref, do_ref, lse_ref, dl_ref, dqt_ref, dk_ref, dv_ref):
        ki = pl.program_id(1)

        @pl.when(ki == 0)
        def _():
            dqt_ref[...] = jnp.zeros_like(dqt_ref)

        kblk, vblk = k_ref[0], v_ref[0]
        kt = jnp.transpose(kblk)

        def block(qi, diagonal):
            rows = pl.ds(pl.multiple_of(qi * tq, tq), tq)
            qv, dov = q_ref[0, rows, :], do_ref[rows, :]
            p = jnp.exp2(_dot_nt(kblk, qv) - lse_ref[0, qi, 0:1, :])
            if diagonal:
                p = jnp.where(_lane_iota((tq, tq)) >= _row_iota((tq, tq)), p, 0.0)
            dv = _dot(p.astype(BF16), dov)
            dp = _dot_nt(vblk, dov)
            ds = (p * (dp - dl_ref[0, qi, 0:1, :]) * math.log(2.0)).astype(BF16)
            dk = _dot(ds, qv)
            if diagonal:
                dv_ref[0], dk_ref[0] = dv, dk
            else:
                dv_ref[0] += dv
                dk_ref[0] += dk
            dqt_ref[0, qi] += _dot(kt, ds)

        block(ki, True)

        def two(i, carry):
            block(ki + 1 + 2 * i, False)
            block(ki + 2 + 2 * i, False)
            return carry

        def one(qi, carry):
            block(qi, False)
            return carry

        pairs = (nq - 1 - ki) // 2
        lax.fori_loop(0, pairs, two, 0)
        lax.fori_loop(ki + 1 + 2 * pairs, nq, one, 0)

    kv_spec = lambda w: pl.BlockSpec((1, tq, w), lambda h, ki: (h, ki, 0))
    stat = pl.BlockSpec((1, nq, 8, tq), lambda h, ki: (h, 0, 0, 0))
    return _pcall(
        body, "flash_bwd", (N_HEADS, nq),
        [pl.BlockSpec((1, t, HEAD_W), lambda h, ki: (h, 0, 0)), kv_spec(HEAD_W), kv_spec(V_HEAD),
         pl.BlockSpec((t, V_HEAD), lambda h, ki: (0, h)), stat, stat],
        [pl.BlockSpec((1, nq, HEAD_W, tq), lambda h, ki: (h, 0, 0, 0)), kv_spec(HEAD_W), kv_spec(V_HEAD)],
        [_sds((N_HEADS, nq, HEAD_W, tq)), _sds((N_HEADS, t, HEAD_W)), _sds((N_HEADS, t, V_HEAD))],
        comm=comm,
    )(q, k, v, do, lse, delta)


def _mla_bwd(dqt, dk, dv, cqn, ckvn, cq, ckv, rope_c, wq, wkv, qn, kvn, tm):
    t = cq.shape[0]

    def body(dq_ref, dk_ref, dv_ref, cqn_ref, ckvn_ref, cq_ref, ckv_ref, c_ref, wq_ref, wkv_ref, qn_ref, kvn_ref,
             dmla_ref, dwq_ref, dwkv_ref, dqn_ref, dkvn_ref):
        @pl.when(pl.program_id(0) == 0)
        def _():
            dwq_ref[...] = jnp.zeros_like(dwq_ref)
            dwkv_ref[...] = jnp.zeros_like(dwkv_ref)
            dqn_ref[...] = jnp.zeros_like(dqn_ref)
            dkvn_ref[...] = jnp.zeros_like(dkvn_ref)

        c = c_ref[...]
        lane = _lane_iota((tm, KR_W))
        cqn, ckvn = cqn_ref[...], ckvn_ref[...]
        dcqn = jnp.zeros((tm, Q_LORA), F32)
        dckvn = jnp.zeros((tm, KV_LORA), F32)
        dkr = jnp.zeros((tm, KR_W), F32)
        for h in range(N_HEADS):
            sl = slice(h * HEAD_W, (h + 1) * HEAD_W)
            dqh = jnp.transpose(dq_ref[h, 0]) * EXP2_SCALE
            droped = jnp.where(lane < 64, dqh[:, 128:256], 0.0)
            dqp = jnp.concatenate([dqh[:, 0:128], _rope_pair(droped) * c], axis=1).astype(BF16)
            dcqn = dcqn + _dot_nt(dqp, wq_ref[:, sl])
            dwq_ref[:, sl] += _dot_tn(cqn, dqp)
            dkr = dkr + jnp.where(lane < 64, dk_ref[h, :, 128:256], 0.0)
            dkvp = jnp.concatenate([dk_ref[h, :, 0:128], dv_ref[h]], axis=1).astype(BF16)
            dckvn = dckvn + _dot_nt(dkvp, wkv_ref[:, sl])
            dwkv_ref[:, sl] += _dot_tn(ckvn, dkvp)
        cqv, ckvv = cq_ref[...], ckv_ref[...]
        dcq, dgq = _rms_bwd(cqv, _rms_scale(cqv), qn_ref[...], dcqn)
        dckv, dgkv = _rms_bwd(ckvv, _rms_scale(ckvv), kvn_ref[...], dckvn)
        dqn_ref[...] += jnp.sum(dgq, axis=0, keepdims=True)
        dkvn_ref[...] += jnp.sum(dgkv, axis=0, keepdims=True)
        dmla_ref[:, 0:256] = dcq.astype(BF16)
        dmla_ref[:, 256:512] = dckv.astype(BF16)
        dmla_ref[:, 512:640] = (_rope_pair(dkr) * c).astype(BF16)

    hb = lambda w: pl.BlockSpec((N_HEADS, tm, w), lambda i: (0, i, 0))
    wide = N_HEADS * HEAD_W
    per_q = dqt.shape[3] // tm
    dq_spec = pl.BlockSpec((N_HEADS, 1, HEAD_W, tm), lambda i: (0, i // per_q, 0, i % per_q))
    return _pcall(
        body, "mla_bwd", (t // tm,),
        [dq_spec, hb(HEAD_W), hb(V_HEAD), _rows(tm, Q_LORA), _rows(tm, KV_LORA), _rows(tm, Q_LORA), _rows(tm, KV_LORA),
         _rows(tm, KR_W), _full(Q_LORA, wide), _full(KV_LORA, wide), _full(1, Q_LORA), _full(1, KV_LORA)],
        [_rows(tm, 640), _full(Q_LORA, wide), _full(KV_LORA, wide), _full(1, Q_LORA), _full(1, KV_LORA)],
        [_sds((t, 640), BF16), _sds((Q_LORA, wide)), _sds((KV_LORA, wide)), _sds((1, Q_LORA)), _sds((1, KV_LORA))],
    )(dqt, dk, dv, cqn, ckvn, cq, ckv, rope_c, wq, wkv, qn, kvn)


def _lru_bwd(dh, xa, h, rx, conv_w, wa, ba, wx, bx, lam, tb, comm=None):
    t = dh.shape[0]
    nb = t // tb

    def body(dh_ref, xa_ref, h_ref, hp_ref, x_ref, cw_ref, wa_ref, ba_ref, wx_ref, bx_ref, lam_ref,
             drx_ref, dcw_ref, dcb_ref, dwa_ref, dba_ref, dwx_ref, dbx_ref, dlam_ref, gc, dxn, tmp, pre_r, pre_i):
        step = pl.program_id(0)
        first_block = step == nb - 1

        @pl.when(step == 0)
        def _():
            gc[...] = jnp.zeros_like(gc)
            dxn[...] = jnp.zeros_like(dxn)
            for ref in (dcw_ref, dcb_ref, dwa_ref, dba_ref, dwx_ref, dbx_ref, dlam_ref):
                ref[...] = jnp.zeros_like(ref)

        xa = xa_ref[...]
        r, gi = _lru_gates(xa, wa_ref, ba_ref[...], wx_ref, bx_ref[...], pre_r, pre_i)
        lamv = lam_ref[...]
        sp = _softplus(-lamv)
        la = (-LRU_C * sp) * r
        a = jnp.exp(la)
        e2 = _expm1(2.0 * la)
        sq = jnp.sqrt(-e2)
        row = _row_iota((tb, D))
        cf = jnp.where(row == tb - 1, 1.0, pltpu.roll(a, tb - 1, 0))
        c3, b3 = _scan_groups_bwd(cf.reshape(tb // 8, 8, D), dh_ref[...].reshape(tb // 8, 8, D))
        carry = gc[...]
        for grp in reversed(range(tb // 8)):
            dg = b3[grp] + c3[grp] * carry
            pre_r[8 * grp:8 * grp + 8, :] = dg
            carry = dg[0:1, :]
        delta = pre_r[...]
        gc[...] = a[0:1, :] * carry
        hv = h_ref[...]
        hr = pltpu.roll(hv, 1, 0)
        tmp[...] = hr
        tmp[0:1, :] = jnp.where(first_block, 0.0, hp_ref[7:8, :])
        hprev = tmp[...]
        ix = gi * xa
        dla = (delta * hprev) * a - (delta * ix) * ((e2 + 1.0) / sq)
        dlam_ref[...] += jnp.sum(dla * r, axis=0, keepdims=True) * (LRU_C * _sigmoid(-lamv))
        dpr = (dla * (-LRU_C * sp)) * r * (1.0 - r)
        dsq = delta * sq
        dpi = (dsq * xa) * gi * (1.0 - gi)
        dba_ref[...] += jnp.sum(dpr, axis=0, keepdims=True)
        dbx_ref[...] += jnp.sum(dpi, axis=0, keepdims=True)
        pre_r[...] = dpr
        pre_i[...] = dpi
        xb = xa.astype(BF16)
        for n in range(RNN_BLOCKS):
            sl = slice(n * RNN_BW, (n + 1) * RNN_BW)
            dprn = pre_r[:, sl].astype(BF16)
            dpin = pre_i[:, sl].astype(BF16)
            dwa_ref[n] += _dot_tn(xb[:, sl], dprn)
            dwx_ref[n] += _dot_tn(xb[:, sl], dpin)
            tmp[:, sl] = _dot_nt(dprn, wa_ref[n]) + _dot_nt(dpin, wx_ref[n])
        dxa = dsq * gi + tmp[...]
        dcb_ref[...] += jnp.sum(dxa, axis=0, keepdims=True)
        xv = x_ref[...]
        drx = cw_ref[3:4, :] * dxa
        dcw_ref[3:4, :] += jnp.sum(dxa * xv, axis=0, keepdims=True)
        row8 = _row_iota((8, D))
        nxt = dxn[...]
        for s in (1, 2, 3):
            dr_ = pltpu.roll(dxa, tb - s, 0)
            tmp[...] = dr_
            tmp[tb - 8:tb, :] = jnp.where(row8 >= 8 - s, pltpu.roll(nxt, 8 - s, 0), dr_[tb - 8:tb, :])
            dxs = tmp[...]
            drx = drx + cw_ref[3 - s:4 - s, :] * dxs
            dcw_ref[3 - s:4 - s, :] += jnp.sum(dxs * xv, axis=0, keepdims=True)
        drx_ref[...] = drx.astype(BF16)
        dxn[...] = dxa[0:8, :]

    rev = pl.BlockSpec((tb, D), lambda i: (nb - 1 - i, 0))
    prev8 = pl.BlockSpec((8, D), lambda i: (jnp.maximum((nb - 1 - i) * (tb // 8) - 1, 0), 0))
    wblk = _full(RNN_BLOCKS, RNN_BW, RNN_BW)
    return _pcall(
        body, "lru_bwd", (nb,),
        [rev, rev, rev, prev8, rev, _full(4, D), wblk, _full(1, D), wblk, _full(1, D), _full(1, D)],
        [rev, _full(4, D), _full(1, D), wblk, _full(1, D), wblk, _full(1, D), _full(1, D)],
        [_sds((t, D), BF16), _sds((4, D)), _sds((1, D)), _sds((RNN_BLOCKS, RNN_BW, RNN_BW)), _sds((1, D)),
         _sds((RNN_BLOCKS, RNN_BW, RNN_BW)), _sds((1, D)), _sds((1, D))],
        scratch=[pltpu.VMEM((1, D), F32), pltpu.VMEM((8, D), F32), pltpu.VMEM((tb, D), F32), pltpu.VMEM((tb, D), F32),
                 pltpu.VMEM((tb, D), F32)],
        comm=comm,
    )(dh, xa, h, h, rx, conv_w, wa, ba, wx, bx, lam)


def _inproj_bwd(x, dh1, drx, dg3, dmla, w1, g, tm, comm=None):
    t = x.shape[0]

    def body(x_ref, d_ref, drx_ref, dg3_ref, dmla_ref, w_ref, g_ref, dx_ref, gacc_ref):
        @pl.when(pl.program_id(0) == 0)
        def _():
            gacc_ref[...] = jnp.zeros_like(gacc_ref)

        dxn = _dot(drx_ref[...], w_ref[0:D, :])
        for c0 in range(0, 3 * D, D):
            dxn = dxn + _dot(dg3_ref[:, c0:c0 + D], w_ref[D + c0:2 * D + c0, :])
        dxn = dxn + _dot(dmla_ref[...], w_ref[4 * D:W1_COLS, :])
        xv = x_ref[...]
        dx, dgr = _rms_bwd(xv, _rms_scale(xv), g_ref[...], dxn)
        dx_ref[...] = d_ref[...] + dx
        gacc_ref[...] += jnp.sum(dgr, axis=0, keepdims=True)

    return _pcall(
        body, "inproj_bwd", (t // tm,),
        [_rows(tm, D), _rows(tm, D), _rows(tm, D), _rows(tm, 3 * D), _rows(tm, 640), _full(W1_COLS, D), _full(1, D)],
        [_rows(tm, D), _full(1, D)],
        [_sds((t, D)), _sds((1, D))],
        comm=comm,
    )(x, dh1, drx, dg3, dmla, w1, g)


def _pcall_indexed(body, name, index, grid, in_specs, out_specs, out_shape):
    call = pl.pallas_call(
        body, name=name, out_shape=out_shape,
        grid_spec=pltpu.PrefetchScalarGridSpec(num_scalar_prefetch=1, grid=grid, in_specs=in_specs, out_specs=out_specs),
        compiler_params=pltpu.CompilerParams(dimension_semantics=("arbitrary",) * len(grid), vmem_limit_bytes=V7X_VMEM_LIMIT))
    return lambda *operands: call(index, *operands)


V7X_STREAM_BLOCK_BYTES = 5 << 19


def _stream_tile(rows, cols):
    return _row_tile(rows, cap=max(256, V7X_STREAM_BLOCK_BYTES // (4 * cols)), mult=16)


def _pair_sum(halves, theirs, core, out_dtype, name):
    _, rows, cols = halves.shape
    tm = _stream_tile(rows, cols)

    def body(c_ref, a_ref, b_ref, o_ref):
        o_ref[...] = (a_ref[0] + b_ref[...]).astype(out_dtype)

    plain = pl.BlockSpec((tm, cols), lambda i, c: (i, 0))
    return _pcall_indexed(body, name, core, (rows // tm,),
                          [pl.BlockSpec((1, tm, cols), lambda i, c: (c[0], i, 0)), plain], plain,
                          _sds((rows, cols), out_dtype))(halves, theirs)


def _chip_sum(parts, recv, chip, name):
    _, rows, cols = parts.shape
    tm = _stream_tile(rows, cols)

    def body(c_ref, a_ref, r_ref, o_ref):
        o_ref[...] = ((a_ref[0].astype(F32) + r_ref[0].astype(F32)) + r_ref[1].astype(F32)) + r_ref[2].astype(F32)

    return _pcall_indexed(body, name, chip, (rows // tm,),
                          [pl.BlockSpec((1, tm, cols), lambda i, c: (c[0], i, 0)),
                           pl.BlockSpec((N_CHIPS - 1, tm, cols), lambda i, c: (0, i, 0))],
                          pl.BlockSpec((tm, cols), lambda i, c: (i, 0)), _sds((rows, cols)))(parts, recv)


def _adam_math(w, gv, m, v):
    mn = ADAM_B1 * m + (1.0 - ADAM_B1) * gv
    vn = ADAM_B2 * v + (1.0 - ADAM_B2) * (gv * gv)
    m_hat = mn / (1.0 - ADAM_B1 ** ADAM_STEP)
    v_hat = vn / (1.0 - ADAM_B2 ** ADAM_STEP)
    return -ADAM_LR * (m_hat / (jnp.sqrt(v_hat) + ADAM_EPS) + ADAM_WD * w), mn, vn


def _adamw(w, g, m, v, name):
    rows, cols = w.shape
    tm = _row_tile(rows)

    def body(w_ref, g_ref, m_ref, v_ref, d_ref, mo_ref, vo_ref):
        d_ref[...], mo_ref[...], vo_ref[...] = _adam_math(w_ref[...], g_ref[...], m_ref[...], v_ref[...])

    spec = _rows(tm, cols)
    return _pcall(body, name, (rows // tm,), [spec] * 4, [spec] * 3, [_sds((rows, cols))] * 3)(w, g, m, v)


def _adamw_halves(w, mine, theirs, m, v, core, name, by_cols=False):
    rows, cols = w.shape
    if by_cols:
        tm, tc = mine.shape[0] // 2, cols // 2
        grid = (rows // tm, 2)
        full = pl.BlockSpec((tm, tc), lambda i, j, c: (i, j))
        half = pl.BlockSpec((tm, tc), lambda i, j, c: (i, 0))
    else:
        tm = _row_tile(rows // 2)
        nh = rows // 2 // tm
        grid = (rows // tm,)
        full = pl.BlockSpec((tm, cols), lambda i, c: (i, 0))
        half = pl.BlockSpec((tm, cols), lambda i, c: (i % nh, 0))

    def body(c_ref, w_ref, a_ref, b_ref, m_ref, v_ref, g_ref, d_ref, mo_ref, vo_ref):
        which = pl.program_id(1) if by_cols else pl.program_id(0) // nh
        gv = jnp.where(which == c_ref[0], a_ref[...], b_ref[...])
        g_ref[...] = gv
        d_ref[...], mo_ref[...], vo_ref[...] = _adam_math(w_ref[...], gv, m_ref[...], v_ref[...])

    return _pcall_indexed(body, name, core, grid, [full, half, half, full, full], [full] * 4,
                          [_sds((rows, cols))] * 4)(w, mine, theirs, m, v)


REL_SIBLING = (0, 0, 1)
REL_CHIPS = ((1, 0, 0), (0, 1, 0), (1, 1, 0))


V7X_DMA_CHUNK_BYTES = 1 << 20


def _split_copy(src, dst, shape, itemsize):
    nbytes = math.prod(shape) * itemsize
    if nbytes <= V7X_DMA_CHUNK_BYTES or len(shape) < 2:
        return [(src, dst)]
    if len(shape) > 2:
        out = []
        for k in range(shape[0]):
            out += _split_copy(src.at[k], dst.at[k], shape[1:], itemsize)
        return out
    rows = shape[0]
    sub = 8 * (4 // itemsize)
    parts = max(1, min(-(-nbytes // V7X_DMA_CHUNK_BYTES), rows // sub))
    while rows % parts or (rows // parts) % sub:
        parts -= 1
    step = rows // parts
    return [(src.at[pl.ds(k * step, step)], dst.at[pl.ds(k * step, step)]) for k in range(parts)]


def _mesh_pos():
    return (lax.axis_index("x"), lax.axis_index("y"), lax.axis_index("c"))


def _make_copy(i, op, sems, pos, src=None, dst=None):
    rel = op[0]
    src, dst = (op[1], op[2]) if src is None else (src, dst)
    send_sems, recv_sems = sems
    if rel is None:
        return pltpu.make_async_copy(src, dst, send_sems.at[i])
    peer = tuple((p + r) % 2 for p, r in zip(pos, rel))
    return pltpu.make_async_remote_copy(src_ref=src, dst_ref=dst, send_sem=send_sems.at[i], recv_sem=recv_sems.at[i],
                                        device_id=peer, device_id_type=MESH_ID)


def _start_copies(ops, sems, pos, base=0):
    for i, op in enumerate(ops):
        for s_piece, d_piece in _split_copy(op[1], op[2], op[1].shape, jnp.dtype(op[1].dtype).itemsize):
            _make_copy(base + i, op, sems, pos, s_piece, d_piece).start()


def _wait_copies(ops, sems, pos, base=0):
    for i, op in enumerate(ops):
        _make_copy(base + i, op, sems, pos).wait()


def _comm(name, ins, out_shapes, n_ops, ops_fn):
    n_in, n_out = len(ins), len(out_shapes)

    def body(*refs):
        in_refs, out_refs = refs[:n_in], refs[n_in:n_in + n_out]
        sems = refs[n_in + n_out:]
        pos = _mesh_pos()
        ops = ops_fn(in_refs, out_refs, pos)
        assert len(ops) == n_ops
        _start_copies(ops, sems, pos)
        _wait_copies(ops, sems, pos)

    hbm = pl.BlockSpec(memory_space=pl.ANY)
    return pl.pallas_call(
        body, name=name, in_specs=[hbm] * n_in, out_specs=[hbm] * n_out, out_shape=list(out_shapes),
        scratch_shapes=[pltpu.SemaphoreType.DMA((n_ops,)), pltpu.SemaphoreType.DMA((n_ops,))],
    )(*ins)


def _chip_of(pos, rel=(0, 0, 0)):
    return 2 * ((pos[0] + rel[0]) % 2) + (pos[1] + rel[1]) % 2


def _halved_gather_ops(pos, srcs, dsts, whole):
    me, c = _chip_of(pos), pos[2]
    ici, d2d = [], []
    for a, (src, dst) in enumerate(zip(srcs, dsts)):
        for rel in REL_CHIPS:
            if a in whole:
                ici.append((rel, src, dst.at[me]))
            else:
                ici.append((rel, src.at[c], dst.at[me, c]))
                arrived = dst.at[_chip_of(pos, rel), c]
                d2d.append((REL_SIBLING, arrived, arrived))
    return ici, d2d


def _halved_gather_comm(shards):
    srcs = [s.reshape(2, s.shape[0] // 2, s.shape[1]) for s in shards]

    def ops_fn(in_refs, out_refs, pos):
        return _halved_gather_ops(pos, in_refs, out_refs, ())

    def finish(outs, chip):
        return [lax.dynamic_update_index_in_dim(o, s, chip, 0).reshape((N_CHIPS,) + sh.shape) for o, s, sh in zip(outs, srcs, shards)]

    return (srcs, [_sds((N_CHIPS,) + s.shape, s.dtype) for s in srcs], 6 * len(shards), ops_fn), finish


def _gather_halved(shards, whole, chip, name):
    srcs = [s if a in whole or s.ndim == 3 else s.reshape(2, s.shape[0] // 2, s.shape[1]) for a, s in enumerate(shards)]
    n_sh = len(shards)
    n_ici, n_d2d = 3 * n_sh, 3 * (n_sh - len(whole))

    def body(*refs):
        in_refs, out_refs, sems = refs[:n_sh], refs[n_sh:2 * n_sh], refs[2 * n_sh:]
        pos = _mesh_pos()
        ici, d2d = _halved_gather_ops(pos, in_refs, out_refs, whole)
        _start_copies(ici, sems, pos)
        _wait_copies(ici, sems, pos)
        _start_copies(d2d, sems, pos, base=n_ici)
        _wait_copies(d2d, sems, pos, base=n_ici)

    hbm = pl.BlockSpec(memory_space=pl.ANY)
    outs = pl.pallas_call(
        body, name=name, in_specs=[hbm] * n_sh, out_specs=[hbm] * n_sh,
        out_shape=[_sds((N_CHIPS,) + s.shape, s.dtype) for s in srcs],
        scratch_shapes=[pltpu.SemaphoreType.DMA((n_ici + n_d2d,)), pltpu.SemaphoreType.DMA((n_ici + n_d2d,))],
    )(*srcs)
    return [lax.dynamic_update_index_in_dim(o, s, chip, 0).reshape((N_CHIPS,) + sh.shape)
            for o, s, sh in zip(outs, srcs, shards)]


def _split_comm(gs):
    def ops_fn(in_refs, out_refs, pos):
        return [(REL_SIBLING, src.at[1 - pos[2]], dst) for src, dst in zip(in_refs, out_refs)]

    return gs, [_sds(g.shape[1:], g.dtype) for g in gs], len(gs), ops_fn


def _exchange_comm(ps):
    def ops_fn(in_refs, out_refs, pos):
        return [(rel, src.at[_chip_of(pos, rel)], dst.at[j])
                for src, dst in zip(in_refs, out_refs) for j, rel in enumerate(REL_CHIPS)]

    return ps, [_sds((N_CHIPS - 1,) + p.shape[1:], p.dtype) for p in ps], 3 * len(ps), ops_fn


REL_OTHERS = tuple((dx, dy, dc) for dx in (0, 1) for dy in (0, 1) for dc in (0, 1) if dx + dy + dc)


def _join_comm(hs, piece=None):
    n = len(hs)

    def ops_fn(in_refs, out_refs, pos):
        ops = [(REL_SIBLING, src, dst) for src, dst in zip(in_refs[:n], out_refs[:n])]
        if piece is not None:
            ops += [(rel, in_refs[n], out_refs[n].at[2 * _chip_of(pos) + pos[2]]) for rel in REL_OTHERS]
        return ops

    ins, shapes = list(hs), [_sds(h.shape, h.dtype) for h in hs]
    if piece is not None:
        ins, shapes = ins + [piece], shapes + [_sds((2 * N_CHIPS,) + piece.shape, piece.dtype)]
    return ins, shapes, n + (len(REL_OTHERS) if piece is not None else 0), ops_fn


def _run_comm(name, comm):
    ins, shapes, n_ops, ops_fn = comm
    return _comm(name, ins, shapes, n_ops, ops_fn)


def _rope_table(t):
    pos = np.arange(t, dtype=np.float32)
    inv_freq = (np.float32(1.0) / (np.float32(ROPE_THETA) ** (np.arange(0, QK_ROPE, 2, dtype=np.float32) / np.float32(QK_ROPE)))).astype(np.float32)
    ang = (pos[:, None] * inv_freq[None, :]).astype(np.float32)
    return np.concatenate([np.cos(ang), np.cos(ang), np.sin(ang), np.sin(ang)], axis=-1).astype(np.float32)


def _rot_cols(w):
    return jnp.concatenate([-w[..., 32:], w[..., :32]], axis=-1)


def _unrot_cols(dw):
    return jnp.concatenate([dw[..., 32:], -dw[..., :32]], axis=-1)


IN_OFFS = (0, 1024, 2048, 2304, 2560, 2624, 3648, 4672)


def _w1t_from_w_in_t(wt):
    seg = [wt[IN_OFFS[i]:IN_OFFS[i + 1]] for i in range(7)]
    rnn_x, rnn_gate, cq, ckv, kr, ga, gb = seg
    return jnp.concatenate([rnn_x, rnn_gate, ga, gb, cq, ckv, kr, _rot_cols(kr.T).T], axis=0)


def _w_in_t_grad_from_parts(d_rx, d_g3, d_mla):
    kr = d_mla[512:576] + _unrot_cols(d_mla[576:640].T).T
    return jnp.concatenate([d_rx, d_g3[0:D], d_mla[0:512], kr, d_g3[D:3 * D]], axis=0)


def _wq_from_w_uq(w):
    w3 = w.reshape(Q_LORA, N_HEADS, QK_NOPE + QK_ROPE)
    rope = w3[..., QK_NOPE:]
    return jnp.concatenate([w3[..., :QK_NOPE], rope, _rot_cols(rope)], axis=-1).reshape(Q_LORA, N_HEADS * HEAD_W)


def _w_uq_grad_from_wq(dw):
    d3 = dw.reshape(Q_LORA, N_HEADS, HEAD_W)
    rope = d3[..., 128:192] + _unrot_cols(d3[..., 192:256])
    return jnp.concatenate([d3[..., :128], rope], axis=-1).reshape(Q_LORA, N_HEADS * (QK_NOPE + QK_ROPE))


def _cols_from_chunks(g):
    return g.transpose(1, 0, 2).reshape(g.shape[1], N_CHIPS * g.shape[2])


def _halves_of_col_chunks(dw):
    r, c4 = dw.shape
    return dw.reshape(2, r // 2, N_CHIPS, c4 // N_CHIPS).transpose(0, 2, 1, 3)


def _halves_of_row_chunks(dw):
    r4, c = dw.shape
    return dw.reshape(N_CHIPS, 2, r4 // (2 * N_CHIPS), c).transpose(1, 0, 2, 3)


def kernel(x, norm_mix, w_in, conv_w, conv_b, lru_wa, lru_ba, lru_wx, lru_bx, lru_lambda, q_norm, w_uq, kv_norm, w_ukv, w_out, norm_mlp, w_up, w_down, norm_final, loss_target, m_norm_mix, m_w_in, m_conv_w, m_conv_b, m_lru_wa, m_lru_ba, m_lru_wx, m_lru_bx, m_lru_lambda, m_q_norm, m_w_uq, m_kv_norm, m_w_ukv, m_w_out, m_norm_mlp, m_w_up, m_w_down, m_norm_final, v_norm_mix, v_w_in, v_conv_w, v_conv_b, v_lru_wa, v_lru_ba, v_lru_wx, v_lru_bx, v_lru_lambda, v_q_norm, v_w_uq, v_kv_norm, v_w_ukv, v_w_out, v_norm_mlp, v_w_up, v_w_down, v_norm_final):
    t = x.shape[1]
    tm = min(512, t)
    tb = min(256, t)
    tq = min(512, max(tm, t // 4))
    x2 = x[0]
    target = loss_target[0]
    chip = 2 * lax.axis_index("x") + lax.axis_index("y")
    core = lax.axis_index("c")
    chip_ix, core_ix = chip.reshape(1).astype(jnp.int32), core.reshape(1).astype(jnp.int32)
    row = lambda p: p.reshape(1, -1)

    big_shards = (w_in, w_uq, w_ukv, w_out, w_up, w_down)
    w_in_t = w_in.T
    col_halves = lambda z: jnp.stack([z[:, :z.shape[1] // 2], z[:, z.shape[1] // 2:]])
    w_in_g, conv_w_g = _gather_halved([col_halves(w_in_t.astype(BF16)), conv_w], (1,), chip, "weight_gather_first")
    w1 = _w1t_from_w_in_t(jnp.concatenate([w_in_g[:, 0], w_in_g[:, 1]], axis=-1).reshape(IN_OFFS[-1], D))
    conv_w_f = _cols_from_chunks(conv_w_g)
    wa_b, wx_b = lru_wa.astype(BF16), lru_wx.astype(BF16)

    rope_c = jnp.asarray(_rope_table(t))

    comm_a, finish_a = _halved_gather_comm([w.astype(BF16) for w in (w_uq, w_ukv, w_out)])
    (xn, rx, g3, cq, ckv, kr), gathered = _inproj(x2, row(norm_mix), w1, tm, comm=comm_a)
    g_uq, g_ukv, g_out = finish_a(gathered, chip)
    wq = _wq_from_w_uq(_cols_from_chunks(g_uq))
    wkv = _cols_from_chunks(g_ukv)
    w_out_f = g_out.reshape(D, D)
    comm_b, finish_b = _halved_gather_comm([w.astype(BF16) for w in (w_up, w_down)])
    (h, xa), gathered = _lru_fwd(rx, conv_w_f, row(conv_b), wa_b, row(lru_ba), wx_b, row(lru_bx), row(lru_lambda), tb, comm=comm_b)
    g_up, g_down = finish_b(gathered, chip)
    w_up_f = _cols_from_chunks(g_up)
    w_down_f = g_down.reshape(D_FF, D)
    q, k, v, cqn, ckvn = _mla_proj(cq, ckv, kr, row(q_norm), row(kv_norm), wq, wkv, rope_c, tm)
    nq = t // tq
    yb, lse = _flash_fwd(q, k, v, tq)
    h1, merged = _merge_out(x2, h, g3, yb, w_out_f, tm)
    u, n2 = _mlp_up(h1, row(norm_mlp), w_up_f, tm)
    act, dh2, loss_blk, g_norm_final = _mlp_down_loss(u, h1, target, w_down_f, row(norm_final), tm)

    g_w_down = _matmul_tn(act, dh2, "grad_w_down", "rows")
    du = _mlp_bwd_act(dh2, u, w_down_f, tm)
    dh1, g_norm_mlp = _mlp_bwd_in(du, dh2, h1, w_up_f, row(norm_mlp), tm)
    g_w_up = _matmul_tn(n2, du, "grad_w_up", "cols")
    dg3, dyb, delta, dh, g_w_out = _merge_bwd(dh1, w_out_f, g3, h, yb, merged, tm)
    delta = delta.reshape(N_HEADS, 8, nq, tq).swapaxes(1, 2)
    def pair_sums(hvs, theirs, dtypes, tag):
        return [_pair_sum(hv.reshape(2, -1, hv.shape[-1]), r.reshape(-1, r.shape[-1]), core_ix, dt, f"grad_pair_sum_{tag}{a}").reshape(r.shape)
                for a, (hv, r, dt) in enumerate(zip(hvs, theirs, dtypes))]

    def chip_sums(parts, received, tag):
        return [_chip_sum(p, r, chip_ix, f"grad_chip_sum_{tag}{a}") for a, (p, r) in enumerate(zip(parts, received))]

    early = [_halves_of_row_chunks(g_w_out), g_w_up, g_w_down]
    (dq, dk, dv), early_theirs = _flash_bwd(q, k, v, dyb, lse, delta, tq, comm=_split_comm(early))
    early_parts = pair_sums(early, early_theirs, [BF16] * 3, "early")
    dmla, g_wq, g_wkv, g_q_norm, g_kv_norm = _mla_bwd(dq, dk, dv, cqn, ckvn, cq, ckv, rope_c, wq, wkv, row(q_norm), row(kv_norm), tm)
    (drx, g_conv_w, g_conv_b, g_wa, g_ba, g_wx, g_bx, g_lam), early_received = _lru_bwd(
        dh, xa, h, rx, conv_w_f, wa_b, row(lru_ba), wx_b, row(lru_bx), row(lru_lambda), tb, comm=_exchange_comm(early_parts))
    early_reduced = chip_sums(early_parts, early_received, "early")
    grad_x, g_norm_mix = _inproj_bwd(x2, dh1, drx, dg3, dmla, w1, row(norm_mix), tm)
    g_w_in_gates, early_sibling = _matmul_tn(dg3, xn, "grad_w_in_gates", comm=_join_comm(early_reduced))
    g_w_in_t = _w_in_t_grad_from_parts(_matmul_tn(drx, xn, "grad_w_in_rx"), g_w_in_gates, _matmul_tn(dmla, xn, "grad_w_in_mla"))
    g_w_uq = _w_uq_grad_from_wq(g_wq)

    smalls = (g_norm_mix, g_conv_b, g_wa, g_ba, g_wx, g_bx, g_lam, g_q_norm, g_kv_norm, g_norm_mlp, g_norm_final, g_conv_w)
    s_flat = jnp.concatenate([s.reshape(-1) for s in smalls] + [loss_blk[0, 0:1], jnp.zeros((S_LEN - N_SMALL - CONVW_SIZE - 1,), F32)])
    g_w_in_halves = lax.optimization_barrier(col_halves(g_w_in_t))
    late = [g_w_in_halves.reshape(2, N_CHIPS, IN_OFFS[-1] // N_CHIPS, D // 2), _halves_of_col_chunks(g_w_uq), _halves_of_col_chunks(g_wkv),
            s_flat.reshape(N_CHIPS, 2, S_ROWS_HALF, 128).transpose(1, 0, 2, 3)]
    late_theirs = _run_comm("grad_sibling_split", _split_comm(late))
    late_parts = pair_sums(late, late_theirs, [BF16] * 3 + [F32], "late")
    late_reduced = chip_sums(late_parts, _run_comm("grad_chip_exchange", _exchange_comm(late_parts)), "late")
    late_sibling = _run_comm("grad_sibling_join", _join_comm(late_reduced[:3], late_reduced[3]))
    reduced = late_reduced[:3] + early_reduced
    reduced_sibling = list(late_sibling[:3]) + list(early_sibling)
    s_all = lax.dynamic_update_index_in_dim(late_sibling[3], late_reduced[3], 2 * chip + core, 0).reshape(-1)

    small_grads = []
    off = 0
    for shp, n in zip(SMALL_SHAPES, SMALL_SIZES):
        small_grads.append(s_all[off:off + n].reshape(shp))
        off += n
    g_conv_w_mine = lax.dynamic_slice_in_dim(s_all[off:off + CONVW_SIZE].reshape(4, D), chip * (D // N_CHIPS), D // N_CHIPS, axis=1)
    loss = s_all[off + CONVW_SIZE]

    big_m = (m_w_in, m_w_uq, m_w_ukv, m_w_out, m_w_up, m_w_down)
    big_v = (v_w_in, v_w_uq, v_w_ukv, v_w_out, v_w_up, v_w_down)
    big_names = ("w_in", "w_uq", "w_ukv", "w_out", "w_up", "w_down")
    big_upd = [_adamw_halves(w, gm, gt, m, v, core_ix, "adamw_" + n)
               for w, gm, gt, m, v, n in zip(big_shards[1:], reduced[1:], reduced_sibling[1:], big_m[1:], big_v[1:], big_names[1:])]
    w_in_upd = _adamw_halves(w_in_t, reduced[0], reduced_sibling[0], m_w_in.T, v_w_in.T, core_ix, "adamw_w_in", by_cols=True)
    big_upd = [[u.T for u in w_in_upd]] + big_upd

    small_w = (norm_mix, conv_b, lru_wa, lru_ba, lru_wx, lru_bx, lru_lambda, q_norm, kv_norm, norm_mlp, norm_final)
    small_m = (m_norm_mix, m_conv_b, m_lru_wa, m_lru_ba, m_lru_wx, m_lru_bx, m_lru_lambda, m_q_norm, m_kv_norm, m_norm_mlp, m_norm_final)
    small_v = (v_norm_mix, v_conv_b, v_lru_wa, v_lru_ba, v_lru_wx, v_lru_bx, v_lru_lambda, v_q_norm, v_kv_norm, v_norm_mlp, v_norm_final)

    def pack(items, last, fill):
        flat = jnp.concatenate([i.reshape(-1) for i in items] + [last.reshape(-1)])
        return jnp.concatenate([flat, jnp.full((PACK_ROWS * 128 - flat.shape[0],), fill, F32)]).reshape(PACK_ROWS, 128)

    packed = _adamw(pack(small_w, conv_w, 0.0), pack(small_grads, g_conv_w_mine, 0.0), pack(small_m, m_conv_w, 0.0),
                    pack(small_v, v_conv_w, 1.0), "adamw_small")

    def unpack(p):
        flat = p.reshape(-1)
        outs, o = [], 0
        for shp, n in zip(SMALL_SHAPES, SMALL_SIZES):
            outs.append(flat[o:o + n].reshape(shp))
            o += n
        return outs, flat[o:o + CONVW_SIZE // N_CHIPS].reshape(4, D // N_CHIPS)

    order = ("norm_mix", "w_in", "conv_w", "conv_b", "lru_wa", "lru_ba", "lru_wx", "lru_bx", "lru_lambda", "q_norm", "w_uq", "kv_norm",
             "w_ukv", "w_out", "norm_mlp", "w_up", "w_down", "norm_final")

    def assemble(small_list, conv_w_item, big_list):
        table = dict(zip(SMALL_NAMES, small_list))
        table["conv_w"] = conv_w_item
        table.update(zip(big_names, big_list))
        return [table[n] for n in order]

    outs = [loss, grad_x.reshape(1, t, D)]
    outs += assemble(small_grads, g_conv_w_mine, [b[0] for b in big_upd])
    for j in range(3):
        sm, cw = unpack(packed[j])
        outs += assemble(sm, cw, [b[j + 1] for b in big_upd])
    return tuple(outs)
```
